```python
import math
import jax
import jax.numpy as jnp
from jax import lax
import numpy as np

D_MODEL = 2048
BATCH = 2
SEQ = 4096
DEPTH = 2

GRID_W = 64
CTX_LEN = 256
NORM_EPS = 1e-6
N_BRANCH = 3

S5_WIDTH = 1024
S5_GROUP = 16
S5_GROUPS = S5_WIDTH // S5_GROUP
S5_STATE = 64
S5_DT_MIN = 1e-3
S5_DT_MAX = 1e-1
S5_MAX_RE = -1e-4

RW_WIDTH = 1024
RW_HEAD = 64
RW_HEADS = RW_WIDTH // RW_HEAD
RW_DECAY_RANK = 64
RW_ICLR_RANK = 64
RW_GATE_RANK = 128
RW_DECAY_SCALE = 0.606531
RW_GN_EPS = 64e-5
RW_SHIFT_WIDTH = 3 * RW_WIDTH + RW_DECAY_RANK + RW_ICLR_RANK

GLA_HEADS = 4
GLA_DK = 128
GLA_DV = 256
GLA_QK_WIDTH = GLA_HEADS * GLA_DK
GLA_V_WIDTH = GLA_HEADS * GLA_DV
GLA_GATE_RANK = 16
GLA_TAU = 16.0
GLA_CHUNK = 64

MOE_GROUPS = 4
MOE_PER_GROUP = 8
MOE_EXPERTS = MOE_GROUPS * MOE_PER_GROUP
MOE_TOPK = 2
MOE_HIDDEN = 256

IN_SPLITS = (S5_WIDTH, RW_SHIFT_WIDTH, RW_GATE_RANK, GLA_QK_WIDTH, GLA_QK_WIDTH, GLA_V_WIDTH,
             GLA_GATE_RANK, GLA_V_WIDTH, N_BRANCH * D_MODEL)
IN_WIDTH = sum(IN_SPLITS)

kernel_name = 'hybrid_s5_rwkv7_gla_hmoe_prefix'


def _split_points():
    points, acc = [], 0
    for w in IN_SPLITS[:-1]:
        acc += w
        points.append(acc)
    return points


def _rmsnorm(x, g):
    xf = x.astype(jnp.float32)
    y = xf * lax.rsqrt(jnp.mean(xf * xf, axis=-1, keepdims=True) + NORM_EPS)
    return (y * g.astype(jnp.float32)).astype(x.dtype)


def _modulate(h, shift, scale):
    return h * (1.0 + scale) + shift


def _to_column_major(t, rows):
    bsz, n, d = t.shape
    return jnp.swapaxes(t.reshape(bsz, rows, GRID_W, d), 1, 2).reshape(bsz, n, d)


def _to_row_major(t, rows):
    bsz, n, d = t.shape
    return jnp.swapaxes(t.reshape(bsz, GRID_W, rows, d), 1, 2).reshape(bsz, n, d)


def _bidirectional(direction_fn, prm_fwd, prm_bwd, ctx_seq, lat_seq):
    def flip(seq):
        return tuple(jnp.flip(a, axis=1) for a in seq)
    y_cf, s_cf = direction_fn(ctx_seq, prm_fwd, None)
    y_lf, _ = direction_fn(lat_seq, prm_fwd, s_cf)
    y_cb, s_cb = direction_fn(flip(ctx_seq), prm_bwd, None)
    y_lb, _ = direction_fn(flip(lat_seq), prm_bwd, s_cb)
    return y_cf + jnp.flip(y_cb, axis=1), y_lf + jnp.flip(y_lb, axis=1)


def _linear_combine(left, right):
    a_l, b_l = left
    a_r, b_r = right
    return a_l * a_r, a_r * b_l + b_r


def _s5_direction(seq, prm, h0):
    (u,) = seq
    a_re, a_im, log_dt, b_re, b_im, c_re, c_im = (p.astype(jnp.float32) for p in prm)
    bsz, n, _ = u.shape
    lam = lax.complex(jnp.minimum(a_re, S5_MAX_RE), a_im)
    lam_bar = jnp.exp(lam * jnp.exp(log_dt)[:, None])
    b_bar = ((lam_bar - 1.0) / lam)[..., None] * lax.complex(b_re, b_im)
    c_mat = lax.complex(c_re, c_im)
    ug = u.astype(jnp.float32).reshape(bsz, n, S5_GROUPS, S5_GROUP).astype(jnp.complex64)
    bu = jnp.einsum('gpi,btgi->btgp', b_bar, ug)
    if h0 is not None:
        bu = bu.at[:, 0].add(lam_bar * h0)
    _, h = lax.associative_scan(_linear_combine, (jnp.broadcast_to(lam_bar, bu.shape), bu), axis=1)
    y = jnp.einsum('gip,btgp->btgi', c_mat, h).real.reshape(bsz, n, S5_WIDTH)
    return y, h[:, -1]


def _s5_out(y, u, d, w_glu):
    z = jax.nn.gelu(y + d.astype(jnp.float32) * u.astype(jnp.float32)).astype(u.dtype)
    val, gate = jnp.split(z @ w_glu, 2, axis=-1)
    return val * jax.nn.sigmoid(gate)


def _rwkv_direction(seq, prm, s0):
    (z,) = seq
    mu, w0, w_up, a0, a_up, k_k, k_a, r_k = (p.astype(jnp.float32) for p in prm)
    z = z.astype(jnp.float32)
    bsz, n, _ = z.shape
    prev = jnp.pad(z, ((0, 0), (1, 0), (0, 0)))[:, :-1]
    z = z + (prev - z) * mu
    r, k, v, w_lo, a_lo = jnp.split(
        z, [RW_WIDTH, 2 * RW_WIDTH, 3 * RW_WIDTH, 3 * RW_WIDTH + RW_DECAY_RANK], axis=-1)
    decay = jnp.exp(-RW_DECAY_SCALE * jax.nn.sigmoid(w0 + jnp.tanh(w_lo) @ w_up))
    a = jax.nn.sigmoid(a0 + a_lo @ a_up)
    kk = (k * k_k).reshape(bsz, n, RW_HEADS, RW_HEAD)
    kk = kk / jnp.maximum(jnp.sqrt(jnp.sum(kk * kk, axis=-1, keepdims=True)), 1e-12)
    k = k * (1.0 + (a - 1.0) * k_a)
    r, k, v, decay, a = (t.reshape(bsz, n, RW_HEADS, RW_HEAD) for t in (r, k, v, decay, a))
    if s0 is None:
        s0 = jnp.zeros((bsz, RW_HEADS, RW_HEAD, RW_HEAD), jnp.float32)

    def step(state, inp):
        r_t, k_t, v_t, w_t, kk_t, a_t = inp
        sa = jnp.einsum('bhvk,bhk->bhv', state, kk_t)
        state = (state * w_t[:, :, None, :] - sa[..., None] * (kk_t * a_t)[:, :, None, :]
                 + v_t[..., None] * k_t[:, :, None, :])
        return state, jnp.einsum('bhvk,bhk->bhv', state, r_t)

    s_fin, y = lax.scan(step, s0, tuple(jnp.swapaxes(t, 0, 1) for t in (r, k, v, decay, kk, a)))
    y = jnp.swapaxes(y, 0, 1) + jnp.sum(r * k * r_k, axis=-1, keepdims=True) * v
    return y.reshape(bsz, n, RW_WIDTH), s_fin


def _rwkv_out(y, g_lo, g_up, ln_w, ln_b, w_proj):
    bsz, n, _ = y.shape
    yh = y.reshape(bsz, n, RW_HEADS, RW_HEAD)
    mean = jnp.mean(yh, axis=-1, keepdims=True)
    var = jnp.mean(jnp.square(yh - mean), axis=-1, keepdims=True)
    yn = ((yh - mean) * lax.rsqrt(var + RW_GN_EPS)).reshape(bsz, n, RW_WIDTH)
    yn = yn * ln_w.astype(jnp.float32) + ln_b.astype(jnp.float32)
    gate = jax.nn.sigmoid(g_lo.astype(jnp.float32)) @ g_up.astype(jnp.float32)
    return (yn * gate).astype(g_lo.dtype) @ w_proj


def _gla_direction(seq, prm, s0):
    q, k, v, gk_lo = seq
    gk_up, gk_b = (p.astype(jnp.float32) for p in prm)
    bsz, n, _ = q.shape
    n_chunk = n // GLA_CHUNK
    log_a = jax.nn.log_sigmoid(gk_lo.astype(jnp.float32) @ gk_up + gk_b) / GLA_TAU

    def chunks(t, d):
        return t.astype(jnp.float32).reshape(bsz, n_chunk, GLA_CHUNK, GLA_HEADS, d)

    q = chunks(q, GLA_DK) * GLA_DK ** -0.5
    k = chunks(k, GLA_DK)
    v = chunks(v, GLA_DV)
    b = jnp.cumsum(chunks(log_a, GLA_DK), axis=2)
    b_ref = b[:, :, GLA_CHUNK // 2:GLA_CHUNK // 2 + 1]
    b_last = b[:, :, -1]
    scores = jnp.einsum('bnchd,bnmhd->bnhcm', q * jnp.exp(b - b_ref), k * jnp.exp(b_ref - b))
    lower = jnp.tril(jnp.ones((GLA_CHUNK, GLA_CHUNK), dtype=bool))
    scores = jnp.where(lower, scores, 0.0)
    o = jnp.einsum('bnhcm,bnmhe->bnche', scores, v)
    u_chunk = jnp.einsum('bnchd,bnche->bnhde', k * jnp.exp(b_last[:, :, None] - b), v)
    if s0 is None:
        s0 = jnp.zeros((bsz, GLA_HEADS, GLA_DK, GLA_DV), jnp.float32)

    def step(state, inp):
        dec, uc = inp
        return dec[..., None] * state + uc, state

    s_fin, s_prev = lax.scan(step, s0, (jnp.swapaxes(jnp.exp(b_last), 0, 1), jnp.swapaxes(u_chunk, 0, 1)))
    o = o + jnp.einsum('bnchd,bnhde->bnche', q * jnp.exp(b), jnp.swapaxes(s_prev, 0, 1))
    return o.reshape(bsz, n, GLA_V_WIDTH), s_fin


def _gla_out(o, og, norm_g, w_proj):
    bsz, n, _ = o.shape
    oh = o.reshape(bsz, n, GLA_HEADS, GLA_DV)
    oh = oh * lax.rsqrt(jnp.mean(oh * oh, axis=-1, keepdims=True) + NORM_EPS)
    on = oh.reshape(bsz, n, GLA_V_WIDTH) * norm_g.astype(jnp.float32)
    return (on * jax.nn.silu(og.astype(jnp.float32))).astype(og.dtype) @ w_proj


def _token_mixer(u_ctx, u_lat, rows, need_ctx, w_in, s5_dirs, s5_d, s5_w_glu, rw_dirs, rw_g_up,
                 rw_ln_w, rw_ln_b, rw_w_proj, gl_dirs, gl_norm_g, gl_w_proj, w_out):
    points = _split_points()
    cc = jnp.split(u_ctx @ w_in, points, axis=-1)
    cl = jnp.split(u_lat @ w_in, points, axis=-1)
    ya_ctx, ya_lat = _bidirectional(_s5_direction, s5_dirs[0], s5_dirs[1], (cc[0],), (cl[0],))
    yb_ctx, yb_lat = _bidirectional(_rwkv_direction, rw_dirs[0], rw_dirs[1], (cc[1],), (cl[1],))
    lat_cm = tuple(_to_column_major(t, rows) for t in cl[3:7])
    yc_ctx, yc_lat = _bidirectional(_gla_direction, gl_dirs[0], gl_dirs[1], tuple(cc[3:7]), lat_cm)
    yc_lat = _to_row_major(yc_lat, rows)

    def merge(cols, ya, yb, yc):
        gate_a, gate_b, gate_c = jnp.split(jax.nn.sigmoid(cols[8]), N_BRANCH, axis=-1)
        pa = _s5_out(ya, cols[0], s5_d, s5_w_glu)
        pb = _rwkv_out(yb, cols[2], rw_g_up, rw_ln_w, rw_ln_b, rw_w_proj)
        pc = _gla_out(yc, cols[7], gl_norm_g, gl_w_proj)
        return (gate_a * pa + gate_b * pb + gate_c * pc) @ w_out

    y_lat = merge(cl, ya_lat, yb_lat, yc_lat)
    y_ctx = merge(cc, ya_ctx, yb_ctx, yc_ctx) if need_ctx else None
    return y_ctx, y_lat


def _moe(v, wg1, bg1, wg2, bg2, w_gate, w_up, w_down):
    lead = v.shape[:-1]
    t = v.reshape(-1, v.shape[-1])
    p_group = jax.nn.softmax((t @ wg1).astype(jnp.float32) + bg1.astype(jnp.float32), axis=-1)
    p_top, grp = lax.top_k(p_group, 1)
    logits = ((t @ wg2).astype(jnp.float32) + bg2.astype(jnp.float32)).reshape(-1, MOE_GROUPS, MOE_PER_GROUP)
    in_group = jnp.take_along_axis(logits, grp[:, :, None], axis=1)[:, 0]
    val, idx = lax.top_k(in_group, MOE_TOPK)
    weight = p_top * jax.nn.softmax(val, axis=-1)
    expert = grp * MOE_PER_GROUP + idx
    comb = jnp.einsum('tk,tke->te', weight, jax.nn.one_hot(expert, MOE_EXPERTS, dtype=jnp.float32)).astype(t.dtype)
    hid = jax.nn.silu(jnp.einsum('td,edf->tef', t, w_gate)) * jnp.einsum('td,edf->tef', t, w_up)
    out = jnp.einsum('tef,efd->td', hid * comb[:, :, None], w_down)
    return out.reshape(*lead, out.shape[-1])


def setup_inputs(seed: int = 0) -> dict:
    key = jax.random.key(seed)
    ks = iter(jax.random.split(key, 64))

    def nrm(shape, scale=1.0):
        return jax.random.normal(next(ks), shape, jnp.float32) * scale

    L, D = DEPTH, D_MODEL
    G, P, I = S5_GROUPS, S5_STATE, S5_GROUP
    n_idx = jnp.arange(P, dtype=jnp.float32)
    return {
        'x': nrm((BATCH, SEQ, D)),
        'c': nrm((BATCH, D)),
        'ctx': nrm((BATCH, CTX_LEN, D)),
        'c_ctx': nrm((D,)),
        'w_mod': nrm((L, D, 6 * D), 0.5 * D ** -0.5),
        'b_mod': nrm((L, 6 * D), 0.02),
        'g_norm1': 1.0 + nrm((L, D), 0.05),
        'g_norm2': 1.0 + nrm((L, D), 0.05),
        'w_in': nrm((L, D, IN_WIDTH), D ** -0.5),
        's5_a_re': -0.5 + nrm((L, 2, G, P), 0.01),
        's5_a_im': math.pi * n_idx + nrm((L, 2, G, P), 0.01),
        's5_log_dt': jax.random.uniform(next(ks), (L, 2, G), jnp.float32,
                                        math.log(S5_DT_MIN), math.log(S5_DT_MAX)),
        's5_b_re': nrm((L, 2, G, P, I), (2 * I) ** -0.5),
        's5_b_im': nrm((L, 2, G, P, I), (2 * I) ** -0.5),
        's5_c_re': nrm((L, 2, G, I, P), (2 * P) ** -0.5),
        's5_c_im': nrm((L, 2, G, I, P), (2 * P) ** -0.5),
        's5_d': nrm((L, S5_WIDTH)),
        's5_w_glu': nrm((L, S5_WIDTH, 2 * D), S5_WIDTH ** -0.5),
        'rw_mu': jax.random.uniform(next(ks), (L, 2, RW_SHIFT_WIDTH), jnp.float32),
        'rw_w0': nrm((L, 2, RW_WIDTH), 0.5),
        'rw_w_up': nrm((L, 2, RW_DECAY_RANK, RW_WIDTH), 0.1),
        'rw_a0': nrm((L, 2, RW_WIDTH), 0.5),
        'rw_a_up': nrm((L, 2, RW_ICLR_RANK, RW_WIDTH), 0.1),
        'rw_k_k': 0.85 + nrm((L, RW_WIDTH), 0.05),
        'rw_k_a': 1.0 + nrm((L, RW_WIDTH), 0.05),
        'rw_r_k': nrm((L, RW_HEADS, RW_HEAD), 0.1),
        'rw_g_up': nrm((L, RW_GATE_RANK, RW_WIDTH), RW_GATE_RANK ** -0.5),
        'rw_ln_w': 1.0 + nrm((L, RW_WIDTH), 0.05),
        'rw_ln_b': nrm((L, RW_WIDTH), 0.02),
        'rw_w_proj': nrm((L, RW_WIDTH, D), RW_WIDTH ** -0.5),
        'gl_gk_up': nrm((L, 2, GLA_GATE_RANK, GLA_QK_WIDTH), GLA_GATE_RANK ** -0.5),
        'gl_gk_b': nrm((L, 2, GLA_QK_WIDTH), 0.5),
        'gl_norm_g': 1.0 + nrm((L, GLA_V_WIDTH), 0.05),
        'gl_w_proj': nrm((L, GLA_V_WIDTH, D), GLA_V_WIDTH ** -0.5),
        'w_out': nrm((L, D, D), D ** -0.5),
        'moe_wg1': nrm((L, D, MOE_GROUPS), D ** -0.5),
        'moe_bg1': nrm((L, MOE_GROUPS), 0.01),
        'moe_wg2': nrm((L, D, MOE_EXPERTS), D ** -0.5),
        'moe_bg2': nrm((L, MOE_EXPERTS), 0.01),
        'moe_w_gate': nrm((L, MOE_EXPERTS, D, MOE_HIDDEN), D ** -0.5),
        'moe_w_up': nrm((L, MOE_EXPERTS, D, MOE_HIDDEN), D ** -0.5),
        'moe_w_down': nrm((L, MOE_EXPERTS, MOE_HIDDEN, D), MOE_HIDDEN ** -0.5),
        'g_final': 1.0 + nrm((D,), 0.05),
    }


def reference(x, c, ctx, c_ctx, w_mod, b_mod, g_norm1, g_norm2, w_in,
              s5_a_re, s5_a_im, s5_log_dt, s5_b_re, s5_b_im, s5_c_re, s5_c_im, s5_d, s5_w_glu,
              rw_mu, rw_w0, rw_w_up, rw_a0, rw_a_up, rw_k_k, rw_k_a, rw_r_k, rw_g_up,
              rw_ln_w, rw_ln_b, rw_w_proj, gl_gk_up, gl_gk_b, gl_norm_g, gl_w_proj, w_out,
              moe_wg1, moe_bg1, moe_wg2, moe_bg2, moe_w_gate, moe_w_up, moe_w_down, g_final):
    rows = x.shape[1] // GRID_W
    silu_c = jax.nn.silu(c)
    silu_cc = jax.nn.silu(c_ctx)
    for i in range(DEPTH):
        need_ctx = i < DEPTH - 1
        mod_lat = jnp.split((silu_c @ w_mod[i] + b_mod[i])[:, None, :], 6, axis=-1)
        mod_ctx = jnp.split(silu_cc @ w_mod[i] + b_mod[i], 6, axis=-1)
        u_lat = _modulate(_rmsnorm(x, g_norm1[i]), mod_lat[0], mod_lat[1])
        u_ctx = _modulate(_rmsnorm(ctx, g_norm1[i]), mod_ctx[0], mod_ctx[1])
        s5_dirs = [tuple(p[i, d] for p in (s5_a_re, s5_a_im, s5_log_dt, s5_b_re, s5_b_im, s5_c_re, s5_c_im))
                   for d in range(2)]
        rw_dirs = [(rw_mu[i, d], rw_w0[i, d], rw_w_up[i, d], rw_a0[i, d], rw_a_up[i, d],
                    rw_k_k[i], rw_k_a[i], rw_r_k[i]) for d in range(2)]
        gl_dirs = [(gl_gk_up[i, d], gl_gk_b[i, d]) for d in range(2)]
        y_ctx, y_lat = _token_mixer(u_ctx, u_lat, rows, need_ctx, w_in[i], s5_dirs, s5_d[i], s5_w_glu[i],
                                    rw_dirs, rw_g_up[i], rw_ln_w[i], rw_ln_b[i], rw_w_proj[i],
                                    gl_dirs, gl_norm_g[i], gl_w_proj[i], w_out[i])
        moe_p = (moe_wg1[i], moe_bg1[i], moe_wg2[i], moe_bg2[i], moe_w_gate[i], moe_w_up[i], moe_w_down[i])
        x = x + mod_lat[2] * y_lat
        v_lat = _modulate(_rmsnorm(x, g_norm2[i]), mod_lat[3], mod_lat[4])
        x = x + mod_lat[5] * _moe(v_lat, *moe_p)
        if need_ctx:
            ctx = ctx + mod_ctx[2] * y_ctx
            v_ctx = _modulate(_rmsnorm(ctx, g_norm2[i]), mod_ctx[3], mod_ctx[4])
            ctx = ctx + mod_ctx[5] * _moe(v_ctx, *moe_p)
    return _rmsnorm(x, g_final)
```

```python
import functools
import math

import jax
import jax.numpy as jnp
from jax import lax
from jax.experimental import pallas as pl
from jax.experimental.pallas import tpu as pltpu

F32 = jnp.float32
BF16 = jnp.bfloat16

D_MODEL = 2048
GRID_W = 64
NORM_EPS = 1e-6

S5_WIDTH = 1024
S5_GROUP = 16
S5_GROUPS = 64
S5_STATE = 64
S5_MAX_RE = -1e-4
S5_TILE = 16
S5_PAIRS = S5_GROUPS // 2

RW_WIDTH = 1024
RW_HEAD = 64
RW_DECAY_SCALE = 0.606531
RW_GN_EPS = 64e-5
RW_BLOCK = 256
RW_CHUNK = 16
RW_PAIRS = RW_WIDTH // 128
RW_Z = 3 * RW_WIDTH + 128

GLA_HEADS = 4
GLA_DK = 128
GLA_DV = 256
GLA_QK = 512
GLA_V = 1024
GLA_TAU = 16.0
GLA_CHUNK = 64

MOE_GROUPS = 4
MOE_PER_GROUP = 8
MOE_EXPERTS = 32
MOE_HIDDEN = 256
MOE_LANE0 = MOE_GROUPS

LANES = 128
VMEM_LIMIT = 56 * 1024 * 1024


def _cp(*sem):
    return pltpu.CompilerParams(dimension_semantics=sem, vmem_limit_bytes=VMEM_LIMIT)


def _dot(a, b):
    return jnp.dot(a, b, preferred_element_type=F32)


def _dot_nt(a, b):
    return lax.dot_general(a, b, (((1,), (1,)), ((), ())), preferred_element_type=F32)


def _dot_tn(a, b):
    return lax.dot_general(a, b, (((0,), (0,)), ((), ())), preferred_element_type=F32)


def _sigmoid(x):
    return 1.0 / (1.0 + jnp.exp(-x))


def _silu(x):
    return x * _sigmoid(x)


def _gelu_tanh(x):
    return 0.5 * x * (1.0 + jnp.tanh(math.sqrt(2.0 / math.pi) * (x + 0.044715 * (x * x * x))))


def _split_hi_lo(x):
    hi = x.astype(BF16)
    lo = (x - hi.astype(F32)).astype(BF16)
    return hi, lo


def _iota(shape, dim):
    return lax.broadcasted_iota(jnp.int32, shape, dim)


def _head_ones(width):
    return (_iota((LANES, LANES), 0) // width == _iota((LANES, LANES), 1) // width).astype(BF16)


def _head_sum(x, ones, exact):
    if exact:
        hi, lo = _split_hi_lo(x)
        return _dot(hi, ones) + _dot(lo, ones)
    return _dot(x.astype(BF16), ones)


def _adaln_kernel(c_ref, w_ref, b_ref, o_ref):
    c = c_ref[...]
    o_ref[...] = _dot(_silu(c).astype(BF16), w_ref[...].astype(BF16)) + b_ref[...]


def _adaln(cc, w_mod, b_mod):
    depth, d, n = w_mod.shape
    bn = 1536
    return pl.pallas_call(
        _adaln_kernel,
        out_shape=jax.ShapeDtypeStruct((depth, 8, n), F32),
        grid=(depth, n // bn),
        in_specs=[
            pl.BlockSpec((8, d), lambda l, j: (0, 0)),
            pl.BlockSpec((None, d, bn), lambda l, j: (l, 0, j)),
            pl.BlockSpec((None, 1, bn), lambda l, j: (l, 0, j)),
        ],
        out_specs=pl.BlockSpec((None, 8, bn), lambda l, j: (l, 0, j)),
        compiler_params=_cp("parallel", "parallel"),
        name="adaln",
    )(cc, w_mod, b_mod.reshape(depth, 1, n))


def _mod_spec(layer, part, bm, nc, seq):
    def index(i, *_):
        r0 = i * bm
        return (layer, jnp.where(r0 < nc, 2, (r0 - nc) // seq), part, 0, 0)
    return pl.BlockSpec((None, None, None, 1, D_MODEL), index)


def _normmod_kernel(x_ref, g_ref, sh_ref, sc_ref, o_ref):
    x = x_ref[...]
    y = x * lax.rsqrt(jnp.mean(x * x, axis=-1, keepdims=True) + NORM_EPS) * g_ref[...]
    o_ref[...] = (y * (1.0 + sc_ref[...]) + sh_ref[...]).astype(o_ref.dtype)


def _normmod(x, g, mod, layer, parts, bm, nc, seq):
    n, d = x.shape
    return pl.pallas_call(
        _normmod_kernel,
        out_shape=jax.ShapeDtypeStruct((n, d), BF16),
        grid=(n // bm,),
        in_specs=[
            pl.BlockSpec((bm, d), lambda i: (i, 0)),
            pl.BlockSpec((None, 1, d), lambda i: (layer, 0, 0)),
            _mod_spec(layer, parts[0], bm, nc, seq),
            _mod_spec(layer, parts[1], bm, nc, seq),
        ],
        out_specs=pl.BlockSpec((bm, d), lambda i: (i, 0)),
        compiler_params=_cp("parallel"),
        name="normmod",
    )(x, g, mod, mod)


def _mm_kernel(x_ref, w_ref, o_ref):
    o_ref[...] = _dot(x_ref[...], w_ref[...]).astype(o_ref.dtype)


def _mm(x, w, bm, bn, out_dtype=F32):
    m, k = x.shape
    n = w.shape[1]
    return pl.pallas_call(
        _mm_kernel,
        out_shape=jax.ShapeDtypeStruct((m, n), out_dtype),
        grid=(m // bm, n // bn),
        in_specs=[pl.BlockSpec((bm, k), lambda i, j: (i, 0)), pl.BlockSpec((k, bn), lambda i, j: (0, j))],
        out_specs=pl.BlockSpec((bm, bn), lambda i, j: (i, j)),
        compiler_params=_cp("parallel", "parallel"),
        name="mm",
    )(x, w)


def _s5_operators(a_re, a_im, log_dt, b_re, b_im, c_re, c_im, reverse):
    hp = lax.Precision.HIGHEST
    t = S5_TILE
    lam = lax.complex(jnp.minimum(a_re, S5_MAX_RE), a_im)
    ldt = lam * jnp.exp(log_dt)[:, None]
    lam_bar = jnp.exp(ldt)
    b_bar = ((lam_bar - 1.0) / lam)[..., None] * lax.complex(b_re, b_im)
    c_mat = lax.complex(c_re, c_im)
    taus = jnp.arange(t + 1, dtype=F32)
    lam_pow = jnp.exp(ldt[None] * taus[:, None, None])
    k_tau = jnp.einsum('gop,tgp,gpi->tgoi', c_mat, lam_pow[:t], b_bar, precision=hp).real
    s_idx = jnp.arange(t)[:, None]
    t_idx = jnp.arange(t)[None, :]
    lag = (s_idx - t_idx) if reverse else (t_idx - s_idx)
    k_st = jnp.where((lag >= 0)[:, :, None, None, None], k_tau[jnp.clip(lag, 0, t - 1)], 0.0)
    k_g = jnp.transpose(k_st, (2, 0, 4, 1, 3))
    eye2 = jnp.eye(2, dtype=F32)
    kmat = jnp.einsum('pgsito,gh->psgitho', k_g.reshape(S5_PAIRS, 2, t, S5_GROUP, t, S5_GROUP), eye2)
    kmat = kmat.reshape(S5_PAIRS, 2 * t * S5_GROUP, 2 * t * S5_GROUP)
    e_s = jnp.arange(t) if reverse else (t - 1 - jnp.arange(t))
    b_end = lam_pow[e_s][:, :, :, None] * b_bar[None]
    b_end = jnp.transpose(b_end, (1, 0, 3, 2)).reshape(S5_PAIRS, 2, t, S5_GROUP, S5_STATE)
    b_ri = jnp.stack([b_end.real, b_end.imag], axis=-2)
    wend = jnp.einsum('pgsicq,gh->psgichq', b_ri, eye2).reshape(S5_PAIRS, 2 * t * S5_GROUP, 4 * S5_STATE)
    e_t = (t - jnp.arange(t)) if reverse else (jnp.arange(t) + 1)
    c_g = jnp.transpose(c_mat, (0, 2, 1))[:, :, None, :] * jnp.transpose(lam_pow[e_t], (1, 2, 0))[..., None]
    c_ri = jnp.stack([c_g.real, -c_g.imag], axis=1).reshape(S5_PAIRS, 2, 2, S5_STATE, t, S5_GROUP)
    cmat = jnp.einsum('pgcqto,gh->pcgqtho', c_ri, eye2).reshape(S5_PAIRS, 4 * S5_STATE, 2 * t * S5_GROUP)
    lam_sq = jnp.exp(ldt[None] * (t * 2.0 ** jnp.arange(8, dtype=F32))[:, None, None])
    lam_sq = jnp.transpose(lam_sq, (1, 0, 2)).reshape(S5_PAIRS, 2, 8, S5_STATE)
    lam_sq = jnp.transpose(lam_sq, (0, 2, 1, 3)).reshape(S5_PAIRS, 8, 2 * S5_STATE)
    return kmat.astype(BF16), wend.astype(BF16), cmat.astype(BF16), lam_sq.real, lam_sq.imag


def _s5_tile_scan(e_re, e_im, lre, lim, reverse):
    n = e_re.shape[0]
    row = _iota((n, 1), 0)
    step = 1
    k = 0
    while step < n:
        if reverse:
            s_re = pltpu.roll(e_re, n - step, 0)
            s_im = pltpu.roll(e_im, n - step, 0)
            ok = row < n - step
        else:
            s_re = pltpu.roll(e_re, step, 0)
            s_im = pltpu.roll(e_im, step, 0)
            ok = row >= step
        a_re = lre[k:k + 1]
        a_im = lim[k:k + 1]
        e_re = e_re + jnp.where(ok, a_re * s_re - a_im * s_im, 0.0)
        e_im = e_im + jnp.where(ok, a_re * s_im + a_im * s_re, 0.0)
        step *= 2
        k += 1
    return e_re, e_im


def _s5_kernel(uc_ref, ul_ref, kmat_ref, wend_ref, cmat_ref, lre_ref, lim_ref, yc_ref, yl_ref, *, reverse):
    kmat = kmat_ref[...]
    wend = wend_ref[...]
    cmat = cmat_ref[...]
    lre = lre_ref[...]
    lim = lim_ref[...]
    half = 2 * S5_STATE

    def entering(h, n, init):
        row = _iota((n, 1), 0)
        if reverse:
            return jnp.where(row == n - 1, init, pltpu.roll(h, n - 1, 0))
        return jnp.where(row == 0, init, pltpu.roll(h, 1, 0))

    for b in range(uc_ref.shape[0]):
        uc = uc_ref[b]
        ul = ul_ref[b]
        nc = uc.shape[0]
        nl = ul.shape[0]
        ec = _dot(uc, wend)
        el = _dot(ul, wend)
        hc_re, hc_im = _s5_tile_scan(ec[:, :half], ec[:, half:], lre, lim, reverse)
        last = 0 if reverse else nc - 1
        car_re = hc_re[last:last + 1]
        car_im = hc_im[last:last + 1]
        first = _iota((nl, 1), 0) == (nl - 1 if reverse else 0)
        el_re = el[:, :half] + jnp.where(first, lre[0:1] * car_re - lim[0:1] * car_im, 0.0)
        el_im = el[:, half:] + jnp.where(first, lre[0:1] * car_im + lim[0:1] * car_re, 0.0)
        hl_re, hl_im = _s5_tile_scan(el_re, el_im, lre, lim, reverse)
        zero = jnp.zeros((1, half), F32)
        pc_re = entering(hc_re, nc, zero).astype(BF16)
        pc_im = entering(hc_im, nc, zero).astype(BF16)
        pl_re = entering(hl_re, nl, car_re).astype(BF16)
        pl_im = entering(hl_im, nl, car_im).astype(BF16)
        yc_ref[b] = _dot(uc, kmat) + _dot(pc_re, cmat[:half]) + _dot(pc_im, cmat[half:])
        yl_ref[b] = _dot(ul, kmat) + _dot(pl_re, cmat[:half]) + _dot(pl_im, cmat[half:])


def _s5_direction(u_ctx, u_lat, ops, reverse):
    kmat, wend, cmat, lre, lim = ops
    _, bsz, nct, w = u_ctx.shape
    nlt = u_lat.shape[2]
    per_pair = lambda shape: pl.BlockSpec((None,) + shape, lambda p: (p,) + (0,) * len(shape))
    return pl.pallas_call(
        functools.partial(_s5_kernel, reverse=reverse),
        out_shape=(jax.ShapeDtypeStruct(u_ctx.shape, F32), jax.ShapeDtypeStruct(u_lat.shape, F32)),
        grid=(S5_PAIRS,),
        in_specs=[per_pair((bsz, nct, w)), per_pair((bsz, nlt, w)), per_pair(kmat.shape[1:]),
                  per_pair(wend.shape[1:]), per_pair(cmat.shape[1:]), per_pair(lre.shape[1:]),
                  per_pair(lim.shape[1:])],
        out_specs=(per_pair((bsz, nct, w)), per_pair((bsz, nlt, w))),
        compiler_params=_cp("parallel"),
        name="s5_bwd" if reverse else "s5_fwd",
    )(u_ctx, u_lat, kmat, wend, cmat, lre, lim)


def _s5_to_tiles(u, bsz):
    n = u.shape[0] // bsz
    t = u.reshape(bsz, n // S5_TILE, S5_TILE, S5_PAIRS, 2 * S5_GROUP)
    return jnp.transpose(t, (3, 0, 1, 2, 4)).reshape(S5_PAIRS, bsz, n // S5_TILE, S5_TILE * 2 * S5_GROUP)


def _s5_from_tiles(y):
    _, bsz, nt, _ = y.shape
    t = y.reshape(S5_PAIRS, bsz, nt, S5_TILE, 2 * S5_GROUP)
    return jnp.transpose(t, (1, 2, 3, 0, 4)).reshape(bsz * nt * S5_TILE, S5_WIDTH)


def _gla_kernel(q_ref, k_ref, v_ref, gk_ref, up_ref, gb_ref, s0_ref, o_ref, sf_ref, st_ref):
    d = pl.program_id(0)
    c = pl.program_id(2)
    n = pl.num_programs(2)

    @pl.when(c == 0)
    def _():
        st_ref[...] = s0_ref[...]

    cs = GLA_CHUNK
    sign = 1 - 2 * d
    row = _iota((cs, cs), 0)
    col = _iota((cs, cs), 1)
    causal = (row - col) * sign >= 0
    causal_b = causal.astype(BF16)
    x = _dot(gk_ref[...].astype(BF16), up_ref[...]) + gb_ref[...]
    log_a = (jnp.minimum(x, 0.0) - jnp.log(1.0 + jnp.exp(-jnp.abs(x)))) * (1.0 / GLA_TAU)
    la_hi, la_lo = _split_hi_lo(log_a)
    bcum = _dot(causal_b, la_hi) + _dot(causal_b, la_lo)
    rid = _iota((cs, 1), 0)
    b_mid = jnp.sum(jnp.where(rid == cs // 2 - d, bcum, 0.0), axis=0, keepdims=True)
    b_end = jnp.sum(jnp.where(rid == (cs - 1) * (1 - d), bcum, 0.0), axis=0, keepdims=True)
    q = q_ref[...] * (GLA_DK ** -0.5)
    k = k_ref[...]
    v = v_ref[...]
    for h in range(GLA_HEADS):
        ks = slice(GLA_DK * h, GLA_DK * (h + 1))
        vs = slice(GLA_DV * h, GLA_DV * (h + 1))
        bh = bcum[:, ks]
        qh = q[:, ks]
        kh = k[:, ks]
        vh = v[:, vs].astype(BF16)
        s_t = st_ref[h]
        q_in = (qh * jnp.exp(bh - b_mid[:, ks])).astype(BF16)
        k_in = (kh * jnp.exp(b_mid[:, ks] - bh)).astype(BF16)
        scores = jnp.where(causal, _dot_nt(q_in, k_in), 0.0).astype(BF16)
        q_st = (qh * jnp.exp(bh)).astype(BF16)
        o_ref[:, vs] = _dot(scores, vh) + _dot_nt(q_st, s_t.astype(BF16))
        k_out = (kh * jnp.exp(b_end[:, ks] - bh)).astype(BF16)
        st_ref[h] = s_t * jnp.exp(b_end[:, ks]) + _dot_tn(vh, k_out)

    @pl.when(c == n - 1)
    def _():
        sf_ref[...] = st_ref[...]


def _gla(qk, v, gk, gk_up, gk_b, s0):
    bsz, cs, w = v.shape
    n = w // GLA_V

    def chunk(d, c):
        return c + d * (n - 1 - 2 * c)

    state_spec = pl.BlockSpec((None, None, GLA_HEADS, GLA_DV, GLA_DK), lambda d, b, c: (d, b, 0, 0, 0))
    return pl.pallas_call(
        _gla_kernel,
        out_shape=(jax.ShapeDtypeStruct((2, bsz, cs, w), F32), jax.ShapeDtypeStruct(s0.shape, F32)),
        grid=(2, bsz, n),
        in_specs=[
            pl.BlockSpec((None, cs, GLA_QK), lambda d, b, c: (b, 0, 2 * chunk(d, c))),
            pl.BlockSpec((None, cs, GLA_QK), lambda d, b, c: (b, 0, 2 * chunk(d, c) + 1)),
            pl.BlockSpec((None, cs, GLA_V), lambda d, b, c: (b, 0, chunk(d, c))),
            pl.BlockSpec((None, cs, LANES), lambda d, b, c: (b, 0, chunk(d, c))),
            pl.BlockSpec((None, LANES, GLA_QK), lambda d, b, c: (d, 0, 0)),
            pl.BlockSpec((None, 1, GLA_QK), lambda d, b, c: (d, 0, 0)),
            state_spec,
        ],
        out_specs=(pl.BlockSpec((None, None, cs, GLA_V), lambda d, b, c: (d, b, 0, chunk(d, c))), state_spec),
        scratch_shapes=[pltpu.VMEM((GLA_HEADS, GLA_DV, GLA_DK), F32)],
        compiler_params=_cp("parallel", "parallel", "arbitrary"),
        name="gla",
    )(qk, qk, v, gk, gk_up, gk_b, s0)


def _to_chunk_major(t, bsz):
    n = t.shape[0] // bsz
    d = t.shape[1]
    t = t.reshape(bsz, n // GLA_CHUNK, GLA_CHUNK, d)
    return jnp.transpose(t, (0, 2, 1, 3)).reshape(bsz, GLA_CHUNK, (n // GLA_CHUNK) * d)


def _from_chunk_major(t, d):
    lead = t.shape[:-2]
    n = t.shape[-1] // d
    t = t.reshape(lead + (GLA_CHUNK, n, d))
    return jnp.swapaxes(t, -3, -2).reshape(lead + (n * GLA_CHUNK, d))


def _rwkv_kernel(z_ref, halo_ref, mu_ref, w0_ref, wup_ref, a0_ref, aup_ref, kk_ref, ka_ref, rk_ref,
                 y_ref, st_ref, kt_s, bt_s, kq_s, rt_s, v_s, w_s, u_s, ya_s, ab_s, pin_s, *, reverse, nc_blocks):
    step = pl.program_id(1)
    blk = RW_BLOCK
    ch = RW_CHUNK
    n_ch = blk // ch

    @pl.when(step == 0)
    def _():
        st_ref[...] = jnp.zeros_like(st_ref)

    z = z_ref[...]
    rowi = _iota((blk, 1), 0)
    if reverse:
        prev = pltpu.roll(z, blk - 1, 0)
        edge = halo_ref[0:1, :]
        at_edge = rowi == blk - 1
    else:
        prev = pltpu.roll(z, 1, 0)
        edge = halo_ref[7:8, :]
        at_edge = rowi == 0
    edge = jnp.where((step == 0) | (step == nc_blocks), 0.0, edge)
    prev = jnp.where(at_edge, edge, prev)
    zs = z + (prev - z) * mu_ref[...]
    r = zs[:, 0:RW_WIDTH]
    k = zs[:, RW_WIDTH:2 * RW_WIDTH]
    v = zs[:, 2 * RW_WIDTH:3 * RW_WIDTH]
    lora = zs[:, 3 * RW_WIDTH:3 * RW_WIDTH + LANES]
    low = _iota((1, LANES), 1) < RW_HEAD
    lora_w = jnp.where(low, jnp.tanh(lora), 0.0).astype(BF16)
    lora_a = jnp.where(low, 0.0, lora).astype(BF16)
    logw = -RW_DECAY_SCALE * _sigmoid(w0_ref[...] + _dot(lora_w, wup_ref[...]))
    a = _sigmoid(a0_ref[...] + _dot(lora_a, aup_ref[...]))
    ones64 = _head_ones(RW_HEAD)
    kk = k * kk_ref[...]
    kp = k * (1.0 + (a - 1.0) * ka_ref[...])
    rkb = r * kp * rk_ref[...]

    pos = rowi % ch
    cl = logw
    sh = 1
    while sh < ch:
        if reverse:
            cl = cl + jnp.where(pos < ch - sh, pltpu.roll(cl, blk - sh, 0), 0.0)
        else:
            cl = cl + jnp.where(pos >= sh, pltpu.roll(cl, sh, 0), 0.0)
        sh *= 2
    p_in = jnp.exp(cl)
    pin_s[...] = p_in
    p_ex = jnp.exp(cl - logw)
    p_inv = jnp.exp(-cl)

    for p in range(RW_PAIRS):
        ls = slice(LANES * p, LANES * (p + 1))
        kkp = kk[:, ls]
        ssq = _head_sum(kkp * kkp, ones64, False)
        kkn = kkp * (1.0 / jnp.maximum(jnp.sqrt(ssq), 1e-12))
        kt_s[p] = (kkn * p_ex[:, ls]).astype(BF16)
        bt_s[p] = (kkn * a[:, ls] * p_inv[:, ls]).astype(BF16)
        kq_s[p] = (kp[:, ls] * p_inv[:, ls]).astype(BF16)
        rt_s[p] = (r[:, ls] * p_in[:, ls]).astype(BF16)
        v_s[p] = v[:, ls].astype(BF16)
        ya_s[p] = _head_sum(rkb[:, ls], ones64, False) * v[:, ls]

    rr = _iota((blk, blk), 0)
    cc = _iota((blk, blk), 1)
    same = rr // ch == cc // ch
    before = (cc > rr) if reverse else (cc < rr)
    strict = same & before
    incl = same & (before | (rr == cc))
    fold0 = (_iota((blk, LANES), 0) % ch == _iota((blk, LANES), 1)).astype(BF16)
    fold1 = (_iota((blk, LANES), 0) % ch + ch == _iota((blk, LANES), 1)).astype(BF16)
    lane_lo = _iota((1, LANES), 1) < RW_HEAD

    def pair_body(p, carry):
        kt = kt_s[p]
        bt = bt_s[p]
        kq = kq_s[p]
        rt = rt_s[p]
        vv = v_s[p]
        zero = jnp.zeros_like(kt)
        rhs = jnp.concatenate([bt, kq], axis=0)
        w_h, u_h, y_h, ab = [], [], [], None
        for hh in range(2):
            mine = lane_lo if hh == 0 else jnp.logical_not(lane_lo)
            lhs = jnp.concatenate([jnp.where(mine, kt, zero), jnp.where(mine, rt, zero)], axis=0)
            g = _dot_nt(lhs, rhs)
            n1 = jnp.where(strict, -g[:blk, :blk], 0.0).astype(BF16)
            a_kq = jnp.where(strict, g[:blk, blk:], 0.0).astype(BF16)
            a_rb = jnp.where(incl, g[blk:, :blk], 0.0).astype(BF16)
            a_rq = jnp.where(incl, g[blk:, blk:], 0.0).astype(BF16)
            n2 = _dot(n1, n1).astype(BF16)
            n4 = _dot(n2, n2).astype(BF16)
            n8 = _dot(n4, n4).astype(BF16)
            rhs_t = jnp.concatenate([kt.astype(F32), _dot(a_kq, vv)], axis=1)
            for nk in (n8, n4, n2, n1):
                rhs_t = rhs_t + _dot(nk, rhs_t.astype(BF16))
            w_h.append(rhs_t[:, :LANES])
            u_h.append(rhs_t[:, LANES:])
            y_h.append(_dot(a_rq, vv))
            fold = _dot(a_rb, fold0 if hh == 0 else fold1)
            ab = fold if ab is None else ab + fold
        w_s[p] = jnp.where(lane_lo, w_h[0], w_h[1]).astype(BF16)
        u_s[p] = jnp.where(lane_lo, u_h[0], u_h[1])
        ya_s[p] = ya_s[p] + jnp.where(lane_lo, y_h[0], y_h[1])
        ab_s[p] = ab.astype(BF16)
        return carry

    lax.fori_loop(0, RW_PAIRS, pair_body, 0)

    blockdiag = (_iota((LANES, LANES), 0) // RW_HEAD) == (_iota((LANES, LANES), 1) // RW_HEAD)

    def chunk_body(i, carry):
        c = (n_ch - 1 - i) if reverse else i
        rows = pl.ds(pl.multiple_of(c * ch, ch), ch)
        end_row = 0 if reverse else ch - 1
        for p in range(RW_PAIRS):
            s = st_ref[p]
            lhs = jnp.concatenate([w_s[p, rows, :], rt_s[p, rows, :]], axis=0)
            m1 = _dot_nt(lhs, s.astype(BF16))
            zc = -(m1[:ch] + u_s[p, rows, :])
            z2 = jnp.concatenate([jnp.where(lane_lo, zc, 0.0), jnp.where(lane_lo, 0.0, zc)], axis=0).astype(BF16)
            yc = m1[ch:] + _dot(ab_s[p, rows, :][:, :2 * ch], z2) + ya_s[p, rows, :]
            y_ref[rows, LANES * p:LANES * (p + 1)] = yc
            zv = jnp.concatenate([zc.astype(BF16), v_s[p, rows, :]], axis=0)
            bk = jnp.concatenate([bt_s[p, rows, :], kq_s[p, rows, :]], axis=0)
            upd = _dot_tn(zv, bk)
            p_end = pin_s[rows, LANES * p:LANES * (p + 1)][end_row:end_row + 1]
            st_ref[p] = (s + jnp.where(blockdiag, upd, 0.0)) * p_end
        return carry

    lax.fori_loop(0, n_ch, chunk_body, 0)


def _rwkv_direction(zall, prm, bsz, nc_blocks, nl_blocks, reverse):
    mu, w0, wup, a0, aup, k_k, k_a, r_k = prm
    rows = zall.shape[0]
    blk = RW_BLOCK
    steps = nc_blocks + nl_blocks
    n_blocks = rows // blk

    def block(b, s):
        lat = bsz * nc_blocks + b * nl_blocks + ((nl_blocks - 1 - (s - nc_blocks)) if reverse else (s - nc_blocks))
        ctx = b * nc_blocks + ((nc_blocks - 1 - s) if reverse else s)
        return jnp.where(s < nc_blocks, ctx, lat)

    def halo(b, s):
        i = block(b, s)
        if reverse:
            return jnp.minimum((i + 1) * (blk // 8), n_blocks * (blk // 8) - 1)
        return jnp.maximum(i * (blk // 8) - 1, 0)

    vec = lambda w: pl.BlockSpec((1, w), lambda b, s: (0, 0))
    mat = lambda: pl.BlockSpec((LANES, RW_WIDTH), lambda b, s: (0, 0))
    pair_bf = pltpu.VMEM((RW_PAIRS, blk, LANES), BF16)
    pair_f = pltpu.VMEM((RW_PAIRS, blk, LANES), F32)
    return pl.pallas_call(
        functools.partial(_rwkv_kernel, reverse=reverse, nc_blocks=nc_blocks),
        out_shape=jax.ShapeDtypeStruct((rows, RW_WIDTH), F32),
        grid=(bsz, steps),
        in_specs=[
            pl.BlockSpec((blk, RW_Z), lambda b, s: (block(b, s), 0)),
            pl.BlockSpec((8, RW_Z), lambda b, s: (halo(b, s), 0)),
            vec(RW_Z), vec(RW_WIDTH), mat(), vec(RW_WIDTH), mat(), vec(RW_WIDTH), vec(RW_WIDTH), vec(RW_WIDTH),
        ],
        out_specs=pl.BlockSpec((blk, RW_WIDTH), lambda b, s: (block(b, s), 0)),
        scratch_shapes=[
            pltpu.VMEM((RW_PAIRS, LANES, LANES), F32),
            pair_bf, pair_bf, pair_bf, pair_bf, pair_bf,
            pair_bf, pair_f, pair_f, pair_bf,
            pltpu.VMEM((blk, RW_WIDTH), F32),
        ],
        compiler_params=_cp("parallel", "arbitrary"),
        name="rwkv_bwd" if reverse else "rwkv_fwd",
    )(zall, zall, mu, w0, wup, a0, aup, k_k, k_a, r_k)


def _s5_out_kernel(y_ref, u_ref, d_ref, wv_ref, wg_ref, o_ref, z_s):
    @pl.when(pl.program_id(1) == 0)
    def _():
        z_s[...] = _gelu_tanh(y_ref[...] + d_ref[...] * u_ref[...]).astype(BF16)

    zz = z_s[...]
    o_ref[...] = _dot(zz, wv_ref[...]) * _sigmoid(_dot(zz, wg_ref[...]))


def _s5_out(y, u_src, u_blk, d, w_glu, bm, row0):
    n = y.shape[0] - row0 * bm
    bn = 1024
    nj = D_MODEL // bn
    return pl.pallas_call(
        _s5_out_kernel,
        out_shape=jax.ShapeDtypeStruct((n, D_MODEL), F32),
        grid=(n // bm, nj),
        in_specs=[
            pl.BlockSpec((bm, S5_WIDTH), lambda i, j: (i + row0, 0)),
            pl.BlockSpec((bm, S5_WIDTH), lambda i, j: (i + row0, u_blk)),
            pl.BlockSpec((1, S5_WIDTH), lambda i, j: (0, 0)),
            pl.BlockSpec((S5_WIDTH, bn), lambda i, j: (0, j)),
            pl.BlockSpec((S5_WIDTH, bn), lambda i, j: (0, j + nj)),
        ],
        out_specs=pl.BlockSpec((bm, bn), lambda i, j: (i, j)),
        scratch_shapes=[pltpu.VMEM((bm, S5_WIDTH), BF16)],
        compiler_params=_cp("parallel", "arbitrary"),
        name="s5_out",
    )(y, u_src, d, w_glu, w_glu)


def _rwkv_out_kernel(yf_ref, yb_ref, g_ref, gup_ref, lw_ref, lb_ref, w_ref, o_ref, z_s):
    @pl.when(pl.program_id(1) == 0)
    def _():
        ones64 = _head_ones(RW_HEAD)
        gate = _dot(_sigmoid(g_ref[...]).astype(BF16), gup_ref[...])
        for p in range(RW_PAIRS):
            ls = slice(LANES * p, LANES * (p + 1))
            y = yf_ref[:, ls] + yb_ref[:, ls]
            mean = _head_sum(y, ones64, True) * (1.0 / RW_HEAD)
            yc = y - mean
            var = _head_sum(yc * yc, ones64, True) * (1.0 / RW_HEAD)
            yn = yc * lax.rsqrt(var + RW_GN_EPS) * lw_ref[:, ls] + lb_ref[:, ls]
            z_s[:, ls] = (yn * gate[:, ls]).astype(BF16)

    o_ref[...] = _dot(z_s[...], w_ref[...])


def _rwkv_out(yf, yb, zall, g_up, ln_w, ln_b, w_proj, bm, row0):
    n = yf.shape[0] - row0 * bm
    bn = 1024
    rows = lambda w, blk=0: pl.BlockSpec((bm, w), lambda i, j: (i + row0, blk))
    vec = pl.BlockSpec((1, RW_WIDTH), lambda i, j: (0, 0))
    return pl.pallas_call(
        _rwkv_out_kernel,
        out_shape=jax.ShapeDtypeStruct((n, D_MODEL), F32),
        grid=(n // bm, D_MODEL // bn),
        in_specs=[rows(RW_WIDTH), rows(RW_WIDTH), rows(LANES, RW_Z // LANES),
                  pl.BlockSpec((LANES, RW_WIDTH), lambda i, j: (0, 0)), vec, vec,
                  pl.BlockSpec((RW_WIDTH, bn), lambda i, j: (0, j))],
        out_specs=pl.BlockSpec((bm, bn), lambda i, j: (i, j)),
        scratch_shapes=[pltpu.VMEM((bm, RW_WIDTH), BF16)],
        compiler_params=_cp("parallel", "arbitrary"),
        name="rwkv_out",
    )(yf, yb, zall, g_up, ln_w, ln_b, w_proj)


def _gla_out_kernel(of_ref, ob_ref, og_ref, g_ref, w_ref, o_ref, z_s):
    @pl.when(pl.program_id(1) == 0)
    def _():
        for h in range(GLA_HEADS):
            vs = slice(GLA_DV * h, GLA_DV * (h + 1))
            o = of_ref[:, vs] + ob_ref[:, vs]
            on = o * lax.rsqrt(jnp.mean(o * o, axis=-1, keepdims=True) + NORM_EPS) * g_ref[:, vs]
            z_s[:, vs] = (on * _silu(og_ref[:, vs])).astype(BF16)

    o_ref[...] = _dot(z_s[...], w_ref[...])


def _gla_out(of, ob, og_src, og_blk, norm_g, w_proj, bm, row0):
    n = of.shape[0] - row0 * bm
    bn = 1024
    rows = lambda blk=0: pl.BlockSpec((bm, GLA_V), lambda i, j: (i + row0, blk))
    return pl.pallas_call(
        _gla_out_kernel,
        out_shape=jax.ShapeDtypeStruct((n, D_MODEL), F32),
        grid=(n // bm, D_MODEL // bn),
        in_specs=[rows(), rows(), rows(og_blk), pl.BlockSpec((1, GLA_V), lambda i, j: (0, 0)),
                  pl.BlockSpec((GLA_V, bn), lambda i, j: (0, j))],
        out_specs=pl.BlockSpec((bm, bn), lambda i, j: (i, j)),
        scratch_shapes=[pltpu.VMEM((bm, GLA_V), BF16)],
        compiler_params=_cp("parallel", "arbitrary"),
        name="gla_out",
    )(of, ob, og_src, norm_g, w_proj)


def _merge_kernel(ga_ref, gb_ref, gc_ref, pa_ref, pb_ref, pc_ref, w_ref, x_ref, gate_ref, o_ref, z_s):
    @pl.when(pl.program_id(1) == 0)
    def _():
        m = (_sigmoid(ga_ref[...]) * pa_ref[...] + _sigmoid(gb_ref[...]) * pb_ref[...]
             + _sigmoid(gc_ref[...]) * pc_ref[...])
        z_s[...] = m.astype(BF16)

    o_ref[...] = x_ref[...] + gate_ref[...] * _dot(z_s[...], w_ref[...])


def _merge(gates_src, gate_blk0, pa, pb, pc, w_out, x, mod, layer, bm, row0, nc, seq):
    n = pa.shape[0]
    bn = 1024
    d = D_MODEL
    nb = d // bn
    gate = lambda t: pl.BlockSpec((bm, d), lambda i, j: (i + row0, gate_blk0 + t))
    branch = pl.BlockSpec((bm, d), lambda i, j: (i, 0))
    mod_index = _mod_spec(layer, 2, bm, nc, seq).index_map
    shifted = pl.BlockSpec((None, None, None, 1, bn), lambda i, j: mod_index(i + row0)[:4] + (j,))
    return pl.pallas_call(
        _merge_kernel,
        out_shape=jax.ShapeDtypeStruct((n, d), F32),
        grid=(n // bm, nb),
        in_specs=[gate(0), gate(1), gate(2), branch, branch, branch,
                  pl.BlockSpec((d, bn), lambda i, j: (0, j)),
                  pl.BlockSpec((bm, bn), lambda i, j: (i + row0, j)), shifted],
        out_specs=pl.BlockSpec((bm, bn), lambda i, j: (i, j)),
        scratch_shapes=[pltpu.VMEM((bm, d), BF16)],
        compiler_params=_cp("parallel", "arbitrary"),
        name="merge",
    )(gates_src, gates_src, gates_src, pa, pb, pc, w_out, x, mod)


def _router_kernel(x_ref, g_ref, sh_ref, sc_ref, wr_hi_ref, wr_lo_ref, br_ref, v_ref, comb_ref):
    x = x_ref[...]
    y = x * lax.rsqrt(jnp.mean(x * x, axis=-1, keepdims=True) + NORM_EPS) * g_ref[...]
    t = y * (1.0 + sc_ref[...]) + sh_ref[...]
    v_ref[...] = t.astype(BF16)
    t_hi, t_lo = _split_hi_lo(t)
    logits = (_dot(t_hi, wr_hi_ref[...]) + _dot(t_lo, wr_hi_ref[...]) + _dot(t_hi, wr_lo_ref[...])) + br_ref[...]
    lane = _iota(logits.shape, 1).astype(F32)
    neg = jnp.float32(-jnp.inf)
    big = jnp.float32(LANES)
    l1 = jnp.where(lane < MOE_GROUPS, logits, neg)
    m1 = jnp.max(l1, axis=-1, keepdims=True)
    p_top = 1.0 / jnp.sum(jnp.exp(l1 - m1), axis=-1, keepdims=True)
    grp = jnp.min(jnp.where(l1 == m1, lane, big), axis=-1, keepdims=True)
    lo = MOE_LANE0 + MOE_PER_GROUP * grp
    in_grp = (lane >= lo) & (lane < lo + MOE_PER_GROUP)
    l2 = jnp.where(in_grp, logits, neg)
    v1 = jnp.max(l2, axis=-1, keepdims=True)
    i1 = jnp.min(jnp.where(l2 == v1, lane, big), axis=-1, keepdims=True)
    l3 = jnp.where(lane == i1, neg, l2)
    v2 = jnp.max(l3, axis=-1, keepdims=True)
    i2 = jnp.min(jnp.where(l3 == v2, lane, big), axis=-1, keepdims=True)
    e2 = jnp.exp(v2 - v1)
    w1 = p_top / (1.0 + e2)
    w2 = p_top * e2 / (1.0 + e2)
    comb_ref[...] = jnp.where(lane == i1, w1, 0.0) + jnp.where(lane == i2, w2, 0.0)


def _router(x, g, mod, layer, wr_hi, wr_lo, br, bm, nc, seq):
    n, d = x.shape
    return pl.pallas_call(
        _router_kernel,
        out_shape=(jax.ShapeDtypeStruct((n, d), BF16), jax.ShapeDtypeStruct((n, LANES), F32)),
        grid=(n // bm,),
        in_specs=[
            pl.BlockSpec((bm, d), lambda i: (i, 0)),
            pl.BlockSpec((None, 1, d), lambda i: (layer, 0, 0)),
            _mod_spec(layer, 3, bm, nc, seq), _mod_spec(layer, 4, bm, nc, seq),
            pl.BlockSpec((d, LANES), lambda i: (0, 0)), pl.BlockSpec((d, LANES), lambda i: (0, 0)),
            pl.BlockSpec((1, LANES), lambda i: (0, 0)),
        ],
        out_specs=(pl.BlockSpec((bm, d), lambda i: (i, 0)), pl.BlockSpec((bm, LANES), lambda i: (i, 0))),
        compiler_params=_cp("parallel"),
        name="moe_router",
    )(x, g, mod, mod, wr_hi, wr_lo, br)


def _experts_kernel(v_ref, comb_ref, wg_ref, wu_ref, wd_ref, x_ref, gate_ref, o_ref, acc_s):
    e = pl.program_id(1)

    @pl.when(e == 0)
    def _():
        acc_s[...] = jnp.zeros_like(acc_s)

    t = v_ref[...]
    comb = comb_ref[...]
    w_e = jnp.sum(jnp.where(_iota(comb.shape, 1) == e + MOE_LANE0, comb, 0.0), axis=-1, keepdims=True)
    hid = _silu(_dot(t, wg_ref[...])) * _dot(t, wu_ref[...]) * w_e
    acc_s[...] += _dot(hid.astype(BF16), wd_ref[...])

    @pl.when(e == pl.num_programs(1) - 1)
    def _():
        o_ref[...] = x_ref[...] + gate_ref[...] * acc_s[...]


def _experts(v, comb, w_gate, w_up, w_down, x, mod, layer, bm, nc, seq):
    n, d = v.shape
    ne, _, hdim = w_gate.shape
    return pl.pallas_call(
        _experts_kernel,
        out_shape=jax.ShapeDtypeStruct((n, d), F32),
        grid=(n // bm, ne),
        in_specs=[
            pl.BlockSpec((bm, d), lambda i, e: (i, 0)),
            pl.BlockSpec((bm, LANES), lambda i, e: (i, 0)),
            pl.BlockSpec((None, d, hdim), lambda i, e: (e, 0, 0)),
            pl.BlockSpec((None, d, hdim), lambda i, e: (e, 0, 0)),
            pl.BlockSpec((None, hdim, d), lambda i, e: (e, 0, 0)),
            pl.BlockSpec((bm, d), lambda i, e: (i, 0)),
            _mod_spec(layer, 5, bm, nc, seq),
        ],
        out_specs=pl.BlockSpec((bm, d), lambda i, e: (i, 0)),
        scratch_shapes=[pltpu.VMEM((bm, d), F32)],
        compiler_params=_cp("parallel", "arbitrary"),
        name="moe_experts",
    )(v, comb, w_gate, w_up, w_down, x, mod)


def _final_norm_kernel(x_ref, g_ref, o_ref):
    x = x_ref[...]
    o_ref[...] = x * lax.rsqrt(jnp.mean(x * x, axis=-1, keepdims=True) + NORM_EPS) * g_ref[...]


def _final_norm(x, g, bm, row0):
    n, d = x.shape
    n -= row0 * bm
    return pl.pallas_call(
        _final_norm_kernel,
        out_shape=jax.ShapeDtypeStruct((n, d), F32),
        grid=(n // bm,),
        in_specs=[pl.BlockSpec((bm, d), lambda i: (i + row0, 0)), pl.BlockSpec((1, d), lambda i: (0, 0))],
        out_specs=pl.BlockSpec((bm, d), lambda i: (i, 0)),
        compiler_params=_cp("parallel"),
        name="final_norm",
    )(x, g)


_COL = dict(s5=0, rw=S5_WIDTH, rg=S5_WIDTH + 3 * RW_WIDTH + 128, q=4352, k=4864, v=5376, gk=6400, og=6416, gates=7440)


def _pad_rows(w, rows):
    return jnp.pad(w, ((0, rows - w.shape[0]), (0, 0)))


def _layer(i, last, xa, mod, bsz, seq, ctx_len, p):
    nc = bsz * ctx_len
    rows = xa.shape[0]
    bm = 512
    msel = dict(nc=nc, seq=seq)
    u = _normmod(xa, p['g_norm1'].reshape(-1, 1, D_MODEL), mod, i, (0, 1), bm, **msel)

    w_in = p['w_in'][i]
    col = lambda a, w: w_in[:, a:a + w].astype(BF16)
    bm_in = 1088 if rows % 1088 == 0 else bm
    w_a = jnp.concatenate([col(0, S5_WIDTH), col(_COL['og'], GLA_V), col(_COL['gates'], 3 * D_MODEL)], axis=1)
    w_z = jnp.concatenate([col(_COL['rw'], 3 * RW_WIDTH + 128), col(_COL['rg'], 128)], axis=1)
    w_qk = col(_COL['q'], 2 * GLA_QK)
    w_v = col(_COL['v'], GLA_V)
    w_gk = jnp.pad(col(_COL['gk'], 16), ((0, 0), (0, LANES - 16)))
    pa_in = _mm(u, w_a, bm_in, 1024)
    zall = _mm(u, w_z, bm_in, 1664)
    qk = _mm(u, w_qk, bm_in, 1024)
    vv = _mm(u, w_v, bm_in, 1024)
    gk = _mm(u, w_gk, bm_in, LANES)

    u_s5 = pa_in[:, :S5_WIDTH].astype(BF16)
    uc_t = _s5_to_tiles(u_s5[:nc], bsz)
    ul_t = _s5_to_tiles(u_s5[nc:], bsz)
    ya = None
    for d in range(2):
        ops = _s5_operators(*(p[k][i, d] for k in ('s5_a_re', 's5_a_im', 's5_log_dt', 's5_b_re', 's5_b_im',
                                                  's5_c_re', 's5_c_im')), reverse=bool(d))
        yc_t, yl_t = _s5_direction(uc_t, ul_t, ops, bool(d))
        yd = jnp.concatenate([_s5_from_tiles(yc_t), _s5_from_tiles(yl_t)], axis=0)
        ya = yd if ya is None else ya + yd

    yb = []
    for d in range(2):
        prm = (
            jnp.pad(p['rw_mu'][i, d], (0, RW_Z - p['rw_mu'].shape[-1])).reshape(1, RW_Z),
            p['rw_w0'][i, d].reshape(1, -1),
            _pad_rows(p['rw_w_up'][i, d], LANES).astype(BF16),
            p['rw_a0'][i, d].reshape(1, -1),
            jnp.pad(p['rw_a_up'][i, d], ((RW_HEAD, 0), (0, 0))).astype(BF16),
            p['rw_k_k'][i].reshape(1, -1), p['rw_k_a'][i].reshape(1, -1), p['rw_r_k'][i].reshape(1, -1),
        )
        yb.append(_rwkv_direction(zall, prm, bsz, ctx_len // RW_BLOCK, seq // RW_BLOCK, bool(d)))

    gk_up = jnp.pad(p['gl_gk_up'][i], ((0, 0), (0, LANES - 16), (0, 0))).astype(BF16)
    gk_b = p['gl_gk_b'][i].reshape(2, 1, GLA_QK)
    s0 = jnp.zeros((2, bsz, GLA_HEADS, GLA_DV, GLA_DK), F32)
    oc, s_ctx = _gla(_to_chunk_major(qk[:nc], bsz), _to_chunk_major(vv[:nc], bsz), _to_chunk_major(gk[:nc], bsz),
                     gk_up, gk_b, s0)
    rows_l = seq // GRID_W
    lat = lambda t: t[nc:].reshape(bsz, rows_l, GRID_W * t.shape[1])
    ol, _ = _gla(lat(qk), lat(vv), lat(gk), gk_up, gk_b, s_ctx)
    o_dir = [jnp.concatenate([_from_chunk_major(oc[d], GLA_V).reshape(nc, GLA_V),
                              ol[d].reshape(bsz * seq, GLA_V)], axis=0) for d in range(2)]

    row0 = (nc // bm) if last else 0
    pa = _s5_out(ya, pa_in, 0, p['s5_d'][i].reshape(1, -1), p['s5_w_glu'][i].astype(BF16), bm, row0)
    pb = _rwkv_out(yb[0], yb[1], zall, p['rw_g_up'][i].astype(BF16), p['rw_ln_w'][i].reshape(1, -1),
                   p['rw_ln_b'][i].reshape(1, -1), p['rw_w_proj'][i].astype(BF16), bm, row0)
    pc = _gla_out(o_dir[0], o_dir[1], pa_in, 1, p['gl_norm_g'][i].reshape(1, -1), p['gl_w_proj'][i].astype(BF16),
                  bm, row0)
    bm_merge = bm // 2
    xm = _merge(pa_in, 1, pa, pb, pc, p['w_out'][i].astype(BF16), xa, mod, i, bm_merge, row0 * 2, nc, seq)
    if last:
        msel = dict(nc=0, seq=seq)

    wr = jnp.pad(jnp.concatenate([p['moe_wg1'][i], p['moe_wg2'][i]], axis=1), ((0, 0), (0, LANES - 36)))
    wr_hi = wr.astype(BF16)
    wr_lo = (wr - wr_hi.astype(F32)).astype(BF16)
    br = jnp.pad(jnp.concatenate([p['moe_bg1'][i], p['moe_bg2'][i]]), (0, LANES - 36)).reshape(1, LANES)
    vmoe, comb = _router(xm, p['g_norm2'].reshape(-1, 1, D_MODEL), mod, i, wr_hi, wr_lo, br, bm, **msel)
    return _experts(vmoe, comb, p['moe_w_gate'][i].astype(BF16), p['moe_w_up'][i].astype(BF16),
                    p['moe_w_down'][i].astype(BF16), xm, mod, i, bm, **msel)


def kernel(x, c, ctx, c_ctx, w_mod, b_mod, g_norm1, g_norm2, w_in, s5_a_re, s5_a_im, s5_log_dt, s5_b_re, s5_b_im,
           s5_c_re, s5_c_im, s5_d, s5_w_glu, rw_mu, rw_w0, rw_w_up, rw_a0, rw_a_up, rw_k_k, rw_k_a, rw_r_k, rw_g_up,
           rw_ln_w, rw_ln_b, rw_w_proj, gl_gk_up, gl_gk_b, gl_norm_g, gl_w_proj, w_out, moe_wg1, moe_bg1, moe_wg2,
           moe_bg2, moe_w_gate, moe_w_up, moe_w_down, g_final):
    p = dict(g_norm1=g_norm1, g_norm2=g_norm2, w_in=w_in, s5_a_re=s5_a_re, s5_a_im=s5_a_im, s5_log_dt=s5_log_dt,
             s5_b_re=s5_b_re, s5_b_im=s5_b_im, s5_c_re=s5_c_re, s5_c_im=s5_c_im, s5_d=s5_d, s5_w_glu=s5_w_glu,
             rw_mu=rw_mu, rw_w0=rw_w0, rw_w_up=rw_w_up, rw_a0=rw_a0, rw_a_up=rw_a_up, rw_k_k=rw_k_k, rw_k_a=rw_k_a,
             rw_r_k=rw_r_k.reshape(rw_r_k.shape[0], -1), rw_g_up=rw_g_up, rw_ln_w=rw_ln_w, rw_ln_b=rw_ln_b,
             rw_w_proj=rw_w_proj, gl_gk_up=gl_gk_up, gl_gk_b=gl_gk_b, gl_norm_g=gl_norm_g, gl_w_proj=gl_w_proj,
             w_out=w_out, moe_wg1=moe_wg1, moe_bg1=moe_bg1, moe_wg2=moe_wg2, moe_bg2=moe_bg2, moe_w_gate=moe_w_gate,
             moe_w_up=moe_w_up, moe_w_down=moe_w_down)
    bsz, seq, d = x.shape
    ctx_len = ctx.shape[1]
    depth = w_mod.shape[0]
    cc = jnp.concatenate([c, c_ctx[None], jnp.zeros((8 - bsz - 1, d), F32)], axis=0)
    mod = _adaln(cc, w_mod, b_mod).reshape(depth, 8, 6, 1, d)
    xa = jnp.concatenate([ctx.reshape(bsz * ctx_len, d), x.reshape(bsz * seq, d)], axis=0)
    for i in range(depth):
        xa = _layer(i, i == depth - 1, xa, mod, bsz, seq, ctx_len, p)
    out = _final_norm(xa, g_final.reshape(1, d), 512, 0)
    return out.reshape(bsz, seq, d)
```

```python
import functools
import math

import jax
import jax.numpy as jnp
from jax import lax
from jax.experimental import pallas as pl
from jax.experimental.pallas import tpu as pltpu

F32 = jnp.float32
BF16 = jnp.bfloat16

D_MODEL = 2048
GRID_W = 64
NORM_EPS = 1e-6

S5_WIDTH = 1024
S5_GROUP = 16
S5_GROUPS = 64
S5_STATE = 64
S5_MAX_RE = -1e-4
S5_TILE = 16
S5_PAIRS = S5_GROUPS // 2

RW_WIDTH = 1024
RW_HEAD = 64
RW_DECAY_SCALE = 0.606531
RW_GN_EPS = 64e-5
RW_BLOCK = 256
RW_CHUNK = 16
RW_PAIRS = RW_WIDTH // 128
RW_Z = 3 * RW_WIDTH + 128

GLA_HEADS = 4
GLA_DK = 128
GLA_DV = 256
GLA_QK = 512
GLA_V = 1024
GLA_TAU = 16.0
GLA_CHUNK = 64

MOE_GROUPS = 4
MOE_PER_GROUP = 8
MOE_EXPERTS = 32
MOE_HIDDEN = 256
MOE_LANE0 = MOE_GROUPS

LANES = 128
VMEM_LIMIT = 56 * 1024 * 1024


def _cp(*sem):
    return pltpu.CompilerParams(dimension_semantics=sem, vmem_limit_bytes=VMEM_LIMIT)


def _dot(a, b):
    return jnp.dot(a, b, preferred_element_type=F32)


def _dot_nt(a, b):
    return lax.dot_general(a, b, (((1,), (1,)), ((), ())), preferred_element_type=F32)


def _dot_tn(a, b):
    return lax.dot_general(a, b, (((0,), (0,)), ((), ())), preferred_element_type=F32)


def _sigmoid(x):
    return 1.0 / (1.0 + jnp.exp(-x))


def _silu(x):
    return x * _sigmoid(x)


def _gelu_tanh(x):
    return 0.5 * x * (1.0 + jnp.tanh(math.sqrt(2.0 / math.pi) * (x + 0.044715 * (x * x * x))))


def _split_hi_lo(x):
    hi = x.astype(BF16)
    lo = (x - hi.astype(F32)).astype(BF16)
    return hi, lo


def _iota(shape, dim):
    return lax.broadcasted_iota(jnp.int32, shape, dim)


def _head_ones(width):
    return (_iota((LANES, LANES), 0) // width == _iota((LANES, LANES), 1) // width).astype(BF16)


def _head_sum(x, ones, exact):
    if exact:
        hi, lo = _split_hi_lo(x)
        return _dot(hi, ones) + _dot(lo, ones)
    return _dot(x.astype(BF16), ones)


def _adaln_kernel(c_ref, w_ref, b_ref, o_ref):
    c = c_ref[...]
    o_ref[...] = _dot(_silu(c).astype(BF16), w_ref[...].astype(BF16)) + b_ref[...]


def _adaln(cc, w_mod, b_mod):
    depth, d, n = w_mod.shape
    bn = 1536
    return pl.pallas_call(
        _adaln_kernel,
        out_shape=jax.ShapeDtypeStruct((depth, 8, n), F32),
        grid=(depth, n // bn),
        in_specs=[
            pl.BlockSpec((8, d), lambda l, j: (0, 0)),
            pl.BlockSpec((None, d, bn), lambda l, j: (l, 0, j)),
            pl.BlockSpec((None, 1, bn), lambda l, j: (l, 0, j)),
        ],
        out_specs=pl.BlockSpec((None, 8, bn), lambda l, j: (l, 0, j)),
        compiler_params=_cp("parallel", "parallel"),
        name="adaln",
    )(cc, w_mod, b_mod.reshape(depth, 1, n))


def _mod_spec(layer, part, bm, nc, seq):
    def index(i, *_):
        r0 = i * bm
        return (layer, jnp.where(r0 < nc, 2, (r0 - nc) // seq), part, 0, 0)
    return pl.BlockSpec((None, None, None, 1, D_MODEL), index)


def _normmod_kernel(x_ref, g_ref, sh_ref, sc_ref, o_ref):
    x = x_ref[...]
    y = x * lax.rsqrt(jnp.mean(x * x, axis=-1, keepdims=True) + NORM_EPS) * g_ref[...]
    o_ref[...] = (y * (1.0 + sc_ref[...]) + sh_ref[...]).astype(o_ref.dtype)


def _normmod(x, g, mod, layer, parts, bm, nc, seq):
    n, d = x.shape
    return pl.pallas_call(
        _normmod_kernel,
        out_shape=jax.ShapeDtypeStruct((n, d), BF16),
        grid=(n // bm,),
        in_specs=[
            pl.BlockSpec((bm, d), lambda i: (i, 0)),
            pl.BlockSpec((None, 1, d), lambda i: (layer, 0, 0)),
            _mod_spec(layer, parts[0], bm, nc, seq),
            _mod_spec(layer, parts[1], bm, nc, seq),
        ],
        out_specs=pl.BlockSpec((bm, d), lambda i: (i, 0)),
        compiler_params=_cp("parallel"),
        name="normmod",
    )(x, g, mod, mod)


def _mm_kernel(x_ref, w_ref, o_ref):
    o_ref[...] = _dot(x_ref[...], w_ref[...]).astype(o_ref.dtype)


def _mm(x, w, bm, bn, out_dtype=F32):
    m, k = x.shape
    n = w.shape[1]
    return pl.pallas_call(
        _mm_kernel,
        out_shape=jax.ShapeDtypeStruct((m, n), out_dtype),
        grid=(m // bm, n // bn),
        in_specs=[pl.BlockSpec((bm, k), lambda i, j: (i, 0)), pl.BlockSpec((k, bn), lambda i, j: (0, j))],
        out_specs=pl.BlockSpec((bm, bn), lambda i, j: (i, j)),
        compiler_params=_cp("parallel", "parallel"),
        name="mm",
    )(x, w)


def _s5_operators(a_re, a_im, log_dt, b_re, b_im, c_re, c_im, reverse):
    hp = lax.Precision.HIGHEST
    t = S5_TILE
    lam = lax.complex(jnp.minimum(a_re, S5_MAX_RE), a_im)
    ldt = lam * jnp.exp(log_dt)[:, None]
    lam_bar = jnp.exp(ldt)
    b_bar = ((lam_bar - 1.0) / lam)[..., None] * lax.complex(b_re, b_im)
    c_mat = lax.complex(c_re, c_im)
    taus = jnp.arange(t + 1, dtype=F32)
    lam_pow = jnp.exp(ldt[None] * taus[:, None, None])
    k_tau = jnp.einsum('gop,tgp,gpi->tgoi', c_mat, lam_pow[:t], b_bar, precision=hp).real
    s_idx = jnp.arange(t)[:, None]
    t_idx = jnp.arange(t)[None, :]
    lag = (s_idx - t_idx) if reverse else (t_idx - s_idx)
    k_st = jnp.where((lag >= 0)[:, :, None, None, None], k_tau[jnp.clip(lag, 0, t - 1)], 0.0)
    k_g = jnp.transpose(k_st, (2, 0, 4, 1, 3))
    eye2 = jnp.eye(2, dtype=F32)
    kmat = jnp.einsum('pgsito,gh->psgitho', k_g.reshape(S5_PAIRS, 2, t, S5_GROUP, t, S5_GROUP), eye2)
    kmat = kmat.reshape(S5_PAIRS, 2 * t * S5_GROUP, 2 * t * S5_GROUP)
    e_s = jnp.arange(t) if reverse else (t - 1 - jnp.arange(t))
    b_end = lam_pow[e_s][:, :, :, None] * b_bar[None]
    b_end = jnp.transpose(b_end, (1, 0, 3, 2)).reshape(S5_PAIRS, 2, t, S5_GROUP, S5_STATE)
    b_ri = jnp.stack([b_end.real, b_end.imag], axis=-2)
    wend = jnp.einsum('pgsicq,gh->psgichq', b_ri, eye2).reshape(S5_PAIRS, 2 * t * S5_GROUP, 4 * S5_STATE)
    e_t = (t - jnp.arange(t)) if reverse else (jnp.arange(t) + 1)
    c_g = jnp.transpose(c_mat, (0, 2, 1))[:, :, None, :] * jnp.transpose(lam_pow[e_t], (1, 2, 0))[..., None]
    c_ri = jnp.stack([c_g.real, -c_g.imag], axis=1).reshape(S5_PAIRS, 2, 2, S5_STATE, t, S5_GROUP)
    cmat = jnp.einsum('pgcqto,gh->pcgqtho', c_ri, eye2).reshape(S5_PAIRS, 4 * S5_STATE, 2 * t * S5_GROUP)
    lam_sq = jnp.exp(ldt[None] * (t * 2.0 ** jnp.arange(8, dtype=F32))[:, None, None])
    lam_sq = jnp.transpose(lam_sq, (1, 0, 2)).reshape(S5_PAIRS, 2, 8, S5_STATE)
    lam_sq = jnp.transpose(lam_sq, (0, 2, 1, 3)).reshape(S5_PAIRS, 8, 2 * S5_STATE)
    return kmat.astype(BF16), wend.astype(BF16), cmat.astype(BF16), lam_sq.real, lam_sq.imag


def _s5_tile_scan(e_re, e_im, lre, lim, reverse):
    n = e_re.shape[0]
    row = _iota((n, 1), 0)
    step = 1
    k = 0
    while step < n:
        if reverse:
            s_re = pltpu.roll(e_re, n - step, 0)
            s_im = pltpu.roll(e_im, n - step, 0)
            ok = row < n - step
        else:
            s_re = pltpu.roll(e_re, step, 0)
            s_im = pltpu.roll(e_im, step, 0)
            ok = row >= step
        a_re = lre[k:k + 1]
        a_im = lim[k:k + 1]
        e_re = e_re + jnp.where(ok, a_re * s_re - a_im * s_im, 0.0)
        e_im = e_im + jnp.where(ok, a_re * s_im + a_im * s_re, 0.0)
        step *= 2
        k += 1
    return e_re, e_im


def _s5_kernel(uc_ref, ul_ref, kmat_ref, wend_ref, cmat_ref, lre_ref, lim_ref, yc_ref, yl_ref, *, reverse):
    kmat = kmat_ref[...]
    wend = wend_ref[...]
    cmat = cmat_ref[...]
    lre = lre_ref[...]
    lim = lim_ref[...]
    half = 2 * S5_STATE

    def entering(h, n, init):
        row = _iota((n, 1), 0)
        if reverse:
            return jnp.where(row == n - 1, init, pltpu.roll(h, n - 1, 0))
        return jnp.where(row == 0, init, pltpu.roll(h, 1, 0))

    for b in range(uc_ref.shape[0]):
        uc = uc_ref[b]
        ul = ul_ref[b]
        nc = uc.shape[0]
        nl = ul.shape[0]
        ec = _dot(uc, wend)
        el = _dot(ul, wend)
        hc_re, hc_im = _s5_tile_scan(ec[:, :half], ec[:, half:], lre, lim, reverse)
        last = 0 if reverse else nc - 1
        car_re = hc_re[last:last + 1]
        car_im = hc_im[last:last + 1]
        first = _iota((nl, 1), 0) == (nl - 1 if reverse else 0)
        el_re = el[:, :half] + jnp.where(first, lre[0:1] * car_re - lim[0:1] * car_im, 0.0)
        el_im = el[:, half:] + jnp.where(first, lre[0:1] * car_im + lim[0:1] * car_re, 0.0)
        hl_re, hl_im = _s5_tile_scan(el_re, el_im, lre, lim, reverse)
        zero = jnp.zeros((1, half), F32)
        pc_re = entering(hc_re, nc, zero).astype(BF16)
        pc_im = entering(hc_im, nc, zero).astype(BF16)
        pl_re = entering(hl_re, nl, car_re).astype(BF16)
        pl_im = entering(hl_im, nl, car_im).astype(BF16)
        yc_ref[b] = _dot(uc, kmat) + _dot(pc_re, cmat[:half]) + _dot(pc_im, cmat[half:])
        yl_ref[b] = _dot(ul, kmat) + _dot(pl_re, cmat[:half]) + _dot(pl_im, cmat[half:])


def _s5_direction(u_ctx, u_lat, ops, reverse):
    kmat, wend, cmat, lre, lim = ops
    _, bsz, nct, w = u_ctx.shape
    nlt = u_lat.shape[2]
    per_pair = lambda shape: pl.BlockSpec((None,) + shape, lambda p: (p,) + (0,) * len(shape))
    return pl.pallas_call(
        functools.partial(_s5_kernel, reverse=reverse),
        out_shape=(jax.ShapeDtypeStruct(u_ctx.shape, F32), jax.ShapeDtypeStruct(u_lat.shape, F32)),
        grid=(S5_PAIRS,),
        in_specs=[per_pair((bsz, nct, w)), per_pair((bsz, nlt, w)), per_pair(kmat.shape[1:]),
                  per_pair(wend.shape[1:]), per_pair(cmat.shape[1:]), per_pair(lre.shape[1:]),
                  per_pair(lim.shape[1:])],
        out_specs=(per_pair((bsz, nct, w)), per_pair((bsz, nlt, w))),
        compiler_params=_cp("parallel"),
        name="s5_bwd" if reverse else "s5_fwd",
    )(u_ctx, u_lat, kmat, wend, cmat, lre, lim)


def _s5_to_tiles(u, bsz):
    n = u.shape[0] // bsz
    t = u.reshape(bsz, n // S5_TILE, S5_TILE, S5_PAIRS, 2 * S5_GROUP)
    return jnp.transpose(t, (3, 0, 1, 2, 4)).reshape(S5_PAIRS, bsz, n // S5_TILE, S5_TILE * 2 * S5_GROUP)


def _s5_from_tiles(y):
    _, bsz, nt, _ = y.shape
    t = y.reshape(S5_PAIRS, bsz, nt, S5_TILE, 2 * S5_GROUP)
    return jnp.transpose(t, (1, 2, 3, 0, 4)).reshape(bsz * nt * S5_TILE, S5_WIDTH)


def _gla_kernel(q_ref, k_ref, v_ref, gk_ref, up_ref, gb_ref, s0_ref, o_ref, sf_ref, st_ref):
    d = pl.program_id(0)
    c = pl.program_id(2)
    n = pl.num_programs(2)

    @pl.when(c == 0)
    def _():
        st_ref[...] = s0_ref[...]

    cs = GLA_CHUNK
    sign = 1 - 2 * d
    row = _iota((cs, cs), 0)
    col = _iota((cs, cs), 1)
    causal = (row - col) * sign >= 0
    causal_b = causal.astype(BF16)
    x = _dot(gk_ref[...].astype(BF16), up_ref[...]) + gb_ref[...]
    log_a = (jnp.minimum(x, 0.0) - jnp.log(1.0 + jnp.exp(-jnp.abs(x)))) * (1.0 / GLA_TAU)
    la_hi, la_lo = _split_hi_lo(log_a)
    bcum = _dot(causal_b, la_hi) + _dot(causal_b, la_lo)
    rid = _iota((cs, 1), 0)
    b_mid = jnp.sum(jnp.where(rid == cs // 2 - d, bcum, 0.0), axis=0, keepdims=True)
    b_end = jnp.sum(jnp.where(rid == (cs - 1) * (1 - d), bcum, 0.0), axis=0, keepdims=True)
    q = q_ref[...] * (GLA_DK ** -0.5)
    k = k_ref[...]
    v = v_ref[...]
    for h in range(GLA_HEADS):
        ks = slice(GLA_DK * h, GLA_DK * (h + 1))
        vs = slice(GLA_DV * h, GLA_DV * (h + 1))
        bh = bcum[:, ks]
        qh = q[:, ks]
        kh = k[:, ks]
        vh = v[:, vs].astype(BF16)
        s_t = st_ref[h]
        q_in = (qh * jnp.exp(bh - b_mid[:, ks])).astype(BF16)
        k_in = (kh * jnp.exp(b_mid[:, ks] - bh)).astype(BF16)
        scores = jnp.where(causal, _dot_nt(q_in, k_in), 0.0).astype(BF16)
        q_st = (qh * jnp.exp(bh)).astype(BF16)
        o_ref[:, vs] = _dot(scores, vh) + _dot_nt(q_st, s_t.astype(BF16))
        k_out = (kh * jnp.exp(b_end[:, ks] - bh)).astype(BF16)
        st_ref[h] = s_t * jnp.exp(b_end[:, ks]) + _dot_tn(vh, k_out)

    @pl.when(c == n - 1)
    def _():
        sf_ref[...] = st_ref[...]


def _gla(qk, v, gk, gk_up, gk_b, s0):
    bsz, cs, w = v.shape
    n = w // GLA_V

    def chunk(d, c):
        return c + d * (n - 1 - 2 * c)

    state_spec = pl.BlockSpec((None, None, GLA_HEADS, GLA_DV, GLA_DK), lambda d, b, c: (d, b, 0, 0, 0))
    return pl.pallas_call(
        _gla_kernel,
        out_shape=(jax.ShapeDtypeStruct((2, bsz, cs, w), F32), jax.ShapeDtypeStruct(s0.shape, F32)),
        grid=(2, bsz, n),
        in_specs=[
            pl.BlockSpec((None, cs, GLA_QK), lambda d, b, c: (b, 0, 2 * chunk(d, c))),
            pl.BlockSpec((None, cs, GLA_QK), lambda d, b, c: (b, 0, 2 * chunk(d, c) + 1)),
            pl.BlockSpec((None, cs, GLA_V), lambda d, b, c: (b, 0, chunk(d, c))),
            pl.BlockSpec((None, cs, LANES), lambda d, b, c: (b, 0, chunk(d, c))),
            pl.BlockSpec((None, LANES, GLA_QK), lambda d, b, c: (d, 0, 0)),
            pl.BlockSpec((None, 1, GLA_QK), lambda d, b, c: (d, 0, 0)),
            state_spec,
        ],
        out_specs=(pl.BlockSpec((None, None, cs, GLA_V), lambda d, b, c: (d, b, 0, chunk(d, c))), state_spec),
        scratch_shapes=[pltpu.VMEM((GLA_HEADS, GLA_DV, GLA_DK), F32)],
        compiler_params=_cp("parallel", "parallel", "arbitrary"),
        name="gla",
    )(qk, qk, v, gk, gk_up, gk_b, s0)


def _to_chunk_major(t, bsz):
    n = t.shape[0] // bsz
    d = t.shape[1]
    t = t.reshape(bsz, n // GLA_CHUNK, GLA_CHUNK, d)
    return jnp.transpose(t, (0, 2, 1, 3)).reshape(bsz, GLA_CHUNK, (n // GLA_CHUNK) * d)


def _from_chunk_major(t, d):
    lead = t.shape[:-2]
    n = t.shape[-1] // d
    t = t.reshape(lead + (GLA_CHUNK, n, d))
    return jnp.swapaxes(t, -3, -2).reshape(lead + (n * GLA_CHUNK, d))


def _rwkv_kernel(z0_ref, z1_ref, halo0_ref, halo1_ref, mu_ref, w0_ref, wup_ref, a0_ref, aup_ref, kk_ref, ka_ref,
                 rk_ref, y_ref, st_ref, kt_s, bt_s, kq_s, rt_s, v_s, w_s, u_s, ya_s, ab_s, pin_s,
                 *, reverse, nc_blocks):
    step = pl.program_id(0)
    blk = RW_BLOCK
    ch = RW_CHUNK
    n_ch = blk // ch
    n_b = 2
    n_units = n_b * RW_PAIRS

    @pl.when(step == 0)
    def _():
        st_ref[...] = jnp.zeros_like(st_ref)

    rowi = _iota((blk, 1), 0)
    low = _iota((1, LANES), 1) < RW_HEAD
    ones64 = _head_ones(RW_HEAD)
    pos = rowi % ch
    seq_start = (step == 0) | (step == nc_blocks)

    for b, (z_ref, halo_ref) in enumerate(((z0_ref, halo0_ref), (z1_ref, halo1_ref))):
        z = z_ref[...]
        if reverse:
            prev = pltpu.roll(z, blk - 1, 0)
            edge = halo_ref[0:1, :]
            at_edge = rowi == blk - 1
        else:
            prev = pltpu.roll(z, 1, 0)
            edge = halo_ref[7:8, :]
            at_edge = rowi == 0
        prev = jnp.where(at_edge, jnp.where(seq_start, 0.0, edge), prev)
        zs = z + (prev - z) * mu_ref[...]
        r = zs[:, 0:RW_WIDTH]
        k = zs[:, RW_WIDTH:2 * RW_WIDTH]
        v = zs[:, 2 * RW_WIDTH:3 * RW_WIDTH]
        lora = zs[:, 3 * RW_WIDTH:3 * RW_WIDTH + LANES]
        lora_w = jnp.where(low, jnp.tanh(lora), 0.0).astype(BF16)
        lora_a = jnp.where(low, 0.0, lora).astype(BF16)
        logw = -RW_DECAY_SCALE * _sigmoid(w0_ref[...] + _dot(lora_w, wup_ref[...]))
        a = _sigmoid(a0_ref[...] + _dot(lora_a, aup_ref[...]))
        kk = k * kk_ref[...]
        kp = k * (1.0 + (a - 1.0) * ka_ref[...])
        rkb = r * kp * rk_ref[...]

        cl = logw
        sh = 1
        while sh < ch:
            if reverse:
                cl = cl + jnp.where(pos < ch - sh, pltpu.roll(cl, blk - sh, 0), 0.0)
            else:
                cl = cl + jnp.where(pos >= sh, pltpu.roll(cl, sh, 0), 0.0)
            sh *= 2
        p_in = jnp.exp(cl)
        pin_s[b] = p_in
        p_ex = jnp.exp(cl - logw)
        p_inv = jnp.exp(-cl)

        for p in range(RW_PAIRS):
            ls = slice(LANES * p, LANES * (p + 1))
            q = b * RW_PAIRS + p
            kkp = kk[:, ls]
            ssq = _head_sum(kkp * kkp, ones64, False)
            kkn = kkp * (1.0 / jnp.maximum(jnp.sqrt(ssq), 1e-12))
            kt_s[q] = (kkn * p_ex[:, ls]).astype(BF16)
            bt_s[q] = (kkn * a[:, ls] * p_inv[:, ls]).astype(BF16)
            kq_s[q] = (kp[:, ls] * p_inv[:, ls]).astype(BF16)
            rt_s[q] = (r[:, ls] * p_in[:, ls]).astype(BF16)
            v_s[q] = v[:, ls].astype(BF16)
            ya_s[q] = _head_sum(rkb[:, ls], ones64, False) * v[:, ls]

    rr = _iota((blk, blk), 0)
    cc = _iota((blk, blk), 1)
    same = rr // ch == cc // ch
    before = (cc > rr) if reverse else (cc < rr)
    strict = same & before
    incl = same & (before | (rr == cc))
    fold0 = (_iota((blk, LANES), 0) % ch == _iota((blk, LANES), 1)).astype(BF16)
    fold1 = (_iota((blk, LANES), 0) % ch + ch == _iota((blk, LANES), 1)).astype(BF16)
    lane_lo = _iota((1, LANES), 1) < RW_HEAD

    def pair_body(p, carry):
        units = [b * RW_PAIRS + p for b in range(n_b)]
        heads = [(u, hh) for u in range(n_b) for hh in range(2)]
        kt = [kt_s[q] for q in units]
        rt = [rt_s[q] for q in units]
        vv = [v_s[q] for q in units]
        ya = [ya_s[q] for q in units]
        rhs = [jnp.concatenate([bt_s[q], kq_s[q]], axis=0) for q in units]
        g = []
        for u, hh in heads:
            mine = lane_lo if hh == 0 else jnp.logical_not(lane_lo)
            zero = jnp.zeros_like(kt[u])
            lhs = jnp.concatenate([jnp.where(mine, kt[u], zero), jnp.where(mine, rt[u], zero)], axis=0)
            g.append(_dot_nt(lhs, rhs[u]))
        n1 = [jnp.where(strict, -gh[:blk, :blk], 0.0).astype(BF16) for gh in g]
        a_kq = [jnp.where(strict, gh[:blk, blk:], 0.0).astype(BF16) for gh in g]
        a_rb = [jnp.where(incl, gh[blk:, :blk], 0.0).astype(BF16) for gh in g]
        a_rq = [jnp.where(incl, gh[blk:, blk:], 0.0).astype(BF16) for gh in g]
        n2 = [_dot(n, n).astype(BF16) for n in n1]
        akv = [_dot(a, vv[u]) for a, (u, _) in zip(a_kq, heads)]
        n4 = [_dot(n, n).astype(BF16) for n in n2]
        y_in = [_dot(a, vv[u]) for a, (u, _) in zip(a_rq, heads)]
        n8 = [_dot(n, n).astype(BF16) for n in n4]
        fold = [_dot(a, fold0 if hh == 0 else fold1) for a, (_, hh) in zip(a_rb, heads)]
        rhs_t = [jnp.concatenate([kt[u].astype(F32), av], axis=1) for av, (u, _) in zip(akv, heads)]
        for nk in (n8, n4, n2, n1):
            rhs_t = [x + _dot(n, x.astype(BF16)) for n, x in zip(nk, rhs_t)]
        for u, q in enumerate(units):
            h0, h1 = 2 * u, 2 * u + 1
            w_s[q] = jnp.where(lane_lo, rhs_t[h0][:, :LANES], rhs_t[h1][:, :LANES]).astype(BF16)
            u_s[q] = jnp.where(lane_lo, rhs_t[h0][:, LANES:], rhs_t[h1][:, LANES:])
            ya_s[q] = ya[u] + jnp.where(lane_lo, y_in[h0], y_in[h1])
            ab_s[q] = (fold[h0] + fold[h1]).astype(BF16)
        return carry

    lax.fori_loop(0, RW_PAIRS, pair_body, 0)

    blockdiag = (_iota((LANES, LANES), 0) // RW_HEAD) == (_iota((LANES, LANES), 1) // RW_HEAD)
    end_row = 0 if reverse else ch - 1

    def chunk_body(i, carry):
        c = (n_ch - 1 - i) if reverse else i
        rows = pl.ds(pl.multiple_of(c * ch, ch), ch)
        s_old = [st_ref[q] for q in range(n_units)]
        m1 = [_dot_nt(jnp.concatenate([w_s[q, rows, :], rt_s[q, rows, :]], axis=0), s_old[q].astype(BF16))
              for q in range(n_units)]
        zc = [-(m1[q][:ch] + u_s[q, rows, :]) for q in range(n_units)]
        upd = []
        for q in range(n_units):
            zv = jnp.concatenate([zc[q].astype(BF16), v_s[q, rows, :]], axis=0)
            bk = jnp.concatenate([bt_s[q, rows, :], kq_s[q, rows, :]], axis=0)
            upd.append(_dot_tn(zv, bk))
        yc = []
        for q in range(n_units):
            z2 = jnp.concatenate([jnp.where(lane_lo, zc[q], 0.0), jnp.where(lane_lo, 0.0, zc[q])], axis=0)
            yc.append(m1[q][ch:] + _dot(ab_s[q, rows, :][:, :2 * ch], z2.astype(BF16)) + ya_s[q, rows, :])
        for q in range(n_units):
            b, p = divmod(q, RW_PAIRS)
            p_end = pin_s[b, rows, LANES * p:LANES * (p + 1)][end_row:end_row + 1]
            st_ref[q] = (s_old[q] + jnp.where(blockdiag, upd[q], 0.0)) * p_end
            y_ref[b, rows, LANES * p:LANES * (p + 1)] = yc[q]
        return carry

    lax.fori_loop(0, n_ch, chunk_body, 0)


def _rwkv_direction(zall, prm, bsz, nc_blocks, nl_blocks, reverse):
    assert bsz == 2
    mu, w0, wup, a0, aup, k_k, k_a, r_k = prm
    rows = zall.shape[0]
    blk = RW_BLOCK
    steps = nc_blocks + nl_blocks
    n_blocks = rows // blk

    def local(s):
        lat = nc_blocks + ((nl_blocks - 1 - (s - nc_blocks)) if reverse else (s - nc_blocks))
        ctx = (nc_blocks - 1 - s) if reverse else s
        return jnp.where(s < nc_blocks, ctx, lat)

    def block(b, s):
        i = local(s)
        return jnp.where(i < nc_blocks, b * nc_blocks + i, bsz * nc_blocks + b * nl_blocks + i - nc_blocks)

    def halo(b, s):
        i = block(b, s)
        if reverse:
            return jnp.minimum((i + 1) * (blk // 8), n_blocks * (blk // 8) - 1)
        return jnp.maximum(i * (blk // 8) - 1, 0)

    vec = lambda w: pl.BlockSpec((1, w), lambda s: (0, 0))
    mat = lambda: pl.BlockSpec((LANES, RW_WIDTH), lambda s: (0, 0))
    n_units = bsz * RW_PAIRS
    unit_bf = pltpu.VMEM((n_units, blk, LANES), BF16)
    unit_f = pltpu.VMEM((n_units, blk, LANES), F32)
    z_spec = lambda b: pl.BlockSpec((blk, RW_Z), lambda s: (block(b, s), 0))
    halo_spec = lambda b: pl.BlockSpec((8, RW_Z), lambda s: (halo(b, s), 0))
    return pl.pallas_call(
        functools.partial(_rwkv_kernel, reverse=reverse, nc_blocks=nc_blocks),
        out_shape=jax.ShapeDtypeStruct((bsz, steps * blk, RW_WIDTH), F32),
        grid=(steps,),
        in_specs=[
            z_spec(0), z_spec(1), halo_spec(0), halo_spec(1),
            vec(RW_Z), vec(RW_WIDTH), mat(), vec(RW_WIDTH), mat(), vec(RW_WIDTH), vec(RW_WIDTH), vec(RW_WIDTH),
        ],
        out_specs=pl.BlockSpec((bsz, blk, RW_WIDTH), lambda s: (0, local(s), 0)),
        scratch_shapes=[
            pltpu.VMEM((n_units, LANES, LANES), F32),
            unit_bf, unit_bf, unit_bf, unit_bf, unit_bf,
            unit_bf, unit_f, unit_f, unit_bf,
            pltpu.VMEM((bsz, blk, RW_WIDTH), F32),
        ],
        compiler_params=_cp("arbitrary"),
        name="rwkv_bwd" if reverse else "rwkv_fwd",
    )(zall, zall, zall, zall, mu, w0, wup, a0, aup, k_k, k_a, r_k)


def _s5_out_kernel(y_ref, u_ref, d_ref, wv_ref, wg_ref, o_ref, z_s):
    @pl.when(pl.program_id(1) == 0)
    def _():
        z_s[...] = _gelu_tanh(y_ref[...] + d_ref[...] * u_ref[...]).astype(BF16)

    zz = z_s[...]
    o_ref[...] = _dot(zz, wv_ref[...]) * _sigmoid(_dot(zz, wg_ref[...]))


def _s5_out(y, u_src, u_blk, d, w_glu, bm, row0):
    n = y.shape[0] - row0 * bm
    bn = 1024
    nj = D_MODEL // bn
    return pl.pallas_call(
        _s5_out_kernel,
        out_shape=jax.ShapeDtypeStruct((n, D_MODEL), F32),
        grid=(n // bm, nj),
        in_specs=[
            pl.BlockSpec((bm, S5_WIDTH), lambda i, j: (i + row0, 0)),
            pl.BlockSpec((bm, S5_WIDTH), lambda i, j: (i + row0, u_blk)),
            pl.BlockSpec((1, S5_WIDTH), lambda i, j: (0, 0)),
            pl.BlockSpec((S5_WIDTH, bn), lambda i, j: (0, j)),
            pl.BlockSpec((S5_WIDTH, bn), lambda i, j: (0, j + nj)),
        ],
        out_specs=pl.BlockSpec((bm, bn), lambda i, j: (i, j)),
        scratch_shapes=[pltpu.VMEM((bm, S5_WIDTH), BF16)],
        compiler_params=_cp("parallel", "arbitrary"),
        name="s5_out",
    )(y, u_src, d, w_glu, w_glu)


def _rwkv_out_kernel(yf_ref, yb_ref, g_ref, gup_ref, lw_ref, lb_ref, w_ref, o_ref, z_s):
    @pl.when(pl.program_id(1) == 0)
    def _():
        ones64 = _head_ones(RW_HEAD)
        gate = _dot(_sigmoid(g_ref[...]).astype(BF16), gup_ref[...])
        for p in range(RW_PAIRS):
            ls = slice(LANES * p, LANES * (p + 1))
            y = yf_ref[:, ls] + yb_ref[:, ls]
            mean = _head_sum(y, ones64, True) * (1.0 / RW_HEAD)
            yc = y - mean
            var = _head_sum(yc * yc, ones64, True) * (1.0 / RW_HEAD)
            yn = yc * lax.rsqrt(var + RW_GN_EPS) * lw_ref[:, ls] + lb_ref[:, ls]
            z_s[:, ls] = (yn * gate[:, ls]).astype(BF16)

    o_ref[...] = _dot(z_s[...], w_ref[...])


def _rwkv_out(yf, yb, zall, g_up, ln_w, ln_b, w_proj, row0, nc_blocks, nl_blocks):
    bm = RW_BLOCK
    bsz = yf.shape[0]
    n = zall.shape[0] - row0 * bm
    bn = 1024
    rows = lambda w, blk=0: pl.BlockSpec((bm, w), lambda i, j: (i + row0, blk))

    def y_index(i, j):
        i = i + row0
        t = i - bsz * nc_blocks
        return (jnp.where(t < 0, i // nc_blocks, t // nl_blocks),
                jnp.where(t < 0, i % nc_blocks, nc_blocks + t % nl_blocks), 0)

    y_spec = pl.BlockSpec((None, bm, RW_WIDTH), y_index)
    vec = pl.BlockSpec((1, RW_WIDTH), lambda i, j: (0, 0))
    return pl.pallas_call(
        _rwkv_out_kernel,
        out_shape=jax.ShapeDtypeStruct((n, D_MODEL), F32),
        grid=(n // bm, D_MODEL // bn),
        in_specs=[y_spec, y_spec, rows(LANES, RW_Z // LANES),
                  pl.BlockSpec((LANES, RW_WIDTH), lambda i, j: (0, 0)), vec, vec,
                  pl.BlockSpec((RW_WIDTH, bn), lambda i, j: (0, j))],
        out_specs=pl.BlockSpec((bm, bn), lambda i, j: (i, j)),
        scratch_shapes=[pltpu.VMEM((bm, RW_WIDTH), BF16)],
        compiler_params=_cp("parallel", "arbitrary"),
        name="rwkv_out",
    )(yf, yb, zall, g_up, ln_w, ln_b, w_proj)


def _gla_out_kernel(of_ref, ob_ref, og_ref, g_ref, w_ref, o_ref, z_s):
    @pl.when(pl.program_id(1) == 0)
    def _():
        for h in range(GLA_HEADS):
            vs = slice(GLA_DV * h, GLA_DV * (h + 1))
            o = of_ref[:, vs] + ob_ref[:, vs]
            on = o * lax.rsqrt(jnp.mean(o * o, axis=-1, keepdims=True) + NORM_EPS) * g_ref[:, vs]
            z_s[:, vs] = (on * _silu(og_ref[:, vs])).astype(BF16)

    o_ref[...] = _dot(z_s[...], w_ref[...])


def _gla_out(of, ob, og_src, og_blk, norm_g, w_proj, bm, row0):
    n = of.shape[0] - row0 * bm
    bn = 1024
    rows = lambda blk=0: pl.BlockSpec((bm, GLA_V), lambda i, j: (i + row0, blk))
    return pl.pallas_call(
        _gla_out_kernel,
        out_shape=jax.ShapeDtypeStruct((n, D_MODEL), F32),
        grid=(n // bm, D_MODEL // bn),
        in_specs=[rows(), rows(), rows(og_blk), pl.BlockSpec((1, GLA_V), lambda i, j: (0, 0)),
                  pl.BlockSpec((GLA_V, bn), lambda i, j: (0, j))],
        out_specs=pl.BlockSpec((bm, bn), lambda i, j: (i, j)),
        scratch_shapes=[pltpu.VMEM((bm, GLA_V), BF16)],
        compiler_params=_cp("parallel", "arbitrary"),
        name="gla_out",
    )(of, ob, og_src, norm_g, w_proj)


def _merge_kernel(ga_ref, gb_ref, gc_ref, pa_ref, pb_ref, pc_ref, w_ref, x_ref, gate_ref, o_ref, z_s):
    @pl.when(pl.program_id(1) == 0)
    def _():
        m = (_sigmoid(ga_ref[...]) * pa_ref[...] + _sigmoid(gb_ref[...]) * pb_ref[...]
             + _sigmoid(gc_ref[...]) * pc_ref[...])
        z_s[...] = m.astype(BF16)

    o_ref[...] = x_ref[...] + gate_ref[...] * _dot(z_s[...], w_ref[...])


def _merge(gates_src, gate_blk0, pa, pb, pc, w_out, x, mod, layer, bm, row0, nc, seq):
    n = pa.shape[0]
    bn = 1024
    d = D_MODEL
    nb = d // bn
    gate = lambda t: pl.BlockSpec((bm, d), lambda i, j: (i + row0, gate_blk0 + t))
    branch = pl.BlockSpec((bm, d), lambda i, j: (i, 0))
    mod_index = _mod_spec(layer, 2, bm, nc, seq).index_map
    shifted = pl.BlockSpec((None, None, None, 1, bn), lambda i, j: mod_index(i + row0)[:4] + (j,))
    return pl.pallas_call(
        _merge_kernel,
        out_shape=jax.ShapeDtypeStruct((n, d), F32),
        grid=(n // bm, nb),
        in_specs=[gate(0), gate(1), gate(2), branch, branch, branch,
                  pl.BlockSpec((d, bn), lambda i, j: (0, j)),
                  pl.BlockSpec((bm, bn), lambda i, j: (i + row0, j)), shifted],
        out_specs=pl.BlockSpec((bm, bn), lambda i, j: (i, j)),
        scratch_shapes=[pltpu.VMEM((bm, d), BF16)],
        compiler_params=_cp("parallel", "arbitrary"),
        name="merge",
    )(gates_src, gates_src, gates_src, pa, pb, pc, w_out, x, mod)


def _router_kernel(x_ref, g_ref, sh_ref, sc_ref, wr_hi_ref, wr_lo_ref, br_ref, v_ref, comb_ref):
    x = x_ref[...]
    y = x * lax.rsqrt(jnp.mean(x * x, axis=-1, keepdims=True) + NORM_EPS) * g_ref[...]
    t = y * (1.0 + sc_ref[...]) + sh_ref[...]
    v_ref[...] = t.astype(BF16)
    t_hi, t_lo = _split_hi_lo(t)
    logits = (_dot(t_hi, wr_hi_ref[...]) + _dot(t_lo, wr_hi_ref[...]) + _dot(t_hi, wr_lo_ref[...])) + br_ref[...]
    lane = _iota(logits.shape, 1).astype(F32)
    neg = jnp.float32(-jnp.inf)
    big = jnp.float32(LANES)
    l1 = jnp.where(lane < MOE_GROUPS, logits, neg)
    m1 = jnp.max(l1, axis=-1, keepdims=True)
    p_top = 1.0 / jnp.sum(jnp.exp(l1 - m1), axis=-1, keepdims=True)
    grp = jnp.min(jnp.where(l1 == m1, lane, big), axis=-1, keepdims=True)
    lo = MOE_LANE0 + MOE_PER_GROUP * grp
    in_grp = (lane >= lo) & (lane < lo + MOE_PER_GROUP)
    l2 = jnp.where(in_grp, logits, neg)
    v1 = jnp.max(l2, axis=-1, keepdims=True)
    i1 = jnp.min(jnp.where(l2 == v1, lane, big), axis=-1, keepdims=True)
    l3 = jnp.where(lane == i1, neg, l2)
    v2 = jnp.max(l3, axis=-1, keepdims=True)
    i2 = jnp.min(jnp.where(l3 == v2, lane, big), axis=-1, keepdims=True)
    e2 = jnp.exp(v2 - v1)
    w1 = p_top / (1.0 + e2)
    w2 = p_top * e2 / (1.0 + e2)
    comb_ref[...] = jnp.where(lane == i1, w1, 0.0) + jnp.where(lane == i2, w2, 0.0)


def _router(x, g, mod, layer, wr_hi, wr_lo, br, bm, nc, seq):
    n, d = x.shape
    return pl.pallas_call(
        _router_kernel,
        out_shape=(jax.ShapeDtypeStruct((n, d), BF16), jax.ShapeDtypeStruct((n, LANES), F32)),
        grid=(n // bm,),
        in_specs=[
            pl.BlockSpec((bm, d), lambda i: (i, 0)),
            pl.BlockSpec((None, 1, d), lambda i: (layer, 0, 0)),
            _mod_spec(layer, 3, bm, nc, seq), _mod_spec(layer, 4, bm, nc, seq),
            pl.BlockSpec((d, LANES), lambda i: (0, 0)), pl.BlockSpec((d, LANES), lambda i: (0, 0)),
            pl.BlockSpec((1, LANES), lambda i: (0, 0)),
        ],
        out_specs=(pl.BlockSpec((bm, d), lambda i: (i, 0)), pl.BlockSpec((bm, LANES), lambda i: (i, 0))),
        compiler_params=_cp("parallel"),
        name="moe_router",
    )(x, g, mod, mod, wr_hi, wr_lo, br)


def _experts_kernel(v_ref, comb_ref, wg_ref, wu_ref, wd_ref, x_ref, gate_ref, o_ref, acc_s):
    e = pl.program_id(1)

    @pl.when(e == 0)
    def _():
        acc_s[...] = jnp.zeros_like(acc_s)

    t = v_ref[...]
    comb = comb_ref[...]
    w_e = jnp.sum(jnp.where(_iota(comb.shape, 1) == e + MOE_LANE0, comb, 0.0), axis=-1, keepdims=True)
    hid = _silu(_dot(t, wg_ref[...])) * _dot(t, wu_ref[...]) * w_e
    acc_s[...] += _dot(hid.astype(BF16), wd_ref[...])

    @pl.when(e == pl.num_programs(1) - 1)
    def _():
        o_ref[...] = x_ref[...] + gate_ref[...] * acc_s[...]


def _experts(v, comb, w_gate, w_up, w_down, x, mod, layer, bm, nc, seq):
    n, d = v.shape
    ne, _, hdim = w_gate.shape
    return pl.pallas_call(
        _experts_kernel,
        out_shape=jax.ShapeDtypeStruct((n, d), F32),
        grid=(n // bm, ne),
        in_specs=[
            pl.BlockSpec((bm, d), lambda i, e: (i, 0)),
            pl.BlockSpec((bm, LANES), lambda i, e: (i, 0)),
            pl.BlockSpec((None, d, hdim), lambda i, e: (e, 0, 0)),
            pl.BlockSpec((None, d, hdim), lambda i, e: (e, 0, 0)),
            pl.BlockSpec((None, hdim, d), lambda i, e: (e, 0, 0)),
            pl.BlockSpec((bm, d), lambda i, e: (i, 0)),
            _mod_spec(layer, 5, bm, nc, seq),
        ],
        out_specs=pl.BlockSpec((bm, d), lambda i, e: (i, 0)),
        scratch_shapes=[pltpu.VMEM((bm, d), F32)],
        compiler_params=_cp("parallel", "arbitrary"),
        name="moe_experts",
    )(v, comb, w_gate, w_up, w_down, x, mod)


def _final_norm_kernel(x_ref, g_ref, o_ref):
    x = x_ref[...]
    o_ref[...] = x * lax.rsqrt(jnp.mean(x * x, axis=-1, keepdims=True) + NORM_EPS) * g_ref[...]


def _final_norm(x, g, bm, row0):
    n, d = x.shape
    n -= row0 * bm
    return pl.pallas_call(
        _final_norm_kernel,
        out_shape=jax.ShapeDtypeStruct((n, d), F32),
        grid=(n // bm,),
        in_specs=[pl.BlockSpec((bm, d), lambda i: (i + row0, 0)), pl.BlockSpec((1, d), lambda i: (0, 0))],
        out_specs=pl.BlockSpec((bm, d), lambda i: (i, 0)),
        compiler_params=_cp("parallel"),
        name="final_norm",
    )(x, g)


_COL = dict(s5=0, rw=S5_WIDTH, rg=S5_WIDTH + 3 * RW_WIDTH + 128, q=4352, k=4864, v=5376, gk=6400, og=6416, gates=7440)


def _pad_rows(w, rows):
    return jnp.pad(w, ((0, rows - w.shape[0]), (0, 0)))


def _layer(i, last, xa, mod, bsz, seq, ctx_len, p):
    nc = bsz * ctx_len
    rows = xa.shape[0]
    bm = 512
    msel = dict(nc=nc, seq=seq)
    u = _normmod(xa, p['g_norm1'].reshape(-1, 1, D_MODEL), mod, i, (0, 1), bm, **msel)

    w_in = p['w_in'][i]
    col = lambda a, w: w_in[:, a:a + w].astype(BF16)
    bm_in = 1088 if rows % 1088 == 0 else bm
    w_a = jnp.concatenate([col(0, S5_WIDTH), col(_COL['og'], GLA_V), col(_COL['gates'], 3 * D_MODEL)], axis=1)
    w_z = jnp.concatenate([col(_COL['rw'], 3 * RW_WIDTH + 128), col(_COL['rg'], 128)], axis=1)
    w_qk = col(_COL['q'], 2 * GLA_QK)
    w_v = col(_COL['v'], GLA_V)
    w_gk = jnp.pad(col(_COL['gk'], 16), ((0, 0), (0, LANES - 16)))
    pa_in = _mm(u, w_a, bm_in, 1024)
    zall = _mm(u, w_z, bm_in, 1664)
    qk = _mm(u, w_qk, bm_in, 1024)
    vv = _mm(u, w_v, bm_in, 1024)
    gk = _mm(u, w_gk, bm_in, LANES)

    u_s5 = pa_in[:, :S5_WIDTH].astype(BF16)
    uc_t = _s5_to_tiles(u_s5[:nc], bsz)
    ul_t = _s5_to_tiles(u_s5[nc:], bsz)
    ya = None
    for d in range(2):
        ops = _s5_operators(*(p[k][i, d] for k in ('s5_a_re', 's5_a_im', 's5_log_dt', 's5_b_re', 's5_b_im',
                                                  's5_c_re', 's5_c_im')), reverse=bool(d))
        yc_t, yl_t = _s5_direction(uc_t, ul_t, ops, bool(d))
        yd = jnp.concatenate([_s5_from_tiles(yc_t), _s5_from_tiles(yl_t)], axis=0)
        ya = yd if ya is None else ya + yd

    yb = []
    for d in range(2):
        prm = (
            jnp.pad(p['rw_mu'][i, d], (0, RW_Z - p['rw_mu'].shape[-1])).reshape(1, RW_Z),
            p['rw_w0'][i, d].reshape(1, -1),
            _pad_rows(p['rw_w_up'][i, d], LANES).astype(BF16),
            p['rw_a0'][i, d].reshape(1, -1),
            jnp.pad(p['rw_a_up'][i, d], ((RW_HEAD, 0), (0, 0))).astype(BF16),
            p['rw_k_k'][i].reshape(1, -1), p['rw_k_a'][i].reshape(1, -1), p['rw_r_k'][i].reshape(1, -1),
        )
        yb.append(_rwkv_direction(zall, prm, bsz, ctx_len // RW_BLOCK, seq // RW_BLOCK, bool(d)))

    gk_up = jnp.pad(p['gl_gk_up'][i], ((0, 0), (0, LANES - 16), (0, 0))).astype(BF16)
    gk_b = p['gl_gk_b'][i].reshape(2, 1, GLA_QK)
    s0 = jnp.zeros((2, bsz, GLA_HEADS, GLA_DV, GLA_DK), F32)
    oc, s_ctx = _gla(_to_chunk_major(qk[:nc], bsz), _to_chunk_major(vv[:nc], bsz), _to_chunk_major(gk[:nc], bsz),
                     gk_up, gk_b, s0)
    rows_l = seq // GRID_W
    lat = lambda t: t[nc:].reshape(bsz, rows_l, GRID_W * t.shape[1])
    ol, _ = _gla(lat(qk), lat(vv), lat(gk), gk_up, gk_b, s_ctx)
    o_dir = [jnp.concatenate([_from_chunk_major(oc[d], GLA_V).reshape(nc, GLA_V),
                              ol[d].reshape(bsz * seq, GLA_V)], axis=0) for d in range(2)]

    row0 = (nc // bm) if last else 0
    pa = _s5_out(ya, pa_in, 0, p['s5_d'][i].reshape(1, -1), p['s5_w_glu'][i].astype(BF16), bm, row0)
    pb = _rwkv_out(yb[0], yb[1], zall, p['rw_g_up'][i].astype(BF16), p['rw_ln_w'][i].reshape(1, -1),
                   p['rw_ln_b'][i].reshape(1, -1), p['rw_w_proj'][i].astype(BF16), row0 * (bm // RW_BLOCK),
                   ctx_len // RW_BLOCK, seq // RW_BLOCK)
    pc = _gla_out(o_dir[0], o_dir[1], pa_in, 1, p['gl_norm_g'][i].reshape(1, -1), p['gl_w_proj'][i].astype(BF16),
                  bm, row0)
    bm_merge = bm // 2
    xm = _merge(pa_in, 1, pa, pb, pc, p['w_out'][i].astype(BF16), xa, mod, i, bm_merge, row0 * 2, nc, seq)
    if last:
        msel = dict(nc=0, seq=seq)

    wr = jnp.pad(jnp.concatenate([p['moe_wg1'][i], p['moe_wg2'][i]], axis=1), ((0, 0), (0, LANES - 36)))
    wr_hi = wr.astype(BF16)
    wr_lo = (wr - wr_hi.astype(F32)).astype(BF16)
    br = jnp.pad(jnp.concatenate([p['moe_bg1'][i], p['moe_bg2'][i]]), (0, LANES - 36)).reshape(1, LANES)
    vmoe, comb = _router(xm, p['g_norm2'].reshape(-1, 1, D_MODEL), mod, i, wr_hi, wr_lo, br, bm, **msel)
    return _experts(vmoe, comb, p['moe_w_gate'][i].astype(BF16), p['moe_w_up'][i].astype(BF16),
                    p['moe_w_down'][i].astype(BF16), xm, mod, i, bm, **msel)


def kernel(x, c, ctx, c_ctx, w_mod, b_mod, g_norm1, g_norm2, w_in, s5_a_re, s5_a_im, s5_log_dt, s5_b_re, s5_b_im,
           s5_c_re, s5_c_im, s5_d, s5_w_glu, rw_mu, rw_w0, rw_w_up, rw_a0, rw_a_up, rw_k_k, rw_k_a, rw_r_k, rw_g_up,
           rw_ln_w, rw_ln_b, rw_w_proj, gl_gk_up, gl_gk_b, gl_norm_g, gl_w_proj, w_out, moe_wg1, moe_bg1, moe_wg2,
           moe_bg2, moe_w_gate, moe_w_up, moe_w_down, g_final):
    p = dict(g_norm1=g_norm1, g_norm2=g_norm2, w_in=w_in, s5_a_re=s5_a_re, s5_a_im=s5_a_im, s5_log_dt=s5_log_dt,
             s5_b_re=s5_b_re, s5_b_im=s5_b_im, s5_c_re=s5_c_re, s5_c_im=s5_c_im, s5_d=s5_d, s5_w_glu=s5_w_glu,
             rw_mu=rw_mu, rw_w0=rw_w0, rw_w_up=rw_w_up, rw_a0=rw_a0, rw_a_up=rw_a_up, rw_k_k=rw_k_k, rw_k_a=rw_k_a,
             rw_r_k=rw_r_k.reshape(rw_r_k.shape[0], -1), rw_g_up=rw_g_up, rw_ln_w=rw_ln_w, rw_ln_b=rw_ln_b,
             rw_w_proj=rw_w_proj, gl_gk_up=gl_gk_up, gl_gk_b=gl_gk_b, gl_norm_g=gl_norm_g, gl_w_proj=gl_w_proj,
             w_out=w_out, moe_wg1=moe_wg1, moe_bg1=moe_bg1, moe_wg2=moe_wg2, moe_bg2=moe_bg2, moe_w_gate=moe_w_gate,
             moe_w_up=moe_w_up, moe_w_down=moe_w_down)
    bsz, seq, d = x.shape
    ctx_len = ctx.shape[1]
    depth = w_mod.shape[0]
    cc = jnp.concatenate([c, c_ctx[None], jnp.zeros((8 - bsz - 1, d), F32)], axis=0)
    mod = _adaln(cc, w_mod, b_mod).reshape(depth, 8, 6, 1, d)
    xa = jnp.concatenate([ctx.reshape(bsz * ctx_len, d), x.reshape(bsz * seq, d)], axis=0)
    for i in range(depth):
        xa = _layer(i, i == depth - 1, xa, mod, bsz, seq, ctx_len, p)
    out = _final_norm(xa, g_final.reshape(1, d), 512, 0)
    return out.reshape(bsz, seq, d)
```

```python
import functools
import math

import jax
import jax.numpy as jnp
from jax import lax
from jax.experimental import pallas as pl
from jax.experimental.pallas import tpu as pltpu

F32 = jnp.float32
BF16 = jnp.bfloat16

D_MODEL = 2048
GRID_W = 64
NORM_EPS = 1e-6

S5_WIDTH = 1024
S5_GROUP = 16
S5_GROUPS = 64
S5_STATE = 64
S5_MAX_RE = -1e-4
S5_TILE = 16
S5_PAIRS = S5_GROUPS // 2

RW_WIDTH = 1024
RW_HEAD = 64
RW_DECAY_SCALE = 0.606531
RW_GN_EPS = 64e-5
RW_BLOCK = 256
RW_CHUNK = 16
RW_PAIRS = RW_WIDTH // 128
RW_Z = 3 * RW_WIDTH + 128

GLA_HEADS = 4
GLA_DK = 128
GLA_DV = 256
GLA_QK = 512
GLA_V = 1024
GLA_TAU = 16.0
GLA_CHUNK = 64

MOE_GROUPS = 4
MOE_PER_GROUP = 8
MOE_EXPERTS = 32
MOE_HIDDEN = 256
MOE_LANE0 = MOE_GROUPS

LANES = 128
VMEM_LIMIT = 56 * 1024 * 1024


def _cp(*sem):
    return pltpu.CompilerParams(dimension_semantics=sem, vmem_limit_bytes=VMEM_LIMIT)


def _dot(a, b):
    return jnp.dot(a, b, preferred_element_type=F32)


def _dot_nt(a, b):
    return lax.dot_general(a, b, (((1,), (1,)), ((), ())), preferred_element_type=F32)


def _dot_tn(a, b):
    return lax.dot_general(a, b, (((0,), (0,)), ((), ())), preferred_element_type=F32)


def _sigmoid(x):
    return 1.0 / (1.0 + jnp.exp(-x))


def _silu(x):
    return x * _sigmoid(x)


def _gelu_tanh(x):
    return 0.5 * x * (1.0 + jnp.tanh(math.sqrt(2.0 / math.pi) * (x + 0.044715 * (x * x * x))))


def _split_hi_lo(x):
    hi = x.astype(BF16)
    lo = (x - hi.astype(F32)).astype(BF16)
    return hi, lo


def _iota(shape, dim):
    return lax.broadcasted_iota(jnp.int32, shape, dim)


def _head_ones(width):
    return (_iota((LANES, LANES), 0) // width == _iota((LANES, LANES), 1) // width).astype(BF16)


def _head_sum(x, ones, exact):
    if exact:
        hi, lo = _split_hi_lo(x)
        return _dot(hi, ones) + _dot(lo, ones)
    return _dot(x.astype(BF16), ones)


def _adaln_kernel(c_ref, w_ref, b_ref, o_ref):
    c = c_ref[...]
    o_ref[...] = _dot(_silu(c).astype(BF16), w_ref[...].astype(BF16)) + b_ref[...]


def _adaln(cc, w_mod, b_mod):
    depth, d, n = w_mod.shape
    bn = 1536
    return pl.pallas_call(
        _adaln_kernel,
        out_shape=jax.ShapeDtypeStruct((depth, 8, n), F32),
        grid=(depth, n // bn),
        in_specs=[
            pl.BlockSpec((8, d), lambda l, j: (0, 0)),
            pl.BlockSpec((None, d, bn), lambda l, j: (l, 0, j)),
            pl.BlockSpec((None, 1, bn), lambda l, j: (l, 0, j)),
        ],
        out_specs=pl.BlockSpec((None, 8, bn), lambda l, j: (l, 0, j)),
        compiler_params=_cp("parallel", "parallel"),
        name="adaln",
    )(cc, w_mod, b_mod.reshape(depth, 1, n))


def _mod_spec(layer, part, bm, nc, seq):
    def index(i, *_):
        r0 = i * bm
        return (layer, jnp.where(r0 < nc, 2, (r0 - nc) // seq), part, 0, 0)
    return pl.BlockSpec((None, None, None, 1, D_MODEL), index)


def _normmod_kernel(x_ref, g_ref, sh_ref, sc_ref, o_ref):
    x = x_ref[...]
    y = x * lax.rsqrt(jnp.mean(x * x, axis=-1, keepdims=True) + NORM_EPS) * g_ref[...]
    o_ref[...] = (y * (1.0 + sc_ref[...]) + sh_ref[...]).astype(o_ref.dtype)


def _normmod(x, g, mod, layer, parts, bm, nc, seq):
    n, d = x.shape
    return pl.pallas_call(
        _normmod_kernel,
        out_shape=jax.ShapeDtypeStruct((n, d), BF16),
        grid=(n // bm,),
        in_specs=[
            pl.BlockSpec((bm, d), lambda i: (i, 0)),
            pl.BlockSpec((None, 1, d), lambda i: (layer, 0, 0)),
            _mod_spec(layer, parts[0], bm, nc, seq),
            _mod_spec(layer, parts[1], bm, nc, seq),
        ],
        out_specs=pl.BlockSpec((bm, d), lambda i: (i, 0)),
        compiler_params=_cp("parallel"),
        name="normmod",
    )(x, g, mod, mod)


def _mm_kernel(x_ref, w_ref, o_ref):
    o_ref[...] = _dot(x_ref[...], w_ref[...]).astype(o_ref.dtype)


def _mm(x, w, bm, bn, out_dtype=F32):
    m, k = x.shape
    n = w.shape[1]
    return pl.pallas_call(
        _mm_kernel,
        out_shape=jax.ShapeDtypeStruct((m, n), out_dtype),
        grid=(m // bm, n // bn),
        in_specs=[pl.BlockSpec((bm, k), lambda i, j: (i, 0)), pl.BlockSpec((k, bn), lambda i, j: (0, j))],
        out_specs=pl.BlockSpec((bm, bn), lambda i, j: (i, j)),
        compiler_params=_cp("parallel", "parallel"),
        name="mm",
    )(x, w)


def _seq_blocks(bsz, nc_blocks, nl_blocks, reverse):
    def local(s):
        lat = nc_blocks + ((nl_blocks - 1 - (s - nc_blocks)) if reverse else (s - nc_blocks))
        ctx = (nc_blocks - 1 - s) if reverse else s
        return jnp.where(s < nc_blocks, ctx, lat)

    def block(b, s):
        i = local(s)
        return jnp.where(i < nc_blocks, b * nc_blocks + i, bsz * nc_blocks + b * nl_blocks + i - nc_blocks)

    return local, block


S5_PACK = 8
S5_PACKS = S5_GROUPS // S5_PACK
S5_PLANE = S5_PACK * S5_STATE
S5_BLOCK = 256
S5_ROWS = 24


def _s5_tables(a_re, a_im, log_dt, b_re, b_im, c_re, c_im):
    lam = lax.complex(jnp.minimum(a_re, S5_MAX_RE), a_im)
    ldt = lam * jnp.exp(log_dt)[:, None]
    lam_bar = jnp.exp(ldt)
    b_bar = ((lam_bar - 1.0) / lam)[..., None] * lax.complex(b_re, b_im)
    c_mat = lax.complex(c_re, c_im)
    eye = jnp.eye(S5_PACK, dtype=F32)

    def block_diag(t):
        k, g, a, b = t.shape
        return (t[:, :, :, None, :] * eye[None, :, None, :, None]).reshape(k, g * a, g * b)

    b_t = jnp.transpose(b_bar, (0, 2, 1)).reshape(S5_PACKS, S5_PACK, S5_GROUP, S5_STATE)
    bblk = jnp.concatenate([block_diag(b_t.real), block_diag(b_t.imag)], axis=2)
    c_t = jnp.transpose(c_mat, (0, 2, 1)).reshape(S5_PACKS, S5_PACK, S5_STATE, S5_GROUP)
    cblk = jnp.concatenate([block_diag(c_t.real), block_diag(-c_t.imag)], axis=1)
    expo = jnp.concatenate([jnp.arange(1, S5_TILE + 1, dtype=F32), jnp.asarray([32.0, 64.0, 128.0], F32),
                            jnp.zeros((S5_ROWS - S5_TILE - 3,), F32)])
    pw = jnp.exp(ldt[None] * expo[:, None, None]).reshape(S5_ROWS, S5_PACKS, S5_PLANE)
    pw = jnp.transpose(pw, (1, 0, 2))
    return bblk.astype(BF16), cblk.astype(BF16), pw.real, pw.imag


def _s5_scan_kernel(u_ref, bblk_ref, cblk_ref, pre_ref, pim_ref, y_ref, car_ref, up_s, h_s, hb_s, yp_s, *, reverse):
    t = S5_TILE
    n_t = S5_BLOCK // t
    pn = S5_PLANE

    @pl.when(pl.program_id(1) == 0)
    def _():
        car_ref[...] = jnp.zeros_like(car_ref)

    ra = _iota((S5_BLOCK, S5_BLOCK), 0)
    cb = _iota((S5_BLOCK, S5_BLOCK), 1)
    perm = ((ra // t == cb % t) & (ra % t == cb // t)).astype(BF16)
    up = _dot(perm, u_ref[...].astype(BF16)).astype(BF16)
    for pk in range(S5_PACKS):
        up_s[pk] = up[:, LANES * pk:LANES * (pk + 1)]
    order = list(range(t - 1, -1, -1)) if reverse else list(range(t))
    rowj = _iota((n_t, 1), 0)

    def pack_body(pk, carry):
        bu = _dot(up_s[pk], bblk_ref[pk])
        pre = pre_ref[pk]
        pim = pim_ref[pk]
        l_re, l_im = pre[0:1], pim[0:1]
        h_re = h_im = None
        for n, s in enumerate(order):
            rows = slice(t * s, t * (s + 1))
            b_re, b_im = bu[rows, :pn], bu[rows, pn:]
            if n == 0:
                h_re, h_im = b_re, b_im
            else:
                h_re, h_im = l_re * h_re - l_im * h_im + b_re, l_re * h_im + l_im * h_re + b_im
            h_s[rows, :pn] = h_re
            h_s[rows, pn:] = h_im
        c_re, c_im = car_ref[pk, 0:1, :pn], car_ref[pk, 0:1, pn:]
        first = rowj == (n_t - 1 if reverse else 0)
        g_re, g_im = pre[t - 1:t], pim[t - 1:t]
        e_re = h_re + jnp.where(first, g_re * c_re - g_im * c_im, 0.0)
        e_im = h_im + jnp.where(first, g_re * c_im + g_im * c_re, 0.0)
        step = 1
        for row in (t - 1, t, t + 1, t + 2):
            if reverse:
                s_re, s_im, ok = pltpu.roll(e_re, n_t - step, 0), pltpu.roll(e_im, n_t - step, 0), rowj < n_t - step
            else:
                s_re, s_im, ok = pltpu.roll(e_re, step, 0), pltpu.roll(e_im, step, 0), rowj >= step
            a_re, a_im = pre[row:row + 1], pim[row:row + 1]
            e_re = e_re + jnp.where(ok, a_re * s_re - a_im * s_im, 0.0)
            e_im = e_im + jnp.where(ok, a_re * s_im + a_im * s_re, 0.0)
            step *= 2
        last = 0 if reverse else n_t - 1
        car_ref[pk, 0:1, :pn] = e_re[last:last + 1]
        car_ref[pk, 0:1, pn:] = e_im[last:last + 1]
        if reverse:
            in_re = jnp.where(first, c_re, pltpu.roll(e_re, n_t - 1, 0))
            in_im = jnp.where(first, c_im, pltpu.roll(e_im, n_t - 1, 0))
        else:
            in_re = jnp.where(first, c_re, pltpu.roll(e_re, 1, 0))
            in_im = jnp.where(first, c_im, pltpu.roll(e_im, 1, 0))
        for n, s in enumerate(order):
            rows = slice(t * s, t * (s + 1))
            a_re, a_im = pre[n:n + 1], pim[n:n + 1]
            hb_s[rows, :pn] = (h_s[rows, :pn] + a_re * in_re - a_im * in_im).astype(BF16)
            hb_s[rows, pn:] = (h_s[rows, pn:] + a_re * in_im + a_im * in_re).astype(BF16)
        yp_s[pk] = _dot(hb_s[...], cblk_ref[pk]).astype(BF16)
        return carry

    lax.fori_loop(0, S5_PACKS, pack_body, 0)
    for pk in range(S5_PACKS):
        y_ref[:, LANES * pk:LANES * (pk + 1)] = _dot(perm, yp_s[pk])


def _s5_scan(u_src, tables, bsz, nc_blocks, nl_blocks, reverse):
    bblk, cblk, pre, pim = tables
    rows = u_src.shape[0]
    blk = S5_BLOCK
    _, block = _seq_blocks(bsz, nc_blocks, nl_blocks, reverse)
    whole = lambda a: pl.BlockSpec(a.shape, lambda b, s: (0,) * a.ndim)
    return pl.pallas_call(
        functools.partial(_s5_scan_kernel, reverse=reverse),
        out_shape=jax.ShapeDtypeStruct((rows, S5_WIDTH), F32),
        grid=(bsz, nc_blocks + nl_blocks),
        in_specs=[pl.BlockSpec((blk, S5_WIDTH), lambda b, s: (block(b, s), 0)),
                  whole(bblk), whole(cblk), whole(pre), whole(pim)],
        out_specs=pl.BlockSpec((blk, S5_WIDTH), lambda b, s: (block(b, s), 0)),
        scratch_shapes=[
            pltpu.VMEM((S5_PACKS, 8, 2 * S5_PLANE), F32),
            pltpu.VMEM((S5_PACKS, blk, LANES), BF16),
            pltpu.VMEM((blk, 2 * S5_PLANE), F32),
            pltpu.VMEM((blk, 2 * S5_PLANE), BF16),
            pltpu.VMEM((S5_PACKS, blk, LANES), BF16),
        ],
        compiler_params=_cp("parallel", "arbitrary"),
        name="s5_bwd" if reverse else "s5_fwd",
    )(u_src, bblk, cblk, pre, pim)


def _gla_kernel(q_ref, k_ref, v_ref, gk_ref, up_ref, gb_ref, s0_ref, o_ref, sf_ref, st_ref):
    d = pl.program_id(0)
    c = pl.program_id(2)
    n = pl.num_programs(2)

    @pl.when(c == 0)
    def _():
        st_ref[...] = s0_ref[...]

    cs = GLA_CHUNK
    sign = 1 - 2 * d
    row = _iota((cs, cs), 0)
    col = _iota((cs, cs), 1)
    causal = (row - col) * sign >= 0
    causal_b = causal.astype(BF16)
    x = _dot(gk_ref[...].astype(BF16), up_ref[...]) + gb_ref[...]
    log_a = (jnp.minimum(x, 0.0) - jnp.log(1.0 + jnp.exp(-jnp.abs(x)))) * (1.0 / GLA_TAU)
    la_hi, la_lo = _split_hi_lo(log_a)
    bcum = _dot(causal_b, la_hi) + _dot(causal_b, la_lo)
    rid = _iota((cs, 1), 0)
    b_mid = jnp.sum(jnp.where(rid == cs // 2 - d, bcum, 0.0), axis=0, keepdims=True)
    b_end = jnp.sum(jnp.where(rid == (cs - 1) * (1 - d), bcum, 0.0), axis=0, keepdims=True)
    q = q_ref[...] * (GLA_DK ** -0.5)
    k = k_ref[...]
    v = v_ref[...]
    for h in range(GLA_HEADS):
        ks = slice(GLA_DK * h, GLA_DK * (h + 1))
        vs = slice(GLA_DV * h, GLA_DV * (h + 1))
        bh = bcum[:, ks]
        qh = q[:, ks]
        kh = k[:, ks]
        vh = v[:, vs].astype(BF16)
        s_t = st_ref[h]
        q_in = (qh * jnp.exp(bh - b_mid[:, ks])).astype(BF16)
        k_in = (kh * jnp.exp(b_mid[:, ks] - bh)).astype(BF16)
        scores = jnp.where(causal, _dot_nt(q_in, k_in), 0.0).astype(BF16)
        q_st = (qh * jnp.exp(bh)).astype(BF16)
        o_ref[:, vs] = _dot(scores, vh) + _dot_nt(q_st, s_t.astype(BF16))
        k_out = (kh * jnp.exp(b_end[:, ks] - bh)).astype(BF16)
        st_ref[h] = s_t * jnp.exp(b_end[:, ks]) + _dot_tn(vh, k_out)

    @pl.when(c == n - 1)
    def _():
        sf_ref[...] = st_ref[...]


def _gla(qk, v, gk, gk_up, gk_b, s0):
    bsz, cs, w = v.shape
    n = w // GLA_V

    def chunk(d, c):
        return c + d * (n - 1 - 2 * c)

    state_spec = pl.BlockSpec((None, None, GLA_HEADS, GLA_DV, GLA_DK), lambda d, b, c: (d, b, 0, 0, 0))
    return pl.pallas_call(
        _gla_kernel,
        out_shape=(jax.ShapeDtypeStruct((2, bsz, cs, w), F32), jax.ShapeDtypeStruct(s0.shape, F32)),
        grid=(2, bsz, n),
        in_specs=[
            pl.BlockSpec((None, cs, GLA_QK), lambda d, b, c: (b, 0, 2 * chunk(d, c))),
            pl.BlockSpec((None, cs, GLA_QK), lambda d, b, c: (b, 0, 2 * chunk(d, c) + 1)),
            pl.BlockSpec((None, cs, GLA_V), lambda d, b, c: (b, 0, chunk(d, c))),
            pl.BlockSpec((None, cs, LANES), lambda d, b, c: (b, 0, chunk(d, c))),
            pl.BlockSpec((None, LANES, GLA_QK), lambda d, b, c: (d, 0, 0)),
            pl.BlockSpec((None, 1, GLA_QK), lambda d, b, c: (d, 0, 0)),
            state_spec,
        ],
        out_specs=(pl.BlockSpec((None, None, cs, GLA_V), lambda d, b, c: (d, b, 0, chunk(d, c))), state_spec),
        scratch_shapes=[pltpu.VMEM((GLA_HEADS, GLA_DV, GLA_DK), F32)],
        compiler_params=_cp("parallel", "parallel", "arbitrary"),
        name="gla",
    )(qk, qk, v, gk, gk_up, gk_b, s0)


def _to_chunk_major(t, bsz):
    n = t.shape[0] // bsz
    d = t.shape[1]
    t = t.reshape(bsz, n // GLA_CHUNK, GLA_CHUNK, d)
    return jnp.transpose(t, (0, 2, 1, 3)).reshape(bsz, GLA_CHUNK, (n // GLA_CHUNK) * d)


def _from_chunk_major(t, d):
    lead = t.shape[:-2]
    n = t.shape[-1] // d
    t = t.reshape(lead + (GLA_CHUNK, n, d))
    return jnp.swapaxes(t, -3, -2).reshape(lead + (n * GLA_CHUNK, d))


def _rwkv_kernel(z0_ref, z1_ref, halo0_ref, halo1_ref, mu_ref, w0_ref, wup_ref, a0_ref, aup_ref, kk_ref, ka_ref,
                 rk_ref, y_ref, st_ref, kt_s, bt_s, kq_s, rt_s, v_s, w_s, u_s, ya_s, ab_s, pin_s,
                 *, reverse, nc_blocks):
    step = pl.program_id(0)
    blk = RW_BLOCK
    ch = RW_CHUNK
    n_ch = blk // ch
    n_b = 2
    n_units = n_b * RW_PAIRS

    @pl.when(step == 0)
    def _():
        st_ref[...] = jnp.zeros_like(st_ref)

    rowi = _iota((blk, 1), 0)
    low = _iota((1, LANES), 1) < RW_HEAD
    ones64 = _head_ones(RW_HEAD)
    pos = rowi % ch
    seq_start = (step == 0) | (step == nc_blocks)

    for b, (z_ref, halo_ref) in enumerate(((z0_ref, halo0_ref), (z1_ref, halo1_ref))):
        z = z_ref[...]
        if reverse:
            prev = pltpu.roll(z, blk - 1, 0)
            edge = halo_ref[0:1, :]
            at_edge = rowi == blk - 1
        else:
            prev = pltpu.roll(z, 1, 0)
            edge = halo_ref[7:8, :]
            at_edge = rowi == 0
        prev = jnp.where(at_edge, jnp.where(seq_start, 0.0, edge), prev)
        zs = z + (prev - z) * mu_ref[...]
        r = zs[:, 0:RW_WIDTH]
        k = zs[:, RW_WIDTH:2 * RW_WIDTH]
        v = zs[:, 2 * RW_WIDTH:3 * RW_WIDTH]
        lora = zs[:, 3 * RW_WIDTH:3 * RW_WIDTH + LANES]
        lora_w = jnp.where(low, jnp.tanh(lora), 0.0).astype(BF16)
        lora_a = jnp.where(low, 0.0, lora).astype(BF16)
        logw = -RW_DECAY_SCALE * _sigmoid(w0_ref[...] + _dot(lora_w, wup_ref[...]))
        a = _sigmoid(a0_ref[...] + _dot(lora_a, aup_ref[...]))
        kk = k * kk_ref[...]
        kp = k * (1.0 + (a - 1.0) * ka_ref[...])
        rkb = r * kp * rk_ref[...]

        cl = logw
        sh = 1
        while sh < ch:
            if reverse:
                cl = cl + jnp.where(pos < ch - sh, pltpu.roll(cl, blk - sh, 0), 0.0)
            else:
                cl = cl + jnp.where(pos >= sh, pltpu.roll(cl, sh, 0), 0.0)
            sh *= 2
        p_in = jnp.exp(cl)
        pin_s[b] = p_in
        p_ex = jnp.exp(cl - logw)
        p_inv = jnp.exp(-cl)

        for p in range(RW_PAIRS):
            ls = slice(LANES * p, LANES * (p + 1))
            q = b * RW_PAIRS + p
            kkp = kk[:, ls]
            ssq = _head_sum(kkp * kkp, ones64, False)
            kkn = kkp * (1.0 / jnp.maximum(jnp.sqrt(ssq), 1e-12))
            kt_s[q] = (kkn * p_ex[:, ls]).astype(BF16)
            bt_s[q] = (kkn * a[:, ls] * p_inv[:, ls]).astype(BF16)
            kq_s[q] = (kp[:, ls] * p_inv[:, ls]).astype(BF16)
            rt_s[q] = (r[:, ls] * p_in[:, ls]).astype(BF16)
            v_s[q] = v[:, ls].astype(BF16)
            ya_s[q] = _head_sum(rkb[:, ls], ones64, False) * v[:, ls]

    rr = _iota((blk, blk), 0)
    cc = _iota((blk, blk), 1)
    same = rr // ch == cc // ch
    before = (cc > rr) if reverse else (cc < rr)
    strict = same & before
    incl = same & (before | (rr == cc))
    fold0 = (_iota((blk, LANES), 0) % ch == _iota((blk, LANES), 1)).astype(BF16)
    fold1 = (_iota((blk, LANES), 0) % ch + ch == _iota((blk, LANES), 1)).astype(BF16)
    lane_lo = _iota((1, LANES), 1) < RW_HEAD

    def pair_body(p, carry):
        units = [b * RW_PAIRS + p for b in range(n_b)]
        heads = [(u, hh) for u in range(n_b) for hh in range(2)]
        kt = [kt_s[q] for q in units]
        rt = [rt_s[q] for q in units]
        vv = [v_s[q] for q in units]
        ya = [ya_s[q] for q in units]
        rhs = [jnp.concatenate([bt_s[q], kq_s[q]], axis=0) for q in units]
        g = []
        for u, hh in heads:
            mine = lane_lo if hh == 0 else jnp.logical_not(lane_lo)
            zero = jnp.zeros_like(kt[u])
            lhs = jnp.concatenate([jnp.where(mine, kt[u], zero), jnp.where(mine, rt[u], zero)], axis=0)
            g.append(_dot_nt(lhs, rhs[u]))
        n1 = [jnp.where(strict, -gh[:blk, :blk], 0.0).astype(BF16) for gh in g]
        a_kq = [jnp.where(strict, gh[:blk, blk:], 0.0).astype(BF16) for gh in g]
        a_rb = [jnp.where(incl, gh[blk:, :blk], 0.0).astype(BF16) for gh in g]
        a_rq = [jnp.where(incl, gh[blk:, blk:], 0.0).astype(BF16) for gh in g]
        n2 = [_dot(n, n).astype(BF16) for n in n1]
        akv = [_dot(a, vv[u]) for a, (u, _) in zip(a_kq, heads)]
        n4 = [_dot(n, n).astype(BF16) for n in n2]
        y_in = [_dot(a, vv[u]) for a, (u, _) in zip(a_rq, heads)]
        n8 = [_dot(n, n).astype(BF16) for n in n4]
        fold = [_dot(a, fold0 if hh == 0 else fold1) for a, (_, hh) in zip(a_rb, heads)]
        rhs_t = [jnp.concatenate([kt[u].astype(F32), av], axis=1) for av, (u, _) in zip(akv, heads)]
        for nk in (n8, n4, n2, n1):
            rhs_t = [x + _dot(n, x.astype(BF16)) for n, x in zip(nk, rhs_t)]
        for u, q in enumerate(units):
            h0, h1 = 2 * u, 2 * u + 1
            w_s[q] = jnp.where(lane_lo, rhs_t[h0][:, :LANES], rhs_t[h1][:, :LANES]).astype(BF16)
            u_s[q] = jnp.where(lane_lo, rhs_t[h0][:, LANES:], rhs_t[h1][:, LANES:])
            ya_s[q] = ya[u] + jnp.where(lane_lo, y_in[h0], y_in[h1])
            ab_s[q] = (fold[h0] + fold[h1]).astype(BF16)
        return carry

    lax.fori_loop(0, RW_PAIRS, pair_body, 0)

    blockdiag = (_iota((LANES, LANES), 0) // RW_HEAD) == (_iota((LANES, LANES), 1) // RW_HEAD)
    end_row = 0 if reverse else ch - 1

    def chunk_body(i, carry):
        c = (n_ch - 1 - i) if reverse else i
        rows = pl.ds(pl.multiple_of(c * ch, ch), ch)
        s_old = [st_ref[q] for q in range(n_units)]
        m1 = [_dot_nt(jnp.concatenate([w_s[q, rows, :], rt_s[q, rows, :]], axis=0), s_old[q].astype(BF16))
              for q in range(n_units)]
        zc = [-(m1[q][:ch] + u_s[q, rows, :]) for q in range(n_units)]
        upd = []
        for q in range(n_units):
            zv = jnp.concatenate([zc[q].astype(BF16), v_s[q, rows, :]], axis=0)
            bk = jnp.concatenate([bt_s[q, rows, :], kq_s[q, rows, :]], axis=0)
            upd.append(_dot_tn(zv, bk))
        yc = []
        for q in range(n_units):
            z2 = jnp.concatenate([jnp.where(lane_lo, zc[q], 0.0), jnp.where(lane_lo, 0.0, zc[q])], axis=0)
            yc.append(m1[q][ch:] + _dot(ab_s[q, rows, :][:, :2 * ch], z2.astype(BF16)) + ya_s[q, rows, :])
        for q in range(n_units):
            b, p = divmod(q, RW_PAIRS)
            p_end = pin_s[b, rows, LANES * p:LANES * (p + 1)][end_row:end_row + 1]
            st_ref[q] = (s_old[q] + jnp.where(blockdiag, upd[q], 0.0)) * p_end
            y_ref[b, rows, LANES * p:LANES * (p + 1)] = yc[q]
        return carry

    lax.fori_loop(0, n_ch, chunk_body, 0)


def _rwkv_direction(zall, prm, bsz, nc_blocks, nl_blocks, reverse):
    assert bsz == 2
    mu, w0, wup, a0, aup, k_k, k_a, r_k = prm
    rows = zall.shape[0]
    blk = RW_BLOCK
    steps = nc_blocks + nl_blocks
    n_blocks = rows // blk

    def local(s):
        lat = nc_blocks + ((nl_blocks - 1 - (s - nc_blocks)) if reverse else (s - nc_blocks))
        ctx = (nc_blocks - 1 - s) if reverse else s
        return jnp.where(s < nc_blocks, ctx, lat)

    def block(b, s):
        i = local(s)
        return jnp.where(i < nc_blocks, b * nc_blocks + i, bsz * nc_blocks + b * nl_blocks + i - nc_blocks)

    def halo(b, s):
        i = block(b, s)
        if reverse:
            return jnp.minimum((i + 1) * (blk // 8), n_blocks * (blk // 8) - 1)
        return jnp.maximum(i * (blk // 8) - 1, 0)

    vec = lambda w: pl.BlockSpec((1, w), lambda s: (0, 0))
    mat = lambda: pl.BlockSpec((LANES, RW_WIDTH), lambda s: (0, 0))
    n_units = bsz * RW_PAIRS
    unit_bf = pltpu.VMEM((n_units, blk, LANES), BF16)
    unit_f = pltpu.VMEM((n_units, blk, LANES), F32)
    z_spec = lambda b: pl.BlockSpec((blk, RW_Z), lambda s: (block(b, s), 0))
    halo_spec = lambda b: pl.BlockSpec((8, RW_Z), lambda s: (halo(b, s), 0))
    return pl.pallas_call(
        functools.partial(_rwkv_kernel, reverse=reverse, nc_blocks=nc_blocks),
        out_shape=jax.ShapeDtypeStruct((bsz, steps * blk, RW_WIDTH), F32),
        grid=(steps,),
        in_specs=[
            z_spec(0), z_spec(1), halo_spec(0), halo_spec(1),
            vec(RW_Z), vec(RW_WIDTH), mat(), vec(RW_WIDTH), mat(), vec(RW_WIDTH), vec(RW_WIDTH), vec(RW_WIDTH),
        ],
        out_specs=pl.BlockSpec((bsz, blk, RW_WIDTH), lambda s: (0, local(s), 0)),
        scratch_shapes=[
            pltpu.VMEM((n_units, LANES, LANES), F32),
            unit_bf, unit_bf, unit_bf, unit_bf, unit_bf,
            unit_bf, unit_f, unit_f, unit_bf,
            pltpu.VMEM((bsz, blk, RW_WIDTH), F32),
        ],
        compiler_params=_cp("arbitrary"),
        name="rwkv_bwd" if reverse else "rwkv_fwd",
    )(zall, zall, zall, zall, mu, w0, wup, a0, aup, k_k, k_a, r_k)


def _s5_out_kernel(yf_ref, yb_ref, u_ref, d_ref, wv_ref, wg_ref, o_ref, z_s):
    @pl.when(pl.program_id(1) == 0)
    def _():
        z_s[...] = _gelu_tanh(yf_ref[...] + yb_ref[...] + d_ref[...] * u_ref[...]).astype(BF16)

    zz = z_s[...]
    o_ref[...] = _dot(zz, wv_ref[...]) * _sigmoid(_dot(zz, wg_ref[...]))


def _s5_out(yf, yb, u_src, u_blk, d, w_glu, bm, row0):
    n = yf.shape[0] - row0 * bm
    bn = 1024
    nj = D_MODEL // bn
    return pl.pallas_call(
        _s5_out_kernel,
        out_shape=jax.ShapeDtypeStruct((n, D_MODEL), F32),
        grid=(n // bm, nj),
        in_specs=[
            pl.BlockSpec((bm, S5_WIDTH), lambda i, j: (i + row0, 0)),
            pl.BlockSpec((bm, S5_WIDTH), lambda i, j: (i + row0, 0)),
            pl.BlockSpec((bm, S5_WIDTH), lambda i, j: (i + row0, u_blk)),
            pl.BlockSpec((1, S5_WIDTH), lambda i, j: (0, 0)),
            pl.BlockSpec((S5_WIDTH, bn), lambda i, j: (0, j)),
            pl.BlockSpec((S5_WIDTH, bn), lambda i, j: (0, j + nj)),
        ],
        out_specs=pl.BlockSpec((bm, bn), lambda i, j: (i, j)),
        scratch_shapes=[pltpu.VMEM((bm, S5_WIDTH), BF16)],
        compiler_params=_cp("parallel", "arbitrary"),
        name="s5_out",
    )(yf, yb, u_src, d, w_glu, w_glu)


def _rwkv_out_kernel(yf_ref, yb_ref, g_ref, gup_ref, lw_ref, lb_ref, w_ref, o_ref, z_s):
    @pl.when(pl.program_id(1) == 0)
    def _():
        ones64 = _head_ones(RW_HEAD)
        gate = _dot(_sigmoid(g_ref[...]).astype(BF16), gup_ref[...])
        for p in range(RW_PAIRS):
            ls = slice(LANES * p, LANES * (p + 1))
            y = yf_ref[:, ls] + yb_ref[:, ls]
            mean = _head_sum(y, ones64, True) * (1.0 / RW_HEAD)
            yc = y - mean
            var = _head_sum(yc * yc, ones64, True) * (1.0 / RW_HEAD)
            yn = yc * lax.rsqrt(var + RW_GN_EPS) * lw_ref[:, ls] + lb_ref[:, ls]
            z_s[:, ls] = (yn * gate[:, ls]).astype(BF16)

    o_ref[...] = _dot(z_s[...], w_ref[...])


def _rwkv_out(yf, yb, zall, g_up, ln_w, ln_b, w_proj, row0, nc_blocks, nl_blocks):
    bm = RW_BLOCK
    bsz = yf.shape[0]
    n = zall.shape[0] - row0 * bm
    bn = 1024
    rows = lambda w, blk=0: pl.BlockSpec((bm, w), lambda i, j: (i + row0, blk))

    def y_index(i, j):
        i = i + row0
        t = i - bsz * nc_blocks
        return (jnp.where(t < 0, i // nc_blocks, t // nl_blocks),
                jnp.where(t < 0, i % nc_blocks, nc_blocks + t % nl_blocks), 0)

    y_spec = pl.BlockSpec((None, bm, RW_WIDTH), y_index)
    vec = pl.BlockSpec((1, RW_WIDTH), lambda i, j: (0, 0))
    return pl.pallas_call(
        _rwkv_out_kernel,
        out_shape=jax.ShapeDtypeStruct((n, D_MODEL), F32),
        grid=(n // bm, D_MODEL // bn),
        in_specs=[y_spec, y_spec, rows(LANES, RW_Z // LANES),
                  pl.BlockSpec((LANES, RW_WIDTH), lambda i, j: (0, 0)), vec, vec,
                  pl.BlockSpec((RW_WIDTH, bn), lambda i, j: (0, j))],
        out_specs=pl.BlockSpec((bm, bn), lambda i, j: (i, j)),
        scratch_shapes=[pltpu.VMEM((bm, RW_WIDTH), BF16)],
        compiler_params=_cp("parallel", "arbitrary"),
        name="rwkv_out",
    )(yf, yb, zall, g_up, ln_w, ln_b, w_proj)


def _gla_out_kernel(of_ref, ob_ref, og_ref, g_ref, w_ref, o_ref, z_s):
    @pl.when(pl.program_id(1) == 0)
    def _():
        for h in range(GLA_HEADS):
            vs = slice(GLA_DV * h, GLA_DV * (h + 1))
            o = of_ref[:, vs] + ob_ref[:, vs]
            on = o * lax.rsqrt(jnp.mean(o * o, axis=-1, keepdims=True) + NORM_EPS) * g_ref[:, vs]
            z_s[:, vs] = (on * _silu(og_ref[:, vs])).astype(BF16)

    o_ref[...] = _dot(z_s[...], w_ref[...])


def _gla_out(of, ob, og_src, og_blk, norm_g, w_proj, bm, row0):
    n = of.shape[0] - row0 * bm
    bn = 1024
    rows = lambda blk=0: pl.BlockSpec((bm, GLA_V), lambda i, j: (i + row0, blk))
    return pl.pallas_call(
        _gla_out_kernel,
        out_shape=jax.ShapeDtypeStruct((n, D_MODEL), F32),
        grid=(n // bm, D_MODEL // bn),
        in_specs=[rows(), rows(), rows(og_blk), pl.BlockSpec((1, GLA_V), lambda i, j: (0, 0)),
                  pl.BlockSpec((GLA_V, bn), lambda i, j: (0, j))],
        out_specs=pl.BlockSpec((bm, bn), lambda i, j: (i, j)),
        scratch_shapes=[pltpu.VMEM((bm, GLA_V), BF16)],
        compiler_params=_cp("parallel", "arbitrary"),
        name="gla_out",
    )(of, ob, og_src, norm_g, w_proj)


def _merge_kernel(ga_ref, gb_ref, gc_ref, pa_ref, pb_ref, pc_ref, w_ref, x_ref, gate_ref, o_ref, z_s):
    @pl.when(pl.program_id(1) == 0)
    def _():
        m = (_sigmoid(ga_ref[...]) * pa_ref[...] + _sigmoid(gb_ref[...]) * pb_ref[...]
             + _sigmoid(gc_ref[...]) * pc_ref[...])
        z_s[...] = m.astype(BF16)

    o_ref[...] = x_ref[...] + gate_ref[...] * _dot(z_s[...], w_ref[...])


def _merge(gates_src, gate_blk0, pa, pb, pc, w_out, x, mod, layer, bm, row0, nc, seq):
    n = pa.shape[0]
    bn = 1024
    d = D_MODEL
    nb = d // bn
    gate = lambda t: pl.BlockSpec((bm, d), lambda i, j: (i + row0, gate_blk0 + t))
    branch = pl.BlockSpec((bm, d), lambda i, j: (i, 0))
    mod_index = _mod_spec(layer, 2, bm, nc, seq).index_map
    shifted = pl.BlockSpec((None, None, None, 1, bn), lambda i, j: mod_index(i + row0)[:4] + (j,))
    return pl.pallas_call(
        _merge_kernel,
        out_shape=jax.ShapeDtypeStruct((n, d), F32),
        grid=(n // bm, nb),
        in_specs=[gate(0), gate(1), gate(2), branch, branch, branch,
                  pl.BlockSpec((d, bn), lambda i, j: (0, j)),
                  pl.BlockSpec((bm, bn), lambda i, j: (i + row0, j)), shifted],
        out_specs=pl.BlockSpec((bm, bn), lambda i, j: (i, j)),
        scratch_shapes=[pltpu.VMEM((bm, d), BF16)],
        compiler_params=_cp("parallel", "arbitrary"),
        name="merge",
    )(gates_src, gates_src, gates_src, pa, pb, pc, w_out, x, mod)


def _router_kernel(x_ref, g_ref, sh_ref, sc_ref, wr_hi_ref, wr_lo_ref, br_ref, v_ref, comb_ref):
    x = x_ref[...]
    y = x * lax.rsqrt(jnp.mean(x * x, axis=-1, keepdims=True) + NORM_EPS) * g_ref[...]
    t = y * (1.0 + sc_ref[...]) + sh_ref[...]
    v_ref[...] = t.astype(BF16)
    t_hi, t_lo = _split_hi_lo(t)
    logits = (_dot(t_hi, wr_hi_ref[...]) + _dot(t_lo, wr_hi_ref[...]) + _dot(t_hi, wr_lo_ref[...])) + br_ref[...]
    lane = _iota(logits.shape, 1).astype(F32)
    neg = jnp.float32(-jnp.inf)
    big = jnp.float32(LANES)
    l1 = jnp.where(lane < MOE_GROUPS, logits, neg)
    m1 = jnp.max(l1, axis=-1, keepdims=True)
    p_top = 1.0 / jnp.sum(jnp.exp(l1 - m1), axis=-1, keepdims=True)
    grp = jnp.min(jnp.where(l1 == m1, lane, big), axis=-1, keepdims=True)
    lo = MOE_LANE0 + MOE_PER_GROUP * grp
    in_grp = (lane >= lo) & (lane < lo + MOE_PER_GROUP)
    l2 = jnp.where(in_grp, logits, neg)
    v1 = jnp.max(l2, axis=-1, keepdims=True)
    i1 = jnp.min(jnp.where(l2 == v1, lane, big), axis=-1, keepdims=True)
    l3 = jnp.where(lane == i1, neg, l2)
    v2 = jnp.max(l3, axis=-1, keepdims=True)
    i2 = jnp.min(jnp.where(l3 == v2, lane, big), axis=-1, keepdims=True)
    e2 = jnp.exp(v2 - v1)
    w1 = p_top / (1.0 + e2)
    w2 = p_top * e2 / (1.0 + e2)
    comb_ref[...] = jnp.where(lane == i1, w1, 0.0) + jnp.where(lane == i2, w2, 0.0)


def _router(x, g, mod, layer, wr_hi, wr_lo, br, bm, nc, seq):
    n, d = x.shape
    return pl.pallas_call(
        _router_kernel,
        out_shape=(jax.ShapeDtypeStruct((n, d), BF16), jax.ShapeDtypeStruct((n, LANES), F32)),
        grid=(n // bm,),
        in_specs=[
            pl.BlockSpec((bm, d), lambda i: (i, 0)),
            pl.BlockSpec((None, 1, d), lambda i: (layer, 0, 0)),
            _mod_spec(layer, 3, bm, nc, seq), _mod_spec(layer, 4, bm, nc, seq),
            pl.BlockSpec((d, LANES), lambda i: (0, 0)), pl.BlockSpec((d, LANES), lambda i: (0, 0)),
            pl.BlockSpec((1, LANES), lambda i: (0, 0)),
        ],
        out_specs=(pl.BlockSpec((bm, d), lambda i: (i, 0)), pl.BlockSpec((bm, LANES), lambda i: (i, 0))),
        compiler_params=_cp("parallel"),
        name="moe_router",
    )(x, g, mod, mod, wr_hi, wr_lo, br)


def _experts_kernel(v_ref, comb_ref, wg_ref, wu_ref, wd_ref, x_ref, gate_ref, o_ref, acc_s):
    e = pl.program_id(1)

    @pl.when(e == 0)
    def _():
        acc_s[...] = jnp.zeros_like(acc_s)

    t = v_ref[...]
    comb = comb_ref[...]
    w_e = jnp.sum(jnp.where(_iota(comb.shape, 1) == e + MOE_LANE0, comb, 0.0), axis=-1, keepdims=True)
    hid = _silu(_dot(t, wg_ref[...])) * _dot(t, wu_ref[...]) * w_e
    acc_s[...] += _dot(hid.astype(BF16), wd_ref[...])

    @pl.when(e == pl.num_programs(1) - 1)
    def _():
        o_ref[...] = x_ref[...] + gate_ref[...] * acc_s[...]


def _experts(v, comb, w_gate, w_up, w_down, x, mod, layer, bm, nc, seq):
    n, d = v.shape
    ne, _, hdim = w_gate.shape
    return pl.pallas_call(
        _experts_kernel,
        out_shape=jax.ShapeDtypeStruct((n, d), F32),
        grid=(n // bm, ne),
        in_specs=[
            pl.BlockSpec((bm, d), lambda i, e: (i, 0)),
            pl.BlockSpec((bm, LANES), lambda i, e: (i, 0)),
            pl.BlockSpec((None, d, hdim), lambda i, e: (e, 0, 0)),
            pl.BlockSpec((None, d, hdim), lambda i, e: (e, 0, 0)),
            pl.BlockSpec((None, hdim, d), lambda i, e: (e, 0, 0)),
            pl.BlockSpec((bm, d), lambda i, e: (i, 0)),
            _mod_spec(layer, 5, bm, nc, seq),
        ],
        out_specs=pl.BlockSpec((bm, d), lambda i, e: (i, 0)),
        scratch_shapes=[pltpu.VMEM((bm, d), F32)],
        compiler_params=_cp("parallel", "arbitrary"),
        name="moe_experts",
    )(v, comb, w_gate, w_up, w_down, x, mod)


def _final_norm_kernel(x_ref, g_ref, o_ref):
    x = x_ref[...]
    o_ref[...] = x * lax.rsqrt(jnp.mean(x * x, axis=-1, keepdims=True) + NORM_EPS) * g_ref[...]


def _final_norm(x, g, bm, row0):
    n, d = x.shape
    n -= row0 * bm
    return pl.pallas_call(
        _final_norm_kernel,
        out_shape=jax.ShapeDtypeStruct((n, d), F32),
        grid=(n // bm,),
        in_specs=[pl.BlockSpec((bm, d), lambda i: (i + row0, 0)), pl.BlockSpec((1, d), lambda i: (0, 0))],
        out_specs=pl.BlockSpec((bm, d), lambda i: (i, 0)),
        compiler_params=_cp("parallel"),
        name="final_norm",
    )(x, g)


_COL = dict(s5=0, rw=S5_WIDTH, rg=S5_WIDTH + 3 * RW_WIDTH + 128, q=4352, k=4864, v=5376, gk=6400, og=6416, gates=7440)


def _pad_rows(w, rows):
    return jnp.pad(w, ((0, rows - w.shape[0]), (0, 0)))


def _layer(i, last, xa, mod, bsz, seq, ctx_len, p):
    nc = bsz * ctx_len
    rows = xa.shape[0]
    bm = 512
    msel = dict(nc=nc, seq=seq)
    u = _normmod(xa, p['g_norm1'].reshape(-1, 1, D_MODEL), mod, i, (0, 1), bm, **msel)

    w_in = p['w_in'][i]
    col = lambda a, w: w_in[:, a:a + w].astype(BF16)
    bm_in = 1088 if rows % 1088 == 0 else bm
    w_a = jnp.concatenate([col(0, S5_WIDTH), col(_COL['og'], GLA_V), col(_COL['gates'], 3 * D_MODEL)], axis=1)
    w_z = jnp.concatenate([col(_COL['rw'], 3 * RW_WIDTH + 128), col(_COL['rg'], 128)], axis=1)
    w_qk = col(_COL['q'], 2 * GLA_QK)
    w_v = col(_COL['v'], GLA_V)
    w_gk = jnp.pad(col(_COL['gk'], 16), ((0, 0), (0, LANES - 16)))
    pa_in = _mm(u, w_a, bm_in, 1024)
    zall = _mm(u, w_z, bm_in, 1664)
    qk = _mm(u, w_qk, bm_in, 1024)
    vv = _mm(u, w_v, bm_in, 1024)
    gk = _mm(u, w_gk, bm_in, LANES)

    ya = []
    for d in range(2):
        tables = _s5_tables(*(p[k][i, d] for k in ('s5_a_re', 's5_a_im', 's5_log_dt', 's5_b_re', 's5_b_im',
                                                   's5_c_re', 's5_c_im')))
        ya.append(_s5_scan(pa_in, tables, bsz, ctx_len // S5_BLOCK, seq // S5_BLOCK, bool(d)))

    yb = []
    for d in range(2):
        prm = (
            jnp.pad(p['rw_mu'][i, d], (0, RW_Z - p['rw_mu'].shape[-1])).reshape(1, RW_Z),
            p['rw_w0'][i, d].reshape(1, -1),
            _pad_rows(p['rw_w_up'][i, d], LANES).astype(BF16),
            p['rw_a0'][i, d].reshape(1, -1),
            jnp.pad(p['rw_a_up'][i, d], ((RW_HEAD, 0), (0, 0))).astype(BF16),
            p['rw_k_k'][i].reshape(1, -1), p['rw_k_a'][i].reshape(1, -1), p['rw_r_k'][i].reshape(1, -1),
        )
        yb.append(_rwkv_direction(zall, prm, bsz, ctx_len // RW_BLOCK, seq // RW_BLOCK, bool(d)))

    gk_up = jnp.pad(p['gl_gk_up'][i], ((0, 0), (0, LANES - 16), (0, 0))).astype(BF16)
    gk_b = p['gl_gk_b'][i].reshape(2, 1, GLA_QK)
    s0 = jnp.zeros((2, bsz, GLA_HEADS, GLA_DV, GLA_DK), F32)
    oc, s_ctx = _gla(_to_chunk_major(qk[:nc], bsz), _to_chunk_major(vv[:nc], bsz), _to_chunk_major(gk[:nc], bsz),
                     gk_up, gk_b, s0)
    rows_l = seq // GRID_W
    lat = lambda t: t[nc:].reshape(bsz, rows_l, GRID_W * t.shape[1])
    ol, _ = _gla(lat(qk), lat(vv), lat(gk), gk_up, gk_b, s_ctx)
    o_dir = [jnp.concatenate([_from_chunk_major(oc[d], GLA_V).reshape(nc, GLA_V),
                              ol[d].reshape(bsz * seq, GLA_V)], axis=0) for d in range(2)]

    row0 = (nc // bm) if last else 0
    pa = _s5_out(ya[0], ya[1], pa_in, 0, p['s5_d'][i].reshape(1, -1), p['s5_w_glu'][i].astype(BF16), bm, row0)
    pb = _rwkv_out(yb[0], yb[1], zall, p['rw_g_up'][i].astype(BF16), p['rw_ln_w'][i].reshape(1, -1),
                   p['rw_ln_b'][i].reshape(1, -1), p['rw_w_proj'][i].astype(BF16), row0 * (bm // RW_BLOCK),
                   ctx_len // RW_BLOCK, seq // RW_BLOCK)
    pc = _gla_out(o_dir[0], o_dir[1], pa_in, 1, p['gl_norm_g'][i].reshape(1, -1), p['gl_w_proj'][i].astype(BF16),
                  bm, row0)
    bm_merge = bm // 2
    xm = _merge(pa_in, 1, pa, pb, pc, p['w_out'][i].astype(BF16), xa, mod, i, bm_merge, row0 * 2, nc, seq)
    if last:
        msel = dict(nc=0, seq=seq)

    wr = jnp.pad(jnp.concatenate([p['moe_wg1'][i], p['moe_wg2'][i]], axis=1), ((0, 0), (0, LANES - 36)))
    wr_hi = wr.astype(BF16)
    wr_lo = (wr - wr_hi.astype(F32)).astype(BF16)
    br = jnp.pad(jnp.concatenate([p['moe_bg1'][i], p['moe_bg2'][i]]), (0, LANES - 36)).reshape(1, LANES)
    vmoe, comb = _router(xm, p['g_norm2'].reshape(-1, 1, D_MODEL), mod, i, wr_hi, wr_lo, br, bm, **msel)
    return _experts(vmoe, comb, p['moe_w_gate'][i].astype(BF16), p['moe_w_up'][i].astype(BF16),
                    p['moe_w_down'][i].astype(BF16), xm, mod, i, bm, **msel)


def kernel(x, c, ctx, c_ctx, w_mod, b_mod, g_norm1, g_norm2, w_in, s5_a_re, s5_a_im, s5_log_dt, s5_b_re, s5_b_im,
           s5_c_re, s5_c_im, s5_d, s5_w_glu, rw_mu, rw_w0, rw_w_up, rw_a0, rw_a_up, rw_k_k, rw_k_a, rw_r_k, rw_g_up,
           rw_ln_w, rw_ln_b, rw_w_proj, gl_gk_up, gl_gk_b, gl_norm_g, gl_w_proj, w_out, moe_wg1, moe_bg1, moe_wg2,
           moe_bg2, moe_w_gate, moe_w_up, moe_w_down, g_final):
    p = dict(g_norm1=g_norm1, g_norm2=g_norm2, w_in=w_in, s5_a_re=s5_a_re, s5_a_im=s5_a_im, s5_log_dt=s5_log_dt,
             s5_b_re=s5_b_re, s5_b_im=s5_b_im, s5_c_re=s5_c_re, s5_c_im=s5_c_im, s5_d=s5_d, s5_w_glu=s5_w_glu,
             rw_mu=rw_mu, rw_w0=rw_w0, rw_w_up=rw_w_up, rw_a0=rw_a0, rw_a_up=rw_a_up, rw_k_k=rw_k_k, rw_k_a=rw_k_a,
             rw_r_k=rw_r_k.reshape(rw_r_k.shape[0], -1), rw_g_up=rw_g_up, rw_ln_w=rw_ln_w, rw_ln_b=rw_ln_b,
             rw_w_proj=rw_w_proj, gl_gk_up=gl_gk_up, gl_gk_b=gl_gk_b, gl_norm_g=gl_norm_g, gl_w_proj=gl_w_proj,
             w_out=w_out, moe_wg1=moe_wg1, moe_bg1=moe_bg1, moe_wg2=moe_wg2, moe_bg2=moe_bg2, moe_w_gate=moe_w_gate,
             moe_w_up=moe_w_up, moe_w_down=moe_w_down)
    bsz, seq, d = x.shape
    ctx_len = ctx.shape[1]
    depth = w_mod.shape[0]
    cc = jnp.concatenate([c, c_ctx[None], jnp.zeros((8 - bsz - 1, d), F32)], axis=0)
    mod = _adaln(cc, w_mod, b_mod).reshape(depth, 8, 6, 1, d)
    xa = jnp.concatenate([ctx.reshape(bsz * ctx_len, d), x.reshape(bsz * seq, d)], axis=0)
    for i in range(depth):
        xa = _layer(i, i == depth - 1, xa, mod, bsz, seq, ctx_len, p)
    out = _final_norm(xa, g_final.reshape(1, d), 512, 0)
    return out.reshape(bsz, seq, d)
```

```python
import functools
import math

import jax
import jax.numpy as jnp
from jax import lax
from jax.experimental import pallas as pl
from jax.experimental.pallas import tpu as pltpu

F32 = jnp.float32
BF16 = jnp.bfloat16

D_MODEL = 2048
GRID_W = 64
NORM_EPS = 1e-6

S5_WIDTH = 1024
S5_GROUP = 16
S5_GROUPS = 64
S5_STATE = 64
S5_MAX_RE = -1e-4
S5_TILE = 16
S5_PAIRS = S5_GROUPS // 2

RW_WIDTH = 1024
RW_HEAD = 64
RW_DECAY_SCALE = 0.606531
RW_GN_EPS = 64e-5
RW_BLOCK = 256
RW_CHUNK = 16
RW_PAIRS = RW_WIDTH // 128
RW_Z = 3 * RW_WIDTH + 128

GLA_HEADS = 4
GLA_DK = 128
GLA_DV = 256
GLA_QK = 512
GLA_V = 1024
GLA_TAU = 16.0
GLA_CHUNK = 64

MOE_GROUPS = 4
MOE_PER_GROUP = 8
MOE_EXPERTS = 32
MOE_HIDDEN = 256
MOE_LANE0 = MOE_GROUPS

LANES = 128
VMEM_LIMIT = 56 * 1024 * 1024


def _cp(*sem):
    return pltpu.CompilerParams(dimension_semantics=sem, vmem_limit_bytes=VMEM_LIMIT)


def _dot(a, b):
    return jnp.dot(a, b, preferred_element_type=F32)


def _dot_nt(a, b):
    return lax.dot_general(a, b, (((1,), (1,)), ((), ())), preferred_element_type=F32)


def _dot_tn(a, b):
    return lax.dot_general(a, b, (((0,), (0,)), ((), ())), preferred_element_type=F32)


def _sigmoid(x):
    return 1.0 / (1.0 + jnp.exp(-x))


def _silu(x):
    return x * _sigmoid(x)


def _gelu_tanh(x):
    return 0.5 * x * (1.0 + jnp.tanh(math.sqrt(2.0 / math.pi) * (x + 0.044715 * (x * x * x))))


def _split_hi_lo(x):
    hi = x.astype(BF16)
    lo = (x - hi.astype(F32)).astype(BF16)
    return hi, lo


def _iota(shape, dim):
    return lax.broadcasted_iota(jnp.int32, shape, dim)


def _head_ones(width):
    return (_iota((LANES, LANES), 0) // width == _iota((LANES, LANES), 1) // width).astype(BF16)


def _head_sum(x, ones, exact):
    if exact:
        hi, lo = _split_hi_lo(x)
        return _dot(hi, ones) + _dot(lo, ones)
    return _dot(x.astype(BF16), ones)


def _adaln_kernel(c_ref, w_ref, b_ref, o_ref):
    c = c_ref[...]
    o_ref[...] = _dot(_silu(c).astype(BF16), w_ref[...].astype(BF16)) + b_ref[...]


def _adaln(cc, w_mod, b_mod):
    depth, d, n = w_mod.shape
    bn = 1536
    return pl.pallas_call(
        _adaln_kernel,
        out_shape=jax.ShapeDtypeStruct((depth, 8, n), F32),
        grid=(depth, n // bn),
        in_specs=[
            pl.BlockSpec((8, d), lambda l, j: (0, 0)),
            pl.BlockSpec((None, d, bn), lambda l, j: (l, 0, j)),
            pl.BlockSpec((None, 1, bn), lambda l, j: (l, 0, j)),
        ],
        out_specs=pl.BlockSpec((None, 8, bn), lambda l, j: (l, 0, j)),
        compiler_params=_cp("parallel", "parallel"),
        name="adaln",
    )(cc, w_mod, b_mod.reshape(depth, 1, n))


def _mod_spec(layer, part, bm, nc, seq):
    def index(i, *_):
        r0 = i * bm
        return (layer, jnp.where(r0 < nc, 2, (r0 - nc) // seq), part, 0, 0)
    return pl.BlockSpec((None, None, None, 1, D_MODEL), index)


def _normmod_kernel(x_ref, g_ref, sh_ref, sc_ref, o_ref):
    x = x_ref[...]
    y = x * lax.rsqrt(jnp.mean(x * x, axis=-1, keepdims=True) + NORM_EPS) * g_ref[...]
    o_ref[...] = (y * (1.0 + sc_ref[...]) + sh_ref[...]).astype(o_ref.dtype)


def _normmod(x, g, mod, layer, parts, bm, nc, seq):
    n, d = x.shape
    return pl.pallas_call(
        _normmod_kernel,
        out_shape=jax.ShapeDtypeStruct((n, d), BF16),
        grid=(n // bm,),
        in_specs=[
            pl.BlockSpec((bm, d), lambda i: (i, 0)),
            pl.BlockSpec((None, 1, d), lambda i: (layer, 0, 0)),
            _mod_spec(layer, parts[0], bm, nc, seq),
            _mod_spec(layer, parts[1], bm, nc, seq),
        ],
        out_specs=pl.BlockSpec((bm, d), lambda i: (i, 0)),
        compiler_params=_cp("parallel"),
        name="normmod",
    )(x, g, mod, mod)


def _mm_kernel(x_ref, w_ref, o_ref):
    o_ref[...] = _dot(x_ref[...], w_ref[...]).astype(o_ref.dtype)


def _mm(x, w, bm, bn, out_dtype=F32):
    m, k = x.shape
    n = w.shape[1]
    return pl.pallas_call(
        _mm_kernel,
        out_shape=jax.ShapeDtypeStruct((m, n), out_dtype),
        grid=(m // bm, n // bn),
        in_specs=[pl.BlockSpec((bm, k), lambda i, j: (i, 0)), pl.BlockSpec((k, bn), lambda i, j: (0, j))],
        out_specs=pl.BlockSpec((bm, bn), lambda i, j: (i, j)),
        compiler_params=_cp("parallel", "parallel"),
        name="mm",
    )(x, w)


def _seq_blocks(bsz, nc_blocks, nl_blocks, reverse):
    def local(s):
        lat = nc_blocks + ((nl_blocks - 1 - (s - nc_blocks)) if reverse else (s - nc_blocks))
        ctx = (nc_blocks - 1 - s) if reverse else s
        return jnp.where(s < nc_blocks, ctx, lat)

    def block(b, s):
        i = local(s)
        return jnp.where(i < nc_blocks, b * nc_blocks + i, bsz * nc_blocks + b * nl_blocks + i - nc_blocks)

    return local, block


S5_PACK = 8
S5_PACKS = S5_GROUPS // S5_PACK
S5_PLANE = S5_PACK * S5_STATE
S5_BLOCK = 256
S5_ROWS = 24


def _s5_tables(a_re, a_im, log_dt, b_re, b_im, c_re, c_im):
    lam = lax.complex(jnp.minimum(a_re, S5_MAX_RE), a_im)
    ldt = lam * jnp.exp(log_dt)[:, None]
    lam_bar = jnp.exp(ldt)
    b_bar = ((lam_bar - 1.0) / lam)[..., None] * lax.complex(b_re, b_im)
    c_mat = lax.complex(c_re, c_im)
    eye = jnp.eye(S5_PACK, dtype=F32)

    def block_diag(t):
        k, g, a, b = t.shape
        return (t[:, :, :, None, :] * eye[None, :, None, :, None]).reshape(k, g * a, g * b)

    b_t = jnp.transpose(b_bar, (0, 2, 1)).reshape(S5_PACKS, S5_PACK, S5_GROUP, S5_STATE)
    bblk = jnp.concatenate([block_diag(b_t.real), block_diag(b_t.imag)], axis=2)
    c_t = jnp.transpose(c_mat, (0, 2, 1)).reshape(S5_PACKS, S5_PACK, S5_STATE, S5_GROUP)
    cblk = jnp.concatenate([block_diag(c_t.real), block_diag(-c_t.imag)], axis=1)
    expo = jnp.concatenate([jnp.arange(1, S5_TILE + 1, dtype=F32), jnp.asarray([32.0, 64.0, 128.0], F32),
                            jnp.zeros((S5_ROWS - S5_TILE - 3,), F32)])
    pw = jnp.exp(ldt[None] * expo[:, None, None]).reshape(S5_ROWS, S5_PACKS, S5_PLANE)
    pw = jnp.transpose(pw, (1, 0, 2))
    return bblk.astype(BF16), cblk.astype(BF16), pw.real, pw.imag


def _s5_scan_kernel(u_ref, bblk_ref, cblk_ref, pre_ref, pim_ref, y_ref, car_ref, up_s, h_s, hb_s, yp_s, *, reverse):
    t = S5_TILE
    n_t = S5_BLOCK // t
    pn = S5_PLANE

    @pl.when(pl.program_id(1) == 0)
    def _():
        car_ref[...] = jnp.zeros_like(car_ref)

    ra = _iota((S5_BLOCK, S5_BLOCK), 0)
    cb = _iota((S5_BLOCK, S5_BLOCK), 1)
    perm = ((ra // t == cb % t) & (ra % t == cb // t)).astype(BF16)
    up = _dot(perm, u_ref[...].astype(BF16)).astype(BF16)
    for pk in range(S5_PACKS):
        up_s[pk] = up[:, LANES * pk:LANES * (pk + 1)]
    order = list(range(t - 1, -1, -1)) if reverse else list(range(t))
    rowj = _iota((n_t, 1), 0)

    def pack_body(pk, carry):
        bu = _dot(up_s[pk], bblk_ref[pk])
        pre = pre_ref[pk]
        pim = pim_ref[pk]
        l_re, l_im = pre[0:1], pim[0:1]
        h_re = h_im = None
        for n, s in enumerate(order):
            rows = slice(t * s, t * (s + 1))
            b_re, b_im = bu[rows, :pn], bu[rows, pn:]
            if n == 0:
                h_re, h_im = b_re, b_im
            else:
                h_re, h_im = l_re * h_re - l_im * h_im + b_re, l_re * h_im + l_im * h_re + b_im
            h_s[rows, :pn] = h_re
            h_s[rows, pn:] = h_im
        c_re, c_im = car_ref[pk, 0:1, :pn], car_ref[pk, 0:1, pn:]
        first = rowj == (n_t - 1 if reverse else 0)
        g_re, g_im = pre[t - 1:t], pim[t - 1:t]
        e_re = h_re + jnp.where(first, g_re * c_re - g_im * c_im, 0.0)
        e_im = h_im + jnp.where(first, g_re * c_im + g_im * c_re, 0.0)
        step = 1
        for row in (t - 1, t, t + 1, t + 2):
            if reverse:
                s_re, s_im, ok = pltpu.roll(e_re, n_t - step, 0), pltpu.roll(e_im, n_t - step, 0), rowj < n_t - step
            else:
                s_re, s_im, ok = pltpu.roll(e_re, step, 0), pltpu.roll(e_im, step, 0), rowj >= step
            a_re, a_im = pre[row:row + 1], pim[row:row + 1]
            e_re = e_re + jnp.where(ok, a_re * s_re - a_im * s_im, 0.0)
            e_im = e_im + jnp.where(ok, a_re * s_im + a_im * s_re, 0.0)
            step *= 2
        last = 0 if reverse else n_t - 1
        car_ref[pk, 0:1, :pn] = e_re[last:last + 1]
        car_ref[pk, 0:1, pn:] = e_im[last:last + 1]
        if reverse:
            in_re = jnp.where(first, c_re, pltpu.roll(e_re, n_t - 1, 0))
            in_im = jnp.where(first, c_im, pltpu.roll(e_im, n_t - 1, 0))
        else:
            in_re = jnp.where(first, c_re, pltpu.roll(e_re, 1, 0))
            in_im = jnp.where(first, c_im, pltpu.roll(e_im, 1, 0))
        for n, s in enumerate(order):
            rows = slice(t * s, t * (s + 1))
            a_re, a_im = pre[n:n + 1], pim[n:n + 1]
            hb_s[rows, :pn] = (h_s[rows, :pn] + a_re * in_re - a_im * in_im).astype(BF16)
            hb_s[rows, pn:] = (h_s[rows, pn:] + a_re * in_im + a_im * in_re).astype(BF16)
        yp_s[pk] = _dot(hb_s[...], cblk_ref[pk]).astype(BF16)
        return carry

    lax.fori_loop(0, S5_PACKS, pack_body, 0)
    for pk in range(S5_PACKS):
        y_ref[:, LANES * pk:LANES * (pk + 1)] = _dot(perm, yp_s[pk])


def _s5_scan(u_src, tables, bsz, nc_blocks, nl_blocks, reverse):
    bblk, cblk, pre, pim = tables
    rows = u_src.shape[0]
    blk = S5_BLOCK
    _, block = _seq_blocks(bsz, nc_blocks, nl_blocks, reverse)
    whole = lambda a: pl.BlockSpec(a.shape, lambda b, s: (0,) * a.ndim)
    return pl.pallas_call(
        functools.partial(_s5_scan_kernel, reverse=reverse),
        out_shape=jax.ShapeDtypeStruct((rows, S5_WIDTH), F32),
        grid=(bsz, nc_blocks + nl_blocks),
        in_specs=[pl.BlockSpec((blk, S5_WIDTH), lambda b, s: (block(b, s), 0)),
                  whole(bblk), whole(cblk), whole(pre), whole(pim)],
        out_specs=pl.BlockSpec((blk, S5_WIDTH), lambda b, s: (block(b, s), 0)),
        scratch_shapes=[
            pltpu.VMEM((S5_PACKS, 8, 2 * S5_PLANE), F32),
            pltpu.VMEM((S5_PACKS, blk, LANES), BF16),
            pltpu.VMEM((blk, 2 * S5_PLANE), F32),
            pltpu.VMEM((blk, 2 * S5_PLANE), BF16),
            pltpu.VMEM((S5_PACKS, blk, LANES), BF16),
        ],
        compiler_params=_cp("parallel", "arbitrary"),
        name="s5_bwd" if reverse else "s5_fwd",
    )(u_src, bblk, cblk, pre, pim)


def _gla_kernel(q_ref, k_ref, v_ref, gk_ref, up_ref, gb_ref, s0_ref, o_ref, sf_ref, st_ref):
    d = pl.program_id(0)
    c = pl.program_id(2)
    n = pl.num_programs(2)

    @pl.when(c == 0)
    def _():
        st_ref[...] = s0_ref[...]

    cs = GLA_CHUNK
    sign = 1 - 2 * d
    row = _iota((cs, cs), 0)
    col = _iota((cs, cs), 1)
    causal = (row - col) * sign >= 0
    causal_b = causal.astype(BF16)
    x = _dot(gk_ref[...].astype(BF16), up_ref[...]) + gb_ref[...]
    log_a = (jnp.minimum(x, 0.0) - jnp.log(1.0 + jnp.exp(-jnp.abs(x)))) * (1.0 / GLA_TAU)
    la_hi, la_lo = _split_hi_lo(log_a)
    bcum = _dot(causal_b, la_hi) + _dot(causal_b, la_lo)
    rid = _iota((cs, 1), 0)
    b_mid = jnp.sum(jnp.where(rid == cs // 2 - d, bcum, 0.0), axis=0, keepdims=True)
    b_end = jnp.sum(jnp.where(rid == (cs - 1) * (1 - d), bcum, 0.0), axis=0, keepdims=True)
    q = q_ref[...] * (GLA_DK ** -0.5)
    k = k_ref[...]
    v = v_ref[...]
    for h in range(GLA_HEADS):
        ks = slice(GLA_DK * h, GLA_DK * (h + 1))
        vs = slice(GLA_DV * h, GLA_DV * (h + 1))
        bh = bcum[:, ks]
        qh = q[:, ks]
        kh = k[:, ks]
        vh = v[:, vs].astype(BF16)
        s_t = st_ref[h]
        q_in = (qh * jnp.exp(bh - b_mid[:, ks])).astype(BF16)
        k_in = (kh * jnp.exp(b_mid[:, ks] - bh)).astype(BF16)
        scores = jnp.where(causal, _dot_nt(q_in, k_in), 0.0).astype(BF16)
        q_st = (qh * jnp.exp(bh)).astype(BF16)
        o_ref[:, vs] = _dot(scores, vh) + _dot_nt(q_st, s_t.astype(BF16))
        k_out = (kh * jnp.exp(b_end[:, ks] - bh)).astype(BF16)
        st_ref[h] = s_t * jnp.exp(b_end[:, ks]) + _dot_tn(vh, k_out)

    @pl.when(c == n - 1)
    def _():
        sf_ref[...] = st_ref[...]


def _gla(qk, v, gk, gk_up, gk_b, s0):
    bsz, cs, w = v.shape
    n = w // GLA_V

    def chunk(d, c):
        return c + d * (n - 1 - 2 * c)

    state_spec = pl.BlockSpec((None, None, GLA_HEADS, GLA_DV, GLA_DK), lambda d, b, c: (d, b, 0, 0, 0))
    return pl.pallas_call(
        _gla_kernel,
        out_shape=(jax.ShapeDtypeStruct((2, bsz, cs, w), F32), jax.ShapeDtypeStruct(s0.shape, F32)),
        grid=(2, bsz, n),
        in_specs=[
            pl.BlockSpec((None, cs, GLA_QK), lambda d, b, c: (b, 0, 2 * chunk(d, c))),
            pl.BlockSpec((None, cs, GLA_QK), lambda d, b, c: (b, 0, 2 * chunk(d, c) + 1)),
            pl.BlockSpec((None, cs, GLA_V), lambda d, b, c: (b, 0, chunk(d, c))),
            pl.BlockSpec((None, cs, LANES), lambda d, b, c: (b, 0, chunk(d, c))),
            pl.BlockSpec((None, LANES, GLA_QK), lambda d, b, c: (d, 0, 0)),
            pl.BlockSpec((None, 1, GLA_QK), lambda d, b, c: (d, 0, 0)),
            state_spec,
        ],
        out_specs=(pl.BlockSpec((None, None, cs, GLA_V), lambda d, b, c: (d, b, 0, chunk(d, c))), state_spec),
        scratch_shapes=[pltpu.VMEM((GLA_HEADS, GLA_DV, GLA_DK), F32)],
        compiler_params=_cp("parallel", "parallel", "arbitrary"),
        name="gla",
    )(qk, qk, v, gk, gk_up, gk_b, s0)


def _to_chunk_major(t, bsz):
    n = t.shape[0] // bsz
    d = t.shape[1]
    t = t.reshape(bsz, n // GLA_CHUNK, GLA_CHUNK, d)
    return jnp.transpose(t, (0, 2, 1, 3)).reshape(bsz, GLA_CHUNK, (n // GLA_CHUNK) * d)


def _from_chunk_major(t, d):
    lead = t.shape[:-2]
    n = t.shape[-1] // d
    t = t.reshape(lead + (GLA_CHUNK, n, d))
    return jnp.swapaxes(t, -3, -2).reshape(lead + (n * GLA_CHUNK, d))


def _rwkv_kernel(z0_ref, z1_ref, halo0_ref, halo1_ref, mu_ref, w0_ref, wup_ref, a0_ref, aup_ref, kk_ref, ka_ref,
                 rk_ref, y_ref, st_ref, kt_s, bt_s, kq_s, rt_s, v_s, w_s, u_s, ya_s, ab_s, pin_s,
                 *, reverse, nc_blocks):
    step = pl.program_id(0)
    blk = RW_BLOCK
    ch = RW_CHUNK
    n_ch = blk // ch
    n_b = 2
    n_units = n_b * RW_PAIRS

    @pl.when(step == 0)
    def _():
        st_ref[...] = jnp.zeros_like(st_ref)

    rowi = _iota((blk, 1), 0)
    low = _iota((1, LANES), 1) < RW_HEAD
    ones64 = _head_ones(RW_HEAD)
    pos = rowi % ch
    seq_start = (step == 0) | (step == nc_blocks)

    for b, (z_ref, halo_ref) in enumerate(((z0_ref, halo0_ref), (z1_ref, halo1_ref))):
        z = z_ref[...]
        if reverse:
            prev = pltpu.roll(z, blk - 1, 0)
            edge = halo_ref[0:1, :]
            at_edge = rowi == blk - 1
        else:
            prev = pltpu.roll(z, 1, 0)
            edge = halo_ref[7:8, :]
            at_edge = rowi == 0
        prev = jnp.where(at_edge, jnp.where(seq_start, 0.0, edge), prev)
        zs = z + (prev - z) * mu_ref[...]
        r = zs[:, 0:RW_WIDTH]
        k = zs[:, RW_WIDTH:2 * RW_WIDTH]
        v = zs[:, 2 * RW_WIDTH:3 * RW_WIDTH]
        lora = zs[:, 3 * RW_WIDTH:3 * RW_WIDTH + LANES]
        lora_w = jnp.where(low, jnp.tanh(lora), 0.0).astype(BF16)
        lora_a = jnp.where(low, 0.0, lora).astype(BF16)
        logw = -RW_DECAY_SCALE * _sigmoid(w0_ref[...] + _dot(lora_w, wup_ref[...]))
        a = _sigmoid(a0_ref[...] + _dot(lora_a, aup_ref[...]))
        kk = k * kk_ref[...]
        kp = k * (1.0 + (a - 1.0) * ka_ref[...])
        rkb = r * kp * rk_ref[...]

        cl = logw
        sh = 1
        while sh < ch:
            if reverse:
                cl = cl + jnp.where(pos < ch - sh, pltpu.roll(cl, blk - sh, 0), 0.0)
            else:
                cl = cl + jnp.where(pos >= sh, pltpu.roll(cl, sh, 0), 0.0)
            sh *= 2
        p_in = jnp.exp(cl)
        pin_s[b] = p_in
        p_ex = jnp.exp(cl - logw)
        p_inv = jnp.exp(-cl)

        for p in range(RW_PAIRS):
            ls = slice(LANES * p, LANES * (p + 1))
            q = b * RW_PAIRS + p
            kkp = kk[:, ls]
            ssq = _head_sum(kkp * kkp, ones64, False)
            kkn = kkp * (1.0 / jnp.maximum(jnp.sqrt(ssq), 1e-12))
            kt_s[q] = (kkn * p_ex[:, ls]).astype(BF16)
            bt_s[q] = (kkn * a[:, ls] * p_inv[:, ls]).astype(BF16)
            kq_s[q] = (kp[:, ls] * p_inv[:, ls]).astype(BF16)
            rt_s[q] = (r[:, ls] * p_in[:, ls]).astype(BF16)
            v_s[q] = v[:, ls].astype(BF16)
            ya_s[q] = _head_sum(rkb[:, ls], ones64, False) * v[:, ls]

    rr = _iota((blk, blk), 0)
    cc = _iota((blk, blk), 1)
    same = rr // ch == cc // ch
    before = (cc > rr) if reverse else (cc < rr)
    strict = same & before
    incl = same & (before | (rr == cc))
    fold0 = (_iota((blk, LANES), 0) % ch == _iota((blk, LANES), 1)).astype(BF16)
    fold1 = (_iota((blk, LANES), 0) % ch + ch == _iota((blk, LANES), 1)).astype(BF16)
    lane_lo = _iota((1, LANES), 1) < RW_HEAD

    def pair_body(p, carry):
        units = [b * RW_PAIRS + p for b in range(n_b)]
        heads = [(u, hh) for u in range(n_b) for hh in range(2)]
        kt = [kt_s[q] for q in units]
        rt = [rt_s[q] for q in units]
        vv = [v_s[q] for q in units]
        ya = [ya_s[q] for q in units]
        rhs = [jnp.concatenate([bt_s[q], kq_s[q]], axis=0) for q in units]
        g = []
        for u, hh in heads:
            mine = lane_lo if hh == 0 else jnp.logical_not(lane_lo)
            zero = jnp.zeros_like(kt[u])
            lhs = jnp.concatenate([jnp.where(mine, kt[u], zero), jnp.where(mine, rt[u], zero)], axis=0)
            g.append(_dot_nt(lhs, rhs[u]))
        n1 = [jnp.where(strict, -gh[:blk, :blk], 0.0).astype(BF16) for gh in g]
        a_kq = [jnp.where(strict, gh[:blk, blk:], 0.0).astype(BF16) for gh in g]
        a_rb = [jnp.where(incl, gh[blk:, :blk], 0.0).astype(BF16) for gh in g]
        a_rq = [jnp.where(incl, gh[blk:, blk:], 0.0).astype(BF16) for gh in g]
        n2 = [_dot(n, n).astype(BF16) for n in n1]
        akv = [_dot(a, vv[u]) for a, (u, _) in zip(a_kq, heads)]
        n4 = [_dot(n, n).astype(BF16) for n in n2]
        y_in = [_dot(a, vv[u]) for a, (u, _) in zip(a_rq, heads)]
        n8 = [_dot(n, n).astype(BF16) for n in n4]
        fold = [_dot(a, fold0 if hh == 0 else fold1) for a, (_, hh) in zip(a_rb, heads)]
        rhs_t = [jnp.concatenate([kt[u].astype(F32), av], axis=1) for av, (u, _) in zip(akv, heads)]
        for nk in (n8, n4, n2, n1):
            rhs_t = [x + _dot(n, x.astype(BF16)) for n, x in zip(nk, rhs_t)]
        for u, q in enumerate(units):
            h0, h1 = 2 * u, 2 * u + 1
            w_s[q] = jnp.where(lane_lo, rhs_t[h0][:, :LANES], rhs_t[h1][:, :LANES]).astype(BF16)
            u_s[q] = jnp.where(lane_lo, rhs_t[h0][:, LANES:], rhs_t[h1][:, LANES:])
            ya_s[q] = ya[u] + jnp.where(lane_lo, y_in[h0], y_in[h1])
            ab_s[q] = (fold[h0] + fold[h1]).astype(BF16)
        return carry

    lax.fori_loop(0, RW_PAIRS, pair_body, 0)

    blockdiag = (_iota((LANES, LANES), 0) // RW_HEAD) == (_iota((LANES, LANES), 1) // RW_HEAD)
    end_row = 0 if reverse else ch - 1

    def chunk_body(i, carry):
        c = (n_ch - 1 - i) if reverse else i
        rows = pl.ds(pl.multiple_of(c * ch, ch), ch)
        s_old = [st_ref[q] for q in range(n_units)]
        m1 = [_dot_nt(jnp.concatenate([w_s[q, rows, :], rt_s[q, rows, :]], axis=0), s_old[q].astype(BF16))
              for q in range(n_units)]
        zc = [-(m1[q][:ch] + u_s[q, rows, :]) for q in range(n_units)]
        upd = []
        for q in range(n_units):
            zv = jnp.concatenate([zc[q].astype(BF16), v_s[q, rows, :]], axis=0)
            bk = jnp.concatenate([bt_s[q, rows, :], kq_s[q, rows, :]], axis=0)
            upd.append(_dot_tn(zv, bk))
        yc = []
        for q in range(n_units):
            z2 = jnp.concatenate([jnp.where(lane_lo, zc[q], 0.0), jnp.where(lane_lo, 0.0, zc[q])], axis=0)
            yc.append(m1[q][ch:] + _dot(ab_s[q, rows, :][:, :2 * ch], z2.astype(BF16)) + ya_s[q, rows, :])
        for q in range(n_units):
            b, p = divmod(q, RW_PAIRS)
            p_end = pin_s[b, rows, LANES * p:LANES * (p + 1)][end_row:end_row + 1]
            st_ref[q] = (s_old[q] + jnp.where(blockdiag, upd[q], 0.0)) * p_end
            y_ref[b, rows, LANES * p:LANES * (p + 1)] = yc[q]
        return carry

    lax.fori_loop(0, n_ch, chunk_body, 0)


def _rwkv_direction(zall, prm, bsz, nc_blocks, nl_blocks, reverse):
    assert bsz == 2
    mu, w0, wup, a0, aup, k_k, k_a, r_k = prm
    rows = zall.shape[0]
    blk = RW_BLOCK
    steps = nc_blocks + nl_blocks
    n_blocks = rows // blk

    def local(s):
        lat = nc_blocks + ((nl_blocks - 1 - (s - nc_blocks)) if reverse else (s - nc_blocks))
        ctx = (nc_blocks - 1 - s) if reverse else s
        return jnp.where(s < nc_blocks, ctx, lat)

    def block(b, s):
        i = local(s)
        return jnp.where(i < nc_blocks, b * nc_blocks + i, bsz * nc_blocks + b * nl_blocks + i - nc_blocks)

    def halo(b, s):
        i = block(b, s)
        if reverse:
            return jnp.minimum((i + 1) * (blk // 8), n_blocks * (blk // 8) - 1)
        return jnp.maximum(i * (blk // 8) - 1, 0)

    vec = lambda w: pl.BlockSpec((1, w), lambda s: (0, 0))
    mat = lambda: pl.BlockSpec((LANES, RW_WIDTH), lambda s: (0, 0))
    n_units = bsz * RW_PAIRS
    unit_bf = pltpu.VMEM((n_units, blk, LANES), BF16)
    unit_f = pltpu.VMEM((n_units, blk, LANES), F32)
    z_spec = lambda b: pl.BlockSpec((blk, RW_Z), lambda s: (block(b, s), 0))
    halo_spec = lambda b: pl.BlockSpec((8, RW_Z), lambda s: (halo(b, s), 0))
    return pl.pallas_call(
        functools.partial(_rwkv_kernel, reverse=reverse, nc_blocks=nc_blocks),
        out_shape=jax.ShapeDtypeStruct((bsz, steps * blk, RW_WIDTH), F32),
        grid=(steps,),
        in_specs=[
            z_spec(0), z_spec(1), halo_spec(0), halo_spec(1),
            vec(RW_Z), vec(RW_WIDTH), mat(), vec(RW_WIDTH), mat(), vec(RW_WIDTH), vec(RW_WIDTH), vec(RW_WIDTH),
        ],
        out_specs=pl.BlockSpec((bsz, blk, RW_WIDTH), lambda s: (0, local(s), 0)),
        scratch_shapes=[
            pltpu.VMEM((n_units, LANES, LANES), F32),
            unit_bf, unit_bf, unit_bf, unit_bf, unit_bf,
            unit_bf, unit_f, unit_f, unit_bf,
            pltpu.VMEM((bsz, blk, RW_WIDTH), F32),
        ],
        compiler_params=_cp("arbitrary"),
        name="rwkv_bwd" if reverse else "rwkv_fwd",
    )(zall, zall, zall, zall, mu, w0, wup, a0, aup, k_k, k_a, r_k)


def _s5_out_kernel(yf_ref, yb_ref, u_ref, d_ref, wv_ref, wg_ref, o_ref, z_s):
    @pl.when(pl.program_id(1) == 0)
    def _():
        z_s[...] = _gelu_tanh(yf_ref[...] + yb_ref[...] + d_ref[...] * u_ref[...]).astype(BF16)

    zz = z_s[...]
    o_ref[...] = _dot(zz, wv_ref[...]) * _sigmoid(_dot(zz, wg_ref[...]))


def _s5_out(yf, yb, u_src, u_blk, d, w_glu, bm, row0):
    n = yf.shape[0] - row0 * bm
    bn = 1024
    nj = D_MODEL // bn
    return pl.pallas_call(
        _s5_out_kernel,
        out_shape=jax.ShapeDtypeStruct((n, D_MODEL), F32),
        grid=(n // bm, nj),
        in_specs=[
            pl.BlockSpec((bm, S5_WIDTH), lambda i, j: (i + row0, 0)),
            pl.BlockSpec((bm, S5_WIDTH), lambda i, j: (i + row0, 0)),
            pl.BlockSpec((bm, S5_WIDTH), lambda i, j: (i + row0, u_blk)),
            pl.BlockSpec((1, S5_WIDTH), lambda i, j: (0, 0)),
            pl.BlockSpec((S5_WIDTH, bn), lambda i, j: (0, j)),
            pl.BlockSpec((S5_WIDTH, bn), lambda i, j: (0, j + nj)),
        ],
        out_specs=pl.BlockSpec((bm, bn), lambda i, j: (i, j)),
        scratch_shapes=[pltpu.VMEM((bm, S5_WIDTH), BF16)],
        compiler_params=_cp("parallel", "arbitrary"),
        name="s5_out",
    )(yf, yb, u_src, d, w_glu, w_glu)


def _rwkv_out_kernel(yf_ref, yb_ref, g_ref, gup_ref, lw_ref, lb_ref, w_ref, o_ref, z_s):
    @pl.when(pl.program_id(1) == 0)
    def _():
        ones64 = _head_ones(RW_HEAD)
        gate = _dot(_sigmoid(g_ref[...]).astype(BF16), gup_ref[...])
        for p in range(RW_PAIRS):
            ls = slice(LANES * p, LANES * (p + 1))
            y = yf_ref[:, ls] + yb_ref[:, ls]
            mean = _head_sum(y, ones64, True) * (1.0 / RW_HEAD)
            yc = y - mean
            var = _head_sum(yc * yc, ones64, True) * (1.0 / RW_HEAD)
            yn = yc * lax.rsqrt(var + RW_GN_EPS) * lw_ref[:, ls] + lb_ref[:, ls]
            z_s[:, ls] = (yn * gate[:, ls]).astype(BF16)

    o_ref[...] = _dot(z_s[...], w_ref[...])


def _rwkv_out(yf, yb, zall, g_up, ln_w, ln_b, w_proj, row0, nc_blocks, nl_blocks):
    bm = RW_BLOCK
    bsz = yf.shape[0]
    n = zall.shape[0] - row0 * bm
    bn = 1024
    rows = lambda w, blk=0: pl.BlockSpec((bm, w), lambda i, j: (i + row0, blk))

    def y_index(i, j):
        i = i + row0
        t = i - bsz * nc_blocks
        return (jnp.where(t < 0, i // nc_blocks, t // nl_blocks),
                jnp.where(t < 0, i % nc_blocks, nc_blocks + t % nl_blocks), 0)

    y_spec = pl.BlockSpec((None, bm, RW_WIDTH), y_index)
    vec = pl.BlockSpec((1, RW_WIDTH), lambda i, j: (0, 0))
    return pl.pallas_call(
        _rwkv_out_kernel,
        out_shape=jax.ShapeDtypeStruct((n, D_MODEL), F32),
        grid=(n // bm, D_MODEL // bn),
        in_specs=[y_spec, y_spec, rows(LANES, RW_Z // LANES),
                  pl.BlockSpec((LANES, RW_WIDTH), lambda i, j: (0, 0)), vec, vec,
                  pl.BlockSpec((RW_WIDTH, bn), lambda i, j: (0, j))],
        out_specs=pl.BlockSpec((bm, bn), lambda i, j: (i, j)),
        scratch_shapes=[pltpu.VMEM((bm, RW_WIDTH), BF16)],
        compiler_params=_cp("parallel", "arbitrary"),
        name="rwkv_out",
    )(yf, yb, zall, g_up, ln_w, ln_b, w_proj)


def _gla_out_kernel(of_ref, ob_ref, og_ref, g_ref, w_ref, o_ref, z_s):
    @pl.when(pl.program_id(1) == 0)
    def _():
        for h in range(GLA_HEADS):
            vs = slice(GLA_DV * h, GLA_DV * (h + 1))
            o = of_ref[:, vs] + ob_ref[:, vs]
            on = o * lax.rsqrt(jnp.mean(o * o, axis=-1, keepdims=True) + NORM_EPS) * g_ref[:, vs]
            z_s[:, vs] = (on * _silu(og_ref[:, vs])).astype(BF16)

    o_ref[...] = _dot(z_s[...], w_ref[...])


def _gla_out(of, ob, og_src, og_blk, norm_g, w_proj, bm, row0):
    n = of.shape[0] - row0 * bm
    bn = 1024
    rows = lambda blk=0: pl.BlockSpec((bm, GLA_V), lambda i, j: (i + row0, blk))
    return pl.pallas_call(
        _gla_out_kernel,
        out_shape=jax.ShapeDtypeStruct((n, D_MODEL), F32),
        grid=(n // bm, D_MODEL // bn),
        in_specs=[rows(), rows(), rows(og_blk), pl.BlockSpec((1, GLA_V), lambda i, j: (0, 0)),
                  pl.BlockSpec((GLA_V, bn), lambda i, j: (0, j))],
        out_specs=pl.BlockSpec((bm, bn), lambda i, j: (i, j)),
        scratch_shapes=[pltpu.VMEM((bm, GLA_V), BF16)],
        compiler_params=_cp("parallel", "arbitrary"),
        name="gla_out",
    )(of, ob, og_src, norm_g, w_proj)


def _merge_kernel(ga_ref, gb_ref, gc_ref, pa_ref, pb_ref, pc_ref, w_ref, x_ref, gate_ref, o_ref, z_s):
    @pl.when(pl.program_id(1) == 0)
    def _():
        m = (_sigmoid(ga_ref[...]) * pa_ref[...] + _sigmoid(gb_ref[...]) * pb_ref[...]
             + _sigmoid(gc_ref[...]) * pc_ref[...])
        z_s[...] = m.astype(BF16)

    o_ref[...] = x_ref[...] + gate_ref[...] * _dot(z_s[...], w_ref[...])


def _merge(gates_src, gate_blk0, pa, pb, pc, w_out, x, mod, layer, bm, row0, nc, seq):
    n = pa.shape[0]
    bn = 1024
    d = D_MODEL
    nb = d // bn
    gate = lambda t: pl.BlockSpec((bm, d), lambda i, j: (i + row0, gate_blk0 + t))
    branch = pl.BlockSpec((bm, d), lambda i, j: (i, 0))
    mod_index = _mod_spec(layer, 2, bm, nc, seq).index_map
    shifted = pl.BlockSpec((None, None, None, 1, bn), lambda i, j: mod_index(i + row0)[:4] + (j,))
    return pl.pallas_call(
        _merge_kernel,
        out_shape=jax.ShapeDtypeStruct((n, d), F32),
        grid=(n // bm, nb),
        in_specs=[gate(0), gate(1), gate(2), branch, branch, branch,
                  pl.BlockSpec((d, bn), lambda i, j: (0, j)),
                  pl.BlockSpec((bm, bn), lambda i, j: (i + row0, j)), shifted],
        out_specs=pl.BlockSpec((bm, bn), lambda i, j: (i, j)),
        scratch_shapes=[pltpu.VMEM((bm, d), BF16)],
        compiler_params=_cp("parallel", "arbitrary"),
        name="merge",
    )(gates_src, gates_src, gates_src, pa, pb, pc, w_out, x, mod)


MOE_TM = 256
META_E1, META_E2, META_R1, META_R2, META_W1, META_W2 = range(6)


def _router_kernel(x_ref, g_ref, sh_ref, sc_ref, wr_hi_ref, wr_lo_ref, br_ref, v_ref, meta_ref, cnt_ref, base_s):
    @pl.when(pl.program_id(0) == 0)
    def _():
        base_s[...] = jnp.zeros_like(base_s)

    x = x_ref[...]
    y = x * lax.rsqrt(jnp.mean(x * x, axis=-1, keepdims=True) + NORM_EPS) * g_ref[...]
    t = y * (1.0 + sc_ref[...]) + sh_ref[...]
    v_ref[...] = t
    t_hi, t_lo = _split_hi_lo(t)
    logits = (_dot(t_hi, wr_hi_ref[...]) + _dot(t_lo, wr_hi_ref[...]) + _dot(t_hi, wr_lo_ref[...])) + br_ref[...]
    lane = _iota(logits.shape, 1).astype(F32)
    neg = jnp.float32(-jnp.inf)
    big = jnp.float32(LANES)
    l1 = jnp.where(lane < MOE_GROUPS, logits, neg)
    m1 = jnp.max(l1, axis=-1, keepdims=True)
    p_top = 1.0 / jnp.sum(jnp.exp(l1 - m1), axis=-1, keepdims=True)
    grp = jnp.min(jnp.where(l1 == m1, lane, big), axis=-1, keepdims=True)
    lo = MOE_LANE0 + MOE_PER_GROUP * grp
    in_grp = (lane >= lo) & (lane < lo + MOE_PER_GROUP)
    l2 = jnp.where(in_grp, logits, neg)
    v1 = jnp.max(l2, axis=-1, keepdims=True)
    i1 = jnp.min(jnp.where(l2 == v1, lane, big), axis=-1, keepdims=True)
    l3 = jnp.where(lane == i1, neg, l2)
    v2 = jnp.max(l3, axis=-1, keepdims=True)
    i2 = jnp.min(jnp.where(l3 == v2, lane, big), axis=-1, keepdims=True)
    e2 = jnp.exp(v2 - v1)
    w1 = p_top / (1.0 + e2)
    w2 = p_top * e2 / (1.0 + e2)
    pick1 = lane == i1
    pick2 = lane == i2
    chosen = jnp.where(pick1 | pick2, 1.0, 0.0)
    bm = x.shape[0]
    earlier = (_iota((bm, bm), 1) < _iota((bm, bm), 0)).astype(BF16)
    before = _dot(earlier, chosen.astype(BF16)) + base_s[...]
    r1 = jnp.sum(jnp.where(pick1, before, 0.0), axis=-1, keepdims=True)
    r2 = jnp.sum(jnp.where(pick2, before, 0.0), axis=-1, keepdims=True)
    base_s[...] += jnp.sum(chosen, axis=0, keepdims=True)
    cnt_ref[...] = base_s[...]
    meta = jnp.zeros_like(logits)
    for slot, val in ((META_E1, i1 - MOE_LANE0), (META_E2, i2 - MOE_LANE0), (META_R1, r1), (META_R2, r2),
                      (META_W1, w1), (META_W2, w2)):
        meta = jnp.where(lane == slot, val, meta)
    meta_ref[...] = meta


def _router(x, g, mod, layer, wr_hi, wr_lo, br, bm, nc, seq):
    n, d = x.shape
    return pl.pallas_call(
        _router_kernel,
        out_shape=(jax.ShapeDtypeStruct((n, d), F32), jax.ShapeDtypeStruct((n, LANES), F32),
                   jax.ShapeDtypeStruct((1, LANES), F32)),
        grid=(n // bm,),
        in_specs=[
            pl.BlockSpec((bm, d), lambda i: (i, 0)),
            pl.BlockSpec((None, 1, d), lambda i: (layer, 0, 0)),
            _mod_spec(layer, 3, bm, nc, seq), _mod_spec(layer, 4, bm, nc, seq),
            pl.BlockSpec((d, LANES), lambda i: (0, 0)), pl.BlockSpec((d, LANES), lambda i: (0, 0)),
            pl.BlockSpec((1, LANES), lambda i: (0, 0)),
        ],
        out_specs=(pl.BlockSpec((bm, d), lambda i: (i, 0)), pl.BlockSpec((bm, LANES), lambda i: (i, 0)),
                   pl.BlockSpec((1, LANES), lambda i: (0, 0))),
        scratch_shapes=[pltpu.VMEM((1, LANES), F32)],
        compiler_params=_cp("arbitrary"),
        name="moe_router",
    )(x, g, mod, mod, wr_hi, wr_lo, br)


def _moe_plan(meta, cnt, n_tok):
    tm = MOE_TM
    counts = cnt[0, MOE_LANE0:MOE_LANE0 + MOE_EXPERTS].astype(jnp.int32)
    seg = ((counts + tm - 1) // tm) * tm
    ends = jnp.cumsum(seg)
    off = ends - seg
    col = lambda c: meta[:, c].astype(jnp.int32)
    d1 = off[col(META_E1)] + col(META_R1)
    d2 = off[col(META_E2)] + col(META_R2)
    n_rows = 2 * n_tok + MOE_EXPERTS * tm
    n_tiles = n_rows // tm
    tile_e = jnp.sum(((jnp.arange(n_tiles) * tm)[:, None] >= ends[None, :]).astype(jnp.int32), axis=1)
    tile_e = jnp.minimum(tile_e, MOE_EXPERTS - 1)
    tok = jnp.arange(n_tok, dtype=jnp.int32)
    src = jnp.zeros((n_rows,), jnp.int32).at[d1].set(tok).at[d2].set(tok)
    return d1, d2, src, tile_e, (ends[-1] // tm).reshape(1)


def _row_copy(src_hbm, row, dst, slot, r, sem):
    return pltpu.make_async_copy(src_hbm.at[pl.ds(row, 1)], dst.at[slot, pl.ds(r, 1)], sem)


def _experts_kernel(src_ref, te_ref, nu_ref, v_hbm, wg_ref, wu_ref, wd_ref, y_ref, xbuf, sem):
    del te_ref
    i = pl.program_id(0)
    n_used = nu_ref[0]
    slot = i % 2

    def gather(tile, into):
        def body(r, carry):
            _row_copy(v_hbm, src_ref[tile * MOE_TM + r], xbuf, into, r, sem.at[into]).start()
            return carry
        lax.fori_loop(0, MOE_TM, body, 0)

    @pl.when((i == 0) & (n_used > 0))
    def _():
        gather(0, 0)

    @pl.when(i + 1 < n_used)
    def _():
        gather(i + 1, 1 - slot)

    @pl.when(i < n_used)
    def _():
        def body(r, carry):
            _row_copy(v_hbm, 0, xbuf, slot, r, sem.at[slot]).wait()
            return carry
        lax.fori_loop(0, MOE_TM, body, 0)
        t = xbuf[slot].astype(BF16)
        hid = _silu(_dot(t, wg_ref[...].astype(BF16))) * _dot(t, wu_ref[...].astype(BF16))
        y_ref[...] = _dot(hid.astype(BF16), wd_ref[...].astype(BF16))

    @pl.when(i >= n_used)
    def _():
        y_ref[...] = jnp.zeros_like(y_ref)


def _experts(v, src, tile_e, n_used, w_gate, w_up, w_down):
    d = v.shape[1]
    hdim = w_gate.shape[2]
    n_rows = src.shape[0]
    tm = MOE_TM
    by_expert = lambda i, src_r, te_r, nu_r: (te_r[i], 0, 0)
    return pl.pallas_call(
        _experts_kernel,
        out_shape=jax.ShapeDtypeStruct((n_rows, d), F32),
        grid_spec=pltpu.PrefetchScalarGridSpec(
            num_scalar_prefetch=3,
            grid=(n_rows // tm,),
            in_specs=[
                pl.BlockSpec(memory_space=pl.ANY),
                pl.BlockSpec((None, d, hdim), by_expert),
                pl.BlockSpec((None, d, hdim), by_expert),
                pl.BlockSpec((None, hdim, d), by_expert),
            ],
            out_specs=pl.BlockSpec((tm, d), lambda i, *_: (i, 0)),
            scratch_shapes=[pltpu.VMEM((2, tm, d), F32), pltpu.SemaphoreType.DMA((2,))],
        ),
        compiler_params=_cp("arbitrary"),
        name="moe_experts",
    )(src, tile_e, n_used, v, w_gate, w_up, w_down)


def _combine_kernel(d1_ref, d2_ref, y_hbm, meta_ref, x_ref, gate_ref, o_ref, buf1, buf2, sem):
    i = pl.program_id(0)
    bm = x_ref.shape[0]
    slot = i % 2

    def gather(tile, into):
        def body(r, carry):
            t = tile * bm + r
            _row_copy(y_hbm, d1_ref[t], buf1, into, r, sem.at[0, into]).start()
            _row_copy(y_hbm, d2_ref[t], buf2, into, r, sem.at[1, into]).start()
            return carry
        lax.fori_loop(0, bm, body, 0)

    @pl.when(i == 0)
    def _():
        gather(0, 0)

    @pl.when(i + 1 < pl.num_programs(0))
    def _():
        gather(i + 1, 1 - slot)

    def wait_body(r, carry):
        _row_copy(y_hbm, 0, buf1, slot, r, sem.at[0, slot]).wait()
        _row_copy(y_hbm, 0, buf2, slot, r, sem.at[1, slot]).wait()
        return carry
    lax.fori_loop(0, bm, wait_body, 0)
    meta = meta_ref[...]
    lane = _iota(meta.shape, 1)
    w1 = jnp.sum(jnp.where(lane == META_W1, meta, 0.0), axis=-1, keepdims=True)
    w2 = jnp.sum(jnp.where(lane == META_W2, meta, 0.0), axis=-1, keepdims=True)
    o_ref[...] = x_ref[...] + gate_ref[...] * (w1 * buf1[slot] + w2 * buf2[slot])


def _combine(y, d1, d2, meta, x, mod, layer, bm, nc, seq):
    n, d = x.shape
    at_tile = lambda i, *_: (i, 0)
    mod_index = _mod_spec(layer, 5, bm, nc, seq).index_map
    return pl.pallas_call(
        _combine_kernel,
        out_shape=jax.ShapeDtypeStruct((n, d), F32),
        grid_spec=pltpu.PrefetchScalarGridSpec(
            num_scalar_prefetch=2,
            grid=(n // bm,),
            in_specs=[
                pl.BlockSpec(memory_space=pl.ANY),
                pl.BlockSpec((bm, LANES), at_tile),
                pl.BlockSpec((bm, d), at_tile),
                pl.BlockSpec((None, None, None, 1, d), lambda i, *_: mod_index(i)),
            ],
            out_specs=pl.BlockSpec((bm, d), at_tile),
            scratch_shapes=[pltpu.VMEM((2, bm, d), F32), pltpu.VMEM((2, bm, d), F32),
                            pltpu.SemaphoreType.DMA((2, 2))],
        ),
        compiler_params=_cp("arbitrary"),
        name="moe_combine",
    )(d1, d2, y, meta, x, mod)


def _final_norm_kernel(x_ref, g_ref, o_ref):
    x = x_ref[...]
    o_ref[...] = x * lax.rsqrt(jnp.mean(x * x, axis=-1, keepdims=True) + NORM_EPS) * g_ref[...]


def _final_norm(x, g, bm, row0):
    n, d = x.shape
    n -= row0 * bm
    return pl.pallas_call(
        _final_norm_kernel,
        out_shape=jax.ShapeDtypeStruct((n, d), F32),
        grid=(n // bm,),
        in_specs=[pl.BlockSpec((bm, d), lambda i: (i + row0, 0)), pl.BlockSpec((1, d), lambda i: (0, 0))],
        out_specs=pl.BlockSpec((bm, d), lambda i: (i, 0)),
        compiler_params=_cp("parallel"),
        name="final_norm",
    )(x, g)


_COL = dict(s5=0, rw=S5_WIDTH, rg=S5_WIDTH + 3 * RW_WIDTH + 128, q=4352, k=4864, v=5376, gk=6400, og=6416, gates=7440)


def _pad_rows(w, rows):
    return jnp.pad(w, ((0, rows - w.shape[0]), (0, 0)))


def _layer(i, last, xa, mod, bsz, seq, ctx_len, p):
    nc = bsz * ctx_len
    rows = xa.shape[0]
    bm = 512
    msel = dict(nc=nc, seq=seq)
    u = _normmod(xa, p['g_norm1'].reshape(-1, 1, D_MODEL), mod, i, (0, 1), bm, **msel)

    w_in = p['w_in'][i]
    col = lambda a, w: w_in[:, a:a + w].astype(BF16)
    bm_in = 1088 if rows % 1088 == 0 else bm
    w_a = jnp.concatenate([col(0, S5_WIDTH), col(_COL['og'], GLA_V), col(_COL['gates'], 3 * D_MODEL)], axis=1)
    w_z = jnp.concatenate([col(_COL['rw'], 3 * RW_WIDTH + 128), col(_COL['rg'], 128)], axis=1)
    w_qk = col(_COL['q'], 2 * GLA_QK)
    w_v = col(_COL['v'], GLA_V)
    w_gk = jnp.pad(col(_COL['gk'], 16), ((0, 0), (0, LANES - 16)))
    pa_in = _mm(u, w_a, bm_in, 1024)
    zall = _mm(u, w_z, bm_in, 1664)
    qk = _mm(u, w_qk, bm_in, 1024)
    vv = _mm(u, w_v, bm_in, 1024)
    gk = _mm(u, w_gk, bm_in, LANES)

    ya = []
    for d in range(2):
        tables = _s5_tables(*(p[k][i, d] for k in ('s5_a_re', 's5_a_im', 's5_log_dt', 's5_b_re', 's5_b_im',
                                                   's5_c_re', 's5_c_im')))
        ya.append(_s5_scan(pa_in, tables, bsz, ctx_len // S5_BLOCK, seq // S5_BLOCK, bool(d)))

    yb = []
    for d in range(2):
        prm = (
            jnp.pad(p['rw_mu'][i, d], (0, RW_Z - p['rw_mu'].shape[-1])).reshape(1, RW_Z),
            p['rw_w0'][i, d].reshape(1, -1),
            _pad_rows(p['rw_w_up'][i, d], LANES).astype(BF16),
            p['rw_a0'][i, d].reshape(1, -1),
            jnp.pad(p['rw_a_up'][i, d], ((RW_HEAD, 0), (0, 0))).astype(BF16),
            p['rw_k_k'][i].reshape(1, -1), p['rw_k_a'][i].reshape(1, -1), p['rw_r_k'][i].reshape(1, -1),
        )
        yb.append(_rwkv_direction(zall, prm, bsz, ctx_len // RW_BLOCK, seq // RW_BLOCK, bool(d)))

    gk_up = jnp.pad(p['gl_gk_up'][i], ((0, 0), (0, LANES - 16), (0, 0))).astype(BF16)
    gk_b = p['gl_gk_b'][i].reshape(2, 1, GLA_QK)
    s0 = jnp.zeros((2, bsz, GLA_HEADS, GLA_DV, GLA_DK), F32)
    oc, s_ctx = _gla(_to_chunk_major(qk[:nc], bsz), _to_chunk_major(vv[:nc], bsz), _to_chunk_major(gk[:nc], bsz),
                     gk_up, gk_b, s0)
    rows_l = seq // GRID_W
    lat = lambda t: t[nc:].reshape(bsz, rows_l, GRID_W * t.shape[1])
    ol, _ = _gla(lat(qk), lat(vv), lat(gk), gk_up, gk_b, s_ctx)
    o_dir = [jnp.concatenate([_from_chunk_major(oc[d], GLA_V).reshape(nc, GLA_V),
                              ol[d].reshape(bsz * seq, GLA_V)], axis=0) for d in range(2)]

    row0 = (nc // bm) if last else 0
    pa = _s5_out(ya[0], ya[1], pa_in, 0, p['s5_d'][i].reshape(1, -1), p['s5_w_glu'][i].astype(BF16), bm, row0)
    pb = _rwkv_out(yb[0], yb[1], zall, p['rw_g_up'][i].astype(BF16), p['rw_ln_w'][i].reshape(1, -1),
                   p['rw_ln_b'][i].reshape(1, -1), p['rw_w_proj'][i].astype(BF16), row0 * (bm // RW_BLOCK),
                   ctx_len // RW_BLOCK, seq // RW_BLOCK)
    pc = _gla_out(o_dir[0], o_dir[1], pa_in, 1, p['gl_norm_g'][i].reshape(1, -1), p['gl_w_proj'][i].astype(BF16),
                  bm, row0)
    bm_merge = bm // 2
    xm = _merge(pa_in, 1, pa, pb, pc, p['w_out'][i].astype(BF16), xa, mod, i, bm_merge, row0 * 2, nc, seq)
    if last:
        msel = dict(nc=0, seq=seq)

    wr = jnp.pad(jnp.concatenate([p['moe_wg1'][i], p['moe_wg2'][i]], axis=1), ((0, 0), (0, LANES - 36)))
    wr_hi = wr.astype(BF16)
    wr_lo = (wr - wr_hi.astype(F32)).astype(BF16)
    br = jnp.pad(jnp.concatenate([p['moe_bg1'][i], p['moe_bg2'][i]]), (0, LANES - 36)).reshape(1, LANES)
    vmoe, meta, cnt = _router(xm, p['g_norm2'].reshape(-1, 1, D_MODEL), mod, i, wr_hi, wr_lo, br, bm, **msel)
    d1, d2, src, tile_e, n_used = _moe_plan(meta, cnt, xm.shape[0])
    y_sorted = _experts(vmoe, src, tile_e, n_used, p['moe_w_gate'][i], p['moe_w_up'][i], p['moe_w_down'][i])
    return _combine(y_sorted, d1, d2, meta, xm, mod, i, bm // 2, **msel)


def kernel(x, c, ctx, c_ctx, w_mod, b_mod, g_norm1, g_norm2, w_in, s5_a_re, s5_a_im, s5_log_dt, s5_b_re, s5_b_im,
           s5_c_re, s5_c_im, s5_d, s5_w_glu, rw_mu, rw_w0, rw_w_up, rw_a0, rw_a_up, rw_k_k, rw_k_a, rw_r_k, rw_g_up,
           rw_ln_w, rw_ln_b, rw_w_proj, gl_gk_up, gl_gk_b, gl_norm_g, gl_w_proj, w_out, moe_wg1, moe_bg1, moe_wg2,
           moe_bg2, moe_w_gate, moe_w_up, moe_w_down, g_final):
    p = dict(g_norm1=g_norm1, g_norm2=g_norm2, w_in=w_in, s5_a_re=s5_a_re, s5_a_im=s5_a_im, s5_log_dt=s5_log_dt,
             s5_b_re=s5_b_re, s5_b_im=s5_b_im, s5_c_re=s5_c_re, s5_c_im=s5_c_im, s5_d=s5_d, s5_w_glu=s5_w_glu,
             rw_mu=rw_mu, rw_w0=rw_w0, rw_w_up=rw_w_up, rw_a0=rw_a0, rw_a_up=rw_a_up, rw_k_k=rw_k_k, rw_k_a=rw_k_a,
             rw_r_k=rw_r_k.reshape(rw_r_k.shape[0], -1), rw_g_up=rw_g_up, rw_ln_w=rw_ln_w, rw_ln_b=rw_ln_b,
             rw_w_proj=rw_w_proj, gl_gk_up=gl_gk_up, gl_gk_b=gl_gk_b, gl_norm_g=gl_norm_g, gl_w_proj=gl_w_proj,
             w_out=w_out, moe_wg1=moe_wg1, moe_bg1=moe_bg1, moe_wg2=moe_wg2, moe_bg2=moe_bg2, moe_w_gate=moe_w_gate,
             moe_w_up=moe_w_up, moe_w_down=moe_w_down)
    bsz, seq, d = x.shape
    ctx_len = ctx.shape[1]
    depth = w_mod.shape[0]
    cc = jnp.concatenate([c, c_ctx[None], jnp.zeros((8 - bsz - 1, d), F32)], axis=0)
    mod = _adaln(cc, w_mod, b_mod).reshape(depth, 8, 6, 1, d)
    xa = jnp.concatenate([ctx.reshape(bsz * ctx_len, d), x.reshape(bsz * seq, d)], axis=0)
    for i in range(depth):
        xa = _layer(i, i == depth - 1, xa, mod, bsz, seq, ctx_len, p)
    out = _final_norm(xa, g_final.reshape(1, d), 512, 0)
    return out.reshape(bsz, seq, d)
```

```python
import functools
import math

import jax
import jax.numpy as jnp
from jax import lax
from jax.experimental import pallas as pl
from jax.experimental.pallas import tpu as pltpu

F32 = jnp.float32
BF16 = jnp.bfloat16

D_MODEL = 2048
GRID_W = 64
NORM_EPS = 1e-6

S5_WIDTH = 1024
S5_GROUP = 16
S5_GROUPS = 64
S5_STATE = 64
S5_MAX_RE = -1e-4
S5_TILE = 16
S5_PAIRS = S5_GROUPS // 2

RW_WIDTH = 1024
RW_HEAD = 64
RW_DECAY_SCALE = 0.606531
RW_GN_EPS = 64e-5
RW_BLOCK = 256
RW_CHUNK = 16
RW_PAIRS = RW_WIDTH // 128
RW_Z = 3 * RW_WIDTH + 128

GLA_HEADS = 4
GLA_DK = 128
GLA_DV = 256
GLA_QK = 512
GLA_V = 1024
GLA_TAU = 16.0
GLA_CHUNK = 64

MOE_GROUPS = 4
MOE_PER_GROUP = 8
MOE_EXPERTS = 32
MOE_HIDDEN = 256
MOE_LANE0 = MOE_GROUPS

LANES = 128
VMEM_LIMIT = 56 * 1024 * 1024


def _cp(*sem):
    return pltpu.CompilerParams(dimension_semantics=sem, vmem_limit_bytes=VMEM_LIMIT)


def _dot(a, b):
    return jnp.dot(a, b, preferred_element_type=F32)


def _dot_nt(a, b):
    return lax.dot_general(a, b, (((1,), (1,)), ((), ())), preferred_element_type=F32)


def _dot_tn(a, b):
    return lax.dot_general(a, b, (((0,), (0,)), ((), ())), preferred_element_type=F32)


def _sigmoid(x):
    return 1.0 / (1.0 + jnp.exp(-x))


def _silu(x):
    return x * _sigmoid(x)


def _gelu_tanh(x):
    return 0.5 * x * (1.0 + jnp.tanh(math.sqrt(2.0 / math.pi) * (x + 0.044715 * (x * x * x))))


def _split_hi_lo(x):
    hi = x.astype(BF16)
    lo = (x - hi.astype(F32)).astype(BF16)
    return hi, lo


def _iota(shape, dim):
    return lax.broadcasted_iota(jnp.int32, shape, dim)


def _head_ones(width):
    return (_iota((LANES, LANES), 0) // width == _iota((LANES, LANES), 1) // width).astype(BF16)


def _head_sum(x, ones, exact):
    if exact:
        hi, lo = _split_hi_lo(x)
        return _dot(hi, ones) + _dot(lo, ones)
    return _dot(x.astype(BF16), ones)


def _adaln_kernel(c_ref, w_ref, b_ref, o_ref):
    c = c_ref[...]
    o_ref[...] = _dot(_silu(c).astype(BF16), w_ref[...].astype(BF16)) + b_ref[...]


def _adaln(cc, w_mod, b_mod):
    depth, d, n = w_mod.shape
    bn = 1536
    return pl.pallas_call(
        _adaln_kernel,
        out_shape=jax.ShapeDtypeStruct((depth, 8, n), F32),
        grid=(depth, n // bn),
        in_specs=[
            pl.BlockSpec((8, d), lambda l, j: (0, 0)),
            pl.BlockSpec((None, d, bn), lambda l, j: (l, 0, j)),
            pl.BlockSpec((None, 1, bn), lambda l, j: (l, 0, j)),
        ],
        out_specs=pl.BlockSpec((None, 8, bn), lambda l, j: (l, 0, j)),
        compiler_params=_cp("parallel", "parallel"),
        name="adaln",
    )(cc, w_mod, b_mod.reshape(depth, 1, n))


def _mod_spec(layer, part, bm, nc, seq):
    def index(i, *_):
        r0 = i * bm
        return (layer, jnp.where(r0 < nc, 2, (r0 - nc) // seq), part, 0, 0)
    return pl.BlockSpec((None, None, None, 1, D_MODEL), index)


def _normmod_kernel(x_ref, g_ref, sh_ref, sc_ref, o_ref):
    x = x_ref[...]
    y = x * lax.rsqrt(jnp.mean(x * x, axis=-1, keepdims=True) + NORM_EPS) * g_ref[...]
    o_ref[...] = (y * (1.0 + sc_ref[...]) + sh_ref[...]).astype(o_ref.dtype)


def _normmod(x, g, mod, layer, parts, bm, nc, seq):
    n, d = x.shape
    return pl.pallas_call(
        _normmod_kernel,
        out_shape=jax.ShapeDtypeStruct((n, d), BF16),
        grid=(n // bm,),
        in_specs=[
            pl.BlockSpec((bm, d), lambda i: (i, 0)),
            pl.BlockSpec((None, 1, d), lambda i: (layer, 0, 0)),
            _mod_spec(layer, parts[0], bm, nc, seq),
            _mod_spec(layer, parts[1], bm, nc, seq),
        ],
        out_specs=pl.BlockSpec((bm, d), lambda i: (i, 0)),
        compiler_params=_cp("parallel"),
        name="normmod",
    )(x, g, mod, mod)


def _mm_kernel(x_ref, w_ref, o_ref):
    o_ref[...] = _dot(x_ref[...], w_ref[...]).astype(o_ref.dtype)


def _mm(x, w, bm, bn, out_dtype=F32):
    m, k = x.shape
    n = w.shape[1]
    return pl.pallas_call(
        _mm_kernel,
        out_shape=jax.ShapeDtypeStruct((m, n), out_dtype),
        grid=(m // bm, n // bn),
        in_specs=[pl.BlockSpec((bm, k), lambda i, j: (i, 0)), pl.BlockSpec((k, bn), lambda i, j: (0, j))],
        out_specs=pl.BlockSpec((bm, bn), lambda i, j: (i, j)),
        compiler_params=_cp("parallel", "parallel"),
        name="mm",
    )(x, w)


def _seq_blocks(bsz, nc_blocks, nl_blocks, reverse):
    def local(s):
        lat = nc_blocks + ((nl_blocks - 1 - (s - nc_blocks)) if reverse else (s - nc_blocks))
        ctx = (nc_blocks - 1 - s) if reverse else s
        return jnp.where(s < nc_blocks, ctx, lat)

    def block(b, s):
        i = local(s)
        return jnp.where(i < nc_blocks, b * nc_blocks + i, bsz * nc_blocks + b * nl_blocks + i - nc_blocks)

    return local, block


S5_PACK = 8
S5_PACKS = S5_GROUPS // S5_PACK
S5_PLANE = S5_PACK * S5_STATE
S5_BLOCK = 256
S5_ROWS = 24


def _s5_tables(a_re, a_im, log_dt, b_re, b_im, c_re, c_im):
    lam = lax.complex(jnp.minimum(a_re, S5_MAX_RE), a_im)
    ldt = lam * jnp.exp(log_dt)[:, None]
    lam_bar = jnp.exp(ldt)
    b_bar = ((lam_bar - 1.0) / lam)[..., None] * lax.complex(b_re, b_im)
    c_mat = lax.complex(c_re, c_im)
    eye = jnp.eye(S5_PACK, dtype=F32)

    def block_diag(t):
        k, g, a, b = t.shape
        return (t[:, :, :, None, :] * eye[None, :, None, :, None]).reshape(k, g * a, g * b)

    b_t = jnp.transpose(b_bar, (0, 2, 1)).reshape(S5_PACKS, S5_PACK, S5_GROUP, S5_STATE)
    bblk = jnp.concatenate([block_diag(b_t.real), block_diag(b_t.imag)], axis=2)
    c_t = jnp.transpose(c_mat, (0, 2, 1)).reshape(S5_PACKS, S5_PACK, S5_STATE, S5_GROUP)
    cblk = jnp.concatenate([block_diag(c_t.real), block_diag(-c_t.imag)], axis=1)
    expo = jnp.concatenate([jnp.arange(1, S5_TILE + 1, dtype=F32), jnp.asarray([32.0, 64.0, 128.0], F32),
                            jnp.zeros((S5_ROWS - S5_TILE - 3,), F32)])
    pw = jnp.exp(ldt[None] * expo[:, None, None]).reshape(S5_ROWS, S5_PACKS, S5_PLANE)
    pw = jnp.transpose(pw, (1, 0, 2))
    return bblk.astype(BF16), cblk.astype(BF16), pw.real, pw.imag


def _s5_scan_kernel(u_ref, bblk_ref, cblk_ref, pre_ref, pim_ref, y_ref, car_ref, up_s, h_s, hb_s, yp_s, *, reverse):
    t = S5_TILE
    n_t = S5_BLOCK // t
    pn = S5_PLANE

    @pl.when(pl.program_id(1) == 0)
    def _():
        car_ref[...] = jnp.zeros_like(car_ref)

    ra = _iota((S5_BLOCK, S5_BLOCK), 0)
    cb = _iota((S5_BLOCK, S5_BLOCK), 1)
    perm = ((ra // t == cb % t) & (ra % t == cb // t)).astype(BF16)
    up = _dot(perm, u_ref[...].astype(BF16)).astype(BF16)
    for pk in range(S5_PACKS):
        up_s[pk] = up[:, LANES * pk:LANES * (pk + 1)]
    order = list(range(t - 1, -1, -1)) if reverse else list(range(t))
    rowj = _iota((n_t, 1), 0)

    def pack_body(pk, carry):
        bu = _dot(up_s[pk], bblk_ref[pk])
        pre = pre_ref[pk]
        pim = pim_ref[pk]
        l_re, l_im = pre[0:1], pim[0:1]
        h_re = h_im = None
        for n, s in enumerate(order):
            rows = slice(t * s, t * (s + 1))
            b_re, b_im = bu[rows, :pn], bu[rows, pn:]
            if n == 0:
                h_re, h_im = b_re, b_im
            else:
                h_re, h_im = l_re * h_re - l_im * h_im + b_re, l_re * h_im + l_im * h_re + b_im
            h_s[rows, :pn] = h_re
            h_s[rows, pn:] = h_im
        c_re, c_im = car_ref[pk, 0:1, :pn], car_ref[pk, 0:1, pn:]
        first = rowj == (n_t - 1 if reverse else 0)
        g_re, g_im = pre[t - 1:t], pim[t - 1:t]
        e_re = h_re + jnp.where(first, g_re * c_re - g_im * c_im, 0.0)
        e_im = h_im + jnp.where(first, g_re * c_im + g_im * c_re, 0.0)
        step = 1
        for row in (t - 1, t, t + 1, t + 2):
            if reverse:
                s_re, s_im, ok = pltpu.roll(e_re, n_t - step, 0), pltpu.roll(e_im, n_t - step, 0), rowj < n_t - step
            else:
                s_re, s_im, ok = pltpu.roll(e_re, step, 0), pltpu.roll(e_im, step, 0), rowj >= step
            a_re, a_im = pre[row:row + 1], pim[row:row + 1]
            e_re = e_re + jnp.where(ok, a_re * s_re - a_im * s_im, 0.0)
            e_im = e_im + jnp.where(ok, a_re * s_im + a_im * s_re, 0.0)
            step *= 2
        last = 0 if reverse else n_t - 1
        car_ref[pk, 0:1, :pn] = e_re[last:last + 1]
        car_ref[pk, 0:1, pn:] = e_im[last:last + 1]
        if reverse:
            in_re = jnp.where(first, c_re, pltpu.roll(e_re, n_t - 1, 0))
            in_im = jnp.where(first, c_im, pltpu.roll(e_im, n_t - 1, 0))
        else:
            in_re = jnp.where(first, c_re, pltpu.roll(e_re, 1, 0))
            in_im = jnp.where(first, c_im, pltpu.roll(e_im, 1, 0))
        for n, s in enumerate(order):
            rows = slice(t * s, t * (s + 1))
            a_re, a_im = pre[n:n + 1], pim[n:n + 1]
            hb_s[rows, :pn] = (h_s[rows, :pn] + a_re * in_re - a_im * in_im).astype(BF16)
            hb_s[rows, pn:] = (h_s[rows, pn:] + a_re * in_im + a_im * in_re).astype(BF16)
        yp_s[pk] = _dot(hb_s[...], cblk_ref[pk]).astype(BF16)
        return carry

    lax.fori_loop(0, S5_PACKS, pack_body, 0)
    for pk in range(S5_PACKS):
        y_ref[:, LANES * pk:LANES * (pk + 1)] = _dot(perm, yp_s[pk])


def _s5_scan(u_src, tables, bsz, nc_blocks, nl_blocks, reverse):
    bblk, cblk, pre, pim = tables
    rows = u_src.shape[0]
    blk = S5_BLOCK
    _, block = _seq_blocks(bsz, nc_blocks, nl_blocks, reverse)
    whole = lambda a: pl.BlockSpec(a.shape, lambda b, s: (0,) * a.ndim)
    return pl.pallas_call(
        functools.partial(_s5_scan_kernel, reverse=reverse),
        out_shape=jax.ShapeDtypeStruct((rows, S5_WIDTH), F32),
        grid=(bsz, nc_blocks + nl_blocks),
        in_specs=[pl.BlockSpec((blk, S5_WIDTH), lambda b, s: (block(b, s), 0)),
                  whole(bblk), whole(cblk), whole(pre), whole(pim)],
        out_specs=pl.BlockSpec((blk, S5_WIDTH), lambda b, s: (block(b, s), 0)),
        scratch_shapes=[
            pltpu.VMEM((S5_PACKS, 8, 2 * S5_PLANE), F32),
            pltpu.VMEM((S5_PACKS, blk, LANES), BF16),
            pltpu.VMEM((blk, 2 * S5_PLANE), F32),
            pltpu.VMEM((blk, 2 * S5_PLANE), BF16),
            pltpu.VMEM((S5_PACKS, blk, LANES), BF16),
        ],
        compiler_params=_cp("parallel", "arbitrary"),
        name="s5_bwd" if reverse else "s5_fwd",
    )(u_src, bblk, cblk, pre, pim)


def _gla_kernel(q_ref, k_ref, v_ref, gk_ref, up_ref, gb_ref, s0_ref, o_ref, sf_ref, st_ref):
    d = pl.program_id(0)
    c = pl.program_id(2)
    n = pl.num_programs(2)

    @pl.when(c == 0)
    def _():
        st_ref[...] = s0_ref[...]

    cs = GLA_CHUNK
    sign = 1 - 2 * d
    row = _iota((cs, cs), 0)
    col = _iota((cs, cs), 1)
    causal = (row - col) * sign >= 0
    causal_b = causal.astype(BF16)
    x = _dot(gk_ref[...].astype(BF16), up_ref[...]) + gb_ref[...]
    log_a = (jnp.minimum(x, 0.0) - jnp.log(1.0 + jnp.exp(-jnp.abs(x)))) * (1.0 / GLA_TAU)
    la_hi, la_lo = _split_hi_lo(log_a)
    bcum = _dot(causal_b, la_hi) + _dot(causal_b, la_lo)
    rid = _iota((cs, 1), 0)
    b_mid = jnp.sum(jnp.where(rid == cs // 2 - d, bcum, 0.0), axis=0, keepdims=True)
    b_end = jnp.sum(jnp.where(rid == (cs - 1) * (1 - d), bcum, 0.0), axis=0, keepdims=True)
    q = q_ref[...] * (GLA_DK ** -0.5)
    k = k_ref[...]
    v = v_ref[...]
    for h in range(GLA_HEADS):
        ks = slice(GLA_DK * h, GLA_DK * (h + 1))
        vs = slice(GLA_DV * h, GLA_DV * (h + 1))
        bh = bcum[:, ks]
        qh = q[:, ks]
        kh = k[:, ks]
        vh = v[:, vs].astype(BF16)
        s_t = st_ref[h]
        q_in = (qh * jnp.exp(bh - b_mid[:, ks])).astype(BF16)
        k_in = (kh * jnp.exp(b_mid[:, ks] - bh)).astype(BF16)
        scores = jnp.where(causal, _dot_nt(q_in, k_in), 0.0).astype(BF16)
        q_st = (qh * jnp.exp(bh)).astype(BF16)
        o_ref[:, vs] = _dot(scores, vh) + _dot_nt(q_st, s_t.astype(BF16))
        k_out = (kh * jnp.exp(b_end[:, ks] - bh)).astype(BF16)
        st_ref[h] = s_t * jnp.exp(b_end[:, ks]) + _dot_tn(vh, k_out)

    @pl.when(c == n - 1)
    def _():
        sf_ref[...] = st_ref[...]


def _gla(qk, v, gk, gk_up, gk_b, s0):
    bsz, cs, w = v.shape
    n = w // GLA_V

    def chunk(d, c):
        return c + d * (n - 1 - 2 * c)

    state_spec = pl.BlockSpec((None, None, GLA_HEADS, GLA_DV, GLA_DK), lambda d, b, c: (d, b, 0, 0, 0))
    return pl.pallas_call(
        _gla_kernel,
        out_shape=(jax.ShapeDtypeStruct((2, bsz, cs, w), F32), jax.ShapeDtypeStruct(s0.shape, F32)),
        grid=(2, bsz, n),
        in_specs=[
            pl.BlockSpec((None, cs, GLA_QK), lambda d, b, c: (b, 0, 2 * chunk(d, c))),
            pl.BlockSpec((None, cs, GLA_QK), lambda d, b, c: (b, 0, 2 * chunk(d, c) + 1)),
            pl.BlockSpec((None, cs, GLA_V), lambda d, b, c: (b, 0, chunk(d, c))),
            pl.BlockSpec((None, cs, LANES), lambda d, b, c: (b, 0, chunk(d, c))),
            pl.BlockSpec((None, LANES, GLA_QK), lambda d, b, c: (d, 0, 0)),
            pl.BlockSpec((None, 1, GLA_QK), lambda d, b, c: (d, 0, 0)),
            state_spec,
        ],
        out_specs=(pl.BlockSpec((None, None, cs, GLA_V), lambda d, b, c: (d, b, 0, chunk(d, c))), state_spec),
        scratch_shapes=[pltpu.VMEM((GLA_HEADS, GLA_DV, GLA_DK), F32)],
        compiler_params=_cp("parallel", "parallel", "arbitrary"),
        name="gla",
    )(qk, qk, v, gk, gk_up, gk_b, s0)


def _to_chunk_major(t, bsz):
    n = t.shape[0] // bsz
    d = t.shape[1]
    t = t.reshape(bsz, n // GLA_CHUNK, GLA_CHUNK, d)
    return jnp.transpose(t, (0, 2, 1, 3)).reshape(bsz, GLA_CHUNK, (n // GLA_CHUNK) * d)


def _from_chunk_major(t, d):
    lead = t.shape[:-2]
    n = t.shape[-1] // d
    t = t.reshape(lead + (GLA_CHUNK, n, d))
    return jnp.swapaxes(t, -3, -2).reshape(lead + (n * GLA_CHUNK, d))


def _rwkv_kernel(z0_ref, z1_ref, halo0_ref, halo1_ref, mu_ref, w0_ref, wup_ref, a0_ref, aup_ref, kk_ref, ka_ref,
                 rk_ref, y_ref, st_ref, kt_s, bt_s, kq_s, rt_s, v_s, w_s, u_s, ya_s, ab_s, pin_s,
                 *, reverse, nc_blocks):
    step = pl.program_id(0)
    blk = RW_BLOCK
    ch = RW_CHUNK
    n_ch = blk // ch
    n_b = 2
    n_units = n_b * RW_PAIRS

    @pl.when(step == 0)
    def _():
        st_ref[...] = jnp.zeros_like(st_ref)

    rowi = _iota((blk, 1), 0)
    low = _iota((1, LANES), 1) < RW_HEAD
    ones64 = _head_ones(RW_HEAD)
    pos = rowi % ch
    seq_start = (step == 0) | (step == nc_blocks)

    for b, (z_ref, halo_ref) in enumerate(((z0_ref, halo0_ref), (z1_ref, halo1_ref))):
        z = z_ref[...]
        if reverse:
            prev = pltpu.roll(z, blk - 1, 0)
            edge = halo_ref[0:1, :]
            at_edge = rowi == blk - 1
        else:
            prev = pltpu.roll(z, 1, 0)
            edge = halo_ref[7:8, :]
            at_edge = rowi == 0
        prev = jnp.where(at_edge, jnp.where(seq_start, 0.0, edge), prev)
        zs = z + (prev - z) * mu_ref[...]
        r = zs[:, 0:RW_WIDTH]
        k = zs[:, RW_WIDTH:2 * RW_WIDTH]
        v = zs[:, 2 * RW_WIDTH:3 * RW_WIDTH]
        lora = zs[:, 3 * RW_WIDTH:3 * RW_WIDTH + LANES]
        lora_w = jnp.where(low, jnp.tanh(lora), 0.0).astype(BF16)
        lora_a = jnp.where(low, 0.0, lora).astype(BF16)
        logw = -RW_DECAY_SCALE * _sigmoid(w0_ref[...] + _dot(lora_w, wup_ref[...]))
        a = _sigmoid(a0_ref[...] + _dot(lora_a, aup_ref[...]))
        kk = k * kk_ref[...]
        kp = k * (1.0 + (a - 1.0) * ka_ref[...])
        rkb = r * kp * rk_ref[...]

        cl = logw
        sh = 1
        while sh < ch:
            if reverse:
                cl = cl + jnp.where(pos < ch - sh, pltpu.roll(cl, blk - sh, 0), 0.0)
            else:
                cl = cl + jnp.where(pos >= sh, pltpu.roll(cl, sh, 0), 0.0)
            sh *= 2
        p_in = jnp.exp(cl)
        pin_s[b] = p_in
        p_ex = jnp.exp(cl - logw)
        p_inv = jnp.exp(-cl)

        for p in range(RW_PAIRS):
            ls = slice(LANES * p, LANES * (p + 1))
            q = b * RW_PAIRS + p
            kkp = kk[:, ls]
            ssq = _head_sum(kkp * kkp, ones64, False)
            kkn = kkp * (1.0 / jnp.maximum(jnp.sqrt(ssq), 1e-12))
            kt_s[q] = (kkn * p_ex[:, ls]).astype(BF16)
            bt_s[q] = (kkn * a[:, ls] * p_inv[:, ls]).astype(BF16)
            kq_s[q] = (kp[:, ls] * p_inv[:, ls]).astype(BF16)
            rt_s[q] = (r[:, ls] * p_in[:, ls]).astype(BF16)
            v_s[q] = v[:, ls].astype(BF16)
            ya_s[q] = _head_sum(rkb[:, ls], ones64, False) * v[:, ls]

    rr = _iota((blk, blk), 0)
    cc = _iota((blk, blk), 1)
    same = rr // ch == cc // ch
    before = (cc > rr) if reverse else (cc < rr)
    strict = same & before
    incl = same & (before | (rr == cc))
    fold0 = (_iota((blk, LANES), 0) % ch == _iota((blk, LANES), 1)).astype(BF16)
    fold1 = (_iota((blk, LANES), 0) % ch + ch == _iota((blk, LANES), 1)).astype(BF16)
    lane_lo = _iota((1, LANES), 1) < RW_HEAD

    def pair_body(p, carry):
        units = [b * RW_PAIRS + p for b in range(n_b)]
        heads = [(u, hh) for u in range(n_b) for hh in range(2)]
        kt = [kt_s[q] for q in units]
        rt = [rt_s[q] for q in units]
        vv = [v_s[q] for q in units]
        ya = [ya_s[q] for q in units]
        rhs = [jnp.concatenate([bt_s[q], kq_s[q]], axis=0) for q in units]
        g = []
        for u, hh in heads:
            mine = lane_lo if hh == 0 else jnp.logical_not(lane_lo)
            zero = jnp.zeros_like(kt[u])
            lhs = jnp.concatenate([jnp.where(mine, kt[u], zero), jnp.where(mine, rt[u], zero)], axis=0)
            g.append(_dot_nt(lhs, rhs[u]))
        n1 = [jnp.where(strict, -gh[:blk, :blk], 0.0).astype(BF16) for gh in g]
        a_kq = [jnp.where(strict, gh[:blk, blk:], 0.0).astype(BF16) for gh in g]
        a_rb = [jnp.where(incl, gh[blk:, :blk], 0.0).astype(BF16) for gh in g]
        a_rq = [jnp.where(incl, gh[blk:, blk:], 0.0).astype(BF16) for gh in g]
        n2 = [_dot(n, n).astype(BF16) for n in n1]
        akv = [_dot(a, vv[u]) for a, (u, _) in zip(a_kq, heads)]
        n4 = [_dot(n, n).astype(BF16) for n in n2]
        y_in = [_dot(a, vv[u]) for a, (u, _) in zip(a_rq, heads)]
        n8 = [_dot(n, n).astype(BF16) for n in n4]
        fold = [_dot(a, fold0 if hh == 0 else fold1) for a, (_, hh) in zip(a_rb, heads)]
        rhs_t = [jnp.concatenate([kt[u].astype(F32), av], axis=1) for av, (u, _) in zip(akv, heads)]
        for nk in (n8, n4, n2, n1):
            rhs_t = [x + _dot(n, x.astype(BF16)) for n, x in zip(nk, rhs_t)]
        for u, q in enumerate(units):
            h0, h1 = 2 * u, 2 * u + 1
            w_s[q] = jnp.where(lane_lo, rhs_t[h0][:, :LANES], rhs_t[h1][:, :LANES]).astype(BF16)
            u_s[q] = jnp.where(lane_lo, rhs_t[h0][:, LANES:], rhs_t[h1][:, LANES:])
            ya_s[q] = ya[u] + jnp.where(lane_lo, y_in[h0], y_in[h1])
            ab_s[q] = (fold[h0] + fold[h1]).astype(BF16)
        return carry

    lax.fori_loop(0, RW_PAIRS, pair_body, 0)

    blockdiag = (_iota((LANES, LANES), 0) // RW_HEAD) == (_iota((LANES, LANES), 1) // RW_HEAD)
    end_row = 0 if reverse else ch - 1

    def chunk_body(i, carry):
        c = (n_ch - 1 - i) if reverse else i
        rows = pl.ds(pl.multiple_of(c * ch, ch), ch)
        s_old = [st_ref[q] for q in range(n_units)]
        m1 = [_dot_nt(jnp.concatenate([w_s[q, rows, :], rt_s[q, rows, :]], axis=0), s_old[q].astype(BF16))
              for q in range(n_units)]
        zc = [-(m1[q][:ch] + u_s[q, rows, :]) for q in range(n_units)]
        upd = []
        for q in range(n_units):
            zv = jnp.concatenate([zc[q].astype(BF16), v_s[q, rows, :]], axis=0)
            bk = jnp.concatenate([bt_s[q, rows, :], kq_s[q, rows, :]], axis=0)
            upd.append(_dot_tn(zv, bk))
        yc = []
        for q in range(n_units):
            z2 = jnp.concatenate([jnp.where(lane_lo, zc[q], 0.0), jnp.where(lane_lo, 0.0, zc[q])], axis=0)
            yc.append(m1[q][ch:] + _dot(ab_s[q, rows, :][:, :2 * ch], z2.astype(BF16)) + ya_s[q, rows, :])
        for q in range(n_units):
            b, p = divmod(q, RW_PAIRS)
            p_end = pin_s[b, rows, LANES * p:LANES * (p + 1)][end_row:end_row + 1]
            st_ref[q] = (s_old[q] + jnp.where(blockdiag, upd[q], 0.0)) * p_end
            y_ref[b, rows, LANES * p:LANES * (p + 1)] = yc[q]
        return carry

    lax.fori_loop(0, n_ch, chunk_body, 0)


def _rwkv_direction(zall, prm, bsz, nc_blocks, nl_blocks, reverse):
    assert bsz == 2
    mu, w0, wup, a0, aup, k_k, k_a, r_k = prm
    rows = zall.shape[0]
    blk = RW_BLOCK
    steps = nc_blocks + nl_blocks
    n_blocks = rows // blk

    def local(s):
        lat = nc_blocks + ((nl_blocks - 1 - (s - nc_blocks)) if reverse else (s - nc_blocks))
        ctx = (nc_blocks - 1 - s) if reverse else s
        return jnp.where(s < nc_blocks, ctx, lat)

    def block(b, s):
        i = local(s)
        return jnp.where(i < nc_blocks, b * nc_blocks + i, bsz * nc_blocks + b * nl_blocks + i - nc_blocks)

    def halo(b, s):
        i = block(b, s)
        if reverse:
            return jnp.minimum((i + 1) * (blk // 8), n_blocks * (blk // 8) - 1)
        return jnp.maximum(i * (blk // 8) - 1, 0)

    vec = lambda w: pl.BlockSpec((1, w), lambda s: (0, 0))
    mat = lambda: pl.BlockSpec((LANES, RW_WIDTH), lambda s: (0, 0))
    n_units = bsz * RW_PAIRS
    unit_bf = pltpu.VMEM((n_units, blk, LANES), BF16)
    unit_f = pltpu.VMEM((n_units, blk, LANES), F32)
    z_spec = lambda b: pl.BlockSpec((blk, RW_Z), lambda s: (block(b, s), 0))
    halo_spec = lambda b: pl.BlockSpec((8, RW_Z), lambda s: (halo(b, s), 0))
    return pl.pallas_call(
        functools.partial(_rwkv_kernel, reverse=reverse, nc_blocks=nc_blocks),
        out_shape=jax.ShapeDtypeStruct((bsz, steps * blk, RW_WIDTH), F32),
        grid=(steps,),
        in_specs=[
            z_spec(0), z_spec(1), halo_spec(0), halo_spec(1),
            vec(RW_Z), vec(RW_WIDTH), mat(), vec(RW_WIDTH), mat(), vec(RW_WIDTH), vec(RW_WIDTH), vec(RW_WIDTH),
        ],
        out_specs=pl.BlockSpec((bsz, blk, RW_WIDTH), lambda s: (0, local(s), 0)),
        scratch_shapes=[
            pltpu.VMEM((n_units, LANES, LANES), F32),
            unit_bf, unit_bf, unit_bf, unit_bf, unit_bf,
            unit_bf, unit_f, unit_f, unit_bf,
            pltpu.VMEM((bsz, blk, RW_WIDTH), F32),
        ],
        compiler_params=_cp("arbitrary"),
        name="rwkv_bwd" if reverse else "rwkv_fwd",
    )(zall, zall, zall, zall, mu, w0, wup, a0, aup, k_k, k_a, r_k)


def _s5_out_kernel(yf_ref, yb_ref, u_ref, d_ref, wv_ref, wg_ref, o_ref, z_s):
    @pl.when(pl.program_id(1) == 0)
    def _():
        z_s[...] = _gelu_tanh(yf_ref[...] + yb_ref[...] + d_ref[...] * u_ref[...]).astype(BF16)

    zz = z_s[...]
    o_ref[...] = _dot(zz, wv_ref[...]) * _sigmoid(_dot(zz, wg_ref[...]))


def _s5_out(yf, yb, u_src, u_blk, d, w_glu, bm, row0):
    n = yf.shape[0] - row0 * bm
    bn = 1024
    nj = D_MODEL // bn
    return pl.pallas_call(
        _s5_out_kernel,
        out_shape=jax.ShapeDtypeStruct((n, D_MODEL), F32),
        grid=(n // bm, nj),
        in_specs=[
            pl.BlockSpec((bm, S5_WIDTH), lambda i, j: (i + row0, 0)),
            pl.BlockSpec((bm, S5_WIDTH), lambda i, j: (i + row0, 0)),
            pl.BlockSpec((bm, S5_WIDTH), lambda i, j: (i + row0, u_blk)),
            pl.BlockSpec((1, S5_WIDTH), lambda i, j: (0, 0)),
            pl.BlockSpec((S5_WIDTH, bn), lambda i, j: (0, j)),
            pl.BlockSpec((S5_WIDTH, bn), lambda i, j: (0, j + nj)),
        ],
        out_specs=pl.BlockSpec((bm, bn), lambda i, j: (i, j)),
        scratch_shapes=[pltpu.VMEM((bm, S5_WIDTH), BF16)],
        compiler_params=_cp("parallel", "arbitrary"),
        name="s5_out",
    )(yf, yb, u_src, d, w_glu, w_glu)


def _rwkv_out_kernel(yf_ref, yb_ref, g_ref, gup_ref, lw_ref, lb_ref, w_ref, o_ref, z_s):
    @pl.when(pl.program_id(1) == 0)
    def _():
        ones64 = _head_ones(RW_HEAD)
        gate = _dot(_sigmoid(g_ref[...]).astype(BF16), gup_ref[...])
        for p in range(RW_PAIRS):
            ls = slice(LANES * p, LANES * (p + 1))
            y = yf_ref[:, ls] + yb_ref[:, ls]
            mean = _head_sum(y, ones64, True) * (1.0 / RW_HEAD)
            yc = y - mean
            var = _head_sum(yc * yc, ones64, True) * (1.0 / RW_HEAD)
            yn = yc * lax.rsqrt(var + RW_GN_EPS) * lw_ref[:, ls] + lb_ref[:, ls]
            z_s[:, ls] = (yn * gate[:, ls]).astype(BF16)

    o_ref[...] = _dot(z_s[...], w_ref[...])


def _rwkv_out(yf, yb, zall, g_up, ln_w, ln_b, w_proj, row0, nc_blocks, nl_blocks):
    bm = RW_BLOCK
    bsz = yf.shape[0]
    n = zall.shape[0] - row0 * bm
    bn = 1024
    rows = lambda w, blk=0: pl.BlockSpec((bm, w), lambda i, j: (i + row0, blk))

    def y_index(i, j):
        i = i + row0
        t = i - bsz * nc_blocks
        return (jnp.where(t < 0, i // nc_blocks, t // nl_blocks),
                jnp.where(t < 0, i % nc_blocks, nc_blocks + t % nl_blocks), 0)

    y_spec = pl.BlockSpec((None, bm, RW_WIDTH), y_index)
    vec = pl.BlockSpec((1, RW_WIDTH), lambda i, j: (0, 0))
    return pl.pallas_call(
        _rwkv_out_kernel,
        out_shape=jax.ShapeDtypeStruct((n, D_MODEL), F32),
        grid=(n // bm, D_MODEL // bn),
        in_specs=[y_spec, y_spec, rows(LANES, RW_Z // LANES),
                  pl.BlockSpec((LANES, RW_WIDTH), lambda i, j: (0, 0)), vec, vec,
                  pl.BlockSpec((RW_WIDTH, bn), lambda i, j: (0, j))],
        out_specs=pl.BlockSpec((bm, bn), lambda i, j: (i, j)),
        scratch_shapes=[pltpu.VMEM((bm, RW_WIDTH), BF16)],
        compiler_params=_cp("parallel", "arbitrary"),
        name="rwkv_out",
    )(yf, yb, zall, g_up, ln_w, ln_b, w_proj)


def _gla_out_kernel(of_ref, ob_ref, og_ref, g_ref, w_ref, o_ref, z_s):
    @pl.when(pl.program_id(1) == 0)
    def _():
        for h in range(GLA_HEADS):
            vs = slice(GLA_DV * h, GLA_DV * (h + 1))
            o = of_ref[:, vs] + ob_ref[:, vs]
            on = o * lax.rsqrt(jnp.mean(o * o, axis=-1, keepdims=True) + NORM_EPS) * g_ref[:, vs]
            z_s[:, vs] = (on * _silu(og_ref[:, vs])).astype(BF16)

    o_ref[...] = _dot(z_s[...], w_ref[...])


def _gla_out(of, ob, og_src, og_blk, norm_g, w_proj, bm, row0):
    n = of.shape[0] - row0 * bm
    bn = 1024
    rows = lambda blk=0: pl.BlockSpec((bm, GLA_V), lambda i, j: (i + row0, blk))
    return pl.pallas_call(
        _gla_out_kernel,
        out_shape=jax.ShapeDtypeStruct((n, D_MODEL), F32),
        grid=(n // bm, D_MODEL // bn),
        in_specs=[rows(), rows(), rows(og_blk), pl.BlockSpec((1, GLA_V), lambda i, j: (0, 0)),
                  pl.BlockSpec((GLA_V, bn), lambda i, j: (0, j))],
        out_specs=pl.BlockSpec((bm, bn), lambda i, j: (i, j)),
        scratch_shapes=[pltpu.VMEM((bm, GLA_V), BF16)],
        compiler_params=_cp("parallel", "arbitrary"),
        name="gla_out",
    )(of, ob, og_src, norm_g, w_proj)


def _merge_kernel(ga_ref, gb_ref, gc_ref, pa_ref, pb_ref, pc_ref, w_ref, x_ref, gate_ref, o_ref, z_s):
    @pl.when(pl.program_id(1) == 0)
    def _():
        m = (_sigmoid(ga_ref[...]) * pa_ref[...] + _sigmoid(gb_ref[...]) * pb_ref[...]
             + _sigmoid(gc_ref[...]) * pc_ref[...])
        z_s[...] = m.astype(BF16)

    o_ref[...] = x_ref[...] + gate_ref[...] * _dot(z_s[...], w_ref[...])


def _merge(gates_src, gate_blk0, pa, pb, pc, w_out, x, mod, layer, bm, row0, nc, seq):
    n = pa.shape[0]
    bn = 1024
    d = D_MODEL
    nb = d // bn
    gate = lambda t: pl.BlockSpec((bm, d), lambda i, j: (i + row0, gate_blk0 + t))
    branch = pl.BlockSpec((bm, d), lambda i, j: (i, 0))
    mod_index = _mod_spec(layer, 2, bm, nc, seq).index_map
    shifted = pl.BlockSpec((None, None, None, 1, bn), lambda i, j: mod_index(i + row0)[:4] + (j,))
    return pl.pallas_call(
        _merge_kernel,
        out_shape=jax.ShapeDtypeStruct((n, d), F32),
        grid=(n // bm, nb),
        in_specs=[gate(0), gate(1), gate(2), branch, branch, branch,
                  pl.BlockSpec((d, bn), lambda i, j: (0, j)),
                  pl.BlockSpec((bm, bn), lambda i, j: (i + row0, j)), shifted],
        out_specs=pl.BlockSpec((bm, bn), lambda i, j: (i, j)),
        scratch_shapes=[pltpu.VMEM((bm, d), BF16)],
        compiler_params=_cp("parallel", "arbitrary"),
        name="merge",
    )(gates_src, gates_src, gates_src, pa, pb, pc, w_out, x, mod)


MOE_TM = 256
META_E1, META_E2, META_R1, META_R2, META_W1, META_W2 = range(6)


def _router_kernel(x_ref, g_ref, sh_ref, sc_ref, wr_hi_ref, wr_lo_ref, br_ref, v_ref, meta_ref, meta_t_ref, cnt_ref,
                   base_s):
    @pl.when(pl.program_id(0) == 0)
    def _():
        base_s[...] = jnp.zeros_like(base_s)

    x = x_ref[...]
    y = x * lax.rsqrt(jnp.mean(x * x, axis=-1, keepdims=True) + NORM_EPS) * g_ref[...]
    t = y * (1.0 + sc_ref[...]) + sh_ref[...]
    v_ref[...] = t
    t_hi, t_lo = _split_hi_lo(t)
    logits = (_dot(t_hi, wr_hi_ref[...]) + _dot(t_lo, wr_hi_ref[...]) + _dot(t_hi, wr_lo_ref[...])) + br_ref[...]
    lane = _iota(logits.shape, 1).astype(F32)
    neg = jnp.float32(-jnp.inf)
    big = jnp.float32(LANES)
    l1 = jnp.where(lane < MOE_GROUPS, logits, neg)
    m1 = jnp.max(l1, axis=-1, keepdims=True)
    p_top = 1.0 / jnp.sum(jnp.exp(l1 - m1), axis=-1, keepdims=True)
    grp = jnp.min(jnp.where(l1 == m1, lane, big), axis=-1, keepdims=True)
    lo = MOE_LANE0 + MOE_PER_GROUP * grp
    in_grp = (lane >= lo) & (lane < lo + MOE_PER_GROUP)
    l2 = jnp.where(in_grp, logits, neg)
    v1 = jnp.max(l2, axis=-1, keepdims=True)
    i1 = jnp.min(jnp.where(l2 == v1, lane, big), axis=-1, keepdims=True)
    l3 = jnp.where(lane == i1, neg, l2)
    v2 = jnp.max(l3, axis=-1, keepdims=True)
    i2 = jnp.min(jnp.where(l3 == v2, lane, big), axis=-1, keepdims=True)
    e2 = jnp.exp(v2 - v1)
    w1 = p_top / (1.0 + e2)
    w2 = p_top * e2 / (1.0 + e2)
    pick1 = lane == i1
    pick2 = lane == i2
    chosen = jnp.where(pick1 | pick2, 1.0, 0.0)
    bm = x.shape[0]
    earlier = (_iota((bm, bm), 1) < _iota((bm, bm), 0)).astype(BF16)
    before = _dot(earlier, chosen.astype(BF16)) + base_s[...]
    r1 = jnp.sum(jnp.where(pick1, before, 0.0), axis=-1, keepdims=True)
    r2 = jnp.sum(jnp.where(pick2, before, 0.0), axis=-1, keepdims=True)
    base_s[...] += jnp.sum(chosen, axis=0, keepdims=True)
    cnt_ref[...] = base_s[...]
    meta = jnp.zeros_like(logits)
    for slot, val in ((META_E1, i1 - MOE_LANE0), (META_E2, i2 - MOE_LANE0), (META_R1, r1), (META_R2, r2),
                      (META_W1, w1), (META_W2, w2)):
        meta = jnp.where(lane == slot, val, meta)
    meta_ref[...] = meta
    meta_t_ref[...] = meta.T[:8]


def _router(x, g, mod, layer, wr_hi, wr_lo, br, bm, nc, seq):
    n, d = x.shape
    return pl.pallas_call(
        _router_kernel,
        out_shape=(jax.ShapeDtypeStruct((n, d), F32), jax.ShapeDtypeStruct((n, LANES), F32),
                   jax.ShapeDtypeStruct((8, n), F32), jax.ShapeDtypeStruct((1, LANES), F32)),
        grid=(n // bm,),
        in_specs=[
            pl.BlockSpec((bm, d), lambda i: (i, 0)),
            pl.BlockSpec((None, 1, d), lambda i: (layer, 0, 0)),
            _mod_spec(layer, 3, bm, nc, seq), _mod_spec(layer, 4, bm, nc, seq),
            pl.BlockSpec((d, LANES), lambda i: (0, 0)), pl.BlockSpec((d, LANES), lambda i: (0, 0)),
            pl.BlockSpec((1, LANES), lambda i: (0, 0)),
        ],
        out_specs=(pl.BlockSpec((bm, d), lambda i: (i, 0)), pl.BlockSpec((bm, LANES), lambda i: (i, 0)),
                   pl.BlockSpec((8, bm), lambda i: (0, i)), pl.BlockSpec((1, LANES), lambda i: (0, 0))),
        scratch_shapes=[pltpu.VMEM((1, LANES), F32)],
        compiler_params=_cp("arbitrary"),
        name="moe_router",
    )(x, g, mod, mod, wr_hi, wr_lo, br)


def _moe_plan(meta_t, cnt):
    tm = MOE_TM
    n_tok = meta_t.shape[1]
    counts = cnt[0, MOE_LANE0:MOE_LANE0 + MOE_EXPERTS].astype(jnp.int32)
    seg = ((counts + tm - 1) // tm) * tm
    ends = jnp.cumsum(seg)
    off = ends - seg
    rec = meta_t[:4].astype(jnp.int32)
    dest = jnp.take(off, rec[:2], axis=0) + rec[2:4]
    n_rows = 2 * n_tok + MOE_EXPERTS * tm
    n_tiles = n_rows // tm
    tile_e = jnp.sum(((jnp.arange(n_tiles) * tm)[:, None] >= ends[None, :]).astype(jnp.int32), axis=1)
    tile_e = jnp.minimum(tile_e, MOE_EXPERTS - 1)
    tok = jnp.broadcast_to(jnp.arange(n_tok, dtype=jnp.int32), (2, n_tok))
    src = jnp.zeros((n_rows,), jnp.int32).at[dest.reshape(-1)].set(tok.reshape(-1))
    return dest[0], dest[1], src, tile_e, (ends[-1] // tm).reshape(1)


def _row_copy(src_hbm, row, dst, slot, r, sem):
    return pltpu.make_async_copy(src_hbm.at[pl.ds(row, 1)], dst.at[slot, pl.ds(r, 1)], sem)


def _experts_kernel(src_ref, te_ref, nu_ref, v_hbm, wg_ref, wu_ref, wd_ref, y_ref, xbuf, sem):
    del te_ref
    i = pl.program_id(0)
    n_used = nu_ref[0]
    slot = i % 2

    def gather(tile, into):
        for r in range(MOE_TM):
            _row_copy(v_hbm, src_ref[tile * MOE_TM + r], xbuf, into, r, sem.at[into]).start()

    def expert():
        for r in range(MOE_TM):
            _row_copy(v_hbm, 0, xbuf, slot, r, sem.at[slot]).wait()
        t = xbuf[slot].astype(BF16)
        hid = _silu(_dot(t, wg_ref[...].astype(BF16))) * _dot(t, wu_ref[...].astype(BF16))
        y_ref[...] = _dot(hid.astype(BF16), wd_ref[...].astype(BF16))

    @pl.when((i == 0) & (n_used > 0))
    def _():
        gather(0, 0)

    @pl.when(i + 1 < n_used)
    def _():
        gather(i + 1, 1 - slot)
        expert()

    @pl.when(i + 1 == n_used)
    def _():
        expert()

    @pl.when(i >= n_used)
    def _():
        y_ref[...] = jnp.zeros_like(y_ref)


def _experts(v, src, tile_e, n_used, w_gate, w_up, w_down):
    d = v.shape[1]
    hdim = w_gate.shape[2]
    n_rows = src.shape[0]
    tm = MOE_TM
    by_expert = lambda i, src_r, te_r, nu_r: (te_r[i], 0, 0)
    return pl.pallas_call(
        _experts_kernel,
        out_shape=jax.ShapeDtypeStruct((n_rows, d), F32),
        grid_spec=pltpu.PrefetchScalarGridSpec(
            num_scalar_prefetch=3,
            grid=(n_rows // tm,),
            in_specs=[
                pl.BlockSpec(memory_space=pl.ANY),
                pl.BlockSpec((None, d, hdim), by_expert),
                pl.BlockSpec((None, d, hdim), by_expert),
                pl.BlockSpec((None, hdim, d), by_expert),
            ],
            out_specs=pl.BlockSpec((tm, d), lambda i, *_: (i, 0)),
            scratch_shapes=[pltpu.VMEM((2, tm, d), F32), pltpu.SemaphoreType.DMA((2,))],
        ),
        compiler_params=_cp("arbitrary"),
        name="moe_experts",
    )(src, tile_e, n_used, v, w_gate, w_up, w_down)


def _combine_kernel(d1_ref, d2_ref, y_hbm, meta_ref, x_ref, gate_ref, o_ref, buf1, buf2, sem):
    i = pl.program_id(0)
    bm = x_ref.shape[0]
    slot = i % 2

    def gather(tile, into):
        for r in range(bm):
            t = tile * bm + r
            _row_copy(y_hbm, d1_ref[t], buf1, into, r, sem.at[0, into]).start()
            _row_copy(y_hbm, d2_ref[t], buf2, into, r, sem.at[1, into]).start()

    @pl.when(i == 0)
    def _():
        gather(0, 0)

    @pl.when(i + 1 < pl.num_programs(0))
    def _():
        gather(i + 1, 1 - slot)

    for r in range(bm):
        _row_copy(y_hbm, 0, buf1, slot, r, sem.at[0, slot]).wait()
        _row_copy(y_hbm, 0, buf2, slot, r, sem.at[1, slot]).wait()
    meta = meta_ref[...]
    lane = _iota(meta.shape, 1)
    w1 = jnp.sum(jnp.where(lane == META_W1, meta, 0.0), axis=-1, keepdims=True)
    w2 = jnp.sum(jnp.where(lane == META_W2, meta, 0.0), axis=-1, keepdims=True)
    o_ref[...] = x_ref[...] + gate_ref[...] * (w1 * buf1[slot] + w2 * buf2[slot])


def _combine(y, d1, d2, meta, x, mod, layer, bm, nc, seq):
    n, d = x.shape
    at_tile = lambda i, *_: (i, 0)
    mod_index = _mod_spec(layer, 5, bm, nc, seq).index_map
    return pl.pallas_call(
        _combine_kernel,
        out_shape=jax.ShapeDtypeStruct((n, d), F32),
        grid_spec=pltpu.PrefetchScalarGridSpec(
            num_scalar_prefetch=2,
            grid=(n // bm,),
            in_specs=[
                pl.BlockSpec(memory_space=pl.ANY),
                pl.BlockSpec((bm, LANES), at_tile),
                pl.BlockSpec((bm, d), at_tile),
                pl.BlockSpec((None, None, None, 1, d), lambda i, *_: mod_index(i)),
            ],
            out_specs=pl.BlockSpec((bm, d), at_tile),
            scratch_shapes=[pltpu.VMEM((2, bm, d), F32), pltpu.VMEM((2, bm, d), F32),
                            pltpu.SemaphoreType.DMA((2, 2))],
        ),
        compiler_params=_cp("arbitrary"),
        name="moe_combine",
    )(d1, d2, y, meta, x, mod)


def _final_norm_kernel(x_ref, g_ref, o_ref):
    x = x_ref[...]
    o_ref[...] = x * lax.rsqrt(jnp.mean(x * x, axis=-1, keepdims=True) + NORM_EPS) * g_ref[...]


def _final_norm(x, g, bm, row0):
    n, d = x.shape
    n -= row0 * bm
    return pl.pallas_call(
        _final_norm_kernel,
        out_shape=jax.ShapeDtypeStruct((n, d), F32),
        grid=(n // bm,),
        in_specs=[pl.BlockSpec((bm, d), lambda i: (i + row0, 0)), pl.BlockSpec((1, d), lambda i: (0, 0))],
        out_specs=pl.BlockSpec((bm, d), lambda i: (i, 0)),
        compiler_params=_cp("parallel"),
        name="final_norm",
    )(x, g)


_COL = dict(s5=0, rw=S5_WIDTH, rg=S5_WIDTH + 3 * RW_WIDTH + 128, q=4352, k=4864, v=5376, gk=6400, og=6416, gates=7440)


def _pad_rows(w, rows):
    return jnp.pad(w, ((0, rows - w.shape[0]), (0, 0)))


def _layer(i, last, xa, mod, bsz, seq, ctx_len, p):
    nc = bsz * ctx_len
    rows = xa.shape[0]
    bm = 512
    msel = dict(nc=nc, seq=seq)
    u = _normmod(xa, p['g_norm1'].reshape(-1, 1, D_MODEL), mod, i, (0, 1), bm, **msel)

    w_in = p['w_in'][i]
    col = lambda a, w: w_in[:, a:a + w].astype(BF16)
    bm_in = 1088 if rows % 1088 == 0 else bm
    w_a = jnp.concatenate([col(0, S5_WIDTH), col(_COL['og'], GLA_V), col(_COL['gates'], 3 * D_MODEL)], axis=1)
    w_z = jnp.concatenate([col(_COL['rw'], 3 * RW_WIDTH + 128), col(_COL['rg'], 128)], axis=1)
    w_qk = col(_COL['q'], 2 * GLA_QK)
    w_v = col(_COL['v'], GLA_V)
    w_gk = jnp.pad(col(_COL['gk'], 16), ((0, 0), (0, LANES - 16)))
    pa_in = _mm(u, w_a, bm_in, 1024)
    zall = _mm(u, w_z, bm_in, 1664)
    qk = _mm(u, w_qk, bm_in, 1024)
    vv = _mm(u, w_v, bm_in, 1024)
    gk = _mm(u, w_gk, bm_in, LANES)

    ya = []
    for d in range(2):
        tables = _s5_tables(*(p[k][i, d] for k in ('s5_a_re', 's5_a_im', 's5_log_dt', 's5_b_re', 's5_b_im',
                                                   's5_c_re', 's5_c_im')))
        ya.append(_s5_scan(pa_in, tables, bsz, ctx_len // S5_BLOCK, seq // S5_BLOCK, bool(d)))

    yb = []
    for d in range(2):
        prm = (
            jnp.pad(p['rw_mu'][i, d], (0, RW_Z - p['rw_mu'].shape[-1])).reshape(1, RW_Z),
            p['rw_w0'][i, d].reshape(1, -1),
            _pad_rows(p['rw_w_up'][i, d], LANES).astype(BF16),
            p['rw_a0'][i, d].reshape(1, -1),
            jnp.pad(p['rw_a_up'][i, d], ((RW_HEAD, 0), (0, 0))).astype(BF16),
            p['rw_k_k'][i].reshape(1, -1), p['rw_k_a'][i].reshape(1, -1), p['rw_r_k'][i].reshape(1, -1),
        )
        yb.append(_rwkv_direction(zall, prm, bsz, ctx_len // RW_BLOCK, seq // RW_BLOCK, bool(d)))

    gk_up = jnp.pad(p['gl_gk_up'][i], ((0, 0), (0, LANES - 16), (0, 0))).astype(BF16)
    gk_b = p['gl_gk_b'][i].reshape(2, 1, GLA_QK)
    s0 = jnp.zeros((2, bsz, GLA_HEADS, GLA_DV, GLA_DK), F32)
    oc, s_ctx = _gla(_to_chunk_major(qk[:nc], bsz), _to_chunk_major(vv[:nc], bsz), _to_chunk_major(gk[:nc], bsz),
                     gk_up, gk_b, s0)
    rows_l = seq // GRID_W
    lat = lambda t: t[nc:].reshape(bsz, rows_l, GRID_W * t.shape[1])
    ol, _ = _gla(lat(qk), lat(vv), lat(gk), gk_up, gk_b, s_ctx)
    o_dir = [jnp.concatenate([_from_chunk_major(oc[d], GLA_V).reshape(nc, GLA_V),
                              ol[d].reshape(bsz * seq, GLA_V)], axis=0) for d in range(2)]

    row0 = (nc // bm) if last else 0
    pa = _s5_out(ya[0], ya[1], pa_in, 0, p['s5_d'][i].reshape(1, -1), p['s5_w_glu'][i].astype(BF16), bm, row0)
    pb = _rwkv_out(yb[0], yb[1], zall, p['rw_g_up'][i].astype(BF16), p['rw_ln_w'][i].reshape(1, -1),
                   p['rw_ln_b'][i].reshape(1, -1), p['rw_w_proj'][i].astype(BF16), row0 * (bm // RW_BLOCK),
                   ctx_len // RW_BLOCK, seq // RW_BLOCK)
    pc = _gla_out(o_dir[0], o_dir[1], pa_in, 1, p['gl_norm_g'][i].reshape(1, -1), p['gl_w_proj'][i].astype(BF16),
                  bm, row0)
    bm_merge = bm // 2
    xm = _merge(pa_in, 1, pa, pb, pc, p['w_out'][i].astype(BF16), xa, mod, i, bm_merge, row0 * 2, nc, seq)
    if last:
        msel = dict(nc=0, seq=seq)

    wr = jnp.pad(jnp.concatenate([p['moe_wg1'][i], p['moe_wg2'][i]], axis=1), ((0, 0), (0, LANES - 36)))
    wr_hi = wr.astype(BF16)
    wr_lo = (wr - wr_hi.astype(F32)).astype(BF16)
    br = jnp.pad(jnp.concatenate([p['moe_bg1'][i], p['moe_bg2'][i]]), (0, LANES - 36)).reshape(1, LANES)
    vmoe, meta, meta_t, cnt = _router(xm, p['g_norm2'].reshape(-1, 1, D_MODEL), mod, i, wr_hi, wr_lo, br, bm, **msel)
    d1, d2, src, tile_e, n_used = _moe_plan(meta_t, cnt)
    y_sorted = _experts(vmoe, src, tile_e, n_used, p['moe_w_gate'][i], p['moe_w_up'][i], p['moe_w_down'][i])
    return _combine(y_sorted, d1, d2, meta, xm, mod, i, bm // 2, **msel)


def kernel(x, c, ctx, c_ctx, w_mod, b_mod, g_norm1, g_norm2, w_in, s5_a_re, s5_a_im, s5_log_dt, s5_b_re, s5_b_im,
           s5_c_re, s5_c_im, s5_d, s5_w_glu, rw_mu, rw_w0, rw_w_up, rw_a0, rw_a_up, rw_k_k, rw_k_a, rw_r_k, rw_g_up,
           rw_ln_w, rw_ln_b, rw_w_proj, gl_gk_up, gl_gk_b, gl_norm_g, gl_w_proj, w_out, moe_wg1, moe_bg1, moe_wg2,
           moe_bg2, moe_w_gate, moe_w_up, moe_w_down, g_final):
    p = dict(g_norm1=g_norm1, g_norm2=g_norm2, w_in=w_in, s5_a_re=s5_a_re, s5_a_im=s5_a_im, s5_log_dt=s5_log_dt,
             s5_b_re=s5_b_re, s5_b_im=s5_b_im, s5_c_re=s5_c_re, s5_c_im=s5_c_im, s5_d=s5_d, s5_w_glu=s5_w_glu,
             rw_mu=rw_mu, rw_w0=rw_w0, rw_w_up=rw_w_up, rw_a0=rw_a0, rw_a_up=rw_a_up, rw_k_k=rw_k_k, rw_k_a=rw_k_a,
             rw_r_k=rw_r_k.reshape(rw_r_k.shape[0], -1), rw_g_up=rw_g_up, rw_ln_w=rw_ln_w, rw_ln_b=rw_ln_b,
             rw_w_proj=rw_w_proj, gl_gk_up=gl_gk_up, gl_gk_b=gl_gk_b, gl_norm_g=gl_norm_g, gl_w_proj=gl_w_proj,
             w_out=w_out, moe_wg1=moe_wg1, moe_bg1=moe_bg1, moe_wg2=moe_wg2, moe_bg2=moe_bg2, moe_w_gate=moe_w_gate,
             moe_w_up=moe_w_up, moe_w_down=moe_w_down)
    bsz, seq, d = x.shape
    ctx_len = ctx.shape[1]
    depth = w_mod.shape[0]
    cc = jnp.concatenate([c, c_ctx[None], jnp.zeros((8 - bsz - 1, d), F32)], axis=0)
    mod = _adaln(cc, w_mod, b_mod).reshape(depth, 8, 6, 1, d)
    xa = jnp.concatenate([ctx.reshape(bsz * ctx_len, d), x.reshape(bsz * seq, d)], axis=0)
    for i in range(depth):
        xa = _layer(i, i == depth - 1, xa, mod, bsz, seq, ctx_len, p)
    out = _final_norm(xa, g_final.reshape(1, d), 512, 0)
    return out.reshape(bsz, seq, d)
```

```python
import functools
import math

import jax
import jax.numpy as jnp
from jax import lax
from jax.experimental import pallas as pl
from jax.experimental.pallas import tpu as pltpu

F32 = jnp.float32
BF16 = jnp.bfloat16

D_MODEL = 2048
GRID_W = 64
NORM_EPS = 1e-6

S5_WIDTH = 1024
S5_GROUP = 16
S5_GROUPS = 64
S5_STATE = 64
S5_MAX_RE = -1e-4
S5_TILE = 16
S5_PAIRS = S5_GROUPS // 2

RW_WIDTH = 1024
RW_HEAD = 64
RW_DECAY_SCALE = 0.606531
RW_GN_EPS = 64e-5
RW_BLOCK = 256
RW_CHUNK = 16
RW_PAIRS = RW_WIDTH // 128
RW_Z = 3 * RW_WIDTH + 128

GLA_HEADS = 4
GLA_DK = 128
GLA_DV = 256
GLA_QK = 512
GLA_V = 1024
GLA_TAU = 16.0
GLA_CHUNK = 64

MOE_GROUPS = 4
MOE_PER_GROUP = 8
MOE_EXPERTS = 32
MOE_HIDDEN = 256
MOE_LANE0 = MOE_GROUPS

LANES = 128
VMEM_LIMIT = 56 * 1024 * 1024


def _cp(*sem):
    return pltpu.CompilerParams(dimension_semantics=sem, vmem_limit_bytes=VMEM_LIMIT)


def _dot(a, b):
    return jnp.dot(a, b, preferred_element_type=F32)


def _dot_nt(a, b):
    return lax.dot_general(a, b, (((1,), (1,)), ((), ())), preferred_element_type=F32)


def _dot_tn(a, b):
    return lax.dot_general(a, b, (((0,), (0,)), ((), ())), preferred_element_type=F32)


def _sigmoid(x):
    return 1.0 / (1.0 + jnp.exp(-x))


def _silu(x):
    return x * _sigmoid(x)


def _gelu_tanh(x):
    return 0.5 * x * (1.0 + jnp.tanh(math.sqrt(2.0 / math.pi) * (x + 0.044715 * (x * x * x))))


def _split_hi_lo(x):
    hi = x.astype(BF16)
    lo = (x - hi.astype(F32)).astype(BF16)
    return hi, lo


def _iota(shape, dim):
    return lax.broadcasted_iota(jnp.int32, shape, dim)


def _head_ones(width):
    return (_iota((LANES, LANES), 0) // width == _iota((LANES, LANES), 1) // width).astype(BF16)


def _head_sum(x, ones, exact):
    if exact:
        hi, lo = _split_hi_lo(x)
        return _dot(hi, ones) + _dot(lo, ones)
    return _dot(x.astype(BF16), ones)


def _adaln_kernel(c_ref, w_ref, b_ref, o_ref):
    c = c_ref[...]
    o_ref[...] = _dot(_silu(c).astype(BF16), w_ref[...].astype(BF16)) + b_ref[...]


def _adaln(cc, w_mod, b_mod):
    depth, d, n = w_mod.shape
    bn = 1536
    return pl.pallas_call(
        _adaln_kernel,
        out_shape=jax.ShapeDtypeStruct((depth, 8, n), F32),
        grid=(depth, n // bn),
        in_specs=[
            pl.BlockSpec((8, d), lambda l, j: (0, 0)),
            pl.BlockSpec((None, d, bn), lambda l, j: (l, 0, j)),
            pl.BlockSpec((None, 1, bn), lambda l, j: (l, 0, j)),
        ],
        out_specs=pl.BlockSpec((None, 8, bn), lambda l, j: (l, 0, j)),
        compiler_params=_cp("parallel", "parallel"),
        name="adaln",
    )(cc, w_mod, b_mod.reshape(depth, 1, n))


def _mod_spec(layer, part, bm, nc, seq):
    def index(i, *_):
        r0 = i * bm
        return (layer, jnp.where(r0 < nc, 2, (r0 - nc) // seq), part, 0, 0)
    return pl.BlockSpec((None, None, None, 1, D_MODEL), index)


def _normmod_kernel(x_ref, g_ref, sh_ref, sc_ref, o_ref):
    x = x_ref[...]
    y = x * lax.rsqrt(jnp.mean(x * x, axis=-1, keepdims=True) + NORM_EPS) * g_ref[...]
    o_ref[...] = (y * (1.0 + sc_ref[...]) + sh_ref[...]).astype(o_ref.dtype)


def _normmod(x, g, mod, layer, parts, bm, nc, seq):
    n, d = x.shape
    return pl.pallas_call(
        _normmod_kernel,
        out_shape=jax.ShapeDtypeStruct((n, d), BF16),
        grid=(n // bm,),
        in_specs=[
            pl.BlockSpec((bm, d), lambda i: (i, 0)),
            pl.BlockSpec((None, 1, d), lambda i: (layer, 0, 0)),
            _mod_spec(layer, parts[0], bm, nc, seq),
            _mod_spec(layer, parts[1], bm, nc, seq),
        ],
        out_specs=pl.BlockSpec((bm, d), lambda i: (i, 0)),
        compiler_params=_cp("parallel"),
        name="normmod",
    )(x, g, mod, mod)


def _mm_kernel(x_ref, w_ref, o_ref):
    o_ref[...] = _dot(x_ref[...], w_ref[...]).astype(o_ref.dtype)


def _mm(x, w, bm, bn, out_dtype=F32):
    m, k = x.shape
    n = w.shape[1]
    return pl.pallas_call(
        _mm_kernel,
        out_shape=jax.ShapeDtypeStruct((m, n), out_dtype),
        grid=(m // bm, n // bn),
        in_specs=[pl.BlockSpec((bm, k), lambda i, j: (i, 0)), pl.BlockSpec((k, bn), lambda i, j: (0, j))],
        out_specs=pl.BlockSpec((bm, bn), lambda i, j: (i, j)),
        compiler_params=_cp("parallel", "parallel"),
        name="mm",
    )(x, w)


def _seq_blocks(bsz, nc_blocks, nl_blocks, reverse):
    def local(s):
        lat = nc_blocks + ((nl_blocks - 1 - (s - nc_blocks)) if reverse else (s - nc_blocks))
        ctx = (nc_blocks - 1 - s) if reverse else s
        return jnp.where(s < nc_blocks, ctx, lat)

    def block(b, s):
        i = local(s)
        return jnp.where(i < nc_blocks, b * nc_blocks + i, bsz * nc_blocks + b * nl_blocks + i - nc_blocks)

    return local, block


S5_PACK = 8
S5_PACKS = S5_GROUPS // S5_PACK
S5_PLANE = S5_PACK * S5_STATE
S5_BLOCK = 256
S5_ROWS = 24


def _s5_tables(a_re, a_im, log_dt, b_re, b_im, c_re, c_im):
    lam = lax.complex(jnp.minimum(a_re, S5_MAX_RE), a_im)
    ldt = lam * jnp.exp(log_dt)[:, None]
    lam_bar = jnp.exp(ldt)
    b_bar = ((lam_bar - 1.0) / lam)[..., None] * lax.complex(b_re, b_im)
    c_mat = lax.complex(c_re, c_im)
    eye = jnp.eye(S5_PACK, dtype=F32)

    def block_diag(t):
        k, g, a, b = t.shape
        return (t[:, :, :, None, :] * eye[None, :, None, :, None]).reshape(k, g * a, g * b)

    b_t = jnp.transpose(b_bar, (0, 2, 1)).reshape(S5_PACKS, S5_PACK, S5_GROUP, S5_STATE)
    bblk = jnp.concatenate([block_diag(b_t.real), block_diag(b_t.imag)], axis=2)
    c_t = jnp.transpose(c_mat, (0, 2, 1)).reshape(S5_PACKS, S5_PACK, S5_STATE, S5_GROUP)
    cblk = jnp.concatenate([block_diag(c_t.real), block_diag(-c_t.imag)], axis=1)
    expo = jnp.concatenate([jnp.arange(1, S5_TILE + 1, dtype=F32), jnp.asarray([32.0, 64.0, 128.0], F32),
                            jnp.zeros((S5_ROWS - S5_TILE - 3,), F32)])
    pw = jnp.exp(ldt[None] * expo[:, None, None]).reshape(S5_ROWS, S5_PACKS, S5_PLANE)
    pw = jnp.transpose(pw, (1, 0, 2))
    return bblk.astype(BF16), cblk.astype(BF16), pw.real, pw.imag


def _s5_scan_kernel(u_ref, bblk_ref, cblk_ref, pre_ref, pim_ref, y_ref, car_ref, up_s, h_s, hb_s, yp_s, *, reverse):
    t = S5_TILE
    n_t = S5_BLOCK // t
    pn = S5_PLANE

    @pl.when(pl.program_id(1) == 0)
    def _():
        car_ref[...] = jnp.zeros_like(car_ref)

    ra = _iota((S5_BLOCK, S5_BLOCK), 0)
    cb = _iota((S5_BLOCK, S5_BLOCK), 1)
    perm = ((ra // t == cb % t) & (ra % t == cb // t)).astype(BF16)
    up = _dot(perm, u_ref[...].astype(BF16)).astype(BF16)
    for pk in range(S5_PACKS):
        up_s[pk] = up[:, LANES * pk:LANES * (pk + 1)]
    order = list(range(t - 1, -1, -1)) if reverse else list(range(t))
    rowj = _iota((n_t, 1), 0)

    def pack_body(pk, carry):
        bu = _dot(up_s[pk], bblk_ref[pk])
        pre = pre_ref[pk]
        pim = pim_ref[pk]
        l_re, l_im = pre[0:1], pim[0:1]
        h_re = h_im = None
        for n, s in enumerate(order):
            rows = slice(t * s, t * (s + 1))
            b_re, b_im = bu[rows, :pn], bu[rows, pn:]
            if n == 0:
                h_re, h_im = b_re, b_im
            else:
                h_re, h_im = l_re * h_re - l_im * h_im + b_re, l_re * h_im + l_im * h_re + b_im
            h_s[rows, :pn] = h_re
            h_s[rows, pn:] = h_im
        c_re, c_im = car_ref[pk, 0:1, :pn], car_ref[pk, 0:1, pn:]
        first = rowj == (n_t - 1 if reverse else 0)
        g_re, g_im = pre[t - 1:t], pim[t - 1:t]
        e_re = h_re + jnp.where(first, g_re * c_re - g_im * c_im, 0.0)
        e_im = h_im + jnp.where(first, g_re * c_im + g_im * c_re, 0.0)
        step = 1
        for row in (t - 1, t, t + 1, t + 2):
            if reverse:
                s_re, s_im, ok = pltpu.roll(e_re, n_t - step, 0), pltpu.roll(e_im, n_t - step, 0), rowj < n_t - step
            else:
                s_re, s_im, ok = pltpu.roll(e_re, step, 0), pltpu.roll(e_im, step, 0), rowj >= step
            a_re, a_im = pre[row:row + 1], pim[row:row + 1]
            e_re = e_re + jnp.where(ok, a_re * s_re - a_im * s_im, 0.0)
            e_im = e_im + jnp.where(ok, a_re * s_im + a_im * s_re, 0.0)
            step *= 2
        last = 0 if reverse else n_t - 1
        car_ref[pk, 0:1, :pn] = e_re[last:last + 1]
        car_ref[pk, 0:1, pn:] = e_im[last:last + 1]
        if reverse:
            in_re = jnp.where(first, c_re, pltpu.roll(e_re, n_t - 1, 0))
            in_im = jnp.where(first, c_im, pltpu.roll(e_im, n_t - 1, 0))
        else:
            in_re = jnp.where(first, c_re, pltpu.roll(e_re, 1, 0))
            in_im = jnp.where(first, c_im, pltpu.roll(e_im, 1, 0))
        for n, s in enumerate(order):
            rows = slice(t * s, t * (s + 1))
            a_re, a_im = pre[n:n + 1], pim[n:n + 1]
            hb_s[rows, :pn] = (h_s[rows, :pn] + a_re * in_re - a_im * in_im).astype(BF16)
            hb_s[rows, pn:] = (h_s[rows, pn:] + a_re * in_im + a_im * in_re).astype(BF16)
        yp_s[pk] = _dot(hb_s[...], cblk_ref[pk]).astype(BF16)
        return carry

    lax.fori_loop(0, S5_PACKS, pack_body, 0)
    for pk in range(S5_PACKS):
        y_ref[:, LANES * pk:LANES * (pk + 1)] = _dot(perm, yp_s[pk])


def _s5_scan(u_src, tables, bsz, nc_blocks, nl_blocks, reverse):
    bblk, cblk, pre, pim = tables
    rows = u_src.shape[0]
    blk = S5_BLOCK
    _, block = _seq_blocks(bsz, nc_blocks, nl_blocks, reverse)
    whole = lambda a: pl.BlockSpec(a.shape, lambda b, s: (0,) * a.ndim)
    return pl.pallas_call(
        functools.partial(_s5_scan_kernel, reverse=reverse),
        out_shape=jax.ShapeDtypeStruct((rows, S5_WIDTH), F32),
        grid=(bsz, nc_blocks + nl_blocks),
        in_specs=[pl.BlockSpec((blk, S5_WIDTH), lambda b, s: (block(b, s), 0)),
                  whole(bblk), whole(cblk), whole(pre), whole(pim)],
        out_specs=pl.BlockSpec((blk, S5_WIDTH), lambda b, s: (block(b, s), 0)),
        scratch_shapes=[
            pltpu.VMEM((S5_PACKS, 8, 2 * S5_PLANE), F32),
            pltpu.VMEM((S5_PACKS, blk, LANES), BF16),
            pltpu.VMEM((blk, 2 * S5_PLANE), F32),
            pltpu.VMEM((blk, 2 * S5_PLANE), BF16),
            pltpu.VMEM((S5_PACKS, blk, LANES), BF16),
        ],
        compiler_params=_cp("parallel", "arbitrary"),
        name="s5_bwd" if reverse else "s5_fwd",
    )(u_src, bblk, cblk, pre, pim)


def _gla_kernel(q_ref, k_ref, v_ref, gk_ref, up_ref, gb_ref, s0_ref, o_ref, sf_ref, st_ref):
    d = pl.program_id(0)
    c = pl.program_id(2)
    n = pl.num_programs(2)

    @pl.when(c == 0)
    def _():
        st_ref[...] = s0_ref[...]

    cs = GLA_CHUNK
    sign = 1 - 2 * d
    row = _iota((cs, cs), 0)
    col = _iota((cs, cs), 1)
    causal = (row - col) * sign >= 0
    causal_b = causal.astype(BF16)
    x = _dot(gk_ref[...].astype(BF16), up_ref[...]) + gb_ref[...]
    log_a = (jnp.minimum(x, 0.0) - jnp.log(1.0 + jnp.exp(-jnp.abs(x)))) * (1.0 / GLA_TAU)
    la_hi, la_lo = _split_hi_lo(log_a)
    bcum = _dot(causal_b, la_hi) + _dot(causal_b, la_lo)
    rid = _iota((cs, 1), 0)
    b_mid = jnp.sum(jnp.where(rid == cs // 2 - d, bcum, 0.0), axis=0, keepdims=True)
    b_end = jnp.sum(jnp.where(rid == (cs - 1) * (1 - d), bcum, 0.0), axis=0, keepdims=True)
    q = q_ref[...] * (GLA_DK ** -0.5)
    k = k_ref[...]
    v = v_ref[...]
    for h in range(GLA_HEADS):
        ks = slice(GLA_DK * h, GLA_DK * (h + 1))
        vs = slice(GLA_DV * h, GLA_DV * (h + 1))
        bh = bcum[:, ks]
        qh = q[:, ks]
        kh = k[:, ks]
        vh = v[:, vs].astype(BF16)
        s_t = st_ref[h]
        q_in = (qh * jnp.exp(bh - b_mid[:, ks])).astype(BF16)
        k_in = (kh * jnp.exp(b_mid[:, ks] - bh)).astype(BF16)
        scores = jnp.where(causal, _dot_nt(q_in, k_in), 0.0).astype(BF16)
        q_st = (qh * jnp.exp(bh)).astype(BF16)
        o_ref[:, vs] = _dot(scores, vh) + _dot_nt(q_st, s_t.astype(BF16))
        k_out = (kh * jnp.exp(b_end[:, ks] - bh)).astype(BF16)
        st_ref[h] = s_t * jnp.exp(b_end[:, ks]) + _dot_tn(vh, k_out)

    @pl.when(c == n - 1)
    def _():
        sf_ref[...] = st_ref[...]


def _gla(qk, v, gk, gk_up, gk_b, s0):
    bsz, cs, w = v.shape
    n = w // GLA_V

    def chunk(d, c):
        return c + d * (n - 1 - 2 * c)

    state_spec = pl.BlockSpec((None, None, GLA_HEADS, GLA_DV, GLA_DK), lambda d, b, c: (d, b, 0, 0, 0))
    return pl.pallas_call(
        _gla_kernel,
        out_shape=(jax.ShapeDtypeStruct((2, bsz, cs, w), F32), jax.ShapeDtypeStruct(s0.shape, F32)),
        grid=(2, bsz, n),
        in_specs=[
            pl.BlockSpec((None, cs, GLA_QK), lambda d, b, c: (b, 0, 2 * chunk(d, c))),
            pl.BlockSpec((None, cs, GLA_QK), lambda d, b, c: (b, 0, 2 * chunk(d, c) + 1)),
            pl.BlockSpec((None, cs, GLA_V), lambda d, b, c: (b, 0, chunk(d, c))),
            pl.BlockSpec((None, cs, LANES), lambda d, b, c: (b, 0, chunk(d, c))),
            pl.BlockSpec((None, LANES, GLA_QK), lambda d, b, c: (d, 0, 0)),
            pl.BlockSpec((None, 1, GLA_QK), lambda d, b, c: (d, 0, 0)),
            state_spec,
        ],
        out_specs=(pl.BlockSpec((None, None, cs, GLA_V), lambda d, b, c: (d, b, 0, chunk(d, c))), state_spec),
        scratch_shapes=[pltpu.VMEM((GLA_HEADS, GLA_DV, GLA_DK), F32)],
        compiler_params=_cp("parallel", "parallel", "arbitrary"),
        name="gla",
    )(qk, qk, v, gk, gk_up, gk_b, s0)


def _to_chunk_major(t, bsz):
    n = t.shape[0] // bsz
    d = t.shape[1]
    t = t.reshape(bsz, n // GLA_CHUNK, GLA_CHUNK, d)
    return jnp.transpose(t, (0, 2, 1, 3)).reshape(bsz, GLA_CHUNK, (n // GLA_CHUNK) * d)


def _from_chunk_major(t, d):
    lead = t.shape[:-2]
    n = t.shape[-1] // d
    t = t.reshape(lead + (GLA_CHUNK, n, d))
    return jnp.swapaxes(t, -3, -2).reshape(lead + (n * GLA_CHUNK, d))


def _rwkv_kernel(z0_ref, z1_ref, halo0_ref, halo1_ref, mu_ref, w0_ref, wup_ref, a0_ref, aup_ref, kk_ref, ka_ref,
                 rk_ref, y_ref, st_ref, kt_s, bt_s, kq_s, rt_s, v_s, w_s, u_s, ya_s, ab_s, pin_s,
                 *, reverse, nc_blocks):
    step = pl.program_id(0)
    blk = RW_BLOCK
    ch = RW_CHUNK
    n_ch = blk // ch
    n_b = 2
    n_units = n_b * RW_PAIRS

    @pl.when(step == 0)
    def _():
        st_ref[...] = jnp.zeros_like(st_ref)

    rowi = _iota((blk, 1), 0)
    low = _iota((1, LANES), 1) < RW_HEAD
    ones64 = _head_ones(RW_HEAD)
    pos = rowi % ch
    seq_start = (step == 0) | (step == nc_blocks)

    for b, (z_ref, halo_ref) in enumerate(((z0_ref, halo0_ref), (z1_ref, halo1_ref))):
        z = z_ref[...]
        if reverse:
            prev = pltpu.roll(z, blk - 1, 0)
            edge = halo_ref[0:1, :]
            at_edge = rowi == blk - 1
        else:
            prev = pltpu.roll(z, 1, 0)
            edge = halo_ref[7:8, :]
            at_edge = rowi == 0
        prev = jnp.where(at_edge, jnp.where(seq_start, 0.0, edge), prev)
        zs = z + (prev - z) * mu_ref[...]
        r = zs[:, 0:RW_WIDTH]
        k = zs[:, RW_WIDTH:2 * RW_WIDTH]
        v = zs[:, 2 * RW_WIDTH:3 * RW_WIDTH]
        lora = zs[:, 3 * RW_WIDTH:3 * RW_WIDTH + LANES]
        lora_w = jnp.where(low, jnp.tanh(lora), 0.0).astype(BF16)
        lora_a = jnp.where(low, 0.0, lora).astype(BF16)
        logw = -RW_DECAY_SCALE * _sigmoid(w0_ref[...] + _dot(lora_w, wup_ref[...]))
        a = _sigmoid(a0_ref[...] + _dot(lora_a, aup_ref[...]))
        kk = k * kk_ref[...]
        kp = k * (1.0 + (a - 1.0) * ka_ref[...])
        rkb = r * kp * rk_ref[...]

        cl = logw
        sh = 1
        while sh < ch:
            if reverse:
                cl = cl + jnp.where(pos < ch - sh, pltpu.roll(cl, blk - sh, 0), 0.0)
            else:
                cl = cl + jnp.where(pos >= sh, pltpu.roll(cl, sh, 0), 0.0)
            sh *= 2
        p_in = jnp.exp(cl)
        pin_s[b] = p_in
        p_ex = jnp.exp(cl - logw)
        p_inv = jnp.exp(-cl)

        for p in range(RW_PAIRS):
            ls = slice(LANES * p, LANES * (p + 1))
            q = b * RW_PAIRS + p
            kkp = kk[:, ls]
            ssq = _head_sum(kkp * kkp, ones64, False)
            kkn = kkp * (1.0 / jnp.maximum(jnp.sqrt(ssq), 1e-12))
            kt_s[q] = (kkn * p_ex[:, ls]).astype(BF16)
            bt_s[q] = (kkn * a[:, ls] * p_inv[:, ls]).astype(BF16)
            kq_s[q] = (kp[:, ls] * p_inv[:, ls]).astype(BF16)
            rt_s[q] = (r[:, ls] * p_in[:, ls]).astype(BF16)
            v_s[q] = v[:, ls].astype(BF16)
            ya_s[q] = _head_sum(rkb[:, ls], ones64, False) * v[:, ls]

    rr = _iota((blk, blk), 0)
    cc = _iota((blk, blk), 1)
    same = rr // ch == cc // ch
    before = (cc > rr) if reverse else (cc < rr)
    strict = same & before
    incl = same & (before | (rr == cc))
    fold0 = (_iota((blk, LANES), 0) % ch == _iota((blk, LANES), 1)).astype(BF16)
    fold1 = (_iota((blk, LANES), 0) % ch + ch == _iota((blk, LANES), 1)).astype(BF16)
    lane_lo = _iota((1, LANES), 1) < RW_HEAD

    def pair_body(p, carry):
        units = [b * RW_PAIRS + p for b in range(n_b)]
        heads = [(u, hh) for u in range(n_b) for hh in range(2)]
        kt = [kt_s[q] for q in units]
        rt = [rt_s[q] for q in units]
        vv = [v_s[q] for q in units]
        ya = [ya_s[q] for q in units]
        rhs = [jnp.concatenate([bt_s[q], kq_s[q]], axis=0) for q in units]
        g = []
        for u, hh in heads:
            mine = lane_lo if hh == 0 else jnp.logical_not(lane_lo)
            zero = jnp.zeros_like(kt[u])
            lhs = jnp.concatenate([jnp.where(mine, kt[u], zero), jnp.where(mine, rt[u], zero)], axis=0)
            g.append(_dot_nt(lhs, rhs[u]))
        n1 = [jnp.where(strict, -gh[:blk, :blk], 0.0).astype(BF16) for gh in g]
        a_kq = [jnp.where(strict, gh[:blk, blk:], 0.0).astype(BF16) for gh in g]
        a_rb = [jnp.where(incl, gh[blk:, :blk], 0.0).astype(BF16) for gh in g]
        a_rq = [jnp.where(incl, gh[blk:, blk:], 0.0).astype(BF16) for gh in g]
        n2 = [_dot(n, n).astype(BF16) for n in n1]
        akv = [_dot(a, vv[u]) for a, (u, _) in zip(a_kq, heads)]
        n4 = [_dot(n, n).astype(BF16) for n in n2]
        y_in = [_dot(a, vv[u]) for a, (u, _) in zip(a_rq, heads)]
        n8 = [_dot(n, n).astype(BF16) for n in n4]
        fold = [_dot(a, fold0 if hh == 0 else fold1) for a, (_, hh) in zip(a_rb, heads)]
        rhs_t = [jnp.concatenate([kt[u].astype(F32), av], axis=1) for av, (u, _) in zip(akv, heads)]
        for nk in (n8, n4, n2, n1):
            rhs_t = [x + _dot(n, x.astype(BF16)) for n, x in zip(nk, rhs_t)]
        for u, q in enumerate(units):
            h0, h1 = 2 * u, 2 * u + 1
            w_s[q] = jnp.where(lane_lo, rhs_t[h0][:, :LANES], rhs_t[h1][:, :LANES]).astype(BF16)
            u_s[q] = jnp.where(lane_lo, rhs_t[h0][:, LANES:], rhs_t[h1][:, LANES:])
            ya_s[q] = ya[u] + jnp.where(lane_lo, y_in[h0], y_in[h1])
            ab_s[q] = (fold[h0] + fold[h1]).astype(BF16)
        return carry

    lax.fori_loop(0, RW_PAIRS, pair_body, 0)

    blockdiag = (_iota((LANES, LANES), 0) // RW_HEAD) == (_iota((LANES, LANES), 1) // RW_HEAD)
    end_row = 0 if reverse else ch - 1

    def chunk_body(i, carry):
        c = (n_ch - 1 - i) if reverse else i
        rows = pl.ds(pl.multiple_of(c * ch, ch), ch)
        s_old = [st_ref[q] for q in range(n_units)]
        m1 = [_dot_nt(jnp.concatenate([w_s[q, rows, :], rt_s[q, rows, :]], axis=0), s_old[q].astype(BF16))
              for q in range(n_units)]
        zc = [-(m1[q][:ch] + u_s[q, rows, :]) for q in range(n_units)]
        upd = []
        for q in range(n_units):
            zv = jnp.concatenate([zc[q].astype(BF16), v_s[q, rows, :]], axis=0)
            bk = jnp.concatenate([bt_s[q, rows, :], kq_s[q, rows, :]], axis=0)
            upd.append(_dot_tn(zv, bk))
        yc = []
        for q in range(n_units):
            z2 = jnp.concatenate([jnp.where(lane_lo, zc[q], 0.0), jnp.where(lane_lo, 0.0, zc[q])], axis=0)
            yc.append(m1[q][ch:] + _dot(ab_s[q, rows, :][:, :2 * ch], z2.astype(BF16)) + ya_s[q, rows, :])
        for q in range(n_units):
            b, p = divmod(q, RW_PAIRS)
            p_end = pin_s[b, rows, LANES * p:LANES * (p + 1)][end_row:end_row + 1]
            st_ref[q] = (s_old[q] + jnp.where(blockdiag, upd[q], 0.0)) * p_end
            y_ref[b, rows, LANES * p:LANES * (p + 1)] = yc[q]
        return carry

    lax.fori_loop(0, n_ch, chunk_body, 0)


def _rwkv_direction(zall, prm, bsz, nc_blocks, nl_blocks, reverse):
    assert bsz == 2
    mu, w0, wup, a0, aup, k_k, k_a, r_k = prm
    rows = zall.shape[0]
    blk = RW_BLOCK
    steps = nc_blocks + nl_blocks
    n_blocks = rows // blk

    def local(s):
        lat = nc_blocks + ((nl_blocks - 1 - (s - nc_blocks)) if reverse else (s - nc_blocks))
        ctx = (nc_blocks - 1 - s) if reverse else s
        return jnp.where(s < nc_blocks, ctx, lat)

    def block(b, s):
        i = local(s)
        return jnp.where(i < nc_blocks, b * nc_blocks + i, bsz * nc_blocks + b * nl_blocks + i - nc_blocks)

    def halo(b, s):
        i = block(b, s)
        if reverse:
            return jnp.minimum((i + 1) * (blk // 8), n_blocks * (blk // 8) - 1)
        return jnp.maximum(i * (blk // 8) - 1, 0)

    vec = lambda w: pl.BlockSpec((1, w), lambda s: (0, 0))
    mat = lambda: pl.BlockSpec((LANES, RW_WIDTH), lambda s: (0, 0))
    n_units = bsz * RW_PAIRS
    unit_bf = pltpu.VMEM((n_units, blk, LANES), BF16)
    unit_f = pltpu.VMEM((n_units, blk, LANES), F32)
    z_spec = lambda b: pl.BlockSpec((blk, RW_Z), lambda s: (block(b, s), 0))
    halo_spec = lambda b: pl.BlockSpec((8, RW_Z), lambda s: (halo(b, s), 0))
    return pl.pallas_call(
        functools.partial(_rwkv_kernel, reverse=reverse, nc_blocks=nc_blocks),
        out_shape=jax.ShapeDtypeStruct((bsz, steps * blk, RW_WIDTH), F32),
        grid=(steps,),
        in_specs=[
            z_spec(0), z_spec(1), halo_spec(0), halo_spec(1),
            vec(RW_Z), vec(RW_WIDTH), mat(), vec(RW_WIDTH), mat(), vec(RW_WIDTH), vec(RW_WIDTH), vec(RW_WIDTH),
        ],
        out_specs=pl.BlockSpec((bsz, blk, RW_WIDTH), lambda s: (0, local(s), 0)),
        scratch_shapes=[
            pltpu.VMEM((n_units, LANES, LANES), F32),
            unit_bf, unit_bf, unit_bf, unit_bf, unit_bf,
            unit_bf, unit_f, unit_f, unit_bf,
            pltpu.VMEM((bsz, blk, RW_WIDTH), F32),
        ],
        compiler_params=_cp("arbitrary"),
        name="rwkv_bwd" if reverse else "rwkv_fwd",
    )(zall, zall, zall, zall, mu, w0, wup, a0, aup, k_k, k_a, r_k)


def _branches_kernel(saf_ref, sab_ref, su_ref, sd_ref, rf_ref, rb_ref, rg_ref, gup_ref, lw_ref, lb_ref,
                     of_ref, ob_ref, og_ref, ng_ref, ga_ref, gb_ref, gc_ref, wglu_ref, wrw_ref, wgl_ref, m_ref):
    za = _gelu_tanh(saf_ref[...] + sab_ref[...] + sd_ref[...] * su_ref[...]).astype(BF16)
    hid = _dot(za, wglu_ref[...])
    m = _sigmoid(ga_ref[...]) * (hid[:, :D_MODEL] * _sigmoid(hid[:, D_MODEL:]))
    ones64 = _head_ones(RW_HEAD)
    gate = _dot(_sigmoid(rg_ref[...]).astype(BF16), gup_ref[...])
    zb = []
    for p in range(RW_PAIRS):
        ls = slice(LANES * p, LANES * (p + 1))
        y = rf_ref[:, ls] + rb_ref[:, ls]
        mean = _head_sum(y, ones64, True) * (1.0 / RW_HEAD)
        yc = y - mean
        var = _head_sum(yc * yc, ones64, True) * (1.0 / RW_HEAD)
        yn = yc * lax.rsqrt(var + RW_GN_EPS) * lw_ref[:, ls] + lb_ref[:, ls]
        zb.append((yn * gate[:, ls]).astype(BF16))
    m = m + _sigmoid(gb_ref[...]) * _dot(jnp.concatenate(zb, axis=1), wrw_ref[...])
    zc = []
    for h in range(GLA_HEADS):
        vs = slice(GLA_DV * h, GLA_DV * (h + 1))
        o = of_ref[:, vs] + ob_ref[:, vs]
        on = o * lax.rsqrt(jnp.mean(o * o, axis=-1, keepdims=True) + NORM_EPS) * ng_ref[:, vs]
        zc.append((on * _silu(og_ref[:, vs])).astype(BF16))
    m = m + _sigmoid(gc_ref[...]) * _dot(jnp.concatenate(zc, axis=1), wgl_ref[...])
    m_ref[...] = m.astype(BF16)


def _branches(ya, yb, o_dir, pa_in, zall, prm, row0, nc_blocks, nl_blocks):
    s5_d, g_up, ln_w, ln_b, norm_g, w_glu, w_rw, w_gl = prm
    bm = RW_BLOCK
    bsz = yb[0].shape[0]
    n = zall.shape[0] - row0 * bm
    rows = lambda w, blk=0: pl.BlockSpec((bm, w), lambda i: (i + row0, blk))

    def y_index(i):
        i = i + row0
        t = i - bsz * nc_blocks
        return (jnp.where(t < 0, i // nc_blocks, t // nl_blocks),
                jnp.where(t < 0, i % nc_blocks, nc_blocks + t % nl_blocks), 0)

    y_spec = pl.BlockSpec((None, bm, RW_WIDTH), y_index)
    vec = pl.BlockSpec((1, 1024), lambda i: (0, 0))
    const = lambda a: pl.BlockSpec(a.shape, lambda i: (0,) * a.ndim, pipeline_mode=pl.Buffered(1))
    return pl.pallas_call(
        _branches_kernel,
        out_shape=jax.ShapeDtypeStruct((n, D_MODEL), BF16),
        grid=(n // bm,),
        in_specs=[rows(1024), rows(1024), rows(1024, 0), vec,
                  y_spec, y_spec, rows(LANES, RW_Z // LANES), const(g_up), vec, vec,
                  rows(1024), rows(1024), rows(1024, 1), vec,
                  rows(D_MODEL, 1), rows(D_MODEL, 2), rows(D_MODEL, 3),
                  const(w_glu), const(w_rw), const(w_gl)],
        out_specs=pl.BlockSpec((bm, D_MODEL), lambda i: (i, 0)),
        compiler_params=_cp("parallel"),
        name="branches",
    )(ya[0], ya[1], pa_in, s5_d, yb[0], yb[1], zall, g_up, ln_w, ln_b,
      o_dir[0], o_dir[1], pa_in, norm_g, pa_in, pa_in, pa_in, w_glu, w_rw, w_gl)


def _out_proj_kernel(m_ref, w_ref, x_ref, gate_ref, o_ref):
    o_ref[...] = x_ref[...] + gate_ref[...] * _dot(m_ref[...], w_ref[...])


def _out_proj(m, w_out, x, mod, layer, bm, row0, nc, seq):
    n, d = m.shape
    bn = 1024
    mod_index = _mod_spec(layer, 2, bm, nc, seq).index_map
    return pl.pallas_call(
        _out_proj_kernel,
        out_shape=jax.ShapeDtypeStruct((n, d), F32),
        grid=(n // bm, d // bn),
        in_specs=[pl.BlockSpec((bm, d), lambda i, j: (i, 0)),
                  pl.BlockSpec((d, bn), lambda i, j: (0, j)),
                  pl.BlockSpec((bm, bn), lambda i, j: (i + row0, j)),
                  pl.BlockSpec((None, None, None, 1, bn), lambda i, j: mod_index(i + row0)[:4] + (j,))],
        out_specs=pl.BlockSpec((bm, bn), lambda i, j: (i, j)),
        compiler_params=_cp("parallel", "parallel"),
        name="out_proj",
    )(m, w_out, x, mod)


MOE_TM = 256
META_E1, META_E2, META_R1, META_R2, META_W1, META_W2 = range(6)


def _router_kernel(x_ref, g_ref, sh_ref, sc_ref, wr_hi_ref, wr_lo_ref, br_ref, v_ref, meta_ref, meta_t_ref, cnt_ref,
                   base_s):
    @pl.when(pl.program_id(0) == 0)
    def _():
        base_s[...] = jnp.zeros_like(base_s)

    x = x_ref[...]
    y = x * lax.rsqrt(jnp.mean(x * x, axis=-1, keepdims=True) + NORM_EPS) * g_ref[...]
    t = y * (1.0 + sc_ref[...]) + sh_ref[...]
    v_ref[...] = t
    t_hi, t_lo = _split_hi_lo(t)
    logits = (_dot(t_hi, wr_hi_ref[...]) + _dot(t_lo, wr_hi_ref[...]) + _dot(t_hi, wr_lo_ref[...])) + br_ref[...]
    lane = _iota(logits.shape, 1).astype(F32)
    neg = jnp.float32(-jnp.inf)
    big = jnp.float32(LANES)
    l1 = jnp.where(lane < MOE_GROUPS, logits, neg)
    m1 = jnp.max(l1, axis=-1, keepdims=True)
    p_top = 1.0 / jnp.sum(jnp.exp(l1 - m1), axis=-1, keepdims=True)
    grp = jnp.min(jnp.where(l1 == m1, lane, big), axis=-1, keepdims=True)
    lo = MOE_LANE0 + MOE_PER_GROUP * grp
    in_grp = (lane >= lo) & (lane < lo + MOE_PER_GROUP)
    l2 = jnp.where(in_grp, logits, neg)
    v1 = jnp.max(l2, axis=-1, keepdims=True)
    i1 = jnp.min(jnp.where(l2 == v1, lane, big), axis=-1, keepdims=True)
    l3 = jnp.where(lane == i1, neg, l2)
    v2 = jnp.max(l3, axis=-1, keepdims=True)
    i2 = jnp.min(jnp.where(l3 == v2, lane, big), axis=-1, keepdims=True)
    e2 = jnp.exp(v2 - v1)
    w1 = p_top / (1.0 + e2)
    w2 = p_top * e2 / (1.0 + e2)
    pick1 = lane == i1
    pick2 = lane == i2
    chosen = jnp.where(pick1 | pick2, 1.0, 0.0)
    bm = x.shape[0]
    earlier = (_iota((bm, bm), 1) < _iota((bm, bm), 0)).astype(BF16)
    before = _dot(earlier, chosen.astype(BF16)) + base_s[...]
    r1 = jnp.sum(jnp.where(pick1, before, 0.0), axis=-1, keepdims=True)
    r2 = jnp.sum(jnp.where(pick2, before, 0.0), axis=-1, keepdims=True)
    base_s[...] += jnp.sum(chosen, axis=0, keepdims=True)
    cnt_ref[...] = base_s[...]
    meta = jnp.zeros_like(logits)
    for slot, val in ((META_E1, i1 - MOE_LANE0), (META_E2, i2 - MOE_LANE0), (META_R1, r1), (META_R2, r2),
                      (META_W1, w1), (META_W2, w2)):
        meta = jnp.where(lane == slot, val, meta)
    meta_ref[...] = meta
    meta_t_ref[...] = meta.T[:8]


def _router(x, g, mod, layer, wr_hi, wr_lo, br, bm, nc, seq):
    n, d = x.shape
    return pl.pallas_call(
        _router_kernel,
        out_shape=(jax.ShapeDtypeStruct((n, d), F32), jax.ShapeDtypeStruct((n, LANES), F32),
                   jax.ShapeDtypeStruct((8, n), F32), jax.ShapeDtypeStruct((1, LANES), F32)),
        grid=(n // bm,),
        in_specs=[
            pl.BlockSpec((bm, d), lambda i: (i, 0)),
            pl.BlockSpec((None, 1, d), lambda i: (layer, 0, 0)),
            _mod_spec(layer, 3, bm, nc, seq), _mod_spec(layer, 4, bm, nc, seq),
            pl.BlockSpec((d, LANES), lambda i: (0, 0)), pl.BlockSpec((d, LANES), lambda i: (0, 0)),
            pl.BlockSpec((1, LANES), lambda i: (0, 0)),
        ],
        out_specs=(pl.BlockSpec((bm, d), lambda i: (i, 0)), pl.BlockSpec((bm, LANES), lambda i: (i, 0)),
                   pl.BlockSpec((8, bm), lambda i: (0, i)), pl.BlockSpec((1, LANES), lambda i: (0, 0))),
        scratch_shapes=[pltpu.VMEM((1, LANES), F32)],
        compiler_params=_cp("arbitrary"),
        name="moe_router",
    )(x, g, mod, mod, wr_hi, wr_lo, br)


def _moe_plan(meta_t, cnt):
    tm = MOE_TM
    n_tok = meta_t.shape[1]
    counts = cnt[0, MOE_LANE0:MOE_LANE0 + MOE_EXPERTS].astype(jnp.int32)
    seg = ((counts + tm - 1) // tm) * tm
    ends = jnp.cumsum(seg)
    off = ends - seg
    rec = meta_t[:4].astype(jnp.int32)
    first_row = jnp.sum(jnp.where(rec[:2, None, :] == jnp.arange(MOE_EXPERTS)[None, :, None], off[None, :, None], 0),
                        axis=1)
    dest = first_row + rec[2:4]
    n_rows = 2 * n_tok + MOE_EXPERTS * tm
    n_tiles = n_rows // tm
    tile_e = jnp.sum(((jnp.arange(n_tiles) * tm)[:, None] >= ends[None, :]).astype(jnp.int32), axis=1)
    tile_e = jnp.minimum(tile_e, MOE_EXPERTS - 1)
    tok = jnp.broadcast_to(jnp.arange(n_tok, dtype=jnp.int32), (2, n_tok))
    src = jnp.zeros((n_rows,), jnp.int32).at[dest.reshape(-1)].set(tok.reshape(-1))
    return dest[0], dest[1], src, tile_e, (ends[-1] // tm).reshape(1)


def _row_copy(src_hbm, row, dst, slot, r, sem):
    return pltpu.make_async_copy(src_hbm.at[pl.ds(row, 1)], dst.at[slot, pl.ds(r, 1)], sem)


def _experts_kernel(src_ref, te_ref, nu_ref, v_hbm, wg_ref, wu_ref, wd_ref, y_ref, xbuf, sem):
    del te_ref
    i = pl.program_id(0)
    n_used = nu_ref[0]
    slot = i % 2

    def gather(tile, into):
        for r in range(MOE_TM):
            _row_copy(v_hbm, src_ref[tile * MOE_TM + r], xbuf, into, r, sem.at[into]).start(priority=1)

    def expert():
        for r in range(MOE_TM):
            _row_copy(v_hbm, 0, xbuf, slot, r, sem.at[slot]).wait()
        t = xbuf[slot].astype(BF16)
        hid = _silu(_dot(t, wg_ref[...].astype(BF16))) * _dot(t, wu_ref[...].astype(BF16))
        y_ref[...] = _dot(hid.astype(BF16), wd_ref[...].astype(BF16))

    @pl.when((i == 0) & (n_used > 0))
    def _():
        gather(0, 0)

    @pl.when(i + 1 < n_used)
    def _():
        gather(i + 1, 1 - slot)
        expert()

    @pl.when(i + 1 == n_used)
    def _():
        expert()

    @pl.when(i >= n_used)
    def _():
        y_ref[...] = jnp.zeros_like(y_ref)


def _experts(v, src, tile_e, n_used, layer, w_gate, w_up, w_down):
    d = v.shape[1]
    hdim = w_gate.shape[3]
    n_rows = src.shape[0]
    tm = MOE_TM
    by_expert = lambda i, src_r, te_r, nu_r: (layer, te_r[i], 0, 0)
    return pl.pallas_call(
        _experts_kernel,
        out_shape=jax.ShapeDtypeStruct((n_rows, d), F32),
        grid_spec=pltpu.PrefetchScalarGridSpec(
            num_scalar_prefetch=3,
            grid=(n_rows // tm,),
            in_specs=[
                pl.BlockSpec(memory_space=pl.ANY),
                pl.BlockSpec((None, None, d, hdim), by_expert),
                pl.BlockSpec((None, None, d, hdim), by_expert),
                pl.BlockSpec((None, None, hdim, d), by_expert),
            ],
            out_specs=pl.BlockSpec((tm, d), lambda i, *_: (i, 0)),
            scratch_shapes=[pltpu.VMEM((2, tm, d), F32), pltpu.SemaphoreType.DMA((2,))],
        ),
        compiler_params=_cp("arbitrary"),
        name="moe_experts",
    )(src, tile_e, n_used, v, w_gate, w_up, w_down)


def _combine_kernel(d1_ref, d2_ref, y_hbm, meta_ref, x_ref, gate_ref, o_ref, buf1, buf2, sem):
    i = pl.program_id(0)
    bm = x_ref.shape[0]
    slot = i % 2

    def gather(tile, into):
        for r in range(bm):
            t = tile * bm + r
            _row_copy(y_hbm, d1_ref[t], buf1, into, r, sem.at[0, into]).start(priority=0)
            _row_copy(y_hbm, d2_ref[t], buf2, into, r, sem.at[1, into]).start(priority=1)

    @pl.when(i == 0)
    def _():
        gather(0, 0)

    @pl.when(i + 1 < pl.num_programs(0))
    def _():
        gather(i + 1, 1 - slot)

    for r in range(bm):
        _row_copy(y_hbm, 0, buf1, slot, r, sem.at[0, slot]).wait()
        _row_copy(y_hbm, 0, buf2, slot, r, sem.at[1, slot]).wait()
    meta = meta_ref[...]
    lane = _iota(meta.shape, 1)
    w1 = jnp.sum(jnp.where(lane == META_W1, meta, 0.0), axis=-1, keepdims=True)
    w2 = jnp.sum(jnp.where(lane == META_W2, meta, 0.0), axis=-1, keepdims=True)
    o_ref[...] = x_ref[...] + gate_ref[...] * (w1 * buf1[slot] + w2 * buf2[slot])


def _combine(y, d1, d2, meta, x, mod, layer, bm, nc, seq):
    n, d = x.shape
    at_tile = lambda i, *_: (i, 0)
    mod_index = _mod_spec(layer, 5, bm, nc, seq).index_map
    return pl.pallas_call(
        _combine_kernel,
        out_shape=jax.ShapeDtypeStruct((n, d), F32),
        grid_spec=pltpu.PrefetchScalarGridSpec(
            num_scalar_prefetch=2,
            grid=(n // bm,),
            in_specs=[
                pl.BlockSpec(memory_space=pl.ANY),
                pl.BlockSpec((bm, LANES), at_tile),
                pl.BlockSpec((bm, d), at_tile),
                pl.BlockSpec((None, None, None, 1, d), lambda i, *_: mod_index(i)),
            ],
            out_specs=pl.BlockSpec((bm, d), at_tile),
            scratch_shapes=[pltpu.VMEM((2, bm, d), F32), pltpu.VMEM((2, bm, d), F32),
                            pltpu.SemaphoreType.DMA((2, 2))],
        ),
        compiler_params=_cp("arbitrary"),
        name="moe_combine",
    )(d1, d2, y, meta, x, mod)


def _final_norm_kernel(x_ref, g_ref, o_ref):
    x = x_ref[...]
    o_ref[...] = x * lax.rsqrt(jnp.mean(x * x, axis=-1, keepdims=True) + NORM_EPS) * g_ref[...]


def _final_norm(x, g, bm, row0):
    n, d = x.shape
    n -= row0 * bm
    return pl.pallas_call(
        _final_norm_kernel,
        out_shape=jax.ShapeDtypeStruct((n, d), F32),
        grid=(n // bm,),
        in_specs=[pl.BlockSpec((bm, d), lambda i: (i + row0, 0)), pl.BlockSpec((1, d), lambda i: (0, 0))],
        out_specs=pl.BlockSpec((bm, d), lambda i: (i, 0)),
        compiler_params=_cp("parallel"),
        name="final_norm",
    )(x, g)


_COL = dict(s5=0, rw=S5_WIDTH, rg=S5_WIDTH + 3 * RW_WIDTH + 128, q=4352, k=4864, v=5376, gk=6400, og=6416, gates=7440)


def _pad_rows(w, rows):
    return jnp.pad(w, ((0, rows - w.shape[0]), (0, 0)))


def _layer(i, last, xa, mod, bsz, seq, ctx_len, p):
    nc = bsz * ctx_len
    rows = xa.shape[0]
    bm = 512
    msel = dict(nc=nc, seq=seq)
    u = _normmod(xa, p['g_norm1'].reshape(-1, 1, D_MODEL), mod, i, (0, 1), bm, **msel)

    w_in = p['w_in'][i]
    col = lambda a, w: w_in[:, a:a + w].astype(BF16)
    bm_in = 1088 if rows % 1088 == 0 else bm
    w_a = jnp.concatenate([col(0, S5_WIDTH), col(_COL['og'], GLA_V), col(_COL['gates'], 3 * D_MODEL)], axis=1)
    w_z = jnp.concatenate([col(_COL['rw'], 3 * RW_WIDTH + 128), col(_COL['rg'], 128)], axis=1)
    w_qk = col(_COL['q'], 2 * GLA_QK)
    w_v = col(_COL['v'], GLA_V)
    w_gk = jnp.pad(col(_COL['gk'], 16), ((0, 0), (0, LANES - 16)))
    pa_in = _mm(u, w_a, bm_in, 1024)
    zall = _mm(u, w_z, bm_in, 1664)
    qk = _mm(u, w_qk, bm_in, 1024)
    vv = _mm(u, w_v, bm_in, 1024)
    gk = _mm(u, w_gk, bm_in, LANES)

    ya = []
    for d in range(2):
        tables = _s5_tables(*(p[k][i, d] for k in ('s5_a_re', 's5_a_im', 's5_log_dt', 's5_b_re', 's5_b_im',
                                                   's5_c_re', 's5_c_im')))
        ya.append(_s5_scan(pa_in, tables, bsz, ctx_len // S5_BLOCK, seq // S5_BLOCK, bool(d)))

    yb = []
    for d in range(2):
        prm = (
            jnp.pad(p['rw_mu'][i, d], (0, RW_Z - p['rw_mu'].shape[-1])).reshape(1, RW_Z),
            p['rw_w0'][i, d].reshape(1, -1),
            _pad_rows(p['rw_w_up'][i, d], LANES).astype(BF16),
            p['rw_a0'][i, d].reshape(1, -1),
            jnp.pad(p['rw_a_up'][i, d], ((RW_HEAD, 0), (0, 0))).astype(BF16),
            p['rw_k_k'][i].reshape(1, -1), p['rw_k_a'][i].reshape(1, -1), p['rw_r_k'][i].reshape(1, -1),
        )
        yb.append(_rwkv_direction(zall, prm, bsz, ctx_len // RW_BLOCK, seq // RW_BLOCK, bool(d)))

    gk_up = jnp.pad(p['gl_gk_up'][i], ((0, 0), (0, LANES - 16), (0, 0))).astype(BF16)
    gk_b = p['gl_gk_b'][i].reshape(2, 1, GLA_QK)
    s0 = jnp.zeros((2, bsz, GLA_HEADS, GLA_DV, GLA_DK), F32)
    oc, s_ctx = _gla(_to_chunk_major(qk[:nc], bsz), _to_chunk_major(vv[:nc], bsz), _to_chunk_major(gk[:nc], bsz),
                     gk_up, gk_b, s0)
    rows_l = seq // GRID_W
    lat = lambda t: t[nc:].reshape(bsz, rows_l, GRID_W * t.shape[1])
    ol, _ = _gla(lat(qk), lat(vv), lat(gk), gk_up, gk_b, s_ctx)
    o_dir = [jnp.concatenate([_from_chunk_major(oc[d], GLA_V).reshape(nc, GLA_V),
                              ol[d].reshape(bsz * seq, GLA_V)], axis=0) for d in range(2)]

    row0 = (nc // bm) if last else 0
    branch_prm = (p['s5_d'][i].reshape(1, -1), p['rw_g_up'][i].astype(BF16), p['rw_ln_w'][i].reshape(1, -1),
                  p['rw_ln_b'][i].reshape(1, -1), p['gl_norm_g'][i].reshape(1, -1), p['s5_w_glu'][i].astype(BF16),
                  p['rw_w_proj'][i].astype(BF16), p['gl_w_proj'][i].astype(BF16))
    merged = _branches(ya, yb, o_dir, pa_in, zall, branch_prm, row0 * (bm // RW_BLOCK), ctx_len // RW_BLOCK,
                       seq // RW_BLOCK)
    xm = _out_proj(merged, p['w_out'][i].astype(BF16), xa, mod, i, bm, row0, nc, seq)
    if last:
        msel = dict(nc=0, seq=seq)

    wr = jnp.pad(jnp.concatenate([p['moe_wg1'][i], p['moe_wg2'][i]], axis=1), ((0, 0), (0, LANES - 36)))
    wr_hi = wr.astype(BF16)
    wr_lo = (wr - wr_hi.astype(F32)).astype(BF16)
    br = jnp.pad(jnp.concatenate([p['moe_bg1'][i], p['moe_bg2'][i]]), (0, LANES - 36)).reshape(1, LANES)
    vmoe, meta, meta_t, cnt = _router(xm, p['g_norm2'].reshape(-1, 1, D_MODEL), mod, i, wr_hi, wr_lo, br, bm, **msel)
    d1, d2, src, tile_e, n_used = _moe_plan(meta_t, cnt)
    y_sorted = _experts(vmoe, src, tile_e, n_used, i, p['moe_w_gate'], p['moe_w_up'], p['moe_w_down'])
    return _combine(y_sorted, d1, d2, meta, xm, mod, i, bm // 2, **msel)


def kernel(x, c, ctx, c_ctx, w_mod, b_mod, g_norm1, g_norm2, w_in, s5_a_re, s5_a_im, s5_log_dt, s5_b_re, s5_b_im,
           s5_c_re, s5_c_im, s5_d, s5_w_glu, rw_mu, rw_w0, rw_w_up, rw_a0, rw_a_up, rw_k_k, rw_k_a, rw_r_k, rw_g_up,
           rw_ln_w, rw_ln_b, rw_w_proj, gl_gk_up, gl_gk_b, gl_norm_g, gl_w_proj, w_out, moe_wg1, moe_bg1, moe_wg2,
           moe_bg2, moe_w_gate, moe_w_up, moe_w_down, g_final):
    p = dict(g_norm1=g_norm1, g_norm2=g_norm2, w_in=w_in, s5_a_re=s5_a_re, s5_a_im=s5_a_im, s5_log_dt=s5_log_dt,
             s5_b_re=s5_b_re, s5_b_im=s5_b_im, s5_c_re=s5_c_re, s5_c_im=s5_c_im, s5_d=s5_d, s5_w_glu=s5_w_glu,
             rw_mu=rw_mu, rw_w0=rw_w0, rw_w_up=rw_w_up, rw_a0=rw_a0, rw_a_up=rw_a_up, rw_k_k=rw_k_k, rw_k_a=rw_k_a,
             rw_r_k=rw_r_k.reshape(rw_r_k.shape[0], -1), rw_g_up=rw_g_up, rw_ln_w=rw_ln_w, rw_ln_b=rw_ln_b,
             rw_w_proj=rw_w_proj, gl_gk_up=gl_gk_up, gl_gk_b=gl_gk_b, gl_norm_g=gl_norm_g, gl_w_proj=gl_w_proj,
             w_out=w_out, moe_wg1=moe_wg1, moe_bg1=moe_bg1, moe_wg2=moe_wg2, moe_bg2=moe_bg2, moe_w_gate=moe_w_gate,
             moe_w_up=moe_w_up, moe_w_down=moe_w_down)
    bsz, seq, d = x.shape
    ctx_len = ctx.shape[1]
    depth = w_mod.shape[0]
    cc = jnp.concatenate([c, c_ctx[None], jnp.zeros((8 - bsz - 1, d), F32)], axis=0)
    mod = _adaln(cc, w_mod, b_mod).reshape(depth, 8, 6, 1, d)
    xa = jnp.concatenate([ctx.reshape(bsz * ctx_len, d), x.reshape(bsz * seq, d)], axis=0)
    for i in range(depth):
        xa = _layer(i, i == depth - 1, xa, mod, bsz, seq, ctx_len, p)
    out = _final_norm(xa, g_final.reshape(1, d), 512, 0)
    return out.reshape(bsz, seq, d)
```

```python
import functools
import math

import jax
import jax.numpy as jnp
from jax import lax
from jax.experimental import pallas as pl
from jax.experimental.pallas import tpu as pltpu

F32 = jnp.float32
BF16 = jnp.bfloat16

D_MODEL = 2048
GRID_W = 64
NORM_EPS = 1e-6

S5_WIDTH = 1024
S5_GROUP = 16
S5_GROUPS = 64
S5_STATE = 64
S5_MAX_RE = -1e-4
S5_TILE = 16
S5_PAIRS = S5_GROUPS // 2

RW_WIDTH = 1024
RW_HEAD = 64
RW_DECAY_SCALE = 0.606531
RW_GN_EPS = 64e-5
RW_BLOCK = 256
RW_CHUNK = 16
RW_PAIRS = RW_WIDTH // 128
RW_Z = 3 * RW_WIDTH + 128

GLA_HEADS = 4
GLA_DK = 128
GLA_DV = 256
GLA_QK = 512
GLA_V = 1024
GLA_TAU = 16.0
GLA_CHUNK = 64

MOE_GROUPS = 4
MOE_PER_GROUP = 8
MOE_EXPERTS = 32
MOE_HIDDEN = 256
MOE_LANE0 = MOE_GROUPS

LANES = 128
VMEM_LIMIT = 56 * 1024 * 1024


def _cp(*sem):
    return pltpu.CompilerParams(dimension_semantics=sem, vmem_limit_bytes=VMEM_LIMIT)


def _dot(a, b):
    return jnp.dot(a, b, preferred_element_type=F32)


def _dot_nt(a, b):
    return lax.dot_general(a, b, (((1,), (1,)), ((), ())), preferred_element_type=F32)


def _dot_tn(a, b):
    return lax.dot_general(a, b, (((0,), (0,)), ((), ())), preferred_element_type=F32)


def _sigmoid(x):
    return 1.0 / (1.0 + jnp.exp(-x))


def _silu(x):
    return x * _sigmoid(x)


def _gelu_tanh(x):
    return 0.5 * x * (1.0 + jnp.tanh(math.sqrt(2.0 / math.pi) * (x + 0.044715 * (x * x * x))))


def _split_hi_lo(x):
    hi = x.astype(BF16)
    lo = (x - hi.astype(F32)).astype(BF16)
    return hi, lo


def _iota(shape, dim):
    return lax.broadcasted_iota(jnp.int32, shape, dim)


def _head_ones(width):
    return (_iota((LANES, LANES), 0) // width == _iota((LANES, LANES), 1) // width).astype(BF16)


def _head_sum(x, ones, exact):
    if exact:
        hi, lo = _split_hi_lo(x)
        return _dot(hi, ones) + _dot(lo, ones)
    return _dot(x.astype(BF16), ones)


def _adaln_kernel(c_ref, w_ref, b_ref, o_ref):
    c = c_ref[...]
    o_ref[...] = _dot(_silu(c).astype(BF16), w_ref[...].astype(BF16)) + b_ref[...]


def _adaln(cc, w_mod, b_mod):
    depth, d, n = w_mod.shape
    bn = 1536
    return pl.pallas_call(
        _adaln_kernel,
        out_shape=jax.ShapeDtypeStruct((depth, 8, n), F32),
        grid=(depth, n // bn),
        in_specs=[
            pl.BlockSpec((8, d), lambda l, j: (0, 0)),
            pl.BlockSpec((None, d, bn), lambda l, j: (l, 0, j)),
            pl.BlockSpec((None, 1, bn), lambda l, j: (l, 0, j)),
        ],
        out_specs=pl.BlockSpec((None, 8, bn), lambda l, j: (l, 0, j)),
        compiler_params=_cp("parallel", "parallel"),
        name="adaln",
    )(cc, w_mod, b_mod.reshape(depth, 1, n))


def _mod_spec(layer, part, bm, nc, seq):
    def index(i, *_):
        r0 = i * bm
        return (layer, jnp.where(r0 < nc, 2, (r0 - nc) // seq), part, 0, 0)
    return pl.BlockSpec((None, None, None, 1, D_MODEL), index)


def _normmod_kernel(x_ref, g_ref, sh_ref, sc_ref, o_ref):
    x = x_ref[...]
    y = x * lax.rsqrt(jnp.mean(x * x, axis=-1, keepdims=True) + NORM_EPS) * g_ref[...]
    o_ref[...] = (y * (1.0 + sc_ref[...]) + sh_ref[...]).astype(o_ref.dtype)


def _normmod(x, g, mod, layer, parts, bm, nc, seq):
    n, d = x.shape
    return pl.pallas_call(
        _normmod_kernel,
        out_shape=jax.ShapeDtypeStruct((n, d), BF16),
        grid=(n // bm,),
        in_specs=[
            pl.BlockSpec((bm, d), lambda i: (i, 0)),
            pl.BlockSpec((None, 1, d), lambda i: (layer, 0, 0)),
            _mod_spec(layer, parts[0], bm, nc, seq),
            _mod_spec(layer, parts[1], bm, nc, seq),
        ],
        out_specs=pl.BlockSpec((bm, d), lambda i: (i, 0)),
        compiler_params=_cp("parallel"),
        name="normmod",
    )(x, g, mod, mod)


def _mm_kernel(x_ref, w_ref, o_ref):
    o_ref[...] = _dot(x_ref[...], w_ref[...]).astype(o_ref.dtype)


def _mm(x, w, bm, bn, out_dtype=F32):
    m, k = x.shape
    n = w.shape[1]
    return pl.pallas_call(
        _mm_kernel,
        out_shape=jax.ShapeDtypeStruct((m, n), out_dtype),
        grid=(m // bm, n // bn),
        in_specs=[pl.BlockSpec((bm, k), lambda i, j: (i, 0)), pl.BlockSpec((k, bn), lambda i, j: (0, j))],
        out_specs=pl.BlockSpec((bm, bn), lambda i, j: (i, j)),
        compiler_params=_cp("parallel", "parallel"),
        name="mm",
    )(x, w)


def _seq_blocks(bsz, nc_blocks, nl_blocks, reverse):
    def local(s):
        lat = nc_blocks + ((nl_blocks - 1 - (s - nc_blocks)) if reverse else (s - nc_blocks))
        ctx = (nc_blocks - 1 - s) if reverse else s
        return jnp.where(s < nc_blocks, ctx, lat)

    def block(b, s):
        i = local(s)
        return jnp.where(i < nc_blocks, b * nc_blocks + i, bsz * nc_blocks + b * nl_blocks + i - nc_blocks)

    return local, block


S5_PACK = 8
S5_PACKS = S5_GROUPS // S5_PACK
S5_PLANE = S5_PACK * S5_STATE
S5_BLOCK = 256
S5_ROWS = 24


def _s5_tables(a_re, a_im, log_dt, b_re, b_im, c_re, c_im):
    lam = lax.complex(jnp.minimum(a_re, S5_MAX_RE), a_im)
    ldt = lam * jnp.exp(log_dt)[:, None]
    lam_bar = jnp.exp(ldt)
    b_bar = ((lam_bar - 1.0) / lam)[..., None] * lax.complex(b_re, b_im)
    c_mat = lax.complex(c_re, c_im)
    eye = jnp.eye(S5_PACK, dtype=F32)

    def block_diag(t):
        k, g, a, b = t.shape
        return (t[:, :, :, None, :] * eye[None, :, None, :, None]).reshape(k, g * a, g * b)

    b_t = jnp.transpose(b_bar, (0, 2, 1)).reshape(S5_PACKS, S5_PACK, S5_GROUP, S5_STATE)
    bblk = jnp.concatenate([block_diag(b_t.real), block_diag(b_t.imag)], axis=2)
    c_t = jnp.transpose(c_mat, (0, 2, 1)).reshape(S5_PACKS, S5_PACK, S5_STATE, S5_GROUP)
    cblk = jnp.concatenate([block_diag(c_t.real), block_diag(-c_t.imag)], axis=1)
    expo = jnp.concatenate([jnp.arange(1, S5_TILE + 1, dtype=F32), jnp.asarray([32.0, 64.0, 128.0], F32),
                            jnp.zeros((S5_ROWS - S5_TILE - 3,), F32)])
    pw = jnp.exp(ldt[None] * expo[:, None, None]).reshape(S5_ROWS, S5_PACKS, S5_PLANE)
    pw = jnp.transpose(pw, (1, 0, 2))
    return bblk.astype(BF16), cblk.astype(BF16), pw.real, pw.imag


def _s5_scan_kernel(u_ref, bblk_ref, cblk_ref, pre_ref, pim_ref, y_ref, car_ref, up_s, h_s, hb_s, yp_s, *, reverse):
    t = S5_TILE
    n_t = S5_BLOCK // t
    pn = S5_PLANE

    @pl.when(pl.program_id(1) == 0)
    def _():
        car_ref[...] = jnp.zeros_like(car_ref)

    ra = _iota((S5_BLOCK, S5_BLOCK), 0)
    cb = _iota((S5_BLOCK, S5_BLOCK), 1)
    perm = ((ra // t == cb % t) & (ra % t == cb // t)).astype(BF16)
    up = _dot(perm, u_ref[...].astype(BF16)).astype(BF16)
    for pk in range(S5_PACKS):
        up_s[pk] = up[:, LANES * pk:LANES * (pk + 1)]
    order = list(range(t - 1, -1, -1)) if reverse else list(range(t))
    rowj = _iota((n_t, 1), 0)

    def pack_body(pk, carry):
        bu = _dot(up_s[pk], bblk_ref[pk])
        pre = pre_ref[pk]
        pim = pim_ref[pk]
        l_re, l_im = pre[0:1], pim[0:1]
        h_re = h_im = None
        for n, s in enumerate(order):
            rows = slice(t * s, t * (s + 1))
            b_re, b_im = bu[rows, :pn], bu[rows, pn:]
            if n == 0:
                h_re, h_im = b_re, b_im
            else:
                h_re, h_im = l_re * h_re - l_im * h_im + b_re, l_re * h_im + l_im * h_re + b_im
            h_s[rows, :pn] = h_re
            h_s[rows, pn:] = h_im
        c_re, c_im = car_ref[pk, 0:1, :pn], car_ref[pk, 0:1, pn:]
        first = rowj == (n_t - 1 if reverse else 0)
        g_re, g_im = pre[t - 1:t], pim[t - 1:t]
        e_re = h_re + jnp.where(first, g_re * c_re - g_im * c_im, 0.0)
        e_im = h_im + jnp.where(first, g_re * c_im + g_im * c_re, 0.0)
        step = 1
        for row in (t - 1, t, t + 1, t + 2):
            if reverse:
                s_re, s_im, ok = pltpu.roll(e_re, n_t - step, 0), pltpu.roll(e_im, n_t - step, 0), rowj < n_t - step
            else:
                s_re, s_im, ok = pltpu.roll(e_re, step, 0), pltpu.roll(e_im, step, 0), rowj >= step
            a_re, a_im = pre[row:row + 1], pim[row:row + 1]
            e_re = e_re + jnp.where(ok, a_re * s_re - a_im * s_im, 0.0)
            e_im = e_im + jnp.where(ok, a_re * s_im + a_im * s_re, 0.0)
            step *= 2
        last = 0 if reverse else n_t - 1
        car_ref[pk, 0:1, :pn] = e_re[last:last + 1]
        car_ref[pk, 0:1, pn:] = e_im[last:last + 1]
        if reverse:
            in_re = jnp.where(first, c_re, pltpu.roll(e_re, n_t - 1, 0))
            in_im = jnp.where(first, c_im, pltpu.roll(e_im, n_t - 1, 0))
        else:
            in_re = jnp.where(first, c_re, pltpu.roll(e_re, 1, 0))
            in_im = jnp.where(first, c_im, pltpu.roll(e_im, 1, 0))
        for n, s in enumerate(order):
            rows = slice(t * s, t * (s + 1))
            a_re, a_im = pre[n:n + 1], pim[n:n + 1]
            hb_s[rows, :pn] = (h_s[rows, :pn] + a_re * in_re - a_im * in_im).astype(BF16)
            hb_s[rows, pn:] = (h_s[rows, pn:] + a_re * in_im + a_im * in_re).astype(BF16)
        yp_s[pk] = _dot(hb_s[...], cblk_ref[pk]).astype(BF16)
        return carry

    lax.fori_loop(0, S5_PACKS, pack_body, 0)
    for pk in range(S5_PACKS):
        y_ref[:, LANES * pk:LANES * (pk + 1)] = _dot(perm, yp_s[pk])


def _s5_scan(u_src, tables, bsz, nc_blocks, nl_blocks, reverse):
    bblk, cblk, pre, pim = tables
    rows = u_src.shape[0]
    blk = S5_BLOCK
    _, block = _seq_blocks(bsz, nc_blocks, nl_blocks, reverse)
    whole = lambda a: pl.BlockSpec(a.shape, lambda b, s: (0,) * a.ndim)
    return pl.pallas_call(
        functools.partial(_s5_scan_kernel, reverse=reverse),
        out_shape=jax.ShapeDtypeStruct((rows, S5_WIDTH), F32),
        grid=(bsz, nc_blocks + nl_blocks),
        in_specs=[pl.BlockSpec((blk, S5_WIDTH), lambda b, s: (block(b, s), 0)),
                  whole(bblk), whole(cblk), whole(pre), whole(pim)],
        out_specs=pl.BlockSpec((blk, S5_WIDTH), lambda b, s: (block(b, s), 0)),
        scratch_shapes=[
            pltpu.VMEM((S5_PACKS, 8, 2 * S5_PLANE), F32),
            pltpu.VMEM((S5_PACKS, blk, LANES), BF16),
            pltpu.VMEM((blk, 2 * S5_PLANE), F32),
            pltpu.VMEM((blk, 2 * S5_PLANE), BF16),
            pltpu.VMEM((S5_PACKS, blk, LANES), BF16),
        ],
        compiler_params=_cp("parallel", "arbitrary"),
        name="s5_bwd" if reverse else "s5_fwd",
    )(u_src, bblk, cblk, pre, pim)


def _gla_kernel(q_ref, k_ref, v_ref, gk_ref, up_ref, gb_ref, s0_ref, o_ref, sf_ref, st_ref):
    d = pl.program_id(0)
    c = pl.program_id(2)
    n = pl.num_programs(2)

    @pl.when(c == 0)
    def _():
        st_ref[...] = s0_ref[...]

    cs = GLA_CHUNK
    sign = 1 - 2 * d
    row = _iota((cs, cs), 0)
    col = _iota((cs, cs), 1)
    causal = (row - col) * sign >= 0
    causal_b = causal.astype(BF16)
    x = _dot(gk_ref[...].astype(BF16), up_ref[...]) + gb_ref[...]
    log_a = (jnp.minimum(x, 0.0) - jnp.log(1.0 + jnp.exp(-jnp.abs(x)))) * (1.0 / GLA_TAU)
    la_hi, la_lo = _split_hi_lo(log_a)
    bcum = _dot(causal_b, la_hi) + _dot(causal_b, la_lo)
    rid = _iota((cs, 1), 0)
    b_mid = jnp.sum(jnp.where(rid == cs // 2 - d, bcum, 0.0), axis=0, keepdims=True)
    b_end = jnp.sum(jnp.where(rid == (cs - 1) * (1 - d), bcum, 0.0), axis=0, keepdims=True)
    q = q_ref[...] * (GLA_DK ** -0.5)
    k = k_ref[...]
    v = v_ref[...]
    for h in range(GLA_HEADS):
        ks = slice(GLA_DK * h, GLA_DK * (h + 1))
        vs = slice(GLA_DV * h, GLA_DV * (h + 1))
        bh = bcum[:, ks]
        qh = q[:, ks]
        kh = k[:, ks]
        vh = v[:, vs].astype(BF16)
        s_t = st_ref[h]
        q_in = (qh * jnp.exp(bh - b_mid[:, ks])).astype(BF16)
        k_in = (kh * jnp.exp(b_mid[:, ks] - bh)).astype(BF16)
        scores = jnp.where(causal, _dot_nt(q_in, k_in), 0.0).astype(BF16)
        q_st = (qh * jnp.exp(bh)).astype(BF16)
        o_ref[:, vs] = _dot(scores, vh) + _dot_nt(q_st, s_t.astype(BF16))
        k_out = (kh * jnp.exp(b_end[:, ks] - bh)).astype(BF16)
        st_ref[h] = s_t * jnp.exp(b_end[:, ks]) + _dot_tn(vh, k_out)

    @pl.when(c == n - 1)
    def _():
        sf_ref[...] = st_ref[...]


def _gla(qk, v, gk, gk_up, gk_b, s0):
    bsz, cs, w = v.shape
    n = w // GLA_V

    def chunk(d, c):
        return c + d * (n - 1 - 2 * c)

    state_spec = pl.BlockSpec((None, None, GLA_HEADS, GLA_DV, GLA_DK), lambda d, b, c: (d, b, 0, 0, 0))
    return pl.pallas_call(
        _gla_kernel,
        out_shape=(jax.ShapeDtypeStruct((2, bsz, cs, w), F32), jax.ShapeDtypeStruct(s0.shape, F32)),
        grid=(2, bsz, n),
        in_specs=[
            pl.BlockSpec((None, cs, GLA_QK), lambda d, b, c: (b, 0, 2 * chunk(d, c))),
            pl.BlockSpec((None, cs, GLA_QK), lambda d, b, c: (b, 0, 2 * chunk(d, c) + 1)),
            pl.BlockSpec((None, cs, GLA_V), lambda d, b, c: (b, 0, chunk(d, c))),
            pl.BlockSpec((None, cs, LANES), lambda d, b, c: (b, 0, chunk(d, c))),
            pl.BlockSpec((None, LANES, GLA_QK), lambda d, b, c: (d, 0, 0)),
            pl.BlockSpec((None, 1, GLA_QK), lambda d, b, c: (d, 0, 0)),
            state_spec,
        ],
        out_specs=(pl.BlockSpec((None, None, cs, GLA_V), lambda d, b, c: (d, b, 0, chunk(d, c))), state_spec),
        scratch_shapes=[pltpu.VMEM((GLA_HEADS, GLA_DV, GLA_DK), F32)],
        compiler_params=_cp("parallel", "parallel", "arbitrary"),
        name="gla",
    )(qk, qk, v, gk, gk_up, gk_b, s0)


def _to_chunk_major(t, bsz):
    n = t.shape[0] // bsz
    d = t.shape[1]
    t = t.reshape(bsz, n // GLA_CHUNK, GLA_CHUNK, d)
    return jnp.transpose(t, (0, 2, 1, 3)).reshape(bsz, GLA_CHUNK, (n // GLA_CHUNK) * d)


def _from_chunk_major(t, d):
    lead = t.shape[:-2]
    n = t.shape[-1] // d
    t = t.reshape(lead + (GLA_CHUNK, n, d))
    return jnp.swapaxes(t, -3, -2).reshape(lead + (n * GLA_CHUNK, d))


def _rwkv_kernel(z0_ref, z1_ref, halo0_ref, halo1_ref, mu_ref, w0_ref, wup_ref, a0_ref, aup_ref, kk_ref, ka_ref,
                 rk_ref, y_ref, st_ref, kt_s, bt_s, kq_s, rt_s, v_s, w_s, u_s, ya_s, ab_s, pin_s,
                 *, reverse, nc_blocks):
    step = pl.program_id(0)
    blk = RW_BLOCK
    ch = RW_CHUNK
    n_ch = blk // ch
    n_b = 2
    n_units = n_b * RW_PAIRS

    @pl.when(step == 0)
    def _():
        st_ref[...] = jnp.zeros_like(st_ref)

    rowi = _iota((blk, 1), 0)
    low = _iota((1, LANES), 1) < RW_HEAD
    ones64 = _head_ones(RW_HEAD)
    pos = rowi % ch
    seq_start = (step == 0) | (step == nc_blocks)

    for b, (z_ref, halo_ref) in enumerate(((z0_ref, halo0_ref), (z1_ref, halo1_ref))):
        z = z_ref[...]
        if reverse:
            prev = pltpu.roll(z, blk - 1, 0)
            edge = halo_ref[0:1, :]
            at_edge = rowi == blk - 1
        else:
            prev = pltpu.roll(z, 1, 0)
            edge = halo_ref[7:8, :]
            at_edge = rowi == 0
        prev = jnp.where(at_edge, jnp.where(seq_start, 0.0, edge), prev)
        zs = z + (prev - z) * mu_ref[...]
        r = zs[:, 0:RW_WIDTH]
        k = zs[:, RW_WIDTH:2 * RW_WIDTH]
        v = zs[:, 2 * RW_WIDTH:3 * RW_WIDTH]
        lora = zs[:, 3 * RW_WIDTH:3 * RW_WIDTH + LANES]
        lora_w = jnp.where(low, jnp.tanh(lora), 0.0).astype(BF16)
        lora_a = jnp.where(low, 0.0, lora).astype(BF16)
        logw = -RW_DECAY_SCALE * _sigmoid(w0_ref[...] + _dot(lora_w, wup_ref[...]))
        a = _sigmoid(a0_ref[...] + _dot(lora_a, aup_ref[...]))
        kk = k * kk_ref[...]
        kp = k * (1.0 + (a - 1.0) * ka_ref[...])
        rkb = r * kp * rk_ref[...]

        cl = logw
        sh = 1
        while sh < ch:
            if reverse:
                cl = cl + jnp.where(pos < ch - sh, pltpu.roll(cl, blk - sh, 0), 0.0)
            else:
                cl = cl + jnp.where(pos >= sh, pltpu.roll(cl, sh, 0), 0.0)
            sh *= 2
        p_in = jnp.exp(cl)
        pin_s[b] = p_in
        p_ex = jnp.exp(cl - logw)
        p_inv = jnp.exp(-cl)

        for p in range(RW_PAIRS):
            ls = slice(LANES * p, LANES * (p + 1))
            q = b * RW_PAIRS + p
            kkp = kk[:, ls]
            ssq = _head_sum(kkp * kkp, ones64, False)
            kkn = kkp * (1.0 / jnp.maximum(jnp.sqrt(ssq), 1e-12))
            kt_s[q] = (kkn * p_ex[:, ls]).astype(BF16)
            bt_s[q] = (kkn * a[:, ls] * p_inv[:, ls]).astype(BF16)
            kq_s[q] = (kp[:, ls] * p_inv[:, ls]).astype(BF16)
            rt_s[q] = (r[:, ls] * p_in[:, ls]).astype(BF16)
            v_s[q] = v[:, ls].astype(BF16)
            ya_s[q] = _head_sum(rkb[:, ls], ones64, False) * v[:, ls]

    rr = _iota((blk, blk), 0)
    cc = _iota((blk, blk), 1)
    same = rr // ch == cc // ch
    before = (cc > rr) if reverse else (cc < rr)
    strict = same & before
    incl = same & (before | (rr == cc))
    fold0 = (_iota((blk, LANES), 0) % ch == _iota((blk, LANES), 1)).astype(BF16)
    fold1 = (_iota((blk, LANES), 0) % ch + ch == _iota((blk, LANES), 1)).astype(BF16)
    lane_lo = _iota((1, LANES), 1) < RW_HEAD

    def pair_body(p, carry):
        units = [b * RW_PAIRS + p for b in range(n_b)]
        heads = [(u, hh) for u in range(n_b) for hh in range(2)]
        kt = [kt_s[q] for q in units]
        rt = [rt_s[q] for q in units]
        vv = [v_s[q] for q in units]
        ya = [ya_s[q] for q in units]
        rhs = [jnp.concatenate([bt_s[q], kq_s[q]], axis=0) for q in units]
        g = []
        for u, hh in heads:
            mine = lane_lo if hh == 0 else jnp.logical_not(lane_lo)
            zero = jnp.zeros_like(kt[u])
            lhs = jnp.concatenate([jnp.where(mine, kt[u], zero), jnp.where(mine, rt[u], zero)], axis=0)
            g.append(_dot_nt(lhs, rhs[u]))
        n1 = [jnp.where(strict, -gh[:blk, :blk], 0.0).astype(BF16) for gh in g]
        a_kq = [jnp.where(strict, gh[:blk, blk:], 0.0).astype(BF16) for gh in g]
        a_rb = [jnp.where(incl, gh[blk:, :blk], 0.0).astype(BF16) for gh in g]
        a_rq = [jnp.where(incl, gh[blk:, blk:], 0.0).astype(BF16) for gh in g]
        n2 = [_dot(n, n).astype(BF16) for n in n1]
        akv = [_dot(a, vv[u]) for a, (u, _) in zip(a_kq, heads)]
        n4 = [_dot(n, n).astype(BF16) for n in n2]
        y_in = [_dot(a, vv[u]) for a, (u, _) in zip(a_rq, heads)]
        n8 = [_dot(n, n).astype(BF16) for n in n4]
        fold = [_dot(a, fold0 if hh == 0 else fold1) for a, (_, hh) in zip(a_rb, heads)]
        rhs_t = [jnp.concatenate([kt[u].astype(F32), av], axis=1) for av, (u, _) in zip(akv, heads)]
        for nk in (n8, n4, n2, n1):
            rhs_t = [x + _dot(n, x.astype(BF16)) for n, x in zip(nk, rhs_t)]
        for u, q in enumerate(units):
            h0, h1 = 2 * u, 2 * u + 1
            w_s[q] = jnp.where(lane_lo, rhs_t[h0][:, :LANES], rhs_t[h1][:, :LANES]).astype(BF16)
            u_s[q] = jnp.where(lane_lo, rhs_t[h0][:, LANES:], rhs_t[h1][:, LANES:])
            ya_s[q] = ya[u] + jnp.where(lane_lo, y_in[h0], y_in[h1])
            ab_s[q] = (fold[h0] + fold[h1]).astype(BF16)
        return carry

    lax.fori_loop(0, RW_PAIRS, pair_body, 0)

    blockdiag = (_iota((LANES, LANES), 0) // RW_HEAD) == (_iota((LANES, LANES), 1) // RW_HEAD)
    end_row = 0 if reverse else ch - 1

    def chunk_body(i, carry):
        c = (n_ch - 1 - i) if reverse else i
        rows = pl.ds(pl.multiple_of(c * ch, ch), ch)
        s_old = [st_ref[q] for q in range(n_units)]
        m1 = [_dot_nt(jnp.concatenate([w_s[q, rows, :], rt_s[q, rows, :]], axis=0), s_old[q].astype(BF16))
              for q in range(n_units)]
        zc = [-(m1[q][:ch] + u_s[q, rows, :]) for q in range(n_units)]
        upd = []
        for q in range(n_units):
            zv = jnp.concatenate([zc[q].astype(BF16), v_s[q, rows, :]], axis=0)
            bk = jnp.concatenate([bt_s[q, rows, :], kq_s[q, rows, :]], axis=0)
            upd.append(_dot_tn(zv, bk))
        yc = []
        for q in range(n_units):
            z2 = jnp.concatenate([jnp.where(lane_lo, zc[q], 0.0), jnp.where(lane_lo, 0.0, zc[q])], axis=0)
            yc.append(m1[q][ch:] + _dot(ab_s[q, rows, :][:, :2 * ch], z2.astype(BF16)) + ya_s[q, rows, :])
        for q in range(n_units):
            b, p = divmod(q, RW_PAIRS)
            p_end = pin_s[b, rows, LANES * p:LANES * (p + 1)][end_row:end_row + 1]
            st_ref[q] = (s_old[q] + jnp.where(blockdiag, upd[q], 0.0)) * p_end
            y_ref[b, rows, LANES * p:LANES * (p + 1)] = yc[q]
        return carry

    lax.fori_loop(0, n_ch, chunk_body, 0)


def _rwkv_direction(zall, prm, bsz, nc_blocks, nl_blocks, reverse):
    assert bsz == 2
    mu, w0, wup, a0, aup, k_k, k_a, r_k = prm
    rows = zall.shape[0]
    blk = RW_BLOCK
    steps = nc_blocks + nl_blocks
    n_blocks = rows // blk

    def local(s):
        lat = nc_blocks + ((nl_blocks - 1 - (s - nc_blocks)) if reverse else (s - nc_blocks))
        ctx = (nc_blocks - 1 - s) if reverse else s
        return jnp.where(s < nc_blocks, ctx, lat)

    def block(b, s):
        i = local(s)
        return jnp.where(i < nc_blocks, b * nc_blocks + i, bsz * nc_blocks + b * nl_blocks + i - nc_blocks)

    def halo(b, s):
        i = block(b, s)
        if reverse:
            return jnp.minimum((i + 1) * (blk // 8), n_blocks * (blk // 8) - 1)
        return jnp.maximum(i * (blk // 8) - 1, 0)

    vec = lambda w: pl.BlockSpec((1, w), lambda s: (0, 0))
    mat = lambda: pl.BlockSpec((LANES, RW_WIDTH), lambda s: (0, 0))
    n_units = bsz * RW_PAIRS
    unit_bf = pltpu.VMEM((n_units, blk, LANES), BF16)
    unit_f = pltpu.VMEM((n_units, blk, LANES), F32)
    z_spec = lambda b: pl.BlockSpec((blk, RW_Z), lambda s: (block(b, s), 0))
    halo_spec = lambda b: pl.BlockSpec((8, RW_Z), lambda s: (halo(b, s), 0))
    return pl.pallas_call(
        functools.partial(_rwkv_kernel, reverse=reverse, nc_blocks=nc_blocks),
        out_shape=jax.ShapeDtypeStruct((bsz, steps * blk, RW_WIDTH), F32),
        grid=(steps,),
        in_specs=[
            z_spec(0), z_spec(1), halo_spec(0), halo_spec(1),
            vec(RW_Z), vec(RW_WIDTH), mat(), vec(RW_WIDTH), mat(), vec(RW_WIDTH), vec(RW_WIDTH), vec(RW_WIDTH),
        ],
        out_specs=pl.BlockSpec((bsz, blk, RW_WIDTH), lambda s: (0, local(s), 0)),
        scratch_shapes=[
            pltpu.VMEM((n_units, LANES, LANES), F32),
            unit_bf, unit_bf, unit_bf, unit_bf, unit_bf,
            unit_bf, unit_f, unit_f, unit_bf,
            pltpu.VMEM((bsz, blk, RW_WIDTH), F32),
        ],
        compiler_params=_cp("arbitrary"),
        name="rwkv_bwd" if reverse else "rwkv_fwd",
    )(zall, zall, zall, zall, mu, w0, wup, a0, aup, k_k, k_a, r_k)


def _branches_kernel(saf_ref, sab_ref, su_ref, sd_ref, rf_ref, rb_ref, rg_ref, gup_ref, lw_ref, lb_ref,
                     of_ref, ob_ref, og_ref, ng_ref, ga_ref, gb_ref, gc_ref, wglu_ref, wrw_ref, wgl_ref, m_ref):
    za = _gelu_tanh(saf_ref[...] + sab_ref[...] + sd_ref[...] * su_ref[...]).astype(BF16)
    hid = _dot(za, wglu_ref[...])
    m = _sigmoid(ga_ref[...]) * (hid[:, :D_MODEL] * _sigmoid(hid[:, D_MODEL:]))
    ones64 = _head_ones(RW_HEAD)
    gate = _dot(_sigmoid(rg_ref[...]).astype(BF16), gup_ref[...])
    zb = []
    for p in range(RW_PAIRS):
        ls = slice(LANES * p, LANES * (p + 1))
        y = rf_ref[:, ls] + rb_ref[:, ls]
        mean = _head_sum(y, ones64, True) * (1.0 / RW_HEAD)
        yc = y - mean
        var = _head_sum(yc * yc, ones64, True) * (1.0 / RW_HEAD)
        yn = yc * lax.rsqrt(var + RW_GN_EPS) * lw_ref[:, ls] + lb_ref[:, ls]
        zb.append((yn * gate[:, ls]).astype(BF16))
    m = m + _sigmoid(gb_ref[...]) * _dot(jnp.concatenate(zb, axis=1), wrw_ref[...])
    zc = []
    for h in range(GLA_HEADS):
        vs = slice(GLA_DV * h, GLA_DV * (h + 1))
        o = of_ref[:, vs] + ob_ref[:, vs]
        on = o * lax.rsqrt(jnp.mean(o * o, axis=-1, keepdims=True) + NORM_EPS) * ng_ref[:, vs]
        zc.append((on * _silu(og_ref[:, vs])).astype(BF16))
    m = m + _sigmoid(gc_ref[...]) * _dot(jnp.concatenate(zc, axis=1), wgl_ref[...])
    m_ref[...] = m.astype(BF16)


def _branches(ya, yb, o_dir, pa_in, zall, prm, row0, nc_blocks, nl_blocks):
    s5_d, g_up, ln_w, ln_b, norm_g, w_glu, w_rw, w_gl = prm
    bm = RW_BLOCK
    bsz = yb[0].shape[0]
    n = zall.shape[0] - row0 * bm
    rows = lambda w, blk=0: pl.BlockSpec((bm, w), lambda i: (i + row0, blk))

    def y_index(i):
        i = i + row0
        t = i - bsz * nc_blocks
        return (jnp.where(t < 0, i // nc_blocks, t // nl_blocks),
                jnp.where(t < 0, i % nc_blocks, nc_blocks + t % nl_blocks), 0)

    y_spec = pl.BlockSpec((None, bm, RW_WIDTH), y_index)
    vec = pl.BlockSpec((1, 1024), lambda i: (0, 0))
    const = lambda a: pl.BlockSpec(a.shape, lambda i: (0,) * a.ndim, pipeline_mode=pl.Buffered(1))
    return pl.pallas_call(
        _branches_kernel,
        out_shape=jax.ShapeDtypeStruct((n, D_MODEL), BF16),
        grid=(n // bm,),
        in_specs=[rows(1024), rows(1024), rows(1024, 0), vec,
                  y_spec, y_spec, rows(LANES, RW_Z // LANES), const(g_up), vec, vec,
                  rows(1024), rows(1024), rows(1024, 1), vec,
                  rows(D_MODEL, 1), rows(D_MODEL, 2), rows(D_MODEL, 3),
                  const(w_glu), const(w_rw), const(w_gl)],
        out_specs=pl.BlockSpec((bm, D_MODEL), lambda i: (i, 0)),
        compiler_params=_cp("parallel"),
        name="branches",
    )(ya[0], ya[1], pa_in, s5_d, yb[0], yb[1], zall, g_up, ln_w, ln_b,
      o_dir[0], o_dir[1], pa_in, norm_g, pa_in, pa_in, pa_in, w_glu, w_rw, w_gl)


def _out_proj_kernel(m_ref, w_ref, x_ref, gate_ref, o_ref):
    o_ref[...] = x_ref[...] + gate_ref[...] * _dot(m_ref[...], w_ref[...])


def _out_proj(m, w_out, x, mod, layer, bm, row0, nc, seq):
    n, d = m.shape
    bn = 1024
    mod_index = _mod_spec(layer, 2, bm, nc, seq).index_map
    return pl.pallas_call(
        _out_proj_kernel,
        out_shape=jax.ShapeDtypeStruct((n, d), F32),
        grid=(n // bm, d // bn),
        in_specs=[pl.BlockSpec((bm, d), lambda i, j: (i, 0)),
                  pl.BlockSpec((d, bn), lambda i, j: (0, j)),
                  pl.BlockSpec((bm, bn), lambda i, j: (i + row0, j)),
                  pl.BlockSpec((None, None, None, 1, bn), lambda i, j: mod_index(i + row0)[:4] + (j,))],
        out_specs=pl.BlockSpec((bm, bn), lambda i, j: (i, j)),
        compiler_params=_cp("parallel", "parallel"),
        name="out_proj",
    )(m, w_out, x, mod)


MOE_TM = 256
MOE_ROW_TILES = D_MODEL // LANES
META_E1, META_E2, META_R1, META_R2, META_W1, META_W2 = range(6)


def _router_kernel(x_ref, g_ref, sh_ref, sc_ref, wr_hi_ref, wr_lo_ref, br_ref, v_ref, meta_ref, meta_t_ref, cnt_ref,
                   base_s):
    @pl.when(pl.program_id(0) == 0)
    def _():
        base_s[...] = jnp.zeros_like(base_s)

    x = x_ref[...]
    y = x * lax.rsqrt(jnp.mean(x * x, axis=-1, keepdims=True) + NORM_EPS) * g_ref[...]
    t = y * (1.0 + sc_ref[...]) + sh_ref[...]
    for s in range(MOE_ROW_TILES):
        v_ref[pl.ds(s, x.shape[0], stride=MOE_ROW_TILES), :] = t[:, LANES * s:LANES * (s + 1)]
    t_hi, t_lo = _split_hi_lo(t)
    logits = (_dot(t_hi, wr_hi_ref[...]) + _dot(t_lo, wr_hi_ref[...]) + _dot(t_hi, wr_lo_ref[...])) + br_ref[...]
    lane = _iota(logits.shape, 1).astype(F32)
    neg = jnp.float32(-jnp.inf)
    big = jnp.float32(LANES)
    l1 = jnp.where(lane < MOE_GROUPS, logits, neg)
    m1 = jnp.max(l1, axis=-1, keepdims=True)
    p_top = 1.0 / jnp.sum(jnp.exp(l1 - m1), axis=-1, keepdims=True)
    grp = jnp.min(jnp.where(l1 == m1, lane, big), axis=-1, keepdims=True)
    lo = MOE_LANE0 + MOE_PER_GROUP * grp
    in_grp = (lane >= lo) & (lane < lo + MOE_PER_GROUP)
    l2 = jnp.where(in_grp, logits, neg)
    v1 = jnp.max(l2, axis=-1, keepdims=True)
    i1 = jnp.min(jnp.where(l2 == v1, lane, big), axis=-1, keepdims=True)
    l3 = jnp.where(lane == i1, neg, l2)
    v2 = jnp.max(l3, axis=-1, keepdims=True)
    i2 = jnp.min(jnp.where(l3 == v2, lane, big), axis=-1, keepdims=True)
    e2 = jnp.exp(v2 - v1)
    w1 = p_top / (1.0 + e2)
    w2 = p_top * e2 / (1.0 + e2)
    pick1 = lane == i1
    pick2 = lane == i2
    chosen = jnp.where(pick1 | pick2, 1.0, 0.0)
    bm = x.shape[0]
    earlier = (_iota((bm, bm), 1) < _iota((bm, bm), 0)).astype(BF16)
    before = _dot(earlier, chosen.astype(BF16)) + base_s[...]
    r1 = jnp.sum(jnp.where(pick1, before, 0.0), axis=-1, keepdims=True)
    r2 = jnp.sum(jnp.where(pick2, before, 0.0), axis=-1, keepdims=True)
    base_s[...] += jnp.sum(chosen, axis=0, keepdims=True)
    cnt_ref[...] = base_s[...]
    meta = jnp.zeros_like(logits)
    for slot, val in ((META_E1, i1 - MOE_LANE0), (META_E2, i2 - MOE_LANE0), (META_R1, r1), (META_R2, r2),
                      (META_W1, w1), (META_W2, w2)):
        meta = jnp.where(lane == slot, val, meta)
    meta_ref[...] = meta
    meta_t_ref[...] = meta.T[:8]


def _router(x, g, mod, layer, wr_hi, wr_lo, br, bm, nc, seq):
    n, d = x.shape
    return pl.pallas_call(
        _router_kernel,
        out_shape=(jax.ShapeDtypeStruct((n * MOE_ROW_TILES, LANES), F32), jax.ShapeDtypeStruct((n, LANES), F32),
                   jax.ShapeDtypeStruct((8, n), F32), jax.ShapeDtypeStruct((1, LANES), F32)),
        grid=(n // bm,),
        in_specs=[
            pl.BlockSpec((bm, d), lambda i: (i, 0)),
            pl.BlockSpec((None, 1, d), lambda i: (layer, 0, 0)),
            _mod_spec(layer, 3, bm, nc, seq), _mod_spec(layer, 4, bm, nc, seq),
            pl.BlockSpec((d, LANES), lambda i: (0, 0)), pl.BlockSpec((d, LANES), lambda i: (0, 0)),
            pl.BlockSpec((1, LANES), lambda i: (0, 0)),
        ],
        out_specs=(pl.BlockSpec((bm * MOE_ROW_TILES, LANES), lambda i: (i, 0)),
                   pl.BlockSpec((bm, LANES), lambda i: (i, 0)),
                   pl.BlockSpec((8, bm), lambda i: (0, i)), pl.BlockSpec((1, LANES), lambda i: (0, 0))),
        scratch_shapes=[pltpu.VMEM((1, LANES), F32)],
        compiler_params=_cp("arbitrary"),
        name="moe_router",
    )(x, g, mod, mod, wr_hi, wr_lo, br)


def _moe_plan(meta_t, cnt):
    tm = MOE_TM
    n_tok = meta_t.shape[1]
    counts = cnt[0, MOE_LANE0:MOE_LANE0 + MOE_EXPERTS].astype(jnp.int32)
    seg = ((counts + tm - 1) // tm) * tm
    ends = jnp.cumsum(seg)
    off = ends - seg
    rec = meta_t[:4].astype(jnp.int32)
    first_row = jnp.sum(jnp.where(rec[:2, None, :] == jnp.arange(MOE_EXPERTS)[None, :, None], off[None, :, None], 0),
                        axis=1)
    dest = first_row + rec[2:4]
    n_rows = 2 * n_tok + MOE_EXPERTS * tm
    n_tiles = n_rows // tm
    tile_e = jnp.sum(((jnp.arange(n_tiles) * tm)[:, None] >= ends[None, :]).astype(jnp.int32), axis=1)
    tile_e = jnp.minimum(tile_e, MOE_EXPERTS - 1)
    tok = jnp.broadcast_to(jnp.arange(n_tok, dtype=jnp.int32), (2, n_tok))
    src = jnp.zeros((n_rows,), jnp.int32).at[dest.reshape(-1)].set(tok.reshape(-1))
    return dest[0], dest[1], src, tile_e, (ends[-1] // tm).reshape(1)


def _row_copy(src_hbm, row, dst, slot, r, sem):
    return pltpu.make_async_copy(src_hbm.at[pl.ds(row, 1)], dst.at[slot, pl.ds(r, 1)], sem)


def _experts_kernel(src_ref, te_ref, nu_ref, v_hbm, wg_ref, wu_ref, wd_ref, y_ref, xbuf, sem):
    del te_ref
    i = pl.program_id(0)
    n_used = nu_ref[0]
    slot = i % 2

    nt = MOE_ROW_TILES

    def token_copy(tok, into, r):
        return pltpu.make_async_copy(v_hbm.at[pl.ds(tok * nt, nt)], xbuf.at[into, pl.ds(r * nt, nt)], sem.at[into])

    def gather(tile, into):
        for r in range(MOE_TM):
            token_copy(src_ref[tile * MOE_TM + r], into, r).start(priority=1)

    def expert():
        for r in range(MOE_TM):
            token_copy(0, slot, r).wait()
        t = jnp.concatenate([xbuf[slot, pl.ds(s, MOE_TM, stride=nt), :] for s in range(nt)], axis=1).astype(BF16)
        hid = _silu(_dot(t, wg_ref[...].astype(BF16))) * _dot(t, wu_ref[...].astype(BF16))
        y_ref[...] = _dot(hid.astype(BF16), wd_ref[...].astype(BF16))

    @pl.when((i == 0) & (n_used > 0))
    def _():
        gather(0, 0)

    @pl.when(i + 1 < n_used)
    def _():
        gather(i + 1, 1 - slot)
        expert()

    @pl.when(i + 1 == n_used)
    def _():
        expert()

    @pl.when(i >= n_used)
    def _():
        y_ref[...] = jnp.zeros_like(y_ref)


def _experts(v, src, tile_e, n_used, layer, w_gate, w_up, w_down):
    d = w_gate.shape[2]
    hdim = w_gate.shape[3]
    n_rows = src.shape[0]
    tm = MOE_TM
    by_expert = lambda i, src_r, te_r, nu_r: (layer, te_r[i], 0, 0)
    return pl.pallas_call(
        _experts_kernel,
        out_shape=jax.ShapeDtypeStruct((n_rows, d), F32),
        grid_spec=pltpu.PrefetchScalarGridSpec(
            num_scalar_prefetch=3,
            grid=(n_rows // tm,),
            in_specs=[
                pl.BlockSpec(memory_space=pl.ANY),
                pl.BlockSpec((None, None, d, hdim), by_expert),
                pl.BlockSpec((None, None, d, hdim), by_expert),
                pl.BlockSpec((None, None, hdim, d), by_expert),
            ],
            out_specs=pl.BlockSpec((tm, d), lambda i, *_: (i, 0)),
            scratch_shapes=[pltpu.VMEM((2, tm * MOE_ROW_TILES, LANES), F32), pltpu.SemaphoreType.DMA((2,))],
        ),
        compiler_params=_cp("arbitrary"),
        name="moe_experts",
    )(src, tile_e, n_used, v, w_gate, w_up, w_down)


def _combine_kernel(d1_ref, d2_ref, y_hbm, meta_ref, x_ref, gate_ref, o_ref, buf1, buf2, sem):
    i = pl.program_id(0)
    bm = x_ref.shape[0]
    slot = i % 2

    def gather(tile, into):
        for r in range(bm):
            t = tile * bm + r
            _row_copy(y_hbm, d1_ref[t], buf1, into, r, sem.at[0, into]).start(priority=0)
            _row_copy(y_hbm, d2_ref[t], buf2, into, r, sem.at[1, into]).start(priority=1)

    @pl.when(i == 0)
    def _():
        gather(0, 0)

    @pl.when(i + 1 < pl.num_programs(0))
    def _():
        gather(i + 1, 1 - slot)

    for r in range(bm):
        _row_copy(y_hbm, 0, buf1, slot, r, sem.at[0, slot]).wait()
        _row_copy(y_hbm, 0, buf2, slot, r, sem.at[1, slot]).wait()
    meta = meta_ref[...]
    lane = _iota(meta.shape, 1)
    w1 = jnp.sum(jnp.where(lane == META_W1, meta, 0.0), axis=-1, keepdims=True)
    w2 = jnp.sum(jnp.where(lane == META_W2, meta, 0.0), axis=-1, keepdims=True)
    o_ref[...] = x_ref[...] + gate_ref[...] * (w1 * buf1[slot] + w2 * buf2[slot])


def _combine(y, d1, d2, meta, x, mod, layer, bm, nc, seq):
    n, d = x.shape
    at_tile = lambda i, *_: (i, 0)
    mod_index = _mod_spec(layer, 5, bm, nc, seq).index_map
    return pl.pallas_call(
        _combine_kernel,
        out_shape=jax.ShapeDtypeStruct((n, d), F32),
        grid_spec=pltpu.PrefetchScalarGridSpec(
            num_scalar_prefetch=2,
            grid=(n // bm,),
            in_specs=[
                pl.BlockSpec(memory_space=pl.ANY),
                pl.BlockSpec((bm, LANES), at_tile),
                pl.BlockSpec((bm, d), at_tile),
                pl.BlockSpec((None, None, None, 1, d), lambda i, *_: mod_index(i)),
            ],
            out_specs=pl.BlockSpec((bm, d), at_tile),
            scratch_shapes=[pltpu.VMEM((2, bm, d), F32), pltpu.VMEM((2, bm, d), F32),
                            pltpu.SemaphoreType.DMA((2, 2))],
        ),
        compiler_params=_cp("arbitrary"),
        name="moe_combine",
    )(d1, d2, y, meta, x, mod)


def _final_norm_kernel(x_ref, g_ref, o_ref):
    x = x_ref[...]
    o_ref[...] = x * lax.rsqrt(jnp.mean(x * x, axis=-1, keepdims=True) + NORM_EPS) * g_ref[...]


def _final_norm(x, g, bm, row0):
    n, d = x.shape
    n -= row0 * bm
    return pl.pallas_call(
        _final_norm_kernel,
        out_shape=jax.ShapeDtypeStruct((n, d), F32),
        grid=(n // bm,),
        in_specs=[pl.BlockSpec((bm, d), lambda i: (i + row0, 0)), pl.BlockSpec((1, d), lambda i: (0, 0))],
        out_specs=pl.BlockSpec((bm, d), lambda i: (i, 0)),
        compiler_params=_cp("parallel"),
        name="final_norm",
    )(x, g)


_COL = dict(s5=0, rw=S5_WIDTH, rg=S5_WIDTH + 3 * RW_WIDTH + 128, q=4352, k=4864, v=5376, gk=6400, og=6416, gates=7440)


def _pad_rows(w, rows):
    return jnp.pad(w, ((0, rows - w.shape[0]), (0, 0)))


def _layer(i, last, xa, mod, bsz, seq, ctx_len, p):
    nc = bsz * ctx_len
    rows = xa.shape[0]
    bm = 512
    msel = dict(nc=nc, seq=seq)
    u = _normmod(xa, p['g_norm1'].reshape(-1, 1, D_MODEL), mod, i, (0, 1), bm, **msel)

    w_in = p['w_in'][i]
    col = lambda a, w: w_in[:, a:a + w].astype(BF16)
    bm_in = 1088 if rows % 1088 == 0 else bm
    w_a = jnp.concatenate([col(0, S5_WIDTH), col(_COL['og'], GLA_V), col(_COL['gates'], 3 * D_MODEL)], axis=1)
    w_z = jnp.concatenate([col(_COL['rw'], 3 * RW_WIDTH + 128), col(_COL['rg'], 128)], axis=1)
    w_qk = col(_COL['q'], 2 * GLA_QK)
    w_v = col(_COL['v'], GLA_V)
    w_gk = jnp.pad(col(_COL['gk'], 16), ((0, 0), (0, LANES - 16)))
    pa_in = _mm(u, w_a, bm_in, 1024)
    zall = _mm(u, w_z, bm_in, 1664)
    qk = _mm(u, w_qk, bm_in, 1024)
    vv = _mm(u, w_v, bm_in, 1024)
    gk = _mm(u, w_gk, bm_in, LANES)

    ya = []
    for d in range(2):
        tables = _s5_tables(*(p[k][i, d] for k in ('s5_a_re', 's5_a_im', 's5_log_dt', 's5_b_re', 's5_b_im',
                                                   's5_c_re', 's5_c_im')))
        ya.append(_s5_scan(pa_in, tables, bsz, ctx_len // S5_BLOCK, seq // S5_BLOCK, bool(d)))

    yb = []
    for d in range(2):
        prm = (
            jnp.pad(p['rw_mu'][i, d], (0, RW_Z - p['rw_mu'].shape[-1])).reshape(1, RW_Z),
            p['rw_w0'][i, d].reshape(1, -1),
            _pad_rows(p['rw_w_up'][i, d], LANES).astype(BF16),
            p['rw_a0'][i, d].reshape(1, -1),
            jnp.pad(p['rw_a_up'][i, d], ((RW_HEAD, 0), (0, 0))).astype(BF16),
            p['rw_k_k'][i].reshape(1, -1), p['rw_k_a'][i].reshape(1, -1), p['rw_r_k'][i].reshape(1, -1),
        )
        yb.append(_rwkv_direction(zall, prm, bsz, ctx_len // RW_BLOCK, seq // RW_BLOCK, bool(d)))

    gk_up = jnp.pad(p['gl_gk_up'][i], ((0, 0), (0, LANES - 16), (0, 0))).astype(BF16)
    gk_b = p['gl_gk_b'][i].reshape(2, 1, GLA_QK)
    s0 = jnp.zeros((2, bsz, GLA_HEADS, GLA_DV, GLA_DK), F32)
    oc, s_ctx = _gla(_to_chunk_major(qk[:nc], bsz), _to_chunk_major(vv[:nc], bsz), _to_chunk_major(gk[:nc], bsz),
                     gk_up, gk_b, s0)
    rows_l = seq // GRID_W
    lat = lambda t: t[nc:].reshape(bsz, rows_l, GRID_W * t.shape[1])
    ol, _ = _gla(lat(qk), lat(vv), lat(gk), gk_up, gk_b, s_ctx)
    o_dir = [jnp.concatenate([_from_chunk_major(oc[d], GLA_V).reshape(nc, GLA_V),
                              ol[d].reshape(bsz * seq, GLA_V)], axis=0) for d in range(2)]

    row0 = (nc // bm) if last else 0
    branch_prm = (p['s5_d'][i].reshape(1, -1), p['rw_g_up'][i].astype(BF16), p['rw_ln_w'][i].reshape(1, -1),
                  p['rw_ln_b'][i].reshape(1, -1), p['gl_norm_g'][i].reshape(1, -1), p['s5_w_glu'][i].astype(BF16),
                  p['rw_w_proj'][i].astype(BF16), p['gl_w_proj'][i].astype(BF16))
    merged = _branches(ya, yb, o_dir, pa_in, zall, branch_prm, row0 * (bm // RW_BLOCK), ctx_len // RW_BLOCK,
                       seq // RW_BLOCK)
    xm = _out_proj(merged, p['w_out'][i].astype(BF16), xa, mod, i, bm, row0, nc, seq)
    if last:
        msel = dict(nc=0, seq=seq)

    wr = jnp.pad(jnp.concatenate([p['moe_wg1'][i], p['moe_wg2'][i]], axis=1), ((0, 0), (0, LANES - 36)))
    wr_hi = wr.astype(BF16)
    wr_lo = (wr - wr_hi.astype(F32)).astype(BF16)
    br = jnp.pad(jnp.concatenate([p['moe_bg1'][i], p['moe_bg2'][i]]), (0, LANES - 36)).reshape(1, LANES)
    vmoe, meta, meta_t, cnt = _router(xm, p['g_norm2'].reshape(-1, 1, D_MODEL), mod, i, wr_hi, wr_lo, br, bm, **msel)
    d1, d2, src, tile_e, n_used = _moe_plan(meta_t, cnt)
    y_sorted = _experts(vmoe, src, tile_e, n_used, i, p['moe_w_gate'], p['moe_w_up'], p['moe_w_down'])
    return _combine(y_sorted, d1, d2, meta, xm, mod, i, bm // 2, **msel)


def kernel(x, c, ctx, c_ctx, w_mod, b_mod, g_norm1, g_norm2, w_in, s5_a_re, s5_a_im, s5_log_dt, s5_b_re, s5_b_im,
           s5_c_re, s5_c_im, s5_d, s5_w_glu, rw_mu, rw_w0, rw_w_up, rw_a0, rw_a_up, rw_k_k, rw_k_a, rw_r_k, rw_g_up,
           rw_ln_w, rw_ln_b, rw_w_proj, gl_gk_up, gl_gk_b, gl_norm_g, gl_w_proj, w_out, moe_wg1, moe_bg1, moe_wg2,
           moe_bg2, moe_w_gate, moe_w_up, moe_w_down, g_final):
    p = dict(g_norm1=g_norm1, g_norm2=g_norm2, w_in=w_in, s5_a_re=s5_a_re, s5_a_im=s5_a_im, s5_log_dt=s5_log_dt,
             s5_b_re=s5_b_re, s5_b_im=s5_b_im, s5_c_re=s5_c_re, s5_c_im=s5_c_im, s5_d=s5_d, s5_w_glu=s5_w_glu,
             rw_mu=rw_mu, rw_w0=rw_w0, rw_w_up=rw_w_up, rw_a0=rw_a0, rw_a_up=rw_a_up, rw_k_k=rw_k_k, rw_k_a=rw_k_a,
             rw_r_k=rw_r_k.reshape(rw_r_k.shape[0], -1), rw_g_up=rw_g_up, rw_ln_w=rw_ln_w, rw_ln_b=rw_ln_b,
             rw_w_proj=rw_w_proj, gl_gk_up=gl_gk_up, gl_gk_b=gl_gk_b, gl_norm_g=gl_norm_g, gl_w_proj=gl_w_proj,
             w_out=w_out, moe_wg1=moe_wg1, moe_bg1=moe_bg1, moe_wg2=moe_wg2, moe_bg2=moe_bg2, moe_w_gate=moe_w_gate,
             moe_w_up=moe_w_up, moe_w_down=moe_w_down)
    bsz, seq, d = x.shape
    ctx_len = ctx.shape[1]
    depth = w_mod.shape[0]
    cc = jnp.concatenate([c, c_ctx[None], jnp.zeros((8 - bsz - 1, d), F32)], axis=0)
    mod = _adaln(cc, w_mod, b_mod).reshape(depth, 8, 6, 1, d)
    xa = jnp.concatenate([ctx.reshape(bsz * ctx_len, d), x.reshape(bsz * seq, d)], axis=0)
    for i in range(depth):
        xa = _layer(i, i == depth - 1, xa, mod, bsz, seq, ctx_len, p)
    out = _final_norm(xa, g_final.reshape(1, d), 512, 0)
    return out.reshape(bsz, seq, d)
```

```python
import functools
import math

import jax
import jax.numpy as jnp
from jax import lax
from jax.experimental import pallas as pl
from jax.experimental.pallas import tpu as pltpu

F32 = jnp.float32
BF16 = jnp.bfloat16

D_MODEL = 2048
GRID_W = 64
NORM_EPS = 1e-6

S5_WIDTH = 1024
S5_GROUP = 16
S5_GROUPS = 64
S5_STATE = 64
S5_MAX_RE = -1e-4
S5_TILE = 16
S5_PAIRS = S5_GROUPS // 2

RW_WIDTH = 1024
RW_HEAD = 64
RW_DECAY_SCALE = 0.606531
RW_GN_EPS = 64e-5
RW_BLOCK = 256
RW_CHUNK = 16
RW_PAIRS = RW_WIDTH // 128
RW_Z = 3 * RW_WIDTH + 128

GLA_HEADS = 4
GLA_DK = 128
GLA_DV = 256
GLA_QK = 512
GLA_V = 1024
GLA_TAU = 16.0
GLA_CHUNK = 64

MOE_GROUPS = 4
MOE_PER_GROUP = 8
MOE_EXPERTS = 32
MOE_HIDDEN = 256
MOE_LANE0 = MOE_GROUPS

LANES = 128
VMEM_LIMIT = 56 * 1024 * 1024


def _cp(*sem):
    return pltpu.CompilerParams(dimension_semantics=sem, vmem_limit_bytes=VMEM_LIMIT)


def _dot(a, b):
    return jnp.dot(a, b, preferred_element_type=F32)


def _dot_nt(a, b):
    return lax.dot_general(a, b, (((1,), (1,)), ((), ())), preferred_element_type=F32)


def _dot_tn(a, b):
    return lax.dot_general(a, b, (((0,), (0,)), ((), ())), preferred_element_type=F32)


def _sigmoid(x):
    return 1.0 / (1.0 + jnp.exp(-x))


def _silu(x):
    return x * _sigmoid(x)


def _gelu_tanh(x):
    return 0.5 * x * (1.0 + jnp.tanh(math.sqrt(2.0 / math.pi) * (x + 0.044715 * (x * x * x))))


def _split_hi_lo(x):
    hi = x.astype(BF16)
    lo = (x - hi.astype(F32)).astype(BF16)
    return hi, lo


def _iota(shape, dim):
    return lax.broadcasted_iota(jnp.int32, shape, dim)


def _head_ones(width):
    return (_iota((LANES, LANES), 0) // width == _iota((LANES, LANES), 1) // width).astype(BF16)


def _head_sum(x, ones, exact):
    if exact:
        hi, lo = _split_hi_lo(x)
        return _dot(hi, ones) + _dot(lo, ones)
    return _dot(x.astype(BF16), ones)


def _adaln_kernel(c_ref, w_ref, b_ref, o_ref):
    c = c_ref[...]
    o_ref[...] = _dot(_silu(c).astype(BF16), w_ref[...].astype(BF16)) + b_ref[...]


def _adaln(cc, w_mod, b_mod):
    depth, d, n = w_mod.shape
    bn = 1536
    return pl.pallas_call(
        _adaln_kernel,
        out_shape=jax.ShapeDtypeStruct((depth, 8, n), F32),
        grid=(depth, n // bn),
        in_specs=[
            pl.BlockSpec((8, d), lambda l, j: (0, 0)),
            pl.BlockSpec((None, d, bn), lambda l, j: (l, 0, j)),
            pl.BlockSpec((None, 1, bn), lambda l, j: (l, 0, j)),
        ],
        out_specs=pl.BlockSpec((None, 8, bn), lambda l, j: (l, 0, j)),
        compiler_params=_cp("parallel", "parallel"),
        name="adaln",
    )(cc, w_mod, b_mod.reshape(depth, 1, n))


def _mod_spec(layer, part, bm, nc, seq):
    def index(i, *_):
        r0 = i * bm
        return (layer, jnp.where(r0 < nc, 2, (r0 - nc) // seq), part, 0, 0)
    return pl.BlockSpec((None, None, None, 1, D_MODEL), index)


def _normmod_kernel(x_ref, g_ref, sh_ref, sc_ref, o_ref):
    x = x_ref[...]
    y = x * lax.rsqrt(jnp.mean(x * x, axis=-1, keepdims=True) + NORM_EPS) * g_ref[...]
    o_ref[...] = (y * (1.0 + sc_ref[...]) + sh_ref[...]).astype(o_ref.dtype)


def _normmod(x, g, mod, layer, parts, bm, nc, seq):
    n, d = x.shape
    return pl.pallas_call(
        _normmod_kernel,
        out_shape=jax.ShapeDtypeStruct((n, d), BF16),
        grid=(n // bm,),
        in_specs=[
            pl.BlockSpec((bm, d), lambda i: (i, 0)),
            pl.BlockSpec((None, 1, d), lambda i: (layer, 0, 0)),
            _mod_spec(layer, parts[0], bm, nc, seq),
            _mod_spec(layer, parts[1], bm, nc, seq),
        ],
        out_specs=pl.BlockSpec((bm, d), lambda i: (i, 0)),
        compiler_params=_cp("parallel"),
        name="normmod",
    )(x, g, mod, mod)


def _mm_kernel(x_ref, w_ref, o_ref):
    o_ref[...] = _dot(x_ref[...], w_ref[...]).astype(o_ref.dtype)


def _mm(x, w, bm, bn, out_dtype=F32):
    m, k = x.shape
    n = w.shape[1]
    return pl.pallas_call(
        _mm_kernel,
        out_shape=jax.ShapeDtypeStruct((m, n), out_dtype),
        grid=(m // bm, n // bn),
        in_specs=[pl.BlockSpec((bm, k), lambda i, j: (i, 0)), pl.BlockSpec((k, bn), lambda i, j: (0, j))],
        out_specs=pl.BlockSpec((bm, bn), lambda i, j: (i, j)),
        compiler_params=_cp("parallel", "parallel"),
        name="mm",
    )(x, w)


def _seq_blocks(bsz, nc_blocks, nl_blocks, reverse):
    def local(s):
        lat = nc_blocks + ((nl_blocks - 1 - (s - nc_blocks)) if reverse else (s - nc_blocks))
        ctx = (nc_blocks - 1 - s) if reverse else s
        return jnp.where(s < nc_blocks, ctx, lat)

    def block(b, s):
        i = local(s)
        return jnp.where(i < nc_blocks, b * nc_blocks + i, bsz * nc_blocks + b * nl_blocks + i - nc_blocks)

    return local, block


S5_PACK = 8
S5_PACKS = S5_GROUPS // S5_PACK
S5_PLANE = S5_PACK * S5_STATE
S5_BLOCK = 256
S5_ROWS = 24


def _s5_tables(a_re, a_im, log_dt, b_re, b_im, c_re, c_im):
    lam = lax.complex(jnp.minimum(a_re, S5_MAX_RE), a_im)
    ldt = lam * jnp.exp(log_dt)[:, None]
    lam_bar = jnp.exp(ldt)
    b_bar = ((lam_bar - 1.0) / lam)[..., None] * lax.complex(b_re, b_im)
    c_mat = lax.complex(c_re, c_im)
    eye = jnp.eye(S5_PACK, dtype=F32)

    def block_diag(t):
        k, g, a, b = t.shape
        return (t[:, :, :, None, :] * eye[None, :, None, :, None]).reshape(k, g * a, g * b)

    b_t = jnp.transpose(b_bar, (0, 2, 1)).reshape(S5_PACKS, S5_PACK, S5_GROUP, S5_STATE)
    bblk = jnp.concatenate([block_diag(b_t.real), block_diag(b_t.imag)], axis=2)
    c_t = jnp.transpose(c_mat, (0, 2, 1)).reshape(S5_PACKS, S5_PACK, S5_STATE, S5_GROUP)
    cblk = jnp.concatenate([block_diag(c_t.real), block_diag(-c_t.imag)], axis=1)
    expo = jnp.concatenate([jnp.arange(1, S5_TILE + 1, dtype=F32), jnp.asarray([32.0, 64.0, 128.0], F32),
                            jnp.zeros((S5_ROWS - S5_TILE - 3,), F32)])
    pw = jnp.exp(ldt[None] * expo[:, None, None]).reshape(S5_ROWS, S5_PACKS, S5_PLANE)
    pw = jnp.transpose(pw, (1, 0, 2))
    return bblk.astype(BF16), cblk.astype(BF16), pw.real, pw.imag


def _s5_scan_kernel(u_ref, bblk_ref, cblk_ref, pre_ref, pim_ref, y_ref, car_ref, up_s, h_s, hb_s, yp_s, *, reverse):
    t = S5_TILE
    n_t = S5_BLOCK // t
    pn = S5_PLANE

    @pl.when(pl.program_id(1) == 0)
    def _():
        car_ref[...] = jnp.zeros_like(car_ref)

    ra = _iota((S5_BLOCK, S5_BLOCK), 0)
    cb = _iota((S5_BLOCK, S5_BLOCK), 1)
    perm = ((ra // t == cb % t) & (ra % t == cb // t)).astype(BF16)
    up = _dot(perm, u_ref[...].astype(BF16)).astype(BF16)
    for pk in range(S5_PACKS):
        up_s[pk] = up[:, LANES * pk:LANES * (pk + 1)]
    order = list(range(t - 1, -1, -1)) if reverse else list(range(t))
    rowj = _iota((n_t, 1), 0)

    def pack_body(pk, carry):
        bu = _dot(up_s[pk], bblk_ref[pk])
        pre = pre_ref[pk]
        pim = pim_ref[pk]
        l_re, l_im = pre[0:1], pim[0:1]
        h_re = h_im = None
        for n, s in enumerate(order):
            rows = slice(t * s, t * (s + 1))
            b_re, b_im = bu[rows, :pn], bu[rows, pn:]
            if n == 0:
                h_re, h_im = b_re, b_im
            else:
                h_re, h_im = l_re * h_re - l_im * h_im + b_re, l_re * h_im + l_im * h_re + b_im
            h_s[rows, :pn] = h_re
            h_s[rows, pn:] = h_im
        c_re, c_im = car_ref[pk, 0:1, :pn], car_ref[pk, 0:1, pn:]
        first = rowj == (n_t - 1 if reverse else 0)
        g_re, g_im = pre[t - 1:t], pim[t - 1:t]
        e_re = h_re + jnp.where(first, g_re * c_re - g_im * c_im, 0.0)
        e_im = h_im + jnp.where(first, g_re * c_im + g_im * c_re, 0.0)
        step = 1
        for row in (t - 1, t, t + 1, t + 2):
            if reverse:
                s_re, s_im, ok = pltpu.roll(e_re, n_t - step, 0), pltpu.roll(e_im, n_t - step, 0), rowj < n_t - step
            else:
                s_re, s_im, ok = pltpu.roll(e_re, step, 0), pltpu.roll(e_im, step, 0), rowj >= step
            a_re, a_im = pre[row:row + 1], pim[row:row + 1]
            e_re = e_re + jnp.where(ok, a_re * s_re - a_im * s_im, 0.0)
            e_im = e_im + jnp.where(ok, a_re * s_im + a_im * s_re, 0.0)
            step *= 2
        last = 0 if reverse else n_t - 1
        car_ref[pk, 0:1, :pn] = e_re[last:last + 1]
        car_ref[pk, 0:1, pn:] = e_im[last:last + 1]
        if reverse:
            in_re = jnp.where(first, c_re, pltpu.roll(e_re, n_t - 1, 0))
            in_im = jnp.where(first, c_im, pltpu.roll(e_im, n_t - 1, 0))
        else:
            in_re = jnp.where(first, c_re, pltpu.roll(e_re, 1, 0))
            in_im = jnp.where(first, c_im, pltpu.roll(e_im, 1, 0))
        for n, s in enumerate(order):
            rows = slice(t * s, t * (s + 1))
            a_re, a_im = pre[n:n + 1], pim[n:n + 1]
            hb_s[rows, :pn] = (h_s[rows, :pn] + a_re * in_re - a_im * in_im).astype(BF16)
            hb_s[rows, pn:] = (h_s[rows, pn:] + a_re * in_im + a_im * in_re).astype(BF16)
        yp_s[pk] = _dot(hb_s[...], cblk_ref[pk]).astype(BF16)
        return carry

    lax.fori_loop(0, S5_PACKS, pack_body, 0)
    for pk in range(S5_PACKS):
        y_ref[:, LANES * pk:LANES * (pk + 1)] = _dot(perm, yp_s[pk])


def _s5_scan(u_src, tables, bsz, nc_blocks, nl_blocks, reverse):
    bblk, cblk, pre, pim = tables
    rows = u_src.shape[0]
    blk = S5_BLOCK
    _, block = _seq_blocks(bsz, nc_blocks, nl_blocks, reverse)
    whole = lambda a: pl.BlockSpec(a.shape, lambda b, s: (0,) * a.ndim)
    return pl.pallas_call(
        functools.partial(_s5_scan_kernel, reverse=reverse),
        out_shape=jax.ShapeDtypeStruct((rows, S5_WIDTH), F32),
        grid=(bsz, nc_blocks + nl_blocks),
        in_specs=[pl.BlockSpec((blk, S5_WIDTH), lambda b, s: (block(b, s), 0)),
                  whole(bblk), whole(cblk), whole(pre), whole(pim)],
        out_specs=pl.BlockSpec((blk, S5_WIDTH), lambda b, s: (block(b, s), 0)),
        scratch_shapes=[
            pltpu.VMEM((S5_PACKS, 8, 2 * S5_PLANE), F32),
            pltpu.VMEM((S5_PACKS, blk, LANES), BF16),
            pltpu.VMEM((blk, 2 * S5_PLANE), F32),
            pltpu.VMEM((blk, 2 * S5_PLANE), BF16),
            pltpu.VMEM((S5_PACKS, blk, LANES), BF16),
        ],
        compiler_params=_cp("parallel", "arbitrary"),
        name="s5_bwd" if reverse else "s5_fwd",
    )(u_src, bblk, cblk, pre, pim)


def _gla_kernel(q_ref, k_ref, v_ref, gk_ref, up_ref, gb_ref, s0_ref, o_ref, sf_ref, st_ref):
    d = pl.program_id(0)
    c = pl.program_id(1)
    n = pl.num_programs(1)

    @pl.when(c == 0)
    def _():
        st_ref[...] = s0_ref[...]

    cs = GLA_CHUNK
    nb = q_ref.shape[0]
    sign = 1 - 2 * d
    row = _iota((cs, cs), 0)
    col = _iota((cs, cs), 1)
    causal = (row - col) * sign >= 0
    causal_b = causal.astype(BF16)
    rid = _iota((cs, 1), 0)
    bcum, b_mid, b_end = [], [], []
    for b in range(nb):
        x = _dot(gk_ref[b].astype(BF16), up_ref[...]) + gb_ref[...]
        log_a = (jnp.minimum(x, 0.0) - jnp.log(1.0 + jnp.exp(-jnp.abs(x)))) * (1.0 / GLA_TAU)
        la_hi, la_lo = _split_hi_lo(log_a)
        bc = _dot(causal_b, la_hi) + _dot(causal_b, la_lo)
        bcum.append(bc)
        b_mid.append(jnp.sum(jnp.where(rid == cs // 2 - d, bc, 0.0), axis=0, keepdims=True))
        b_end.append(jnp.sum(jnp.where(rid == (cs - 1) * (1 - d), bc, 0.0), axis=0, keepdims=True))
    units = [(b, h) for b in range(nb) for h in range(GLA_HEADS)]
    ks = lambda h: slice(GLA_DK * h, GLA_DK * (h + 1))
    vs = lambda h: slice(GLA_DV * h, GLA_DV * (h + 1))
    s_old = [st_ref[b, h] for b, h in units]
    qh = [q_ref[b, :, ks(h)] * (GLA_DK ** -0.5) for b, h in units]
    kh = [k_ref[b, :, ks(h)] for b, h in units]
    vh = [v_ref[b, :, vs(h)].astype(BF16) for b, h in units]
    bh = [bcum[b][:, ks(h)] for b, h in units]
    mid = [b_mid[b][:, ks(h)] for b, h in units]
    end = [b_end[b][:, ks(h)] for b, h in units]
    scores = [_dot_nt((q * jnp.exp(x - m)).astype(BF16), (k * jnp.exp(m - x)).astype(BF16))
              for q, k, x, m in zip(qh, kh, bh, mid)]
    inter = [_dot_nt((q * jnp.exp(x)).astype(BF16), s.astype(BF16)) for q, x, s in zip(qh, bh, s_old)]
    upd = [_dot_tn(v, (k * jnp.exp(e - x)).astype(BF16)) for v, k, e, x in zip(vh, kh, end, bh)]
    intra = [_dot(jnp.where(causal, sc, 0.0).astype(BF16), v) for sc, v in zip(scores, vh)]
    for u, (b, h) in enumerate(units):
        o_ref[b, :, vs(h)] = intra[u] + inter[u]
        st_ref[b, h] = s_old[u] * jnp.exp(end[u]) + upd[u]

    @pl.when(c == n - 1)
    def _():
        sf_ref[...] = st_ref[...]


def _gla(qk, v, gk, gk_up, gk_b, s0):
    bsz, cs, w = v.shape
    n = w // GLA_V

    def chunk(d, c):
        return c + d * (n - 1 - 2 * c)

    state_spec = pl.BlockSpec((None, bsz, GLA_HEADS, GLA_DV, GLA_DK), lambda d, c: (d, 0, 0, 0, 0))
    return pl.pallas_call(
        _gla_kernel,
        out_shape=(jax.ShapeDtypeStruct((2, bsz, cs, w), F32), jax.ShapeDtypeStruct(s0.shape, F32)),
        grid=(2, n),
        in_specs=[
            pl.BlockSpec((bsz, cs, GLA_QK), lambda d, c: (0, 0, 2 * chunk(d, c))),
            pl.BlockSpec((bsz, cs, GLA_QK), lambda d, c: (0, 0, 2 * chunk(d, c) + 1)),
            pl.BlockSpec((bsz, cs, GLA_V), lambda d, c: (0, 0, chunk(d, c))),
            pl.BlockSpec((bsz, cs, LANES), lambda d, c: (0, 0, chunk(d, c))),
            pl.BlockSpec((None, LANES, GLA_QK), lambda d, c: (d, 0, 0)),
            pl.BlockSpec((None, 1, GLA_QK), lambda d, c: (d, 0, 0)),
            state_spec,
        ],
        out_specs=(pl.BlockSpec((None, bsz, cs, GLA_V), lambda d, c: (d, 0, 0, chunk(d, c))), state_spec),
        scratch_shapes=[pltpu.VMEM((bsz, GLA_HEADS, GLA_DV, GLA_DK), F32)],
        compiler_params=_cp("parallel", "arbitrary"),
        name="gla",
    )(qk, qk, v, gk, gk_up, gk_b, s0)


def _to_chunk_major(t, bsz):
    n = t.shape[0] // bsz
    d = t.shape[1]
    t = t.reshape(bsz, n // GLA_CHUNK, GLA_CHUNK, d)
    return jnp.transpose(t, (0, 2, 1, 3)).reshape(bsz, GLA_CHUNK, (n // GLA_CHUNK) * d)


def _from_chunk_major(t, d):
    lead = t.shape[:-2]
    n = t.shape[-1] // d
    t = t.reshape(lead + (GLA_CHUNK, n, d))
    return jnp.swapaxes(t, -3, -2).reshape(lead + (n * GLA_CHUNK, d))


def _rwkv_kernel(z0_ref, z1_ref, halo0_ref, halo1_ref, mu_ref, w0_ref, wup_ref, a0_ref, aup_ref, kk_ref, ka_ref,
                 rk_ref, y_ref, st_ref, kt_s, bt_s, kq_s, rt_s, v_s, w_s, u_s, ya_s, ab_s, pin_s,
                 *, reverse, nc_blocks):
    step = pl.program_id(0)
    blk = RW_BLOCK
    ch = RW_CHUNK
    n_ch = blk // ch
    n_b = 2
    n_units = n_b * RW_PAIRS

    @pl.when(step == 0)
    def _():
        st_ref[...] = jnp.zeros_like(st_ref)

    rowi = _iota((blk, 1), 0)
    low = _iota((1, LANES), 1) < RW_HEAD
    ones64 = _head_ones(RW_HEAD)
    pos = rowi % ch
    seq_start = (step == 0) | (step == nc_blocks)

    for b, (z_ref, halo_ref) in enumerate(((z0_ref, halo0_ref), (z1_ref, halo1_ref))):
        z = z_ref[...]
        if reverse:
            prev = pltpu.roll(z, blk - 1, 0)
            edge = halo_ref[0:1, :]
            at_edge = rowi == blk - 1
        else:
            prev = pltpu.roll(z, 1, 0)
            edge = halo_ref[7:8, :]
            at_edge = rowi == 0
        prev = jnp.where(at_edge, jnp.where(seq_start, 0.0, edge), prev)
        zs = z + (prev - z) * mu_ref[...]
        r = zs[:, 0:RW_WIDTH]
        k = zs[:, RW_WIDTH:2 * RW_WIDTH]
        v = zs[:, 2 * RW_WIDTH:3 * RW_WIDTH]
        lora = zs[:, 3 * RW_WIDTH:3 * RW_WIDTH + LANES]
        lora_w = jnp.where(low, jnp.tanh(lora), 0.0).astype(BF16)
        lora_a = jnp.where(low, 0.0, lora).astype(BF16)
        logw = -RW_DECAY_SCALE * _sigmoid(w0_ref[...] + _dot(lora_w, wup_ref[...]))
        a = _sigmoid(a0_ref[...] + _dot(lora_a, aup_ref[...]))
        kk = k * kk_ref[...]
        kp = k * (1.0 + (a - 1.0) * ka_ref[...])
        rkb = r * kp * rk_ref[...]

        cl = logw
        sh = 1
        while sh < ch:
            if reverse:
                cl = cl + jnp.where(pos < ch - sh, pltpu.roll(cl, blk - sh, 0), 0.0)
            else:
                cl = cl + jnp.where(pos >= sh, pltpu.roll(cl, sh, 0), 0.0)
            sh *= 2
        p_in = jnp.exp(cl)
        pin_s[b] = p_in
        p_ex = jnp.exp(cl - logw)
        p_inv = jnp.exp(-cl)

        for p in range(RW_PAIRS):
            ls = slice(LANES * p, LANES * (p + 1))
            q = b * RW_PAIRS + p
            kkp = kk[:, ls]
            ssq = _head_sum(kkp * kkp, ones64, False)
            kkn = kkp * (1.0 / jnp.maximum(jnp.sqrt(ssq), 1e-12))
            kt_s[q] = (kkn * p_ex[:, ls]).astype(BF16)
            bt_s[q] = (kkn * a[:, ls] * p_inv[:, ls]).astype(BF16)
            kq_s[q] = (kp[:, ls] * p_inv[:, ls]).astype(BF16)
            rt_s[q] = (r[:, ls] * p_in[:, ls]).astype(BF16)
            v_s[q] = v[:, ls].astype(BF16)
            ya_s[q] = _head_sum(rkb[:, ls], ones64, False) * v[:, ls]

    rr = _iota((blk, blk), 0)
    cc = _iota((blk, blk), 1)
    same = rr // ch == cc // ch
    before = (cc > rr) if reverse else (cc < rr)
    strict = same & before
    incl = same & (before | (rr == cc))
    fold0 = (_iota((blk, LANES), 0) % ch == _iota((blk, LANES), 1)).astype(BF16)
    fold1 = (_iota((blk, LANES), 0) % ch + ch == _iota((blk, LANES), 1)).astype(BF16)
    lane_lo = _iota((1, LANES), 1) < RW_HEAD

    def pair_body(p, carry):
        units = [b * RW_PAIRS + p for b in range(n_b)]
        heads = [(u, hh) for u in range(n_b) for hh in range(2)]
        kt = [kt_s[q] for q in units]
        rt = [rt_s[q] for q in units]
        vv = [v_s[q] for q in units]
        ya = [ya_s[q] for q in units]
        rhs = [jnp.concatenate([bt_s[q], kq_s[q]], axis=0) for q in units]
        g = []
        for u, hh in heads:
            mine = lane_lo if hh == 0 else jnp.logical_not(lane_lo)
            zero = jnp.zeros_like(kt[u])
            lhs = jnp.concatenate([jnp.where(mine, kt[u], zero), jnp.where(mine, rt[u], zero)], axis=0)
            g.append(_dot_nt(lhs, rhs[u]))
        n1 = [jnp.where(strict, -gh[:blk, :blk], 0.0).astype(BF16) for gh in g]
        a_kq = [jnp.where(strict, gh[:blk, blk:], 0.0).astype(BF16) for gh in g]
        a_rb = [jnp.where(incl, gh[blk:, :blk], 0.0).astype(BF16) for gh in g]
        a_rq = [jnp.where(incl, gh[blk:, blk:], 0.0).astype(BF16) for gh in g]
        n2 = [_dot(n, n).astype(BF16) for n in n1]
        akv = [_dot(a, vv[u]) for a, (u, _) in zip(a_kq, heads)]
        n4 = [_dot(n, n).astype(BF16) for n in n2]
        y_in = [_dot(a, vv[u]) for a, (u, _) in zip(a_rq, heads)]
        n8 = [_dot(n, n).astype(BF16) for n in n4]
        fold = [_dot(a, fold0 if hh == 0 else fold1) for a, (_, hh) in zip(a_rb, heads)]
        rhs_t = [jnp.concatenate([kt[u].astype(F32), av], axis=1) for av, (u, _) in zip(akv, heads)]
        for nk in (n8, n4, n2, n1):
            rhs_t = [x + _dot(n, x.astype(BF16)) for n, x in zip(nk, rhs_t)]
        for u, q in enumerate(units):
            h0, h1 = 2 * u, 2 * u + 1
            w_s[q] = jnp.where(lane_lo, rhs_t[h0][:, :LANES], rhs_t[h1][:, :LANES]).astype(BF16)
            u_s[q] = jnp.where(lane_lo, rhs_t[h0][:, LANES:], rhs_t[h1][:, LANES:])
            ya_s[q] = ya[u] + jnp.where(lane_lo, y_in[h0], y_in[h1])
            ab_s[q] = (fold[h0] + fold[h1]).astype(BF16)
        return carry

    lax.fori_loop(0, RW_PAIRS, pair_body, 0)

    blockdiag = (_iota((LANES, LANES), 0) // RW_HEAD) == (_iota((LANES, LANES), 1) // RW_HEAD)
    end_row = 0 if reverse else ch - 1

    def chunk_body(i, carry):
        c = (n_ch - 1 - i) if reverse else i
        rows = pl.ds(pl.multiple_of(c * ch, ch), ch)
        s_old = [st_ref[q] for q in range(n_units)]
        m1 = [_dot_nt(jnp.concatenate([w_s[q, rows, :], rt_s[q, rows, :]], axis=0), s_old[q].astype(BF16))
              for q in range(n_units)]
        zc = [-(m1[q][:ch] + u_s[q, rows, :]) for q in range(n_units)]
        upd = []
        for q in range(n_units):
            zv = jnp.concatenate([zc[q].astype(BF16), v_s[q, rows, :]], axis=0)
            bk = jnp.concatenate([bt_s[q, rows, :], kq_s[q, rows, :]], axis=0)
            upd.append(_dot_tn(zv, bk))
        yc = []
        for q in range(n_units):
            z2 = jnp.concatenate([jnp.where(lane_lo, zc[q], 0.0), jnp.where(lane_lo, 0.0, zc[q])], axis=0)
            yc.append(m1[q][ch:] + _dot(ab_s[q, rows, :][:, :2 * ch], z2.astype(BF16)) + ya_s[q, rows, :])
        for q in range(n_units):
            b, p = divmod(q, RW_PAIRS)
            p_end = pin_s[b, rows, LANES * p:LANES * (p + 1)][end_row:end_row + 1]
            st_ref[q] = (s_old[q] + jnp.where(blockdiag, upd[q], 0.0)) * p_end
            y_ref[b, rows, LANES * p:LANES * (p + 1)] = yc[q]
        return carry

    lax.fori_loop(0, n_ch, chunk_body, 0)


def _rwkv_direction(zall, prm, bsz, nc_blocks, nl_blocks, reverse):
    assert bsz == 2
    mu, w0, wup, a0, aup, k_k, k_a, r_k = prm
    rows = zall.shape[0]
    blk = RW_BLOCK
    steps = nc_blocks + nl_blocks
    n_blocks = rows // blk

    def local(s):
        lat = nc_blocks + ((nl_blocks - 1 - (s - nc_blocks)) if reverse else (s - nc_blocks))
        ctx = (nc_blocks - 1 - s) if reverse else s
        return jnp.where(s < nc_blocks, ctx, lat)

    def block(b, s):
        i = local(s)
        return jnp.where(i < nc_blocks, b * nc_blocks + i, bsz * nc_blocks + b * nl_blocks + i - nc_blocks)

    def halo(b, s):
        i = block(b, s)
        if reverse:
            return jnp.minimum((i + 1) * (blk // 8), n_blocks * (blk // 8) - 1)
        return jnp.maximum(i * (blk // 8) - 1, 0)

    vec = lambda w: pl.BlockSpec((1, w), lambda s: (0, 0))
    mat = lambda: pl.BlockSpec((LANES, RW_WIDTH), lambda s: (0, 0))
    n_units = bsz * RW_PAIRS
    unit_bf = pltpu.VMEM((n_units, blk, LANES), BF16)
    unit_f = pltpu.VMEM((n_units, blk, LANES), F32)
    z_spec = lambda b: pl.BlockSpec((blk, RW_Z), lambda s: (block(b, s), 0))
    halo_spec = lambda b: pl.BlockSpec((8, RW_Z), lambda s: (halo(b, s), 0))
    return pl.pallas_call(
        functools.partial(_rwkv_kernel, reverse=reverse, nc_blocks=nc_blocks),
        out_shape=jax.ShapeDtypeStruct((bsz, steps * blk, RW_WIDTH), F32),
        grid=(steps,),
        in_specs=[
            z_spec(0), z_spec(1), halo_spec(0), halo_spec(1),
            vec(RW_Z), vec(RW_WIDTH), mat(), vec(RW_WIDTH), mat(), vec(RW_WIDTH), vec(RW_WIDTH), vec(RW_WIDTH),
        ],
        out_specs=pl.BlockSpec((bsz, blk, RW_WIDTH), lambda s: (0, local(s), 0)),
        scratch_shapes=[
            pltpu.VMEM((n_units, LANES, LANES), F32),
            unit_bf, unit_bf, unit_bf, unit_bf, unit_bf,
            unit_bf, unit_f, unit_f, unit_bf,
            pltpu.VMEM((bsz, blk, RW_WIDTH), F32),
        ],
        compiler_params=_cp("arbitrary"),
        name="rwkv_bwd" if reverse else "rwkv_fwd",
    )(zall, zall, zall, zall, mu, w0, wup, a0, aup, k_k, k_a, r_k)


def _branches_kernel(saf_ref, sab_ref, su_ref, sd_ref, rf_ref, rb_ref, rg_ref, gup_ref, lw_ref, lb_ref,
                     of_ref, ob_ref, og_ref, ng_ref, ga_ref, gb_ref, gc_ref, wglu_ref, wrw_ref, wgl_ref, m_ref):
    za = _gelu_tanh(saf_ref[...] + sab_ref[...] + sd_ref[...] * su_ref[...]).astype(BF16)
    hid = _dot(za, wglu_ref[...])
    m = _sigmoid(ga_ref[...]) * (hid[:, :D_MODEL] * _sigmoid(hid[:, D_MODEL:]))
    ones64 = _head_ones(RW_HEAD)
    gate = _dot(_sigmoid(rg_ref[...]).astype(BF16), gup_ref[...])
    zb = []
    for p in range(RW_PAIRS):
        ls = slice(LANES * p, LANES * (p + 1))
        y = rf_ref[:, ls] + rb_ref[:, ls]
        mean = _head_sum(y, ones64, True) * (1.0 / RW_HEAD)
        yc = y - mean
        var = _head_sum(yc * yc, ones64, True) * (1.0 / RW_HEAD)
        yn = yc * lax.rsqrt(var + RW_GN_EPS) * lw_ref[:, ls] + lb_ref[:, ls]
        zb.append((yn * gate[:, ls]).astype(BF16))
    m = m + _sigmoid(gb_ref[...]) * _dot(jnp.concatenate(zb, axis=1), wrw_ref[...])
    zc = []
    for h in range(GLA_HEADS):
        vs = slice(GLA_DV * h, GLA_DV * (h + 1))
        o = of_ref[:, vs] + ob_ref[:, vs]
        on = o * lax.rsqrt(jnp.mean(o * o, axis=-1, keepdims=True) + NORM_EPS) * ng_ref[:, vs]
        zc.append((on * _silu(og_ref[:, vs])).astype(BF16))
    m = m + _sigmoid(gc_ref[...]) * _dot(jnp.concatenate(zc, axis=1), wgl_ref[...])
    m_ref[...] = m.astype(BF16)


def _branches(ya, yb, o_dir, pa_in, zall, prm, row0, nc_blocks, nl_blocks):
    s5_d, g_up, ln_w, ln_b, norm_g, w_glu, w_rw, w_gl = prm
    bm = RW_BLOCK
    bsz = yb[0].shape[0]
    n = zall.shape[0] - row0 * bm
    rows = lambda w, blk=0: pl.BlockSpec((bm, w), lambda i: (i + row0, blk))

    def y_index(i):
        i = i + row0
        t = i - bsz * nc_blocks
        return (jnp.where(t < 0, i // nc_blocks, t // nl_blocks),
                jnp.where(t < 0, i % nc_blocks, nc_blocks + t % nl_blocks), 0)

    y_spec = pl.BlockSpec((None, bm, RW_WIDTH), y_index)
    vec = pl.BlockSpec((1, 1024), lambda i: (0, 0))
    const = lambda a: pl.BlockSpec(a.shape, lambda i: (0,) * a.ndim, pipeline_mode=pl.Buffered(1))
    return pl.pallas_call(
        _branches_kernel,
        out_shape=jax.ShapeDtypeStruct((n, D_MODEL), BF16),
        grid=(n // bm,),
        in_specs=[rows(1024), rows(1024), rows(1024, 0), vec,
                  y_spec, y_spec, rows(LANES, RW_Z // LANES), const(g_up), vec, vec,
                  rows(1024), rows(1024), rows(1024, 1), vec,
                  rows(D_MODEL, 1), rows(D_MODEL, 2), rows(D_MODEL, 3),
                  const(w_glu), const(w_rw), const(w_gl)],
        out_specs=pl.BlockSpec((bm, D_MODEL), lambda i: (i, 0)),
        compiler_params=_cp("parallel"),
        name="branches",
    )(ya[0], ya[1], pa_in, s5_d, yb[0], yb[1], zall, g_up, ln_w, ln_b,
      o_dir[0], o_dir[1], pa_in, norm_g, pa_in, pa_in, pa_in, w_glu, w_rw, w_gl)


def _out_proj_kernel(m_ref, w_ref, x_ref, gate_ref, o_ref):
    o_ref[...] = x_ref[...] + gate_ref[...] * _dot(m_ref[...], w_ref[...])


def _out_proj(m, w_out, x, mod, layer, bm, row0, nc, seq):
    n, d = m.shape
    bn = 1024
    mod_index = _mod_spec(layer, 2, bm, nc, seq).index_map
    return pl.pallas_call(
        _out_proj_kernel,
        out_shape=jax.ShapeDtypeStruct((n, d), F32),
        grid=(n // bm, d // bn),
        in_specs=[pl.BlockSpec((bm, d), lambda i, j: (i, 0)),
                  pl.BlockSpec((d, bn), lambda i, j: (0, j)),
                  pl.BlockSpec((bm, bn), lambda i, j: (i + row0, j)),
                  pl.BlockSpec((None, None, None, 1, bn), lambda i, j: mod_index(i + row0)[:4] + (j,))],
        out_specs=pl.BlockSpec((bm, bn), lambda i, j: (i, j)),
        compiler_params=_cp("parallel", "parallel"),
        name="out_proj",
    )(m, w_out, x, mod)


MOE_TM = 256
MOE_ROW_TILES = D_MODEL // (2 * LANES)
META_E1, META_E2, META_R1, META_R2, META_W1, META_W2 = range(6)


def _router_kernel(x_ref, g_ref, sh_ref, sc_ref, wr_hi_ref, wr_lo_ref, br_ref, v_ref, meta_ref, meta_t_ref, cnt_ref,
                   base_s):
    @pl.when(pl.program_id(0) == 0)
    def _():
        base_s[...] = jnp.zeros_like(base_s)

    x = x_ref[...]
    y = x * lax.rsqrt(jnp.mean(x * x, axis=-1, keepdims=True) + NORM_EPS) * g_ref[...]
    t = y * (1.0 + sc_ref[...]) + sh_ref[...]
    bits = lax.bitcast_convert_type(t.astype(BF16).astype(F32), jnp.uint32)
    half = D_MODEL // 2
    word = (bits[:, half:] & jnp.uint32(0xFFFF0000)) | (bits[:, :half] >> 16)
    for s in range(MOE_ROW_TILES):
        v_ref[pl.ds(s, x.shape[0], stride=MOE_ROW_TILES), :] = word[:, LANES * s:LANES * (s + 1)]
    t_hi, t_lo = _split_hi_lo(t)
    logits = (_dot(t_hi, wr_hi_ref[...]) + _dot(t_lo, wr_hi_ref[...]) + _dot(t_hi, wr_lo_ref[...])) + br_ref[...]
    lane = _iota(logits.shape, 1).astype(F32)
    neg = jnp.float32(-jnp.inf)
    big = jnp.float32(LANES)
    l1 = jnp.where(lane < MOE_GROUPS, logits, neg)
    m1 = jnp.max(l1, axis=-1, keepdims=True)
    p_top = 1.0 / jnp.sum(jnp.exp(l1 - m1), axis=-1, keepdims=True)
    grp = jnp.min(jnp.where(l1 == m1, lane, big), axis=-1, keepdims=True)
    lo = MOE_LANE0 + MOE_PER_GROUP * grp
    in_grp = (lane >= lo) & (lane < lo + MOE_PER_GROUP)
    l2 = jnp.where(in_grp, logits, neg)
    v1 = jnp.max(l2, axis=-1, keepdims=True)
    i1 = jnp.min(jnp.where(l2 == v1, lane, big), axis=-1, keepdims=True)
    l3 = jnp.where(lane == i1, neg, l2)
    v2 = jnp.max(l3, axis=-1, keepdims=True)
    i2 = jnp.min(jnp.where(l3 == v2, lane, big), axis=-1, keepdims=True)
    e2 = jnp.exp(v2 - v1)
    w1 = p_top / (1.0 + e2)
    w2 = p_top * e2 / (1.0 + e2)
    pick1 = lane == i1
    pick2 = lane == i2
    chosen = jnp.where(pick1 | pick2, 1.0, 0.0)
    bm = x.shape[0]
    earlier = (_iota((bm, bm), 1) < _iota((bm, bm), 0)).astype(BF16)
    before = _dot(earlier, chosen.astype(BF16)) + base_s[...]
    r1 = jnp.sum(jnp.where(pick1, before, 0.0), axis=-1, keepdims=True)
    r2 = jnp.sum(jnp.where(pick2, before, 0.0), axis=-1, keepdims=True)
    base_s[...] += jnp.sum(chosen, axis=0, keepdims=True)
    cnt_ref[...] = base_s[...]
    meta = jnp.zeros_like(logits)
    for slot, val in ((META_E1, i1 - MOE_LANE0), (META_E2, i2 - MOE_LANE0), (META_R1, r1), (META_R2, r2),
                      (META_W1, w1), (META_W2, w2)):
        meta = jnp.where(lane == slot, val, meta)
    meta_ref[...] = meta
    meta_t_ref[...] = meta.T[:8]


def _router(x, g, mod, layer, wr_hi, wr_lo, br, bm, nc, seq):
    n, d = x.shape
    return pl.pallas_call(
        _router_kernel,
        out_shape=(jax.ShapeDtypeStruct((n * MOE_ROW_TILES, LANES), jnp.uint32), jax.ShapeDtypeStruct((n, LANES), F32),
                   jax.ShapeDtypeStruct((8, n), F32), jax.ShapeDtypeStruct((1, LANES), F32)),
        grid=(n // bm,),
        in_specs=[
            pl.BlockSpec((bm, d), lambda i: (i, 0)),
            pl.BlockSpec((None, 1, d), lambda i: (layer, 0, 0)),
            _mod_spec(layer, 3, bm, nc, seq), _mod_spec(layer, 4, bm, nc, seq),
            pl.BlockSpec((d, LANES), lambda i: (0, 0)), pl.BlockSpec((d, LANES), lambda i: (0, 0)),
            pl.BlockSpec((1, LANES), lambda i: (0, 0)),
        ],
        out_specs=(pl.BlockSpec((bm * MOE_ROW_TILES, LANES), lambda i: (i, 0)),
                   pl.BlockSpec((bm, LANES), lambda i: (i, 0)),
                   pl.BlockSpec((8, bm), lambda i: (0, i)), pl.BlockSpec((1, LANES), lambda i: (0, 0))),
        scratch_shapes=[pltpu.VMEM((1, LANES), F32)],
        compiler_params=_cp("arbitrary"),
        name="moe_router",
    )(x, g, mod, mod, wr_hi, wr_lo, br)


def _moe_plan(meta_t, cnt):
    tm = MOE_TM
    n_tok = meta_t.shape[1]
    counts = cnt[0, MOE_LANE0:MOE_LANE0 + MOE_EXPERTS].astype(jnp.int32)
    seg = ((counts + tm - 1) // tm) * tm
    ends = jnp.cumsum(seg)
    off = ends - seg
    rec = meta_t[:4].astype(jnp.int32)
    first_row = jnp.sum(jnp.where(rec[:2, None, :] == jnp.arange(MOE_EXPERTS)[None, :, None], off[None, :, None], 0),
                        axis=1)
    dest = first_row + rec[2:4]
    n_rows = 2 * n_tok + MOE_EXPERTS * tm
    n_tiles = n_rows // tm
    tile_e = jnp.sum(((jnp.arange(n_tiles) * tm)[:, None] >= ends[None, :]).astype(jnp.int32), axis=1)
    tile_e = jnp.minimum(tile_e, MOE_EXPERTS - 1)
    tok = jnp.broadcast_to(jnp.arange(n_tok, dtype=jnp.int32), (2, n_tok))
    src = jnp.zeros((n_rows,), jnp.int32).at[dest.reshape(-1)].set(tok.reshape(-1))
    return dest[0], dest[1], src, tile_e, (ends[-1] // tm).reshape(1)


def _row_copy(src_hbm, row, dst, slot, r, sem):
    return pltpu.make_async_copy(src_hbm.at[pl.ds(row, 1)], dst.at[slot, pl.ds(r, 1)], sem)


def _experts_kernel(src_ref, te_ref, nu_ref, v_hbm, wg_ref, wu_ref, wd_ref, y_ref, xbuf, sem):
    del te_ref
    i = pl.program_id(0)
    n_used = nu_ref[0]
    slot = i % 2

    nt = MOE_ROW_TILES

    def token_copy(tok, into, r):
        return pltpu.make_async_copy(v_hbm.at[pl.ds(tok * nt, nt)], xbuf.at[into, pl.ds(r * nt, nt)], sem.at[into])

    def gather(tile, into):
        for r in range(MOE_TM):
            token_copy(src_ref[tile * MOE_TM + r], into, r).start(priority=r % 2)

    def expert():
        for r in range(MOE_TM):
            token_copy(0, slot, r).wait()
        word = jnp.concatenate([xbuf[slot, pl.ds(s, MOE_TM, stride=nt), :] for s in range(nt)], axis=1)
        lo = lax.bitcast_convert_type(word << 16, F32)
        hi = lax.bitcast_convert_type(word & jnp.uint32(0xFFFF0000), F32)
        t = jnp.concatenate([lo, hi], axis=1).astype(BF16)
        hid = _silu(_dot(t, wg_ref[...].astype(BF16))) * _dot(t, wu_ref[...].astype(BF16))
        y_ref[...] = _dot(hid.astype(BF16), wd_ref[...].astype(BF16))

    @pl.when((i == 0) & (n_used > 0))
    def _():
        gather(0, 0)

    @pl.when(i + 1 < n_used)
    def _():
        gather(i + 1, 1 - slot)
        expert()

    @pl.when(i + 1 == n_used)
    def _():
        expert()

    @pl.when(i >= n_used)
    def _():
        y_ref[...] = jnp.zeros_like(y_ref)


def _experts(v, src, tile_e, n_used, layer, w_gate, w_up, w_down):
    d = w_gate.shape[2]
    hdim = w_gate.shape[3]
    n_rows = src.shape[0]
    tm = MOE_TM
    by_expert = lambda i, src_r, te_r, nu_r: (layer, te_r[i], 0, 0)
    return pl.pallas_call(
        _experts_kernel,
        out_shape=jax.ShapeDtypeStruct((n_rows, d), F32),
        grid_spec=pltpu.PrefetchScalarGridSpec(
            num_scalar_prefetch=3,
            grid=(n_rows // tm,),
            in_specs=[
                pl.BlockSpec(memory_space=pl.ANY),
                pl.BlockSpec((None, None, d, hdim), by_expert),
                pl.BlockSpec((None, None, d, hdim), by_expert),
                pl.BlockSpec((None, None, hdim, d), by_expert),
            ],
            out_specs=pl.BlockSpec((tm, d), lambda i, *_: (i, 0)),
            scratch_shapes=[pltpu.VMEM((2, tm * MOE_ROW_TILES, LANES), jnp.uint32), pltpu.SemaphoreType.DMA((2,))],
        ),
        compiler_params=_cp("arbitrary"),
        name="moe_experts",
    )(src, tile_e, n_used, v, w_gate, w_up, w_down)


def _combine_kernel(d1_ref, d2_ref, y_hbm, meta_ref, x_ref, gate_ref, o_ref, buf1, buf2, sem):
    i = pl.program_id(0)
    bm = x_ref.shape[0]
    slot = i % 2

    def gather(tile, into):
        for r in range(bm):
            t = tile * bm + r
            _row_copy(y_hbm, d1_ref[t], buf1, into, r, sem.at[0, into]).start(priority=0)
            _row_copy(y_hbm, d2_ref[t], buf2, into, r, sem.at[1, into]).start(priority=1)

    @pl.when(i == 0)
    def _():
        gather(0, 0)

    @pl.when(i + 1 < pl.num_programs(0))
    def _():
        gather(i + 1, 1 - slot)

    for r in range(bm):
        _row_copy(y_hbm, 0, buf1, slot, r, sem.at[0, slot]).wait()
        _row_copy(y_hbm, 0, buf2, slot, r, sem.at[1, slot]).wait()
    meta = meta_ref[...]
    lane = _iota(meta.shape, 1)
    w1 = jnp.sum(jnp.where(lane == META_W1, meta, 0.0), axis=-1, keepdims=True)
    w2 = jnp.sum(jnp.where(lane == META_W2, meta, 0.0), axis=-1, keepdims=True)
    o_ref[...] = x_ref[...] + gate_ref[...] * (w1 * buf1[slot] + w2 * buf2[slot])


def _combine(y, d1, d2, meta, x, mod, layer, bm, nc, seq):
    n, d = x.shape
    at_tile = lambda i, *_: (i, 0)
    mod_index = _mod_spec(layer, 5, bm, nc, seq).index_map
    return pl.pallas_call(
        _combine_kernel,
        out_shape=jax.ShapeDtypeStruct((n, d), F32),
        grid_spec=pltpu.PrefetchScalarGridSpec(
            num_scalar_prefetch=2,
            grid=(n // bm,),
            in_specs=[
                pl.BlockSpec(memory_space=pl.ANY),
                pl.BlockSpec((bm, LANES), at_tile),
                pl.BlockSpec((bm, d), at_tile),
                pl.BlockSpec((None, None, None, 1, d), lambda i, *_: mod_index(i)),
            ],
            out_specs=pl.BlockSpec((bm, d), at_tile),
            scratch_shapes=[pltpu.VMEM((2, bm, d), F32), pltpu.VMEM((2, bm, d), F32),
                            pltpu.SemaphoreType.DMA((2, 2))],
        ),
        compiler_params=_cp("arbitrary"),
        name="moe_combine",
    )(d1, d2, y, meta, x, mod)


def _final_norm_kernel(x_ref, g_ref, o_ref):
    x = x_ref[...]
    o_ref[...] = x * lax.rsqrt(jnp.mean(x * x, axis=-1, keepdims=True) + NORM_EPS) * g_ref[...]


def _final_norm(x, g, bm, row0):
    n, d = x.shape
    n -= row0 * bm
    return pl.pallas_call(
        _final_norm_kernel,
        out_shape=jax.ShapeDtypeStruct((n, d), F32),
        grid=(n // bm,),
        in_specs=[pl.BlockSpec((bm, d), lambda i: (i + row0, 0)), pl.BlockSpec((1, d), lambda i: (0, 0))],
        out_specs=pl.BlockSpec((bm, d), lambda i: (i, 0)),
        compiler_params=_cp("parallel"),
        name="final_norm",
    )(x, g)


_COL = dict(s5=0, rw=S5_WIDTH, rg=S5_WIDTH + 3 * RW_WIDTH + 128, q=4352, k=4864, v=5376, gk=6400, og=6416, gates=7440)


def _pad_rows(w, rows):
    return jnp.pad(w, ((0, rows - w.shape[0]), (0, 0)))


def _layer(i, last, xa, mod, bsz, seq, ctx_len, p):
    nc = bsz * ctx_len
    rows = xa.shape[0]
    bm = 512
    msel = dict(nc=nc, seq=seq)
    u = _normmod(xa, p['g_norm1'].reshape(-1, 1, D_MODEL), mod, i, (0, 1), bm, **msel)

    w_in = p['w_in'][i]
    col = lambda a, w: w_in[:, a:a + w].astype(BF16)
    bm_in = 1088 if rows % 1088 == 0 else bm
    w_a = jnp.concatenate([col(0, S5_WIDTH), col(_COL['og'], GLA_V), col(_COL['gates'], 3 * D_MODEL)], axis=1)
    w_z = jnp.concatenate([col(_COL['rw'], 3 * RW_WIDTH + 128), col(_COL['rg'], 128)], axis=1)
    w_qk = col(_COL['q'], 2 * GLA_QK)
    w_v = col(_COL['v'], GLA_V)
    w_gk = jnp.pad(col(_COL['gk'], 16), ((0, 0), (0, LANES - 16)))
    pa_in = _mm(u, w_a, bm_in, 1024)
    zall = _mm(u, w_z, bm_in, 1664)
    qk = _mm(u, w_qk, bm_in, 1024)
    vv = _mm(u, w_v, bm_in, 1024)
    gk = _mm(u, w_gk, bm_in, LANES)

    ya = []
    for d in range(2):
        tables = _s5_tables(*(p[k][i, d] for k in ('s5_a_re', 's5_a_im', 's5_log_dt', 's5_b_re', 's5_b_im',
                                                   's5_c_re', 's5_c_im')))
        ya.append(_s5_scan(pa_in, tables, bsz, ctx_len // S5_BLOCK, seq // S5_BLOCK, bool(d)))

    yb = []
    for d in range(2):
        prm = (
            jnp.pad(p['rw_mu'][i, d], (0, RW_Z - p['rw_mu'].shape[-1])).reshape(1, RW_Z),
            p['rw_w0'][i, d].reshape(1, -1),
            _pad_rows(p['rw_w_up'][i, d], LANES).astype(BF16),
            p['rw_a0'][i, d].reshape(1, -1),
            jnp.pad(p['rw_a_up'][i, d], ((RW_HEAD, 0), (0, 0))).astype(BF16),
            p['rw_k_k'][i].reshape(1, -1), p['rw_k_a'][i].reshape(1, -1), p['rw_r_k'][i].reshape(1, -1),
        )
        yb.append(_rwkv_direction(zall, prm, bsz, ctx_len // RW_BLOCK, seq // RW_BLOCK, bool(d)))

    gk_up = jnp.pad(p['gl_gk_up'][i], ((0, 0), (0, LANES - 16), (0, 0))).astype(BF16)
    gk_b = p['gl_gk_b'][i].reshape(2, 1, GLA_QK)
    s0 = jnp.zeros((2, bsz, GLA_HEADS, GLA_DV, GLA_DK), F32)
    oc, s_ctx = _gla(_to_chunk_major(qk[:nc], bsz), _to_chunk_major(vv[:nc], bsz), _to_chunk_major(gk[:nc], bsz),
                     gk_up, gk_b, s0)
    rows_l = seq // GRID_W
    lat = lambda t: t[nc:].reshape(bsz, rows_l, GRID_W * t.shape[1])
    ol, _ = _gla(lat(qk), lat(vv), lat(gk), gk_up, gk_b, s_ctx)
    o_dir = [jnp.concatenate([_from_chunk_major(oc[d], GLA_V).reshape(nc, GLA_V),
                              ol[d].reshape(bsz * seq, GLA_V)], axis=0) for d in range(2)]

    row0 = (nc // bm) if last else 0
    branch_prm = (p['s5_d'][i].reshape(1, -1), p['rw_g_up'][i].astype(BF16), p['rw_ln_w'][i].reshape(1, -1),
                  p['rw_ln_b'][i].reshape(1, -1), p['gl_norm_g'][i].reshape(1, -1), p['s5_w_glu'][i].astype(BF16),
                  p['rw_w_proj'][i].astype(BF16), p['gl_w_proj'][i].astype(BF16))
    merged = _branches(ya, yb, o_dir, pa_in, zall, branch_prm, row0 * (bm // RW_BLOCK), ctx_len // RW_BLOCK,
                       seq // RW_BLOCK)
    xm = _out_proj(merged, p['w_out'][i].astype(BF16), xa, mod, i, bm, row0, nc, seq)
    if last:
        msel = dict(nc=0, seq=seq)

    wr = jnp.pad(jnp.concatenate([p['moe_wg1'][i], p['moe_wg2'][i]], axis=1), ((0, 0), (0, LANES - 36)))
    wr_hi = wr.astype(BF16)
    wr_lo = (wr - wr_hi.astype(F32)).astype(BF16)
    br = jnp.pad(jnp.concatenate([p['moe_bg1'][i], p['moe_bg2'][i]]), (0, LANES - 36)).reshape(1, LANES)
    vmoe, meta, meta_t, cnt = _router(xm, p['g_norm2'].reshape(-1, 1, D_MODEL), mod, i, wr_hi, wr_lo, br, bm, **msel)
    d1, d2, src, tile_e, n_used = _moe_plan(meta_t, cnt)
    y_sorted = _experts(vmoe, src, tile_e, n_used, i, p['moe_w_gate'], p['moe_w_up'], p['moe_w_down'])
    return _combine(y_sorted, d1, d2, meta, xm, mod, i, bm // 2, **msel)


def kernel(x, c, ctx, c_ctx, w_mod, b_mod, g_norm1, g_norm2, w_in, s5_a_re, s5_a_im, s5_log_dt, s5_b_re, s5_b_im,
           s5_c_re, s5_c_im, s5_d, s5_w_glu, rw_mu, rw_w0, rw_w_up, rw_a0, rw_a_up, rw_k_k, rw_k_a, rw_r_k, rw_g_up,
           rw_ln_w, rw_ln_b, rw_w_proj, gl_gk_up, gl_gk_b, gl_norm_g, gl_w_proj, w_out, moe_wg1, moe_bg1, moe_wg2,
           moe_bg2, moe_w_gate, moe_w_up, moe_w_down, g_final):
    p = dict(g_norm1=g_norm1, g_norm2=g_norm2, w_in=w_in, s5_a_re=s5_a_re, s5_a_im=s5_a_im, s5_log_dt=s5_log_dt,
             s5_b_re=s5_b_re, s5_b_im=s5_b_im, s5_c_re=s5_c_re, s5_c_im=s5_c_im, s5_d=s5_d, s5_w_glu=s5_w_glu,
             rw_mu=rw_mu, rw_w0=rw_w0, rw_w_up=rw_w_up, rw_a0=rw_a0, rw_a_up=rw_a_up, rw_k_k=rw_k_k, rw_k_a=rw_k_a,
             rw_r_k=rw_r_k.reshape(rw_r_k.shape[0], -1), rw_g_up=rw_g_up, rw_ln_w=rw_ln_w, rw_ln_b=rw_ln_b,
             rw_w_proj=rw_w_proj, gl_gk_up=gl_gk_up, gl_gk_b=gl_gk_b, gl_norm_g=gl_norm_g, gl_w_proj=gl_w_proj,
             w_out=w_out, moe_wg1=moe_wg1, moe_bg1=moe_bg1, moe_wg2=moe_wg2, moe_bg2=moe_bg2, moe_w_gate=moe_w_gate,
             moe_w_up=moe_w_up, moe_w_down=moe_w_down)
    bsz, seq, d = x.shape
    ctx_len = ctx.shape[1]
    depth = w_mod.shape[0]
    cc = jnp.concatenate([c, c_ctx[None], jnp.zeros((8 - bsz - 1, d), F32)], axis=0)
    mod = _adaln(cc, w_mod, b_mod).reshape(depth, 8, 6, 1, d)
    xa = jnp.concatenate([ctx.reshape(bsz * ctx_len, d), x.reshape(bsz * seq, d)], axis=0)
    for i in range(depth):
        xa = _layer(i, i == depth - 1, xa, mod, bsz, seq, ctx_len, p)
    out = _final_norm(xa, g_final.reshape(1, d), 512, 0)
    return out.reshape(bsz, seq, d)
```

```python
import functools
import math

import jax
import jax.numpy as jnp
from jax import lax
from jax.experimental import pallas as pl
from jax.experimental.pallas import tpu as pltpu

F32 = jnp.float32
BF16 = jnp.bfloat16

D_MODEL = 2048
GRID_W = 64
NORM_EPS = 1e-6

S5_WIDTH = 1024
S5_GROUP = 16
S5_GROUPS = 64
S5_STATE = 64
S5_MAX_RE = -1e-4
S5_TILE = 16
S5_PAIRS = S5_GROUPS // 2

RW_WIDTH = 1024
RW_HEAD = 64
RW_DECAY_SCALE = 0.606531
RW_GN_EPS = 64e-5
RW_BLOCK = 256
RW_CHUNK = 16
RW_PAIRS = RW_WIDTH // 128
RW_Z = 3 * RW_WIDTH + 128

GLA_HEADS = 4
GLA_DK = 128
GLA_DV = 256
GLA_QK = 512
GLA_V = 1024
GLA_TAU = 16.0
GLA_CHUNK = 64

MOE_GROUPS = 4
MOE_PER_GROUP = 8
MOE_EXPERTS = 32
MOE_HIDDEN = 256
MOE_LANE0 = MOE_GROUPS

LANES = 128
VMEM_LIMIT = 56 * 1024 * 1024


def _cp(*sem):
    return pltpu.CompilerParams(dimension_semantics=sem, vmem_limit_bytes=VMEM_LIMIT)


def _dot(a, b):
    return jnp.dot(a, b, preferred_element_type=F32)


def _dot_nt(a, b):
    return lax.dot_general(a, b, (((1,), (1,)), ((), ())), preferred_element_type=F32)


def _dot_tn(a, b):
    return lax.dot_general(a, b, (((0,), (0,)), ((), ())), preferred_element_type=F32)


def _sigmoid(x):
    return 1.0 / (1.0 + jnp.exp(-x))


def _silu(x):
    return x * _sigmoid(x)


def _gelu_tanh(x):
    return 0.5 * x * (1.0 + jnp.tanh(math.sqrt(2.0 / math.pi) * (x + 0.044715 * (x * x * x))))


def _split_hi_lo(x):
    hi = x.astype(BF16)
    lo = (x - hi.astype(F32)).astype(BF16)
    return hi, lo


def _iota(shape, dim):
    return lax.broadcasted_iota(jnp.int32, shape, dim)


def _head_ones(width):
    return (_iota((LANES, LANES), 0) // width == _iota((LANES, LANES), 1) // width).astype(BF16)


def _head_sum(x, ones, exact):
    if exact:
        hi, lo = _split_hi_lo(x)
        return _dot(hi, ones) + _dot(lo, ones)
    return _dot(x.astype(BF16), ones)


def _adaln_kernel(c_ref, w_ref, b_ref, o_ref):
    c = c_ref[...]
    o_ref[...] = _dot(_silu(c).astype(BF16), w_ref[...].astype(BF16)) + b_ref[...]


def _adaln(cc, w_mod, b_mod):
    depth, d, n = w_mod.shape
    bn = 1536
    return pl.pallas_call(
        _adaln_kernel,
        out_shape=jax.ShapeDtypeStruct((depth, 8, n), F32),
        grid=(depth, n // bn),
        in_specs=[
            pl.BlockSpec((8, d), lambda l, j: (0, 0)),
            pl.BlockSpec((None, d, bn), lambda l, j: (l, 0, j)),
            pl.BlockSpec((None, 1, bn), lambda l, j: (l, 0, j)),
        ],
        out_specs=pl.BlockSpec((None, 8, bn), lambda l, j: (l, 0, j)),
        compiler_params=_cp("parallel", "parallel"),
        name="adaln",
    )(cc, w_mod, b_mod.reshape(depth, 1, n))


def _mod_spec(layer, part, bm, nc, seq):
    def index(i, *_):
        r0 = i * bm
        return (layer, jnp.where(r0 < nc, 2, (r0 - nc) // seq), part, 0, 0)
    return pl.BlockSpec((None, None, None, 1, D_MODEL), index)


def _normmod_kernel(x_ref, g_ref, sh_ref, sc_ref, o_ref):
    x = x_ref[...]
    y = x * lax.rsqrt(jnp.mean(x * x, axis=-1, keepdims=True) + NORM_EPS) * g_ref[...]
    o_ref[...] = (y * (1.0 + sc_ref[...]) + sh_ref[...]).astype(o_ref.dtype)


def _normmod(x, g, mod, layer, parts, bm, nc, seq):
    n, d = x.shape
    return pl.pallas_call(
        _normmod_kernel,
        out_shape=jax.ShapeDtypeStruct((n, d), BF16),
        grid=(n // bm,),
        in_specs=[
            pl.BlockSpec((bm, d), lambda i: (i, 0)),
            pl.BlockSpec((None, 1, d), lambda i: (layer, 0, 0)),
            _mod_spec(layer, parts[0], bm, nc, seq),
            _mod_spec(layer, parts[1], bm, nc, seq),
        ],
        out_specs=pl.BlockSpec((bm, d), lambda i: (i, 0)),
        compiler_params=_cp("parallel"),
        name="normmod",
    )(x, g, mod, mod)


def _mm_kernel(x_ref, w_ref, o_ref):
    o_ref[...] = _dot(x_ref[...], w_ref[...]).astype(o_ref.dtype)


def _mm(x, w, bm, bn, out_dtype=F32):
    m, k = x.shape
    n = w.shape[1]
    return pl.pallas_call(
        _mm_kernel,
        out_shape=jax.ShapeDtypeStruct((m, n), out_dtype),
        grid=(m // bm, n // bn),
        in_specs=[pl.BlockSpec((bm, k), lambda i, j: (i, 0)), pl.BlockSpec((k, bn), lambda i, j: (0, j))],
        out_specs=pl.BlockSpec((bm, bn), lambda i, j: (i, j)),
        compiler_params=_cp("parallel", "parallel"),
        name="mm",
    )(x, w)


def _seq_blocks(bsz, nc_blocks, nl_blocks, reverse):
    def local(s):
        lat = nc_blocks + ((nl_blocks - 1 - (s - nc_blocks)) if reverse else (s - nc_blocks))
        ctx = (nc_blocks - 1 - s) if reverse else s
        return jnp.where(s < nc_blocks, ctx, lat)

    def block(b, s):
        i = local(s)
        return jnp.where(i < nc_blocks, b * nc_blocks + i, bsz * nc_blocks + b * nl_blocks + i - nc_blocks)

    return local, block


S5_PACK = 8
S5_PACKS = S5_GROUPS // S5_PACK
S5_PLANE = S5_PACK * S5_STATE
S5_BLOCK = 256
S5_ROWS = 24


def _s5_tables(a_re, a_im, log_dt, b_re, b_im, c_re, c_im):
    lam = lax.complex(jnp.minimum(a_re, S5_MAX_RE), a_im)
    ldt = lam * jnp.exp(log_dt)[:, None]
    lam_bar = jnp.exp(ldt)
    b_bar = ((lam_bar - 1.0) / lam)[..., None] * lax.complex(b_re, b_im)
    c_mat = lax.complex(c_re, c_im)
    eye = jnp.eye(S5_PACK, dtype=F32)

    def block_diag(t):
        k, g, a, b = t.shape
        return (t[:, :, :, None, :] * eye[None, :, None, :, None]).reshape(k, g * a, g * b)

    b_t = jnp.transpose(b_bar, (0, 2, 1)).reshape(S5_PACKS, S5_PACK, S5_GROUP, S5_STATE)
    bblk = jnp.concatenate([block_diag(b_t.real), block_diag(b_t.imag)], axis=2)
    c_t = jnp.transpose(c_mat, (0, 2, 1)).reshape(S5_PACKS, S5_PACK, S5_STATE, S5_GROUP)
    cblk = jnp.concatenate([block_diag(c_t.real), block_diag(-c_t.imag)], axis=1)
    expo = jnp.concatenate([jnp.arange(1, S5_TILE + 1, dtype=F32), jnp.asarray([32.0, 64.0, 128.0], F32),
                            jnp.zeros((S5_ROWS - S5_TILE - 3,), F32)])
    pw = jnp.exp(ldt[None] * expo[:, None, None]).reshape(S5_ROWS, S5_PACKS, S5_PLANE)
    pw = jnp.transpose(pw, (1, 0, 2))
    return bblk.astype(BF16), cblk.astype(BF16), pw.real, pw.imag


def _s5_scan_kernel(u_ref, bblk_ref, cblk_ref, pre_ref, pim_ref, y_ref, car_ref, up_s, h2_s, hb2_s, yp_s, *, reverse):
    t = S5_TILE
    n_t = S5_BLOCK // t
    pn = S5_PLANE

    @pl.when(pl.program_id(1) == 0)
    def _():
        car_ref[...] = jnp.zeros_like(car_ref)

    ra = _iota((S5_BLOCK, S5_BLOCK), 0)
    cb = _iota((S5_BLOCK, S5_BLOCK), 1)
    perm = ((ra // t == cb % t) & (ra % t == cb // t)).astype(BF16)
    up = _dot(perm, u_ref[...].astype(BF16)).astype(BF16)
    for pk in range(S5_PACKS):
        up_s[pk] = up[:, LANES * pk:LANES * (pk + 1)]
    order = list(range(t - 1, -1, -1)) if reverse else list(range(t))
    rowj = _iota((n_t, 1), 0)

    def states(pk, bu, h_s, hb_s):
        pre = pre_ref[pk]
        pim = pim_ref[pk]
        l_re, l_im = pre[0:1], pim[0:1]
        h_re = h_im = None
        for n, s in enumerate(order):
            rows = slice(t * s, t * (s + 1))
            b_re, b_im = bu[rows, :pn], bu[rows, pn:]
            if n == 0:
                h_re, h_im = b_re, b_im
            else:
                h_re, h_im = l_re * h_re - l_im * h_im + b_re, l_re * h_im + l_im * h_re + b_im
            h_s[rows, :pn] = h_re
            h_s[rows, pn:] = h_im
        c_re, c_im = car_ref[pk, 0:1, :pn], car_ref[pk, 0:1, pn:]
        first = rowj == (n_t - 1 if reverse else 0)
        g_re, g_im = pre[t - 1:t], pim[t - 1:t]
        e_re = h_re + jnp.where(first, g_re * c_re - g_im * c_im, 0.0)
        e_im = h_im + jnp.where(first, g_re * c_im + g_im * c_re, 0.0)
        step = 1
        for row in (t - 1, t, t + 1, t + 2):
            if reverse:
                s_re, s_im, ok = pltpu.roll(e_re, n_t - step, 0), pltpu.roll(e_im, n_t - step, 0), rowj < n_t - step
            else:
                s_re, s_im, ok = pltpu.roll(e_re, step, 0), pltpu.roll(e_im, step, 0), rowj >= step
            a_re, a_im = pre[row:row + 1], pim[row:row + 1]
            e_re = e_re + jnp.where(ok, a_re * s_re - a_im * s_im, 0.0)
            e_im = e_im + jnp.where(ok, a_re * s_im + a_im * s_re, 0.0)
            step *= 2
        last = 0 if reverse else n_t - 1
        car_ref[pk, 0:1, :pn] = e_re[last:last + 1]
        car_ref[pk, 0:1, pn:] = e_im[last:last + 1]
        if reverse:
            in_re = jnp.where(first, c_re, pltpu.roll(e_re, n_t - 1, 0))
            in_im = jnp.where(first, c_im, pltpu.roll(e_im, n_t - 1, 0))
        else:
            in_re = jnp.where(first, c_re, pltpu.roll(e_re, 1, 0))
            in_im = jnp.where(first, c_im, pltpu.roll(e_im, 1, 0))
        for n, s in enumerate(order):
            rows = slice(t * s, t * (s + 1))
            a_re, a_im = pre[n:n + 1], pim[n:n + 1]
            hb_s[rows, :pn] = (h_s[rows, :pn] + a_re * in_re - a_im * in_im).astype(BF16)
            hb_s[rows, pn:] = (h_s[rows, pn:] + a_re * in_im + a_im * in_re).astype(BF16)

    def pack_pair(j, carry):
        pks = (2 * j, 2 * j + 1)
        bu = [_dot(up_s[pk], bblk_ref[pk]) for pk in pks]
        for u, pk in enumerate(pks):
            states(pk, bu[u], h2_s.at[u], hb2_s.at[u])
            yp_s[pk] = _dot(hb2_s[u], cblk_ref[pk]).astype(BF16)
        return carry

    lax.fori_loop(0, S5_PACKS // 2, pack_pair, 0)
    for pk in range(S5_PACKS):
        y_ref[:, LANES * pk:LANES * (pk + 1)] = _dot(perm, yp_s[pk])


def _s5_scan(u_src, tables, bsz, nc_blocks, nl_blocks, reverse):
    bblk, cblk, pre, pim = tables
    rows = u_src.shape[0]
    blk = S5_BLOCK
    _, block = _seq_blocks(bsz, nc_blocks, nl_blocks, reverse)
    whole = lambda a: pl.BlockSpec(a.shape, lambda b, s: (0,) * a.ndim)
    return pl.pallas_call(
        functools.partial(_s5_scan_kernel, reverse=reverse),
        out_shape=jax.ShapeDtypeStruct((rows, S5_WIDTH), F32),
        grid=(bsz, nc_blocks + nl_blocks),
        in_specs=[pl.BlockSpec((blk, S5_WIDTH), lambda b, s: (block(b, s), 0)),
                  whole(bblk), whole(cblk), whole(pre), whole(pim)],
        out_specs=pl.BlockSpec((blk, S5_WIDTH), lambda b, s: (block(b, s), 0)),
        scratch_shapes=[
            pltpu.VMEM((S5_PACKS, 8, 2 * S5_PLANE), F32),
            pltpu.VMEM((S5_PACKS, blk, LANES), BF16),
            pltpu.VMEM((2, blk, 2 * S5_PLANE), F32),
            pltpu.VMEM((2, blk, 2 * S5_PLANE), BF16),
            pltpu.VMEM((S5_PACKS, blk, LANES), BF16),
        ],
        compiler_params=_cp("parallel", "arbitrary"),
        name="s5_bwd" if reverse else "s5_fwd",
    )(u_src, bblk, cblk, pre, pim)


def _gla_kernel(q_ref, k_ref, v_ref, gk_ref, up_ref, gb_ref, s0_ref, o_ref, sf_ref, st_ref):
    d = pl.program_id(0)
    c = pl.program_id(1)
    n = pl.num_programs(1)

    @pl.when(c == 0)
    def _():
        st_ref[...] = s0_ref[...]

    cs = GLA_CHUNK
    nb = q_ref.shape[0]
    sign = 1 - 2 * d
    row = _iota((cs, cs), 0)
    col = _iota((cs, cs), 1)
    causal = (row - col) * sign >= 0
    causal_b = causal.astype(BF16)
    rid = _iota((cs, 1), 0)
    bcum, b_mid, b_end = [], [], []
    for b in range(nb):
        x = _dot(gk_ref[b].astype(BF16), up_ref[...]) + gb_ref[...]
        log_a = (jnp.minimum(x, 0.0) - jnp.log(1.0 + jnp.exp(-jnp.abs(x)))) * (1.0 / GLA_TAU)
        la_hi, la_lo = _split_hi_lo(log_a)
        bc = _dot(causal_b, la_hi) + _dot(causal_b, la_lo)
        bcum.append(bc)
        b_mid.append(jnp.sum(jnp.where(rid == cs // 2 - d, bc, 0.0), axis=0, keepdims=True))
        b_end.append(jnp.sum(jnp.where(rid == (cs - 1) * (1 - d), bc, 0.0), axis=0, keepdims=True))
    units = [(b, h) for b in range(nb) for h in range(GLA_HEADS)]
    ks = lambda h: slice(GLA_DK * h, GLA_DK * (h + 1))
    vs = lambda h: slice(GLA_DV * h, GLA_DV * (h + 1))
    s_old = [st_ref[b, h] for b, h in units]
    qh = [q_ref[b, :, ks(h)] * (GLA_DK ** -0.5) for b, h in units]
    kh = [k_ref[b, :, ks(h)] for b, h in units]
    vh = [v_ref[b, :, vs(h)].astype(BF16) for b, h in units]
    bh = [bcum[b][:, ks(h)] for b, h in units]
    mid = [b_mid[b][:, ks(h)] for b, h in units]
    end = [b_end[b][:, ks(h)] for b, h in units]
    scores = [_dot_nt((q * jnp.exp(x - m)).astype(BF16), (k * jnp.exp(m - x)).astype(BF16))
              for q, k, x, m in zip(qh, kh, bh, mid)]
    inter = [_dot_nt((q * jnp.exp(x)).astype(BF16), s.astype(BF16)) for q, x, s in zip(qh, bh, s_old)]
    upd = [_dot_tn(v, (k * jnp.exp(e - x)).astype(BF16)) for v, k, e, x in zip(vh, kh, end, bh)]
    intra = [_dot(jnp.where(causal, sc, 0.0).astype(BF16), v) for sc, v in zip(scores, vh)]
    for u, (b, h) in enumerate(units):
        o_ref[b, :, vs(h)] = intra[u] + inter[u]
        st_ref[b, h] = s_old[u] * jnp.exp(end[u]) + upd[u]

    @pl.when(c == n - 1)
    def _():
        sf_ref[...] = st_ref[...]


def _gla(qk, v, gk, gk_up, gk_b, s0):
    bsz, cs, w = v.shape
    n = w // GLA_V

    def chunk(d, c):
        return c + d * (n - 1 - 2 * c)

    state_spec = pl.BlockSpec((None, bsz, GLA_HEADS, GLA_DV, GLA_DK), lambda d, c: (d, 0, 0, 0, 0))
    return pl.pallas_call(
        _gla_kernel,
        out_shape=(jax.ShapeDtypeStruct((2, bsz, cs, w), F32), jax.ShapeDtypeStruct(s0.shape, F32)),
        grid=(2, n),
        in_specs=[
            pl.BlockSpec((bsz, cs, GLA_QK), lambda d, c: (0, 0, 2 * chunk(d, c))),
            pl.BlockSpec((bsz, cs, GLA_QK), lambda d, c: (0, 0, 2 * chunk(d, c) + 1)),
            pl.BlockSpec((bsz, cs, GLA_V), lambda d, c: (0, 0, chunk(d, c))),
            pl.BlockSpec((bsz, cs, LANES), lambda d, c: (0, 0, chunk(d, c))),
            pl.BlockSpec((None, LANES, GLA_QK), lambda d, c: (d, 0, 0)),
            pl.BlockSpec((None, 1, GLA_QK), lambda d, c: (d, 0, 0)),
            state_spec,
        ],
        out_specs=(pl.BlockSpec((None, bsz, cs, GLA_V), lambda d, c: (d, 0, 0, chunk(d, c))), state_spec),
        scratch_shapes=[pltpu.VMEM((bsz, GLA_HEADS, GLA_DV, GLA_DK), F32)],
        compiler_params=_cp("parallel", "arbitrary"),
        name="gla",
    )(qk, qk, v, gk, gk_up, gk_b, s0)


def _to_chunk_major(t, bsz):
    n = t.shape[0] // bsz
    d = t.shape[1]
    t = t.reshape(bsz, n // GLA_CHUNK, GLA_CHUNK, d)
    return jnp.transpose(t, (0, 2, 1, 3)).reshape(bsz, GLA_CHUNK, (n // GLA_CHUNK) * d)


def _from_chunk_major(t, d):
    lead = t.shape[:-2]
    n = t.shape[-1] // d
    t = t.reshape(lead + (GLA_CHUNK, n, d))
    return jnp.swapaxes(t, -3, -2).reshape(lead + (n * GLA_CHUNK, d))


def _rwkv_kernel(z0_ref, z1_ref, halo0_ref, halo1_ref, mu_ref, w0_ref, wup_ref, a0_ref, aup_ref, kk_ref, ka_ref,
                 rk_ref, y_ref, st_ref, kt_s, bt_s, kq_s, rt_s, v_s, w_s, u_s, ya_s, ab_s, pin_s,
                 *, reverse, nc_blocks):
    step = pl.program_id(0)
    blk = RW_BLOCK
    ch = RW_CHUNK
    n_ch = blk // ch
    n_b = 2
    n_units = n_b * RW_PAIRS

    @pl.when(step == 0)
    def _():
        st_ref[...] = jnp.zeros_like(st_ref)

    rowi = _iota((blk, 1), 0)
    low = _iota((1, LANES), 1) < RW_HEAD
    ones64 = _head_ones(RW_HEAD)
    pos = rowi % ch
    seq_start = (step == 0) | (step == nc_blocks)

    for b, (z_ref, halo_ref) in enumerate(((z0_ref, halo0_ref), (z1_ref, halo1_ref))):
        z = z_ref[...]
        if reverse:
            prev = pltpu.roll(z, blk - 1, 0)
            edge = halo_ref[0:1, :]
            at_edge = rowi == blk - 1
        else:
            prev = pltpu.roll(z, 1, 0)
            edge = halo_ref[7:8, :]
            at_edge = rowi == 0
        prev = jnp.where(at_edge, jnp.where(seq_start, 0.0, edge), prev)
        zs = z + (prev - z) * mu_ref[...]
        r = zs[:, 0:RW_WIDTH]
        k = zs[:, RW_WIDTH:2 * RW_WIDTH]
        v = zs[:, 2 * RW_WIDTH:3 * RW_WIDTH]
        lora = zs[:, 3 * RW_WIDTH:3 * RW_WIDTH + LANES]
        lora_w = jnp.where(low, jnp.tanh(lora), 0.0).astype(BF16)
        lora_a = jnp.where(low, 0.0, lora).astype(BF16)
        logw = -RW_DECAY_SCALE * _sigmoid(w0_ref[...] + _dot(lora_w, wup_ref[...]))
        a = _sigmoid(a0_ref[...] + _dot(lora_a, aup_ref[...]))
        kk = k * kk_ref[...]
        kp = k * (1.0 + (a - 1.0) * ka_ref[...])
        rkb = r * kp * rk_ref[...]

        cl = logw
        sh = 1
        while sh < ch:
            if reverse:
                cl = cl + jnp.where(pos < ch - sh, pltpu.roll(cl, blk - sh, 0), 0.0)
            else:
                cl = cl + jnp.where(pos >= sh, pltpu.roll(cl, sh, 0), 0.0)
            sh *= 2
        p_in = jnp.exp(cl)
        pin_s[b] = p_in
        p_ex = jnp.exp(cl - logw)
        p_inv = jnp.exp(-cl)

        for p in range(RW_PAIRS):
            ls = slice(LANES * p, LANES * (p + 1))
            q = b * RW_PAIRS + p
            kkp = kk[:, ls]
            ssq = _head_sum(kkp * kkp, ones64, False)
            kkn = kkp * (1.0 / jnp.maximum(jnp.sqrt(ssq), 1e-12))
            kt_s[q] = (kkn * p_ex[:, ls]).astype(BF16)
            bt_s[q] = (kkn * a[:, ls] * p_inv[:, ls]).astype(BF16)
            kq_s[q] = (kp[:, ls] * p_inv[:, ls]).astype(BF16)
            rt_s[q] = (r[:, ls] * p_in[:, ls]).astype(BF16)
            v_s[q] = v[:, ls].astype(BF16)
            ya_s[q] = _head_sum(rkb[:, ls], ones64, False) * v[:, ls]

    rr = _iota((blk, blk), 0)
    cc = _iota((blk, blk), 1)
    same = rr // ch == cc // ch
    before = (cc > rr) if reverse else (cc < rr)
    strict = same & before
    incl = same & (before | (rr == cc))
    fold0 = (_iota((blk, LANES), 0) % ch == _iota((blk, LANES), 1)).astype(BF16)
    fold1 = (_iota((blk, LANES), 0) % ch + ch == _iota((blk, LANES), 1)).astype(BF16)
    lane_lo = _iota((1, LANES), 1) < RW_HEAD

    def pair_body(p, carry):
        units = [b * RW_PAIRS + p for b in range(n_b)]
        heads = [(u, hh) for u in range(n_b) for hh in range(2)]
        kt = [kt_s[q] for q in units]
        rt = [rt_s[q] for q in units]
        vv = [v_s[q] for q in units]
        ya = [ya_s[q] for q in units]
        bt = [bt_s[q] for q in units]
        kq = [kq_s[q] for q in units]
        hb = blk // 2
        g = []
        for u, hh in heads:
            mine = lane_lo if hh == 0 else jnp.logical_not(lane_lo)
            zero = jnp.zeros_like(kt[u])
            ktm, rtm = jnp.where(mine, kt[u], zero), jnp.where(mine, rt[u], zero)
            halves = []
            for rows in (slice(0, hb), slice(hb, blk)):
                lhs = jnp.concatenate([ktm[rows], rtm[rows]], axis=0)
                rhs = jnp.concatenate([bt[u][rows], kq[u][rows]], axis=0)
                halves.append(_dot_nt(lhs, rhs))
            g.append(halves)
        zero_q = jnp.zeros((hb, hb), F32)

        def diag2(gh, r0, c0):
            a, b = gh[0][r0:r0 + hb, c0:c0 + hb], gh[1][r0:r0 + hb, c0:c0 + hb]
            return jnp.concatenate([jnp.concatenate([a, zero_q], axis=1), jnp.concatenate([zero_q, b], axis=1)],
                                   axis=0)

        n1 = [jnp.where(strict, -diag2(gh, 0, 0), 0.0).astype(BF16) for gh in g]
        a_kq = [jnp.where(strict, diag2(gh, 0, hb), 0.0).astype(BF16) for gh in g]
        a_rb = [jnp.where(incl, diag2(gh, hb, 0), 0.0).astype(BF16) for gh in g]
        a_rq = [jnp.where(incl, diag2(gh, hb, hb), 0.0).astype(BF16) for gh in g]
        n2 = [_dot(n, n).astype(BF16) for n in n1]
        akv = [_dot(a, vv[u]) for a, (u, _) in zip(a_kq, heads)]
        n4 = [_dot(n, n).astype(BF16) for n in n2]
        y_in = [_dot(a, vv[u]) for a, (u, _) in zip(a_rq, heads)]
        n8 = [_dot(n, n).astype(BF16) for n in n4]
        fold = [_dot(a, fold0 if hh == 0 else fold1) for a, (_, hh) in zip(a_rb, heads)]
        rhs_t = [jnp.concatenate([kt[u].astype(F32), av], axis=1) for av, (u, _) in zip(akv, heads)]
        for nk in (n8, n4, n2, n1):
            rhs_t = [x + _dot(n, x.astype(BF16)) for n, x in zip(nk, rhs_t)]
        for u, q in enumerate(units):
            h0, h1 = 2 * u, 2 * u + 1
            w_s[q] = jnp.where(lane_lo, rhs_t[h0][:, :LANES], rhs_t[h1][:, :LANES]).astype(BF16)
            u_s[q] = jnp.where(lane_lo, rhs_t[h0][:, LANES:], rhs_t[h1][:, LANES:])
            ya_s[q] = ya[u] + jnp.where(lane_lo, y_in[h0], y_in[h1])
            ab_s[q] = (fold[h0] + fold[h1]).astype(BF16)
        return carry

    lax.fori_loop(0, RW_PAIRS, pair_body, 0)

    blockdiag = (_iota((LANES, LANES), 0) // RW_HEAD) == (_iota((LANES, LANES), 1) // RW_HEAD)
    end_row = 0 if reverse else ch - 1

    def chunk_body(i, carry):
        c = (n_ch - 1 - i) if reverse else i
        rows = pl.ds(pl.multiple_of(c * ch, ch), ch)
        s_old = [st_ref[q] for q in range(n_units)]
        m1 = [_dot_nt(jnp.concatenate([w_s[q, rows, :], rt_s[q, rows, :]], axis=0), s_old[q].astype(BF16))
              for q in range(n_units)]
        zc = [-(m1[q][:ch] + u_s[q, rows, :]) for q in range(n_units)]
        upd = []
        for q in range(n_units):
            zv = jnp.concatenate([zc[q].astype(BF16), v_s[q, rows, :]], axis=0)
            bk = jnp.concatenate([bt_s[q, rows, :], kq_s[q, rows, :]], axis=0)
            upd.append(_dot_tn(zv, bk))
        yc = []
        for q in range(n_units):
            z2 = jnp.concatenate([jnp.where(lane_lo, zc[q], 0.0), jnp.where(lane_lo, 0.0, zc[q])], axis=0)
            yc.append(m1[q][ch:] + _dot(ab_s[q, rows, :][:, :2 * ch], z2.astype(BF16)) + ya_s[q, rows, :])
        for q in range(n_units):
            b, p = divmod(q, RW_PAIRS)
            p_end = pin_s[b, rows, LANES * p:LANES * (p + 1)][end_row:end_row + 1]
            st_ref[q] = (s_old[q] + jnp.where(blockdiag, upd[q], 0.0)) * p_end
            y_ref[b, rows, LANES * p:LANES * (p + 1)] = yc[q]
        return carry

    lax.fori_loop(0, n_ch, chunk_body, 0)


def _rwkv_direction(zall, prm, bsz, nc_blocks, nl_blocks, reverse):
    assert bsz == 2
    mu, w0, wup, a0, aup, k_k, k_a, r_k = prm
    rows = zall.shape[0]
    blk = RW_BLOCK
    steps = nc_blocks + nl_blocks
    n_blocks = rows // blk

    def local(s):
        lat = nc_blocks + ((nl_blocks - 1 - (s - nc_blocks)) if reverse else (s - nc_blocks))
        ctx = (nc_blocks - 1 - s) if reverse else s
        return jnp.where(s < nc_blocks, ctx, lat)

    def block(b, s):
        i = local(s)
        return jnp.where(i < nc_blocks, b * nc_blocks + i, bsz * nc_blocks + b * nl_blocks + i - nc_blocks)

    def halo(b, s):
        i = block(b, s)
        if reverse:
            return jnp.minimum((i + 1) * (blk // 8), n_blocks * (blk // 8) - 1)
        return jnp.maximum(i * (blk // 8) - 1, 0)

    vec = lambda w: pl.BlockSpec((1, w), lambda s: (0, 0))
    mat = lambda: pl.BlockSpec((LANES, RW_WIDTH), lambda s: (0, 0))
    n_units = bsz * RW_PAIRS
    unit_bf = pltpu.VMEM((n_units, blk, LANES), BF16)
    unit_f = pltpu.VMEM((n_units, blk, LANES), F32)
    z_spec = lambda b: pl.BlockSpec((blk, RW_Z), lambda s: (block(b, s), 0))
    halo_spec = lambda b: pl.BlockSpec((8, RW_Z), lambda s: (halo(b, s), 0))
    return pl.pallas_call(
        functools.partial(_rwkv_kernel, reverse=reverse, nc_blocks=nc_blocks),
        out_shape=jax.ShapeDtypeStruct((bsz, steps * blk, RW_WIDTH), F32),
        grid=(steps,),
        in_specs=[
            z_spec(0), z_spec(1), halo_spec(0), halo_spec(1),
            vec(RW_Z), vec(RW_WIDTH), mat(), vec(RW_WIDTH), mat(), vec(RW_WIDTH), vec(RW_WIDTH), vec(RW_WIDTH),
        ],
        out_specs=pl.BlockSpec((bsz, blk, RW_WIDTH), lambda s: (0, local(s), 0)),
        scratch_shapes=[
            pltpu.VMEM((n_units, LANES, LANES), F32),
            unit_bf, unit_bf, unit_bf, unit_bf, unit_bf,
            unit_bf, unit_f, unit_f, unit_bf,
            pltpu.VMEM((bsz, blk, RW_WIDTH), F32),
        ],
        compiler_params=_cp("arbitrary"),
        name="rwkv_bwd" if reverse else "rwkv_fwd",
    )(zall, zall, zall, zall, mu, w0, wup, a0, aup, k_k, k_a, r_k)


def _branches_kernel(saf_ref, sab_ref, su_ref, sd_ref, rf_ref, rb_ref, rg_ref, gup_ref, lw_ref, lb_ref,
                     of_ref, ob_ref, og_ref, ng_ref, ga_ref, gb_ref, gc_ref, wglu_ref, wrw_ref, wgl_ref, m_ref):
    za = _gelu_tanh(saf_ref[...] + sab_ref[...] + sd_ref[...] * su_ref[...]).astype(BF16)
    hid = _dot(za, wglu_ref[...])
    m = _sigmoid(ga_ref[...]) * (hid[:, :D_MODEL] * _sigmoid(hid[:, D_MODEL:]))
    ones64 = _head_ones(RW_HEAD)
    gate = _dot(_sigmoid(rg_ref[...]).astype(BF16), gup_ref[...])
    zb = []
    for p in range(RW_PAIRS):
        ls = slice(LANES * p, LANES * (p + 1))
        y = rf_ref[:, ls] + rb_ref[:, ls]
        mean = _head_sum(y, ones64, True) * (1.0 / RW_HEAD)
        yc = y - mean
        var = _head_sum(yc * yc, ones64, True) * (1.0 / RW_HEAD)
        yn = yc * lax.rsqrt(var + RW_GN_EPS) * lw_ref[:, ls] + lb_ref[:, ls]
        zb.append((yn * gate[:, ls]).astype(BF16))
    m = m + _sigmoid(gb_ref[...]) * _dot(jnp.concatenate(zb, axis=1), wrw_ref[...])
    zc = []
    for h in range(GLA_HEADS):
        vs = slice(GLA_DV * h, GLA_DV * (h + 1))
        o = of_ref[:, vs] + ob_ref[:, vs]
        on = o * lax.rsqrt(jnp.mean(o * o, axis=-1, keepdims=True) + NORM_EPS) * ng_ref[:, vs]
        zc.append((on * _silu(og_ref[:, vs])).astype(BF16))
    m = m + _sigmoid(gc_ref[...]) * _dot(jnp.concatenate(zc, axis=1), wgl_ref[...])
    m_ref[...] = m.astype(BF16)


def _branches(ya, yb, o_dir, pa_in, zall, prm, row0, nc_blocks, nl_blocks):
    s5_d, g_up, ln_w, ln_b, norm_g, w_glu, w_rw, w_gl = prm
    bm = RW_BLOCK
    bsz = yb[0].shape[0]
    n = zall.shape[0] - row0 * bm
    rows = lambda w, blk=0: pl.BlockSpec((bm, w), lambda i: (i + row0, blk))

    def y_index(i):
        i = i + row0
        t = i - bsz * nc_blocks
        return (jnp.where(t < 0, i // nc_blocks, t // nl_blocks),
                jnp.where(t < 0, i % nc_blocks, nc_blocks + t % nl_blocks), 0)

    y_spec = pl.BlockSpec((None, bm, RW_WIDTH), y_index)
    vec = pl.BlockSpec((1, 1024), lambda i: (0, 0))
    const = lambda a: pl.BlockSpec(a.shape, lambda i: (0,) * a.ndim, pipeline_mode=pl.Buffered(1))
    return pl.pallas_call(
        _branches_kernel,
        out_shape=jax.ShapeDtypeStruct((n, D_MODEL), BF16),
        grid=(n // bm,),
        in_specs=[rows(1024), rows(1024), rows(1024, 0), vec,
                  y_spec, y_spec, rows(LANES, RW_Z // LANES), const(g_up), vec, vec,
                  rows(1024), rows(1024), rows(1024, 1), vec,
                  rows(D_MODEL, 1), rows(D_MODEL, 2), rows(D_MODEL, 3),
                  const(w_glu), const(w_rw), const(w_gl)],
        out_specs=pl.BlockSpec((bm, D_MODEL), lambda i: (i, 0)),
        compiler_params=_cp("parallel"),
        name="branches",
    )(ya[0], ya[1], pa_in, s5_d, yb[0], yb[1], zall, g_up, ln_w, ln_b,
      o_dir[0], o_dir[1], pa_in, norm_g, pa_in, pa_in, pa_in, w_glu, w_rw, w_gl)


def _out_proj_kernel(m_ref, w_ref, x_ref, gate_ref, o_ref):
    o_ref[...] = x_ref[...] + gate_ref[...] * _dot(m_ref[...], w_ref[...])


def _out_proj(m, w_out, x, mod, layer, bm, row0, nc, seq):
    n, d = m.shape
    bn = 1024
    mod_index = _mod_spec(layer, 2, bm, nc, seq).index_map
    return pl.pallas_call(
        _out_proj_kernel,
        out_shape=jax.ShapeDtypeStruct((n, d), F32),
        grid=(n // bm, d // bn),
        in_specs=[pl.BlockSpec((bm, d), lambda i, j: (i, 0)),
                  pl.BlockSpec((d, bn), lambda i, j: (0, j)),
                  pl.BlockSpec((bm, bn), lambda i, j: (i + row0, j)),
                  pl.BlockSpec((None, None, None, 1, bn), lambda i, j: mod_index(i + row0)[:4] + (j,))],
        out_specs=pl.BlockSpec((bm, bn), lambda i, j: (i, j)),
        compiler_params=_cp("parallel", "parallel"),
        name="out_proj",
    )(m, w_out, x, mod)


MOE_TM = 256
MOE_ROW_TILES = D_MODEL // (2 * LANES)
META_E1, META_E2, META_R1, META_R2, META_W1, META_W2 = range(6)


def _router_kernel(x_ref, g_ref, sh_ref, sc_ref, wr_hi_ref, wr_lo_ref, br_ref, v_ref, meta_ref, meta_t_ref, cnt_ref,
                   base_s):
    @pl.when(pl.program_id(0) == 0)
    def _():
        base_s[...] = jnp.zeros_like(base_s)

    x = x_ref[...]
    y = x * lax.rsqrt(jnp.mean(x * x, axis=-1, keepdims=True) + NORM_EPS) * g_ref[...]
    t = y * (1.0 + sc_ref[...]) + sh_ref[...]
    bits = lax.bitcast_convert_type(t.astype(BF16).astype(F32), jnp.uint32)
    half = D_MODEL // 2
    word = (bits[:, half:] & jnp.uint32(0xFFFF0000)) | (bits[:, :half] >> 16)
    for s in range(MOE_ROW_TILES):
        v_ref[pl.ds(s, x.shape[0], stride=MOE_ROW_TILES), :] = word[:, LANES * s:LANES * (s + 1)]
    t_hi, t_lo = _split_hi_lo(t)
    logits = (_dot(t_hi, wr_hi_ref[...]) + _dot(t_lo, wr_hi_ref[...]) + _dot(t_hi, wr_lo_ref[...])) + br_ref[...]
    lane = _iota(logits.shape, 1).astype(F32)
    neg = jnp.float32(-jnp.inf)
    big = jnp.float32(LANES)
    l1 = jnp.where(lane < MOE_GROUPS, logits, neg)
    m1 = jnp.max(l1, axis=-1, keepdims=True)
    p_top = 1.0 / jnp.sum(jnp.exp(l1 - m1), axis=-1, keepdims=True)
    grp = jnp.min(jnp.where(l1 == m1, lane, big), axis=-1, keepdims=True)
    lo = MOE_LANE0 + MOE_PER_GROUP * grp
    in_grp = (lane >= lo) & (lane < lo + MOE_PER_GROUP)
    l2 = jnp.where(in_grp, logits, neg)
    v1 = jnp.max(l2, axis=-1, keepdims=True)
    i1 = jnp.min(jnp.where(l2 == v1, lane, big), axis=-1, keepdims=True)
    l3 = jnp.where(lane == i1, neg, l2)
    v2 = jnp.max(l3, axis=-1, keepdims=True)
    i2 = jnp.min(jnp.where(l3 == v2, lane, big), axis=-1, keepdims=True)
    e2 = jnp.exp(v2 - v1)
    w1 = p_top / (1.0 + e2)
    w2 = p_top * e2 / (1.0 + e2)
    pick1 = lane == i1
    pick2 = lane == i2
    chosen = jnp.where(pick1 | pick2, 1.0, 0.0)
    bm = x.shape[0]
    earlier = (_iota((bm, bm), 1) < _iota((bm, bm), 0)).astype(BF16)
    before = _dot(earlier, chosen.astype(BF16)) + base_s[...]
    r1 = jnp.sum(jnp.where(pick1, before, 0.0), axis=-1, keepdims=True)
    r2 = jnp.sum(jnp.where(pick2, before, 0.0), axis=-1, keepdims=True)
    base_s[...] += jnp.sum(chosen, axis=0, keepdims=True)
    cnt_ref[...] = base_s[...]
    meta = jnp.zeros_like(logits)
    for slot, val in ((META_E1, i1 - MOE_LANE0), (META_E2, i2 - MOE_LANE0), (META_R1, r1), (META_R2, r2),
                      (META_W1, w1), (META_W2, w2)):
        meta = jnp.where(lane == slot, val, meta)
    meta_ref[...] = meta
    meta_t_ref[...] = meta.T[:8]


def _router(x, g, mod, layer, wr_hi, wr_lo, br, bm, nc, seq):
    n, d = x.shape
    return pl.pallas_call(
        _router_kernel,
        out_shape=(jax.ShapeDtypeStruct((n * MOE_ROW_TILES, LANES), jnp.uint32), jax.ShapeDtypeStruct((n, LANES), F32),
                   jax.ShapeDtypeStruct((8, n), F32), jax.ShapeDtypeStruct((1, LANES), F32)),
        grid=(n // bm,),
        in_specs=[
            pl.BlockSpec((bm, d), lambda i: (i, 0)),
            pl.BlockSpec((None, 1, d), lambda i: (layer, 0, 0)),
            _mod_spec(layer, 3, bm, nc, seq), _mod_spec(layer, 4, bm, nc, seq),
            pl.BlockSpec((d, LANES), lambda i: (0, 0)), pl.BlockSpec((d, LANES), lambda i: (0, 0)),
            pl.BlockSpec((1, LANES), lambda i: (0, 0)),
        ],
        out_specs=(pl.BlockSpec((bm * MOE_ROW_TILES, LANES), lambda i: (i, 0)),
                   pl.BlockSpec((bm, LANES), lambda i: (i, 0)),
                   pl.BlockSpec((8, bm), lambda i: (0, i)), pl.BlockSpec((1, LANES), lambda i: (0, 0))),
        scratch_shapes=[pltpu.VMEM((1, LANES), F32)],
        compiler_params=_cp("arbitrary"),
        name="moe_router",
    )(x, g, mod, mod, wr_hi, wr_lo, br)


def _moe_plan(meta_t, cnt):
    tm = MOE_TM
    n_tok = meta_t.shape[1]
    counts = cnt[0, MOE_LANE0:MOE_LANE0 + MOE_EXPERTS].astype(jnp.int32)
    seg = ((counts + tm - 1) // tm) * tm
    ends = jnp.cumsum(seg)
    off = ends - seg
    rec = meta_t[:4].astype(jnp.int32)
    first_row = jnp.sum(jnp.where(rec[:2, None, :] == jnp.arange(MOE_EXPERTS)[None, :, None], off[None, :, None], 0),
                        axis=1)
    dest = first_row + rec[2:4]
    n_rows = 2 * n_tok + MOE_EXPERTS * tm
    n_tiles = n_rows // tm
    tile_e = jnp.sum(((jnp.arange(n_tiles) * tm)[:, None] >= ends[None, :]).astype(jnp.int32), axis=1)
    tile_e = jnp.minimum(tile_e, MOE_EXPERTS - 1)
    tok = jnp.broadcast_to(jnp.arange(n_tok, dtype=jnp.int32), (2, n_tok))
    src = jnp.zeros((n_rows,), jnp.int32).at[dest.reshape(-1)].set(tok.reshape(-1))
    return dest[0], dest[1], src, tile_e, (ends[-1] // tm).reshape(1)


def _row_copy(src_hbm, row, dst, slot, r, sem):
    return pltpu.make_async_copy(src_hbm.at[pl.ds(row, 1)], dst.at[slot, pl.ds(r, 1)], sem)


def _experts_kernel(src_ref, te_ref, nu_ref, v_hbm, wg_ref, wu_ref, wd_ref, y_ref, xbuf, sem):
    del te_ref
    i = pl.program_id(0)
    n_used = nu_ref[0]
    slot = i % 2

    nt = MOE_ROW_TILES

    def token_copy(tok, into, r):
        return pltpu.make_async_copy(v_hbm.at[pl.ds(tok * nt, nt)], xbuf.at[into, pl.ds(r * nt, nt)], sem.at[into])

    def gather(tile, into):
        for r in range(MOE_TM):
            token_copy(src_ref[tile * MOE_TM + r], into, r).start(priority=r % 2)

    def expert():
        for r in range(MOE_TM):
            token_copy(0, slot, r).wait()
        word = jnp.concatenate([xbuf[slot, pl.ds(s, MOE_TM, stride=nt), :] for s in range(nt)], axis=1)
        lo = lax.bitcast_convert_type(word << 16, F32)
        hi = lax.bitcast_convert_type(word & jnp.uint32(0xFFFF0000), F32)
        t = jnp.concatenate([lo, hi], axis=1).astype(BF16)
        hid = _silu(_dot(t, wg_ref[...].astype(BF16))) * _dot(t, wu_ref[...].astype(BF16))
        y_ref[...] = _dot(hid.astype(BF16), wd_ref[...].astype(BF16))

    @pl.when((i == 0) & (n_used > 0))
    def _():
        gather(0, 0)

    @pl.when(i + 1 < n_used)
    def _():
        gather(i + 1, 1 - slot)
        expert()

    @pl.when(i + 1 == n_used)
    def _():
        expert()

    @pl.when(i >= n_used)
    def _():
        y_ref[...] = jnp.zeros_like(y_ref)


def _experts(v, src, tile_e, n_used, layer, w_gate, w_up, w_down):
    d = w_gate.shape[2]
    hdim = w_gate.shape[3]
    n_rows = src.shape[0]
    tm = MOE_TM
    by_expert = lambda i, src_r, te_r, nu_r: (layer, te_r[i], 0, 0)
    return pl.pallas_call(
        _experts_kernel,
        out_shape=jax.ShapeDtypeStruct((n_rows, d), F32),
        grid_spec=pltpu.PrefetchScalarGridSpec(
            num_scalar_prefetch=3,
            grid=(n_rows // tm,),
            in_specs=[
                pl.BlockSpec(memory_space=pl.ANY),
                pl.BlockSpec((None, None, d, hdim), by_expert),
                pl.BlockSpec((None, None, d, hdim), by_expert),
                pl.BlockSpec((None, None, hdim, d), by_expert),
            ],
            out_specs=pl.BlockSpec((tm, d), lambda i, *_: (i, 0)),
            scratch_shapes=[pltpu.VMEM((2, tm * MOE_ROW_TILES, LANES), jnp.uint32), pltpu.SemaphoreType.DMA((2,))],
        ),
        compiler_params=_cp("arbitrary"),
        name="moe_experts",
    )(src, tile_e, n_used, v, w_gate, w_up, w_down)


def _combine_kernel(d1_ref, d2_ref, y_hbm, meta_ref, x_ref, gate_ref, o_ref, buf1, buf2, sem):
    i = pl.program_id(0)
    bm = x_ref.shape[0]
    slot = i % 2

    def gather(tile, into):
        for r in range(bm):
            t = tile * bm + r
            _row_copy(y_hbm, d1_ref[t], buf1, into, r, sem.at[0, into]).start(priority=0)
            _row_copy(y_hbm, d2_ref[t], buf2, into, r, sem.at[1, into]).start(priority=1)

    @pl.when(i == 0)
    def _():
        gather(0, 0)

    @pl.when(i + 1 < pl.num_programs(0))
    def _():
        gather(i + 1, 1 - slot)

    for r in range(bm):
        _row_copy(y_hbm, 0, buf1, slot, r, sem.at[0, slot]).wait()
        _row_copy(y_hbm, 0, buf2, slot, r, sem.at[1, slot]).wait()
    meta = meta_ref[...]
    lane = _iota(meta.shape, 1)
    w1 = jnp.sum(jnp.where(lane == META_W1, meta, 0.0), axis=-1, keepdims=True)
    w2 = jnp.sum(jnp.where(lane == META_W2, meta, 0.0), axis=-1, keepdims=True)
    o_ref[...] = x_ref[...] + gate_ref[...] * (w1 * buf1[slot] + w2 * buf2[slot])


def _combine(y, d1, d2, meta, x, mod, layer, bm, nc, seq):
    n, d = x.shape
    at_tile = lambda i, *_: (i, 0)
    mod_index = _mod_spec(layer, 5, bm, nc, seq).index_map
    return pl.pallas_call(
        _combine_kernel,
        out_shape=jax.ShapeDtypeStruct((n, d), F32),
        grid_spec=pltpu.PrefetchScalarGridSpec(
            num_scalar_prefetch=2,
            grid=(n // bm,),
            in_specs=[
                pl.BlockSpec(memory_space=pl.ANY),
                pl.BlockSpec((bm, LANES), at_tile),
                pl.BlockSpec((bm, d), at_tile),
                pl.BlockSpec((None, None, None, 1, d), lambda i, *_: mod_index(i)),
            ],
            out_specs=pl.BlockSpec((bm, d), at_tile),
            scratch_shapes=[pltpu.VMEM((2, bm, d), F32), pltpu.VMEM((2, bm, d), F32),
                            pltpu.SemaphoreType.DMA((2, 2))],
        ),
        compiler_params=_cp("arbitrary"),
        name="moe_combine",
    )(d1, d2, y, meta, x, mod)


def _final_norm_kernel(x_ref, g_ref, o_ref):
    x = x_ref[...]
    o_ref[...] = x * lax.rsqrt(jnp.mean(x * x, axis=-1, keepdims=True) + NORM_EPS) * g_ref[...]


def _final_norm(x, g, bm, row0):
    n, d = x.shape
    n -= row0 * bm
    return pl.pallas_call(
        _final_norm_kernel,
        out_shape=jax.ShapeDtypeStruct((n, d), F32),
        grid=(n // bm,),
        in_specs=[pl.BlockSpec((bm, d), lambda i: (i + row0, 0)), pl.BlockSpec((1, d), lambda i: (0, 0))],
        out_specs=pl.BlockSpec((bm, d), lambda i: (i, 0)),
        compiler_params=_cp("parallel"),
        name="final_norm",
    )(x, g)


_COL = dict(s5=0, rw=S5_WIDTH, rg=S5_WIDTH + 3 * RW_WIDTH + 128, q=4352, k=4864, v=5376, gk=6400, og=6416, gates=7440)


def _pad_rows(w, rows):
    return jnp.pad(w, ((0, rows - w.shape[0]), (0, 0)))


def _layer(i, last, xa, mod, bsz, seq, ctx_len, p):
    nc = bsz * ctx_len
    rows = xa.shape[0]
    bm = 512
    msel = dict(nc=nc, seq=seq)
    u = _normmod(xa, p['g_norm1'].reshape(-1, 1, D_MODEL), mod, i, (0, 1), bm, **msel)

    w_in = p['w_in'][i]
    col = lambda a, w: w_in[:, a:a + w].astype(BF16)
    bm_in = 1088 if rows % 1088 == 0 else bm
    w_a = jnp.concatenate([col(0, S5_WIDTH), col(_COL['og'], GLA_V), col(_COL['gates'], 3 * D_MODEL)], axis=1)
    w_z = jnp.concatenate([col(_COL['rw'], 3 * RW_WIDTH + 128), col(_COL['rg'], 128)], axis=1)
    w_qk = col(_COL['q'], 2 * GLA_QK)
    w_v = col(_COL['v'], GLA_V)
    w_gk = jnp.pad(col(_COL['gk'], 16), ((0, 0), (0, LANES - 16)))
    pa_in = _mm(u, w_a, bm_in, 1024)
    zall = _mm(u, w_z, bm_in, 1664)
    qk = _mm(u, w_qk, bm_in, 1024)
    vv = _mm(u, w_v, bm_in, 1024)
    gk = _mm(u, w_gk, bm_in, LANES)

    ya = []
    for d in range(2):
        tables = _s5_tables(*(p[k][i, d] for k in ('s5_a_re', 's5_a_im', 's5_log_dt', 's5_b_re', 's5_b_im',
                                                   's5_c_re', 's5_c_im')))
        ya.append(_s5_scan(pa_in, tables, bsz, ctx_len // S5_BLOCK, seq // S5_BLOCK, bool(d)))

    yb = []
    for d in range(2):
        prm = (
            jnp.pad(p['rw_mu'][i, d], (0, RW_Z - p['rw_mu'].shape[-1])).reshape(1, RW_Z),
            p['rw_w0'][i, d].reshape(1, -1),
            _pad_rows(p['rw_w_up'][i, d], LANES).astype(BF16),
            p['rw_a0'][i, d].reshape(1, -1),
            jnp.pad(p['rw_a_up'][i, d], ((RW_HEAD, 0), (0, 0))).astype(BF16),
            p['rw_k_k'][i].reshape(1, -1), p['rw_k_a'][i].reshape(1, -1), p['rw_r_k'][i].reshape(1, -1),
        )
        yb.append(_rwkv_direction(zall, prm, bsz, ctx_len // RW_BLOCK, seq // RW_BLOCK, bool(d)))

    gk_up = jnp.pad(p['gl_gk_up'][i], ((0, 0), (0, LANES - 16), (0, 0))).astype(BF16)
    gk_b = p['gl_gk_b'][i].reshape(2, 1, GLA_QK)
    s0 = jnp.zeros((2, bsz, GLA_HEADS, GLA_DV, GLA_DK), F32)
    oc, s_ctx = _gla(_to_chunk_major(qk[:nc], bsz), _to_chunk_major(vv[:nc], bsz), _to_chunk_major(gk[:nc], bsz),
                     gk_up, gk_b, s0)
    rows_l = seq // GRID_W
    lat = lambda t: t[nc:].reshape(bsz, rows_l, GRID_W * t.shape[1])
    ol, _ = _gla(lat(qk), lat(vv), lat(gk), gk_up, gk_b, s_ctx)
    o_dir = [jnp.concatenate([_from_chunk_major(oc[d], GLA_V).reshape(nc, GLA_V),
                              ol[d].reshape(bsz * seq, GLA_V)], axis=0) for d in range(2)]

    row0 = (nc // bm) if last else 0
    branch_prm = (p['s5_d'][i].reshape(1, -1), p['rw_g_up'][i].astype(BF16), p['rw_ln_w'][i].reshape(1, -1),
                  p['rw_ln_b'][i].reshape(1, -1), p['gl_norm_g'][i].reshape(1, -1), p['s5_w_glu'][i].astype(BF16),
                  p['rw_w_proj'][i].astype(BF16), p['gl_w_proj'][i].astype(BF16))
    merged = _branches(ya, yb, o_dir, pa_in, zall, branch_prm, row0 * (bm // RW_BLOCK), ctx_len // RW_BLOCK,
                       seq // RW_BLOCK)
    xm = _out_proj(merged, p['w_out'][i].astype(BF16), xa, mod, i, bm, row0, nc, seq)
    if last:
        msel = dict(nc=0, seq=seq)

    wr = jnp.pad(jnp.concatenate([p['moe_wg1'][i], p['moe_wg2'][i]], axis=1), ((0, 0), (0, LANES - 36)))
    wr_hi = wr.astype(BF16)
    wr_lo = (wr - wr_hi.astype(F32)).astype(BF16)
    br = jnp.pad(jnp.concatenate([p['moe_bg1'][i], p['moe_bg2'][i]]), (0, LANES - 36)).reshape(1, LANES)
    vmoe, meta, meta_t, cnt = _router(xm, p['g_norm2'].reshape(-1, 1, D_MODEL), mod, i, wr_hi, wr_lo, br, bm, **msel)
    d1, d2, src, tile_e, n_used = _moe_plan(meta_t, cnt)
    y_sorted = _experts(vmoe, src, tile_e, n_used, i, p['moe_w_gate'], p['moe_w_up'], p['moe_w_down'])
    return _combine(y_sorted, d1, d2, meta, xm, mod, i, bm // 2, **msel)


def kernel(x, c, ctx, c_ctx, w_mod, b_mod, g_norm1, g_norm2, w_in, s5_a_re, s5_a_im, s5_log_dt, s5_b_re, s5_b_im,
           s5_c_re, s5_c_im, s5_d, s5_w_glu, rw_mu, rw_w0, rw_w_up, rw_a0, rw_a_up, rw_k_k, rw_k_a, rw_r_k, rw_g_up,
           rw_ln_w, rw_ln_b, rw_w_proj, gl_gk_up, gl_gk_b, gl_norm_g, gl_w_proj, w_out, moe_wg1, moe_bg1, moe_wg2,
           moe_bg2, moe_w_gate, moe_w_up, moe_w_down, g_final):
    p = dict(g_norm1=g_norm1, g_norm2=g_norm2, w_in=w_in, s5_a_re=s5_a_re, s5_a_im=s5_a_im, s5_log_dt=s5_log_dt,
             s5_b_re=s5_b_re, s5_b_im=s5_b_im, s5_c_re=s5_c_re, s5_c_im=s5_c_im, s5_d=s5_d, s5_w_glu=s5_w_glu,
             rw_mu=rw_mu, rw_w0=rw_w0, rw_w_up=rw_w_up, rw_a0=rw_a0, rw_a_up=rw_a_up, rw_k_k=rw_k_k, rw_k_a=rw_k_a,
             rw_r_k=rw_r_k.reshape(rw_r_k.shape[0], -1), rw_g_up=rw_g_up, rw_ln_w=rw_ln_w, rw_ln_b=rw_ln_b,
             rw_w_proj=rw_w_proj, gl_gk_up=gl_gk_up, gl_gk_b=gl_gk_b, gl_norm_g=gl_norm_g, gl_w_proj=gl_w_proj,
             w_out=w_out, moe_wg1=moe_wg1, moe_bg1=moe_bg1, moe_wg2=moe_wg2, moe_bg2=moe_bg2, moe_w_gate=moe_w_gate,
             moe_w_up=moe_w_up, moe_w_down=moe_w_down)
    bsz, seq, d = x.shape
    ctx_len = ctx.shape[1]
    depth = w_mod.shape[0]
    cc = jnp.concatenate([c, c_ctx[None], jnp.zeros((8 - bsz - 1, d), F32)], axis=0)
    mod = _adaln(cc, w_mod, b_mod).reshape(depth, 8, 6, 1, d)
    xa = jnp.concatenate([ctx.reshape(bsz * ctx_len, d), x.reshape(bsz * seq, d)], axis=0)
    for i in range(depth):
        xa = _layer(i, i == depth - 1, xa, mod, bsz, seq, ctx_len, p)
    out = _final_norm(xa, g_final.reshape(1, d), 512, 0)
    return out.reshape(bsz, seq, d)
```

```python
import functools
import math

import jax
import jax.numpy as jnp
from jax import lax
from jax.experimental import pallas as pl
from jax.experimental.pallas import tpu as pltpu

F32 = jnp.float32
BF16 = jnp.bfloat16

D_MODEL = 2048
GRID_W = 64
NORM_EPS = 1e-6

S5_WIDTH = 1024
S5_GROUP = 16
S5_GROUPS = 64
S5_STATE = 64
S5_MAX_RE = -1e-4
S5_TILE = 16
S5_PAIRS = S5_GROUPS // 2

RW_WIDTH = 1024
RW_HEAD = 64
RW_DECAY_SCALE = 0.606531
RW_GN_EPS = 64e-5
RW_BLOCK = 256
RW_CHUNK = 16
RW_PAIRS = RW_WIDTH // 128
RW_Z = 3 * RW_WIDTH + 128

GLA_HEADS = 4
GLA_DK = 128
GLA_DV = 256
GLA_QK = 512
GLA_V = 1024
GLA_TAU = 16.0
GLA_CHUNK = 64

MOE_GROUPS = 4
MOE_PER_GROUP = 8
MOE_EXPERTS = 32
MOE_HIDDEN = 256
MOE_LANE0 = MOE_GROUPS

LANES = 128
VMEM_LIMIT = 56 * 1024 * 1024


def _cp(*sem):
    return pltpu.CompilerParams(dimension_semantics=sem, vmem_limit_bytes=VMEM_LIMIT)


def _dot(a, b):
    return jnp.dot(a, b, preferred_element_type=F32)


def _dot_nt(a, b):
    return lax.dot_general(a, b, (((1,), (1,)), ((), ())), preferred_element_type=F32)


def _dot_tn(a, b):
    return lax.dot_general(a, b, (((0,), (0,)), ((), ())), preferred_element_type=F32)


def _sigmoid(x):
    return 1.0 / (1.0 + jnp.exp(-x))


def _silu(x):
    return x * _sigmoid(x)


def _gelu_tanh(x):
    return 0.5 * x * (1.0 + jnp.tanh(math.sqrt(2.0 / math.pi) * (x + 0.044715 * (x * x * x))))


def _split_hi_lo(x):
    hi = x.astype(BF16)
    lo = (x - hi.astype(F32)).astype(BF16)
    return hi, lo


def _pack_halves(x):
    bits = lax.bitcast_convert_type(x.astype(BF16).astype(F32), jnp.uint32)
    n = x.shape[1] // 2
    return (bits[:, n:] & jnp.uint32(0xFFFF0000)) | (bits[:, :n] >> 16)


def _unpack_halves(word):
    return (lax.bitcast_convert_type(word << 16, F32),
            lax.bitcast_convert_type(word & jnp.uint32(0xFFFF0000), F32))


def _iota(shape, dim):
    return lax.broadcasted_iota(jnp.int32, shape, dim)


def _head_ones(width):
    return (_iota((LANES, LANES), 0) // width == _iota((LANES, LANES), 1) // width).astype(BF16)


def _head_sum(x, ones, exact):
    if exact:
        hi, lo = _split_hi_lo(x)
        return _dot(hi, ones) + _dot(lo, ones)
    return _dot(x.astype(BF16), ones)


def _adaln_kernel(c_ref, w_ref, b_ref, o_ref):
    c = c_ref[...]
    o_ref[...] = _dot(_silu(c).astype(BF16), w_ref[...].astype(BF16)) + b_ref[...]


def _adaln(cc, w_mod, b_mod):
    depth, d, n = w_mod.shape
    bn = 1536
    return pl.pallas_call(
        _adaln_kernel,
        out_shape=jax.ShapeDtypeStruct((depth, 8, n), F32),
        grid=(depth, n // bn),
        in_specs=[
            pl.BlockSpec((8, d), lambda l, j: (0, 0)),
            pl.BlockSpec((None, d, bn), lambda l, j: (l, 0, j)),
            pl.BlockSpec((None, 1, bn), lambda l, j: (l, 0, j)),
        ],
        out_specs=pl.BlockSpec((None, 8, bn), lambda l, j: (l, 0, j)),
        compiler_params=_cp("parallel", "parallel"),
        name="adaln",
    )(cc, w_mod, b_mod.reshape(depth, 1, n))


def _mod_spec(layer, part, bm, nc, seq):
    def index(i, *_):
        r0 = i * bm
        return (layer, jnp.where(r0 < nc, 2, (r0 - nc) // seq), part, 0, 0)
    return pl.BlockSpec((None, None, None, 1, D_MODEL), index)


def _normmod_kernel(x_ref, g_ref, sh_ref, sc_ref, o_ref):
    x = x_ref[...]
    y = x * lax.rsqrt(jnp.mean(x * x, axis=-1, keepdims=True) + NORM_EPS) * g_ref[...]
    o_ref[...] = (y * (1.0 + sc_ref[...]) + sh_ref[...]).astype(o_ref.dtype)


def _normmod(x, g, mod, layer, parts, bm, nc, seq):
    n, d = x.shape
    return pl.pallas_call(
        _normmod_kernel,
        out_shape=jax.ShapeDtypeStruct((n, d), BF16),
        grid=(n // bm,),
        in_specs=[
            pl.BlockSpec((bm, d), lambda i: (i, 0)),
            pl.BlockSpec((None, 1, d), lambda i: (layer, 0, 0)),
            _mod_spec(layer, parts[0], bm, nc, seq),
            _mod_spec(layer, parts[1], bm, nc, seq),
        ],
        out_specs=pl.BlockSpec((bm, d), lambda i: (i, 0)),
        compiler_params=_cp("parallel"),
        name="normmod",
    )(x, g, mod, mod)


def _mm_kernel(x_ref, w_ref, o_ref):
    o_ref[...] = _dot(x_ref[...], w_ref[...]).astype(o_ref.dtype)


def _mm(x, w, bm, bn, out_dtype=F32):
    m, k = x.shape
    n = w.shape[1]
    return pl.pallas_call(
        _mm_kernel,
        out_shape=jax.ShapeDtypeStruct((m, n), out_dtype),
        grid=(m // bm, n // bn),
        in_specs=[pl.BlockSpec((bm, k), lambda i, j: (i, 0)), pl.BlockSpec((k, bn), lambda i, j: (0, j))],
        out_specs=pl.BlockSpec((bm, bn), lambda i, j: (i, j)),
        compiler_params=_cp("parallel", "parallel"),
        name="mm",
    )(x, w)


def _seq_blocks(bsz, nc_blocks, nl_blocks, reverse):
    def local(s):
        lat = nc_blocks + ((nl_blocks - 1 - (s - nc_blocks)) if reverse else (s - nc_blocks))
        ctx = (nc_blocks - 1 - s) if reverse else s
        return jnp.where(s < nc_blocks, ctx, lat)

    def block(b, s):
        i = local(s)
        return jnp.where(i < nc_blocks, b * nc_blocks + i, bsz * nc_blocks + b * nl_blocks + i - nc_blocks)

    return local, block


S5_PACK = 8
S5_PACKS = S5_GROUPS // S5_PACK
S5_PLANE = S5_PACK * S5_STATE
S5_BLOCK = 256
S5_ROWS = 24


def _s5_tables(a_re, a_im, log_dt, b_re, b_im, c_re, c_im):
    lam = lax.complex(jnp.minimum(a_re, S5_MAX_RE), a_im)
    ldt = lam * jnp.exp(log_dt)[:, None]
    lam_bar = jnp.exp(ldt)
    b_bar = ((lam_bar - 1.0) / lam)[..., None] * lax.complex(b_re, b_im)
    c_mat = lax.complex(c_re, c_im)
    eye = jnp.eye(S5_PACK, dtype=F32)

    def block_diag(t):
        k, g, a, b = t.shape
        return (t[:, :, :, None, :] * eye[None, :, None, :, None]).reshape(k, g * a, g * b)

    b_t = jnp.transpose(b_bar, (0, 2, 1)).reshape(S5_PACKS, S5_PACK, S5_GROUP, S5_STATE)
    bblk = jnp.concatenate([block_diag(b_t.real), block_diag(b_t.imag)], axis=2)
    c_t = jnp.transpose(c_mat, (0, 2, 1)).reshape(S5_PACKS, S5_PACK, S5_STATE, S5_GROUP)
    cblk = jnp.concatenate([block_diag(c_t.real), block_diag(-c_t.imag)], axis=1)
    expo = jnp.concatenate([jnp.arange(1, S5_TILE + 1, dtype=F32), jnp.asarray([32.0, 64.0, 128.0], F32),
                            jnp.zeros((S5_ROWS - S5_TILE - 3,), F32)])
    pw = jnp.exp(ldt[None] * expo[:, None, None]).reshape(S5_ROWS, S5_PACKS, S5_PLANE)
    pw = jnp.transpose(pw, (1, 0, 2))
    return bblk.astype(BF16), cblk.astype(BF16), pw.real, pw.imag


def _s5_scan_kernel(u_ref, bblk_ref, cblk_ref, pre_ref, pim_ref, y_ref, car_ref, up_s, h2_s, hb2_s, yp_s, *, reverse):
    t = S5_TILE
    n_t = S5_BLOCK // t
    pn = S5_PLANE

    @pl.when(pl.program_id(1) == 0)
    def _():
        car_ref[...] = jnp.zeros_like(car_ref)

    ra = _iota((S5_BLOCK, S5_BLOCK), 0)
    cb = _iota((S5_BLOCK, S5_BLOCK), 1)
    perm = ((ra // t == cb % t) & (ra % t == cb // t)).astype(BF16)
    up = _dot(perm, u_ref[...].astype(BF16)).astype(BF16)
    for pk in range(S5_PACKS):
        up_s[pk] = up[:, LANES * pk:LANES * (pk + 1)]
    order = list(range(t - 1, -1, -1)) if reverse else list(range(t))
    rowj = _iota((n_t, 1), 0)

    def states(pk, bu, h_s, hb_s):
        pre = pre_ref[pk]
        pim = pim_ref[pk]
        l_re, l_im = pre[0:1], pim[0:1]
        h_re = h_im = None
        for n, s in enumerate(order):
            rows = slice(t * s, t * (s + 1))
            b_re, b_im = bu[rows, :pn], bu[rows, pn:]
            if n == 0:
                h_re, h_im = b_re, b_im
            else:
                h_re, h_im = l_re * h_re - l_im * h_im + b_re, l_re * h_im + l_im * h_re + b_im
            h_s[rows, :pn] = h_re
            h_s[rows, pn:] = h_im
        c_re, c_im = car_ref[pk, 0:1, :pn], car_ref[pk, 0:1, pn:]
        first = rowj == (n_t - 1 if reverse else 0)
        g_re, g_im = pre[t - 1:t], pim[t - 1:t]
        e_re = h_re + jnp.where(first, g_re * c_re - g_im * c_im, 0.0)
        e_im = h_im + jnp.where(first, g_re * c_im + g_im * c_re, 0.0)
        step = 1
        for row in (t - 1, t, t + 1, t + 2):
            if reverse:
                s_re, s_im, ok = pltpu.roll(e_re, n_t - step, 0), pltpu.roll(e_im, n_t - step, 0), rowj < n_t - step
            else:
                s_re, s_im, ok = pltpu.roll(e_re, step, 0), pltpu.roll(e_im, step, 0), rowj >= step
            a_re, a_im = pre[row:row + 1], pim[row:row + 1]
            e_re = e_re + jnp.where(ok, a_re * s_re - a_im * s_im, 0.0)
            e_im = e_im + jnp.where(ok, a_re * s_im + a_im * s_re, 0.0)
            step *= 2
        last = 0 if reverse else n_t - 1
        car_ref[pk, 0:1, :pn] = e_re[last:last + 1]
        car_ref[pk, 0:1, pn:] = e_im[last:last + 1]
        if reverse:
            in_re = jnp.where(first, c_re, pltpu.roll(e_re, n_t - 1, 0))
            in_im = jnp.where(first, c_im, pltpu.roll(e_im, n_t - 1, 0))
        else:
            in_re = jnp.where(first, c_re, pltpu.roll(e_re, 1, 0))
            in_im = jnp.where(first, c_im, pltpu.roll(e_im, 1, 0))
        for n, s in enumerate(order):
            rows = slice(t * s, t * (s + 1))
            a_re, a_im = pre[n:n + 1], pim[n:n + 1]
            hb_s[rows, :pn] = (h_s[rows, :pn] + a_re * in_re - a_im * in_im).astype(BF16)
            hb_s[rows, pn:] = (h_s[rows, pn:] + a_re * in_im + a_im * in_re).astype(BF16)

    def pack_pair(j, carry):
        pks = (2 * j, 2 * j + 1)
        bu = [_dot(up_s[pk], bblk_ref[pk]) for pk in pks]
        for u, pk in enumerate(pks):
            states(pk, bu[u], h2_s.at[u], hb2_s.at[u])
            yp_s[pk] = _dot(hb2_s[u], cblk_ref[pk]).astype(BF16)
        return carry

    lax.fori_loop(0, S5_PACKS // 2, pack_pair, 0)
    for pk in range(S5_PACKS):
        y_ref[:, LANES * pk:LANES * (pk + 1)] = _dot(perm, yp_s[pk])


def _s5_scan(u_src, tables, bsz, nc_blocks, nl_blocks, reverse):
    bblk, cblk, pre, pim = tables
    rows = u_src.shape[0]
    blk = S5_BLOCK
    _, block = _seq_blocks(bsz, nc_blocks, nl_blocks, reverse)
    whole = lambda a: pl.BlockSpec(a.shape, lambda b, s: (0,) * a.ndim)
    return pl.pallas_call(
        functools.partial(_s5_scan_kernel, reverse=reverse),
        out_shape=jax.ShapeDtypeStruct((rows, S5_WIDTH), F32),
        grid=(bsz, nc_blocks + nl_blocks),
        in_specs=[pl.BlockSpec((blk, S5_WIDTH), lambda b, s: (block(b, s), 0)),
                  whole(bblk), whole(cblk), whole(pre), whole(pim)],
        out_specs=pl.BlockSpec((blk, S5_WIDTH), lambda b, s: (block(b, s), 0)),
        scratch_shapes=[
            pltpu.VMEM((S5_PACKS, 8, 2 * S5_PLANE), F32),
            pltpu.VMEM((S5_PACKS, blk, LANES), BF16),
            pltpu.VMEM((2, blk, 2 * S5_PLANE), F32),
            pltpu.VMEM((2, blk, 2 * S5_PLANE), BF16),
            pltpu.VMEM((S5_PACKS, blk, LANES), BF16),
        ],
        compiler_params=_cp("parallel", "arbitrary"),
        name="s5_bwd" if reverse else "s5_fwd",
    )(u_src, bblk, cblk, pre, pim)


def _gla_kernel(q_ref, k_ref, v_ref, gk_ref, up_ref, gb_ref, s0_ref, o_ref, sf_ref, st_ref):
    d = pl.program_id(0)
    c = pl.program_id(1)
    n = pl.num_programs(1)

    @pl.when(c == 0)
    def _():
        st_ref[...] = s0_ref[...]

    cs = GLA_CHUNK
    nb = q_ref.shape[0]
    sign = 1 - 2 * d
    row = _iota((cs, cs), 0)
    col = _iota((cs, cs), 1)
    causal = (row - col) * sign >= 0
    causal_b = causal.astype(BF16)
    rid = _iota((cs, 1), 0)
    bcum, b_mid, b_end = [], [], []
    for b in range(nb):
        x = _dot(gk_ref[b].astype(BF16), up_ref[...]) + gb_ref[...]
        log_a = (jnp.minimum(x, 0.0) - jnp.log(1.0 + jnp.exp(-jnp.abs(x)))) * (1.0 / GLA_TAU)
        la_hi, la_lo = _split_hi_lo(log_a)
        bc = _dot(causal_b, la_hi) + _dot(causal_b, la_lo)
        bcum.append(bc)
        b_mid.append(jnp.sum(jnp.where(rid == cs // 2 - d, bc, 0.0), axis=0, keepdims=True))
        b_end.append(jnp.sum(jnp.where(rid == (cs - 1) * (1 - d), bc, 0.0), axis=0, keepdims=True))
    units = [(b, h) for b in range(nb) for h in range(GLA_HEADS)]
    ks = lambda h: slice(GLA_DK * h, GLA_DK * (h + 1))
    vs = lambda h: slice(GLA_DV * h, GLA_DV * (h + 1))
    s_old = [st_ref[b, h] for b, h in units]
    qh = [q_ref[b, :, ks(h)] * (GLA_DK ** -0.5) for b, h in units]
    kh = [k_ref[b, :, ks(h)] for b, h in units]
    vh = [v_ref[b, :, vs(h)].astype(BF16) for b, h in units]
    bh = [bcum[b][:, ks(h)] for b, h in units]
    mid = [b_mid[b][:, ks(h)] for b, h in units]
    end = [b_end[b][:, ks(h)] for b, h in units]
    scores = [_dot_nt((q * jnp.exp(x - m)).astype(BF16), (k * jnp.exp(m - x)).astype(BF16))
              for q, k, x, m in zip(qh, kh, bh, mid)]
    inter = [_dot_nt((q * jnp.exp(x)).astype(BF16), s.astype(BF16)) for q, x, s in zip(qh, bh, s_old)]
    upd = [_dot_tn(v, (k * jnp.exp(e - x)).astype(BF16)) for v, k, e, x in zip(vh, kh, end, bh)]
    intra = [_dot(jnp.where(causal, sc, 0.0).astype(BF16), v) for sc, v in zip(scores, vh)]
    for u, (b, h) in enumerate(units):
        o_ref[b, :, vs(h)] = intra[u] + inter[u]
        st_ref[b, h] = s_old[u] * jnp.exp(end[u]) + upd[u]

    @pl.when(c == n - 1)
    def _():
        sf_ref[...] = st_ref[...]


def _gla(qk, v, gk, gk_up, gk_b, s0):
    bsz, cs, w = v.shape
    n = w // GLA_V

    def chunk(d, c):
        return c + d * (n - 1 - 2 * c)

    state_spec = pl.BlockSpec((None, bsz, GLA_HEADS, GLA_DV, GLA_DK), lambda d, c: (d, 0, 0, 0, 0))
    return pl.pallas_call(
        _gla_kernel,
        out_shape=(jax.ShapeDtypeStruct((2, bsz, cs, w), F32), jax.ShapeDtypeStruct(s0.shape, F32)),
        grid=(2, n),
        in_specs=[
            pl.BlockSpec((bsz, cs, GLA_QK), lambda d, c: (0, 0, 2 * chunk(d, c))),
            pl.BlockSpec((bsz, cs, GLA_QK), lambda d, c: (0, 0, 2 * chunk(d, c) + 1)),
            pl.BlockSpec((bsz, cs, GLA_V), lambda d, c: (0, 0, chunk(d, c))),
            pl.BlockSpec((bsz, cs, LANES), lambda d, c: (0, 0, chunk(d, c))),
            pl.BlockSpec((None, LANES, GLA_QK), lambda d, c: (d, 0, 0)),
            pl.BlockSpec((None, 1, GLA_QK), lambda d, c: (d, 0, 0)),
            state_spec,
        ],
        out_specs=(pl.BlockSpec((None, bsz, cs, GLA_V), lambda d, c: (d, 0, 0, chunk(d, c))), state_spec),
        scratch_shapes=[pltpu.VMEM((bsz, GLA_HEADS, GLA_DV, GLA_DK), F32)],
        compiler_params=_cp("parallel", "arbitrary"),
        name="gla",
    )(qk, qk, v, gk, gk_up, gk_b, s0)


def _to_chunk_major(t, bsz):
    n = t.shape[0] // bsz
    d = t.shape[1]
    t = t.reshape(bsz, n // GLA_CHUNK, GLA_CHUNK, d)
    return jnp.transpose(t, (0, 2, 1, 3)).reshape(bsz, GLA_CHUNK, (n // GLA_CHUNK) * d)


def _from_chunk_major(t, d):
    lead = t.shape[:-2]
    n = t.shape[-1] // d
    t = t.reshape(lead + (GLA_CHUNK, n, d))
    return jnp.swapaxes(t, -3, -2).reshape(lead + (n * GLA_CHUNK, d))


def _rwkv_kernel(z0_ref, z1_ref, halo0_ref, halo1_ref, mu_ref, w0_ref, wup_ref, a0_ref, aup_ref, kk_ref, ka_ref,
                 rk_ref, y_ref, st_ref, kt_s, bt_s, kq_s, rt_s, v_s, w_s, u_s, ya_s, ab_s, pin_s,
                 *, reverse, nc_blocks):
    step = pl.program_id(0)
    blk = RW_BLOCK
    ch = RW_CHUNK
    n_ch = blk // ch
    n_b = 2
    n_units = n_b * RW_PAIRS

    @pl.when(step == 0)
    def _():
        st_ref[...] = jnp.zeros_like(st_ref)

    rowi = _iota((blk, 1), 0)
    low = _iota((1, LANES), 1) < RW_HEAD
    ones64 = _head_ones(RW_HEAD)
    pos = rowi % ch
    seq_start = (step == 0) | (step == nc_blocks)

    for b, (z_ref, halo_ref) in enumerate(((z0_ref, halo0_ref), (z1_ref, halo1_ref))):
        z = z_ref[...]
        if reverse:
            prev = pltpu.roll(z, blk - 1, 0)
            edge = halo_ref[0:1, :]
            at_edge = rowi == blk - 1
        else:
            prev = pltpu.roll(z, 1, 0)
            edge = halo_ref[7:8, :]
            at_edge = rowi == 0
        prev = jnp.where(at_edge, jnp.where(seq_start, 0.0, edge), prev)
        zs = z + (prev - z) * mu_ref[...]
        r = zs[:, 0:RW_WIDTH]
        k = zs[:, RW_WIDTH:2 * RW_WIDTH]
        v = zs[:, 2 * RW_WIDTH:3 * RW_WIDTH]
        lora = zs[:, 3 * RW_WIDTH:3 * RW_WIDTH + LANES]
        lora_w = jnp.where(low, jnp.tanh(lora), 0.0).astype(BF16)
        lora_a = jnp.where(low, 0.0, lora).astype(BF16)
        logw = -RW_DECAY_SCALE * _sigmoid(w0_ref[...] + _dot(lora_w, wup_ref[...]))
        a = _sigmoid(a0_ref[...] + _dot(lora_a, aup_ref[...]))
        kk = k * kk_ref[...]
        kp = k * (1.0 + (a - 1.0) * ka_ref[...])
        rkb = r * kp * rk_ref[...]

        cl = logw
        sh = 1
        while sh < ch:
            if reverse:
                cl = cl + jnp.where(pos < ch - sh, pltpu.roll(cl, blk - sh, 0), 0.0)
            else:
                cl = cl + jnp.where(pos >= sh, pltpu.roll(cl, sh, 0), 0.0)
            sh *= 2
        p_in = jnp.exp(cl)
        pin_s[b] = p_in
        p_ex = jnp.exp(cl - logw)
        p_inv = jnp.exp(-cl)

        for p in range(RW_PAIRS):
            ls = slice(LANES * p, LANES * (p + 1))
            q = b * RW_PAIRS + p
            kkp = kk[:, ls]
            ssq = _head_sum(kkp * kkp, ones64, False)
            kkn = kkp * (1.0 / jnp.maximum(jnp.sqrt(ssq), 1e-12))
            kt_s[q] = (kkn * p_ex[:, ls]).astype(BF16)
            bt_s[q] = (kkn * a[:, ls] * p_inv[:, ls]).astype(BF16)
            kq_s[q] = (kp[:, ls] * p_inv[:, ls]).astype(BF16)
            rt_s[q] = (r[:, ls] * p_in[:, ls]).astype(BF16)
            v_s[q] = v[:, ls].astype(BF16)
            ya_s[q] = _head_sum(rkb[:, ls], ones64, False) * v[:, ls]

    rr = _iota((blk, blk), 0)
    cc = _iota((blk, blk), 1)
    same = rr // ch == cc // ch
    before = (cc > rr) if reverse else (cc < rr)
    strict = same & before
    incl = same & (before | (rr == cc))
    fold0 = (_iota((blk, LANES), 0) % ch == _iota((blk, LANES), 1)).astype(BF16)
    fold1 = (_iota((blk, LANES), 0) % ch + ch == _iota((blk, LANES), 1)).astype(BF16)
    lane_lo = _iota((1, LANES), 1) < RW_HEAD

    def pair_body(p, carry):
        units = [b * RW_PAIRS + p for b in range(n_b)]
        heads = [(u, hh) for u in range(n_b) for hh in range(2)]
        kt = [kt_s[q] for q in units]
        rt = [rt_s[q] for q in units]
        vv = [v_s[q] for q in units]
        ya = [ya_s[q] for q in units]
        bt = [bt_s[q] for q in units]
        kq = [kq_s[q] for q in units]
        hb = blk // 2
        g = []
        for u, hh in heads:
            mine = lane_lo if hh == 0 else jnp.logical_not(lane_lo)
            zero = jnp.zeros_like(kt[u])
            ktm, rtm = jnp.where(mine, kt[u], zero), jnp.where(mine, rt[u], zero)
            halves = []
            for rows in (slice(0, hb), slice(hb, blk)):
                lhs = jnp.concatenate([ktm[rows], rtm[rows]], axis=0)
                rhs = jnp.concatenate([bt[u][rows], kq[u][rows]], axis=0)
                halves.append(_dot_nt(lhs, rhs))
            g.append(halves)
        zero_q = jnp.zeros((hb, hb), F32)

        def diag2(gh, r0, c0):
            a, b = gh[0][r0:r0 + hb, c0:c0 + hb], gh[1][r0:r0 + hb, c0:c0 + hb]
            return jnp.concatenate([jnp.concatenate([a, zero_q], axis=1), jnp.concatenate([zero_q, b], axis=1)],
                                   axis=0)

        n1 = [jnp.where(strict, -diag2(gh, 0, 0), 0.0).astype(BF16) for gh in g]
        a_kq = [jnp.where(strict, diag2(gh, 0, hb), 0.0).astype(BF16) for gh in g]
        a_rb = [jnp.where(incl, diag2(gh, hb, 0), 0.0).astype(BF16) for gh in g]
        a_rq = [jnp.where(incl, diag2(gh, hb, hb), 0.0).astype(BF16) for gh in g]
        n2 = [_dot(n, n).astype(BF16) for n in n1]
        akv = [_dot(a, vv[u]) for a, (u, _) in zip(a_kq, heads)]
        n4 = [_dot(n, n).astype(BF16) for n in n2]
        y_in = [_dot(a, vv[u]) for a, (u, _) in zip(a_rq, heads)]
        n8 = [_dot(n, n).astype(BF16) for n in n4]
        fold = [_dot(a, fold0 if hh == 0 else fold1) for a, (_, hh) in zip(a_rb, heads)]
        rhs_t = [jnp.concatenate([kt[u].astype(F32), av], axis=1) for av, (u, _) in zip(akv, heads)]
        for nk in (n8, n4, n2, n1):
            rhs_t = [x + _dot(n, x.astype(BF16)) for n, x in zip(nk, rhs_t)]
        for u, q in enumerate(units):
            h0, h1 = 2 * u, 2 * u + 1
            w_s[q] = jnp.where(lane_lo, rhs_t[h0][:, :LANES], rhs_t[h1][:, :LANES]).astype(BF16)
            u_s[q] = jnp.where(lane_lo, rhs_t[h0][:, LANES:], rhs_t[h1][:, LANES:])
            ya_s[q] = ya[u] + jnp.where(lane_lo, y_in[h0], y_in[h1])
            ab_s[q] = (fold[h0] + fold[h1]).astype(BF16)
        return carry

    lax.fori_loop(0, RW_PAIRS, pair_body, 0)

    blockdiag = (_iota((LANES, LANES), 0) // RW_HEAD) == (_iota((LANES, LANES), 1) // RW_HEAD)
    end_row = 0 if reverse else ch - 1

    def chunk_body(i, carry):
        c = (n_ch - 1 - i) if reverse else i
        rows = pl.ds(pl.multiple_of(c * ch, ch), ch)
        s_old = [st_ref[q] for q in range(n_units)]
        m1 = [_dot_nt(jnp.concatenate([w_s[q, rows, :], rt_s[q, rows, :]], axis=0), s_old[q].astype(BF16))
              for q in range(n_units)]
        zc = [-(m1[q][:ch] + u_s[q, rows, :]) for q in range(n_units)]
        upd = []
        for q in range(n_units):
            zv = jnp.concatenate([zc[q].astype(BF16), v_s[q, rows, :]], axis=0)
            bk = jnp.concatenate([bt_s[q, rows, :], kq_s[q, rows, :]], axis=0)
            upd.append(_dot_tn(zv, bk))
        yc = []
        for q in range(n_units):
            z2 = jnp.concatenate([jnp.where(lane_lo, zc[q], 0.0), jnp.where(lane_lo, 0.0, zc[q])], axis=0)
            yc.append(m1[q][ch:] + _dot(ab_s[q, rows, :][:, :2 * ch], z2.astype(BF16)) + ya_s[q, rows, :])
        for q in range(n_units):
            b, p = divmod(q, RW_PAIRS)
            p_end = pin_s[b, rows, LANES * p:LANES * (p + 1)][end_row:end_row + 1]
            st_ref[q] = (s_old[q] + jnp.where(blockdiag, upd[q], 0.0)) * p_end
            y_ref[b, rows, LANES * p:LANES * (p + 1)] = yc[q]
        return carry

    lax.fori_loop(0, n_ch, chunk_body, 0)


def _rwkv_direction(zall, prm, bsz, nc_blocks, nl_blocks, reverse):
    assert bsz == 2
    mu, w0, wup, a0, aup, k_k, k_a, r_k = prm
    rows = zall.shape[0]
    blk = RW_BLOCK
    steps = nc_blocks + nl_blocks
    n_blocks = rows // blk

    def local(s):
        lat = nc_blocks + ((nl_blocks - 1 - (s - nc_blocks)) if reverse else (s - nc_blocks))
        ctx = (nc_blocks - 1 - s) if reverse else s
        return jnp.where(s < nc_blocks, ctx, lat)

    def block(b, s):
        i = local(s)
        return jnp.where(i < nc_blocks, b * nc_blocks + i, bsz * nc_blocks + b * nl_blocks + i - nc_blocks)

    def halo(b, s):
        i = block(b, s)
        if reverse:
            return jnp.minimum((i + 1) * (blk // 8), n_blocks * (blk // 8) - 1)
        return jnp.maximum(i * (blk // 8) - 1, 0)

    vec = lambda w: pl.BlockSpec((1, w), lambda s: (0, 0))
    mat = lambda: pl.BlockSpec((LANES, RW_WIDTH), lambda s: (0, 0))
    n_units = bsz * RW_PAIRS
    unit_bf = pltpu.VMEM((n_units, blk, LANES), BF16)
    unit_f = pltpu.VMEM((n_units, blk, LANES), F32)
    z_spec = lambda b: pl.BlockSpec((blk, RW_Z), lambda s: (block(b, s), 0))
    halo_spec = lambda b: pl.BlockSpec((8, RW_Z), lambda s: (halo(b, s), 0))
    return pl.pallas_call(
        functools.partial(_rwkv_kernel, reverse=reverse, nc_blocks=nc_blocks),
        out_shape=jax.ShapeDtypeStruct((bsz, steps * blk, RW_WIDTH), F32),
        grid=(steps,),
        in_specs=[
            z_spec(0), z_spec(1), halo_spec(0), halo_spec(1),
            vec(RW_Z), vec(RW_WIDTH), mat(), vec(RW_WIDTH), mat(), vec(RW_WIDTH), vec(RW_WIDTH), vec(RW_WIDTH),
        ],
        out_specs=pl.BlockSpec((bsz, blk, RW_WIDTH), lambda s: (0, local(s), 0)),
        scratch_shapes=[
            pltpu.VMEM((n_units, LANES, LANES), F32),
            unit_bf, unit_bf, unit_bf, unit_bf, unit_bf,
            unit_bf, unit_f, unit_f, unit_bf,
            pltpu.VMEM((bsz, blk, RW_WIDTH), F32),
        ],
        compiler_params=_cp("arbitrary"),
        name="rwkv_bwd" if reverse else "rwkv_fwd",
    )(zall, zall, zall, zall, mu, w0, wup, a0, aup, k_k, k_a, r_k)


def _branches_kernel(saf_ref, sab_ref, su_ref, sd_ref, rf_ref, rb_ref, rg_ref, gup_ref, lw_ref, lb_ref,
                     of_ref, ob_ref, og_ref, ng_ref, ga_ref, gb_ref, gc_ref, wglu_ref, wrw_ref, wgl_ref, m_ref):
    za = _gelu_tanh(saf_ref[...] + sab_ref[...] + sd_ref[...] * su_ref[...].astype(F32)).astype(BF16)
    hid = _dot(za, wglu_ref[...])
    m = _sigmoid(ga_ref[...].astype(F32)) * (hid[:, :D_MODEL] * _sigmoid(hid[:, D_MODEL:]))
    ones64 = _head_ones(RW_HEAD)
    gate = _dot(_sigmoid(rg_ref[...]).astype(BF16), gup_ref[...])
    zb = []
    for p in range(RW_PAIRS):
        ls = slice(LANES * p, LANES * (p + 1))
        y = rf_ref[:, ls] + rb_ref[:, ls]
        mean = _head_sum(y, ones64, True) * (1.0 / RW_HEAD)
        yc = y - mean
        var = _head_sum(yc * yc, ones64, True) * (1.0 / RW_HEAD)
        yn = yc * lax.rsqrt(var + RW_GN_EPS) * lw_ref[:, ls] + lb_ref[:, ls]
        zb.append((yn * gate[:, ls]).astype(BF16))
    m = m + _sigmoid(gb_ref[...].astype(F32)) * _dot(jnp.concatenate(zb, axis=1), wrw_ref[...])
    zc = []
    for h in range(GLA_HEADS):
        vs = slice(GLA_DV * h, GLA_DV * (h + 1))
        o = of_ref[:, vs] + ob_ref[:, vs]
        on = o * lax.rsqrt(jnp.mean(o * o, axis=-1, keepdims=True) + NORM_EPS) * ng_ref[:, vs]
        zc.append((on * _silu(og_ref[:, vs].astype(F32))).astype(BF16))
    m = m + _sigmoid(gc_ref[...].astype(F32)) * _dot(jnp.concatenate(zc, axis=1), wgl_ref[...])
    m_ref[...] = m.astype(BF16)


def _branches(ya, yb, o_dir, pa_in, zall, prm, row0, nc_blocks, nl_blocks):
    s5_d, g_up, ln_w, ln_b, norm_g, w_glu, w_rw, w_gl = prm
    bm = RW_BLOCK
    bsz = yb[0].shape[0]
    n = zall.shape[0] - row0 * bm
    rows = lambda w, blk=0: pl.BlockSpec((bm, w), lambda i: (i + row0, blk))

    def y_index(i):
        i = i + row0
        t = i - bsz * nc_blocks
        return (jnp.where(t < 0, i // nc_blocks, t // nl_blocks),
                jnp.where(t < 0, i % nc_blocks, nc_blocks + t % nl_blocks), 0)

    y_spec = pl.BlockSpec((None, bm, RW_WIDTH), y_index)
    vec = pl.BlockSpec((1, 1024), lambda i: (0, 0))
    const = lambda a: pl.BlockSpec(a.shape, lambda i: (0,) * a.ndim, pipeline_mode=pl.Buffered(1))
    return pl.pallas_call(
        _branches_kernel,
        out_shape=jax.ShapeDtypeStruct((n, D_MODEL), BF16),
        grid=(n // bm,),
        in_specs=[rows(1024), rows(1024), rows(1024, 0), vec,
                  y_spec, y_spec, rows(LANES, RW_Z // LANES), const(g_up), vec, vec,
                  rows(1024), rows(1024), rows(1024, 1), vec,
                  rows(D_MODEL, 1), rows(D_MODEL, 2), rows(D_MODEL, 3),
                  const(w_glu), const(w_rw), const(w_gl)],
        out_specs=pl.BlockSpec((bm, D_MODEL), lambda i: (i, 0)),
        compiler_params=_cp("parallel"),
        name="branches",
    )(ya[0], ya[1], pa_in, s5_d, yb[0], yb[1], zall, g_up, ln_w, ln_b,
      o_dir[0], o_dir[1], pa_in, norm_g, pa_in, pa_in, pa_in, w_glu, w_rw, w_gl)


def _out_proj_kernel(m_ref, w_ref, x_ref, gate_ref, o_ref):
    o_ref[...] = x_ref[...] + gate_ref[...] * _dot(m_ref[...], w_ref[...])


def _out_proj(m, w_out, x, mod, layer, bm, row0, nc, seq):
    n, d = m.shape
    bn = 1024
    mod_index = _mod_spec(layer, 2, bm, nc, seq).index_map
    return pl.pallas_call(
        _out_proj_kernel,
        out_shape=jax.ShapeDtypeStruct((n, d), F32),
        grid=(n // bm, d // bn),
        in_specs=[pl.BlockSpec((bm, d), lambda i, j: (i, 0)),
                  pl.BlockSpec((d, bn), lambda i, j: (0, j)),
                  pl.BlockSpec((bm, bn), lambda i, j: (i + row0, j)),
                  pl.BlockSpec((None, None, None, 1, bn), lambda i, j: mod_index(i + row0)[:4] + (j,))],
        out_specs=pl.BlockSpec((bm, bn), lambda i, j: (i, j)),
        compiler_params=_cp("parallel", "parallel"),
        name="out_proj",
    )(m, w_out, x, mod)


MOE_TM = 256
MOE_ROW_TILES = D_MODEL // (2 * LANES)
META_E1, META_E2, META_R1, META_R2, META_W1, META_W2 = range(6)


def _router_kernel(x_ref, g_ref, sh_ref, sc_ref, wr_hi_ref, wr_lo_ref, br_ref, v_ref, meta_ref, meta_t_ref, cnt_ref,
                   base_s):
    @pl.when(pl.program_id(0) == 0)
    def _():
        base_s[...] = jnp.zeros_like(base_s)

    x = x_ref[...]
    y = x * lax.rsqrt(jnp.mean(x * x, axis=-1, keepdims=True) + NORM_EPS) * g_ref[...]
    t = y * (1.0 + sc_ref[...]) + sh_ref[...]
    word = _pack_halves(t)
    for s in range(MOE_ROW_TILES):
        v_ref[pl.ds(s, x.shape[0], stride=MOE_ROW_TILES), :] = word[:, LANES * s:LANES * (s + 1)]
    t_hi, t_lo = _split_hi_lo(t)
    logits = (_dot(t_hi, wr_hi_ref[...]) + _dot(t_lo, wr_hi_ref[...]) + _dot(t_hi, wr_lo_ref[...])) + br_ref[...]
    lane = _iota(logits.shape, 1).astype(F32)
    neg = jnp.float32(-jnp.inf)
    big = jnp.float32(LANES)
    l1 = jnp.where(lane < MOE_GROUPS, logits, neg)
    m1 = jnp.max(l1, axis=-1, keepdims=True)
    p_top = 1.0 / jnp.sum(jnp.exp(l1 - m1), axis=-1, keepdims=True)
    grp = jnp.min(jnp.where(l1 == m1, lane, big), axis=-1, keepdims=True)
    lo = MOE_LANE0 + MOE_PER_GROUP * grp
    in_grp = (lane >= lo) & (lane < lo + MOE_PER_GROUP)
    l2 = jnp.where(in_grp, logits, neg)
    v1 = jnp.max(l2, axis=-1, keepdims=True)
    i1 = jnp.min(jnp.where(l2 == v1, lane, big), axis=-1, keepdims=True)
    l3 = jnp.where(lane == i1, neg, l2)
    v2 = jnp.max(l3, axis=-1, keepdims=True)
    i2 = jnp.min(jnp.where(l3 == v2, lane, big), axis=-1, keepdims=True)
    e2 = jnp.exp(v2 - v1)
    w1 = p_top / (1.0 + e2)
    w2 = p_top * e2 / (1.0 + e2)
    pick1 = lane == i1
    pick2 = lane == i2
    chosen = jnp.where(pick1 | pick2, 1.0, 0.0)
    bm = x.shape[0]
    earlier = (_iota((bm, bm), 1) < _iota((bm, bm), 0)).astype(BF16)
    before = _dot(earlier, chosen.astype(BF16)) + base_s[...]
    r1 = jnp.sum(jnp.where(pick1, before, 0.0), axis=-1, keepdims=True)
    r2 = jnp.sum(jnp.where(pick2, before, 0.0), axis=-1, keepdims=True)
    base_s[...] += jnp.sum(chosen, axis=0, keepdims=True)
    cnt_ref[...] = base_s[...]
    meta = jnp.zeros_like(logits)
    for slot, val in ((META_E1, i1 - MOE_LANE0), (META_E2, i2 - MOE_LANE0), (META_R1, r1), (META_R2, r2),
                      (META_W1, w1), (META_W2, w2)):
        meta = jnp.where(lane == slot, val, meta)
    meta_ref[...] = meta
    meta_t_ref[...] = meta.T[:8]


def _router(x, g, mod, layer, wr_hi, wr_lo, br, bm, nc, seq):
    n, d = x.shape
    return pl.pallas_call(
        _router_kernel,
        out_shape=(jax.ShapeDtypeStruct((n * MOE_ROW_TILES, LANES), jnp.uint32), jax.ShapeDtypeStruct((n, LANES), F32),
                   jax.ShapeDtypeStruct((8, n), F32), jax.ShapeDtypeStruct((1, LANES), F32)),
        grid=(n // bm,),
        in_specs=[
            pl.BlockSpec((bm, d), lambda i: (i, 0)),
            pl.BlockSpec((None, 1, d), lambda i: (layer, 0, 0)),
            _mod_spec(layer, 3, bm, nc, seq), _mod_spec(layer, 4, bm, nc, seq),
            pl.BlockSpec((d, LANES), lambda i: (0, 0)), pl.BlockSpec((d, LANES), lambda i: (0, 0)),
            pl.BlockSpec((1, LANES), lambda i: (0, 0)),
        ],
        out_specs=(pl.BlockSpec((bm * MOE_ROW_TILES, LANES), lambda i: (i, 0)),
                   pl.BlockSpec((bm, LANES), lambda i: (i, 0)),
                   pl.BlockSpec((8, bm), lambda i: (0, i)), pl.BlockSpec((1, LANES), lambda i: (0, 0))),
        scratch_shapes=[pltpu.VMEM((1, LANES), F32)],
        compiler_params=_cp("arbitrary"),
        name="moe_router",
    )(x, g, mod, mod, wr_hi, wr_lo, br)


def _moe_plan(meta_t, cnt):
    tm = MOE_TM
    n_tok = meta_t.shape[1]
    counts = cnt[0, MOE_LANE0:MOE_LANE0 + MOE_EXPERTS].astype(jnp.int32)
    seg = ((counts + tm - 1) // tm) * tm
    ends = jnp.cumsum(seg)
    off = ends - seg
    rec = meta_t[:4].astype(jnp.int32)
    first_row = jnp.sum(jnp.where(rec[:2, None, :] == jnp.arange(MOE_EXPERTS)[None, :, None], off[None, :, None], 0),
                        axis=1)
    dest = first_row + rec[2:4]
    n_rows = 2 * n_tok + MOE_EXPERTS * tm
    n_tiles = n_rows // tm
    tile_e = jnp.sum(((jnp.arange(n_tiles) * tm)[:, None] >= ends[None, :]).astype(jnp.int32), axis=1)
    tile_e = jnp.minimum(tile_e, MOE_EXPERTS - 1)
    tok = jnp.broadcast_to(jnp.arange(n_tok, dtype=jnp.int32), (2, n_tok))
    src = jnp.zeros((n_rows,), jnp.int32).at[dest.reshape(-1)].set(tok.reshape(-1))
    return dest[0], dest[1], src, tile_e, (ends[-1] // tm).reshape(1)


def _row_copy(src_hbm, row, dst, slot, r, sem):
    return pltpu.make_async_copy(src_hbm.at[pl.ds(row, 1)], dst.at[slot, pl.ds(r, 1)], sem)


def _experts_kernel(src_ref, te_ref, nu_ref, v_hbm, wg_ref, wu_ref, wd_ref, y_ref, xbuf, sem):
    del te_ref
    i = pl.program_id(0)
    n_used = nu_ref[0]
    slot = i % 2

    nt = MOE_ROW_TILES

    def token_copy(tok, into, r):
        return pltpu.make_async_copy(v_hbm.at[pl.ds(tok * nt, nt)], xbuf.at[into, pl.ds(r * nt, nt)], sem.at[into])

    def gather(tile, into):
        for r in range(MOE_TM):
            token_copy(src_ref[tile * MOE_TM + r], into, r).start(priority=r % 2)

    def expert():
        for r in range(MOE_TM):
            token_copy(0, slot, r).wait()
        word = jnp.concatenate([xbuf[slot, pl.ds(s, MOE_TM, stride=nt), :] for s in range(nt)], axis=1)
        t = jnp.concatenate(_unpack_halves(word), axis=1).astype(BF16)
        hid = _silu(_dot(t, wg_ref[...].astype(BF16))) * _dot(t, wu_ref[...].astype(BF16))
        y_ref[...] = _pack_halves(_dot(hid.astype(BF16), wd_ref[...].astype(BF16)))

    @pl.when((i == 0) & (n_used > 0))
    def _():
        gather(0, 0)

    @pl.when(i + 1 < n_used)
    def _():
        gather(i + 1, 1 - slot)
        expert()

    @pl.when(i + 1 == n_used)
    def _():
        expert()

    @pl.when(i >= n_used)
    def _():
        y_ref[...] = jnp.zeros_like(y_ref)


def _experts(v, src, tile_e, n_used, layer, w_gate, w_up, w_down):
    d = w_gate.shape[2]
    hdim = w_gate.shape[3]
    n_rows = src.shape[0]
    tm = MOE_TM
    by_expert = lambda i, src_r, te_r, nu_r: (layer, te_r[i], 0, 0)
    return pl.pallas_call(
        _experts_kernel,
        out_shape=jax.ShapeDtypeStruct((n_rows, d // 2), jnp.uint32),
        grid_spec=pltpu.PrefetchScalarGridSpec(
            num_scalar_prefetch=3,
            grid=(n_rows // tm,),
            in_specs=[
                pl.BlockSpec(memory_space=pl.ANY),
                pl.BlockSpec((None, None, d, hdim), by_expert),
                pl.BlockSpec((None, None, d, hdim), by_expert),
                pl.BlockSpec((None, None, hdim, d), by_expert),
            ],
            out_specs=pl.BlockSpec((tm, d // 2), lambda i, *_: (i, 0)),
            scratch_shapes=[pltpu.VMEM((2, tm * MOE_ROW_TILES, LANES), jnp.uint32), pltpu.SemaphoreType.DMA((2,))],
        ),
        compiler_params=_cp("arbitrary"),
        name="moe_experts",
    )(src, tile_e, n_used, v, w_gate, w_up, w_down)


def _combine_kernel(d1_ref, d2_ref, y_hbm, meta_ref, x_ref, gate_ref, o_ref, buf1, buf2, sem):
    i = pl.program_id(0)
    bm = x_ref.shape[0]
    slot = i % 2

    def gather(tile, into):
        for r in range(bm):
            t = tile * bm + r
            _row_copy(y_hbm, d1_ref[t], buf1, into, r, sem.at[0, into]).start(priority=0)
            _row_copy(y_hbm, d2_ref[t], buf2, into, r, sem.at[1, into]).start(priority=1)

    @pl.when(i == 0)
    def _():
        gather(0, 0)

    @pl.when(i + 1 < pl.num_programs(0))
    def _():
        gather(i + 1, 1 - slot)

    for r in range(bm):
        _row_copy(y_hbm, 0, buf1, slot, r, sem.at[0, slot]).wait()
        _row_copy(y_hbm, 0, buf2, slot, r, sem.at[1, slot]).wait()
    meta = meta_ref[...]
    lane = _iota(meta.shape, 1)
    w1 = jnp.sum(jnp.where(lane == META_W1, meta, 0.0), axis=-1, keepdims=True)
    w2 = jnp.sum(jnp.where(lane == META_W2, meta, 0.0), axis=-1, keepdims=True)
    lo1, hi1 = _unpack_halves(buf1[slot])
    lo2, hi2 = _unpack_halves(buf2[slot])
    moe = jnp.concatenate([w1 * lo1 + w2 * lo2, w1 * hi1 + w2 * hi2], axis=1)
    o_ref[...] = x_ref[...] + gate_ref[...] * moe


def _combine(y, d1, d2, meta, x, mod, layer, bm, nc, seq):
    n, d = x.shape
    at_tile = lambda i, *_: (i, 0)
    mod_index = _mod_spec(layer, 5, bm, nc, seq).index_map
    return pl.pallas_call(
        _combine_kernel,
        out_shape=jax.ShapeDtypeStruct((n, d), F32),
        grid_spec=pltpu.PrefetchScalarGridSpec(
            num_scalar_prefetch=2,
            grid=(n // bm,),
            in_specs=[
                pl.BlockSpec(memory_space=pl.ANY),
                pl.BlockSpec((bm, LANES), at_tile),
                pl.BlockSpec((bm, d), at_tile),
                pl.BlockSpec((None, None, None, 1, d), lambda i, *_: mod_index(i)),
            ],
            out_specs=pl.BlockSpec((bm, d), at_tile),
            scratch_shapes=[pltpu.VMEM((2, bm, d // 2), jnp.uint32), pltpu.VMEM((2, bm, d // 2), jnp.uint32),
                            pltpu.SemaphoreType.DMA((2, 2))],
        ),
        compiler_params=_cp("arbitrary"),
        name="moe_combine",
    )(d1, d2, y, meta, x, mod)


def _final_norm_kernel(x_ref, g_ref, o_ref):
    x = x_ref[...]
    o_ref[...] = x * lax.rsqrt(jnp.mean(x * x, axis=-1, keepdims=True) + NORM_EPS) * g_ref[...]


def _final_norm(x, g, bm, row0):
    n, d = x.shape
    n -= row0 * bm
    return pl.pallas_call(
        _final_norm_kernel,
        out_shape=jax.ShapeDtypeStruct((n, d), F32),
        grid=(n // bm,),
        in_specs=[pl.BlockSpec((bm, d), lambda i: (i + row0, 0)), pl.BlockSpec((1, d), lambda i: (0, 0))],
        out_specs=pl.BlockSpec((bm, d), lambda i: (i, 0)),
        compiler_params=_cp("parallel"),
        name="final_norm",
    )(x, g)


_COL = dict(s5=0, rw=S5_WIDTH, rg=S5_WIDTH + 3 * RW_WIDTH + 128, q=4352, k=4864, v=5376, gk=6400, og=6416, gates=7440)


def _pad_rows(w, rows):
    return jnp.pad(w, ((0, rows - w.shape[0]), (0, 0)))


def _layer(i, last, xa, mod, bsz, seq, ctx_len, p):
    nc = bsz * ctx_len
    rows = xa.shape[0]
    bm = 512
    msel = dict(nc=nc, seq=seq)
    u = _normmod(xa, p['g_norm1'].reshape(-1, 1, D_MODEL), mod, i, (0, 1), bm, **msel)

    w_in = p['w_in'][i]
    col = lambda a, w: w_in[:, a:a + w].astype(BF16)
    bm_in = 1088 if rows % 1088 == 0 else bm
    w_a = jnp.concatenate([col(0, S5_WIDTH), col(_COL['og'], GLA_V), col(_COL['gates'], 3 * D_MODEL)], axis=1)
    w_z = jnp.concatenate([col(_COL['rw'], 3 * RW_WIDTH + 128), col(_COL['rg'], 128)], axis=1)
    w_qk = col(_COL['q'], 2 * GLA_QK)
    w_v = col(_COL['v'], GLA_V)
    w_gk = jnp.pad(col(_COL['gk'], 16), ((0, 0), (0, LANES - 16)))
    pa_in = _mm(u, w_a, bm_in, 1024, BF16)
    zall = _mm(u, w_z, bm_in, 1664)
    qk = _mm(u, w_qk, bm_in, 1024)
    vv = _mm(u, w_v, bm_in, 1024)
    gk = _mm(u, w_gk, bm_in, LANES)

    ya = []
    for d in range(2):
        tables = _s5_tables(*(p[k][i, d] for k in ('s5_a_re', 's5_a_im', 's5_log_dt', 's5_b_re', 's5_b_im',
                                                   's5_c_re', 's5_c_im')))
        ya.append(_s5_scan(pa_in, tables, bsz, ctx_len // S5_BLOCK, seq // S5_BLOCK, bool(d)))

    yb = []
    for d in range(2):
        prm = (
            jnp.pad(p['rw_mu'][i, d], (0, RW_Z - p['rw_mu'].shape[-1])).reshape(1, RW_Z),
            p['rw_w0'][i, d].reshape(1, -1),
            _pad_rows(p['rw_w_up'][i, d], LANES).astype(BF16),
            p['rw_a0'][i, d].reshape(1, -1),
            jnp.pad(p['rw_a_up'][i, d], ((RW_HEAD, 0), (0, 0))).astype(BF16),
            p['rw_k_k'][i].reshape(1, -1), p['rw_k_a'][i].reshape(1, -1), p['rw_r_k'][i].reshape(1, -1),
        )
        yb.append(_rwkv_direction(zall, prm, bsz, ctx_len // RW_BLOCK, seq // RW_BLOCK, bool(d)))

    gk_up = jnp.pad(p['gl_gk_up'][i], ((0, 0), (0, LANES - 16), (0, 0))).astype(BF16)
    gk_b = p['gl_gk_b'][i].reshape(2, 1, GLA_QK)
    s0 = jnp.zeros((2, bsz, GLA_HEADS, GLA_DV, GLA_DK), F32)
    oc, s_ctx = _gla(_to_chunk_major(qk[:nc], bsz), _to_chunk_major(vv[:nc], bsz), _to_chunk_major(gk[:nc], bsz),
                     gk_up, gk_b, s0)
    rows_l = seq // GRID_W
    lat = lambda t: t[nc:].reshape(bsz, rows_l, GRID_W * t.shape[1])
    ol, _ = _gla(lat(qk), lat(vv), lat(gk), gk_up, gk_b, s_ctx)
    o_dir = [jnp.concatenate([_from_chunk_major(oc[d], GLA_V).reshape(nc, GLA_V),
                              ol[d].reshape(bsz * seq, GLA_V)], axis=0) for d in range(2)]

    row0 = (nc // bm) if last else 0
    branch_prm = (p['s5_d'][i].reshape(1, -1), p['rw_g_up'][i].astype(BF16), p['rw_ln_w'][i].reshape(1, -1),
                  p['rw_ln_b'][i].reshape(1, -1), p['gl_norm_g'][i].reshape(1, -1), p['s5_w_glu'][i].astype(BF16),
                  p['rw_w_proj'][i].astype(BF16), p['gl_w_proj'][i].astype(BF16))
    merged = _branches(ya, yb, o_dir, pa_in, zall, branch_prm, row0 * (bm // RW_BLOCK), ctx_len // RW_BLOCK,
                       seq // RW_BLOCK)
    xm = _out_proj(merged, p['w_out'][i].astype(BF16), xa, mod, i, bm, row0, nc, seq)
    if last:
        msel = dict(nc=0, seq=seq)

    wr = jnp.pad(jnp.concatenate([p['moe_wg1'][i], p['moe_wg2'][i]], axis=1), ((0, 0), (0, LANES - 36)))
    wr_hi = wr.astype(BF16)
    wr_lo = (wr - wr_hi.astype(F32)).astype(BF16)
    br = jnp.pad(jnp.concatenate([p['moe_bg1'][i], p['moe_bg2'][i]]), (0, LANES - 36)).reshape(1, LANES)
    vmoe, meta, meta_t, cnt = _router(xm, p['g_norm2'].reshape(-1, 1, D_MODEL), mod, i, wr_hi, wr_lo, br, bm, **msel)
    d1, d2, src, tile_e, n_used = _moe_plan(meta_t, cnt)
    y_sorted = _experts(vmoe, src, tile_e, n_used, i, p['moe_w_gate'], p['moe_w_up'], p['moe_w_down'])
    return _combine(y_sorted, d1, d2, meta, xm, mod, i, bm // 2, **msel)


def kernel(x, c, ctx, c_ctx, w_mod, b_mod, g_norm1, g_norm2, w_in, s5_a_re, s5_a_im, s5_log_dt, s5_b_re, s5_b_im,
           s5_c_re, s5_c_im, s5_d, s5_w_glu, rw_mu, rw_w0, rw_w_up, rw_a0, rw_a_up, rw_k_k, rw_k_a, rw_r_k, rw_g_up,
           rw_ln_w, rw_ln_b, rw_w_proj, gl_gk_up, gl_gk_b, gl_norm_g, gl_w_proj, w_out, moe_wg1, moe_bg1, moe_wg2,
           moe_bg2, moe_w_gate, moe_w_up, moe_w_down, g_final):
    p = dict(g_norm1=g_norm1, g_norm2=g_norm2, w_in=w_in, s5_a_re=s5_a_re, s5_a_im=s5_a_im, s5_log_dt=s5_log_dt,
             s5_b_re=s5_b_re, s5_b_im=s5_b_im, s5_c_re=s5_c_re, s5_c_im=s5_c_im, s5_d=s5_d, s5_w_glu=s5_w_glu,
             rw_mu=rw_mu, rw_w0=rw_w0, rw_w_up=rw_w_up, rw_a0=rw_a0, rw_a_up=rw_a_up, rw_k_k=rw_k_k, rw_k_a=rw_k_a,
             rw_r_k=rw_r_k.reshape(rw_r_k.shape[0], -1), rw_g_up=rw_g_up, rw_ln_w=rw_ln_w, rw_ln_b=rw_ln_b,
             rw_w_proj=rw_w_proj, gl_gk_up=gl_gk_up, gl_gk_b=gl_gk_b, gl_norm_g=gl_norm_g, gl_w_proj=gl_w_proj,
             w_out=w_out, moe_wg1=moe_wg1, moe_bg1=moe_bg1, moe_wg2=moe_wg2, moe_bg2=moe_bg2, moe_w_gate=moe_w_gate,
             moe_w_up=moe_w_up, moe_w_down=moe_w_down)
    bsz, seq, d = x.shape
    ctx_len = ctx.shape[1]
    depth = w_mod.shape[0]
    cc = jnp.concatenate([c, c_ctx[None], jnp.zeros((8 - bsz - 1, d), F32)], axis=0)
    mod = _adaln(cc, w_mod, b_mod).reshape(depth, 8, 6, 1, d)
    xa = jnp.concatenate([ctx.reshape(bsz * ctx_len, d), x.reshape(bsz * seq, d)], axis=0)
    for i in range(depth):
        xa = _layer(i, i == depth - 1, xa, mod, bsz, seq, ctx_len, p)
    out = _final_norm(xa, g_final.reshape(1, d), 512, 0)
    return out.reshape(bsz, seq, d)
```

```python
import functools
import math

import jax
import jax.numpy as jnp
from jax import lax
from jax.experimental import pallas as pl
from jax.experimental.pallas import tpu as pltpu

F32 = jnp.float32
BF16 = jnp.bfloat16

D_MODEL = 2048
GRID_W = 64
NORM_EPS = 1e-6

S5_WIDTH = 1024
S5_GROUP = 16
S5_GROUPS = 64
S5_STATE = 64
S5_MAX_RE = -1e-4
S5_TILE = 16
S5_PAIRS = S5_GROUPS // 2

RW_WIDTH = 1024
RW_HEAD = 64
RW_DECAY_SCALE = 0.606531
RW_GN_EPS = 64e-5
RW_BLOCK = 256
RW_CHUNK = 16
RW_PAIRS = RW_WIDTH // 128
RW_Z = 3 * RW_WIDTH + 128

GLA_HEADS = 4
GLA_DK = 128
GLA_DV = 256
GLA_QK = 512
GLA_V = 1024
GLA_TAU = 16.0
GLA_CHUNK = 64

MOE_GROUPS = 4
MOE_PER_GROUP = 8
MOE_EXPERTS = 32
MOE_HIDDEN = 256
MOE_LANE0 = MOE_GROUPS

LANES = 128
VMEM_LIMIT = 56 * 1024 * 1024


def _cp(*sem):
    return pltpu.CompilerParams(dimension_semantics=sem, vmem_limit_bytes=VMEM_LIMIT)


def _dot(a, b):
    return jnp.dot(a, b, preferred_element_type=F32)


def _dot_nt(a, b):
    return lax.dot_general(a, b, (((1,), (1,)), ((), ())), preferred_element_type=F32)


def _dot_tn(a, b):
    return lax.dot_general(a, b, (((0,), (0,)), ((), ())), preferred_element_type=F32)


def _sigmoid(x):
    return 1.0 / (1.0 + jnp.exp(-x))


def _silu(x):
    return x * _sigmoid(x)


def _gelu_tanh(x):
    return 0.5 * x * (1.0 + jnp.tanh(math.sqrt(2.0 / math.pi) * (x + 0.044715 * (x * x * x))))


def _split_hi_lo(x):
    hi = x.astype(BF16)
    lo = (x - hi.astype(F32)).astype(BF16)
    return hi, lo


def _pack_halves(x):
    bits = lax.bitcast_convert_type(x.astype(BF16).astype(F32), jnp.uint32)
    n = x.shape[1] // 2
    return (bits[:, n:] & jnp.uint32(0xFFFF0000)) | (bits[:, :n] >> 16)


def _unpack_halves(word):
    return (lax.bitcast_convert_type(word << 16, F32),
            lax.bitcast_convert_type(word & jnp.uint32(0xFFFF0000), F32))


def _iota(shape, dim):
    return lax.broadcasted_iota(jnp.int32, shape, dim)


def _head_ones(width):
    return (_iota((LANES, LANES), 0) // width == _iota((LANES, LANES), 1) // width).astype(BF16)


def _head_sum(x, ones, exact):
    if exact:
        hi, lo = _split_hi_lo(x)
        return _dot(hi, ones) + _dot(lo, ones)
    return _dot(x.astype(BF16), ones)


def _adaln_kernel(c_ref, w_ref, b_ref, o_ref):
    c = c_ref[...]
    o_ref[...] = _dot(_silu(c).astype(BF16), w_ref[...].astype(BF16)) + b_ref[...]


def _adaln(cc, w_mod, b_mod):
    depth, d, n = w_mod.shape
    bn = 1536
    return pl.pallas_call(
        _adaln_kernel,
        out_shape=jax.ShapeDtypeStruct((depth, 8, n), F32),
        grid=(depth, n // bn),
        in_specs=[
            pl.BlockSpec((8, d), lambda l, j: (0, 0)),
            pl.BlockSpec((None, d, bn), lambda l, j: (l, 0, j)),
            pl.BlockSpec((None, 1, bn), lambda l, j: (l, 0, j)),
        ],
        out_specs=pl.BlockSpec((None, 8, bn), lambda l, j: (l, 0, j)),
        compiler_params=_cp("parallel", "parallel"),
        name="adaln",
    )(cc, w_mod, b_mod.reshape(depth, 1, n))


def _mod_spec(layer, part, bm, nc, seq):
    def index(i, *_):
        r0 = i * bm
        return (layer, jnp.where(r0 < nc, 2, (r0 - nc) // seq), part, 0, 0)
    return pl.BlockSpec((None, None, None, 1, D_MODEL), index)


def _normmod_kernel(x_ref, g_ref, sh_ref, sc_ref, o_ref):
    x = x_ref[...]
    y = x * lax.rsqrt(jnp.mean(x * x, axis=-1, keepdims=True) + NORM_EPS) * g_ref[...]
    o_ref[...] = (y * (1.0 + sc_ref[...]) + sh_ref[...]).astype(o_ref.dtype)


def _normmod(x, g, mod, layer, parts, bm, nc, seq):
    n, d = x.shape
    return pl.pallas_call(
        _normmod_kernel,
        out_shape=jax.ShapeDtypeStruct((n, d), BF16),
        grid=(n // bm,),
        in_specs=[
            pl.BlockSpec((bm, d), lambda i: (i, 0)),
            pl.BlockSpec((None, 1, d), lambda i: (layer, 0, 0)),
            _mod_spec(layer, parts[0], bm, nc, seq),
            _mod_spec(layer, parts[1], bm, nc, seq),
        ],
        out_specs=pl.BlockSpec((bm, d), lambda i: (i, 0)),
        compiler_params=_cp("parallel"),
        name="normmod",
    )(x, g, mod, mod)


def _mm_kernel(x_ref, w_ref, o_ref):
    o_ref[...] = _dot(x_ref[...], w_ref[...]).astype(o_ref.dtype)


def _mm(x, w, bm, bn, out_dtype=F32):
    m, k = x.shape
    n = w.shape[1]
    return pl.pallas_call(
        _mm_kernel,
        out_shape=jax.ShapeDtypeStruct((m, n), out_dtype),
        grid=(m // bm, n // bn),
        in_specs=[pl.BlockSpec((bm, k), lambda i, j: (i, 0)), pl.BlockSpec((k, bn), lambda i, j: (0, j))],
        out_specs=pl.BlockSpec((bm, bn), lambda i, j: (i, j)),
        compiler_params=_cp("parallel", "parallel"),
        name="mm",
    )(x, w)


def _seq_blocks(bsz, nc_blocks, nl_blocks, reverse):
    def local(s):
        lat = nc_blocks + ((nl_blocks - 1 - (s - nc_blocks)) if reverse else (s - nc_blocks))
        ctx = (nc_blocks - 1 - s) if reverse else s
        return jnp.where(s < nc_blocks, ctx, lat)

    def block(b, s):
        i = local(s)
        return jnp.where(i < nc_blocks, b * nc_blocks + i, bsz * nc_blocks + b * nl_blocks + i - nc_blocks)

    return local, block


S5_PACK = 8
S5_PACKS = S5_GROUPS // S5_PACK
S5_PLANE = S5_PACK * S5_STATE
S5_BLOCK = 256
S5_ROWS = 24


def _s5_tables(a_re, a_im, log_dt, b_re, b_im, c_re, c_im):
    lam = lax.complex(jnp.minimum(a_re, S5_MAX_RE), a_im)
    ldt = lam * jnp.exp(log_dt)[:, None]
    lam_bar = jnp.exp(ldt)
    b_bar = ((lam_bar - 1.0) / lam)[..., None] * lax.complex(b_re, b_im)
    c_mat = lax.complex(c_re, c_im)
    eye = jnp.eye(S5_PACK, dtype=F32)

    def block_diag(t):
        k, g, a, b = t.shape
        return (t[:, :, :, None, :] * eye[None, :, None, :, None]).reshape(k, g * a, g * b)

    b_t = jnp.transpose(b_bar, (0, 2, 1)).reshape(S5_PACKS, S5_PACK, S5_GROUP, S5_STATE)
    bblk = jnp.concatenate([block_diag(b_t.real), block_diag(b_t.imag)], axis=2)
    c_t = jnp.transpose(c_mat, (0, 2, 1)).reshape(S5_PACKS, S5_PACK, S5_STATE, S5_GROUP)
    cblk = jnp.concatenate([block_diag(c_t.real), block_diag(-c_t.imag)], axis=1)
    expo = jnp.concatenate([jnp.arange(1, S5_TILE + 1, dtype=F32), jnp.asarray([32.0, 64.0, 128.0], F32),
                            jnp.zeros((S5_ROWS - S5_TILE - 3,), F32)])
    pw = jnp.exp(ldt[None] * expo[:, None, None]).reshape(S5_ROWS, S5_PACKS, S5_PLANE)
    pw = jnp.transpose(pw, (1, 0, 2))
    return bblk.astype(BF16), cblk.astype(BF16), pw.real, pw.imag


def _s5_scan_kernel(u_ref, bblk_ref, cblk_ref, pre_ref, pim_ref, y_ref, car_ref, up_s, h2_s, hb2_s, yp_s, *, reverse):
    t = S5_TILE
    n_t = S5_BLOCK // t
    pn = S5_PLANE

    @pl.when(pl.program_id(1) == 0)
    def _():
        car_ref[...] = jnp.zeros_like(car_ref)

    ra = _iota((S5_BLOCK, S5_BLOCK), 0)
    cb = _iota((S5_BLOCK, S5_BLOCK), 1)
    perm = ((ra // t == cb % t) & (ra % t == cb // t)).astype(BF16)
    up = _dot(perm, u_ref[...].astype(BF16)).astype(BF16)
    for pk in range(S5_PACKS):
        up_s[pk] = up[:, LANES * pk:LANES * (pk + 1)]
    order = list(range(t - 1, -1, -1)) if reverse else list(range(t))
    rowj = _iota((n_t, 1), 0)

    def states(pk, bu, h_s, hb_s):
        pre = pre_ref[pk]
        pim = pim_ref[pk]
        l_re, l_im = pre[0:1], pim[0:1]
        h_re = h_im = None
        for n, s in enumerate(order):
            rows = slice(t * s, t * (s + 1))
            b_re, b_im = bu[rows, :pn], bu[rows, pn:]
            if n == 0:
                h_re, h_im = b_re, b_im
            else:
                h_re, h_im = l_re * h_re - l_im * h_im + b_re, l_re * h_im + l_im * h_re + b_im
            h_s[rows, :pn] = h_re
            h_s[rows, pn:] = h_im
        c_re, c_im = car_ref[pk, 0:1, :pn], car_ref[pk, 0:1, pn:]
        first = rowj == (n_t - 1 if reverse else 0)
        g_re, g_im = pre[t - 1:t], pim[t - 1:t]
        e_re = h_re + jnp.where(first, g_re * c_re - g_im * c_im, 0.0)
        e_im = h_im + jnp.where(first, g_re * c_im + g_im * c_re, 0.0)
        step = 1
        for row in (t - 1, t, t + 1, t + 2):
            if reverse:
                s_re, s_im, ok = pltpu.roll(e_re, n_t - step, 0), pltpu.roll(e_im, n_t - step, 0), rowj < n_t - step
            else:
                s_re, s_im, ok = pltpu.roll(e_re, step, 0), pltpu.roll(e_im, step, 0), rowj >= step
            a_re, a_im = pre[row:row + 1], pim[row:row + 1]
            e_re = e_re + jnp.where(ok, a_re * s_re - a_im * s_im, 0.0)
            e_im = e_im + jnp.where(ok, a_re * s_im + a_im * s_re, 0.0)
            step *= 2
        last = 0 if reverse else n_t - 1
        car_ref[pk, 0:1, :pn] = e_re[last:last + 1]
        car_ref[pk, 0:1, pn:] = e_im[last:last + 1]
        if reverse:
            in_re = jnp.where(first, c_re, pltpu.roll(e_re, n_t - 1, 0))
            in_im = jnp.where(first, c_im, pltpu.roll(e_im, n_t - 1, 0))
        else:
            in_re = jnp.where(first, c_re, pltpu.roll(e_re, 1, 0))
            in_im = jnp.where(first, c_im, pltpu.roll(e_im, 1, 0))
        for n, s in enumerate(order):
            rows = slice(t * s, t * (s + 1))
            a_re, a_im = pre[n:n + 1], pim[n:n + 1]
            hb_s[rows, :pn] = (h_s[rows, :pn] + a_re * in_re - a_im * in_im).astype(BF16)
            hb_s[rows, pn:] = (h_s[rows, pn:] + a_re * in_im + a_im * in_re).astype(BF16)

    def pack_pair(j, carry):
        pks = (2 * j, 2 * j + 1)
        bu = [_dot(up_s[pk], bblk_ref[pk]) for pk in pks]
        for u, pk in enumerate(pks):
            states(pk, bu[u], h2_s.at[u], hb2_s.at[u])
            yp_s[pk] = _dot(hb2_s[u], cblk_ref[pk]).astype(BF16)
        return carry

    lax.fori_loop(0, S5_PACKS // 2, pack_pair, 0)
    for pk in range(S5_PACKS):
        y_ref[:, LANES * pk:LANES * (pk + 1)] = _dot(perm, yp_s[pk])


def _s5_scan(u_src, tables, bsz, nc_blocks, nl_blocks, reverse):
    bblk, cblk, pre, pim = tables
    rows = u_src.shape[0]
    blk = S5_BLOCK
    _, block = _seq_blocks(bsz, nc_blocks, nl_blocks, reverse)
    whole = lambda a: pl.BlockSpec(a.shape, lambda b, s: (0,) * a.ndim)
    return pl.pallas_call(
        functools.partial(_s5_scan_kernel, reverse=reverse),
        out_shape=jax.ShapeDtypeStruct((rows, S5_WIDTH), F32),
        grid=(bsz, nc_blocks + nl_blocks),
        in_specs=[pl.BlockSpec((blk, S5_WIDTH), lambda b, s: (block(b, s), 0)),
                  whole(bblk), whole(cblk), whole(pre), whole(pim)],
        out_specs=pl.BlockSpec((blk, S5_WIDTH), lambda b, s: (block(b, s), 0)),
        scratch_shapes=[
            pltpu.VMEM((S5_PACKS, 8, 2 * S5_PLANE), F32),
            pltpu.VMEM((S5_PACKS, blk, LANES), BF16),
            pltpu.VMEM((2, blk, 2 * S5_PLANE), F32),
            pltpu.VMEM((2, blk, 2 * S5_PLANE), BF16),
            pltpu.VMEM((S5_PACKS, blk, LANES), BF16),
        ],
        compiler_params=_cp("parallel", "arbitrary"),
        name="s5_bwd" if reverse else "s5_fwd",
    )(u_src, bblk, cblk, pre, pim)


def _gla_kernel(q_ref, k_ref, v_ref, gk_ref, up_ref, gb_ref, s0_ref, o_ref, sf_ref, st_ref):
    d = pl.program_id(0)
    c = pl.program_id(1)
    n = pl.num_programs(1)

    @pl.when(c == 0)
    def _():
        st_ref[...] = s0_ref[...]

    cs = GLA_CHUNK
    nb = q_ref.shape[0]
    sign = 1 - 2 * d
    row = _iota((cs, cs), 0)
    col = _iota((cs, cs), 1)
    causal = (row - col) * sign >= 0
    causal_b = causal.astype(BF16)
    rid = _iota((cs, 1), 0)
    bcum, b_mid, b_end = [], [], []
    for b in range(nb):
        x = _dot(gk_ref[b].astype(BF16), up_ref[...]) + gb_ref[...]
        log_a = (jnp.minimum(x, 0.0) - jnp.log(1.0 + jnp.exp(-jnp.abs(x)))) * (1.0 / GLA_TAU)
        la_hi, la_lo = _split_hi_lo(log_a)
        bc = _dot(causal_b, la_hi) + _dot(causal_b, la_lo)
        bcum.append(bc)
        b_mid.append(jnp.sum(jnp.where(rid == cs // 2 - d, bc, 0.0), axis=0, keepdims=True))
        b_end.append(jnp.sum(jnp.where(rid == (cs - 1) * (1 - d), bc, 0.0), axis=0, keepdims=True))
    units = [(b, h) for b in range(nb) for h in range(GLA_HEADS)]
    ks = lambda h: slice(GLA_DK * h, GLA_DK * (h + 1))
    vs = lambda h: slice(GLA_DV * h, GLA_DV * (h + 1))
    s_old = [st_ref[b, h] for b, h in units]
    qh = [q_ref[b, :, ks(h)].astype(F32) * (GLA_DK ** -0.5) for b, h in units]
    kh = [k_ref[b, :, ks(h)].astype(F32) for b, h in units]
    vh = [v_ref[b, :, vs(h)].astype(BF16) for b, h in units]
    bh = [bcum[b][:, ks(h)] for b, h in units]
    mid = [b_mid[b][:, ks(h)] for b, h in units]
    end = [b_end[b][:, ks(h)] for b, h in units]
    scores = [_dot_nt((q * jnp.exp(x - m)).astype(BF16), (k * jnp.exp(m - x)).astype(BF16))
              for q, k, x, m in zip(qh, kh, bh, mid)]
    inter = [_dot_nt((q * jnp.exp(x)).astype(BF16), s.astype(BF16)) for q, x, s in zip(qh, bh, s_old)]
    upd = [_dot_tn(v, (k * jnp.exp(e - x)).astype(BF16)) for v, k, e, x in zip(vh, kh, end, bh)]
    intra = [_dot(jnp.where(causal, sc, 0.0).astype(BF16), v) for sc, v in zip(scores, vh)]
    for u, (b, h) in enumerate(units):
        o_ref[b, :, vs(h)] = (intra[u] + inter[u]).astype(o_ref.dtype)
        st_ref[b, h] = s_old[u] * jnp.exp(end[u]) + upd[u]

    @pl.when(c == n - 1)
    def _():
        sf_ref[...] = st_ref[...]


def _gla(qk, v, gk, gk_up, gk_b, s0):
    bsz, cs, w = v.shape
    n = w // GLA_V

    def chunk(d, c):
        return c + d * (n - 1 - 2 * c)

    state_spec = pl.BlockSpec((None, bsz, GLA_HEADS, GLA_DV, GLA_DK), lambda d, c: (d, 0, 0, 0, 0))
    return pl.pallas_call(
        _gla_kernel,
        out_shape=(jax.ShapeDtypeStruct((2, bsz, cs, w), BF16), jax.ShapeDtypeStruct(s0.shape, F32)),
        grid=(2, n),
        in_specs=[
            pl.BlockSpec((bsz, cs, GLA_QK), lambda d, c: (0, 0, 2 * chunk(d, c))),
            pl.BlockSpec((bsz, cs, GLA_QK), lambda d, c: (0, 0, 2 * chunk(d, c) + 1)),
            pl.BlockSpec((bsz, cs, GLA_V), lambda d, c: (0, 0, chunk(d, c))),
            pl.BlockSpec((bsz, cs, LANES), lambda d, c: (0, 0, chunk(d, c))),
            pl.BlockSpec((None, LANES, GLA_QK), lambda d, c: (d, 0, 0)),
            pl.BlockSpec((None, 1, GLA_QK), lambda d, c: (d, 0, 0)),
            state_spec,
        ],
        out_specs=(pl.BlockSpec((None, bsz, cs, GLA_V), lambda d, c: (d, 0, 0, chunk(d, c))), state_spec),
        scratch_shapes=[pltpu.VMEM((bsz, GLA_HEADS, GLA_DV, GLA_DK), F32)],
        compiler_params=_cp("parallel", "arbitrary"),
        name="gla",
    )(qk, qk, v, gk, gk_up, gk_b, s0)


def _to_chunk_major(t, bsz):
    n = t.shape[0] // bsz
    d = t.shape[1]
    t = t.reshape(bsz, n // GLA_CHUNK, GLA_CHUNK, d)
    return jnp.transpose(t, (0, 2, 1, 3)).reshape(bsz, GLA_CHUNK, (n // GLA_CHUNK) * d)


def _from_chunk_major(t, d):
    lead = t.shape[:-2]
    n = t.shape[-1] // d
    t = t.reshape(lead + (GLA_CHUNK, n, d))
    return jnp.swapaxes(t, -3, -2).reshape(lead + (n * GLA_CHUNK, d))


def _rwkv_kernel(z0_ref, z1_ref, halo0_ref, halo1_ref, mu_ref, w0_ref, wup_ref, a0_ref, aup_ref, kk_ref, ka_ref,
                 rk_ref, y_ref, st_ref, kt_s, bt_s, kq_s, rt_s, v_s, w_s, u_s, ya_s, ab_s, pin_s,
                 *, reverse, nc_blocks):
    step = pl.program_id(0)
    blk = RW_BLOCK
    ch = RW_CHUNK
    n_ch = blk // ch
    n_b = 2
    n_units = n_b * RW_PAIRS

    @pl.when(step == 0)
    def _():
        st_ref[...] = jnp.zeros_like(st_ref)

    rowi = _iota((blk, 1), 0)
    low = _iota((1, LANES), 1) < RW_HEAD
    ones64 = _head_ones(RW_HEAD)
    pos = rowi % ch
    seq_start = (step == 0) | (step == nc_blocks)

    for b, (z_ref, halo_ref) in enumerate(((z0_ref, halo0_ref), (z1_ref, halo1_ref))):
        z = z_ref[...]
        if reverse:
            prev = pltpu.roll(z, blk - 1, 0)
            edge = halo_ref[0:1, :]
            at_edge = rowi == blk - 1
        else:
            prev = pltpu.roll(z, 1, 0)
            edge = halo_ref[7:8, :]
            at_edge = rowi == 0
        prev = jnp.where(at_edge, jnp.where(seq_start, 0.0, edge), prev)
        zs = z + (prev - z) * mu_ref[...]
        r = zs[:, 0:RW_WIDTH]
        k = zs[:, RW_WIDTH:2 * RW_WIDTH]
        v = zs[:, 2 * RW_WIDTH:3 * RW_WIDTH]
        lora = zs[:, 3 * RW_WIDTH:3 * RW_WIDTH + LANES]
        lora_w = jnp.where(low, jnp.tanh(lora), 0.0).astype(BF16)
        lora_a = jnp.where(low, 0.0, lora).astype(BF16)
        logw = -RW_DECAY_SCALE * _sigmoid(w0_ref[...] + _dot(lora_w, wup_ref[...]))
        a = _sigmoid(a0_ref[...] + _dot(lora_a, aup_ref[...]))
        kk = k * kk_ref[...]
        kp = k * (1.0 + (a - 1.0) * ka_ref[...])
        rkb = r * kp * rk_ref[...]

        cl = logw
        sh = 1
        while sh < ch:
            if reverse:
                cl = cl + jnp.where(pos < ch - sh, pltpu.roll(cl, blk - sh, 0), 0.0)
            else:
                cl = cl + jnp.where(pos >= sh, pltpu.roll(cl, sh, 0), 0.0)
            sh *= 2
        p_in = jnp.exp(cl)
        pin_s[b] = p_in
        p_ex = jnp.exp(cl - logw)
        p_inv = jnp.exp(-cl)

        for p in range(RW_PAIRS):
            ls = slice(LANES * p, LANES * (p + 1))
            q = b * RW_PAIRS + p
            kkp = kk[:, ls]
            ssq = _head_sum(kkp * kkp, ones64, False)
            kkn = kkp * (1.0 / jnp.maximum(jnp.sqrt(ssq), 1e-12))
            kt_s[q] = (kkn * p_ex[:, ls]).astype(BF16)
            bt_s[q] = (kkn * a[:, ls] * p_inv[:, ls]).astype(BF16)
            kq_s[q] = (kp[:, ls] * p_inv[:, ls]).astype(BF16)
            rt_s[q] = (r[:, ls] * p_in[:, ls]).astype(BF16)
            v_s[q] = v[:, ls].astype(BF16)
            ya_s[q] = _head_sum(rkb[:, ls], ones64, False) * v[:, ls]

    rr = _iota((blk, blk), 0)
    cc = _iota((blk, blk), 1)
    same = rr // ch == cc // ch
    before = (cc > rr) if reverse else (cc < rr)
    strict = same & before
    incl = same & (before | (rr == cc))
    fold0 = (_iota((blk, LANES), 0) % ch == _iota((blk, LANES), 1)).astype(BF16)
    fold1 = (_iota((blk, LANES), 0) % ch + ch == _iota((blk, LANES), 1)).astype(BF16)
    lane_lo = _iota((1, LANES), 1) < RW_HEAD

    def pair_body(p, carry):
        units = [b * RW_PAIRS + p for b in range(n_b)]
        heads = [(u, hh) for u in range(n_b) for hh in range(2)]
        kt = [kt_s[q] for q in units]
        rt = [rt_s[q] for q in units]
        vv = [v_s[q] for q in units]
        ya = [ya_s[q] for q in units]
        bt = [bt_s[q] for q in units]
        kq = [kq_s[q] for q in units]
        hb = blk // 2
        g = []
        for u, hh in heads:
            mine = lane_lo if hh == 0 else jnp.logical_not(lane_lo)
            zero = jnp.zeros_like(kt[u])
            ktm, rtm = jnp.where(mine, kt[u], zero), jnp.where(mine, rt[u], zero)
            halves = []
            for rows in (slice(0, hb), slice(hb, blk)):
                lhs = jnp.concatenate([ktm[rows], rtm[rows]], axis=0)
                rhs = jnp.concatenate([bt[u][rows], kq[u][rows]], axis=0)
                halves.append(_dot_nt(lhs, rhs))
            g.append(halves)
        zero_q = jnp.zeros((hb, hb), F32)

        def diag2(gh, r0, c0):
            a, b = gh[0][r0:r0 + hb, c0:c0 + hb], gh[1][r0:r0 + hb, c0:c0 + hb]
            return jnp.concatenate([jnp.concatenate([a, zero_q], axis=1), jnp.concatenate([zero_q, b], axis=1)],
                                   axis=0)

        n1 = [jnp.where(strict, -diag2(gh, 0, 0), 0.0).astype(BF16) for gh in g]
        a_kq = [jnp.where(strict, diag2(gh, 0, hb), 0.0).astype(BF16) for gh in g]
        a_rb = [jnp.where(incl, diag2(gh, hb, 0), 0.0).astype(BF16) for gh in g]
        a_rq = [jnp.where(incl, diag2(gh, hb, hb), 0.0).astype(BF16) for gh in g]
        n2 = [_dot(n, n).astype(BF16) for n in n1]
        akv = [_dot(a, vv[u]) for a, (u, _) in zip(a_kq, heads)]
        n4 = [_dot(n, n).astype(BF16) for n in n2]
        y_in = [_dot(a, vv[u]) for a, (u, _) in zip(a_rq, heads)]
        n8 = [_dot(n, n).astype(BF16) for n in n4]
        fold = [_dot(a, fold0 if hh == 0 else fold1) for a, (_, hh) in zip(a_rb, heads)]
        rhs_t = [jnp.concatenate([kt[u].astype(F32), av], axis=1) for av, (u, _) in zip(akv, heads)]
        for nk in (n8, n4, n2, n1):
            rhs_t = [x + _dot(n, x.astype(BF16)) for n, x in zip(nk, rhs_t)]
        for u, q in enumerate(units):
            h0, h1 = 2 * u, 2 * u + 1
            w_s[q] = jnp.where(lane_lo, rhs_t[h0][:, :LANES], rhs_t[h1][:, :LANES]).astype(BF16)
            u_s[q] = jnp.where(lane_lo, rhs_t[h0][:, LANES:], rhs_t[h1][:, LANES:])
            ya_s[q] = ya[u] + jnp.where(lane_lo, y_in[h0], y_in[h1])
            ab_s[q] = (fold[h0] + fold[h1]).astype(BF16)
        return carry

    lax.fori_loop(0, RW_PAIRS, pair_body, 0)

    blockdiag = (_iota((LANES, LANES), 0) // RW_HEAD) == (_iota((LANES, LANES), 1) // RW_HEAD)
    end_row = 0 if reverse else ch - 1

    def chunk_body(i, carry):
        c = (n_ch - 1 - i) if reverse else i
        rows = pl.ds(pl.multiple_of(c * ch, ch), ch)
        s_old = [st_ref[q] for q in range(n_units)]
        m1 = [_dot_nt(jnp.concatenate([w_s[q, rows, :], rt_s[q, rows, :]], axis=0), s_old[q].astype(BF16))
              for q in range(n_units)]
        zc = [-(m1[q][:ch] + u_s[q, rows, :]) for q in range(n_units)]
        upd = []
        for q in range(n_units):
            zv = jnp.concatenate([zc[q].astype(BF16), v_s[q, rows, :]], axis=0)
            bk = jnp.concatenate([bt_s[q, rows, :], kq_s[q, rows, :]], axis=0)
            upd.append(_dot_tn(zv, bk))
        yc = []
        for q in range(n_units):
            z2 = jnp.concatenate([jnp.where(lane_lo, zc[q], 0.0), jnp.where(lane_lo, 0.0, zc[q])], axis=0)
            yc.append(m1[q][ch:] + _dot(ab_s[q, rows, :][:, :2 * ch], z2.astype(BF16)) + ya_s[q, rows, :])
        for q in range(n_units):
            b, p = divmod(q, RW_PAIRS)
            p_end = pin_s[b, rows, LANES * p:LANES * (p + 1)][end_row:end_row + 1]
            st_ref[q] = (s_old[q] + jnp.where(blockdiag, upd[q], 0.0)) * p_end
            y_ref[b, rows, LANES * p:LANES * (p + 1)] = yc[q]
        return carry

    lax.fori_loop(0, n_ch, chunk_body, 0)


def _rwkv_direction(zall, prm, bsz, nc_blocks, nl_blocks, reverse):
    assert bsz == 2
    mu, w0, wup, a0, aup, k_k, k_a, r_k = prm
    rows = zall.shape[0]
    blk = RW_BLOCK
    steps = nc_blocks + nl_blocks
    n_blocks = rows // blk

    def local(s):
        lat = nc_blocks + ((nl_blocks - 1 - (s - nc_blocks)) if reverse else (s - nc_blocks))
        ctx = (nc_blocks - 1 - s) if reverse else s
        return jnp.where(s < nc_blocks, ctx, lat)

    def block(b, s):
        i = local(s)
        return jnp.where(i < nc_blocks, b * nc_blocks + i, bsz * nc_blocks + b * nl_blocks + i - nc_blocks)

    def halo(b, s):
        i = block(b, s)
        if reverse:
            return jnp.minimum((i + 1) * (blk // 8), n_blocks * (blk // 8) - 1)
        return jnp.maximum(i * (blk // 8) - 1, 0)

    vec = lambda w: pl.BlockSpec((1, w), lambda s: (0, 0))
    mat = lambda: pl.BlockSpec((LANES, RW_WIDTH), lambda s: (0, 0))
    n_units = bsz * RW_PAIRS
    unit_bf = pltpu.VMEM((n_units, blk, LANES), BF16)
    unit_f = pltpu.VMEM((n_units, blk, LANES), F32)
    z_spec = lambda b: pl.BlockSpec((blk, RW_Z), lambda s: (block(b, s), 0))
    halo_spec = lambda b: pl.BlockSpec((8, RW_Z), lambda s: (halo(b, s), 0))
    return pl.pallas_call(
        functools.partial(_rwkv_kernel, reverse=reverse, nc_blocks=nc_blocks),
        out_shape=jax.ShapeDtypeStruct((bsz, steps * blk, RW_WIDTH), F32),
        grid=(steps,),
        in_specs=[
            z_spec(0), z_spec(1), halo_spec(0), halo_spec(1),
            vec(RW_Z), vec(RW_WIDTH), mat(), vec(RW_WIDTH), mat(), vec(RW_WIDTH), vec(RW_WIDTH), vec(RW_WIDTH),
        ],
        out_specs=pl.BlockSpec((bsz, blk, RW_WIDTH), lambda s: (0, local(s), 0)),
        scratch_shapes=[
            pltpu.VMEM((n_units, LANES, LANES), F32),
            unit_bf, unit_bf, unit_bf, unit_bf, unit_bf,
            unit_bf, unit_f, unit_f, unit_bf,
            pltpu.VMEM((bsz, blk, RW_WIDTH), F32),
        ],
        compiler_params=_cp("arbitrary"),
        name="rwkv_bwd" if reverse else "rwkv_fwd",
    )(zall, zall, zall, zall, mu, w0, wup, a0, aup, k_k, k_a, r_k)


def _branches_kernel(saf_ref, sab_ref, su_ref, sd_ref, rf_ref, rb_ref, rg_ref, gup_ref, lw_ref, lb_ref,
                     of_ref, ob_ref, og_ref, ng_ref, ga_ref, gb_ref, gc_ref, wglu_ref, wrw_ref, wgl_ref, m_ref):
    za = _gelu_tanh(saf_ref[...] + sab_ref[...] + sd_ref[...] * su_ref[...].astype(F32)).astype(BF16)
    hid = _dot(za, wglu_ref[...])
    m = _sigmoid(ga_ref[...].astype(F32)) * (hid[:, :D_MODEL] * _sigmoid(hid[:, D_MODEL:]))
    ones64 = _head_ones(RW_HEAD)
    gate = _dot(_sigmoid(rg_ref[...]).astype(BF16), gup_ref[...])
    zb = []
    for p in range(RW_PAIRS):
        ls = slice(LANES * p, LANES * (p + 1))
        y = rf_ref[:, ls] + rb_ref[:, ls]
        mean = _head_sum(y, ones64, True) * (1.0 / RW_HEAD)
        yc = y - mean
        var = _head_sum(yc * yc, ones64, True) * (1.0 / RW_HEAD)
        yn = yc * lax.rsqrt(var + RW_GN_EPS) * lw_ref[:, ls] + lb_ref[:, ls]
        zb.append((yn * gate[:, ls]).astype(BF16))
    m = m + _sigmoid(gb_ref[...].astype(F32)) * _dot(jnp.concatenate(zb, axis=1), wrw_ref[...])
    zc = []
    for h in range(GLA_HEADS):
        vs = slice(GLA_DV * h, GLA_DV * (h + 1))
        o = of_ref[:, vs].astype(F32) + ob_ref[:, vs].astype(F32)
        on = o * lax.rsqrt(jnp.mean(o * o, axis=-1, keepdims=True) + NORM_EPS) * ng_ref[:, vs]
        zc.append((on * _silu(og_ref[:, vs].astype(F32))).astype(BF16))
    m = m + _sigmoid(gc_ref[...].astype(F32)) * _dot(jnp.concatenate(zc, axis=1), wgl_ref[...])
    m_ref[...] = m.astype(BF16)


def _branches(ya, yb, o_dir, pa_in, zall, prm, row0, nc_blocks, nl_blocks):
    s5_d, g_up, ln_w, ln_b, norm_g, w_glu, w_rw, w_gl = prm
    bm = RW_BLOCK
    bsz = yb[0].shape[0]
    n = zall.shape[0] - row0 * bm
    rows = lambda w, blk=0: pl.BlockSpec((bm, w), lambda i: (i + row0, blk))

    def y_index(i):
        i = i + row0
        t = i - bsz * nc_blocks
        return (jnp.where(t < 0, i // nc_blocks, t // nl_blocks),
                jnp.where(t < 0, i % nc_blocks, nc_blocks + t % nl_blocks), 0)

    y_spec = pl.BlockSpec((None, bm, RW_WIDTH), y_index)
    vec = pl.BlockSpec((1, 1024), lambda i: (0, 0))
    const = lambda a: pl.BlockSpec(a.shape, lambda i: (0,) * a.ndim, pipeline_mode=pl.Buffered(1))
    return pl.pallas_call(
        _branches_kernel,
        out_shape=jax.ShapeDtypeStruct((n, D_MODEL), BF16),
        grid=(n // bm,),
        in_specs=[rows(1024), rows(1024), rows(1024, 0), vec,
                  y_spec, y_spec, rows(LANES, RW_Z // LANES), const(g_up), vec, vec,
                  rows(1024), rows(1024), rows(1024, 1), vec,
                  rows(D_MODEL, 1), rows(D_MODEL, 2), rows(D_MODEL, 3),
                  const(w_glu), const(w_rw), const(w_gl)],
        out_specs=pl.BlockSpec((bm, D_MODEL), lambda i: (i, 0)),
        compiler_params=_cp("parallel"),
        name="branches",
    )(ya[0], ya[1], pa_in, s5_d, yb[0], yb[1], zall, g_up, ln_w, ln_b,
      o_dir[0], o_dir[1], pa_in, norm_g, pa_in, pa_in, pa_in, w_glu, w_rw, w_gl)


def _out_proj_kernel(m_ref, w_ref, x_ref, gate_ref, o_ref):
    o_ref[...] = x_ref[...] + gate_ref[...] * _dot(m_ref[...], w_ref[...])


def _out_proj(m, w_out, x, mod, layer, bm, row0, nc, seq):
    n, d = m.shape
    bn = 1024
    mod_index = _mod_spec(layer, 2, bm, nc, seq).index_map
    return pl.pallas_call(
        _out_proj_kernel,
        out_shape=jax.ShapeDtypeStruct((n, d), F32),
        grid=(n // bm, d // bn),
        in_specs=[pl.BlockSpec((bm, d), lambda i, j: (i, 0)),
                  pl.BlockSpec((d, bn), lambda i, j: (0, j)),
                  pl.BlockSpec((bm, bn), lambda i, j: (i + row0, j)),
                  pl.BlockSpec((None, None, None, 1, bn), lambda i, j: mod_index(i + row0)[:4] + (j,))],
        out_specs=pl.BlockSpec((bm, bn), lambda i, j: (i, j)),
        compiler_params=_cp("parallel", "parallel"),
        name="out_proj",
    )(m, w_out, x, mod)


MOE_TM = 256
MOE_ROW_TILES = D_MODEL // (2 * LANES)
META_E1, META_E2, META_R1, META_R2, META_W1, META_W2 = range(6)


def _router_kernel(x_ref, g_ref, sh_ref, sc_ref, wr_hi_ref, wr_lo_ref, br_ref, v_ref, meta_ref, meta_t_ref, cnt_ref,
                   base_s):
    @pl.when(pl.program_id(0) == 0)
    def _():
        base_s[...] = jnp.zeros_like(base_s)

    x = x_ref[...]
    y = x * lax.rsqrt(jnp.mean(x * x, axis=-1, keepdims=True) + NORM_EPS) * g_ref[...]
    t = y * (1.0 + sc_ref[...]) + sh_ref[...]
    word = _pack_halves(t)
    for s in range(MOE_ROW_TILES):
        v_ref[pl.ds(s, x.shape[0], stride=MOE_ROW_TILES), :] = word[:, LANES * s:LANES * (s + 1)]
    t_hi, t_lo = _split_hi_lo(t)
    logits = (_dot(t_hi, wr_hi_ref[...]) + _dot(t_lo, wr_hi_ref[...]) + _dot(t_hi, wr_lo_ref[...])) + br_ref[...]
    lane = _iota(logits.shape, 1).astype(F32)
    neg = jnp.float32(-jnp.inf)
    big = jnp.float32(LANES)
    l1 = jnp.where(lane < MOE_GROUPS, logits, neg)
    m1 = jnp.max(l1, axis=-1, keepdims=True)
    p_top = 1.0 / jnp.sum(jnp.exp(l1 - m1), axis=-1, keepdims=True)
    grp = jnp.min(jnp.where(l1 == m1, lane, big), axis=-1, keepdims=True)
    lo = MOE_LANE0 + MOE_PER_GROUP * grp
    in_grp = (lane >= lo) & (lane < lo + MOE_PER_GROUP)
    l2 = jnp.where(in_grp, logits, neg)
    v1 = jnp.max(l2, axis=-1, keepdims=True)
    i1 = jnp.min(jnp.where(l2 == v1, lane, big), axis=-1, keepdims=True)
    l3 = jnp.where(lane == i1, neg, l2)
    v2 = jnp.max(l3, axis=-1, keepdims=True)
    i2 = jnp.min(jnp.where(l3 == v2, lane, big), axis=-1, keepdims=True)
    e2 = jnp.exp(v2 - v1)
    w1 = p_top / (1.0 + e2)
    w2 = p_top * e2 / (1.0 + e2)
    pick1 = lane == i1
    pick2 = lane == i2
    chosen = jnp.where(pick1 | pick2, 1.0, 0.0)
    bm = x.shape[0]
    earlier = (_iota((bm, bm), 1) < _iota((bm, bm), 0)).astype(BF16)
    before = _dot(earlier, chosen.astype(BF16)) + base_s[...]
    r1 = jnp.sum(jnp.where(pick1, before, 0.0), axis=-1, keepdims=True)
    r2 = jnp.sum(jnp.where(pick2, before, 0.0), axis=-1, keepdims=True)
    base_s[...] += jnp.sum(chosen, axis=0, keepdims=True)
    cnt_ref[...] = base_s[...]
    meta = jnp.zeros_like(logits)
    for slot, val in ((META_E1, i1 - MOE_LANE0), (META_E2, i2 - MOE_LANE0), (META_R1, r1), (META_R2, r2),
                      (META_W1, w1), (META_W2, w2)):
        meta = jnp.where(lane == slot, val, meta)
    meta_ref[...] = meta
    meta_t_ref[...] = meta.T[:8]


def _router(x, g, mod, layer, wr_hi, wr_lo, br, bm, nc, seq):
    n, d = x.shape
    return pl.pallas_call(
        _router_kernel,
        out_shape=(jax.ShapeDtypeStruct((n * MOE_ROW_TILES, LANES), jnp.uint32), jax.ShapeDtypeStruct((n, LANES), F32),
                   jax.ShapeDtypeStruct((8, n), F32), jax.ShapeDtypeStruct((1, LANES), F32)),
        grid=(n // bm,),
        in_specs=[
            pl.BlockSpec((bm, d), lambda i: (i, 0)),
            pl.BlockSpec((None, 1, d), lambda i: (layer, 0, 0)),
            _mod_spec(layer, 3, bm, nc, seq), _mod_spec(layer, 4, bm, nc, seq),
            pl.BlockSpec((d, LANES), lambda i: (0, 0)), pl.BlockSpec((d, LANES), lambda i: (0, 0)),
            pl.BlockSpec((1, LANES), lambda i: (0, 0)),
        ],
        out_specs=(pl.BlockSpec((bm * MOE_ROW_TILES, LANES), lambda i: (i, 0)),
                   pl.BlockSpec((bm, LANES), lambda i: (i, 0)),
                   pl.BlockSpec((8, bm), lambda i: (0, i)), pl.BlockSpec((1, LANES), lambda i: (0, 0))),
        scratch_shapes=[pltpu.VMEM((1, LANES), F32)],
        compiler_params=_cp("arbitrary"),
        name="moe_router",
    )(x, g, mod, mod, wr_hi, wr_lo, br)


def _moe_plan(meta_t, cnt):
    tm = MOE_TM
    n_tok = meta_t.shape[1]
    counts = cnt[0, MOE_LANE0:MOE_LANE0 + MOE_EXPERTS].astype(jnp.int32)
    seg = ((counts + tm - 1) // tm) * tm
    ends = jnp.cumsum(seg)
    off = ends - seg
    rec = meta_t[:4].astype(jnp.int32)
    first_row = jnp.sum(jnp.where(rec[:2, None, :] == jnp.arange(MOE_EXPERTS)[None, :, None], off[None, :, None], 0),
                        axis=1)
    dest = first_row + rec[2:4]
    n_rows = 2 * n_tok + MOE_EXPERTS * tm
    n_tiles = n_rows // tm
    tile_e = jnp.sum(((jnp.arange(n_tiles) * tm)[:, None] >= ends[None, :]).astype(jnp.int32), axis=1)
    tile_e = jnp.minimum(tile_e, MOE_EXPERTS - 1)
    tok = jnp.broadcast_to(jnp.arange(n_tok, dtype=jnp.int32), (2, n_tok))
    src = jnp.zeros((n_rows,), jnp.int32).at[dest.reshape(-1)].set(tok.reshape(-1))
    return dest[0], dest[1], src, tile_e, (ends[-1] // tm).reshape(1)


def _row_copy(src_hbm, row, dst, slot, r, sem):
    return pltpu.make_async_copy(src_hbm.at[pl.ds(row, 1)], dst.at[slot, pl.ds(r, 1)], sem)


def _experts_kernel(src_ref, te_ref, nu_ref, v_hbm, wg_ref, wu_ref, wd_ref, y_ref, xbuf, sem):
    del te_ref
    i = pl.program_id(0)
    n_used = nu_ref[0]
    slot = i % 2

    nt = MOE_ROW_TILES

    def token_copy(tok, into, r):
        return pltpu.make_async_copy(v_hbm.at[pl.ds(tok * nt, nt)], xbuf.at[into, pl.ds(r * nt, nt)], sem.at[into])

    def gather(tile, into):
        for r in range(MOE_TM):
            token_copy(src_ref[tile * MOE_TM + r], into, r).start(priority=r % 2)

    def expert():
        for r in range(MOE_TM):
            token_copy(0, slot, r).wait()
        word = jnp.concatenate([xbuf[slot, pl.ds(s, MOE_TM, stride=nt), :] for s in range(nt)], axis=1)
        t = jnp.concatenate(_unpack_halves(word), axis=1).astype(BF16)
        hid = _silu(_dot(t, wg_ref[...].astype(BF16))) * _dot(t, wu_ref[...].astype(BF16))
        y_ref[...] = _pack_halves(_dot(hid.astype(BF16), wd_ref[...].astype(BF16)))

    @pl.when((i == 0) & (n_used > 0))
    def _():
        gather(0, 0)

    @pl.when(i + 1 < n_used)
    def _():
        gather(i + 1, 1 - slot)
        expert()

    @pl.when(i + 1 == n_used)
    def _():
        expert()

    @pl.when(i >= n_used)
    def _():
        y_ref[...] = jnp.zeros_like(y_ref)


def _experts(v, src, tile_e, n_used, layer, w_gate, w_up, w_down):
    d = w_gate.shape[2]
    hdim = w_gate.shape[3]
    n_rows = src.shape[0]
    tm = MOE_TM
    by_expert = lambda i, src_r, te_r, nu_r: (layer, te_r[i], 0, 0)
    return pl.pallas_call(
        _experts_kernel,
        out_shape=jax.ShapeDtypeStruct((n_rows, d // 2), jnp.uint32),
        grid_spec=pltpu.PrefetchScalarGridSpec(
            num_scalar_prefetch=3,
            grid=(n_rows // tm,),
            in_specs=[
                pl.BlockSpec(memory_space=pl.ANY),
                pl.BlockSpec((None, None, d, hdim), by_expert),
                pl.BlockSpec((None, None, d, hdim), by_expert),
                pl.BlockSpec((None, None, hdim, d), by_expert),
            ],
            out_specs=pl.BlockSpec((tm, d // 2), lambda i, *_: (i, 0)),
            scratch_shapes=[pltpu.VMEM((2, tm * MOE_ROW_TILES, LANES), jnp.uint32), pltpu.SemaphoreType.DMA((2,))],
        ),
        compiler_params=_cp("arbitrary"),
        name="moe_experts",
    )(src, tile_e, n_used, v, w_gate, w_up, w_down)


def _combine_kernel(d1_ref, d2_ref, y_hbm, meta_ref, x_ref, gate_ref, o_ref, buf1, buf2, sem):
    i = pl.program_id(0)
    bm = x_ref.shape[0]
    slot = i % 2

    def gather(tile, into):
        for r in range(bm):
            t = tile * bm + r
            _row_copy(y_hbm, d1_ref[t], buf1, into, r, sem.at[0, into]).start(priority=0)
            _row_copy(y_hbm, d2_ref[t], buf2, into, r, sem.at[1, into]).start(priority=1)

    @pl.when(i == 0)
    def _():
        gather(0, 0)

    @pl.when(i + 1 < pl.num_programs(0))
    def _():
        gather(i + 1, 1 - slot)

    for r in range(bm):
        _row_copy(y_hbm, 0, buf1, slot, r, sem.at[0, slot]).wait()
        _row_copy(y_hbm, 0, buf2, slot, r, sem.at[1, slot]).wait()
    meta = meta_ref[...]
    lane = _iota(meta.shape, 1)
    w1 = jnp.sum(jnp.where(lane == META_W1, meta, 0.0), axis=-1, keepdims=True)
    w2 = jnp.sum(jnp.where(lane == META_W2, meta, 0.0), axis=-1, keepdims=True)
    lo1, hi1 = _unpack_halves(buf1[slot])
    lo2, hi2 = _unpack_halves(buf2[slot])
    moe = jnp.concatenate([w1 * lo1 + w2 * lo2, w1 * hi1 + w2 * hi2], axis=1)
    o_ref[...] = x_ref[...] + gate_ref[...] * moe


def _combine(y, d1, d2, meta, x, mod, layer, bm, nc, seq):
    n, d = x.shape
    at_tile = lambda i, *_: (i, 0)
    mod_index = _mod_spec(layer, 5, bm, nc, seq).index_map
    return pl.pallas_call(
        _combine_kernel,
        out_shape=jax.ShapeDtypeStruct((n, d), F32),
        grid_spec=pltpu.PrefetchScalarGridSpec(
            num_scalar_prefetch=2,
            grid=(n // bm,),
            in_specs=[
                pl.BlockSpec(memory_space=pl.ANY),
                pl.BlockSpec((bm, LANES), at_tile),
                pl.BlockSpec((bm, d), at_tile),
                pl.BlockSpec((None, None, None, 1, d), lambda i, *_: mod_index(i)),
            ],
            out_specs=pl.BlockSpec((bm, d), at_tile),
            scratch_shapes=[pltpu.VMEM((2, bm, d // 2), jnp.uint32), pltpu.VMEM((2, bm, d // 2), jnp.uint32),
                            pltpu.SemaphoreType.DMA((2, 2))],
        ),
        compiler_params=_cp("arbitrary"),
        name="moe_combine",
    )(d1, d2, y, meta, x, mod)


def _final_norm_kernel(x_ref, g_ref, o_ref):
    x = x_ref[...]
    o_ref[...] = x * lax.rsqrt(jnp.mean(x * x, axis=-1, keepdims=True) + NORM_EPS) * g_ref[...]


def _final_norm(x, g, bm, row0):
    n, d = x.shape
    n -= row0 * bm
    return pl.pallas_call(
        _final_norm_kernel,
        out_shape=jax.ShapeDtypeStruct((n, d), F32),
        grid=(n // bm,),
        in_specs=[pl.BlockSpec((bm, d), lambda i: (i + row0, 0)), pl.BlockSpec((1, d), lambda i: (0, 0))],
        out_specs=pl.BlockSpec((bm, d), lambda i: (i, 0)),
        compiler_params=_cp("parallel"),
        name="final_norm",
    )(x, g)


_COL = dict(s5=0, rw=S5_WIDTH, rg=S5_WIDTH + 3 * RW_WIDTH + 128, q=4352, k=4864, v=5376, gk=6400, og=6416, gates=7440)


def _pad_rows(w, rows):
    return jnp.pad(w, ((0, rows - w.shape[0]), (0, 0)))


def _layer(i, last, xa, mod, bsz, seq, ctx_len, p):
    nc = bsz * ctx_len
    rows = xa.shape[0]
    bm = 512
    msel = dict(nc=nc, seq=seq)
    u = _normmod(xa, p['g_norm1'].reshape(-1, 1, D_MODEL), mod, i, (0, 1), bm, **msel)

    w_in = p['w_in'][i]
    col = lambda a, w: w_in[:, a:a + w].astype(BF16)
    bm_in = 1088 if rows % 1088 == 0 else bm
    w_a = jnp.concatenate([col(0, S5_WIDTH), col(_COL['og'], GLA_V), col(_COL['gates'], 3 * D_MODEL)], axis=1)
    w_z = jnp.concatenate([col(_COL['rw'], 3 * RW_WIDTH + 128), col(_COL['rg'], 128)], axis=1)
    w_qk = col(_COL['q'], 2 * GLA_QK)
    w_v = col(_COL['v'], GLA_V)
    w_gk = jnp.pad(col(_COL['gk'], 16), ((0, 0), (0, LANES - 16)))
    pa_in = _mm(u, w_a, bm_in, 1024, BF16)
    zall = _mm(u, w_z, bm_in, 1664)
    qk = _mm(u, w_qk, bm_in, 1024, BF16)
    vv = _mm(u, w_v, bm_in, 1024, BF16)
    gk = _mm(u, w_gk, bm_in, LANES, BF16)

    ya = []
    for d in range(2):
        tables = _s5_tables(*(p[k][i, d] for k in ('s5_a_re', 's5_a_im', 's5_log_dt', 's5_b_re', 's5_b_im',
                                                   's5_c_re', 's5_c_im')))
        ya.append(_s5_scan(pa_in, tables, bsz, ctx_len // S5_BLOCK, seq // S5_BLOCK, bool(d)))

    yb = []
    for d in range(2):
        prm = (
            jnp.pad(p['rw_mu'][i, d], (0, RW_Z - p['rw_mu'].shape[-1])).reshape(1, RW_Z),
            p['rw_w0'][i, d].reshape(1, -1),
            _pad_rows(p['rw_w_up'][i, d], LANES).astype(BF16),
            p['rw_a0'][i, d].reshape(1, -1),
            jnp.pad(p['rw_a_up'][i, d], ((RW_HEAD, 0), (0, 0))).astype(BF16),
            p['rw_k_k'][i].reshape(1, -1), p['rw_k_a'][i].reshape(1, -1), p['rw_r_k'][i].reshape(1, -1),
        )
        yb.append(_rwkv_direction(zall, prm, bsz, ctx_len // RW_BLOCK, seq // RW_BLOCK, bool(d)))

    gk_up = jnp.pad(p['gl_gk_up'][i], ((0, 0), (0, LANES - 16), (0, 0))).astype(BF16)
    gk_b = p['gl_gk_b'][i].reshape(2, 1, GLA_QK)
    s0 = jnp.zeros((2, bsz, GLA_HEADS, GLA_DV, GLA_DK), F32)
    oc, s_ctx = _gla(_to_chunk_major(qk[:nc], bsz), _to_chunk_major(vv[:nc], bsz), _to_chunk_major(gk[:nc], bsz),
                     gk_up, gk_b, s0)
    rows_l = seq // GRID_W
    lat = lambda t: t[nc:].reshape(bsz, rows_l, GRID_W * t.shape[1])
    ol, _ = _gla(lat(qk), lat(vv), lat(gk), gk_up, gk_b, s_ctx)
    o_dir = [jnp.concatenate([_from_chunk_major(oc[d], GLA_V).reshape(nc, GLA_V),
                              ol[d].reshape(bsz * seq, GLA_V)], axis=0) for d in range(2)]

    row0 = (nc // bm) if last else 0
    branch_prm = (p['s5_d'][i].reshape(1, -1), p['rw_g_up'][i].astype(BF16), p['rw_ln_w'][i].reshape(1, -1),
                  p['rw_ln_b'][i].reshape(1, -1), p['gl_norm_g'][i].reshape(1, -1), p['s5_w_glu'][i].astype(BF16),
                  p['rw_w_proj'][i].astype(BF16), p['gl_w_proj'][i].astype(BF16))
    merged = _branches(ya, yb, o_dir, pa_in, zall, branch_prm, row0 * (bm // RW_BLOCK), ctx_len // RW_BLOCK,
                       seq // RW_BLOCK)
    xm = _out_proj(merged, p['w_out'][i].astype(BF16), xa, mod, i, bm, row0, nc, seq)
    if last:
        msel = dict(nc=0, seq=seq)

    wr = jnp.pad(jnp.concatenate([p['moe_wg1'][i], p['moe_wg2'][i]], axis=1), ((0, 0), (0, LANES - 36)))
    wr_hi = wr.astype(BF16)
    wr_lo = (wr - wr_hi.astype(F32)).astype(BF16)
    br = jnp.pad(jnp.concatenate([p['moe_bg1'][i], p['moe_bg2'][i]]), (0, LANES - 36)).reshape(1, LANES)
    vmoe, meta, meta_t, cnt = _router(xm, p['g_norm2'].reshape(-1, 1, D_MODEL), mod, i, wr_hi, wr_lo, br, bm, **msel)
    d1, d2, src, tile_e, n_used = _moe_plan(meta_t, cnt)
    y_sorted = _experts(vmoe, src, tile_e, n_used, i, p['moe_w_gate'], p['moe_w_up'], p['moe_w_down'])
    return _combine(y_sorted, d1, d2, meta, xm, mod, i, bm // 2, **msel)


def kernel(x, c, ctx, c_ctx, w_mod, b_mod, g_norm1, g_norm2, w_in, s5_a_re, s5_a_im, s5_log_dt, s5_b_re, s5_b_im,
           s5_c_re, s5_c_im, s5_d, s5_w_glu, rw_mu, rw_w0, rw_w_up, rw_a0, rw_a_up, rw_k_k, rw_k_a, rw_r_k, rw_g_up,
           rw_ln_w, rw_ln_b, rw_w_proj, gl_gk_up, gl_gk_b, gl_norm_g, gl_w_proj, w_out, moe_wg1, moe_bg1, moe_wg2,
           moe_bg2, moe_w_gate, moe_w_up, moe_w_down, g_final):
    p = dict(g_norm1=g_norm1, g_norm2=g_norm2, w_in=w_in, s5_a_re=s5_a_re, s5_a_im=s5_a_im, s5_log_dt=s5_log_dt,
             s5_b_re=s5_b_re, s5_b_im=s5_b_im, s5_c_re=s5_c_re, s5_c_im=s5_c_im, s5_d=s5_d, s5_w_glu=s5_w_glu,
             rw_mu=rw_mu, rw_w0=rw_w0, rw_w_up=rw_w_up, rw_a0=rw_a0, rw_a_up=rw_a_up, rw_k_k=rw_k_k, rw_k_a=rw_k_a,
             rw_r_k=rw_r_k.reshape(rw_r_k.shape[0], -1), rw_g_up=rw_g_up, rw_ln_w=rw_ln_w, rw_ln_b=rw_ln_b,
             rw_w_proj=rw_w_proj, gl_gk_up=gl_gk_up, gl_gk_b=gl_gk_b, gl_norm_g=gl_norm_g, gl_w_proj=gl_w_proj,
             w_out=w_out, moe_wg1=moe_wg1, moe_bg1=moe_bg1, moe_wg2=moe_wg2, moe_bg2=moe_bg2, moe_w_gate=moe_w_gate,
             moe_w_up=moe_w_up, moe_w_down=moe_w_down)
    bsz, seq, d = x.shape
    ctx_len = ctx.shape[1]
    depth = w_mod.shape[0]
    cc = jnp.concatenate([c, c_ctx[None], jnp.zeros((8 - bsz - 1, d), F32)], axis=0)
    mod = _adaln(cc, w_mod, b_mod).reshape(depth, 8, 6, 1, d)
    xa = jnp.concatenate([ctx.reshape(bsz * ctx_len, d), x.reshape(bsz * seq, d)], axis=0)
    for i in range(depth):
        xa = _layer(i, i == depth - 1, xa, mod, bsz, seq, ctx_len, p)
    out = _final_norm(xa, g_final.reshape(1, d), 512, 0)
    return out.reshape(bsz, seq, d)
```

```python
import functools
import math

import jax
import jax.numpy as jnp
from jax import lax
from jax.experimental import pallas as pl
from jax.experimental.pallas import tpu as pltpu

F32 = jnp.float32
BF16 = jnp.bfloat16

D_MODEL = 2048
GRID_W = 64
NORM_EPS = 1e-6

S5_WIDTH = 1024
S5_GROUP = 16
S5_GROUPS = 64
S5_STATE = 64
S5_MAX_RE = -1e-4
S5_TILE = 16
S5_PAIRS = S5_GROUPS // 2

RW_WIDTH = 1024
RW_HEAD = 64
RW_DECAY_SCALE = 0.606531
RW_GN_EPS = 64e-5
RW_BLOCK = 256
RW_CHUNK = 16
RW_PAIRS = RW_WIDTH // 128
RW_Z = 3 * RW_WIDTH + 128

GLA_HEADS = 4
GLA_DK = 128
GLA_DV = 256
GLA_QK = 512
GLA_V = 1024
GLA_TAU = 16.0
GLA_CHUNK = 64

MOE_GROUPS = 4
MOE_PER_GROUP = 8
MOE_EXPERTS = 32
MOE_HIDDEN = 256
MOE_LANE0 = MOE_GROUPS

LANES = 128
VMEM_LIMIT = 56 * 1024 * 1024


def _cp(*sem):
    return pltpu.CompilerParams(dimension_semantics=sem, vmem_limit_bytes=VMEM_LIMIT)


def _dot(a, b):
    return jnp.dot(a, b, preferred_element_type=F32)


def _dot_nt(a, b):
    return lax.dot_general(a, b, (((1,), (1,)), ((), ())), preferred_element_type=F32)


def _dot_tn(a, b):
    return lax.dot_general(a, b, (((0,), (0,)), ((), ())), preferred_element_type=F32)


def _sigmoid(x):
    return 1.0 / (1.0 + jnp.exp(-x))


def _silu(x):
    return x * _sigmoid(x)


def _gelu_tanh(x):
    return 0.5 * x * (1.0 + jnp.tanh(math.sqrt(2.0 / math.pi) * (x + 0.044715 * (x * x * x))))


def _split_hi_lo(x):
    hi = x.astype(BF16)
    lo = (x - hi.astype(F32)).astype(BF16)
    return hi, lo


def _pack_halves(x):
    bits = lax.bitcast_convert_type(x.astype(BF16).astype(F32), jnp.uint32)
    n = x.shape[1] // 2
    return (bits[:, n:] & jnp.uint32(0xFFFF0000)) | (bits[:, :n] >> 16)


def _unpack_halves(word):
    return (lax.bitcast_convert_type(word << 16, F32),
            lax.bitcast_convert_type(word & jnp.uint32(0xFFFF0000), F32))


def _iota(shape, dim):
    return lax.broadcasted_iota(jnp.int32, shape, dim)


def _head_ones(width):
    return (_iota((LANES, LANES), 0) // width == _iota((LANES, LANES), 1) // width).astype(BF16)


def _head_sum(x, ones, exact):
    if exact:
        hi, lo = _split_hi_lo(x)
        return _dot(hi, ones) + _dot(lo, ones)
    return _dot(x.astype(BF16), ones)


def _adaln_kernel(c_ref, w_ref, b_ref, o_ref):
    c = c_ref[...]
    o_ref[...] = _dot(_silu(c).astype(BF16), w_ref[...].astype(BF16)) + b_ref[...]


def _adaln(cc, w_mod, b_mod):
    depth, d, n = w_mod.shape
    bn = 1536
    return pl.pallas_call(
        _adaln_kernel,
        out_shape=jax.ShapeDtypeStruct((depth, 8, n), F32),
        grid=(depth, n // bn),
        in_specs=[
            pl.BlockSpec((8, d), lambda l, j: (0, 0)),
            pl.BlockSpec((None, d, bn), lambda l, j: (l, 0, j)),
            pl.BlockSpec((None, 1, bn), lambda l, j: (l, 0, j)),
        ],
        out_specs=pl.BlockSpec((None, 8, bn), lambda l, j: (l, 0, j)),
        compiler_params=_cp("parallel", "parallel"),
        name="adaln",
    )(cc, w_mod, b_mod.reshape(depth, 1, n))


def _mod_spec(layer, part, bm, nc, seq):
    def index(i, *_):
        r0 = i * bm
        return (layer, jnp.where(r0 < nc, 2, (r0 - nc) // seq), part, 0, 0)
    return pl.BlockSpec((None, None, None, 1, D_MODEL), index)


def _normmod_kernel(x_ref, g_ref, sh_ref, sc_ref, o_ref):
    x = x_ref[...]
    y = x * lax.rsqrt(jnp.mean(x * x, axis=-1, keepdims=True) + NORM_EPS) * g_ref[...]
    o_ref[...] = (y * (1.0 + sc_ref[...]) + sh_ref[...]).astype(o_ref.dtype)


def _normmod(x, g, mod, layer, parts, bm, nc, seq):
    n, d = x.shape
    return pl.pallas_call(
        _normmod_kernel,
        out_shape=jax.ShapeDtypeStruct((n, d), BF16),
        grid=(n // bm,),
        in_specs=[
            pl.BlockSpec((bm, d), lambda i: (i, 0)),
            pl.BlockSpec((None, 1, d), lambda i: (layer, 0, 0)),
            _mod_spec(layer, parts[0], bm, nc, seq),
            _mod_spec(layer, parts[1], bm, nc, seq),
        ],
        out_specs=pl.BlockSpec((bm, d), lambda i: (i, 0)),
        compiler_params=_cp("parallel"),
        name="normmod",
    )(x, g, mod, mod)


def _mm_kernel(x_ref, w_ref, o_ref):
    o_ref[...] = _dot(x_ref[...], w_ref[...]).astype(o_ref.dtype)


def _mm(x, w, bm, bn, out_dtype=F32):
    m, k = x.shape
    n = w.shape[1]
    return pl.pallas_call(
        _mm_kernel,
        out_shape=jax.ShapeDtypeStruct((m, n), out_dtype),
        grid=(m // bm, n // bn),
        in_specs=[pl.BlockSpec((bm, k), lambda i, j: (i, 0)), pl.BlockSpec((k, bn), lambda i, j: (0, j))],
        out_specs=pl.BlockSpec((bm, bn), lambda i, j: (i, j)),
        compiler_params=_cp("parallel", "parallel"),
        name="mm",
    )(x, w)


def _seq_blocks(bsz, nc_blocks, nl_blocks, reverse):
    def local(s):
        lat = nc_blocks + ((nl_blocks - 1 - (s - nc_blocks)) if reverse else (s - nc_blocks))
        ctx = (nc_blocks - 1 - s) if reverse else s
        return jnp.where(s < nc_blocks, ctx, lat)

    def block(b, s):
        i = local(s)
        return jnp.where(i < nc_blocks, b * nc_blocks + i, bsz * nc_blocks + b * nl_blocks + i - nc_blocks)

    return local, block


S5_PACK = 8
S5_PACKS = S5_GROUPS // S5_PACK
S5_PLANE = S5_PACK * S5_STATE
S5_BLOCK = 256
S5_ROWS = 24


def _s5_tables(a_re, a_im, log_dt, b_re, b_im, c_re, c_im):
    lam = lax.complex(jnp.minimum(a_re, S5_MAX_RE), a_im)
    ldt = lam * jnp.exp(log_dt)[:, None]
    lam_bar = jnp.exp(ldt)
    b_bar = ((lam_bar - 1.0) / lam)[..., None] * lax.complex(b_re, b_im)
    c_mat = lax.complex(c_re, c_im)
    eye = jnp.eye(S5_PACK, dtype=F32)

    def block_diag(t):
        k, g, a, b = t.shape
        return (t[:, :, :, None, :] * eye[None, :, None, :, None]).reshape(k, g * a, g * b)

    b_t = jnp.transpose(b_bar, (0, 2, 1)).reshape(S5_PACKS, S5_PACK, S5_GROUP, S5_STATE)
    bblk = jnp.concatenate([block_diag(b_t.real), block_diag(b_t.imag)], axis=2)
    c_t = jnp.transpose(c_mat, (0, 2, 1)).reshape(S5_PACKS, S5_PACK, S5_STATE, S5_GROUP)
    cblk = jnp.concatenate([block_diag(c_t.real), block_diag(-c_t.imag)], axis=1)
    expo = jnp.concatenate([jnp.arange(1, S5_TILE + 1, dtype=F32), jnp.asarray([32.0, 64.0, 128.0], F32),
                            jnp.zeros((S5_ROWS - S5_TILE - 3,), F32)])
    pw = jnp.exp(ldt[None] * expo[:, None, None]).reshape(S5_ROWS, S5_PACKS, S5_PLANE)
    pw = jnp.transpose(pw, (1, 0, 2))
    return bblk.astype(BF16), cblk.astype(BF16), pw.real, pw.imag


def _s5_scan_kernel(u_ref, bblk_ref, cblk_ref, pre_ref, pim_ref, y_ref, car_ref, up_s, h2_s, hb2_s, yp_s, *, reverse):
    t = S5_TILE
    n_t = S5_BLOCK // t
    pn = S5_PLANE

    @pl.when(pl.program_id(1) == 0)
    def _():
        car_ref[...] = jnp.zeros_like(car_ref)

    ra = _iota((S5_BLOCK, S5_BLOCK), 0)
    cb = _iota((S5_BLOCK, S5_BLOCK), 1)
    perm = ((ra // t == cb % t) & (ra % t == cb // t)).astype(BF16)
    up = _dot(perm, u_ref[...].astype(BF16)).astype(BF16)
    for pk in range(S5_PACKS):
        up_s[pk] = up[:, LANES * pk:LANES * (pk + 1)]
    order = list(range(t - 1, -1, -1)) if reverse else list(range(t))
    rowj = _iota((n_t, 1), 0)

    def states(pk, bu, h_s, hb_s):
        pre = pre_ref[pk]
        pim = pim_ref[pk]
        l_re, l_im = pre[0:1], pim[0:1]
        h_re = h_im = None
        for n, s in enumerate(order):
            rows = slice(t * s, t * (s + 1))
            b_re, b_im = bu[rows, :pn], bu[rows, pn:]
            if n == 0:
                h_re, h_im = b_re, b_im
            else:
                h_re, h_im = l_re * h_re - l_im * h_im + b_re, l_re * h_im + l_im * h_re + b_im
            h_s[rows, :pn] = h_re
            h_s[rows, pn:] = h_im
        c_re, c_im = car_ref[pk, 0:1, :pn], car_ref[pk, 0:1, pn:]
        first = rowj == (n_t - 1 if reverse else 0)
        g_re, g_im = pre[t - 1:t], pim[t - 1:t]
        e_re = h_re + jnp.where(first, g_re * c_re - g_im * c_im, 0.0)
        e_im = h_im + jnp.where(first, g_re * c_im + g_im * c_re, 0.0)
        step = 1
        for row in (t - 1, t, t + 1, t + 2):
            if reverse:
                s_re, s_im, ok = pltpu.roll(e_re, n_t - step, 0), pltpu.roll(e_im, n_t - step, 0), rowj < n_t - step
            else:
                s_re, s_im, ok = pltpu.roll(e_re, step, 0), pltpu.roll(e_im, step, 0), rowj >= step
            a_re, a_im = pre[row:row + 1], pim[row:row + 1]
            e_re = e_re + jnp.where(ok, a_re * s_re - a_im * s_im, 0.0)
            e_im = e_im + jnp.where(ok, a_re * s_im + a_im * s_re, 0.0)
            step *= 2
        last = 0 if reverse else n_t - 1
        car_ref[pk, 0:1, :pn] = e_re[last:last + 1]
        car_ref[pk, 0:1, pn:] = e_im[last:last + 1]
        if reverse:
            in_re = jnp.where(first, c_re, pltpu.roll(e_re, n_t - 1, 0))
            in_im = jnp.where(first, c_im, pltpu.roll(e_im, n_t - 1, 0))
        else:
            in_re = jnp.where(first, c_re, pltpu.roll(e_re, 1, 0))
            in_im = jnp.where(first, c_im, pltpu.roll(e_im, 1, 0))
        for n, s in enumerate(order):
            rows = slice(t * s, t * (s + 1))
            a_re, a_im = pre[n:n + 1], pim[n:n + 1]
            hb_s[rows, :pn] = (h_s[rows, :pn] + a_re * in_re - a_im * in_im).astype(BF16)
            hb_s[rows, pn:] = (h_s[rows, pn:] + a_re * in_im + a_im * in_re).astype(BF16)

    def pack_pair(j, carry):
        pks = (2 * j, 2 * j + 1)
        bu = [_dot(up_s[pk], bblk_ref[pk]) for pk in pks]
        for u, pk in enumerate(pks):
            states(pk, bu[u], h2_s.at[u], hb2_s.at[u])
            yp_s[pk] = _dot(hb2_s[u], cblk_ref[pk]).astype(BF16)
        return carry

    lax.fori_loop(0, S5_PACKS // 2, pack_pair, 0)
    for pk in range(S5_PACKS):
        y_ref[:, LANES * pk:LANES * (pk + 1)] = _dot(perm, yp_s[pk])


def _s5_scan(u_src, tables, bsz, nc_blocks, nl_blocks, reverse):
    bblk, cblk, pre, pim = tables
    rows = u_src.shape[0]
    blk = S5_BLOCK
    _, block = _seq_blocks(bsz, nc_blocks, nl_blocks, reverse)
    whole = lambda a: pl.BlockSpec(a.shape, lambda b, s: (0,) * a.ndim)
    return pl.pallas_call(
        functools.partial(_s5_scan_kernel, reverse=reverse),
        out_shape=jax.ShapeDtypeStruct((rows, S5_WIDTH), F32),
        grid=(bsz, nc_blocks + nl_blocks),
        in_specs=[pl.BlockSpec((blk, S5_WIDTH), lambda b, s: (block(b, s), 0)),
                  whole(bblk), whole(cblk), whole(pre), whole(pim)],
        out_specs=pl.BlockSpec((blk, S5_WIDTH), lambda b, s: (block(b, s), 0)),
        scratch_shapes=[
            pltpu.VMEM((S5_PACKS, 8, 2 * S5_PLANE), F32),
            pltpu.VMEM((S5_PACKS, blk, LANES), BF16),
            pltpu.VMEM((2, blk, 2 * S5_PLANE), F32),
            pltpu.VMEM((2, blk, 2 * S5_PLANE), BF16),
            pltpu.VMEM((S5_PACKS, blk, LANES), BF16),
        ],
        compiler_params=_cp("parallel", "arbitrary"),
        name="s5_bwd" if reverse else "s5_fwd",
    )(u_src, bblk, cblk, pre, pim)


def _gla_kernel(q_ref, k_ref, v_ref, gk_ref, up_ref, gb_ref, s0_ref, o_ref, sf_ref, st_ref):
    d = pl.program_id(0)
    c = pl.program_id(1)
    n = pl.num_programs(1)

    @pl.when(c == 0)
    def _():
        st_ref[...] = s0_ref[...]

    cs = GLA_CHUNK
    nb = q_ref.shape[0]
    sign = 1 - 2 * d
    row = _iota((cs, cs), 0)
    col = _iota((cs, cs), 1)
    causal = (row - col) * sign >= 0
    causal_b = causal.astype(BF16)
    rid = _iota((cs, 1), 0)
    bcum, b_mid, b_end = [], [], []
    for b in range(nb):
        x = _dot(gk_ref[b].astype(BF16), up_ref[...]) + gb_ref[...]
        log_a = (jnp.minimum(x, 0.0) - jnp.log(1.0 + jnp.exp(-jnp.abs(x)))) * (1.0 / GLA_TAU)
        la_hi, la_lo = _split_hi_lo(log_a)
        bc = _dot(causal_b, la_hi) + _dot(causal_b, la_lo)
        bcum.append(bc)
        b_mid.append(jnp.sum(jnp.where(rid == cs // 2 - d, bc, 0.0), axis=0, keepdims=True))
        b_end.append(jnp.sum(jnp.where(rid == (cs - 1) * (1 - d), bc, 0.0), axis=0, keepdims=True))
    units = [(b, h) for b in range(nb) for h in range(GLA_HEADS)]
    ks = lambda h: slice(GLA_DK * h, GLA_DK * (h + 1))
    vs = lambda h: slice(GLA_DV * h, GLA_DV * (h + 1))
    s_old = [st_ref[b, h] for b, h in units]
    qh = [q_ref[b, :, ks(h)].astype(F32) * (GLA_DK ** -0.5) for b, h in units]
    kh = [k_ref[b, :, ks(h)].astype(F32) for b, h in units]
    vh = [v_ref[b, :, vs(h)].astype(BF16) for b, h in units]
    bh = [bcum[b][:, ks(h)] for b, h in units]
    mid = [b_mid[b][:, ks(h)] for b, h in units]
    end = [b_end[b][:, ks(h)] for b, h in units]
    scores = [_dot_nt((q * jnp.exp(x - m)).astype(BF16), (k * jnp.exp(m - x)).astype(BF16))
              for q, k, x, m in zip(qh, kh, bh, mid)]
    inter = [_dot_nt((q * jnp.exp(x)).astype(BF16), s.astype(BF16)) for q, x, s in zip(qh, bh, s_old)]
    upd = [_dot_tn(v, (k * jnp.exp(e - x)).astype(BF16)) for v, k, e, x in zip(vh, kh, end, bh)]
    intra = [_dot(jnp.where(causal, sc, 0.0).astype(BF16), v) for sc, v in zip(scores, vh)]
    for u, (b, h) in enumerate(units):
        o_ref[b, :, vs(h)] = (intra[u] + inter[u]).astype(o_ref.dtype)
        st_ref[b, h] = s_old[u] * jnp.exp(end[u]) + upd[u]

    @pl.when(c == n - 1)
    def _():
        sf_ref[...] = st_ref[...]


def _gla(qk, v, gk, gk_up, gk_b, s0):
    bsz, cs, w = v.shape
    n = w // GLA_V

    def chunk(d, c):
        return c + d * (n - 1 - 2 * c)

    state_spec = pl.BlockSpec((None, bsz, GLA_HEADS, GLA_DV, GLA_DK), lambda d, c: (d, 0, 0, 0, 0))
    return pl.pallas_call(
        _gla_kernel,
        out_shape=(jax.ShapeDtypeStruct((2, bsz, cs, w), BF16), jax.ShapeDtypeStruct(s0.shape, F32)),
        grid=(2, n),
        in_specs=[
            pl.BlockSpec((bsz, cs, GLA_QK), lambda d, c: (0, 0, 2 * chunk(d, c))),
            pl.BlockSpec((bsz, cs, GLA_QK), lambda d, c: (0, 0, 2 * chunk(d, c) + 1)),
            pl.BlockSpec((bsz, cs, GLA_V), lambda d, c: (0, 0, chunk(d, c))),
            pl.BlockSpec((bsz, cs, LANES), lambda d, c: (0, 0, chunk(d, c))),
            pl.BlockSpec((None, LANES, GLA_QK), lambda d, c: (d, 0, 0)),
            pl.BlockSpec((None, 1, GLA_QK), lambda d, c: (d, 0, 0)),
            state_spec,
        ],
        out_specs=(pl.BlockSpec((None, bsz, cs, GLA_V), lambda d, c: (d, 0, 0, chunk(d, c))), state_spec),
        scratch_shapes=[pltpu.VMEM((bsz, GLA_HEADS, GLA_DV, GLA_DK), F32)],
        compiler_params=_cp("parallel", "arbitrary"),
        name="gla",
    )(qk, qk, v, gk, gk_up, gk_b, s0)


def _to_chunk_major(t, bsz):
    n = t.shape[0] // bsz
    d = t.shape[1]
    t = t.reshape(bsz, n // GLA_CHUNK, GLA_CHUNK, d)
    return jnp.transpose(t, (0, 2, 1, 3)).reshape(bsz, GLA_CHUNK, (n // GLA_CHUNK) * d)


def _from_chunk_major(t, d):
    lead = t.shape[:-2]
    n = t.shape[-1] // d
    t = t.reshape(lead + (GLA_CHUNK, n, d))
    return jnp.swapaxes(t, -3, -2).reshape(lead + (n * GLA_CHUNK, d))


def _rwkv_kernel(z0_ref, z1_ref, halo0_ref, halo1_ref, mu_ref, w0_ref, wup_ref, a0_ref, aup_ref, kk_ref, ka_ref,
                 rk_ref, y_ref, st_ref, kt_s, bt_s, kq_s, rt_s, v_s, w_s, u_s, ya_s, ab_s, pin_s,
                 *, reverse, nc_blocks):
    step = pl.program_id(0)
    blk = RW_BLOCK
    ch = RW_CHUNK
    n_ch = blk // ch
    n_b = 2
    n_units = n_b * RW_PAIRS

    @pl.when(step == 0)
    def _():
        st_ref[...] = jnp.zeros_like(st_ref)

    rowi = _iota((blk, 1), 0)
    low = _iota((1, LANES), 1) < RW_HEAD
    ones64 = _head_ones(RW_HEAD)
    pos = rowi % ch
    seq_start = (step == 0) | (step == nc_blocks)

    for b, (z_ref, halo_ref) in enumerate(((z0_ref, halo0_ref), (z1_ref, halo1_ref))):
        z = z_ref[...]
        if reverse:
            prev = pltpu.roll(z, blk - 1, 0)
            edge = halo_ref[0:1, :]
            at_edge = rowi == blk - 1
        else:
            prev = pltpu.roll(z, 1, 0)
            edge = halo_ref[7:8, :]
            at_edge = rowi == 0
        prev = jnp.where(at_edge, jnp.where(seq_start, 0.0, edge), prev)
        zs = z + (prev - z) * mu_ref[...]
        r = zs[:, 0:RW_WIDTH]
        k = zs[:, RW_WIDTH:2 * RW_WIDTH]
        v = zs[:, 2 * RW_WIDTH:3 * RW_WIDTH]
        lora = zs[:, 3 * RW_WIDTH:3 * RW_WIDTH + LANES]
        lora_w = jnp.where(low, jnp.tanh(lora), 0.0).astype(BF16)
        lora_a = jnp.where(low, 0.0, lora).astype(BF16)
        logw = -RW_DECAY_SCALE * _sigmoid(w0_ref[...] + _dot(lora_w, wup_ref[...]))
        a = _sigmoid(a0_ref[...] + _dot(lora_a, aup_ref[...]))
        kk = k * kk_ref[...]
        kp = k * (1.0 + (a - 1.0) * ka_ref[...])
        rkb = r * kp * rk_ref[...]

        cl = logw
        sh = 1
        while sh < ch:
            if reverse:
                cl = cl + jnp.where(pos < ch - sh, pltpu.roll(cl, blk - sh, 0), 0.0)
            else:
                cl = cl + jnp.where(pos >= sh, pltpu.roll(cl, sh, 0), 0.0)
            sh *= 2
        p_in = jnp.exp(cl)
        pin_s[b] = p_in
        p_ex = jnp.exp(cl - logw)
        p_inv = jnp.exp(-cl)

        for p in range(RW_PAIRS):
            ls = slice(LANES * p, LANES * (p + 1))
            q = b * RW_PAIRS + p
            kkp = kk[:, ls]
            ssq = _head_sum(kkp * kkp, ones64, False)
            kkn = kkp * (1.0 / jnp.maximum(jnp.sqrt(ssq), 1e-12))
            kt_s[q] = (kkn * p_ex[:, ls]).astype(BF16)
            bt_s[q] = (kkn * a[:, ls] * p_inv[:, ls]).astype(BF16)
            kq_s[q] = (kp[:, ls] * p_inv[:, ls]).astype(BF16)
            rt_s[q] = (r[:, ls] * p_in[:, ls]).astype(BF16)
            v_s[q] = v[:, ls].astype(BF16)
            ya_s[q] = _head_sum(rkb[:, ls], ones64, False) * v[:, ls]

    rr = _iota((blk, blk), 0)
    cc = _iota((blk, blk), 1)
    same = rr // ch == cc // ch
    before = (cc > rr) if reverse else (cc < rr)
    strict = same & before
    incl = same & (before | (rr == cc))
    fold0 = (_iota((blk, LANES), 0) % ch == _iota((blk, LANES), 1)).astype(BF16)
    fold1 = (_iota((blk, LANES), 0) % ch + ch == _iota((blk, LANES), 1)).astype(BF16)
    lane_lo = _iota((1, LANES), 1) < RW_HEAD

    def pair_body(p, carry):
        units = [b * RW_PAIRS + p for b in range(n_b)]
        heads = [(u, hh) for u in range(n_b) for hh in range(2)]
        kt = [kt_s[q] for q in units]
        rt = [rt_s[q] for q in units]
        vv = [v_s[q] for q in units]
        ya = [ya_s[q] for q in units]
        bt = [bt_s[q] for q in units]
        kq = [kq_s[q] for q in units]
        hb = blk // 2
        g = []
        for u, hh in heads:
            mine = lane_lo if hh == 0 else jnp.logical_not(lane_lo)
            zero = jnp.zeros_like(kt[u])
            ktm, rtm = jnp.where(mine, kt[u], zero), jnp.where(mine, rt[u], zero)
            halves = []
            for rows in (slice(0, hb), slice(hb, blk)):
                lhs = jnp.concatenate([ktm[rows], rtm[rows]], axis=0)
                rhs = jnp.concatenate([bt[u][rows], kq[u][rows]], axis=0)
                halves.append(_dot_nt(lhs, rhs))
            g.append(halves)
        zero_q = jnp.zeros((hb, hb), F32)

        def diag2(gh, r0, c0):
            a, b = gh[0][r0:r0 + hb, c0:c0 + hb], gh[1][r0:r0 + hb, c0:c0 + hb]
            return jnp.concatenate([jnp.concatenate([a, zero_q], axis=1), jnp.concatenate([zero_q, b], axis=1)],
                                   axis=0)

        n1 = [jnp.where(strict, -diag2(gh, 0, 0), 0.0).astype(BF16) for gh in g]
        a_kq = [jnp.where(strict, diag2(gh, 0, hb), 0.0).astype(BF16) for gh in g]
        a_rb = [jnp.where(incl, diag2(gh, hb, 0), 0.0).astype(BF16) for gh in g]
        a_rq = [jnp.where(incl, diag2(gh, hb, hb), 0.0).astype(BF16) for gh in g]
        n2 = [_dot(n, n).astype(BF16) for n in n1]
        akv = [_dot(a, vv[u]) for a, (u, _) in zip(a_kq, heads)]
        n4 = [_dot(n, n).astype(BF16) for n in n2]
        y_in = [_dot(a, vv[u]) for a, (u, _) in zip(a_rq, heads)]
        n8 = [_dot(n, n).astype(BF16) for n in n4]
        fold = [_dot(a, fold0 if hh == 0 else fold1) for a, (_, hh) in zip(a_rb, heads)]
        rhs_t = [jnp.concatenate([kt[u].astype(F32), av], axis=1) for av, (u, _) in zip(akv, heads)]
        for nk in (n8, n4, n2, n1):
            rhs_t = [x + _dot(n, x.astype(BF16)) for n, x in zip(nk, rhs_t)]
        for u, q in enumerate(units):
            h0, h1 = 2 * u, 2 * u + 1
            w_s[q] = jnp.where(lane_lo, rhs_t[h0][:, :LANES], rhs_t[h1][:, :LANES]).astype(BF16)
            u_s[q] = jnp.where(lane_lo, rhs_t[h0][:, LANES:], rhs_t[h1][:, LANES:])
            ya_s[q] = ya[u] + jnp.where(lane_lo, y_in[h0], y_in[h1])
            ab_s[q] = (fold[h0] + fold[h1]).astype(BF16)
        return carry

    lax.fori_loop(0, RW_PAIRS, pair_body, 0)

    blockdiag = (_iota((LANES, LANES), 0) // RW_HEAD) == (_iota((LANES, LANES), 1) // RW_HEAD)
    end_row = 0 if reverse else ch - 1

    def chunk_body(i, carry):
        c = (n_ch - 1 - i) if reverse else i
        rows = pl.ds(pl.multiple_of(c * ch, ch), ch)
        s_old = [st_ref[q] for q in range(n_units)]
        m1 = [_dot_nt(jnp.concatenate([w_s[q, rows, :], rt_s[q, rows, :]], axis=0), s_old[q].astype(BF16))
              for q in range(n_units)]
        zc = [-(m1[q][:ch] + u_s[q, rows, :]) for q in range(n_units)]
        upd = []
        for q in range(n_units):
            zv = jnp.concatenate([zc[q].astype(BF16), v_s[q, rows, :]], axis=0)
            bk = jnp.concatenate([bt_s[q, rows, :], kq_s[q, rows, :]], axis=0)
            upd.append(_dot_tn(zv, bk))
        yc = []
        for q in range(n_units):
            z2 = jnp.concatenate([jnp.where(lane_lo, zc[q], 0.0), jnp.where(lane_lo, 0.0, zc[q])], axis=0)
            yc.append(m1[q][ch:] + _dot(ab_s[q, rows, :][:, :2 * ch], z2.astype(BF16)) + ya_s[q, rows, :])
        for q in range(n_units):
            b, p = divmod(q, RW_PAIRS)
            p_end = pin_s[b, rows, LANES * p:LANES * (p + 1)][end_row:end_row + 1]
            st_ref[q] = (s_old[q] + jnp.where(blockdiag, upd[q], 0.0)) * p_end
            y_ref[b, rows, LANES * p:LANES * (p + 1)] = yc[q]
        return carry

    lax.fori_loop(0, n_ch, chunk_body, 0)


def _rwkv_direction(zall, prm, bsz, nc_blocks, nl_blocks, reverse):
    assert bsz == 2
    mu, w0, wup, a0, aup, k_k, k_a, r_k = prm
    rows = zall.shape[0]
    blk = RW_BLOCK
    steps = nc_blocks + nl_blocks
    n_blocks = rows // blk

    def local(s):
        lat = nc_blocks + ((nl_blocks - 1 - (s - nc_blocks)) if reverse else (s - nc_blocks))
        ctx = (nc_blocks - 1 - s) if reverse else s
        return jnp.where(s < nc_blocks, ctx, lat)

    def block(b, s):
        i = local(s)
        return jnp.where(i < nc_blocks, b * nc_blocks + i, bsz * nc_blocks + b * nl_blocks + i - nc_blocks)

    def halo(b, s):
        i = block(b, s)
        if reverse:
            return jnp.minimum((i + 1) * (blk // 8), n_blocks * (blk // 8) - 1)
        return jnp.maximum(i * (blk // 8) - 1, 0)

    vec = lambda w: pl.BlockSpec((1, w), lambda s: (0, 0))
    mat = lambda: pl.BlockSpec((LANES, RW_WIDTH), lambda s: (0, 0))
    n_units = bsz * RW_PAIRS
    unit_bf = pltpu.VMEM((n_units, blk, LANES), BF16)
    unit_f = pltpu.VMEM((n_units, blk, LANES), F32)
    z_spec = lambda b: pl.BlockSpec((blk, RW_Z), lambda s: (block(b, s), 0))
    halo_spec = lambda b: pl.BlockSpec((8, RW_Z), lambda s: (halo(b, s), 0))
    return pl.pallas_call(
        functools.partial(_rwkv_kernel, reverse=reverse, nc_blocks=nc_blocks),
        out_shape=jax.ShapeDtypeStruct((bsz, steps * blk, RW_WIDTH), F32),
        grid=(steps,),
        in_specs=[
            z_spec(0), z_spec(1), halo_spec(0), halo_spec(1),
            vec(RW_Z), vec(RW_WIDTH), mat(), vec(RW_WIDTH), mat(), vec(RW_WIDTH), vec(RW_WIDTH), vec(RW_WIDTH),
        ],
        out_specs=pl.BlockSpec((bsz, blk, RW_WIDTH), lambda s: (0, local(s), 0)),
        scratch_shapes=[
            pltpu.VMEM((n_units, LANES, LANES), F32),
            unit_bf, unit_bf, unit_bf, unit_bf, unit_bf,
            unit_bf, unit_f, unit_f, unit_bf,
            pltpu.VMEM((bsz, blk, RW_WIDTH), F32),
        ],
        compiler_params=_cp("arbitrary"),
        name="rwkv_bwd" if reverse else "rwkv_fwd",
    )(zall, zall, zall, zall, mu, w0, wup, a0, aup, k_k, k_a, r_k)


def _branches_kernel(saf_ref, sab_ref, su_ref, sd_ref, rf_ref, rb_ref, rg_ref, gup_ref, lw_ref, lb_ref,
                     of_ref, ob_ref, og_ref, ng_ref, ga_ref, gb_ref, gc_ref, wglu_ref, wrw_ref, wgl_ref, m_ref):
    za = _gelu_tanh(saf_ref[...] + sab_ref[...] + sd_ref[...] * su_ref[...].astype(F32)).astype(BF16)
    hid = _dot(za, wglu_ref[...])
    m = _sigmoid(ga_ref[...].astype(F32)) * (hid[:, :D_MODEL] * _sigmoid(hid[:, D_MODEL:]))
    ones64 = _head_ones(RW_HEAD)
    gate = _dot(_sigmoid(rg_ref[...]).astype(BF16), gup_ref[...])
    zb = []
    for p in range(RW_PAIRS):
        ls = slice(LANES * p, LANES * (p + 1))
        y = rf_ref[:, ls] + rb_ref[:, ls]
        mean = _head_sum(y, ones64, True) * (1.0 / RW_HEAD)
        yc = y - mean
        var = _head_sum(yc * yc, ones64, True) * (1.0 / RW_HEAD)
        yn = yc * lax.rsqrt(var + RW_GN_EPS) * lw_ref[:, ls] + lb_ref[:, ls]
        zb.append((yn * gate[:, ls]).astype(BF16))
    m = m + _sigmoid(gb_ref[...].astype(F32)) * _dot(jnp.concatenate(zb, axis=1), wrw_ref[...])
    zc = []
    for h in range(GLA_HEADS):
        vs = slice(GLA_DV * h, GLA_DV * (h + 1))
        o = of_ref[:, vs].astype(F32) + ob_ref[:, vs].astype(F32)
        on = o * lax.rsqrt(jnp.mean(o * o, axis=-1, keepdims=True) + NORM_EPS) * ng_ref[:, vs]
        zc.append((on * _silu(og_ref[:, vs].astype(F32))).astype(BF16))
    m = m + _sigmoid(gc_ref[...].astype(F32)) * _dot(jnp.concatenate(zc, axis=1), wgl_ref[...])
    m_ref[...] = m.astype(BF16)


def _branches(ya, yb, o_dir, u_s5, og, gates, zall, prm, row0, nc_blocks, nl_blocks):
    s5_d, g_up, ln_w, ln_b, norm_g, w_glu, w_rw, w_gl = prm
    bm = RW_BLOCK
    bsz = yb[0].shape[0]
    n = zall.shape[0] - row0 * bm
    rows = lambda w, blk=0: pl.BlockSpec((bm, w), lambda i: (i + row0, blk))

    def y_index(i):
        i = i + row0
        t = i - bsz * nc_blocks
        return (jnp.where(t < 0, i // nc_blocks, t // nl_blocks),
                jnp.where(t < 0, i % nc_blocks, nc_blocks + t % nl_blocks), 0)

    y_spec = pl.BlockSpec((None, bm, RW_WIDTH), y_index)
    vec = pl.BlockSpec((1, 1024), lambda i: (0, 0))
    const = lambda a: pl.BlockSpec(a.shape, lambda i: (0,) * a.ndim, pipeline_mode=pl.Buffered(1))
    return pl.pallas_call(
        _branches_kernel,
        out_shape=jax.ShapeDtypeStruct((n, D_MODEL), BF16),
        grid=(n // bm,),
        in_specs=[rows(1024), rows(1024), rows(1024, 0), vec,
                  y_spec, y_spec, rows(LANES, RW_Z // LANES), const(g_up), vec, vec,
                  rows(1024), rows(1024), rows(1024), vec,
                  rows(D_MODEL, 0), rows(D_MODEL, 1), rows(D_MODEL, 2),
                  const(w_glu), const(w_rw), const(w_gl)],
        out_specs=pl.BlockSpec((bm, D_MODEL), lambda i: (i, 0)),
        compiler_params=_cp("parallel"),
        name="branches",
    )(ya[0], ya[1], u_s5, s5_d, yb[0], yb[1], zall, g_up, ln_w, ln_b,
      o_dir[0], o_dir[1], og, norm_g, gates, gates, gates, w_glu, w_rw, w_gl)


def _out_proj_kernel(m_ref, w_ref, x_ref, gate_ref, o_ref):
    o_ref[...] = x_ref[...] + gate_ref[...] * _dot(m_ref[...], w_ref[...])


def _out_proj(m, w_out, x, mod, layer, bm, row0, nc, seq):
    n, d = m.shape
    bn = 1024
    mod_index = _mod_spec(layer, 2, bm, nc, seq).index_map
    return pl.pallas_call(
        _out_proj_kernel,
        out_shape=jax.ShapeDtypeStruct((n, d), F32),
        grid=(n // bm, d // bn),
        in_specs=[pl.BlockSpec((bm, d), lambda i, j: (i, 0)),
                  pl.BlockSpec((d, bn), lambda i, j: (0, j)),
                  pl.BlockSpec((bm, bn), lambda i, j: (i + row0, j)),
                  pl.BlockSpec((None, None, None, 1, bn), lambda i, j: mod_index(i + row0)[:4] + (j,))],
        out_specs=pl.BlockSpec((bm, bn), lambda i, j: (i, j)),
        compiler_params=_cp("parallel", "parallel"),
        name="out_proj",
    )(m, w_out, x, mod)


MOE_TM = 256
MOE_ROW_TILES = D_MODEL // (2 * LANES)
META_E1, META_E2, META_R1, META_R2, META_W1, META_W2 = range(6)


def _router_kernel(x_ref, g_ref, sh_ref, sc_ref, wr_hi_ref, wr_lo_ref, br_ref, v_ref, meta_ref, meta_t_ref, cnt_ref,
                   base_s):
    @pl.when(pl.program_id(0) == 0)
    def _():
        base_s[...] = jnp.zeros_like(base_s)

    x = x_ref[...]
    y = x * lax.rsqrt(jnp.mean(x * x, axis=-1, keepdims=True) + NORM_EPS) * g_ref[...]
    t = y * (1.0 + sc_ref[...]) + sh_ref[...]
    word = _pack_halves(t)
    for s in range(MOE_ROW_TILES):
        v_ref[pl.ds(s, x.shape[0], stride=MOE_ROW_TILES), :] = word[:, LANES * s:LANES * (s + 1)]
    t_hi, t_lo = _split_hi_lo(t)
    logits = (_dot(t_hi, wr_hi_ref[...]) + _dot(t_lo, wr_hi_ref[...]) + _dot(t_hi, wr_lo_ref[...])) + br_ref[...]
    lane = _iota(logits.shape, 1).astype(F32)
    neg = jnp.float32(-jnp.inf)
    big = jnp.float32(LANES)
    l1 = jnp.where(lane < MOE_GROUPS, logits, neg)
    m1 = jnp.max(l1, axis=-1, keepdims=True)
    p_top = 1.0 / jnp.sum(jnp.exp(l1 - m1), axis=-1, keepdims=True)
    grp = jnp.min(jnp.where(l1 == m1, lane, big), axis=-1, keepdims=True)
    lo = MOE_LANE0 + MOE_PER_GROUP * grp
    in_grp = (lane >= lo) & (lane < lo + MOE_PER_GROUP)
    l2 = jnp.where(in_grp, logits, neg)
    v1 = jnp.max(l2, axis=-1, keepdims=True)
    i1 = jnp.min(jnp.where(l2 == v1, lane, big), axis=-1, keepdims=True)
    l3 = jnp.where(lane == i1, neg, l2)
    v2 = jnp.max(l3, axis=-1, keepdims=True)
    i2 = jnp.min(jnp.where(l3 == v2, lane, big), axis=-1, keepdims=True)
    e2 = jnp.exp(v2 - v1)
    w1 = p_top / (1.0 + e2)
    w2 = p_top * e2 / (1.0 + e2)
    pick1 = lane == i1
    pick2 = lane == i2
    chosen = jnp.where(pick1 | pick2, 1.0, 0.0)
    bm = x.shape[0]
    earlier = (_iota((bm, bm), 1) < _iota((bm, bm), 0)).astype(BF16)
    before = _dot(earlier, chosen.astype(BF16)) + base_s[...]
    r1 = jnp.sum(jnp.where(pick1, before, 0.0), axis=-1, keepdims=True)
    r2 = jnp.sum(jnp.where(pick2, before, 0.0), axis=-1, keepdims=True)
    base_s[...] += jnp.sum(chosen, axis=0, keepdims=True)
    cnt_ref[...] = base_s[...]
    meta = jnp.zeros_like(logits)
    for slot, val in ((META_E1, i1 - MOE_LANE0), (META_E2, i2 - MOE_LANE0), (META_R1, r1), (META_R2, r2),
                      (META_W1, w1), (META_W2, w2)):
        meta = jnp.where(lane == slot, val, meta)
    meta_ref[...] = meta
    meta_t_ref[...] = meta.T[:8]


def _router(x, g, mod, layer, wr_hi, wr_lo, br, bm, nc, seq):
    n, d = x.shape
    return pl.pallas_call(
        _router_kernel,
        out_shape=(jax.ShapeDtypeStruct((n * MOE_ROW_TILES, LANES), jnp.uint32), jax.ShapeDtypeStruct((n, LANES), F32),
                   jax.ShapeDtypeStruct((8, n), F32), jax.ShapeDtypeStruct((1, LANES), F32)),
        grid=(n // bm,),
        in_specs=[
            pl.BlockSpec((bm, d), lambda i: (i, 0)),
            pl.BlockSpec((None, 1, d), lambda i: (layer, 0, 0)),
            _mod_spec(layer, 3, bm, nc, seq), _mod_spec(layer, 4, bm, nc, seq),
            pl.BlockSpec((d, LANES), lambda i: (0, 0)), pl.BlockSpec((d, LANES), lambda i: (0, 0)),
            pl.BlockSpec((1, LANES), lambda i: (0, 0)),
        ],
        out_specs=(pl.BlockSpec((bm * MOE_ROW_TILES, LANES), lambda i: (i, 0)),
                   pl.BlockSpec((bm, LANES), lambda i: (i, 0)),
                   pl.BlockSpec((8, bm), lambda i: (0, i)), pl.BlockSpec((1, LANES), lambda i: (0, 0))),
        scratch_shapes=[pltpu.VMEM((1, LANES), F32)],
        compiler_params=_cp("arbitrary"),
        name="moe_router",
    )(x, g, mod, mod, wr_hi, wr_lo, br)


def _moe_plan(meta_t, cnt):
    tm = MOE_TM
    n_tok = meta_t.shape[1]
    counts = cnt[0, MOE_LANE0:MOE_LANE0 + MOE_EXPERTS].astype(jnp.int32)
    seg = ((counts + tm - 1) // tm) * tm
    ends = jnp.cumsum(seg)
    off = ends - seg
    rec = meta_t[:4].astype(jnp.int32)
    first_row = jnp.sum(jnp.where(rec[:2, None, :] == jnp.arange(MOE_EXPERTS)[None, :, None], off[None, :, None], 0),
                        axis=1)
    dest = first_row + rec[2:4]
    n_rows = 2 * n_tok + MOE_EXPERTS * tm
    n_tiles = n_rows // tm
    tile_e = jnp.sum(((jnp.arange(n_tiles) * tm)[:, None] >= ends[None, :]).astype(jnp.int32), axis=1)
    tile_e = jnp.minimum(tile_e, MOE_EXPERTS - 1)
    tok = jnp.broadcast_to(jnp.arange(n_tok, dtype=jnp.int32), (2, n_tok))
    src = jnp.zeros((n_rows,), jnp.int32).at[dest.reshape(-1)].set(tok.reshape(-1))
    return dest[0], dest[1], src, tile_e, (ends[-1] // tm).reshape(1)


def _row_copy(src_hbm, row, dst, slot, r, sem):
    return pltpu.make_async_copy(src_hbm.at[pl.ds(row, 1)], dst.at[slot, pl.ds(r, 1)], sem)


def _experts_kernel(src_ref, te_ref, nu_ref, v_hbm, wg_ref, wu_ref, wd_ref, y_ref, xbuf, sem):
    del te_ref
    i = pl.program_id(0)
    n_used = nu_ref[0]
    slot = i % 2

    nt = MOE_ROW_TILES

    def token_copy(tok, into, r):
        return pltpu.make_async_copy(v_hbm.at[pl.ds(tok * nt, nt)], xbuf.at[into, pl.ds(r * nt, nt)], sem.at[into])

    def gather(tile, into):
        for r in range(MOE_TM):
            token_copy(src_ref[tile * MOE_TM + r], into, r).start(priority=r % 2)

    def expert():
        for r in range(MOE_TM):
            token_copy(0, slot, r).wait()
        word = jnp.concatenate([xbuf[slot, pl.ds(s, MOE_TM, stride=nt), :] for s in range(nt)], axis=1)
        t = jnp.concatenate(_unpack_halves(word), axis=1).astype(BF16)
        hid = _silu(_dot(t, wg_ref[...].astype(BF16))) * _dot(t, wu_ref[...].astype(BF16))
        y_ref[...] = _pack_halves(_dot(hid.astype(BF16), wd_ref[...].astype(BF16)))

    @pl.when((i == 0) & (n_used > 0))
    def _():
        gather(0, 0)

    @pl.when(i + 1 < n_used)
    def _():
        gather(i + 1, 1 - slot)
        expert()

    @pl.when(i + 1 == n_used)
    def _():
        expert()

    @pl.when(i >= n_used)
    def _():
        y_ref[...] = jnp.zeros_like(y_ref)


def _experts(v, src, tile_e, n_used, layer, w_gate, w_up, w_down):
    d = w_gate.shape[2]
    hdim = w_gate.shape[3]
    n_rows = src.shape[0]
    tm = MOE_TM
    by_expert = lambda i, src_r, te_r, nu_r: (layer, te_r[i], 0, 0)
    return pl.pallas_call(
        _experts_kernel,
        out_shape=jax.ShapeDtypeStruct((n_rows, d // 2), jnp.uint32),
        grid_spec=pltpu.PrefetchScalarGridSpec(
            num_scalar_prefetch=3,
            grid=(n_rows // tm,),
            in_specs=[
                pl.BlockSpec(memory_space=pl.ANY),
                pl.BlockSpec((None, None, d, hdim), by_expert),
                pl.BlockSpec((None, None, d, hdim), by_expert),
                pl.BlockSpec((None, None, hdim, d), by_expert),
            ],
            out_specs=pl.BlockSpec((tm, d // 2), lambda i, *_: (i, 0)),
            scratch_shapes=[pltpu.VMEM((2, tm * MOE_ROW_TILES, LANES), jnp.uint32), pltpu.SemaphoreType.DMA((2,))],
        ),
        compiler_params=_cp("arbitrary"),
        name="moe_experts",
    )(src, tile_e, n_used, v, w_gate, w_up, w_down)


def _combine_kernel(d1_ref, d2_ref, y_hbm, meta_ref, x_ref, gate_ref, gfin_ref, o_ref, buf1, buf2, sem, *, final):
    i = pl.program_id(0)
    bm = x_ref.shape[0]
    slot = i % 2

    def gather(tile, into):
        for r in range(bm):
            t = tile * bm + r
            _row_copy(y_hbm, d1_ref[t], buf1, into, r, sem.at[0, into]).start(priority=0)
            _row_copy(y_hbm, d2_ref[t], buf2, into, r, sem.at[1, into]).start(priority=1)

    @pl.when(i == 0)
    def _():
        gather(0, 0)

    @pl.when(i + 1 < pl.num_programs(0))
    def _():
        gather(i + 1, 1 - slot)

    for r in range(bm):
        _row_copy(y_hbm, 0, buf1, slot, r, sem.at[0, slot]).wait()
        _row_copy(y_hbm, 0, buf2, slot, r, sem.at[1, slot]).wait()
    meta = meta_ref[...]
    lane = _iota(meta.shape, 1)
    w1 = jnp.sum(jnp.where(lane == META_W1, meta, 0.0), axis=-1, keepdims=True)
    w2 = jnp.sum(jnp.where(lane == META_W2, meta, 0.0), axis=-1, keepdims=True)
    lo1, hi1 = _unpack_halves(buf1[slot])
    lo2, hi2 = _unpack_halves(buf2[slot])
    moe = jnp.concatenate([w1 * lo1 + w2 * lo2, w1 * hi1 + w2 * hi2], axis=1)
    out = x_ref[...] + gate_ref[...] * moe
    if final:
        out = out * lax.rsqrt(jnp.mean(out * out, axis=-1, keepdims=True) + NORM_EPS) * gfin_ref[...]
    o_ref[...] = out


def _combine(y, d1, d2, meta, x, mod, layer, bm, nc, seq, g_final, final):
    n, d = x.shape
    at_tile = lambda i, *_: (i, 0)
    mod_index = _mod_spec(layer, 5, bm, nc, seq).index_map
    return pl.pallas_call(
        functools.partial(_combine_kernel, final=final),
        out_shape=jax.ShapeDtypeStruct((n, d), F32),
        grid_spec=pltpu.PrefetchScalarGridSpec(
            num_scalar_prefetch=2,
            grid=(n // bm,),
            in_specs=[
                pl.BlockSpec(memory_space=pl.ANY),
                pl.BlockSpec((bm, LANES), at_tile),
                pl.BlockSpec((bm, d), at_tile),
                pl.BlockSpec((None, None, None, 1, d), lambda i, *_: mod_index(i)),
                pl.BlockSpec((1, d), lambda i, *_: (0, 0)),
            ],
            out_specs=pl.BlockSpec((bm, d), at_tile),
            scratch_shapes=[pltpu.VMEM((2, bm, d // 2), jnp.uint32), pltpu.VMEM((2, bm, d // 2), jnp.uint32),
                            pltpu.SemaphoreType.DMA((2, 2))],
        ),
        compiler_params=_cp("arbitrary"),
        name="moe_combine",
    )(d1, d2, y, meta, x, mod, g_final)


_COL = dict(s5=0, rw=S5_WIDTH, rg=S5_WIDTH + 3 * RW_WIDTH + 128, q=4352, k=4864, v=5376, gk=6400, og=6416, gates=7440)


def _pad_rows(w, rows):
    return jnp.pad(w, ((0, rows - w.shape[0]), (0, 0)))


def _layer(i, last, xa, mod, bsz, seq, ctx_len, p):
    nc = bsz * ctx_len
    rows = xa.shape[0]
    bm = 512
    msel = dict(nc=nc, seq=seq)
    u = _normmod(xa, p['g_norm1'].reshape(-1, 1, D_MODEL), mod, i, (0, 1), bm, **msel)

    w_in = p['w_in'][i]
    col = lambda a, w: w_in[:, a:a + w].astype(BF16)
    bm_in = 1088 if rows % 1088 == 0 else bm
    w_z = col(_COL['rw'], 3 * RW_WIDTH + 128 + 128)
    w_qk = col(_COL['q'], 2 * GLA_QK)
    w_v = col(_COL['v'], GLA_V)
    w_gk = jnp.pad(col(_COL['gk'], 16), ((0, 0), (0, LANES - 16)))
    u_s5 = _mm(u, col(0, S5_WIDTH), bm_in, 1024, BF16)
    og = _mm(u, col(_COL['og'], GLA_V), bm_in, 1024, BF16)
    gates = _mm(u, col(_COL['gates'], 3 * D_MODEL), bm_in, 1024, BF16)
    zall = _mm(u, w_z, bm_in, 1664)
    qk = _mm(u, w_qk, bm_in, 1024, BF16)
    vv = _mm(u, w_v, bm_in, 1024, BF16)
    gk = _mm(u, w_gk, bm_in, LANES, BF16)

    ya = []
    for d in range(2):
        tables = _s5_tables(*(p[k][i, d] for k in ('s5_a_re', 's5_a_im', 's5_log_dt', 's5_b_re', 's5_b_im',
                                                   's5_c_re', 's5_c_im')))
        ya.append(_s5_scan(u_s5, tables, bsz, ctx_len // S5_BLOCK, seq // S5_BLOCK, bool(d)))

    yb = []
    for d in range(2):
        prm = (
            jnp.pad(p['rw_mu'][i, d], (0, RW_Z - p['rw_mu'].shape[-1])).reshape(1, RW_Z),
            p['rw_w0'][i, d].reshape(1, -1),
            _pad_rows(p['rw_w_up'][i, d], LANES).astype(BF16),
            p['rw_a0'][i, d].reshape(1, -1),
            jnp.pad(p['rw_a_up'][i, d], ((RW_HEAD, 0), (0, 0))).astype(BF16),
            p['rw_k_k'][i].reshape(1, -1), p['rw_k_a'][i].reshape(1, -1), p['rw_r_k'][i].reshape(1, -1),
        )
        yb.append(_rwkv_direction(zall, prm, bsz, ctx_len // RW_BLOCK, seq // RW_BLOCK, bool(d)))

    gk_up = jnp.pad(p['gl_gk_up'][i], ((0, 0), (0, LANES - 16), (0, 0))).astype(BF16)
    gk_b = p['gl_gk_b'][i].reshape(2, 1, GLA_QK)
    s0 = jnp.zeros((2, bsz, GLA_HEADS, GLA_DV, GLA_DK), F32)
    oc, s_ctx = _gla(_to_chunk_major(qk[:nc], bsz), _to_chunk_major(vv[:nc], bsz), _to_chunk_major(gk[:nc], bsz),
                     gk_up, gk_b, s0)
    rows_l = seq // GRID_W
    lat = lambda t: t[nc:].reshape(bsz, rows_l, GRID_W * t.shape[1])
    ol, _ = _gla(lat(qk), lat(vv), lat(gk), gk_up, gk_b, s_ctx)
    o_dir = [jnp.concatenate([_from_chunk_major(oc[d], GLA_V).reshape(nc, GLA_V),
                              ol[d].reshape(bsz * seq, GLA_V)], axis=0) for d in range(2)]

    row0 = (nc // bm) if last else 0
    branch_prm = (p['s5_d'][i].reshape(1, -1), p['rw_g_up'][i].astype(BF16), p['rw_ln_w'][i].reshape(1, -1),
                  p['rw_ln_b'][i].reshape(1, -1), p['gl_norm_g'][i].reshape(1, -1), p['s5_w_glu'][i].astype(BF16),
                  p['rw_w_proj'][i].astype(BF16), p['gl_w_proj'][i].astype(BF16))
    merged = _branches(ya, yb, o_dir, u_s5, og, gates, zall, branch_prm, row0 * (bm // RW_BLOCK), ctx_len // RW_BLOCK,
                       seq // RW_BLOCK)
    xm = _out_proj(merged, p['w_out'][i].astype(BF16), xa, mod, i, bm, row0, nc, seq)
    if last:
        msel = dict(nc=0, seq=seq)

    wr = jnp.pad(jnp.concatenate([p['moe_wg1'][i], p['moe_wg2'][i]], axis=1), ((0, 0), (0, LANES - 36)))
    wr_hi = wr.astype(BF16)
    wr_lo = (wr - wr_hi.astype(F32)).astype(BF16)
    br = jnp.pad(jnp.concatenate([p['moe_bg1'][i], p['moe_bg2'][i]]), (0, LANES - 36)).reshape(1, LANES)
    vmoe, meta, meta_t, cnt = _router(xm, p['g_norm2'].reshape(-1, 1, D_MODEL), mod, i, wr_hi, wr_lo, br, bm, **msel)
    d1, d2, src, tile_e, n_used = _moe_plan(meta_t, cnt)
    y_sorted = _experts(vmoe, src, tile_e, n_used, i, p['moe_w_gate'], p['moe_w_up'], p['moe_w_down'])
    return _combine(y_sorted, d1, d2, meta, xm, mod, i, bm // 2, g_final=p['g_final'], final=last, **msel)


def kernel(x, c, ctx, c_ctx, w_mod, b_mod, g_norm1, g_norm2, w_in, s5_a_re, s5_a_im, s5_log_dt, s5_b_re, s5_b_im,
           s5_c_re, s5_c_im, s5_d, s5_w_glu, rw_mu, rw_w0, rw_w_up, rw_a0, rw_a_up, rw_k_k, rw_k_a, rw_r_k, rw_g_up,
           rw_ln_w, rw_ln_b, rw_w_proj, gl_gk_up, gl_gk_b, gl_norm_g, gl_w_proj, w_out, moe_wg1, moe_bg1, moe_wg2,
           moe_bg2, moe_w_gate, moe_w_up, moe_w_down, g_final):
    p = dict(g_norm1=g_norm1, g_norm2=g_norm2, w_in=w_in, s5_a_re=s5_a_re, s5_a_im=s5_a_im, s5_log_dt=s5_log_dt,
             s5_b_re=s5_b_re, s5_b_im=s5_b_im, s5_c_re=s5_c_re, s5_c_im=s5_c_im, s5_d=s5_d, s5_w_glu=s5_w_glu,
             rw_mu=rw_mu, rw_w0=rw_w0, rw_w_up=rw_w_up, rw_a0=rw_a0, rw_a_up=rw_a_up, rw_k_k=rw_k_k, rw_k_a=rw_k_a,
             rw_r_k=rw_r_k.reshape(rw_r_k.shape[0], -1), rw_g_up=rw_g_up, rw_ln_w=rw_ln_w, rw_ln_b=rw_ln_b,
             rw_w_proj=rw_w_proj, gl_gk_up=gl_gk_up, gl_gk_b=gl_gk_b, gl_norm_g=gl_norm_g, gl_w_proj=gl_w_proj,
             w_out=w_out, moe_wg1=moe_wg1, moe_bg1=moe_bg1, moe_wg2=moe_wg2, moe_bg2=moe_bg2, moe_w_gate=moe_w_gate,
             moe_w_up=moe_w_up, moe_w_down=moe_w_down, g_final=g_final.reshape(1, -1))
    bsz, seq, d = x.shape
    ctx_len = ctx.shape[1]
    depth = w_mod.shape[0]
    cc = jnp.concatenate([c, c_ctx[None], jnp.zeros((8 - bsz - 1, d), F32)], axis=0)
    mod = _adaln(cc, w_mod, b_mod).reshape(depth, 8, 6, 1, d)
    xa = jnp.concatenate([ctx.reshape(bsz * ctx_len, d), x.reshape(bsz * seq, d)], axis=0)
    for i in range(depth):
        xa = _layer(i, i == depth - 1, xa, mod, bsz, seq, ctx_len, p)
    return xa.reshape(bsz, seq, d)
```

```python
import functools
import math

import jax
import jax.numpy as jnp
from jax import lax
from jax.experimental import pallas as pl
from jax.experimental.pallas import tpu as pltpu

F32 = jnp.float32
BF16 = jnp.bfloat16

D_MODEL = 2048
GRID_W = 64
NORM_EPS = 1e-6

S5_WIDTH = 1024
S5_GROUP = 16
S5_GROUPS = 64
S5_STATE = 64
S5_MAX_RE = -1e-4
S5_TILE = 16
S5_PAIRS = S5_GROUPS // 2

RW_WIDTH = 1024
RW_HEAD = 64
RW_DECAY_SCALE = 0.606531
RW_GN_EPS = 64e-5
RW_BLOCK = 256
RW_CHUNK = 16
RW_PAIRS = RW_WIDTH // 128
RW_Z = 3 * RW_WIDTH + 128

GLA_HEADS = 4
GLA_DK = 128
GLA_DV = 256
GLA_QK = 512
GLA_V = 1024
GLA_TAU = 16.0
GLA_CHUNK = 64

MOE_GROUPS = 4
MOE_PER_GROUP = 8
MOE_EXPERTS = 32
MOE_HIDDEN = 256
MOE_LANE0 = MOE_GROUPS

LANES = 128
VMEM_LIMIT = 56 * 1024 * 1024


def _cp(*sem):
    return pltpu.CompilerParams(dimension_semantics=sem, vmem_limit_bytes=VMEM_LIMIT)


def _dot(a, b):
    return jnp.dot(a, b, preferred_element_type=F32)


def _dot_nt(a, b):
    return lax.dot_general(a, b, (((1,), (1,)), ((), ())), preferred_element_type=F32)


def _dot_tn(a, b):
    return lax.dot_general(a, b, (((0,), (0,)), ((), ())), preferred_element_type=F32)


def _sigmoid(x):
    return 0.5 * jnp.tanh(0.5 * x) + 0.5


def _silu(x):
    return x * _sigmoid(x)


def _gelu_tanh(x):
    return 0.5 * x * (1.0 + jnp.tanh(math.sqrt(2.0 / math.pi) * (x + 0.044715 * (x * x * x))))


def _split_hi_lo(x):
    hi = x.astype(BF16)
    lo = (x - hi.astype(F32)).astype(BF16)
    return hi, lo


def _pack_halves(x):
    bits = lax.bitcast_convert_type(x.astype(BF16).astype(F32), jnp.uint32)
    n = x.shape[1] // 2
    return (bits[:, n:] & jnp.uint32(0xFFFF0000)) | (bits[:, :n] >> 16)


def _unpack_halves(word):
    return (lax.bitcast_convert_type(word << 16, F32),
            lax.bitcast_convert_type(word & jnp.uint32(0xFFFF0000), F32))


def _iota(shape, dim):
    return lax.broadcasted_iota(jnp.int32, shape, dim)


def _head_ones(width):
    return (_iota((LANES, LANES), 0) // width == _iota((LANES, LANES), 1) // width).astype(BF16)


def _head_sum(x, ones, exact):
    if exact:
        hi, lo = _split_hi_lo(x)
        return _dot(hi, ones) + _dot(lo, ones)
    return _dot(x.astype(BF16), ones)


def _adaln_kernel(c_ref, w_ref, b_ref, o_ref):
    c = c_ref[...]
    o_ref[...] = _dot(_silu(c).astype(BF16), w_ref[...].astype(BF16)) + b_ref[...]


def _adaln(cc, w_mod, b_mod):
    depth, d, n = w_mod.shape
    bn = 1536
    return pl.pallas_call(
        _adaln_kernel,
        out_shape=jax.ShapeDtypeStruct((depth, 8, n), F32),
        grid=(depth, n // bn),
        in_specs=[
            pl.BlockSpec((8, d), lambda l, j: (0, 0)),
            pl.BlockSpec((None, d, bn), lambda l, j: (l, 0, j)),
            pl.BlockSpec((None, 1, bn), lambda l, j: (l, 0, j)),
        ],
        out_specs=pl.BlockSpec((None, 8, bn), lambda l, j: (l, 0, j)),
        compiler_params=_cp("parallel", "parallel"),
        name="adaln",
    )(cc, w_mod, b_mod.reshape(depth, 1, n))


def _mod_spec(layer, part, bm, nc, seq):
    def index(i, *_):
        r0 = i * bm
        return (layer, jnp.where(r0 < nc, 2, (r0 - nc) // seq), part, 0, 0)
    return pl.BlockSpec((None, None, None, 1, D_MODEL), index)


def _normmod_kernel(x_ref, g_ref, sh_ref, sc_ref, o_ref):
    x = x_ref[...]
    y = x * lax.rsqrt(jnp.mean(x * x, axis=-1, keepdims=True) + NORM_EPS) * g_ref[...]
    o_ref[...] = (y * (1.0 + sc_ref[...]) + sh_ref[...]).astype(o_ref.dtype)


def _normmod(x, g, mod, layer, parts, bm, nc, seq):
    n, d = x.shape
    return pl.pallas_call(
        _normmod_kernel,
        out_shape=jax.ShapeDtypeStruct((n, d), BF16),
        grid=(n // bm,),
        in_specs=[
            pl.BlockSpec((bm, d), lambda i: (i, 0)),
            pl.BlockSpec((None, 1, d), lambda i: (layer, 0, 0)),
            _mod_spec(layer, parts[0], bm, nc, seq),
            _mod_spec(layer, parts[1], bm, nc, seq),
        ],
        out_specs=pl.BlockSpec((bm, d), lambda i: (i, 0)),
        compiler_params=_cp("parallel"),
        name="normmod",
    )(x, g, mod, mod)


def _mm_kernel(x_ref, w_ref, o_ref):
    o_ref[...] = _dot(x_ref[...], w_ref[...]).astype(o_ref.dtype)


def _mm(x, w, bm, bn, out_dtype=F32):
    m, k = x.shape
    n = w.shape[1]
    return pl.pallas_call(
        _mm_kernel,
        out_shape=jax.ShapeDtypeStruct((m, n), out_dtype),
        grid=(m // bm, n // bn),
        in_specs=[pl.BlockSpec((bm, k), lambda i, j: (i, 0)), pl.BlockSpec((k, bn), lambda i, j: (0, j))],
        out_specs=pl.BlockSpec((bm, bn), lambda i, j: (i, j)),
        compiler_params=_cp("parallel", "parallel"),
        name="mm",
    )(x, w)


def _seq_blocks(bsz, nc_blocks, nl_blocks, reverse):
    def local(s):
        lat = nc_blocks + ((nl_blocks - 1 - (s - nc_blocks)) if reverse else (s - nc_blocks))
        ctx = (nc_blocks - 1 - s) if reverse else s
        return jnp.where(s < nc_blocks, ctx, lat)

    def block(b, s):
        i = local(s)
        return jnp.where(i < nc_blocks, b * nc_blocks + i, bsz * nc_blocks + b * nl_blocks + i - nc_blocks)

    return local, block


S5_PACK = 8
S5_PACKS = S5_GROUPS // S5_PACK
S5_PLANE = S5_PACK * S5_STATE
S5_BLOCK = 256
S5_ROWS = 24


def _s5_tables(a_re, a_im, log_dt, b_re, b_im, c_re, c_im):
    lam = lax.complex(jnp.minimum(a_re, S5_MAX_RE), a_im)
    ldt = lam * jnp.exp(log_dt)[:, None]
    lam_bar = jnp.exp(ldt)
    b_bar = ((lam_bar - 1.0) / lam)[..., None] * lax.complex(b_re, b_im)
    c_mat = lax.complex(c_re, c_im)
    eye = jnp.eye(S5_PACK, dtype=F32)

    def block_diag(t):
        k, g, a, b = t.shape
        return (t[:, :, :, None, :] * eye[None, :, None, :, None]).reshape(k, g * a, g * b)

    b_t = jnp.transpose(b_bar, (0, 2, 1)).reshape(S5_PACKS, S5_PACK, S5_GROUP, S5_STATE)
    bblk = jnp.concatenate([block_diag(b_t.real), block_diag(b_t.imag)], axis=2)
    c_t = jnp.transpose(c_mat, (0, 2, 1)).reshape(S5_PACKS, S5_PACK, S5_STATE, S5_GROUP)
    cblk = jnp.concatenate([block_diag(c_t.real), block_diag(-c_t.imag)], axis=1)
    expo = jnp.concatenate([jnp.arange(1, S5_TILE + 1, dtype=F32), jnp.asarray([32.0, 64.0, 128.0], F32),
                            jnp.zeros((S5_ROWS - S5_TILE - 3,), F32)])
    pw = jnp.exp(ldt[None] * expo[:, None, None]).reshape(S5_ROWS, S5_PACKS, S5_PLANE)
    pw = jnp.transpose(pw, (1, 0, 2))
    return bblk.astype(BF16), cblk.astype(BF16), pw.real, pw.imag


def _s5_scan_kernel(u_ref, bblk_ref, cblk_ref, pre_ref, pim_ref, y_ref, car_ref, up_s, h2_s, hb2_s, yp_s, *, reverse):
    t = S5_TILE
    n_t = S5_BLOCK // t
    pn = S5_PLANE

    @pl.when(pl.program_id(1) == 0)
    def _():
        car_ref[...] = jnp.zeros_like(car_ref)

    ra = _iota((S5_BLOCK, S5_BLOCK), 0)
    cb = _iota((S5_BLOCK, S5_BLOCK), 1)
    perm = ((ra // t == cb % t) & (ra % t == cb // t)).astype(BF16)
    up = _dot(perm, u_ref[...].astype(BF16)).astype(BF16)
    for pk in range(S5_PACKS):
        up_s[pk] = up[:, LANES * pk:LANES * (pk + 1)]
    order = list(range(t - 1, -1, -1)) if reverse else list(range(t))
    rowj = _iota((n_t, 1), 0)

    def states(pk, bu, h_s, hb_s):
        pre = pre_ref[pk]
        pim = pim_ref[pk]
        l_re, l_im = pre[0:1], pim[0:1]
        h_re = h_im = None
        for n, s in enumerate(order):
            rows = slice(t * s, t * (s + 1))
            b_re, b_im = bu[rows, :pn], bu[rows, pn:]
            if n == 0:
                h_re, h_im = b_re, b_im
            else:
                h_re, h_im = l_re * h_re - l_im * h_im + b_re, l_re * h_im + l_im * h_re + b_im
            h_s[rows, :pn] = h_re
            h_s[rows, pn:] = h_im
        c_re, c_im = car_ref[pk, 0:1, :pn], car_ref[pk, 0:1, pn:]
        first = rowj == (n_t - 1 if reverse else 0)
        g_re, g_im = pre[t - 1:t], pim[t - 1:t]
        e_re = h_re + jnp.where(first, g_re * c_re - g_im * c_im, 0.0)
        e_im = h_im + jnp.where(first, g_re * c_im + g_im * c_re, 0.0)
        step = 1
        for row in (t - 1, t, t + 1, t + 2):
            if reverse:
                s_re, s_im, ok = pltpu.roll(e_re, n_t - step, 0), pltpu.roll(e_im, n_t - step, 0), rowj < n_t - step
            else:
                s_re, s_im, ok = pltpu.roll(e_re, step, 0), pltpu.roll(e_im, step, 0), rowj >= step
            a_re, a_im = pre[row:row + 1], pim[row:row + 1]
            e_re = e_re + jnp.where(ok, a_re * s_re - a_im * s_im, 0.0)
            e_im = e_im + jnp.where(ok, a_re * s_im + a_im * s_re, 0.0)
            step *= 2
        last = 0 if reverse else n_t - 1
        car_ref[pk, 0:1, :pn] = e_re[last:last + 1]
        car_ref[pk, 0:1, pn:] = e_im[last:last + 1]
        if reverse:
            in_re = jnp.where(first, c_re, pltpu.roll(e_re, n_t - 1, 0))
            in_im = jnp.where(first, c_im, pltpu.roll(e_im, n_t - 1, 0))
        else:
            in_re = jnp.where(first, c_re, pltpu.roll(e_re, 1, 0))
            in_im = jnp.where(first, c_im, pltpu.roll(e_im, 1, 0))
        for n, s in enumerate(order):
            rows = slice(t * s, t * (s + 1))
            a_re, a_im = pre[n:n + 1], pim[n:n + 1]
            hb_s[rows, :pn] = (h_s[rows, :pn] + a_re * in_re - a_im * in_im).astype(BF16)
            hb_s[rows, pn:] = (h_s[rows, pn:] + a_re * in_im + a_im * in_re).astype(BF16)

    def pack_pair(j, carry):
        pks = (2 * j, 2 * j + 1)
        bu = [_dot(up_s[pk], bblk_ref[pk]) for pk in pks]
        for u, pk in enumerate(pks):
            states(pk, bu[u], h2_s.at[u], hb2_s.at[u])
            yp_s[pk] = _dot(hb2_s[u], cblk_ref[pk]).astype(BF16)
        return carry

    lax.fori_loop(0, S5_PACKS // 2, pack_pair, 0)
    for pk in range(S5_PACKS):
        y_ref[:, LANES * pk:LANES * (pk + 1)] = _dot(perm, yp_s[pk])


def _s5_scan(u_src, tables, bsz, nc_blocks, nl_blocks, reverse):
    bblk, cblk, pre, pim = tables
    rows = u_src.shape[0]
    blk = S5_BLOCK
    _, block = _seq_blocks(bsz, nc_blocks, nl_blocks, reverse)
    whole = lambda a: pl.BlockSpec(a.shape, lambda b, s: (0,) * a.ndim)
    return pl.pallas_call(
        functools.partial(_s5_scan_kernel, reverse=reverse),
        out_shape=jax.ShapeDtypeStruct((rows, S5_WIDTH), F32),
        grid=(bsz, nc_blocks + nl_blocks),
        in_specs=[pl.BlockSpec((blk, S5_WIDTH), lambda b, s: (block(b, s), 0)),
                  whole(bblk), whole(cblk), whole(pre), whole(pim)],
        out_specs=pl.BlockSpec((blk, S5_WIDTH), lambda b, s: (block(b, s), 0)),
        scratch_shapes=[
            pltpu.VMEM((S5_PACKS, 8, 2 * S5_PLANE), F32),
            pltpu.VMEM((S5_PACKS, blk, LANES), BF16),
            pltpu.VMEM((2, blk, 2 * S5_PLANE), F32),
            pltpu.VMEM((2, blk, 2 * S5_PLANE), BF16),
            pltpu.VMEM((S5_PACKS, blk, LANES), BF16),
        ],
        compiler_params=_cp("parallel", "arbitrary"),
        name="s5_bwd" if reverse else "s5_fwd",
    )(u_src, bblk, cblk, pre, pim)


def _gla_kernel(q_ref, k_ref, v_ref, gk_ref, up_ref, gb_ref, s0_ref, o_ref, sf_ref, st_ref):
    d = pl.program_id(0)
    c = pl.program_id(1)
    n = pl.num_programs(1)

    @pl.when(c == 0)
    def _():
        st_ref[...] = s0_ref[...]

    cs = GLA_CHUNK
    nb = q_ref.shape[0]
    sign = 1 - 2 * d
    row = _iota((cs, cs), 0)
    col = _iota((cs, cs), 1)
    causal = (row - col) * sign >= 0
    causal_b = causal.astype(BF16)
    rid = _iota((cs, 1), 0)
    bcum, b_mid, b_end = [], [], []
    for b in range(nb):
        x = _dot(gk_ref[b].astype(BF16), up_ref[...]) + gb_ref[...]
        log_a = (jnp.minimum(x, 0.0) - jnp.log(1.0 + jnp.exp(-jnp.abs(x)))) * (1.0 / GLA_TAU)
        la_hi, la_lo = _split_hi_lo(log_a)
        bc = _dot(causal_b, la_hi) + _dot(causal_b, la_lo)
        bcum.append(bc)
        b_mid.append(jnp.sum(jnp.where(rid == cs // 2 - d, bc, 0.0), axis=0, keepdims=True))
        b_end.append(jnp.sum(jnp.where(rid == (cs - 1) * (1 - d), bc, 0.0), axis=0, keepdims=True))
    units = [(b, h) for b in range(nb) for h in range(GLA_HEADS)]
    ks = lambda h: slice(GLA_DK * h, GLA_DK * (h + 1))
    vs = lambda h: slice(GLA_DV * h, GLA_DV * (h + 1))
    s_old = [st_ref[b, h] for b, h in units]
    qh = [q_ref[b, :, ks(h)].astype(F32) * (GLA_DK ** -0.5) for b, h in units]
    kh = [k_ref[b, :, ks(h)].astype(F32) for b, h in units]
    vh = [v_ref[b, :, vs(h)].astype(BF16) for b, h in units]
    bh = [bcum[b][:, ks(h)] for b, h in units]
    mid = [b_mid[b][:, ks(h)] for b, h in units]
    end = [b_end[b][:, ks(h)] for b, h in units]
    scores = [_dot_nt((q * jnp.exp(x - m)).astype(BF16), (k * jnp.exp(m - x)).astype(BF16))
              for q, k, x, m in zip(qh, kh, bh, mid)]
    inter = [_dot_nt((q * jnp.exp(x)).astype(BF16), s.astype(BF16)) for q, x, s in zip(qh, bh, s_old)]
    upd = [_dot_tn(v, (k * jnp.exp(e - x)).astype(BF16)) for v, k, e, x in zip(vh, kh, end, bh)]
    intra = [_dot(jnp.where(causal, sc, 0.0).astype(BF16), v) for sc, v in zip(scores, vh)]
    for u, (b, h) in enumerate(units):
        o_ref[b, :, vs(h)] = (intra[u] + inter[u]).astype(o_ref.dtype)
        st_ref[b, h] = s_old[u] * jnp.exp(end[u]) + upd[u]

    @pl.when(c == n - 1)
    def _():
        sf_ref[...] = st_ref[...]


def _gla(qk, v, gk, gk_up, gk_b, s0):
    bsz, cs, w = v.shape
    n = w // GLA_V

    def chunk(d, c):
        return c + d * (n - 1 - 2 * c)

    state_spec = pl.BlockSpec((None, bsz, GLA_HEADS, GLA_DV, GLA_DK), lambda d, c: (d, 0, 0, 0, 0))
    return pl.pallas_call(
        _gla_kernel,
        out_shape=(jax.ShapeDtypeStruct((2, bsz, cs, w), BF16), jax.ShapeDtypeStruct(s0.shape, F32)),
        grid=(2, n),
        in_specs=[
            pl.BlockSpec((bsz, cs, GLA_QK), lambda d, c: (0, 0, 2 * chunk(d, c))),
            pl.BlockSpec((bsz, cs, GLA_QK), lambda d, c: (0, 0, 2 * chunk(d, c) + 1)),
            pl.BlockSpec((bsz, cs, GLA_V), lambda d, c: (0, 0, chunk(d, c))),
            pl.BlockSpec((bsz, cs, LANES), lambda d, c: (0, 0, chunk(d, c))),
            pl.BlockSpec((None, LANES, GLA_QK), lambda d, c: (d, 0, 0)),
            pl.BlockSpec((None, 1, GLA_QK), lambda d, c: (d, 0, 0)),
            state_spec,
        ],
        out_specs=(pl.BlockSpec((None, bsz, cs, GLA_V), lambda d, c: (d, 0, 0, chunk(d, c))), state_spec),
        scratch_shapes=[pltpu.VMEM((bsz, GLA_HEADS, GLA_DV, GLA_DK), F32)],
        compiler_params=_cp("parallel", "arbitrary"),
        name="gla",
    )(qk, qk, v, gk, gk_up, gk_b, s0)


def _to_chunk_major(t, bsz):
    n = t.shape[0] // bsz
    d = t.shape[1]
    t = t.reshape(bsz, n // GLA_CHUNK, GLA_CHUNK, d)
    return jnp.transpose(t, (0, 2, 1, 3)).reshape(bsz, GLA_CHUNK, (n // GLA_CHUNK) * d)


def _from_chunk_major(t, d):
    lead = t.shape[:-2]
    n = t.shape[-1] // d
    t = t.reshape(lead + (GLA_CHUNK, n, d))
    return jnp.swapaxes(t, -3, -2).reshape(lead + (n * GLA_CHUNK, d))


def _rwkv_kernel(z0_ref, z1_ref, halo0_ref, halo1_ref, mu_ref, w0_ref, wup_ref, a0_ref, aup_ref, kk_ref, ka_ref,
                 rk_ref, y_ref, st_ref, kt_s, bt_s, kq_s, rt_s, v_s, w_s, u_s, ya_s, ab_s, pin_s,
                 *, reverse, nc_blocks):
    step = pl.program_id(0)
    blk = RW_BLOCK
    ch = RW_CHUNK
    n_ch = blk // ch
    n_b = 2
    n_units = n_b * RW_PAIRS

    @pl.when(step == 0)
    def _():
        st_ref[...] = jnp.zeros_like(st_ref)

    rowi = _iota((blk, 1), 0)
    low = _iota((1, LANES), 1) < RW_HEAD
    ones64 = _head_ones(RW_HEAD)
    seq_start = (step == 0) | (step == nc_blocks)
    rr = _iota((blk, blk), 0)
    cc = _iota((blk, blk), 1)
    same = rr // ch == cc // ch
    before = (cc > rr) if reverse else (cc < rr)
    strict = same & before
    incl = same & (before | (rr == cc))
    incl_b = incl.astype(BF16)

    for b, (z_ref, halo_ref) in enumerate(((z0_ref, halo0_ref), (z1_ref, halo1_ref))):
        z = z_ref[...]
        if reverse:
            prev = pltpu.roll(z, blk - 1, 0)
            edge = halo_ref[0:1, :]
            at_edge = rowi == blk - 1
        else:
            prev = pltpu.roll(z, 1, 0)
            edge = halo_ref[7:8, :]
            at_edge = rowi == 0
        prev = jnp.where(at_edge, jnp.where(seq_start, 0.0, edge), prev)
        zs = z + (prev - z) * mu_ref[...]
        r = zs[:, 0:RW_WIDTH]
        k = zs[:, RW_WIDTH:2 * RW_WIDTH]
        v = zs[:, 2 * RW_WIDTH:3 * RW_WIDTH]
        lora = zs[:, 3 * RW_WIDTH:3 * RW_WIDTH + LANES]
        lora_w = jnp.where(low, jnp.tanh(lora), 0.0).astype(BF16)
        lora_a = jnp.where(low, 0.0, lora).astype(BF16)
        logw = -RW_DECAY_SCALE * _sigmoid(w0_ref[...] + _dot(lora_w, wup_ref[...]))
        a = _sigmoid(a0_ref[...] + _dot(lora_a, aup_ref[...]))
        kk = k * kk_ref[...]
        kp = k * (1.0 + (a - 1.0) * ka_ref[...])
        rkb = r * kp * rk_ref[...]

        lw_hi, lw_lo = _split_hi_lo(logw)
        cl = _dot(incl_b, lw_hi) + _dot(incl_b, lw_lo)
        p_in = jnp.exp(cl)
        pin_s[b] = p_in
        p_ex = jnp.exp(cl - logw)
        p_inv = jnp.exp(-cl)

        for p in range(RW_PAIRS):
            ls = slice(LANES * p, LANES * (p + 1))
            q = b * RW_PAIRS + p
            kkp = kk[:, ls]
            ssq = _head_sum(kkp * kkp, ones64, False)
            kkn = kkp * (1.0 / jnp.maximum(jnp.sqrt(ssq), 1e-12))
            kt_s[q] = (kkn * p_ex[:, ls]).astype(BF16)
            bt_s[q] = (kkn * a[:, ls] * p_inv[:, ls]).astype(BF16)
            kq_s[q] = (kp[:, ls] * p_inv[:, ls]).astype(BF16)
            rt_s[q] = (r[:, ls] * p_in[:, ls]).astype(BF16)
            v_s[q] = v[:, ls].astype(BF16)
            ya_s[q] = _head_sum(rkb[:, ls], ones64, False) * v[:, ls]

    fold0 = (_iota((blk, LANES), 0) % ch == _iota((blk, LANES), 1)).astype(BF16)
    fold1 = (_iota((blk, LANES), 0) % ch + ch == _iota((blk, LANES), 1)).astype(BF16)
    lane_lo = _iota((1, LANES), 1) < RW_HEAD

    def pair_body(p, carry):
        units = [b * RW_PAIRS + p for b in range(n_b)]
        heads = [(u, hh) for u in range(n_b) for hh in range(2)]
        kt = [kt_s[q] for q in units]
        rt = [rt_s[q] for q in units]
        vv = [v_s[q] for q in units]
        ya = [ya_s[q] for q in units]
        bt = [bt_s[q] for q in units]
        kq = [kq_s[q] for q in units]
        hb = blk // 2
        g = []
        for u, hh in heads:
            mine = lane_lo if hh == 0 else jnp.logical_not(lane_lo)
            zero = jnp.zeros_like(kt[u])
            ktm, rtm = jnp.where(mine, kt[u], zero), jnp.where(mine, rt[u], zero)
            halves = []
            for rows in (slice(0, hb), slice(hb, blk)):
                lhs = jnp.concatenate([ktm[rows], rtm[rows]], axis=0)
                rhs = jnp.concatenate([bt[u][rows], kq[u][rows]], axis=0)
                halves.append(_dot_nt(lhs, rhs))
            g.append(halves)
        zero_q = jnp.zeros((hb, hb), F32)

        def diag2(gh, r0, c0):
            a, b = gh[0][r0:r0 + hb, c0:c0 + hb], gh[1][r0:r0 + hb, c0:c0 + hb]
            return jnp.concatenate([jnp.concatenate([a, zero_q], axis=1), jnp.concatenate([zero_q, b], axis=1)],
                                   axis=0)

        n1 = [jnp.where(strict, -diag2(gh, 0, 0), 0.0).astype(BF16) for gh in g]
        a_kq = [jnp.where(strict, diag2(gh, 0, hb), 0.0).astype(BF16) for gh in g]
        a_rb = [jnp.where(incl, diag2(gh, hb, 0), 0.0).astype(BF16) for gh in g]
        a_rq = [jnp.where(incl, diag2(gh, hb, hb), 0.0).astype(BF16) for gh in g]
        n2 = [_dot(n, n).astype(BF16) for n in n1]
        akv = [_dot(a, vv[u]) for a, (u, _) in zip(a_kq, heads)]
        n4 = [_dot(n, n).astype(BF16) for n in n2]
        y_in = [_dot(a, vv[u]) for a, (u, _) in zip(a_rq, heads)]
        n8 = [_dot(n, n).astype(BF16) for n in n4]
        fold = [_dot(a, fold0 if hh == 0 else fold1) for a, (_, hh) in zip(a_rb, heads)]
        rhs_t = [jnp.concatenate([kt[u].astype(F32), av], axis=1) for av, (u, _) in zip(akv, heads)]
        for nk in (n8, n4, n2, n1):
            rhs_t = [x + _dot(n, x.astype(BF16)) for n, x in zip(nk, rhs_t)]
        for u, q in enumerate(units):
            h0, h1 = 2 * u, 2 * u + 1
            w_s[q] = jnp.where(lane_lo, rhs_t[h0][:, :LANES], rhs_t[h1][:, :LANES]).astype(BF16)
            u_s[q] = jnp.where(lane_lo, rhs_t[h0][:, LANES:], rhs_t[h1][:, LANES:])
            ya_s[q] = ya[u] + jnp.where(lane_lo, y_in[h0], y_in[h1])
            ab_s[q] = (fold[h0] + fold[h1]).astype(BF16)
        return carry

    lax.fori_loop(0, RW_PAIRS, pair_body, 0)

    blockdiag = (_iota((LANES, LANES), 0) // RW_HEAD) == (_iota((LANES, LANES), 1) // RW_HEAD)
    end_row = 0 if reverse else ch - 1

    def chunk_body(i, carry):
        c = (n_ch - 1 - i) if reverse else i
        rows = pl.ds(pl.multiple_of(c * ch, ch), ch)
        s_old = [st_ref[q] for q in range(n_units)]
        m1 = [_dot_nt(jnp.concatenate([w_s[q, rows, :], rt_s[q, rows, :]], axis=0), s_old[q].astype(BF16))
              for q in range(n_units)]
        zc = [-(m1[q][:ch] + u_s[q, rows, :]) for q in range(n_units)]
        upd = []
        for q in range(n_units):
            zv = jnp.concatenate([zc[q].astype(BF16), v_s[q, rows, :]], axis=0)
            bk = jnp.concatenate([bt_s[q, rows, :], kq_s[q, rows, :]], axis=0)
            upd.append(_dot_tn(zv, bk))
        yc = []
        for q in range(n_units):
            z2 = jnp.concatenate([jnp.where(lane_lo, zc[q], 0.0), jnp.where(lane_lo, 0.0, zc[q])], axis=0)
            yc.append(m1[q][ch:] + _dot(ab_s[q, rows, :][:, :2 * ch], z2.astype(BF16)) + ya_s[q, rows, :])
        for q in range(n_units):
            b, p = divmod(q, RW_PAIRS)
            p_end = pin_s[b, rows, LANES * p:LANES * (p + 1)][end_row:end_row + 1]
            st_ref[q] = (s_old[q] + jnp.where(blockdiag, upd[q], 0.0)) * p_end
            y_ref[b, rows, LANES * p:LANES * (p + 1)] = yc[q]
        return carry

    lax.fori_loop(0, n_ch, chunk_body, 0)


def _rwkv_direction(zall, prm, bsz, nc_blocks, nl_blocks, reverse):
    assert bsz == 2
    mu, w0, wup, a0, aup, k_k, k_a, r_k = prm
    rows = zall.shape[0]
    blk = RW_BLOCK
    steps = nc_blocks + nl_blocks
    n_blocks = rows // blk

    def local(s):
        lat = nc_blocks + ((nl_blocks - 1 - (s - nc_blocks)) if reverse else (s - nc_blocks))
        ctx = (nc_blocks - 1 - s) if reverse else s
        return jnp.where(s < nc_blocks, ctx, lat)

    def block(b, s):
        i = local(s)
        return jnp.where(i < nc_blocks, b * nc_blocks + i, bsz * nc_blocks + b * nl_blocks + i - nc_blocks)

    def halo(b, s):
        i = block(b, s)
        if reverse:
            return jnp.minimum((i + 1) * (blk // 8), n_blocks * (blk // 8) - 1)
        return jnp.maximum(i * (blk // 8) - 1, 0)

    vec = lambda w: pl.BlockSpec((1, w), lambda s: (0, 0))
    mat = lambda: pl.BlockSpec((LANES, RW_WIDTH), lambda s: (0, 0))
    n_units = bsz * RW_PAIRS
    unit_bf = pltpu.VMEM((n_units, blk, LANES), BF16)
    unit_f = pltpu.VMEM((n_units, blk, LANES), F32)
    z_spec = lambda b: pl.BlockSpec((blk, RW_Z), lambda s: (block(b, s), 0))
    halo_spec = lambda b: pl.BlockSpec((8, RW_Z), lambda s: (halo(b, s), 0))
    return pl.pallas_call(
        functools.partial(_rwkv_kernel, reverse=reverse, nc_blocks=nc_blocks),
        out_shape=jax.ShapeDtypeStruct((bsz, steps * blk, RW_WIDTH), F32),
        grid=(steps,),
        in_specs=[
            z_spec(0), z_spec(1), halo_spec(0), halo_spec(1),
            vec(RW_Z), vec(RW_WIDTH), mat(), vec(RW_WIDTH), mat(), vec(RW_WIDTH), vec(RW_WIDTH), vec(RW_WIDTH),
        ],
        out_specs=pl.BlockSpec((bsz, blk, RW_WIDTH), lambda s: (0, local(s), 0)),
        scratch_shapes=[
            pltpu.VMEM((n_units, LANES, LANES), F32),
            unit_bf, unit_bf, unit_bf, unit_bf, unit_bf,
            unit_bf, unit_f, unit_f, unit_bf,
            pltpu.VMEM((bsz, blk, RW_WIDTH), F32),
        ],
        compiler_params=_cp("arbitrary"),
        name="rwkv_bwd" if reverse else "rwkv_fwd",
    )(zall, zall, zall, zall, mu, w0, wup, a0, aup, k_k, k_a, r_k)


def _branches_kernel(saf_ref, sab_ref, su_ref, sd_ref, rf_ref, rb_ref, rg_ref, gup_ref, lw_ref, lb_ref,
                     of_ref, ob_ref, og_ref, ng_ref, ga_ref, gb_ref, gc_ref, wglu_ref, wrw_ref, wgl_ref, m_ref):
    za = _gelu_tanh(saf_ref[...] + sab_ref[...] + sd_ref[...] * su_ref[...].astype(F32)).astype(BF16)
    hid = _dot(za, wglu_ref[...])
    m = _sigmoid(ga_ref[...].astype(F32)) * (hid[:, :D_MODEL] * _sigmoid(hid[:, D_MODEL:]))
    ones64 = _head_ones(RW_HEAD)
    gate = _dot(_sigmoid(rg_ref[...]).astype(BF16), gup_ref[...])
    zb = []
    for p in range(RW_PAIRS):
        ls = slice(LANES * p, LANES * (p + 1))
        y = rf_ref[:, ls] + rb_ref[:, ls]
        mean = _head_sum(y, ones64, True) * (1.0 / RW_HEAD)
        yc = y - mean
        var = _head_sum(yc * yc, ones64, True) * (1.0 / RW_HEAD)
        yn = yc * lax.rsqrt(var + RW_GN_EPS) * lw_ref[:, ls] + lb_ref[:, ls]
        zb.append((yn * gate[:, ls]).astype(BF16))
    m = m + _sigmoid(gb_ref[...].astype(F32)) * _dot(jnp.concatenate(zb, axis=1), wrw_ref[...])
    zc = []
    for h in range(GLA_HEADS):
        vs = slice(GLA_DV * h, GLA_DV * (h + 1))
        o = of_ref[:, vs].astype(F32) + ob_ref[:, vs].astype(F32)
        on = o * lax.rsqrt(jnp.mean(o * o, axis=-1, keepdims=True) + NORM_EPS) * ng_ref[:, vs]
        zc.append((on * _silu(og_ref[:, vs].astype(F32))).astype(BF16))
    m = m + _sigmoid(gc_ref[...].astype(F32)) * _dot(jnp.concatenate(zc, axis=1), wgl_ref[...])
    m_ref[...] = m.astype(BF16)


def _branches(ya, yb, o_dir, u_s5, og, gates, zall, prm, row0, nc_blocks, nl_blocks):
    s5_d, g_up, ln_w, ln_b, norm_g, w_glu, w_rw, w_gl = prm
    bm = RW_BLOCK
    bsz = yb[0].shape[0]
    n = zall.shape[0] - row0 * bm
    rows = lambda w, blk=0: pl.BlockSpec((bm, w), lambda i: (i + row0, blk))

    def y_index(i):
        i = i + row0
        t = i - bsz * nc_blocks
        return (jnp.where(t < 0, i // nc_blocks, t // nl_blocks),
                jnp.where(t < 0, i % nc_blocks, nc_blocks + t % nl_blocks), 0)

    y_spec = pl.BlockSpec((None, bm, RW_WIDTH), y_index)
    vec = pl.BlockSpec((1, 1024), lambda i: (0, 0))
    const = lambda a: pl.BlockSpec(a.shape, lambda i: (0,) * a.ndim, pipeline_mode=pl.Buffered(1))
    return pl.pallas_call(
        _branches_kernel,
        out_shape=jax.ShapeDtypeStruct((n, D_MODEL), BF16),
        grid=(n // bm,),
        in_specs=[rows(1024), rows(1024), rows(1024, 0), vec,
                  y_spec, y_spec, rows(LANES, RW_Z // LANES), const(g_up), vec, vec,
                  rows(1024), rows(1024), rows(1024), vec,
                  rows(D_MODEL, 0), rows(D_MODEL, 1), rows(D_MODEL, 2),
                  const(w_glu), const(w_rw), const(w_gl)],
        out_specs=pl.BlockSpec((bm, D_MODEL), lambda i: (i, 0)),
        compiler_params=_cp("parallel"),
        name="branches",
    )(ya[0], ya[1], u_s5, s5_d, yb[0], yb[1], zall, g_up, ln_w, ln_b,
      o_dir[0], o_dir[1], og, norm_g, gates, gates, gates, w_glu, w_rw, w_gl)


def _out_proj_kernel(m_ref, w_ref, x_ref, gate_ref, o_ref):
    o_ref[...] = x_ref[...] + gate_ref[...] * _dot(m_ref[...], w_ref[...])


def _out_proj(m, w_out, x, mod, layer, bm, row0, nc, seq):
    n, d = m.shape
    bn = 1024
    mod_index = _mod_spec(layer, 2, bm, nc, seq).index_map
    return pl.pallas_call(
        _out_proj_kernel,
        out_shape=jax.ShapeDtypeStruct((n, d), F32),
        grid=(n // bm, d // bn),
        in_specs=[pl.BlockSpec((bm, d), lambda i, j: (i, 0)),
                  pl.BlockSpec((d, bn), lambda i, j: (0, j)),
                  pl.BlockSpec((bm, bn), lambda i, j: (i + row0, j)),
                  pl.BlockSpec((None, None, None, 1, bn), lambda i, j: mod_index(i + row0)[:4] + (j,))],
        out_specs=pl.BlockSpec((bm, bn), lambda i, j: (i, j)),
        compiler_params=_cp("parallel", "parallel"),
        name="out_proj",
    )(m, w_out, x, mod)


MOE_TM = 256
MOE_ROW_TILES = D_MODEL // (2 * LANES)
META_E1, META_E2, META_R1, META_R2, META_W1, META_W2 = range(6)


def _router_kernel(x_ref, g_ref, sh_ref, sc_ref, wr_hi_ref, wr_lo_ref, br_ref, v_ref, meta_ref, meta_t_ref, cnt_ref,
                   base_s):
    @pl.when(pl.program_id(0) == 0)
    def _():
        base_s[...] = jnp.zeros_like(base_s)

    x = x_ref[...]
    y = x * lax.rsqrt(jnp.mean(x * x, axis=-1, keepdims=True) + NORM_EPS) * g_ref[...]
    t = y * (1.0 + sc_ref[...]) + sh_ref[...]
    word = _pack_halves(t)
    for s in range(MOE_ROW_TILES):
        v_ref[pl.ds(s, x.shape[0], stride=MOE_ROW_TILES), :] = word[:, LANES * s:LANES * (s + 1)]
    t_hi, t_lo = _split_hi_lo(t)
    logits = (_dot(t_hi, wr_hi_ref[...]) + _dot(t_lo, wr_hi_ref[...]) + _dot(t_hi, wr_lo_ref[...])) + br_ref[...]
    lane = _iota(logits.shape, 1).astype(F32)
    neg = jnp.float32(-jnp.inf)
    big = jnp.float32(LANES)
    l1 = jnp.where(lane < MOE_GROUPS, logits, neg)
    m1 = jnp.max(l1, axis=-1, keepdims=True)
    p_top = 1.0 / jnp.sum(jnp.exp(l1 - m1), axis=-1, keepdims=True)
    grp = jnp.min(jnp.where(l1 == m1, lane, big), axis=-1, keepdims=True)
    lo = MOE_LANE0 + MOE_PER_GROUP * grp
    in_grp = (lane >= lo) & (lane < lo + MOE_PER_GROUP)
    l2 = jnp.where(in_grp, logits, neg)
    v1 = jnp.max(l2, axis=-1, keepdims=True)
    i1 = jnp.min(jnp.where(l2 == v1, lane, big), axis=-1, keepdims=True)
    l3 = jnp.where(lane == i1, neg, l2)
    v2 = jnp.max(l3, axis=-1, keepdims=True)
    i2 = jnp.min(jnp.where(l3 == v2, lane, big), axis=-1, keepdims=True)
    e2 = jnp.exp(v2 - v1)
    w1 = p_top / (1.0 + e2)
    w2 = p_top * e2 / (1.0 + e2)
    pick1 = lane == i1
    pick2 = lane == i2
    chosen = jnp.where(pick1 | pick2, 1.0, 0.0)
    bm = x.shape[0]
    earlier = (_iota((bm, bm), 1) < _iota((bm, bm), 0)).astype(BF16)
    before = _dot(earlier, chosen.astype(BF16)) + base_s[...]
    r1 = jnp.sum(jnp.where(pick1, before, 0.0), axis=-1, keepdims=True)
    r2 = jnp.sum(jnp.where(pick2, before, 0.0), axis=-1, keepdims=True)
    base_s[...] += jnp.sum(chosen, axis=0, keepdims=True)
    cnt_ref[...] = base_s[...]
    meta = jnp.zeros_like(logits)
    for slot, val in ((META_E1, i1 - MOE_LANE0), (META_E2, i2 - MOE_LANE0), (META_R1, r1), (META_R2, r2),
                      (META_W1, w1), (META_W2, w2)):
        meta = jnp.where(lane == slot, val, meta)
    meta_ref[...] = meta
    meta_t_ref[...] = meta.T[:8]


def _router(x, g, mod, layer, wr_hi, wr_lo, br, bm, nc, seq):
    n, d = x.shape
    return pl.pallas_call(
        _router_kernel,
        out_shape=(jax.ShapeDtypeStruct((n * MOE_ROW_TILES, LANES), jnp.uint32), jax.ShapeDtypeStruct((n, LANES), F32),
                   jax.ShapeDtypeStruct((8, n), F32), jax.ShapeDtypeStruct((1, LANES), F32)),
        grid=(n // bm,),
        in_specs=[
            pl.BlockSpec((bm, d), lambda i: (i, 0)),
            pl.BlockSpec((None, 1, d), lambda i: (layer, 0, 0)),
            _mod_spec(layer, 3, bm, nc, seq), _mod_spec(layer, 4, bm, nc, seq),
            pl.BlockSpec((d, LANES), lambda i: (0, 0)), pl.BlockSpec((d, LANES), lambda i: (0, 0)),
            pl.BlockSpec((1, LANES), lambda i: (0, 0)),
        ],
        out_specs=(pl.BlockSpec((bm * MOE_ROW_TILES, LANES), lambda i: (i, 0)),
                   pl.BlockSpec((bm, LANES), lambda i: (i, 0)),
                   pl.BlockSpec((8, bm), lambda i: (0, i)), pl.BlockSpec((1, LANES), lambda i: (0, 0))),
        scratch_shapes=[pltpu.VMEM((1, LANES), F32)],
        compiler_params=_cp("arbitrary"),
        name="moe_router",
    )(x, g, mod, mod, wr_hi, wr_lo, br)


def _moe_plan(meta_t, cnt):
    tm = MOE_TM
    n_tok = meta_t.shape[1]
    counts = cnt[0, MOE_LANE0:MOE_LANE0 + MOE_EXPERTS].astype(jnp.int32)
    seg = ((counts + tm - 1) // tm) * tm
    ends = jnp.cumsum(seg)
    off = ends - seg
    rec = meta_t[:4].astype(jnp.int32)
    first_row = jnp.sum(jnp.where(rec[:2, None, :] == jnp.arange(MOE_EXPERTS)[None, :, None], off[None, :, None], 0),
                        axis=1)
    dest = first_row + rec[2:4]
    n_rows = 2 * n_tok + MOE_EXPERTS * tm
    n_tiles = n_rows // tm
    tile_e = jnp.sum(((jnp.arange(n_tiles) * tm)[:, None] >= ends[None, :]).astype(jnp.int32), axis=1)
    tile_e = jnp.minimum(tile_e, MOE_EXPERTS - 1)
    tok = jnp.broadcast_to(jnp.arange(n_tok, dtype=jnp.int32), (2, n_tok))
    src = jnp.zeros((n_rows,), jnp.int32).at[dest.reshape(-1)].set(tok.reshape(-1))
    return dest[0], dest[1], src, tile_e, (ends[-1] // tm).reshape(1)


def _row_copy(src_hbm, row, dst, slot, r, sem):
    return pltpu.make_async_copy(src_hbm.at[pl.ds(row, 1)], dst.at[slot, pl.ds(r, 1)], sem)


def _experts_kernel(src_ref, te_ref, nu_ref, v_hbm, wg_ref, wu_ref, wd_ref, y_ref, xbuf, sem):
    del te_ref
    i = pl.program_id(0)
    n_used = nu_ref[0]
    slot = i % 2

    nt = MOE_ROW_TILES

    def token_copy(tok, into, r):
        return pltpu.make_async_copy(v_hbm.at[pl.ds(tok * nt, nt)], xbuf.at[into, pl.ds(r * nt, nt)], sem.at[into])

    def gather(tile, into):
        for r in range(MOE_TM):
            token_copy(src_ref[tile * MOE_TM + r], into, r).start(priority=r % 2)

    def expert():
        for r in range(MOE_TM):
            token_copy(0, slot, r).wait()
        word = jnp.concatenate([xbuf[slot, pl.ds(s, MOE_TM, stride=nt), :] for s in range(nt)], axis=1)
        t = jnp.concatenate(_unpack_halves(word), axis=1).astype(BF16)
        hid = _silu(_dot(t, wg_ref[...].astype(BF16))) * _dot(t, wu_ref[...].astype(BF16))
        y_ref[...] = _pack_halves(_dot(hid.astype(BF16), wd_ref[...].astype(BF16)))

    @pl.when((i == 0) & (n_used > 0))
    def _():
        gather(0, 0)

    @pl.when(i + 1 < n_used)
    def _():
        gather(i + 1, 1 - slot)
        expert()

    @pl.when(i + 1 == n_used)
    def _():
        expert()

    @pl.when(i >= n_used)
    def _():
        y_ref[...] = jnp.zeros_like(y_ref)


def _experts(v, src, tile_e, n_used, layer, w_gate, w_up, w_down):
    d = w_gate.shape[2]
    hdim = w_gate.shape[3]
    n_rows = src.shape[0]
    tm = MOE_TM
    by_expert = lambda i, src_r, te_r, nu_r: (layer, te_r[i], 0, 0)
    return pl.pallas_call(
        _experts_kernel,
        out_shape=jax.ShapeDtypeStruct((n_rows, d // 2), jnp.uint32),
        grid_spec=pltpu.PrefetchScalarGridSpec(
            num_scalar_prefetch=3,
            grid=(n_rows // tm,),
            in_specs=[
                pl.BlockSpec(memory_space=pl.ANY),
                pl.BlockSpec((None, None, d, hdim), by_expert),
                pl.BlockSpec((None, None, d, hdim), by_expert),
                pl.BlockSpec((None, None, hdim, d), by_expert),
            ],
            out_specs=pl.BlockSpec((tm, d // 2), lambda i, *_: (i, 0)),
            scratch_shapes=[pltpu.VMEM((2, tm * MOE_ROW_TILES, LANES), jnp.uint32), pltpu.SemaphoreType.DMA((2,))],
        ),
        compiler_params=_cp("arbitrary"),
        name="moe_experts",
    )(src, tile_e, n_used, v, w_gate, w_up, w_down)


def _combine_kernel(d1_ref, d2_ref, y_hbm, meta_ref, x_ref, gate_ref, gfin_ref, o_ref, buf1, buf2, sem, *, final):
    i = pl.program_id(0)
    bm = x_ref.shape[0]
    slot = i % 2

    def gather(tile, into):
        for r in range(bm):
            t = tile * bm + r
            _row_copy(y_hbm, d1_ref[t], buf1, into, r, sem.at[0, into]).start(priority=0)
            _row_copy(y_hbm, d2_ref[t], buf2, into, r, sem.at[1, into]).start(priority=1)

    @pl.when(i == 0)
    def _():
        gather(0, 0)

    @pl.when(i + 1 < pl.num_programs(0))
    def _():
        gather(i + 1, 1 - slot)

    for r in range(bm):
        _row_copy(y_hbm, 0, buf1, slot, r, sem.at[0, slot]).wait()
        _row_copy(y_hbm, 0, buf2, slot, r, sem.at[1, slot]).wait()
    meta = meta_ref[...]
    lane = _iota(meta.shape, 1)
    w1 = jnp.sum(jnp.where(lane == META_W1, meta, 0.0), axis=-1, keepdims=True)
    w2 = jnp.sum(jnp.where(lane == META_W2, meta, 0.0), axis=-1, keepdims=True)
    lo1, hi1 = _unpack_halves(buf1[slot])
    lo2, hi2 = _unpack_halves(buf2[slot])
    moe = jnp.concatenate([w1 * lo1 + w2 * lo2, w1 * hi1 + w2 * hi2], axis=1)
    out = x_ref[...] + gate_ref[...] * moe
    if final:
        out = out * lax.rsqrt(jnp.mean(out * out, axis=-1, keepdims=True) + NORM_EPS) * gfin_ref[...]
    o_ref[...] = out


def _combine(y, d1, d2, meta, x, mod, layer, bm, nc, seq, g_final, final):
    n, d = x.shape
    at_tile = lambda i, *_: (i, 0)
    mod_index = _mod_spec(layer, 5, bm, nc, seq).index_map
    return pl.pallas_call(
        functools.partial(_combine_kernel, final=final),
        out_shape=jax.ShapeDtypeStruct((n, d), F32),
        grid_spec=pltpu.PrefetchScalarGridSpec(
            num_scalar_prefetch=2,
            grid=(n // bm,),
            in_specs=[
                pl.BlockSpec(memory_space=pl.ANY),
                pl.BlockSpec((bm, LANES), at_tile),
                pl.BlockSpec((bm, d), at_tile),
                pl.BlockSpec((None, None, None, 1, d), lambda i, *_: mod_index(i)),
                pl.BlockSpec((1, d), lambda i, *_: (0, 0)),
            ],
            out_specs=pl.BlockSpec((bm, d), at_tile),
            scratch_shapes=[pltpu.VMEM((2, bm, d // 2), jnp.uint32), pltpu.VMEM((2, bm, d // 2), jnp.uint32),
                            pltpu.SemaphoreType.DMA((2, 2))],
        ),
        compiler_params=_cp("arbitrary"),
        name="moe_combine",
    )(d1, d2, y, meta, x, mod, g_final)


_COL = dict(s5=0, rw=S5_WIDTH, rg=S5_WIDTH + 3 * RW_WIDTH + 128, q=4352, k=4864, v=5376, gk=6400, og=6416, gates=7440)


def _pad_rows(w, rows):
    return jnp.pad(w, ((0, rows - w.shape[0]), (0, 0)))


def _layer(i, last, xa, mod, bsz, seq, ctx_len, p):
    nc = bsz * ctx_len
    rows = xa.shape[0]
    bm = 512
    msel = dict(nc=nc, seq=seq)
    u = _normmod(xa, p['g_norm1'].reshape(-1, 1, D_MODEL), mod, i, (0, 1), bm, **msel)

    w_in = p['w_in'][i]
    col = lambda a, w: w_in[:, a:a + w].astype(BF16)
    bm_in = 1088 if rows % 1088 == 0 else bm
    w_z = col(_COL['rw'], 3 * RW_WIDTH + 128 + 128)
    w_qk = col(_COL['q'], 2 * GLA_QK)
    w_v = col(_COL['v'], GLA_V)
    w_gk = jnp.pad(col(_COL['gk'], 16), ((0, 0), (0, LANES - 16)))
    u_s5 = _mm(u, col(0, S5_WIDTH), bm_in, 1024, BF16)
    og = _mm(u, col(_COL['og'], GLA_V), bm_in, 1024, BF16)
    gates = _mm(u, col(_COL['gates'], 3 * D_MODEL), bm_in, 1024, BF16)
    zall = _mm(u, w_z, bm_in, 1664)
    qk = _mm(u, w_qk, bm_in, 1024, BF16)
    vv = _mm(u, w_v, bm_in, 1024, BF16)
    gk = _mm(u, w_gk, bm_in, LANES, BF16)

    ya = []
    for d in range(2):
        tables = _s5_tables(*(p[k][i, d] for k in ('s5_a_re', 's5_a_im', 's5_log_dt', 's5_b_re', 's5_b_im',
                                                   's5_c_re', 's5_c_im')))
        ya.append(_s5_scan(u_s5, tables, bsz, ctx_len // S5_BLOCK, seq // S5_BLOCK, bool(d)))

    yb = []
    for d in range(2):
        prm = (
            jnp.pad(p['rw_mu'][i, d], (0, RW_Z - p['rw_mu'].shape[-1])).reshape(1, RW_Z),
            p['rw_w0'][i, d].reshape(1, -1),
            _pad_rows(p['rw_w_up'][i, d], LANES).astype(BF16),
            p['rw_a0'][i, d].reshape(1, -1),
            jnp.pad(p['rw_a_up'][i, d], ((RW_HEAD, 0), (0, 0))).astype(BF16),
            p['rw_k_k'][i].reshape(1, -1), p['rw_k_a'][i].reshape(1, -1), p['rw_r_k'][i].reshape(1, -1),
        )
        yb.append(_rwkv_direction(zall, prm, bsz, ctx_len // RW_BLOCK, seq // RW_BLOCK, bool(d)))

    gk_up = jnp.pad(p['gl_gk_up'][i], ((0, 0), (0, LANES - 16), (0, 0))).astype(BF16)
    gk_b = p['gl_gk_b'][i].reshape(2, 1, GLA_QK)
    s0 = jnp.zeros((2, bsz, GLA_HEADS, GLA_DV, GLA_DK), F32)
    oc, s_ctx = _gla(_to_chunk_major(qk[:nc], bsz), _to_chunk_major(vv[:nc], bsz), _to_chunk_major(gk[:nc], bsz),
                     gk_up, gk_b, s0)
    rows_l = seq // GRID_W
    lat = lambda t: t[nc:].reshape(bsz, rows_l, GRID_W * t.shape[1])
    ol, _ = _gla(lat(qk), lat(vv), lat(gk), gk_up, gk_b, s_ctx)
    o_dir = [jnp.concatenate([_from_chunk_major(oc[d], GLA_V).reshape(nc, GLA_V),
                              ol[d].reshape(bsz * seq, GLA_V)], axis=0) for d in range(2)]

    row0 = (nc // bm) if last else 0
    branch_prm = (p['s5_d'][i].reshape(1, -1), p['rw_g_up'][i].astype(BF16), p['rw_ln_w'][i].reshape(1, -1),
                  p['rw_ln_b'][i].reshape(1, -1), p['gl_norm_g'][i].reshape(1, -1), p['s5_w_glu'][i].astype(BF16),
                  p['rw_w_proj'][i].astype(BF16), p['gl_w_proj'][i].astype(BF16))
    merged = _branches(ya, yb, o_dir, u_s5, og, gates, zall, branch_prm, row0 * (bm // RW_BLOCK), ctx_len // RW_BLOCK,
                       seq // RW_BLOCK)
    xm = _out_proj(merged, p['w_out'][i].astype(BF16), xa, mod, i, bm, row0, nc, seq)
    if last:
        msel = dict(nc=0, seq=seq)

    wr = jnp.pad(jnp.concatenate([p['moe_wg1'][i], p['moe_wg2'][i]], axis=1), ((0, 0), (0, LANES - 36)))
    wr_hi = wr.astype(BF16)
    wr_lo = (wr - wr_hi.astype(F32)).astype(BF16)
    br = jnp.pad(jnp.concatenate([p['moe_bg1'][i], p['moe_bg2'][i]]), (0, LANES - 36)).reshape(1, LANES)
    vmoe, meta, meta_t, cnt = _router(xm, p['g_norm2'].reshape(-1, 1, D_MODEL), mod, i, wr_hi, wr_lo, br, bm, **msel)
    d1, d2, src, tile_e, n_used = _moe_plan(meta_t, cnt)
    y_sorted = _experts(vmoe, src, tile_e, n_used, i, p['moe_w_gate'], p['moe_w_up'], p['moe_w_down'])
    return _combine(y_sorted, d1, d2, meta, xm, mod, i, bm // 2, g_final=p['g_final'], final=last, **msel)


def kernel(x, c, ctx, c_ctx, w_mod, b_mod, g_norm1, g_norm2, w_in, s5_a_re, s5_a_im, s5_log_dt, s5_b_re, s5_b_im,
           s5_c_re, s5_c_im, s5_d, s5_w_glu, rw_mu, rw_w0, rw_w_up, rw_a0, rw_a_up, rw_k_k, rw_k_a, rw_r_k, rw_g_up,
           rw_ln_w, rw_ln_b, rw_w_proj, gl_gk_up, gl_gk_b, gl_norm_g, gl_w_proj, w_out, moe_wg1, moe_bg1, moe_wg2,
           moe_bg2, moe_w_gate, moe_w_up, moe_w_down, g_final):
    p = dict(g_norm1=g_norm1, g_norm2=g_norm2, w_in=w_in, s5_a_re=s5_a_re, s5_a_im=s5_a_im, s5_log_dt=s5_log_dt,
             s5_b_re=s5_b_re, s5_b_im=s5_b_im, s5_c_re=s5_c_re, s5_c_im=s5_c_im, s5_d=s5_d, s5_w_glu=s5_w_glu,
             rw_mu=rw_mu, rw_w0=rw_w0, rw_w_up=rw_w_up, rw_a0=rw_a0, rw_a_up=rw_a_up, rw_k_k=rw_k_k, rw_k_a=rw_k_a,
             rw_r_k=rw_r_k.reshape(rw_r_k.shape[0], -1), rw_g_up=rw_g_up, rw_ln_w=rw_ln_w, rw_ln_b=rw_ln_b,
             rw_w_proj=rw_w_proj, gl_gk_up=gl_gk_up, gl_gk_b=gl_gk_b, gl_norm_g=gl_norm_g, gl_w_proj=gl_w_proj,
             w_out=w_out, moe_wg1=moe_wg1, moe_bg1=moe_bg1, moe_wg2=moe_wg2, moe_bg2=moe_bg2, moe_w_gate=moe_w_gate,
             moe_w_up=moe_w_up, moe_w_down=moe_w_down, g_final=g_final.reshape(1, -1))
    bsz, seq, d = x.shape
    ctx_len = ctx.shape[1]
    depth = w_mod.shape[0]
    cc = jnp.concatenate([c, c_ctx[None], jnp.zeros((8 - bsz - 1, d), F32)], axis=0)
    mod = _adaln(cc, w_mod, b_mod).reshape(depth, 8, 6, 1, d)
    xa = jnp.concatenate([ctx.reshape(bsz * ctx_len, d), x.reshape(bsz * seq, d)], axis=0)
    for i in range(depth):
        xa = _layer(i, i == depth - 1, xa, mod, bsz, seq, ctx_len, p)
    return xa.reshape(bsz, seq, d)
```

```python
import functools
import math

import jax
import jax.numpy as jnp
from jax import lax
from jax.experimental import pallas as pl
from jax.experimental.pallas import tpu as pltpu

F32 = jnp.float32
BF16 = jnp.bfloat16

D_MODEL = 2048
GRID_W = 64
NORM_EPS = 1e-6

S5_WIDTH = 1024
S5_GROUP = 16
S5_GROUPS = 64
S5_STATE = 64
S5_MAX_RE = -1e-4
S5_TILE = 16
S5_PAIRS = S5_GROUPS // 2

RW_WIDTH = 1024
RW_HEAD = 64
RW_DECAY_SCALE = 0.606531
RW_GN_EPS = 64e-5
RW_BLOCK = 256
RW_CHUNK = 16
RW_PAIRS = RW_WIDTH // 128
RW_Z = 3 * RW_WIDTH + 128

GLA_HEADS = 4
GLA_DK = 128
GLA_DV = 256
GLA_QK = 512
GLA_V = 1024
GLA_TAU = 16.0
GLA_CHUNK = 64

MOE_GROUPS = 4
MOE_PER_GROUP = 8
MOE_EXPERTS = 32
MOE_HIDDEN = 256
MOE_LANE0 = MOE_GROUPS

LANES = 128
VMEM_LIMIT = 56 * 1024 * 1024


def _cp(*sem):
    return pltpu.CompilerParams(dimension_semantics=sem, vmem_limit_bytes=VMEM_LIMIT)


def _dot(a, b):
    return jnp.dot(a, b, preferred_element_type=F32)


def _dot_nt(a, b):
    return lax.dot_general(a, b, (((1,), (1,)), ((), ())), preferred_element_type=F32)


def _dot_tn(a, b):
    return lax.dot_general(a, b, (((0,), (0,)), ((), ())), preferred_element_type=F32)


def _sigmoid(x):
    return 0.5 * jnp.tanh(0.5 * x) + 0.5


def _silu(x):
    return x * _sigmoid(x)


def _gelu_tanh(x):
    return 0.5 * x * (1.0 + jnp.tanh(math.sqrt(2.0 / math.pi) * (x + 0.044715 * (x * x * x))))


def _split_hi_lo(x):
    hi = x.astype(BF16)
    lo = (x - hi.astype(F32)).astype(BF16)
    return hi, lo


def _pack_halves(x):
    bits = lax.bitcast_convert_type(x.astype(BF16).astype(F32), jnp.uint32)
    n = x.shape[1] // 2
    return (bits[:, n:] & jnp.uint32(0xFFFF0000)) | (bits[:, :n] >> 16)


def _unpack_halves(word):
    return (lax.bitcast_convert_type(word << 16, F32),
            lax.bitcast_convert_type(word & jnp.uint32(0xFFFF0000), F32))


def _iota(shape, dim):
    return lax.broadcasted_iota(jnp.int32, shape, dim)


def _head_ones(width):
    return (_iota((LANES, LANES), 0) // width == _iota((LANES, LANES), 1) // width).astype(BF16)


def _head_sum(x, ones, exact):
    if exact:
        hi, lo = _split_hi_lo(x)
        return _dot(hi, ones) + _dot(lo, ones)
    return _dot(x.astype(BF16), ones)


def _adaln_kernel(c_ref, w_ref, b_ref, o_ref):
    c = c_ref[...]
    o_ref[...] = _dot(_silu(c).astype(BF16), w_ref[...].astype(BF16)) + b_ref[...]


def _adaln(cc, w_mod, b_mod):
    depth, d, n = w_mod.shape
    bn = 1536
    return pl.pallas_call(
        _adaln_kernel,
        out_shape=jax.ShapeDtypeStruct((depth, 8, n), F32),
        grid=(depth, n // bn),
        in_specs=[
            pl.BlockSpec((8, d), lambda l, j: (0, 0)),
            pl.BlockSpec((None, d, bn), lambda l, j: (l, 0, j)),
            pl.BlockSpec((None, 1, bn), lambda l, j: (l, 0, j)),
        ],
        out_specs=pl.BlockSpec((None, 8, bn), lambda l, j: (l, 0, j)),
        compiler_params=_cp("parallel", "parallel"),
        name="adaln",
    )(cc, w_mod, b_mod.reshape(depth, 1, n))


def _mod_spec(layer, part, bm, nc, seq):
    def index(i, *_):
        r0 = i * bm
        return (layer, jnp.where(r0 < nc, 2, (r0 - nc) // seq), part, 0, 0)
    return pl.BlockSpec((None, None, None, 1, D_MODEL), index)


def _normmod_kernel(x_ref, g_ref, sh_ref, sc_ref, o_ref):
    x = x_ref[...]
    y = x * lax.rsqrt(jnp.mean(x * x, axis=-1, keepdims=True) + NORM_EPS) * g_ref[...]
    o_ref[...] = (y * (1.0 + sc_ref[...]) + sh_ref[...]).astype(o_ref.dtype)


def _normmod(x, g, mod, layer, parts, bm, nc, seq):
    n, d = x.shape
    return pl.pallas_call(
        _normmod_kernel,
        out_shape=jax.ShapeDtypeStruct((n, d), BF16),
        grid=(n // bm,),
        in_specs=[
            pl.BlockSpec((bm, d), lambda i: (i, 0)),
            pl.BlockSpec((None, 1, d), lambda i: (layer, 0, 0)),
            _mod_spec(layer, parts[0], bm, nc, seq),
            _mod_spec(layer, parts[1], bm, nc, seq),
        ],
        out_specs=pl.BlockSpec((bm, d), lambda i: (i, 0)),
        compiler_params=_cp("parallel"),
        name="normmod",
    )(x, g, mod, mod)


def _mm_kernel(x_ref, w_ref, o_ref):
    o_ref[...] = _dot(x_ref[...], w_ref[...]).astype(o_ref.dtype)


def _mm(x, w, bm, bn, out_dtype=F32):
    m, k = x.shape
    n = w.shape[1]
    return pl.pallas_call(
        _mm_kernel,
        out_shape=jax.ShapeDtypeStruct((m, n), out_dtype),
        grid=(m // bm, n // bn),
        in_specs=[pl.BlockSpec((bm, k), lambda i, j: (i, 0)), pl.BlockSpec((k, bn), lambda i, j: (0, j))],
        out_specs=pl.BlockSpec((bm, bn), lambda i, j: (i, j)),
        compiler_params=_cp("parallel", "parallel"),
        name="mm",
    )(x, w)


def _seq_blocks(bsz, nc_blocks, nl_blocks, reverse):
    def local(s):
        lat = nc_blocks + ((nl_blocks - 1 - (s - nc_blocks)) if reverse else (s - nc_blocks))
        ctx = (nc_blocks - 1 - s) if reverse else s
        return jnp.where(s < nc_blocks, ctx, lat)

    def block(b, s):
        i = local(s)
        return jnp.where(i < nc_blocks, b * nc_blocks + i, bsz * nc_blocks + b * nl_blocks + i - nc_blocks)

    return local, block


S5_PACK = 8
S5_PACKS = S5_GROUPS // S5_PACK
S5_PLANE = S5_PACK * S5_STATE
S5_BLOCK = 256
S5_ROWS = 24


def _s5_tables(a_re, a_im, log_dt, b_re, b_im, c_re, c_im):
    lam = lax.complex(jnp.minimum(a_re, S5_MAX_RE), a_im)
    ldt = lam * jnp.exp(log_dt)[:, None]
    lam_bar = jnp.exp(ldt)
    b_bar = ((lam_bar - 1.0) / lam)[..., None] * lax.complex(b_re, b_im)
    c_mat = lax.complex(c_re, c_im)
    eye = jnp.eye(S5_PACK, dtype=F32)

    def block_diag(t):
        k, g, a, b = t.shape
        return (t[:, :, :, None, :] * eye[None, :, None, :, None]).reshape(k, g * a, g * b)

    b_t = jnp.transpose(b_bar, (0, 2, 1)).reshape(S5_PACKS, S5_PACK, S5_GROUP, S5_STATE)
    bblk = jnp.concatenate([block_diag(b_t.real), block_diag(b_t.imag)], axis=2)
    c_t = jnp.transpose(c_mat, (0, 2, 1)).reshape(S5_PACKS, S5_PACK, S5_STATE, S5_GROUP)
    cblk = jnp.concatenate([block_diag(c_t.real), block_diag(-c_t.imag)], axis=1)
    expo = jnp.concatenate([jnp.arange(1, S5_TILE + 1, dtype=F32), jnp.asarray([32.0, 64.0, 128.0], F32),
                            jnp.zeros((S5_ROWS - S5_TILE - 3,), F32)])
    pw = jnp.exp(ldt[None] * expo[:, None, None]).reshape(S5_ROWS, S5_PACKS, S5_PLANE)
    pw = jnp.transpose(pw, (1, 0, 2))
    return bblk.astype(BF16), cblk.astype(BF16), pw.real, pw.imag


def _s5_scan_kernel(u_ref, bblk_ref, cblk_ref, pre_ref, pim_ref, y_ref, car_ref, up_s, h2_s, hb2_s, yp_s, *, reverse):
    t = S5_TILE
    n_t = S5_BLOCK // t
    pn = S5_PLANE

    @pl.when(pl.program_id(1) == 0)
    def _():
        car_ref[...] = jnp.zeros_like(car_ref)

    ra = _iota((S5_BLOCK, S5_BLOCK), 0)
    cb = _iota((S5_BLOCK, S5_BLOCK), 1)
    perm = ((ra // t == cb % t) & (ra % t == cb // t)).astype(BF16)
    up = _dot(perm, u_ref[...].astype(BF16)).astype(BF16)
    for pk in range(S5_PACKS):
        up_s[pk] = up[:, LANES * pk:LANES * (pk + 1)]
    order = list(range(t - 1, -1, -1)) if reverse else list(range(t))
    rowj = _iota((n_t, 1), 0)

    def states(pk, bu, h_s, hb_s):
        pre = pre_ref[pk]
        pim = pim_ref[pk]
        l_re, l_im = pre[0:1], pim[0:1]
        h_re = h_im = None
        for n, s in enumerate(order):
            rows = slice(t * s, t * (s + 1))
            b_re, b_im = bu[rows, :pn], bu[rows, pn:]
            if n == 0:
                h_re, h_im = b_re, b_im
            else:
                h_re, h_im = l_re * h_re - l_im * h_im + b_re, l_re * h_im + l_im * h_re + b_im
            h_s[rows, :pn] = h_re
            h_s[rows, pn:] = h_im
        c_re, c_im = car_ref[pk, 0:1, :pn], car_ref[pk, 0:1, pn:]
        first = rowj == (n_t - 1 if reverse else 0)
        g_re, g_im = pre[t - 1:t], pim[t - 1:t]
        e_re = h_re + jnp.where(first, g_re * c_re - g_im * c_im, 0.0)
        e_im = h_im + jnp.where(first, g_re * c_im + g_im * c_re, 0.0)
        step = 1
        for row in (t - 1, t, t + 1, t + 2):
            if reverse:
                s_re, s_im, ok = pltpu.roll(e_re, n_t - step, 0), pltpu.roll(e_im, n_t - step, 0), rowj < n_t - step
            else:
                s_re, s_im, ok = pltpu.roll(e_re, step, 0), pltpu.roll(e_im, step, 0), rowj >= step
            a_re, a_im = pre[row:row + 1], pim[row:row + 1]
            e_re = e_re + jnp.where(ok, a_re * s_re - a_im * s_im, 0.0)
            e_im = e_im + jnp.where(ok, a_re * s_im + a_im * s_re, 0.0)
            step *= 2
        last = 0 if reverse else n_t - 1
        car_ref[pk, 0:1, :pn] = e_re[last:last + 1]
        car_ref[pk, 0:1, pn:] = e_im[last:last + 1]
        if reverse:
            in_re = jnp.where(first, c_re, pltpu.roll(e_re, n_t - 1, 0))
            in_im = jnp.where(first, c_im, pltpu.roll(e_im, n_t - 1, 0))
        else:
            in_re = jnp.where(first, c_re, pltpu.roll(e_re, 1, 0))
            in_im = jnp.where(first, c_im, pltpu.roll(e_im, 1, 0))
        for n, s in enumerate(order):
            rows = slice(t * s, t * (s + 1))
            a_re, a_im = pre[n:n + 1], pim[n:n + 1]
            hb_s[rows, :pn] = (h_s[rows, :pn] + a_re * in_re - a_im * in_im).astype(BF16)
            hb_s[rows, pn:] = (h_s[rows, pn:] + a_re * in_im + a_im * in_re).astype(BF16)

    def pack_pair(j, carry):
        pks = (2 * j, 2 * j + 1)
        bu = [_dot(up_s[pk], bblk_ref[pk]) for pk in pks]
        for u, pk in enumerate(pks):
            states(pk, bu[u], h2_s.at[u], hb2_s.at[u])
            yp_s[pk] = _dot(hb2_s[u], cblk_ref[pk]).astype(BF16)
        return carry

    lax.fori_loop(0, S5_PACKS // 2, pack_pair, 0)
    for pk in range(S5_PACKS):
        y_ref[:, LANES * pk:LANES * (pk + 1)] = _dot(perm, yp_s[pk])


def _s5_scan(u_src, tables, bsz, nc_blocks, nl_blocks, reverse):
    bblk, cblk, pre, pim = tables
    rows = u_src.shape[0]
    blk = S5_BLOCK
    _, block = _seq_blocks(bsz, nc_blocks, nl_blocks, reverse)
    whole = lambda a: pl.BlockSpec(a.shape, lambda b, s: (0,) * a.ndim)
    return pl.pallas_call(
        functools.partial(_s5_scan_kernel, reverse=reverse),
        out_shape=jax.ShapeDtypeStruct((rows, S5_WIDTH), F32),
        grid=(bsz, nc_blocks + nl_blocks),
        in_specs=[pl.BlockSpec((blk, S5_WIDTH), lambda b, s: (block(b, s), 0)),
                  whole(bblk), whole(cblk), whole(pre), whole(pim)],
        out_specs=pl.BlockSpec((blk, S5_WIDTH), lambda b, s: (block(b, s), 0)),
        scratch_shapes=[
            pltpu.VMEM((S5_PACKS, 8, 2 * S5_PLANE), F32),
            pltpu.VMEM((S5_PACKS, blk, LANES), BF16),
            pltpu.VMEM((2, blk, 2 * S5_PLANE), F32),
            pltpu.VMEM((2, blk, 2 * S5_PLANE), BF16),
            pltpu.VMEM((S5_PACKS, blk, LANES), BF16),
        ],
        compiler_params=_cp("parallel", "arbitrary"),
        name="s5_bwd" if reverse else "s5_fwd",
    )(u_src, bblk, cblk, pre, pim)


def _gla_kernel(q_ref, k_ref, v_ref, gk_ref, up_ref, gb_ref, s0_ref, o_ref, sf_ref, st_ref):
    d = pl.program_id(0)
    c = pl.program_id(1)
    n = pl.num_programs(1)

    @pl.when(c == 0)
    def _():
        st_ref[...] = s0_ref[...]

    cs = GLA_CHUNK
    nb = q_ref.shape[0]
    sign = 1 - 2 * d
    row = _iota((cs, cs), 0)
    col = _iota((cs, cs), 1)
    causal = (row - col) * sign >= 0
    causal_b = causal.astype(BF16)
    rid = _iota((cs, 1), 0)
    bcum, b_mid, b_end = [], [], []
    for b in range(nb):
        x = _dot(gk_ref[b].astype(BF16), up_ref[...]) + gb_ref[...]
        log_a = (jnp.minimum(x, 0.0) - jnp.log(1.0 + jnp.exp(-jnp.abs(x)))) * (1.0 / GLA_TAU)
        la_hi, la_lo = _split_hi_lo(log_a)
        bc = _dot(causal_b, la_hi) + _dot(causal_b, la_lo)
        bcum.append(bc)
        b_mid.append(jnp.sum(jnp.where(rid == cs // 2 - d, bc, 0.0), axis=0, keepdims=True))
        b_end.append(jnp.sum(jnp.where(rid == (cs - 1) * (1 - d), bc, 0.0), axis=0, keepdims=True))
    units = [(b, h) for b in range(nb) for h in range(GLA_HEADS)]
    ks = lambda h: slice(GLA_DK * h, GLA_DK * (h + 1))
    vs = lambda h: slice(GLA_DV * h, GLA_DV * (h + 1))
    s_old = [st_ref[b, h] for b, h in units]
    qh = [q_ref[b, :, ks(h)].astype(F32) * (GLA_DK ** -0.5) for b, h in units]
    kh = [k_ref[b, :, ks(h)].astype(F32) for b, h in units]
    vh = [v_ref[b, :, vs(h)].astype(BF16) for b, h in units]
    bh = [bcum[b][:, ks(h)] for b, h in units]
    mid = [b_mid[b][:, ks(h)] for b, h in units]
    end = [b_end[b][:, ks(h)] for b, h in units]
    scores = [_dot_nt((q * jnp.exp(x - m)).astype(BF16), (k * jnp.exp(m - x)).astype(BF16))
              for q, k, x, m in zip(qh, kh, bh, mid)]
    inter = [_dot_nt((q * jnp.exp(x)).astype(BF16), s.astype(BF16)) for q, x, s in zip(qh, bh, s_old)]
    upd = [_dot_tn(v, (k * jnp.exp(e - x)).astype(BF16)) for v, k, e, x in zip(vh, kh, end, bh)]
    intra = [_dot(jnp.where(causal, sc, 0.0).astype(BF16), v) for sc, v in zip(scores, vh)]
    for u, (b, h) in enumerate(units):
        o_ref[b, :, vs(h)] = (intra[u] + inter[u]).astype(o_ref.dtype)
        st_ref[b, h] = s_old[u] * jnp.exp(end[u]) + upd[u]

    @pl.when(c == n - 1)
    def _():
        sf_ref[...] = st_ref[...]


def _gla(qk, v, gk, gk_up, gk_b, s0):
    bsz, cs, w = v.shape
    n = w // GLA_V

    def chunk(d, c):
        return c + d * (n - 1 - 2 * c)

    state_spec = pl.BlockSpec((None, bsz, GLA_HEADS, GLA_DV, GLA_DK), lambda d, c: (d, 0, 0, 0, 0))
    return pl.pallas_call(
        _gla_kernel,
        out_shape=(jax.ShapeDtypeStruct((2, bsz, cs, w), BF16), jax.ShapeDtypeStruct(s0.shape, F32)),
        grid=(2, n),
        in_specs=[
            pl.BlockSpec((bsz, cs, GLA_QK), lambda d, c: (0, 0, 2 * chunk(d, c))),
            pl.BlockSpec((bsz, cs, GLA_QK), lambda d, c: (0, 0, 2 * chunk(d, c) + 1)),
            pl.BlockSpec((bsz, cs, GLA_V), lambda d, c: (0, 0, chunk(d, c))),
            pl.BlockSpec((bsz, cs, LANES), lambda d, c: (0, 0, chunk(d, c))),
            pl.BlockSpec((None, LANES, GLA_QK), lambda d, c: (d, 0, 0)),
            pl.BlockSpec((None, 1, GLA_QK), lambda d, c: (d, 0, 0)),
            state_spec,
        ],
        out_specs=(pl.BlockSpec((None, bsz, cs, GLA_V), lambda d, c: (d, 0, 0, chunk(d, c))), state_spec),
        scratch_shapes=[pltpu.VMEM((bsz, GLA_HEADS, GLA_DV, GLA_DK), F32)],
        compiler_params=_cp("parallel", "arbitrary"),
        name="gla",
    )(qk, qk, v, gk, gk_up, gk_b, s0)


def _to_chunk_major(t, bsz):
    n = t.shape[0] // bsz
    d = t.shape[1]
    t = t.reshape(bsz, n // GLA_CHUNK, GLA_CHUNK, d)
    return jnp.transpose(t, (0, 2, 1, 3)).reshape(bsz, GLA_CHUNK, (n // GLA_CHUNK) * d)


def _from_chunk_major(t, d):
    lead = t.shape[:-2]
    n = t.shape[-1] // d
    t = t.reshape(lead + (GLA_CHUNK, n, d))
    return jnp.swapaxes(t, -3, -2).reshape(lead + (n * GLA_CHUNK, d))


def _rwkv_kernel(z0_ref, z1_ref, halo0_ref, halo1_ref, mu_ref, w0_ref, wup_ref, a0_ref, aup_ref, kk_ref, ka_ref,
                 rk_ref, y_ref, st_ref, kt_s, bt_s, kq_s, rt_s, v_s, w_s, u_s, ya_s, ab_s, pin_s,
                 *, reverse, nc_blocks):
    step = pl.program_id(0)
    blk = RW_BLOCK
    ch = RW_CHUNK
    n_ch = blk // ch
    n_b = 2
    n_units = n_b * RW_PAIRS

    @pl.when(step == 0)
    def _():
        st_ref[...] = jnp.zeros_like(st_ref)

    rowi = _iota((blk, 1), 0)
    low = _iota((1, LANES), 1) < RW_HEAD
    ones64 = _head_ones(RW_HEAD)
    seq_start = (step == 0) | (step == nc_blocks)
    rr = _iota((blk, blk), 0)
    cc = _iota((blk, blk), 1)
    same = rr // ch == cc // ch
    before = (cc > rr) if reverse else (cc < rr)
    strict = same & before
    incl = same & (before | (rr == cc))
    incl_b = incl.astype(BF16)

    for b, (z_ref, halo_ref) in enumerate(((z0_ref, halo0_ref), (z1_ref, halo1_ref))):
        z = z_ref[...]
        if reverse:
            prev = pltpu.roll(z, blk - 1, 0)
            edge = halo_ref[0:1, :]
            at_edge = rowi == blk - 1
        else:
            prev = pltpu.roll(z, 1, 0)
            edge = halo_ref[7:8, :]
            at_edge = rowi == 0
        prev = jnp.where(at_edge, jnp.where(seq_start, 0.0, edge), prev)
        zs = z + (prev - z) * mu_ref[...]
        r = zs[:, 0:RW_WIDTH]
        k = zs[:, RW_WIDTH:2 * RW_WIDTH]
        v = zs[:, 2 * RW_WIDTH:3 * RW_WIDTH]
        lora = zs[:, 3 * RW_WIDTH:3 * RW_WIDTH + LANES]
        lora_w = jnp.where(low, jnp.tanh(lora), 0.0).astype(BF16)
        lora_a = jnp.where(low, 0.0, lora).astype(BF16)
        logw = -RW_DECAY_SCALE * _sigmoid(w0_ref[...] + _dot(lora_w, wup_ref[...]))
        a = _sigmoid(a0_ref[...] + _dot(lora_a, aup_ref[...]))
        kk = k * kk_ref[...]
        kp = k * (1.0 + (a - 1.0) * ka_ref[...])
        rkb = r * kp * rk_ref[...]

        lw_hi, lw_lo = _split_hi_lo(logw)
        cl = _dot(incl_b, lw_hi) + _dot(incl_b, lw_lo)
        p_in = jnp.exp(cl)
        pin_s[b] = p_in
        p_ex = jnp.exp(cl - logw)
        p_inv = jnp.exp(-cl)

        for p in range(RW_PAIRS):
            ls = slice(LANES * p, LANES * (p + 1))
            q = b * RW_PAIRS + p
            kkp = kk[:, ls]
            ssq = _head_sum(kkp * kkp, ones64, False)
            kkn = kkp * (1.0 / jnp.maximum(jnp.sqrt(ssq), 1e-12))
            kt_s[q] = (kkn * p_ex[:, ls]).astype(BF16)
            bt_s[q] = (kkn * a[:, ls] * p_inv[:, ls]).astype(BF16)
            kq_s[q] = (kp[:, ls] * p_inv[:, ls]).astype(BF16)
            rt_s[q] = (r[:, ls] * p_in[:, ls]).astype(BF16)
            v_s[q] = v[:, ls].astype(BF16)
            ya_s[q] = _head_sum(rkb[:, ls], ones64, False) * v[:, ls]

    fold0 = (_iota((blk, LANES), 0) % ch == _iota((blk, LANES), 1)).astype(BF16)
    fold1 = (_iota((blk, LANES), 0) % ch + ch == _iota((blk, LANES), 1)).astype(BF16)
    lane_lo = _iota((1, LANES), 1) < RW_HEAD

    def pair_body(p, carry):
        units = [b * RW_PAIRS + p for b in range(n_b)]
        heads = [(u, hh) for u in range(n_b) for hh in range(2)]
        kt = [kt_s[q] for q in units]
        rt = [rt_s[q] for q in units]
        vv = [v_s[q] for q in units]
        ya = [ya_s[q] for q in units]
        bt = [bt_s[q] for q in units]
        kq = [kq_s[q] for q in units]
        hb = blk // 2
        g = []
        for u, hh in heads:
            mine = lane_lo if hh == 0 else jnp.logical_not(lane_lo)
            zero = jnp.zeros_like(kt[u])
            ktm, rtm = jnp.where(mine, kt[u], zero), jnp.where(mine, rt[u], zero)
            halves = []
            for rows in (slice(0, hb), slice(hb, blk)):
                lhs = jnp.concatenate([ktm[rows], rtm[rows]], axis=0)
                rhs = jnp.concatenate([bt[u][rows], kq[u][rows]], axis=0)
                halves.append(_dot_nt(lhs, rhs))
            g.append(halves)
        zero_q = jnp.zeros((hb, hb), F32)

        def diag2(gh, r0, c0):
            a, b = gh[0][r0:r0 + hb, c0:c0 + hb], gh[1][r0:r0 + hb, c0:c0 + hb]
            return jnp.concatenate([jnp.concatenate([a, zero_q], axis=1), jnp.concatenate([zero_q, b], axis=1)],
                                   axis=0)

        n1 = [jnp.where(strict, -diag2(gh, 0, 0), 0.0).astype(BF16) for gh in g]
        a_kq = [jnp.where(strict, diag2(gh, 0, hb), 0.0).astype(BF16) for gh in g]
        a_rb = [jnp.where(incl, diag2(gh, hb, 0), 0.0).astype(BF16) for gh in g]
        a_rq = [jnp.where(incl, diag2(gh, hb, hb), 0.0).astype(BF16) for gh in g]
        n2 = [_dot(n, n).astype(BF16) for n in n1]
        akv = [_dot(a, vv[u]) for a, (u, _) in zip(a_kq, heads)]
        n4 = [_dot(n, n).astype(BF16) for n in n2]
        y_in = [_dot(a, vv[u]) for a, (u, _) in zip(a_rq, heads)]
        n8 = [_dot(n, n).astype(BF16) for n in n4]
        fold = [_dot(a, fold0 if hh == 0 else fold1) for a, (_, hh) in zip(a_rb, heads)]
        rhs_t = [jnp.concatenate([kt[u].astype(F32), av], axis=1) for av, (u, _) in zip(akv, heads)]
        for nk in (n8, n4, n2, n1):
            rhs_t = [x + _dot(n, x.astype(BF16)) for n, x in zip(nk, rhs_t)]
        for u, q in enumerate(units):
            h0, h1 = 2 * u, 2 * u + 1
            w_s[q] = jnp.where(lane_lo, rhs_t[h0][:, :LANES], rhs_t[h1][:, :LANES]).astype(BF16)
            u_s[q] = jnp.where(lane_lo, rhs_t[h0][:, LANES:], rhs_t[h1][:, LANES:])
            ya_s[q] = ya[u] + jnp.where(lane_lo, y_in[h0], y_in[h1])
            ab_s[q] = (fold[h0] + fold[h1]).astype(BF16)
        return carry

    lax.fori_loop(0, RW_PAIRS, pair_body, 0)

    blockdiag = (_iota((LANES, LANES), 0) // RW_HEAD) == (_iota((LANES, LANES), 1) // RW_HEAD)
    end_row = 0 if reverse else ch - 1

    def chunk_body(i, carry):
        c = (n_ch - 1 - i) if reverse else i
        rows = pl.ds(pl.multiple_of(c * ch, ch), ch)
        s_old = [st_ref[q] for q in range(n_units)]
        m1 = [_dot_nt(jnp.concatenate([w_s[q, rows, :], rt_s[q, rows, :]], axis=0), s_old[q].astype(BF16))
              for q in range(n_units)]
        zc = [-(m1[q][:ch] + u_s[q, rows, :]) for q in range(n_units)]
        upd = []
        for q in range(n_units):
            zv = jnp.concatenate([zc[q].astype(BF16), v_s[q, rows, :]], axis=0)
            bk = jnp.concatenate([bt_s[q, rows, :], kq_s[q, rows, :]], axis=0)
            upd.append(_dot_tn(zv, bk))
        yc = []
        for q in range(n_units):
            z2 = jnp.concatenate([jnp.where(lane_lo, zc[q], 0.0), jnp.where(lane_lo, 0.0, zc[q])], axis=0)
            yc.append(m1[q][ch:] + _dot(ab_s[q, rows, :][:, :2 * ch], z2.astype(BF16)) + ya_s[q, rows, :])
        for q in range(n_units):
            b, p = divmod(q, RW_PAIRS)
            p_end = pin_s[b, rows, LANES * p:LANES * (p + 1)][end_row:end_row + 1]
            st_ref[q] = (s_old[q] + jnp.where(blockdiag, upd[q], 0.0)) * p_end
            y_ref[b, rows, LANES * p:LANES * (p + 1)] = yc[q]
        return carry

    lax.fori_loop(0, n_ch, chunk_body, 0)


def _rwkv_direction(zall, prm, bsz, nc_blocks, nl_blocks, reverse):
    assert bsz == 2
    mu, w0, wup, a0, aup, k_k, k_a, r_k = prm
    rows = zall.shape[0]
    blk = RW_BLOCK
    steps = nc_blocks + nl_blocks
    n_blocks = rows // blk

    def local(s):
        lat = nc_blocks + ((nl_blocks - 1 - (s - nc_blocks)) if reverse else (s - nc_blocks))
        ctx = (nc_blocks - 1 - s) if reverse else s
        return jnp.where(s < nc_blocks, ctx, lat)

    def block(b, s):
        i = local(s)
        return jnp.where(i < nc_blocks, b * nc_blocks + i, bsz * nc_blocks + b * nl_blocks + i - nc_blocks)

    def halo(b, s):
        i = block(b, s)
        if reverse:
            return jnp.minimum((i + 1) * (blk // 8), n_blocks * (blk // 8) - 1)
        return jnp.maximum(i * (blk // 8) - 1, 0)

    vec = lambda w: pl.BlockSpec((1, w), lambda s: (0, 0))
    mat = lambda: pl.BlockSpec((LANES, RW_WIDTH), lambda s: (0, 0))
    n_units = bsz * RW_PAIRS
    unit_bf = pltpu.VMEM((n_units, blk, LANES), BF16)
    unit_f = pltpu.VMEM((n_units, blk, LANES), F32)
    z_spec = lambda b: pl.BlockSpec((blk, RW_Z), lambda s: (block(b, s), 0))
    halo_spec = lambda b: pl.BlockSpec((8, RW_Z), lambda s: (halo(b, s), 0))
    return pl.pallas_call(
        functools.partial(_rwkv_kernel, reverse=reverse, nc_blocks=nc_blocks),
        out_shape=jax.ShapeDtypeStruct((bsz, steps * blk, RW_WIDTH), F32),
        grid=(steps,),
        in_specs=[
            z_spec(0), z_spec(1), halo_spec(0), halo_spec(1),
            vec(RW_Z), vec(RW_WIDTH), mat(), vec(RW_WIDTH), mat(), vec(RW_WIDTH), vec(RW_WIDTH), vec(RW_WIDTH),
        ],
        out_specs=pl.BlockSpec((bsz, blk, RW_WIDTH), lambda s: (0, local(s), 0)),
        scratch_shapes=[
            pltpu.VMEM((n_units, LANES, LANES), F32),
            unit_bf, unit_bf, unit_bf, unit_bf, unit_bf,
            unit_bf, unit_f, unit_f, unit_bf,
            pltpu.VMEM((bsz, blk, RW_WIDTH), F32),
        ],
        compiler_params=_cp("arbitrary"),
        name="rwkv_bwd" if reverse else "rwkv_fwd",
    )(zall, zall, zall, zall, mu, w0, wup, a0, aup, k_k, k_a, r_k)


def _branches_kernel(saf_ref, sab_ref, su_ref, sd_ref, rf_ref, rb_ref, rg_ref, gup_ref, lw_ref, lb_ref,
                     of_ref, ob_ref, og_ref, ng_ref, ga_ref, gb_ref, gc_ref, wglu_ref, wrw_ref, wgl_ref, m_ref):
    za = _gelu_tanh(saf_ref[...] + sab_ref[...] + sd_ref[...] * su_ref[...].astype(F32)).astype(BF16)
    hid = _dot(za, wglu_ref[...])
    m = _sigmoid(ga_ref[...].astype(F32)) * (hid[:, :D_MODEL] * _sigmoid(hid[:, D_MODEL:]))
    ones64 = _head_ones(RW_HEAD)
    gate = _dot(_sigmoid(rg_ref[...]).astype(BF16), gup_ref[...])
    zb = []
    for p in range(RW_PAIRS):
        ls = slice(LANES * p, LANES * (p + 1))
        y = rf_ref[:, ls] + rb_ref[:, ls]
        mean = _head_sum(y, ones64, True) * (1.0 / RW_HEAD)
        yc = y - mean
        var = _head_sum(yc * yc, ones64, True) * (1.0 / RW_HEAD)
        yn = yc * lax.rsqrt(var + RW_GN_EPS) * lw_ref[:, ls] + lb_ref[:, ls]
        zb.append((yn * gate[:, ls]).astype(BF16))
    m = m + _sigmoid(gb_ref[...].astype(F32)) * _dot(jnp.concatenate(zb, axis=1), wrw_ref[...])
    zc = []
    for h in range(GLA_HEADS):
        vs = slice(GLA_DV * h, GLA_DV * (h + 1))
        o = of_ref[:, vs].astype(F32) + ob_ref[:, vs].astype(F32)
        on = o * lax.rsqrt(jnp.mean(o * o, axis=-1, keepdims=True) + NORM_EPS) * ng_ref[:, vs]
        zc.append((on * _silu(og_ref[:, vs].astype(F32))).astype(BF16))
    m = m + _sigmoid(gc_ref[...].astype(F32)) * _dot(jnp.concatenate(zc, axis=1), wgl_ref[...])
    m_ref[...] = m.astype(BF16)


def _branches(ya, yb, o_dir, u_s5, og, gates, zall, prm, row0, nc_blocks, nl_blocks):
    s5_d, g_up, ln_w, ln_b, norm_g, w_glu, w_rw, w_gl = prm
    bm = RW_BLOCK
    bsz = yb[0].shape[0]
    n = zall.shape[0] - row0 * bm
    rows = lambda w, blk=0: pl.BlockSpec((bm, w), lambda i: (i + row0, blk))

    def y_index(i):
        i = i + row0
        t = i - bsz * nc_blocks
        return (jnp.where(t < 0, i // nc_blocks, t // nl_blocks),
                jnp.where(t < 0, i % nc_blocks, nc_blocks + t % nl_blocks), 0)

    y_spec = pl.BlockSpec((None, bm, RW_WIDTH), y_index)
    vec = pl.BlockSpec((1, 1024), lambda i: (0, 0))
    const = lambda a: pl.BlockSpec(a.shape, lambda i: (0,) * a.ndim, pipeline_mode=pl.Buffered(1))
    return pl.pallas_call(
        _branches_kernel,
        out_shape=jax.ShapeDtypeStruct((n, D_MODEL), BF16),
        grid=(n // bm,),
        in_specs=[rows(1024), rows(1024), rows(1024, 0), vec,
                  y_spec, y_spec, rows(LANES, RW_Z // LANES), const(g_up), vec, vec,
                  rows(1024), rows(1024), rows(1024), vec,
                  rows(D_MODEL, 0), rows(D_MODEL, 1), rows(D_MODEL, 2),
                  const(w_glu), const(w_rw), const(w_gl)],
        out_specs=pl.BlockSpec((bm, D_MODEL), lambda i: (i, 0)),
        compiler_params=_cp("parallel"),
        name="branches",
    )(ya[0], ya[1], u_s5, s5_d, yb[0], yb[1], zall, g_up, ln_w, ln_b,
      o_dir[0], o_dir[1], og, norm_g, gates, gates, gates, w_glu, w_rw, w_gl)


def _out_proj_kernel(m_ref, w_ref, x_ref, gate_ref, o_ref):
    o_ref[...] = x_ref[...] + gate_ref[...] * _dot(m_ref[...], w_ref[...])


def _out_proj(m, w_out, x, mod, layer, bm, row0, nc, seq):
    n, d = m.shape
    bn = d
    mod_index = _mod_spec(layer, 2, bm, nc, seq).index_map
    return pl.pallas_call(
        _out_proj_kernel,
        out_shape=jax.ShapeDtypeStruct((n, d), F32),
        grid=(n // bm, d // bn),
        in_specs=[pl.BlockSpec((bm, d), lambda i, j: (i, 0)),
                  pl.BlockSpec((d, bn), lambda i, j: (0, j)),
                  pl.BlockSpec((bm, bn), lambda i, j: (i + row0, j)),
                  pl.BlockSpec((None, None, None, 1, bn), lambda i, j: mod_index(i + row0)[:4] + (j,))],
        out_specs=pl.BlockSpec((bm, bn), lambda i, j: (i, j)),
        compiler_params=_cp("parallel", "parallel"),
        name="out_proj",
    )(m, w_out, x, mod)


MOE_TM = 256
MOE_ROW_TILES = D_MODEL // (2 * LANES)
META_E1, META_E2, META_R1, META_R2, META_W1, META_W2 = range(6)


def _router_kernel(x_ref, g_ref, sh_ref, sc_ref, wr_hi_ref, wr_lo_ref, br_ref, v_ref, meta_ref, meta_t_ref, cnt_ref,
                   base_s):
    @pl.when(pl.program_id(0) == 0)
    def _():
        base_s[...] = jnp.zeros_like(base_s)

    x = x_ref[...]
    y = x * lax.rsqrt(jnp.mean(x * x, axis=-1, keepdims=True) + NORM_EPS) * g_ref[...]
    t = y * (1.0 + sc_ref[...]) + sh_ref[...]
    word = _pack_halves(t)
    for s in range(MOE_ROW_TILES):
        v_ref[pl.ds(s, x.shape[0], stride=MOE_ROW_TILES), :] = word[:, LANES * s:LANES * (s + 1)]
    t_hi, t_lo = _split_hi_lo(t)
    logits = (_dot(t_hi, wr_hi_ref[...]) + _dot(t_lo, wr_hi_ref[...]) + _dot(t_hi, wr_lo_ref[...])) + br_ref[...]
    lane = _iota(logits.shape, 1).astype(F32)
    neg = jnp.float32(-jnp.inf)
    big = jnp.float32(LANES)
    l1 = jnp.where(lane < MOE_GROUPS, logits, neg)
    m1 = jnp.max(l1, axis=-1, keepdims=True)
    p_top = 1.0 / jnp.sum(jnp.exp(l1 - m1), axis=-1, keepdims=True)
    grp = jnp.min(jnp.where(l1 == m1, lane, big), axis=-1, keepdims=True)
    lo = MOE_LANE0 + MOE_PER_GROUP * grp
    in_grp = (lane >= lo) & (lane < lo + MOE_PER_GROUP)
    l2 = jnp.where(in_grp, logits, neg)
    v1 = jnp.max(l2, axis=-1, keepdims=True)
    i1 = jnp.min(jnp.where(l2 == v1, lane, big), axis=-1, keepdims=True)
    l3 = jnp.where(lane == i1, neg, l2)
    v2 = jnp.max(l3, axis=-1, keepdims=True)
    i2 = jnp.min(jnp.where(l3 == v2, lane, big), axis=-1, keepdims=True)
    e2 = jnp.exp(v2 - v1)
    w1 = p_top / (1.0 + e2)
    w2 = p_top * e2 / (1.0 + e2)
    pick1 = lane == i1
    pick2 = lane == i2
    chosen = jnp.where(pick1 | pick2, 1.0, 0.0)
    bm = x.shape[0]
    earlier = (_iota((bm, bm), 1) < _iota((bm, bm), 0)).astype(BF16)
    before = _dot(earlier, chosen.astype(BF16)) + base_s[...]
    r1 = jnp.sum(jnp.where(pick1, before, 0.0), axis=-1, keepdims=True)
    r2 = jnp.sum(jnp.where(pick2, before, 0.0), axis=-1, keepdims=True)
    base_s[...] += jnp.sum(chosen, axis=0, keepdims=True)
    cnt_ref[...] = base_s[...]
    meta = jnp.zeros_like(logits)
    for slot, val in ((META_E1, i1 - MOE_LANE0), (META_E2, i2 - MOE_LANE0), (META_R1, r1), (META_R2, r2),
                      (META_W1, w1), (META_W2, w2)):
        meta = jnp.where(lane == slot, val, meta)
    meta_ref[...] = meta
    meta_t_ref[...] = meta.T[:8]


def _router(x, g, mod, layer, wr_hi, wr_lo, br, bm, nc, seq):
    n, d = x.shape
    return pl.pallas_call(
        _router_kernel,
        out_shape=(jax.ShapeDtypeStruct((n * MOE_ROW_TILES, LANES), jnp.uint32), jax.ShapeDtypeStruct((n, LANES), F32),
                   jax.ShapeDtypeStruct((8, n), F32), jax.ShapeDtypeStruct((1, LANES), F32)),
        grid=(n // bm,),
        in_specs=[
            pl.BlockSpec((bm, d), lambda i: (i, 0)),
            pl.BlockSpec((None, 1, d), lambda i: (layer, 0, 0)),
            _mod_spec(layer, 3, bm, nc, seq), _mod_spec(layer, 4, bm, nc, seq),
            pl.BlockSpec((d, LANES), lambda i: (0, 0)), pl.BlockSpec((d, LANES), lambda i: (0, 0)),
            pl.BlockSpec((1, LANES), lambda i: (0, 0)),
        ],
        out_specs=(pl.BlockSpec((bm * MOE_ROW_TILES, LANES), lambda i: (i, 0)),
                   pl.BlockSpec((bm, LANES), lambda i: (i, 0)),
                   pl.BlockSpec((8, bm), lambda i: (0, i)), pl.BlockSpec((1, LANES), lambda i: (0, 0))),
        scratch_shapes=[pltpu.VMEM((1, LANES), F32)],
        compiler_params=_cp("arbitrary"),
        name="moe_router",
    )(x, g, mod, mod, wr_hi, wr_lo, br)


def _moe_plan(meta_t, cnt):
    tm = MOE_TM
    n_tok = meta_t.shape[1]
    counts = cnt[0, MOE_LANE0:MOE_LANE0 + MOE_EXPERTS].astype(jnp.int32)
    seg = ((counts + tm - 1) // tm) * tm
    ends = jnp.cumsum(seg)
    off = ends - seg
    rec = meta_t[:4].astype(jnp.int32)
    first_row = jnp.sum(jnp.where(rec[:2, None, :] == jnp.arange(MOE_EXPERTS)[None, :, None], off[None, :, None], 0),
                        axis=1)
    dest = first_row + rec[2:4]
    n_rows = 2 * n_tok + MOE_EXPERTS * tm
    n_tiles = n_rows // tm
    tile_e = jnp.sum(((jnp.arange(n_tiles) * tm)[:, None] >= ends[None, :]).astype(jnp.int32), axis=1)
    tile_e = jnp.minimum(tile_e, MOE_EXPERTS - 1)
    tok = jnp.broadcast_to(jnp.arange(n_tok, dtype=jnp.int32), (2, n_tok))
    src = jnp.zeros((n_rows,), jnp.int32).at[dest.reshape(-1)].set(tok.reshape(-1))
    return dest[0], dest[1], src, tile_e, (ends[-1] // tm).reshape(1)


def _row_copy(src_hbm, row, dst, slot, r, sem):
    return pltpu.make_async_copy(src_hbm.at[pl.ds(row, 1)], dst.at[slot, pl.ds(r, 1)], sem)


def _experts_kernel(src_ref, te_ref, nu_ref, v_hbm, wg_ref, wu_ref, wd_ref, y_ref, xbuf, sem):
    del te_ref
    i = pl.program_id(0)
    n_used = nu_ref[0]
    slot = i % 2

    nt = MOE_ROW_TILES

    def token_copy(tok, into, r):
        return pltpu.make_async_copy(v_hbm.at[pl.ds(tok * nt, nt)], xbuf.at[into, pl.ds(r * nt, nt)], sem.at[into])

    def gather(tile, into):
        for r in range(MOE_TM):
            token_copy(src_ref[tile * MOE_TM + r], into, r).start(priority=r % 2)

    def expert():
        for r in range(MOE_TM):
            token_copy(0, slot, r).wait()
        word = jnp.concatenate([xbuf[slot, pl.ds(s, MOE_TM, stride=nt), :] for s in range(nt)], axis=1)
        t = jnp.concatenate(_unpack_halves(word), axis=1).astype(BF16)
        hid = _silu(_dot(t, wg_ref[...].astype(BF16))) * _dot(t, wu_ref[...].astype(BF16))
        y_ref[...] = _pack_halves(_dot(hid.astype(BF16), wd_ref[...].astype(BF16)))

    @pl.when((i == 0) & (n_used > 0))
    def _():
        gather(0, 0)

    @pl.when(i + 1 < n_used)
    def _():
        gather(i + 1, 1 - slot)
        expert()

    @pl.when(i + 1 == n_used)
    def _():
        expert()

    @pl.when(i >= n_used)
    def _():
        y_ref[...] = jnp.zeros_like(y_ref)


def _experts(v, src, tile_e, n_used, layer, w_gate, w_up, w_down):
    d = w_gate.shape[2]
    hdim = w_gate.shape[3]
    n_rows = src.shape[0]
    tm = MOE_TM
    by_expert = lambda i, src_r, te_r, nu_r: (layer, te_r[i], 0, 0)
    return pl.pallas_call(
        _experts_kernel,
        out_shape=jax.ShapeDtypeStruct((n_rows, d // 2), jnp.uint32),
        grid_spec=pltpu.PrefetchScalarGridSpec(
            num_scalar_prefetch=3,
            grid=(n_rows // tm,),
            in_specs=[
                pl.BlockSpec(memory_space=pl.ANY),
                pl.BlockSpec((None, None, d, hdim), by_expert),
                pl.BlockSpec((None, None, d, hdim), by_expert),
                pl.BlockSpec((None, None, hdim, d), by_expert),
            ],
            out_specs=pl.BlockSpec((tm, d // 2), lambda i, *_: (i, 0)),
            scratch_shapes=[pltpu.VMEM((2, tm * MOE_ROW_TILES, LANES), jnp.uint32), pltpu.SemaphoreType.DMA((2,))],
        ),
        compiler_params=_cp("arbitrary"),
        name="moe_experts",
    )(src, tile_e, n_used, v, w_gate, w_up, w_down)


def _combine_kernel(d1_ref, d2_ref, y_hbm, meta_ref, x_ref, gate_ref, gfin_ref, o_ref, buf1, buf2, sem, *, final):
    i = pl.program_id(0)
    bm = x_ref.shape[0]
    slot = i % 2

    def gather(tile, into):
        for r in range(bm):
            t = tile * bm + r
            _row_copy(y_hbm, d1_ref[t], buf1, into, r, sem.at[0, into]).start(priority=0)
            _row_copy(y_hbm, d2_ref[t], buf2, into, r, sem.at[1, into]).start(priority=1)

    @pl.when(i == 0)
    def _():
        gather(0, 0)

    @pl.when(i + 1 < pl.num_programs(0))
    def _():
        gather(i + 1, 1 - slot)

    for r in range(bm):
        _row_copy(y_hbm, 0, buf1, slot, r, sem.at[0, slot]).wait()
        _row_copy(y_hbm, 0, buf2, slot, r, sem.at[1, slot]).wait()
    meta = meta_ref[...]
    lane = _iota(meta.shape, 1)
    w1 = jnp.sum(jnp.where(lane == META_W1, meta, 0.0), axis=-1, keepdims=True)
    w2 = jnp.sum(jnp.where(lane == META_W2, meta, 0.0), axis=-1, keepdims=True)
    lo1, hi1 = _unpack_halves(buf1[slot])
    lo2, hi2 = _unpack_halves(buf2[slot])
    moe = jnp.concatenate([w1 * lo1 + w2 * lo2, w1 * hi1 + w2 * hi2], axis=1)
    out = x_ref[...] + gate_ref[...] * moe
    if final:
        out = out * lax.rsqrt(jnp.mean(out * out, axis=-1, keepdims=True) + NORM_EPS) * gfin_ref[...]
    o_ref[...] = out


def _combine(y, d1, d2, meta, x, mod, layer, bm, nc, seq, g_final, final):
    n, d = x.shape
    at_tile = lambda i, *_: (i, 0)
    mod_index = _mod_spec(layer, 5, bm, nc, seq).index_map
    return pl.pallas_call(
        functools.partial(_combine_kernel, final=final),
        out_shape=jax.ShapeDtypeStruct((n, d), F32),
        grid_spec=pltpu.PrefetchScalarGridSpec(
            num_scalar_prefetch=2,
            grid=(n // bm,),
            in_specs=[
                pl.BlockSpec(memory_space=pl.ANY),
                pl.BlockSpec((bm, LANES), at_tile),
                pl.BlockSpec((bm, d), at_tile),
                pl.BlockSpec((None, None, None, 1, d), lambda i, *_: mod_index(i)),
                pl.BlockSpec((1, d), lambda i, *_: (0, 0)),
            ],
            out_specs=pl.BlockSpec((bm, d), at_tile),
            scratch_shapes=[pltpu.VMEM((2, bm, d // 2), jnp.uint32), pltpu.VMEM((2, bm, d // 2), jnp.uint32),
                            pltpu.SemaphoreType.DMA((2, 2))],
        ),
        compiler_params=_cp("arbitrary"),
        name="moe_combine",
    )(d1, d2, y, meta, x, mod, g_final)


_COL = dict(s5=0, rw=S5_WIDTH, rg=S5_WIDTH + 3 * RW_WIDTH + 128, q=4352, k=4864, v=5376, gk=6400, og=6416, gates=7440)


def _pad_rows(w, rows):
    return jnp.pad(w, ((0, rows - w.shape[0]), (0, 0)))


def _layer(i, last, xa, mod, bsz, seq, ctx_len, p):
    nc = bsz * ctx_len
    rows = xa.shape[0]
    bm = 512
    msel = dict(nc=nc, seq=seq)
    u = _normmod(xa, p['g_norm1'].reshape(-1, 1, D_MODEL), mod, i, (0, 1), bm, **msel)

    w_in = p['w_in'][i]
    col = lambda a, w: w_in[:, a:a + w].astype(BF16)
    bm_in = 1088 if rows % 1088 == 0 else bm
    w_z = col(_COL['rw'], 3 * RW_WIDTH + 128 + 128)
    w_qk = col(_COL['q'], 2 * GLA_QK)
    w_v = col(_COL['v'], GLA_V)
    w_gk = jnp.pad(col(_COL['gk'], 16), ((0, 0), (0, LANES - 16)))
    u_s5 = _mm(u, col(0, S5_WIDTH), bm_in, 1024, BF16)
    og = _mm(u, col(_COL['og'], GLA_V), bm_in, 1024, BF16)
    gates = _mm(u, col(_COL['gates'], 3 * D_MODEL), bm_in, 1024, BF16)
    zall = _mm(u, w_z, bm_in, 1664)
    qk = _mm(u, w_qk, bm_in, 1024, BF16)
    vv = _mm(u, w_v, bm_in, 1024, BF16)
    gk = _mm(u, w_gk, bm_in, LANES, BF16)

    ya = []
    for d in range(2):
        tables = _s5_tables(*(p[k][i, d] for k in ('s5_a_re', 's5_a_im', 's5_log_dt', 's5_b_re', 's5_b_im',
                                                   's5_c_re', 's5_c_im')))
        ya.append(_s5_scan(u_s5, tables, bsz, ctx_len // S5_BLOCK, seq // S5_BLOCK, bool(d)))

    yb = []
    for d in range(2):
        prm = (
            jnp.pad(p['rw_mu'][i, d], (0, RW_Z - p['rw_mu'].shape[-1])).reshape(1, RW_Z),
            p['rw_w0'][i, d].reshape(1, -1),
            _pad_rows(p['rw_w_up'][i, d], LANES).astype(BF16),
            p['rw_a0'][i, d].reshape(1, -1),
            jnp.pad(p['rw_a_up'][i, d], ((RW_HEAD, 0), (0, 0))).astype(BF16),
            p['rw_k_k'][i].reshape(1, -1), p['rw_k_a'][i].reshape(1, -1), p['rw_r_k'][i].reshape(1, -1),
        )
        yb.append(_rwkv_direction(zall, prm, bsz, ctx_len // RW_BLOCK, seq // RW_BLOCK, bool(d)))

    gk_up = jnp.pad(p['gl_gk_up'][i], ((0, 0), (0, LANES - 16), (0, 0))).astype(BF16)
    gk_b = p['gl_gk_b'][i].reshape(2, 1, GLA_QK)
    s0 = jnp.zeros((2, bsz, GLA_HEADS, GLA_DV, GLA_DK), F32)
    oc, s_ctx = _gla(_to_chunk_major(qk[:nc], bsz), _to_chunk_major(vv[:nc], bsz), _to_chunk_major(gk[:nc], bsz),
                     gk_up, gk_b, s0)
    rows_l = seq // GRID_W
    lat = lambda t: t[nc:].reshape(bsz, rows_l, GRID_W * t.shape[1])
    ol, _ = _gla(lat(qk), lat(vv), lat(gk), gk_up, gk_b, s_ctx)
    o_dir = [jnp.concatenate([_from_chunk_major(oc[d], GLA_V).reshape(nc, GLA_V),
                              ol[d].reshape(bsz * seq, GLA_V)], axis=0) for d in range(2)]

    row0 = (nc // bm) if last else 0
    branch_prm = (p['s5_d'][i].reshape(1, -1), p['rw_g_up'][i].astype(BF16), p['rw_ln_w'][i].reshape(1, -1),
                  p['rw_ln_b'][i].reshape(1, -1), p['gl_norm_g'][i].reshape(1, -1), p['s5_w_glu'][i].astype(BF16),
                  p['rw_w_proj'][i].astype(BF16), p['gl_w_proj'][i].astype(BF16))
    merged = _branches(ya, yb, o_dir, u_s5, og, gates, zall, branch_prm, row0 * (bm // RW_BLOCK), ctx_len // RW_BLOCK,
                       seq // RW_BLOCK)
    xm = _out_proj(merged, p['w_out'][i].astype(BF16), xa, mod, i, bm, row0, nc, seq)
    if last:
        msel = dict(nc=0, seq=seq)

    wr = jnp.pad(jnp.concatenate([p['moe_wg1'][i], p['moe_wg2'][i]], axis=1), ((0, 0), (0, LANES - 36)))
    wr_hi = wr.astype(BF16)
    wr_lo = (wr - wr_hi.astype(F32)).astype(BF16)
    br = jnp.pad(jnp.concatenate([p['moe_bg1'][i], p['moe_bg2'][i]]), (0, LANES - 36)).reshape(1, LANES)
    vmoe, meta, meta_t, cnt = _router(xm, p['g_norm2'].reshape(-1, 1, D_MODEL), mod, i, wr_hi, wr_lo, br, bm, **msel)
    d1, d2, src, tile_e, n_used = _moe_plan(meta_t, cnt)
    y_sorted = _experts(vmoe, src, tile_e, n_used, i, p['moe_w_gate'], p['moe_w_up'], p['moe_w_down'])
    return _combine(y_sorted, d1, d2, meta, xm, mod, i, bm // 2, g_final=p['g_final'], final=last, **msel)


def kernel(x, c, ctx, c_ctx, w_mod, b_mod, g_norm1, g_norm2, w_in, s5_a_re, s5_a_im, s5_log_dt, s5_b_re, s5_b_im,
           s5_c_re, s5_c_im, s5_d, s5_w_glu, rw_mu, rw_w0, rw_w_up, rw_a0, rw_a_up, rw_k_k, rw_k_a, rw_r_k, rw_g_up,
           rw_ln_w, rw_ln_b, rw_w_proj, gl_gk_up, gl_gk_b, gl_norm_g, gl_w_proj, w_out, moe_wg1, moe_bg1, moe_wg2,
           moe_bg2, moe_w_gate, moe_w_up, moe_w_down, g_final):
    p = dict(g_norm1=g_norm1, g_norm2=g_norm2, w_in=w_in, s5_a_re=s5_a_re, s5_a_im=s5_a_im, s5_log_dt=s5_log_dt,
             s5_b_re=s5_b_re, s5_b_im=s5_b_im, s5_c_re=s5_c_re, s5_c_im=s5_c_im, s5_d=s5_d, s5_w_glu=s5_w_glu,
             rw_mu=rw_mu, rw_w0=rw_w0, rw_w_up=rw_w_up, rw_a0=rw_a0, rw_a_up=rw_a_up, rw_k_k=rw_k_k, rw_k_a=rw_k_a,
             rw_r_k=rw_r_k.reshape(rw_r_k.shape[0], -1), rw_g_up=rw_g_up, rw_ln_w=rw_ln_w, rw_ln_b=rw_ln_b,
             rw_w_proj=rw_w_proj, gl_gk_up=gl_gk_up, gl_gk_b=gl_gk_b, gl_norm_g=gl_norm_g, gl_w_proj=gl_w_proj,
             w_out=w_out, moe_wg1=moe_wg1, moe_bg1=moe_bg1, moe_wg2=moe_wg2, moe_bg2=moe_bg2, moe_w_gate=moe_w_gate,
             moe_w_up=moe_w_up, moe_w_down=moe_w_down, g_final=g_final.reshape(1, -1))
    bsz, seq, d = x.shape
    ctx_len = ctx.shape[1]
    depth = w_mod.shape[0]
    cc = jnp.concatenate([c, c_ctx[None], jnp.zeros((8 - bsz - 1, d), F32)], axis=0)
    mod = _adaln(cc, w_mod, b_mod).reshape(depth, 8, 6, 1, d)
    xa = jnp.concatenate([ctx.reshape(bsz * ctx_len, d), x.reshape(bsz * seq, d)], axis=0)
    for i in range(depth):
        xa = _layer(i, i == depth - 1, xa, mod, bsz, seq, ctx_len, p)
    return xa.reshape(bsz, seq, d)
```

```python
import functools
import math

import jax
import jax.numpy as jnp
from jax import lax
from jax.experimental import pallas as pl
from jax.experimental.pallas import tpu as pltpu

F32 = jnp.float32
BF16 = jnp.bfloat16

D_MODEL = 2048
GRID_W = 64
NORM_EPS = 1e-6

S5_WIDTH = 1024
S5_GROUP = 16
S5_GROUPS = 64
S5_STATE = 64
S5_MAX_RE = -1e-4
S5_TILE = 16
S5_PAIRS = S5_GROUPS // 2

RW_WIDTH = 1024
RW_HEAD = 64
RW_DECAY_SCALE = 0.606531
RW_GN_EPS = 64e-5
RW_BLOCK = 256
RW_CHUNK = 16
RW_PAIRS = RW_WIDTH // 128
RW_Z = 3 * RW_WIDTH + 128

GLA_HEADS = 4
GLA_DK = 128
GLA_DV = 256
GLA_QK = 512
GLA_V = 1024
GLA_TAU = 16.0
GLA_CHUNK = 64

MOE_GROUPS = 4
MOE_PER_GROUP = 8
MOE_EXPERTS = 32
MOE_HIDDEN = 256
MOE_LANE0 = MOE_GROUPS

LANES = 128
VMEM_LIMIT = 56 * 1024 * 1024


def _cp(*sem):
    return pltpu.CompilerParams(dimension_semantics=sem, vmem_limit_bytes=VMEM_LIMIT)


def _dot(a, b):
    return jnp.dot(a, b, preferred_element_type=F32)


def _dot_nt(a, b):
    return lax.dot_general(a, b, (((1,), (1,)), ((), ())), preferred_element_type=F32)


def _dot_tn(a, b):
    return lax.dot_general(a, b, (((0,), (0,)), ((), ())), preferred_element_type=F32)


def _sigmoid(x):
    return 0.5 * jnp.tanh(0.5 * x) + 0.5


def _silu(x):
    return x * _sigmoid(x)


def _gelu_tanh(x):
    return 0.5 * x * (1.0 + jnp.tanh(math.sqrt(2.0 / math.pi) * (x + 0.044715 * (x * x * x))))


def _split_hi_lo(x):
    hi = x.astype(BF16)
    lo = (x - hi.astype(F32)).astype(BF16)
    return hi, lo


def _pack_halves(x):
    bits = lax.bitcast_convert_type(x.astype(BF16).astype(F32), jnp.uint32)
    n = x.shape[1] // 2
    return (bits[:, n:] & jnp.uint32(0xFFFF0000)) | (bits[:, :n] >> 16)


def _unpack_halves(word):
    return (lax.bitcast_convert_type(word << 16, F32),
            lax.bitcast_convert_type(word & jnp.uint32(0xFFFF0000), F32))


def _iota(shape, dim):
    return lax.broadcasted_iota(jnp.int32, shape, dim)


def _head_ones(width):
    return (_iota((LANES, LANES), 0) // width == _iota((LANES, LANES), 1) // width).astype(BF16)


def _head_sum(x, ones, exact):
    if exact:
        hi, lo = _split_hi_lo(x)
        return _dot(hi, ones) + _dot(lo, ones)
    return _dot(x.astype(BF16), ones)


def _adaln_kernel(c_ref, w_ref, b_ref, o_ref):
    c = c_ref[...]
    o_ref[...] = _dot(_silu(c).astype(BF16), w_ref[...].astype(BF16)) + b_ref[...]


def _adaln(cc, w_mod, b_mod):
    depth, d, n = w_mod.shape
    bn = 1536
    return pl.pallas_call(
        _adaln_kernel,
        out_shape=jax.ShapeDtypeStruct((depth, 8, n), F32),
        grid=(depth, n // bn),
        in_specs=[
            pl.BlockSpec((8, d), lambda l, j: (0, 0)),
            pl.BlockSpec((None, d, bn), lambda l, j: (l, 0, j)),
            pl.BlockSpec((None, 1, bn), lambda l, j: (l, 0, j)),
        ],
        out_specs=pl.BlockSpec((None, 8, bn), lambda l, j: (l, 0, j)),
        compiler_params=_cp("parallel", "parallel"),
        name="adaln",
    )(cc, w_mod, b_mod.reshape(depth, 1, n))


def _mod_spec(layer, part, bm, nc, seq):
    def index(i, *_):
        r0 = i * bm
        return (layer, jnp.where(r0 < nc, 2, (r0 - nc) // seq), part, 0, 0)
    return pl.BlockSpec((None, None, None, 1, D_MODEL), index)


def _normmod_kernel(x_ref, g_ref, sh_ref, sc_ref, o_ref):
    x = x_ref[...]
    y = x * lax.rsqrt(jnp.mean(x * x, axis=-1, keepdims=True) + NORM_EPS) * g_ref[...]
    o_ref[...] = (y * (1.0 + sc_ref[...]) + sh_ref[...]).astype(o_ref.dtype)


def _normmod(x, g, mod, layer, parts, bm, nc, seq):
    n, d = x.shape
    return pl.pallas_call(
        _normmod_kernel,
        out_shape=jax.ShapeDtypeStruct((n, d), BF16),
        grid=(n // bm,),
        in_specs=[
            pl.BlockSpec((bm, d), lambda i: (i, 0)),
            pl.BlockSpec((None, 1, d), lambda i: (layer, 0, 0)),
            _mod_spec(layer, parts[0], bm, nc, seq),
            _mod_spec(layer, parts[1], bm, nc, seq),
        ],
        out_specs=pl.BlockSpec((bm, d), lambda i: (i, 0)),
        compiler_params=_cp("parallel"),
        name="normmod",
    )(x, g, mod, mod)


def _mm_kernel(x_ref, w_ref, o_ref):
    o_ref[...] = _dot(x_ref[...], w_ref[...]).astype(o_ref.dtype)


def _mm(x, w, bm, bn, out_dtype=F32):
    m, k = x.shape
    n = w.shape[1]
    return pl.pallas_call(
        _mm_kernel,
        out_shape=jax.ShapeDtypeStruct((m, n), out_dtype),
        grid=(m // bm, n // bn),
        in_specs=[pl.BlockSpec((bm, k), lambda i, j: (i, 0)), pl.BlockSpec((k, bn), lambda i, j: (0, j))],
        out_specs=pl.BlockSpec((bm, bn), lambda i, j: (i, j)),
        compiler_params=_cp("parallel", "parallel"),
        name="mm",
    )(x, w)


def _seq_blocks(bsz, nc_blocks, nl_blocks, reverse):
    def local(s):
        lat = nc_blocks + ((nl_blocks - 1 - (s - nc_blocks)) if reverse else (s - nc_blocks))
        ctx = (nc_blocks - 1 - s) if reverse else s
        return jnp.where(s < nc_blocks, ctx, lat)

    def block(b, s):
        i = local(s)
        return jnp.where(i < nc_blocks, b * nc_blocks + i, bsz * nc_blocks + b * nl_blocks + i - nc_blocks)

    return local, block


S5_PACK = 8
S5_PACKS = S5_GROUPS // S5_PACK
S5_PLANE = S5_PACK * S5_STATE
S5_BLOCK = 256
S5_ROWS = 24


def _s5_tables(a_re, a_im, log_dt, b_re, b_im, c_re, c_im):
    lam = lax.complex(jnp.minimum(a_re, S5_MAX_RE), a_im)
    ldt = lam * jnp.exp(log_dt)[:, None]
    lam_bar = jnp.exp(ldt)
    b_bar = ((lam_bar - 1.0) / lam)[..., None] * lax.complex(b_re, b_im)
    c_mat = lax.complex(c_re, c_im)
    eye = jnp.eye(S5_PACK, dtype=F32)

    def block_diag(t):
        k, g, a, b = t.shape
        return (t[:, :, :, None, :] * eye[None, :, None, :, None]).reshape(k, g * a, g * b)

    b_t = jnp.transpose(b_bar, (0, 2, 1)).reshape(S5_PACKS, S5_PACK, S5_GROUP, S5_STATE)
    bblk = jnp.concatenate([block_diag(b_t.real), block_diag(b_t.imag)], axis=2)
    c_t = jnp.transpose(c_mat, (0, 2, 1)).reshape(S5_PACKS, S5_PACK, S5_STATE, S5_GROUP)
    cblk = jnp.concatenate([block_diag(c_t.real), block_diag(-c_t.imag)], axis=1)
    expo = jnp.concatenate([jnp.arange(1, S5_TILE + 1, dtype=F32), jnp.asarray([32.0, 64.0, 128.0], F32),
                            jnp.zeros((S5_ROWS - S5_TILE - 3,), F32)])
    pw = jnp.exp(ldt[None] * expo[:, None, None]).reshape(S5_ROWS, S5_PACKS, S5_PLANE)
    pw = jnp.transpose(pw, (1, 0, 2))
    return bblk.astype(BF16), cblk.astype(BF16), pw.real, pw.imag


def _s5_scan_kernel(u_ref, bblk_ref, cblk_ref, pre_ref, pim_ref, y_ref, car_ref, up_s, h2_s, hb2_s, yp_s, *, reverse):
    t = S5_TILE
    n_t = S5_BLOCK // t
    pn = S5_PLANE

    @pl.when(pl.program_id(1) == 0)
    def _():
        car_ref[...] = jnp.zeros_like(car_ref)

    ra = _iota((S5_BLOCK, S5_BLOCK), 0)
    cb = _iota((S5_BLOCK, S5_BLOCK), 1)
    perm = ((ra // t == cb % t) & (ra % t == cb // t)).astype(BF16)
    up = _dot(perm, u_ref[...].astype(BF16)).astype(BF16)
    for pk in range(S5_PACKS):
        up_s[pk] = up[:, LANES * pk:LANES * (pk + 1)]
    order = list(range(t - 1, -1, -1)) if reverse else list(range(t))
    rowj = _iota((n_t, 1), 0)

    def states(pk, bu, h_s, hb_s):
        pre = pre_ref[pk]
        pim = pim_ref[pk]
        l_re, l_im = pre[0:1], pim[0:1]
        h_re = h_im = None
        for n, s in enumerate(order):
            rows = slice(t * s, t * (s + 1))
            b_re, b_im = bu[rows, :pn], bu[rows, pn:]
            if n == 0:
                h_re, h_im = b_re, b_im
            else:
                h_re, h_im = l_re * h_re - l_im * h_im + b_re, l_re * h_im + l_im * h_re + b_im
            h_s[rows, :pn] = h_re
            h_s[rows, pn:] = h_im
        c_re, c_im = car_ref[pk, 0:1, :pn], car_ref[pk, 0:1, pn:]
        first = rowj == (n_t - 1 if reverse else 0)
        g_re, g_im = pre[t - 1:t], pim[t - 1:t]
        e_re = h_re + jnp.where(first, g_re * c_re - g_im * c_im, 0.0)
        e_im = h_im + jnp.where(first, g_re * c_im + g_im * c_re, 0.0)
        step = 1
        for row in (t - 1, t, t + 1, t + 2):
            if reverse:
                s_re, s_im, ok = pltpu.roll(e_re, n_t - step, 0), pltpu.roll(e_im, n_t - step, 0), rowj < n_t - step
            else:
                s_re, s_im, ok = pltpu.roll(e_re, step, 0), pltpu.roll(e_im, step, 0), rowj >= step
            a_re, a_im = pre[row:row + 1], pim[row:row + 1]
            e_re = e_re + jnp.where(ok, a_re * s_re - a_im * s_im, 0.0)
            e_im = e_im + jnp.where(ok, a_re * s_im + a_im * s_re, 0.0)
            step *= 2
        last = 0 if reverse else n_t - 1
        car_ref[pk, 0:1, :pn] = e_re[last:last + 1]
        car_ref[pk, 0:1, pn:] = e_im[last:last + 1]
        if reverse:
            in_re = jnp.where(first, c_re, pltpu.roll(e_re, n_t - 1, 0))
            in_im = jnp.where(first, c_im, pltpu.roll(e_im, n_t - 1, 0))
        else:
            in_re = jnp.where(first, c_re, pltpu.roll(e_re, 1, 0))
            in_im = jnp.where(first, c_im, pltpu.roll(e_im, 1, 0))
        for n, s in enumerate(order):
            rows = slice(t * s, t * (s + 1))
            a_re, a_im = pre[n:n + 1], pim[n:n + 1]
            hb_s[rows, :pn] = (h_s[rows, :pn] + a_re * in_re - a_im * in_im).astype(BF16)
            hb_s[rows, pn:] = (h_s[rows, pn:] + a_re * in_im + a_im * in_re).astype(BF16)

    def pack_pair(j, carry):
        pks = (2 * j, 2 * j + 1)
        bu = [_dot(up_s[pk], bblk_ref[pk]) for pk in pks]
        for u, pk in enumerate(pks):
            states(pk, bu[u], h2_s.at[u], hb2_s.at[u])
            yp_s[pk] = _dot(hb2_s[u], cblk_ref[pk]).astype(BF16)
        return carry

    lax.fori_loop(0, S5_PACKS // 2, pack_pair, 0)
    for pk in range(S5_PACKS):
        y_ref[:, LANES * pk:LANES * (pk + 1)] = _dot(perm, yp_s[pk])


def _s5_scan(u_src, tables, bsz, nc_blocks, nl_blocks, reverse):
    bblk, cblk, pre, pim = tables
    rows = u_src.shape[0]
    blk = S5_BLOCK
    _, block = _seq_blocks(bsz, nc_blocks, nl_blocks, reverse)
    whole = lambda a: pl.BlockSpec(a.shape, lambda b, s: (0,) * a.ndim)
    return pl.pallas_call(
        functools.partial(_s5_scan_kernel, reverse=reverse),
        out_shape=jax.ShapeDtypeStruct((rows, S5_WIDTH), F32),
        grid=(bsz, nc_blocks + nl_blocks),
        in_specs=[pl.BlockSpec((blk, S5_WIDTH), lambda b, s: (block(b, s), 0)),
                  whole(bblk), whole(cblk), whole(pre), whole(pim)],
        out_specs=pl.BlockSpec((blk, S5_WIDTH), lambda b, s: (block(b, s), 0)),
        scratch_shapes=[
            pltpu.VMEM((S5_PACKS, 8, 2 * S5_PLANE), F32),
            pltpu.VMEM((S5_PACKS, blk, LANES), BF16),
            pltpu.VMEM((2, blk, 2 * S5_PLANE), F32),
            pltpu.VMEM((2, blk, 2 * S5_PLANE), BF16),
            pltpu.VMEM((S5_PACKS, blk, LANES), BF16),
        ],
        compiler_params=_cp("parallel", "arbitrary"),
        name="s5_bwd" if reverse else "s5_fwd",
    )(u_src, bblk, cblk, pre, pim)


def _gla_kernel(q_ref, k_ref, v_ref, gk_ref, up_ref, gb_ref, s0_ref, o_ref, sf_ref, st_ref):
    d = pl.program_id(0)
    c = pl.program_id(1)
    n = pl.num_programs(1)

    @pl.when(c == 0)
    def _():
        st_ref[...] = s0_ref[...]

    cs = GLA_CHUNK
    nb = q_ref.shape[0]
    sign = 1 - 2 * d
    row = _iota((cs, cs), 0)
    col = _iota((cs, cs), 1)
    causal = (row - col) * sign >= 0
    causal_b = causal.astype(BF16)
    rid = _iota((cs, 1), 0)
    bcum, b_mid, b_end = [], [], []
    for b in range(nb):
        x = _dot(gk_ref[b].astype(BF16), up_ref[...]) + gb_ref[...]
        log_a = (jnp.minimum(x, 0.0) - jnp.log(1.0 + jnp.exp(-jnp.abs(x)))) * (1.0 / GLA_TAU)
        la_hi, la_lo = _split_hi_lo(log_a)
        bc = _dot(causal_b, la_hi) + _dot(causal_b, la_lo)
        bcum.append(bc)
        b_mid.append(jnp.sum(jnp.where(rid == cs // 2 - d, bc, 0.0), axis=0, keepdims=True))
        b_end.append(jnp.sum(jnp.where(rid == (cs - 1) * (1 - d), bc, 0.0), axis=0, keepdims=True))
    units = [(b, h) for b in range(nb) for h in range(GLA_HEADS)]
    ks = lambda h: slice(GLA_DK * h, GLA_DK * (h + 1))
    vs = lambda h: slice(GLA_DV * h, GLA_DV * (h + 1))
    s_old = [st_ref[b, h] for b, h in units]
    qh = [q_ref[b, :, ks(h)].astype(F32) * (GLA_DK ** -0.5) for b, h in units]
    kh = [k_ref[b, :, ks(h)].astype(F32) for b, h in units]
    vh = [v_ref[b, :, vs(h)].astype(BF16) for b, h in units]
    bh = [bcum[b][:, ks(h)] for b, h in units]
    mid = [b_mid[b][:, ks(h)] for b, h in units]
    end = [b_end[b][:, ks(h)] for b, h in units]
    scores = [_dot_nt((q * jnp.exp(x - m)).astype(BF16), (k * jnp.exp(m - x)).astype(BF16))
              for q, k, x, m in zip(qh, kh, bh, mid)]
    inter = [_dot_nt((q * jnp.exp(x)).astype(BF16), s.astype(BF16)) for q, x, s in zip(qh, bh, s_old)]
    upd = [_dot_tn(v, (k * jnp.exp(e - x)).astype(BF16)) for v, k, e, x in zip(vh, kh, end, bh)]
    intra = [_dot(jnp.where(causal, sc, 0.0).astype(BF16), v) for sc, v in zip(scores, vh)]
    for u, (b, h) in enumerate(units):
        o_ref[b, :, vs(h)] = (intra[u] + inter[u]).astype(o_ref.dtype)
        st_ref[b, h] = s_old[u] * jnp.exp(end[u]) + upd[u]

    @pl.when(c == n - 1)
    def _():
        sf_ref[...] = st_ref[...]


def _gla(qk, v, gk, gk_up, gk_b, s0):
    bsz, cs, w = v.shape
    n = w // GLA_V

    def chunk(d, c):
        return c + d * (n - 1 - 2 * c)

    state_spec = pl.BlockSpec((None, bsz, GLA_HEADS, GLA_DV, GLA_DK), lambda d, c: (d, 0, 0, 0, 0))
    return pl.pallas_call(
        _gla_kernel,
        out_shape=(jax.ShapeDtypeStruct((2, bsz, cs, w), BF16), jax.ShapeDtypeStruct(s0.shape, F32)),
        grid=(2, n),
        in_specs=[
            pl.BlockSpec((bsz, cs, GLA_QK), lambda d, c: (0, 0, 2 * chunk(d, c))),
            pl.BlockSpec((bsz, cs, GLA_QK), lambda d, c: (0, 0, 2 * chunk(d, c) + 1)),
            pl.BlockSpec((bsz, cs, GLA_V), lambda d, c: (0, 0, chunk(d, c))),
            pl.BlockSpec((bsz, cs, LANES), lambda d, c: (0, 0, chunk(d, c))),
            pl.BlockSpec((None, LANES, GLA_QK), lambda d, c: (d, 0, 0)),
            pl.BlockSpec((None, 1, GLA_QK), lambda d, c: (d, 0, 0)),
            state_spec,
        ],
        out_specs=(pl.BlockSpec((None, bsz, cs, GLA_V), lambda d, c: (d, 0, 0, chunk(d, c))), state_spec),
        scratch_shapes=[pltpu.VMEM((bsz, GLA_HEADS, GLA_DV, GLA_DK), F32)],
        compiler_params=_cp("parallel", "arbitrary"),
        name="gla",
    )(qk, qk, v, gk, gk_up, gk_b, s0)


def _to_chunk_major(t, bsz):
    n = t.shape[0] // bsz
    d = t.shape[1]
    t = t.reshape(bsz, n // GLA_CHUNK, GLA_CHUNK, d)
    return jnp.transpose(t, (0, 2, 1, 3)).reshape(bsz, GLA_CHUNK, (n // GLA_CHUNK) * d)


def _from_chunk_major(t, d):
    lead = t.shape[:-2]
    n = t.shape[-1] // d
    t = t.reshape(lead + (GLA_CHUNK, n, d))
    return jnp.swapaxes(t, -3, -2).reshape(lead + (n * GLA_CHUNK, d))


def _rwkv_kernel(z0_ref, z1_ref, halo0_ref, halo1_ref, mu_ref, w0_ref, wup_ref, a0_ref, aup_ref, kk_ref, ka_ref,
                 rk_ref, y_ref, st_ref, kt_s, bt_s, kq_s, rt_s, v_s, w_s, u_s, ya_s, ab_s, pin_s,
                 *, reverse, nc_blocks):
    step = pl.program_id(0)
    blk = RW_BLOCK
    ch = RW_CHUNK
    n_ch = blk // ch
    n_b = 2
    n_units = n_b * RW_PAIRS

    @pl.when(step == 0)
    def _():
        st_ref[...] = jnp.zeros_like(st_ref)

    rowi = _iota((blk, 1), 0)
    low = _iota((1, LANES), 1) < RW_HEAD
    ones64 = _head_ones(RW_HEAD)
    seq_start = (step == 0) | (step == nc_blocks)
    rr = _iota((blk, blk), 0)
    cc = _iota((blk, blk), 1)
    same = rr // ch == cc // ch
    before = (cc > rr) if reverse else (cc < rr)
    strict = same & before
    incl = same & (before | (rr == cc))
    incl_b = incl.astype(BF16)

    for b, (z_ref, halo_ref) in enumerate(((z0_ref, halo0_ref), (z1_ref, halo1_ref))):
        z = z_ref[...]
        if reverse:
            prev = pltpu.roll(z, blk - 1, 0)
            edge = halo_ref[0:1, :]
            at_edge = rowi == blk - 1
        else:
            prev = pltpu.roll(z, 1, 0)
            edge = halo_ref[7:8, :]
            at_edge = rowi == 0
        prev = jnp.where(at_edge, jnp.where(seq_start, 0.0, edge), prev)
        zs = z + (prev - z) * mu_ref[...]
        r = zs[:, 0:RW_WIDTH]
        k = zs[:, RW_WIDTH:2 * RW_WIDTH]
        v = zs[:, 2 * RW_WIDTH:3 * RW_WIDTH]
        lora = zs[:, 3 * RW_WIDTH:3 * RW_WIDTH + LANES]
        lora_w = jnp.where(low, jnp.tanh(lora), 0.0).astype(BF16)
        lora_a = jnp.where(low, 0.0, lora).astype(BF16)
        logw = -RW_DECAY_SCALE * _sigmoid(w0_ref[...] + _dot(lora_w, wup_ref[...]))
        a = _sigmoid(a0_ref[...] + _dot(lora_a, aup_ref[...]))
        kk = k * kk_ref[...]
        kp = k * (1.0 + (a - 1.0) * ka_ref[...])
        rkb = r * kp * rk_ref[...]

        lw_hi, lw_lo = _split_hi_lo(logw)
        cl = _dot(incl_b, lw_hi) + _dot(incl_b, lw_lo)
        p_in = jnp.exp(cl)
        pin_s[b] = p_in
        p_ex = jnp.exp(cl - logw)
        p_inv = jnp.exp(-cl)

        for p in range(RW_PAIRS):
            ls = slice(LANES * p, LANES * (p + 1))
            q = b * RW_PAIRS + p
            kkp = kk[:, ls]
            ssq = _head_sum(kkp * kkp, ones64, False)
            kkn = kkp * (1.0 / jnp.maximum(jnp.sqrt(ssq), 1e-12))
            kt_s[q] = (kkn * p_ex[:, ls]).astype(BF16)
            bt_s[q] = (kkn * a[:, ls] * p_inv[:, ls]).astype(BF16)
            kq_s[q] = (kp[:, ls] * p_inv[:, ls]).astype(BF16)
            rt_s[q] = (r[:, ls] * p_in[:, ls]).astype(BF16)
            v_s[q] = v[:, ls].astype(BF16)
            ya_s[q] = _head_sum(rkb[:, ls], ones64, False) * v[:, ls]

    fold0 = (_iota((blk, LANES), 0) % ch == _iota((blk, LANES), 1)).astype(BF16)
    fold1 = (_iota((blk, LANES), 0) % ch + ch == _iota((blk, LANES), 1)).astype(BF16)
    lane_lo = _iota((1, LANES), 1) < RW_HEAD

    def pair_body(p, carry):
        units = [b * RW_PAIRS + p for b in range(n_b)]
        heads = [(u, hh) for u in range(n_b) for hh in range(2)]
        kt = [kt_s[q] for q in units]
        rt = [rt_s[q] for q in units]
        vv = [v_s[q] for q in units]
        ya = [ya_s[q] for q in units]
        bt = [bt_s[q] for q in units]
        kq = [kq_s[q] for q in units]
        hb = blk // 2
        g = []
        for u, hh in heads:
            mine = lane_lo if hh == 0 else jnp.logical_not(lane_lo)
            zero = jnp.zeros_like(kt[u])
            ktm, rtm = jnp.where(mine, kt[u], zero), jnp.where(mine, rt[u], zero)
            halves = []
            for rows in (slice(0, hb), slice(hb, blk)):
                lhs = jnp.concatenate([ktm[rows], rtm[rows]], axis=0)
                rhs = jnp.concatenate([bt[u][rows], kq[u][rows]], axis=0)
                halves.append(_dot_nt(lhs, rhs))
            g.append(halves)
        zero_q = jnp.zeros((hb, hb), F32)

        def diag2(gh, r0, c0):
            a, b = gh[0][r0:r0 + hb, c0:c0 + hb], gh[1][r0:r0 + hb, c0:c0 + hb]
            return jnp.concatenate([jnp.concatenate([a, zero_q], axis=1), jnp.concatenate([zero_q, b], axis=1)],
                                   axis=0)

        n1 = [jnp.where(strict, -diag2(gh, 0, 0), 0.0).astype(BF16) for gh in g]
        a_kq = [jnp.where(strict, diag2(gh, 0, hb), 0.0).astype(BF16) for gh in g]
        a_rb = [jnp.where(incl, diag2(gh, hb, 0), 0.0).astype(BF16) for gh in g]
        a_rq = [jnp.where(incl, diag2(gh, hb, hb), 0.0).astype(BF16) for gh in g]
        n2 = [_dot(n, n).astype(BF16) for n in n1]
        akv = [_dot(a, vv[u]) for a, (u, _) in zip(a_kq, heads)]
        n4 = [_dot(n, n).astype(BF16) for n in n2]
        y_in = [_dot(a, vv[u]) for a, (u, _) in zip(a_rq, heads)]
        n8 = [_dot(n, n).astype(BF16) for n in n4]
        fold = [_dot(a, fold0 if hh == 0 else fold1) for a, (_, hh) in zip(a_rb, heads)]
        rhs_t = [jnp.concatenate([kt[u].astype(F32), av], axis=1) for av, (u, _) in zip(akv, heads)]
        for nk in (n8, n4, n2, n1):
            rhs_t = [x + _dot(n, x.astype(BF16)) for n, x in zip(nk, rhs_t)]
        for u, q in enumerate(units):
            h0, h1 = 2 * u, 2 * u + 1
            w_s[q] = jnp.where(lane_lo, rhs_t[h0][:, :LANES], rhs_t[h1][:, :LANES]).astype(BF16)
            u_s[q] = jnp.where(lane_lo, rhs_t[h0][:, LANES:], rhs_t[h1][:, LANES:])
            ya_s[q] = ya[u] + jnp.where(lane_lo, y_in[h0], y_in[h1])
            ab_s[q] = (fold[h0] + fold[h1]).astype(BF16)
        return carry

    lax.fori_loop(0, RW_PAIRS, pair_body, 0)

    blockdiag = (_iota((LANES, LANES), 0) // RW_HEAD) == (_iota((LANES, LANES), 1) // RW_HEAD)
    end_row = 0 if reverse else ch - 1

    def chunk_body(i, carry):
        c = (n_ch - 1 - i) if reverse else i
        rows = pl.ds(pl.multiple_of(c * ch, ch), ch)
        s_old = [st_ref[q] for q in range(n_units)]
        m1 = [_dot_nt(jnp.concatenate([w_s[q, rows, :], rt_s[q, rows, :]], axis=0), s_old[q].astype(BF16))
              for q in range(n_units)]
        zc = [-(m1[q][:ch] + u_s[q, rows, :]) for q in range(n_units)]
        upd = []
        for q in range(n_units):
            zv = jnp.concatenate([zc[q].astype(BF16), v_s[q, rows, :]], axis=0)
            bk = jnp.concatenate([bt_s[q, rows, :], kq_s[q, rows, :]], axis=0)
            upd.append(_dot_tn(zv, bk))
        yc = []
        for q in range(n_units):
            z2 = jnp.concatenate([jnp.where(lane_lo, zc[q], 0.0), jnp.where(lane_lo, 0.0, zc[q])], axis=0)
            yc.append(m1[q][ch:] + _dot(ab_s[q, rows, :][:, :2 * ch], z2.astype(BF16)) + ya_s[q, rows, :])
        for q in range(n_units):
            b, p = divmod(q, RW_PAIRS)
            p_end = pin_s[b, rows, LANES * p:LANES * (p + 1)][end_row:end_row + 1]
            st_ref[q] = (s_old[q] + jnp.where(blockdiag, upd[q], 0.0)) * p_end
            y_ref[b, rows, LANES * p:LANES * (p + 1)] = yc[q]
        return carry

    lax.fori_loop(0, n_ch, chunk_body, 0)


def _rwkv_direction(zall, prm, bsz, nc_blocks, nl_blocks, reverse):
    assert bsz == 2
    mu, w0, wup, a0, aup, k_k, k_a, r_k = prm
    rows = zall.shape[0]
    blk = RW_BLOCK
    steps = nc_blocks + nl_blocks
    n_blocks = rows // blk

    def local(s):
        lat = nc_blocks + ((nl_blocks - 1 - (s - nc_blocks)) if reverse else (s - nc_blocks))
        ctx = (nc_blocks - 1 - s) if reverse else s
        return jnp.where(s < nc_blocks, ctx, lat)

    def block(b, s):
        i = local(s)
        return jnp.where(i < nc_blocks, b * nc_blocks + i, bsz * nc_blocks + b * nl_blocks + i - nc_blocks)

    def halo(b, s):
        i = block(b, s)
        if reverse:
            return jnp.minimum((i + 1) * (blk // 8), n_blocks * (blk // 8) - 1)
        return jnp.maximum(i * (blk // 8) - 1, 0)

    vec = lambda w: pl.BlockSpec((1, w), lambda s: (0, 0))
    mat = lambda: pl.BlockSpec((LANES, RW_WIDTH), lambda s: (0, 0))
    n_units = bsz * RW_PAIRS
    unit_bf = pltpu.VMEM((n_units, blk, LANES), BF16)
    unit_f = pltpu.VMEM((n_units, blk, LANES), F32)
    z_spec = lambda b: pl.BlockSpec((blk, RW_Z), lambda s: (block(b, s), 0))
    halo_spec = lambda b: pl.BlockSpec((8, RW_Z), lambda s: (halo(b, s), 0))
    return pl.pallas_call(
        functools.partial(_rwkv_kernel, reverse=reverse, nc_blocks=nc_blocks),
        out_shape=jax.ShapeDtypeStruct((bsz, steps * blk, RW_WIDTH), F32),
        grid=(steps,),
        in_specs=[
            z_spec(0), z_spec(1), halo_spec(0), halo_spec(1),
            vec(RW_Z), vec(RW_WIDTH), mat(), vec(RW_WIDTH), mat(), vec(RW_WIDTH), vec(RW_WIDTH), vec(RW_WIDTH),
        ],
        out_specs=pl.BlockSpec((bsz, blk, RW_WIDTH), lambda s: (0, local(s), 0)),
        scratch_shapes=[
            pltpu.VMEM((n_units, LANES, LANES), F32),
            unit_bf, unit_bf, unit_bf, unit_bf, unit_bf,
            unit_bf, unit_f, unit_f, unit_bf,
            pltpu.VMEM((bsz, blk, RW_WIDTH), F32),
        ],
        compiler_params=_cp("arbitrary"),
        name="rwkv_bwd" if reverse else "rwkv_fwd",
    )(zall, zall, zall, zall, mu, w0, wup, a0, aup, k_k, k_a, r_k)


def _branches_kernel(saf_ref, sab_ref, su_ref, sd_ref, rf_ref, rb_ref, rg_ref, gup_ref, lw_ref, lb_ref,
                     of_ref, ob_ref, og_ref, ng_ref, ga_ref, gb_ref, gc_ref, wglu_ref, wrw_ref, wgl_ref, m_ref):
    za = _gelu_tanh(saf_ref[...] + sab_ref[...] + sd_ref[...] * su_ref[...].astype(F32)).astype(BF16)
    hid = _dot(za, wglu_ref[...])
    m = _sigmoid(ga_ref[...].astype(F32)) * (hid[:, :D_MODEL] * _sigmoid(hid[:, D_MODEL:]))
    ones64 = _head_ones(RW_HEAD)
    gate = _dot(_sigmoid(rg_ref[...]).astype(BF16), gup_ref[...])
    zb = []
    for p in range(RW_PAIRS):
        ls = slice(LANES * p, LANES * (p + 1))
        y = rf_ref[:, ls] + rb_ref[:, ls]
        mean = _head_sum(y, ones64, True) * (1.0 / RW_HEAD)
        yc = y - mean
        var = _head_sum(yc * yc, ones64, True) * (1.0 / RW_HEAD)
        yn = yc * lax.rsqrt(var + RW_GN_EPS) * lw_ref[:, ls] + lb_ref[:, ls]
        zb.append((yn * gate[:, ls]).astype(BF16))
    m = m + _sigmoid(gb_ref[...].astype(F32)) * _dot(jnp.concatenate(zb, axis=1), wrw_ref[...])
    zc = []
    for h in range(GLA_HEADS):
        vs = slice(GLA_DV * h, GLA_DV * (h + 1))
        o = of_ref[:, vs].astype(F32) + ob_ref[:, vs].astype(F32)
        on = o * lax.rsqrt(jnp.mean(o * o, axis=-1, keepdims=True) + NORM_EPS) * ng_ref[:, vs]
        zc.append((on * _silu(og_ref[:, vs].astype(F32))).astype(BF16))
    m = m + _sigmoid(gc_ref[...].astype(F32)) * _dot(jnp.concatenate(zc, axis=1), wgl_ref[...])
    m_ref[...] = m.astype(BF16)


def _branches(ya, yb, o_dir, u_s5, og, gates, zall, prm, row0, nc_blocks, nl_blocks):
    s5_d, g_up, ln_w, ln_b, norm_g, w_glu, w_rw, w_gl = prm
    bm = RW_BLOCK
    bsz = yb[0].shape[0]
    n = zall.shape[0] - row0 * bm
    rows = lambda w, blk=0: pl.BlockSpec((bm, w), lambda i: (i + row0, blk))

    def y_index(i):
        i = i + row0
        t = i - bsz * nc_blocks
        return (jnp.where(t < 0, i // nc_blocks, t // nl_blocks),
                jnp.where(t < 0, i % nc_blocks, nc_blocks + t % nl_blocks), 0)

    y_spec = pl.BlockSpec((None, bm, RW_WIDTH), y_index)
    vec = pl.BlockSpec((1, 1024), lambda i: (0, 0))
    const = lambda a: pl.BlockSpec(a.shape, lambda i: (0,) * a.ndim, pipeline_mode=pl.Buffered(1))
    return pl.pallas_call(
        _branches_kernel,
        out_shape=jax.ShapeDtypeStruct((n, D_MODEL), BF16),
        grid=(n // bm,),
        in_specs=[rows(1024), rows(1024), rows(1024, 0), vec,
                  y_spec, y_spec, rows(LANES, RW_Z // LANES), const(g_up), vec, vec,
                  rows(1024), rows(1024), rows(1024), vec,
                  rows(D_MODEL, 0), rows(D_MODEL, 1), rows(D_MODEL, 2),
                  const(w_glu), const(w_rw), const(w_gl)],
        out_specs=pl.BlockSpec((bm, D_MODEL), lambda i: (i, 0)),
        compiler_params=_cp("parallel"),
        name="branches",
    )(ya[0], ya[1], u_s5, s5_d, yb[0], yb[1], zall, g_up, ln_w, ln_b,
      o_dir[0], o_dir[1], og, norm_g, gates, gates, gates, w_glu, w_rw, w_gl)


def _out_proj_kernel(m_ref, w_ref, x_ref, gate_ref, o_ref):
    o_ref[...] = x_ref[...] + gate_ref[...] * _dot(m_ref[...], w_ref[...])


def _out_proj(m, w_out, x, mod, layer, bm, row0, nc, seq):
    n, d = m.shape
    bn = d
    mod_index = _mod_spec(layer, 2, bm, nc, seq).index_map
    return pl.pallas_call(
        _out_proj_kernel,
        out_shape=jax.ShapeDtypeStruct((n, d), F32),
        grid=(n // bm, d // bn),
        in_specs=[pl.BlockSpec((bm, d), lambda i, j: (i, 0)),
                  pl.BlockSpec((d, bn), lambda i, j: (0, j)),
                  pl.BlockSpec((bm, bn), lambda i, j: (i + row0, j)),
                  pl.BlockSpec((None, None, None, 1, bn), lambda i, j: mod_index(i + row0)[:4] + (j,))],
        out_specs=pl.BlockSpec((bm, bn), lambda i, j: (i, j)),
        compiler_params=_cp("parallel", "parallel"),
        name="out_proj",
    )(m, w_out, x, mod)


MOE_TM = 256
MOE_ROW_TILES = D_MODEL // (2 * LANES)
META_E1, META_E2, META_R1, META_R2, META_W1, META_W2 = range(6)


def _router_kernel(x_ref, g_ref, sh_ref, sc_ref, wr_hi_ref, wr_lo_ref, br_ref, v_ref, meta_ref, meta_t_ref, cnt_ref,
                   base_s):
    @pl.when(pl.program_id(0) == 0)
    def _():
        base_s[...] = jnp.zeros_like(base_s)

    x = x_ref[...]
    y = x * lax.rsqrt(jnp.mean(x * x, axis=-1, keepdims=True) + NORM_EPS) * g_ref[...]
    t = y * (1.0 + sc_ref[...]) + sh_ref[...]
    word = _pack_halves(t)
    for s in range(MOE_ROW_TILES):
        v_ref[pl.ds(s, x.shape[0], stride=MOE_ROW_TILES), :] = word[:, LANES * s:LANES * (s + 1)]
    t_hi, t_lo = _split_hi_lo(t)
    logits = (_dot(t_hi, wr_hi_ref[...]) + _dot(t_lo, wr_hi_ref[...]) + _dot(t_hi, wr_lo_ref[...])) + br_ref[...]
    lane = _iota(logits.shape, 1).astype(F32)
    neg = jnp.float32(-jnp.inf)
    big = jnp.float32(LANES)
    l1 = jnp.where(lane < MOE_GROUPS, logits, neg)
    m1 = jnp.max(l1, axis=-1, keepdims=True)
    p_top = 1.0 / jnp.sum(jnp.exp(l1 - m1), axis=-1, keepdims=True)
    grp = jnp.min(jnp.where(l1 == m1, lane, big), axis=-1, keepdims=True)
    lo = MOE_LANE0 + MOE_PER_GROUP * grp
    in_grp = (lane >= lo) & (lane < lo + MOE_PER_GROUP)
    l2 = jnp.where(in_grp, logits, neg)
    v1 = jnp.max(l2, axis=-1, keepdims=True)
    i1 = jnp.min(jnp.where(l2 == v1, lane, big), axis=-1, keepdims=True)
    l3 = jnp.where(lane == i1, neg, l2)
    v2 = jnp.max(l3, axis=-1, keepdims=True)
    i2 = jnp.min(jnp.where(l3 == v2, lane, big), axis=-1, keepdims=True)
    e2 = jnp.exp(v2 - v1)
    w1 = p_top / (1.0 + e2)
    w2 = p_top * e2 / (1.0 + e2)
    pick1 = lane == i1
    pick2 = lane == i2
    chosen = jnp.where(pick1 | pick2, 1.0, 0.0)
    bm = x.shape[0]
    earlier = (_iota((bm, bm), 1) < _iota((bm, bm), 0)).astype(BF16)
    before = _dot(earlier, chosen.astype(BF16)) + base_s[...]
    r1 = jnp.sum(jnp.where(pick1, before, 0.0), axis=-1, keepdims=True)
    r2 = jnp.sum(jnp.where(pick2, before, 0.0), axis=-1, keepdims=True)
    base_s[...] += jnp.sum(chosen, axis=0, keepdims=True)
    cnt_ref[...] = base_s[...]
    meta = jnp.zeros_like(logits)
    for slot, val in ((META_E1, i1 - MOE_LANE0), (META_E2, i2 - MOE_LANE0), (META_R1, r1), (META_R2, r2),
                      (META_W1, w1), (META_W2, w2)):
        meta = jnp.where(lane == slot, val, meta)
    meta_ref[...] = meta
    meta_t_ref[...] = meta.T[:8]


def _router(x, g, mod, layer, wr_hi, wr_lo, br, bm, nc, seq):
    n, d = x.shape
    return pl.pallas_call(
        _router_kernel,
        out_shape=(jax.ShapeDtypeStruct((n * MOE_ROW_TILES, LANES), jnp.uint32), jax.ShapeDtypeStruct((n, LANES), F32),
                   jax.ShapeDtypeStruct((8, n), F32), jax.ShapeDtypeStruct((1, LANES), F32)),
        grid=(n // bm,),
        in_specs=[
            pl.BlockSpec((bm, d), lambda i: (i, 0)),
            pl.BlockSpec((None, 1, d), lambda i: (layer, 0, 0)),
            _mod_spec(layer, 3, bm, nc, seq), _mod_spec(layer, 4, bm, nc, seq),
            pl.BlockSpec((d, LANES), lambda i: (0, 0)), pl.BlockSpec((d, LANES), lambda i: (0, 0)),
            pl.BlockSpec((1, LANES), lambda i: (0, 0)),
        ],
        out_specs=(pl.BlockSpec((bm * MOE_ROW_TILES, LANES), lambda i: (i, 0)),
                   pl.BlockSpec((bm, LANES), lambda i: (i, 0)),
                   pl.BlockSpec((8, bm), lambda i: (0, i)), pl.BlockSpec((1, LANES), lambda i: (0, 0))),
        scratch_shapes=[pltpu.VMEM((1, LANES), F32)],
        compiler_params=_cp("arbitrary"),
        name="moe_router",
    )(x, g, mod, mod, wr_hi, wr_lo, br)


def _moe_plan(meta_t, cnt):
    tm = MOE_TM
    n_tok = meta_t.shape[1]
    counts = cnt[0, MOE_LANE0:MOE_LANE0 + MOE_EXPERTS].astype(jnp.int32)
    seg = ((counts + tm - 1) // tm) * tm
    ends = jnp.cumsum(seg)
    off = ends - seg
    rec = meta_t[:4].astype(jnp.int32)
    first_row = jnp.sum(jnp.where(rec[:2, None, :] == jnp.arange(MOE_EXPERTS)[None, :, None], off[None, :, None], 0),
                        axis=1)
    dest = first_row + rec[2:4]
    n_rows = 2 * n_tok + MOE_EXPERTS * tm
    n_tiles = n_rows // tm
    tile_e = jnp.sum(((jnp.arange(n_tiles) * tm)[:, None] >= ends[None, :]).astype(jnp.int32), axis=1)
    tile_e = jnp.minimum(tile_e, MOE_EXPERTS - 1)
    tok = jnp.broadcast_to(jnp.arange(n_tok, dtype=jnp.int32), (2, n_tok))
    src = jnp.zeros((n_rows,), jnp.int32).at[dest.reshape(-1)].set(tok.reshape(-1))
    return dest[0], dest[1], src, tile_e, (ends[-1] // tm).reshape(1)


def _row_copy(src_hbm, row, dst, slot, r, sem):
    return pltpu.make_async_copy(src_hbm.at[pl.ds(row, 1)], dst.at[slot, pl.ds(r, 1)], sem)


def _experts_kernel(src_ref, te_ref, nu_ref, v_hbm, wg_ref, wu_ref, wd_ref, y_ref, xbuf, sem):
    del te_ref
    i = pl.program_id(0)
    n_used = nu_ref[0]
    slot = i % 2

    nt = MOE_ROW_TILES

    def token_copy(tok, into, r):
        return pltpu.make_async_copy(v_hbm.at[pl.ds(tok * nt, nt)], xbuf.at[into, pl.ds(r * nt, nt)], sem.at[into])

    def gather(tile, into):
        for r in range(MOE_TM):
            token_copy(src_ref[tile * MOE_TM + r], into, r).start(priority=r % 2)

    def expert():
        for r in range(MOE_TM):
            token_copy(0, slot, r).wait()
        word = jnp.concatenate([xbuf[slot, pl.ds(s, MOE_TM, stride=nt), :] for s in range(nt)], axis=1)
        t = jnp.concatenate(_unpack_halves(word), axis=1).astype(BF16)
        hid = _silu(_dot(t, wg_ref[...].astype(BF16))) * _dot(t, wu_ref[...].astype(BF16))
        y_ref[...] = _pack_halves(_dot(hid.astype(BF16), wd_ref[...].astype(BF16)))

    @pl.when((i == 0) & (n_used > 0))
    def _():
        gather(0, 0)

    @pl.when(i + 1 < n_used)
    def _():
        gather(i + 1, 1 - slot)
        expert()

    @pl.when(i + 1 == n_used)
    def _():
        expert()

    @pl.when(i >= n_used)
    def _():
        y_ref[...] = jnp.zeros_like(y_ref)


def _experts(v, src, tile_e, n_used, layer, w_gate, w_up, w_down):
    d = w_gate.shape[2]
    hdim = w_gate.shape[3]
    n_rows = src.shape[0]
    tm = MOE_TM
    by_expert = lambda i, src_r, te_r, nu_r: (layer, te_r[i], 0, 0)
    return pl.pallas_call(
        _experts_kernel,
        out_shape=jax.ShapeDtypeStruct((n_rows, d // 2), jnp.uint32),
        grid_spec=pltpu.PrefetchScalarGridSpec(
            num_scalar_prefetch=3,
            grid=(n_rows // tm,),
            in_specs=[
                pl.BlockSpec(memory_space=pl.ANY),
                pl.BlockSpec((None, None, d, hdim), by_expert),
                pl.BlockSpec((None, None, d, hdim), by_expert),
                pl.BlockSpec((None, None, hdim, d), by_expert),
            ],
            out_specs=pl.BlockSpec((tm, d // 2), lambda i, *_: (i, 0)),
            scratch_shapes=[pltpu.VMEM((2, tm * MOE_ROW_TILES, LANES), jnp.uint32), pltpu.SemaphoreType.DMA((2,))],
        ),
        compiler_params=_cp("arbitrary"),
        name="moe_experts",
    )(src, tile_e, n_used, v, w_gate, w_up, w_down)


def _combine_kernel(d1_ref, d2_ref, y_hbm, meta_ref, x_ref, gate_ref, gfin_ref, o_ref, buf1, buf2, sem, *, final):
    i = pl.program_id(0)
    bm = x_ref.shape[0]
    slot = i % 2

    def gather(tile, into):
        for r in range(bm):
            t = tile * bm + r
            _row_copy(y_hbm, d1_ref[t], buf1, into, r, sem.at[0, into]).start(priority=0)
            _row_copy(y_hbm, d2_ref[t], buf2, into, r, sem.at[1, into]).start(priority=1)

    @pl.when(i == 0)
    def _():
        gather(0, 0)

    @pl.when(i + 1 < pl.num_programs(0))
    def _():
        gather(i + 1, 1 - slot)

    for r in range(bm):
        _row_copy(y_hbm, 0, buf1, slot, r, sem.at[0, slot]).wait()
        _row_copy(y_hbm, 0, buf2, slot, r, sem.at[1, slot]).wait()
    meta = meta_ref[...]
    lane = _iota(meta.shape, 1)
    w1 = jnp.sum(jnp.where(lane == META_W1, meta, 0.0), axis=-1, keepdims=True)
    w2 = jnp.sum(jnp.where(lane == META_W2, meta, 0.0), axis=-1, keepdims=True)
    lo1, hi1 = _unpack_halves(buf1[slot])
    lo2, hi2 = _unpack_halves(buf2[slot])
    moe = jnp.concatenate([w1 * lo1 + w2 * lo2, w1 * hi1 + w2 * hi2], axis=1)
    out = x_ref[...] + gate_ref[...] * moe
    if final:
        out = out * lax.rsqrt(jnp.mean(out * out, axis=-1, keepdims=True) + NORM_EPS) * gfin_ref[...]
    o_ref[...] = out


def _combine(y, d1, d2, meta, x, mod, layer, bm, nc, seq, g_final, final):
    n, d = x.shape
    at_tile = lambda i, *_: (i, 0)
    mod_index = _mod_spec(layer, 5, bm, nc, seq).index_map
    return pl.pallas_call(
        functools.partial(_combine_kernel, final=final),
        out_shape=jax.ShapeDtypeStruct((n, d), F32),
        grid_spec=pltpu.PrefetchScalarGridSpec(
            num_scalar_prefetch=2,
            grid=(n // bm,),
            in_specs=[
                pl.BlockSpec(memory_space=pl.ANY),
                pl.BlockSpec((bm, LANES), at_tile),
                pl.BlockSpec((bm, d), at_tile),
                pl.BlockSpec((None, None, None, 1, d), lambda i, *_: mod_index(i)),
                pl.BlockSpec((1, d), lambda i, *_: (0, 0)),
            ],
            out_specs=pl.BlockSpec((bm, d), at_tile),
            scratch_shapes=[pltpu.VMEM((2, bm, d // 2), jnp.uint32), pltpu.VMEM((2, bm, d // 2), jnp.uint32),
                            pltpu.SemaphoreType.DMA((2, 2))],
        ),
        compiler_params=_cp("arbitrary"),
        name="moe_combine",
    )(d1, d2, y, meta, x, mod, g_final)


_COL = dict(s5=0, rw=S5_WIDTH, rg=S5_WIDTH + 3 * RW_WIDTH + 128, q=4352, k=4864, v=5376, gk=6400, og=6416, gates=7440)


def _pad_rows(w, rows):
    return jnp.pad(w, ((0, rows - w.shape[0]), (0, 0)))


def _layer(i, last, xa, mod, bsz, seq, ctx_len, p):
    nc = bsz * ctx_len
    rows = xa.shape[0]
    bm = 512
    msel = dict(nc=nc, seq=seq)
    u = _normmod(xa, p['g_norm1'].reshape(-1, 1, D_MODEL), mod, i, (0, 1), bm, **msel)

    w_in = p['w_in'][i]
    col = lambda a, w: w_in[:, a:a + w].astype(BF16)
    bm_in = 1088 if rows % 1088 == 0 else bm
    w_z = col(_COL['rw'], 3 * RW_WIDTH + 128 + 128)
    w_qk = col(_COL['q'], 2 * GLA_QK)
    w_v = col(_COL['v'], GLA_V)
    w_gk = jnp.pad(col(_COL['gk'], 16), ((0, 0), (0, LANES - 16)))
    u_s5 = _mm(u, col(0, S5_WIDTH), bm_in, 1024, BF16)
    og = _mm(u, col(_COL['og'], GLA_V), bm_in, 1024, BF16)
    gates = _mm(u, col(_COL['gates'], 3 * D_MODEL), bm_in, 2048, BF16)
    zall = _mm(u, w_z, bm_in, 1664)
    qk = _mm(u, w_qk, bm_in, 1024, BF16)
    vv = _mm(u, w_v, bm_in, 1024, BF16)
    gk = _mm(u, w_gk, bm_in, LANES, BF16)

    ya = []
    for d in range(2):
        tables = _s5_tables(*(p[k][i, d] for k in ('s5_a_re', 's5_a_im', 's5_log_dt', 's5_b_re', 's5_b_im',
                                                   's5_c_re', 's5_c_im')))
        ya.append(_s5_scan(u_s5, tables, bsz, ctx_len // S5_BLOCK, seq // S5_BLOCK, bool(d)))

    yb = []
    for d in range(2):
        prm = (
            jnp.pad(p['rw_mu'][i, d], (0, RW_Z - p['rw_mu'].shape[-1])).reshape(1, RW_Z),
            p['rw_w0'][i, d].reshape(1, -1),
            _pad_rows(p['rw_w_up'][i, d], LANES).astype(BF16),
            p['rw_a0'][i, d].reshape(1, -1),
            jnp.pad(p['rw_a_up'][i, d], ((RW_HEAD, 0), (0, 0))).astype(BF16),
            p['rw_k_k'][i].reshape(1, -1), p['rw_k_a'][i].reshape(1, -1), p['rw_r_k'][i].reshape(1, -1),
        )
        yb.append(_rwkv_direction(zall, prm, bsz, ctx_len // RW_BLOCK, seq // RW_BLOCK, bool(d)))

    gk_up = jnp.pad(p['gl_gk_up'][i], ((0, 0), (0, LANES - 16), (0, 0))).astype(BF16)
    gk_b = p['gl_gk_b'][i].reshape(2, 1, GLA_QK)
    s0 = jnp.zeros((2, bsz, GLA_HEADS, GLA_DV, GLA_DK), F32)
    oc, s_ctx = _gla(_to_chunk_major(qk[:nc], bsz), _to_chunk_major(vv[:nc], bsz), _to_chunk_major(gk[:nc], bsz),
                     gk_up, gk_b, s0)
    rows_l = seq // GRID_W
    lat = lambda t: t[nc:].reshape(bsz, rows_l, GRID_W * t.shape[1])
    ol, _ = _gla(lat(qk), lat(vv), lat(gk), gk_up, gk_b, s_ctx)
    o_dir = [jnp.concatenate([_from_chunk_major(oc[d], GLA_V).reshape(nc, GLA_V),
                              ol[d].reshape(bsz * seq, GLA_V)], axis=0) for d in range(2)]

    row0 = (nc // bm) if last else 0
    branch_prm = (p['s5_d'][i].reshape(1, -1), p['rw_g_up'][i].astype(BF16), p['rw_ln_w'][i].reshape(1, -1),
                  p['rw_ln_b'][i].reshape(1, -1), p['gl_norm_g'][i].reshape(1, -1), p['s5_w_glu'][i].astype(BF16),
                  p['rw_w_proj'][i].astype(BF16), p['gl_w_proj'][i].astype(BF16))
    merged = _branches(ya, yb, o_dir, u_s5, og, gates, zall, branch_prm, row0 * (bm // RW_BLOCK), ctx_len // RW_BLOCK,
                       seq // RW_BLOCK)
    xm = _out_proj(merged, p['w_out'][i].astype(BF16), xa, mod, i, bm, row0, nc, seq)
    if last:
        msel = dict(nc=0, seq=seq)

    wr = jnp.pad(jnp.concatenate([p['moe_wg1'][i], p['moe_wg2'][i]], axis=1), ((0, 0), (0, LANES - 36)))
    wr_hi = wr.astype(BF16)
    wr_lo = (wr - wr_hi.astype(F32)).astype(BF16)
    br = jnp.pad(jnp.concatenate([p['moe_bg1'][i], p['moe_bg2'][i]]), (0, LANES - 36)).reshape(1, LANES)
    vmoe, meta, meta_t, cnt = _router(xm, p['g_norm2'].reshape(-1, 1, D_MODEL), mod, i, wr_hi, wr_lo, br, bm, **msel)
    d1, d2, src, tile_e, n_used = _moe_plan(meta_t, cnt)
    y_sorted = _experts(vmoe, src, tile_e, n_used, i, p['moe_w_gate'], p['moe_w_up'], p['moe_w_down'])
    return _combine(y_sorted, d1, d2, meta, xm, mod, i, bm // 2, g_final=p['g_final'], final=last, **msel)


def kernel(x, c, ctx, c_ctx, w_mod, b_mod, g_norm1, g_norm2, w_in, s5_a_re, s5_a_im, s5_log_dt, s5_b_re, s5_b_im,
           s5_c_re, s5_c_im, s5_d, s5_w_glu, rw_mu, rw_w0, rw_w_up, rw_a0, rw_a_up, rw_k_k, rw_k_a, rw_r_k, rw_g_up,
           rw_ln_w, rw_ln_b, rw_w_proj, gl_gk_up, gl_gk_b, gl_norm_g, gl_w_proj, w_out, moe_wg1, moe_bg1, moe_wg2,
           moe_bg2, moe_w_gate, moe_w_up, moe_w_down, g_final):
    p = dict(g_norm1=g_norm1, g_norm2=g_norm2, w_in=w_in, s5_a_re=s5_a_re, s5_a_im=s5_a_im, s5_log_dt=s5_log_dt,
             s5_b_re=s5_b_re, s5_b_im=s5_b_im, s5_c_re=s5_c_re, s5_c_im=s5_c_im, s5_d=s5_d, s5_w_glu=s5_w_glu,
             rw_mu=rw_mu, rw_w0=rw_w0, rw_w_up=rw_w_up, rw_a0=rw_a0, rw_a_up=rw_a_up, rw_k_k=rw_k_k, rw_k_a=rw_k_a,
             rw_r_k=rw_r_k.reshape(rw_r_k.shape[0], -1), rw_g_up=rw_g_up, rw_ln_w=rw_ln_w, rw_ln_b=rw_ln_b,
             rw_w_proj=rw_w_proj, gl_gk_up=gl_gk_up, gl_gk_b=gl_gk_b, gl_norm_g=gl_norm_g, gl_w_proj=gl_w_proj,
             w_out=w_out, moe_wg1=moe_wg1, moe_bg1=moe_bg1, moe_wg2=moe_wg2, moe_bg2=moe_bg2, moe_w_gate=moe_w_gate,
             moe_w_up=moe_w_up, moe_w_down=moe_w_down, g_final=g_final.reshape(1, -1))
    bsz, seq, d = x.shape
    ctx_len = ctx.shape[1]
    depth = w_mod.shape[0]
    cc = jnp.concatenate([c, c_ctx[None], jnp.zeros((8 - bsz - 1, d), F32)], axis=0)
    mod = _adaln(cc, w_mod, b_mod).reshape(depth, 8, 6, 1, d)
    xa = jnp.concatenate([ctx.reshape(bsz * ctx_len, d), x.reshape(bsz * seq, d)], axis=0)
    for i in range(depth):
        xa = _layer(i, i == depth - 1, xa, mod, bsz, seq, ctx_len, p)
    return xa.reshape(bsz, seq, d)
```

```python
import functools
import math

import jax
import jax.numpy as jnp
from jax import lax
from jax.experimental import pallas as pl
from jax.experimental.pallas import tpu as pltpu

F32 = jnp.float32
BF16 = jnp.bfloat16

D_MODEL = 2048
GRID_W = 64
NORM_EPS = 1e-6

S5_WIDTH = 1024
S5_GROUP = 16
S5_GROUPS = 64
S5_STATE = 64
S5_MAX_RE = -1e-4
S5_TILE = 16
S5_PAIRS = S5_GROUPS // 2

RW_WIDTH = 1024
RW_HEAD = 64
RW_DECAY_SCALE = 0.606531
RW_GN_EPS = 64e-5
RW_BLOCK = 256
RW_CHUNK = 16
RW_PAIRS = RW_WIDTH // 128
RW_Z = 3 * RW_WIDTH + 128

GLA_HEADS = 4
GLA_DK = 128
GLA_DV = 256
GLA_QK = 512
GLA_V = 1024
GLA_TAU = 16.0
GLA_CHUNK = 64

MOE_GROUPS = 4
MOE_PER_GROUP = 8
MOE_EXPERTS = 32
MOE_HIDDEN = 256
MOE_LANE0 = MOE_GROUPS

LANES = 128
VMEM_LIMIT = 56 * 1024 * 1024


def _cp(*sem):
    return pltpu.CompilerParams(dimension_semantics=sem, vmem_limit_bytes=VMEM_LIMIT)


def _dot(a, b):
    return jnp.dot(a, b, preferred_element_type=F32)


def _dot_nt(a, b):
    return lax.dot_general(a, b, (((1,), (1,)), ((), ())), preferred_element_type=F32)


def _dot_tn(a, b):
    return lax.dot_general(a, b, (((0,), (0,)), ((), ())), preferred_element_type=F32)


def _sigmoid(x):
    return 0.5 * jnp.tanh(0.5 * x) + 0.5


def _silu(x):
    return x * _sigmoid(x)


def _gelu_tanh(x):
    return 0.5 * x * (1.0 + jnp.tanh(math.sqrt(2.0 / math.pi) * (x + 0.044715 * (x * x * x))))


def _split_hi_lo(x):
    hi = x.astype(BF16)
    lo = (x - hi.astype(F32)).astype(BF16)
    return hi, lo


def _pack_halves(x):
    bits = lax.bitcast_convert_type(x.astype(BF16).astype(F32), jnp.uint32)
    n = x.shape[1] // 2
    return (bits[:, n:] & jnp.uint32(0xFFFF0000)) | (bits[:, :n] >> 16)


def _unpack_halves(word):
    return (lax.bitcast_convert_type(word << 16, F32),
            lax.bitcast_convert_type(word & jnp.uint32(0xFFFF0000), F32))


def _iota(shape, dim):
    return lax.broadcasted_iota(jnp.int32, shape, dim)


def _head_ones(width):
    return (_iota((LANES, LANES), 0) // width == _iota((LANES, LANES), 1) // width).astype(BF16)


def _head_sum(x, ones, exact):
    if exact:
        hi, lo = _split_hi_lo(x)
        return _dot(hi, ones) + _dot(lo, ones)
    return _dot(x.astype(BF16), ones)


def _adaln_kernel(c_ref, w_ref, b_ref, o_ref):
    c = c_ref[...]
    o_ref[...] = _dot(_silu(c).astype(BF16), w_ref[...].astype(BF16)) + b_ref[...]


def _adaln(cc, w_mod, b_mod):
    depth, d, n = w_mod.shape
    bn = 1536
    return pl.pallas_call(
        _adaln_kernel,
        out_shape=jax.ShapeDtypeStruct((depth, 8, n), F32),
        grid=(depth, n // bn),
        in_specs=[
            pl.BlockSpec((8, d), lambda l, j: (0, 0)),
            pl.BlockSpec((None, d, bn), lambda l, j: (l, 0, j)),
            pl.BlockSpec((None, 1, bn), lambda l, j: (l, 0, j)),
        ],
        out_specs=pl.BlockSpec((None, 8, bn), lambda l, j: (l, 0, j)),
        compiler_params=_cp("parallel", "parallel"),
        name="adaln",
    )(cc, w_mod, b_mod.reshape(depth, 1, n))


def _mod_spec(layer, part, bm, nc, seq):
    def index(i, *_):
        r0 = i * bm
        return (layer, jnp.where(r0 < nc, 2, (r0 - nc) // seq), part, 0, 0)
    return pl.BlockSpec((None, None, None, 1, D_MODEL), index)


def _normmod_kernel(x_ref, g_ref, sh_ref, sc_ref, o_ref):
    x = x_ref[...]
    y = x * lax.rsqrt(jnp.mean(x * x, axis=-1, keepdims=True) + NORM_EPS) * g_ref[...]
    o_ref[...] = (y * (1.0 + sc_ref[...]) + sh_ref[...]).astype(o_ref.dtype)


def _normmod(x, g, mod, layer, parts, bm, nc, seq):
    n, d = x.shape
    return pl.pallas_call(
        _normmod_kernel,
        out_shape=jax.ShapeDtypeStruct((n, d), BF16),
        grid=(n // bm,),
        in_specs=[
            pl.BlockSpec((bm, d), lambda i: (i, 0)),
            pl.BlockSpec((None, 1, d), lambda i: (layer, 0, 0)),
            _mod_spec(layer, parts[0], bm, nc, seq),
            _mod_spec(layer, parts[1], bm, nc, seq),
        ],
        out_specs=pl.BlockSpec((bm, d), lambda i: (i, 0)),
        compiler_params=_cp("parallel"),
        name="normmod",
    )(x, g, mod, mod)


def _mm_kernel(x_ref, w_ref, o_ref):
    o_ref[...] = _dot(x_ref[...], w_ref[...]).astype(o_ref.dtype)


def _mm(x, w, bm, bn, out_dtype=F32):
    m, k = x.shape
    n = w.shape[1]
    return pl.pallas_call(
        _mm_kernel,
        out_shape=jax.ShapeDtypeStruct((m, n), out_dtype),
        grid=(m // bm, n // bn),
        in_specs=[pl.BlockSpec((bm, k), lambda i, j: (i, 0)), pl.BlockSpec((k, bn), lambda i, j: (0, j))],
        out_specs=pl.BlockSpec((bm, bn), lambda i, j: (i, j)),
        compiler_params=_cp("parallel", "parallel"),
        name="mm",
    )(x, w)


def _seq_blocks(bsz, nc_blocks, nl_blocks, reverse):
    def local(s):
        lat = nc_blocks + ((nl_blocks - 1 - (s - nc_blocks)) if reverse else (s - nc_blocks))
        ctx = (nc_blocks - 1 - s) if reverse else s
        return jnp.where(s < nc_blocks, ctx, lat)

    def block(b, s):
        i = local(s)
        return jnp.where(i < nc_blocks, b * nc_blocks + i, bsz * nc_blocks + b * nl_blocks + i - nc_blocks)

    return local, block


S5_PACK = 8
S5_PACKS = S5_GROUPS // S5_PACK
S5_PLANE = S5_PACK * S5_STATE
S5_BLOCK = 256
S5_ROWS = 24


def _s5_tables(a_re, a_im, log_dt, b_re, b_im, c_re, c_im):
    lam = lax.complex(jnp.minimum(a_re, S5_MAX_RE), a_im)
    ldt = lam * jnp.exp(log_dt)[:, None]
    lam_bar = jnp.exp(ldt)
    b_bar = ((lam_bar - 1.0) / lam)[..., None] * lax.complex(b_re, b_im)
    c_mat = lax.complex(c_re, c_im)
    eye = jnp.eye(S5_PACK, dtype=F32)

    def block_diag(t):
        k, g, a, b = t.shape
        return (t[:, :, :, None, :] * eye[None, :, None, :, None]).reshape(k, g * a, g * b)

    b_t = jnp.transpose(b_bar, (0, 2, 1)).reshape(S5_PACKS, S5_PACK, S5_GROUP, S5_STATE)
    bblk = jnp.concatenate([block_diag(b_t.real), block_diag(b_t.imag)], axis=2)
    c_t = jnp.transpose(c_mat, (0, 2, 1)).reshape(S5_PACKS, S5_PACK, S5_STATE, S5_GROUP)
    cblk = jnp.concatenate([block_diag(c_t.real), block_diag(-c_t.imag)], axis=1)
    expo = jnp.concatenate([jnp.arange(1, S5_TILE + 1, dtype=F32), jnp.asarray([32.0, 64.0, 128.0], F32),
                            jnp.zeros((S5_ROWS - S5_TILE - 3,), F32)])
    pw = jnp.exp(ldt[None] * expo[:, None, None]).reshape(S5_ROWS, S5_PACKS, S5_PLANE)
    pw = jnp.transpose(pw, (1, 0, 2))
    return bblk.astype(BF16), cblk.astype(BF16), pw.real, pw.imag


def _s5_scan_kernel(u_ref, bblk_ref, cblk_ref, pre_ref, pim_ref, y_ref, car_ref, up_s, h2_s, hb2_s, yp_s, *, reverse):
    t = S5_TILE
    n_t = S5_BLOCK // t
    pn = S5_PLANE

    @pl.when(pl.program_id(1) == 0)
    def _():
        car_ref[...] = jnp.zeros_like(car_ref)

    ra = _iota((S5_BLOCK, S5_BLOCK), 0)
    cb = _iota((S5_BLOCK, S5_BLOCK), 1)
    perm = ((ra // t == cb % t) & (ra % t == cb // t)).astype(BF16)
    up = _dot(perm, u_ref[...].astype(BF16)).astype(BF16)
    for pk in range(S5_PACKS):
        up_s[pk] = up[:, LANES * pk:LANES * (pk + 1)]
    order = list(range(t - 1, -1, -1)) if reverse else list(range(t))
    rowj = _iota((n_t, 1), 0)

    def states(pk, bu, h_s, hb_s):
        pre = pre_ref[pk]
        pim = pim_ref[pk]
        l_re, l_im = pre[0:1], pim[0:1]
        h_re = h_im = None
        for n, s in enumerate(order):
            rows = slice(t * s, t * (s + 1))
            b_re, b_im = bu[rows, :pn], bu[rows, pn:]
            if n == 0:
                h_re, h_im = b_re, b_im
            else:
                h_re, h_im = l_re * h_re - l_im * h_im + b_re, l_re * h_im + l_im * h_re + b_im
            h_s[rows, :pn] = h_re
            h_s[rows, pn:] = h_im
        c_re, c_im = car_ref[pk, 0:1, :pn], car_ref[pk, 0:1, pn:]
        first = rowj == (n_t - 1 if reverse else 0)
        g_re, g_im = pre[t - 1:t], pim[t - 1:t]
        e_re = h_re + jnp.where(first, g_re * c_re - g_im * c_im, 0.0)
        e_im = h_im + jnp.where(first, g_re * c_im + g_im * c_re, 0.0)
        step = 1
        for row in (t - 1, t, t + 1, t + 2):
            if reverse:
                s_re, s_im, ok = pltpu.roll(e_re, n_t - step, 0), pltpu.roll(e_im, n_t - step, 0), rowj < n_t - step
            else:
                s_re, s_im, ok = pltpu.roll(e_re, step, 0), pltpu.roll(e_im, step, 0), rowj >= step
            a_re, a_im = pre[row:row + 1], pim[row:row + 1]
            e_re = e_re + jnp.where(ok, a_re * s_re - a_im * s_im, 0.0)
            e_im = e_im + jnp.where(ok, a_re * s_im + a_im * s_re, 0.0)
            step *= 2
        last = 0 if reverse else n_t - 1
        car_ref[pk, 0:1, :pn] = e_re[last:last + 1]
        car_ref[pk, 0:1, pn:] = e_im[last:last + 1]
        if reverse:
            in_re = jnp.where(first, c_re, pltpu.roll(e_re, n_t - 1, 0))
            in_im = jnp.where(first, c_im, pltpu.roll(e_im, n_t - 1, 0))
        else:
            in_re = jnp.where(first, c_re, pltpu.roll(e_re, 1, 0))
            in_im = jnp.where(first, c_im, pltpu.roll(e_im, 1, 0))
        for n, s in enumerate(order):
            rows = slice(t * s, t * (s + 1))
            a_re, a_im = pre[n:n + 1], pim[n:n + 1]
            hb_s[rows, :pn] = (h_s[rows, :pn] + a_re * in_re - a_im * in_im).astype(BF16)
            hb_s[rows, pn:] = (h_s[rows, pn:] + a_re * in_im + a_im * in_re).astype(BF16)

    def pack_pair(j, carry):
        pks = (2 * j, 2 * j + 1)
        bu = [_dot(up_s[pk], bblk_ref[pk]) for pk in pks]
        for u, pk in enumerate(pks):
            states(pk, bu[u], h2_s.at[u], hb2_s.at[u])
            yp_s[pk] = _dot(hb2_s[u], cblk_ref[pk]).astype(BF16)
        return carry

    lax.fori_loop(0, S5_PACKS // 2, pack_pair, 0)
    for pk in range(S5_PACKS):
        y_ref[:, LANES * pk:LANES * (pk + 1)] = _dot(perm, yp_s[pk])


def _s5_scan(u_src, tables, bsz, nc_blocks, nl_blocks, reverse):
    bblk, cblk, pre, pim = tables
    rows = u_src.shape[0]
    blk = S5_BLOCK
    _, block = _seq_blocks(bsz, nc_blocks, nl_blocks, reverse)
    whole = lambda a: pl.BlockSpec(a.shape, lambda b, s: (0,) * a.ndim)
    return pl.pallas_call(
        functools.partial(_s5_scan_kernel, reverse=reverse),
        out_shape=jax.ShapeDtypeStruct((rows, S5_WIDTH), F32),
        grid=(bsz, nc_blocks + nl_blocks),
        in_specs=[pl.BlockSpec((blk, S5_WIDTH), lambda b, s: (block(b, s), 0)),
                  whole(bblk), whole(cblk), whole(pre), whole(pim)],
        out_specs=pl.BlockSpec((blk, S5_WIDTH), lambda b, s: (block(b, s), 0)),
        scratch_shapes=[
            pltpu.VMEM((S5_PACKS, 8, 2 * S5_PLANE), F32),
            pltpu.VMEM((S5_PACKS, blk, LANES), BF16),
            pltpu.VMEM((2, blk, 2 * S5_PLANE), F32),
            pltpu.VMEM((2, blk, 2 * S5_PLANE), BF16),
            pltpu.VMEM((S5_PACKS, blk, LANES), BF16),
        ],
        compiler_params=_cp("parallel", "arbitrary"),
        name="s5_bwd" if reverse else "s5_fwd",
    )(u_src, bblk, cblk, pre, pim)


def _gla_kernel(q_ref, k_ref, v_ref, gk_ref, up_ref, gb_ref, s0_ref, o_ref, sf_ref, st_ref):
    d = pl.program_id(0)
    c = pl.program_id(1)
    n = pl.num_programs(1)

    @pl.when(c == 0)
    def _():
        st_ref[...] = s0_ref[...]

    cs = GLA_CHUNK
    nb = q_ref.shape[0]
    sign = 1 - 2 * d
    row = _iota((cs, cs), 0)
    col = _iota((cs, cs), 1)
    causal = (row - col) * sign >= 0
    causal_b = causal.astype(BF16)
    rid = _iota((cs, 1), 0)
    bcum, b_mid, b_end = [], [], []
    for b in range(nb):
        x = _dot(gk_ref[b].astype(BF16), up_ref[...]) + gb_ref[...]
        log_a = (jnp.minimum(x, 0.0) - jnp.log(1.0 + jnp.exp(-jnp.abs(x)))) * (1.0 / GLA_TAU)
        la_hi, la_lo = _split_hi_lo(log_a)
        bc = _dot(causal_b, la_hi) + _dot(causal_b, la_lo)
        bcum.append(bc)
        b_mid.append(jnp.sum(jnp.where(rid == cs // 2 - d, bc, 0.0), axis=0, keepdims=True))
        b_end.append(jnp.sum(jnp.where(rid == (cs - 1) * (1 - d), bc, 0.0), axis=0, keepdims=True))
    units = [(b, h) for b in range(nb) for h in range(GLA_HEADS)]
    ks = lambda h: slice(GLA_DK * h, GLA_DK * (h + 1))
    vs = lambda h: slice(GLA_DV * h, GLA_DV * (h + 1))
    s_old = [st_ref[b, h] for b, h in units]
    qh = [q_ref[b, :, ks(h)].astype(F32) * (GLA_DK ** -0.5) for b, h in units]
    kh = [k_ref[b, :, ks(h)].astype(F32) for b, h in units]
    vh = [v_ref[b, :, vs(h)].astype(BF16) for b, h in units]
    bh = [bcum[b][:, ks(h)] for b, h in units]
    mid = [b_mid[b][:, ks(h)] for b, h in units]
    end = [b_end[b][:, ks(h)] for b, h in units]
    scores = [_dot_nt((q * jnp.exp(x - m)).astype(BF16), (k * jnp.exp(m - x)).astype(BF16))
              for q, k, x, m in zip(qh, kh, bh, mid)]
    inter = [_dot_nt((q * jnp.exp(x)).astype(BF16), s.astype(BF16)) for q, x, s in zip(qh, bh, s_old)]
    upd = [_dot_tn(v, (k * jnp.exp(e - x)).astype(BF16)) for v, k, e, x in zip(vh, kh, end, bh)]
    intra = [_dot(jnp.where(causal, sc, 0.0).astype(BF16), v) for sc, v in zip(scores, vh)]
    for u, (b, h) in enumerate(units):
        o_ref[b, :, vs(h)] = (intra[u] + inter[u]).astype(o_ref.dtype)
        st_ref[b, h] = s_old[u] * jnp.exp(end[u]) + upd[u]

    @pl.when(c == n - 1)
    def _():
        sf_ref[...] = st_ref[...]


def _gla(qk, v, gk, gk_up, gk_b, s0):
    bsz, cs, w = v.shape
    n = w // GLA_V

    def chunk(d, c):
        return c + d * (n - 1 - 2 * c)

    state_spec = pl.BlockSpec((None, bsz, GLA_HEADS, GLA_DV, GLA_DK), lambda d, c: (d, 0, 0, 0, 0))
    return pl.pallas_call(
        _gla_kernel,
        out_shape=(jax.ShapeDtypeStruct((2, bsz, cs, w), BF16), jax.ShapeDtypeStruct(s0.shape, F32)),
        grid=(2, n),
        in_specs=[
            pl.BlockSpec((bsz, cs, GLA_QK), lambda d, c: (0, 0, 2 * chunk(d, c))),
            pl.BlockSpec((bsz, cs, GLA_QK), lambda d, c: (0, 0, 2 * chunk(d, c) + 1)),
            pl.BlockSpec((bsz, cs, GLA_V), lambda d, c: (0, 0, chunk(d, c))),
            pl.BlockSpec((bsz, cs, LANES), lambda d, c: (0, 0, chunk(d, c))),
            pl.BlockSpec((None, LANES, GLA_QK), lambda d, c: (d, 0, 0)),
            pl.BlockSpec((None, 1, GLA_QK), lambda d, c: (d, 0, 0)),
            state_spec,
        ],
        out_specs=(pl.BlockSpec((None, bsz, cs, GLA_V), lambda d, c: (d, 0, 0, chunk(d, c))), state_spec),
        scratch_shapes=[pltpu.VMEM((bsz, GLA_HEADS, GLA_DV, GLA_DK), F32)],
        compiler_params=_cp("parallel", "arbitrary"),
        name="gla",
    )(qk, qk, v, gk, gk_up, gk_b, s0)


def _to_chunk_major(t, bsz):
    n = t.shape[0] // bsz
    d = t.shape[1]
    t = t.reshape(bsz, n // GLA_CHUNK, GLA_CHUNK, d)
    return jnp.transpose(t, (0, 2, 1, 3)).reshape(bsz, GLA_CHUNK, (n // GLA_CHUNK) * d)


def _from_chunk_major(t, d):
    lead = t.shape[:-2]
    n = t.shape[-1] // d
    t = t.reshape(lead + (GLA_CHUNK, n, d))
    return jnp.swapaxes(t, -3, -2).reshape(lead + (n * GLA_CHUNK, d))


def _rwkv_kernel(z0_ref, z1_ref, halo0_ref, halo1_ref, mu_ref, w0_ref, wup_ref, a0_ref, aup_ref, kk_ref, ka_ref,
                 rk_ref, y_ref, st_ref, kt_s, bt_s, kq_s, rt_s, v_s, w_s, u_s, ya_s, ab_s, pin_s,
                 *, reverse, nc_blocks):
    step = pl.program_id(0)
    blk = RW_BLOCK
    ch = RW_CHUNK
    n_ch = blk // ch
    n_b = 2
    n_units = n_b * RW_PAIRS

    @pl.when(step == 0)
    def _():
        st_ref[...] = jnp.zeros_like(st_ref)

    rowi = _iota((blk, 1), 0)
    low = _iota((1, LANES), 1) < RW_HEAD
    ones64 = _head_ones(RW_HEAD)
    seq_start = (step == 0) | (step == nc_blocks)
    rr = _iota((blk, blk), 0)
    cc = _iota((blk, blk), 1)
    same = rr // ch == cc // ch
    before = (cc > rr) if reverse else (cc < rr)
    strict = same & before
    incl = same & (before | (rr == cc))
    incl_b = incl.astype(BF16)

    for b, (z_ref, halo_ref) in enumerate(((z0_ref, halo0_ref), (z1_ref, halo1_ref))):
        z = z_ref[...]
        if reverse:
            prev = pltpu.roll(z, blk - 1, 0)
            edge = halo_ref[0:1, :]
            at_edge = rowi == blk - 1
        else:
            prev = pltpu.roll(z, 1, 0)
            edge = halo_ref[7:8, :]
            at_edge = rowi == 0
        prev = jnp.where(at_edge, jnp.where(seq_start, 0.0, edge), prev)
        zs = z + (prev - z) * mu_ref[...]
        r = zs[:, 0:RW_WIDTH]
        k = zs[:, RW_WIDTH:2 * RW_WIDTH]
        v = zs[:, 2 * RW_WIDTH:3 * RW_WIDTH]
        lora = zs[:, 3 * RW_WIDTH:3 * RW_WIDTH + LANES]
        lora_w = jnp.where(low, jnp.tanh(lora), 0.0).astype(BF16)
        lora_a = jnp.where(low, 0.0, lora).astype(BF16)
        logw = -RW_DECAY_SCALE * _sigmoid(w0_ref[...] + _dot(lora_w, wup_ref[...]))
        a = _sigmoid(a0_ref[...] + _dot(lora_a, aup_ref[...]))
        kk = k * kk_ref[...]
        kp = k * (1.0 + (a - 1.0) * ka_ref[...])
        rkb = r * kp * rk_ref[...]

        lw_hi, lw_lo = _split_hi_lo(logw)
        cl = _dot(incl_b, lw_hi) + _dot(incl_b, lw_lo)
        p_in = jnp.exp(cl)
        pin_s[b] = p_in
        p_ex = jnp.exp(cl - logw)
        p_inv = jnp.exp(-cl)

        for p in range(RW_PAIRS):
            ls = slice(LANES * p, LANES * (p + 1))
            q = b * RW_PAIRS + p
            kkp = kk[:, ls]
            ssq = _head_sum(kkp * kkp, ones64, False)
            kkn = kkp * (1.0 / jnp.maximum(jnp.sqrt(ssq), 1e-12))
            kt_s[q] = (kkn * p_ex[:, ls]).astype(BF16)
            bt_s[q] = (kkn * a[:, ls] * p_inv[:, ls]).astype(BF16)
            kq_s[q] = (kp[:, ls] * p_inv[:, ls]).astype(BF16)
            rt_s[q] = (r[:, ls] * p_in[:, ls]).astype(BF16)
            v_s[q] = v[:, ls].astype(BF16)
            ya_s[q] = _head_sum(rkb[:, ls], ones64, False) * v[:, ls]

    fold0 = (_iota((blk, LANES), 0) % ch == _iota((blk, LANES), 1)).astype(BF16)
    fold1 = (_iota((blk, LANES), 0) % ch + ch == _iota((blk, LANES), 1)).astype(BF16)
    lane_lo = _iota((1, LANES), 1) < RW_HEAD

    def pair_body(p, carry):
        units = [b * RW_PAIRS + p for b in range(n_b)]
        heads = [(u, hh) for u in range(n_b) for hh in range(2)]
        kt = [kt_s[q] for q in units]
        rt = [rt_s[q] for q in units]
        vv = [v_s[q] for q in units]
        ya = [ya_s[q] for q in units]
        bt = [bt_s[q] for q in units]
        kq = [kq_s[q] for q in units]
        hb = blk // 2
        g = []
        for u, hh in heads:
            mine = lane_lo if hh == 0 else jnp.logical_not(lane_lo)
            zero = jnp.zeros_like(kt[u])
            ktm, rtm = jnp.where(mine, kt[u], zero), jnp.where(mine, rt[u], zero)
            halves = []
            for rows in (slice(0, hb), slice(hb, blk)):
                lhs = jnp.concatenate([ktm[rows], rtm[rows]], axis=0)
                rhs = jnp.concatenate([bt[u][rows], kq[u][rows]], axis=0)
                halves.append(_dot_nt(lhs, rhs))
            g.append(halves)
        zero_q = jnp.zeros((hb, hb), F32)

        def diag2(gh, r0, c0):
            a, b = gh[0][r0:r0 + hb, c0:c0 + hb], gh[1][r0:r0 + hb, c0:c0 + hb]
            return jnp.concatenate([jnp.concatenate([a, zero_q], axis=1), jnp.concatenate([zero_q, b], axis=1)],
                                   axis=0)

        n1 = [jnp.where(strict, -diag2(gh, 0, 0), 0.0).astype(BF16) for gh in g]
        a_kq = [jnp.where(strict, diag2(gh, 0, hb), 0.0).astype(BF16) for gh in g]
        a_rb = [jnp.where(incl, diag2(gh, hb, 0), 0.0).astype(BF16) for gh in g]
        a_rq = [jnp.where(incl, diag2(gh, hb, hb), 0.0).astype(BF16) for gh in g]
        n2 = [_dot(n, n).astype(BF16) for n in n1]
        akv = [_dot(a, vv[u]) for a, (u, _) in zip(a_kq, heads)]
        n4 = [_dot(n, n).astype(BF16) for n in n2]
        y_in = [_dot(a, vv[u]) for a, (u, _) in zip(a_rq, heads)]
        n8 = [_dot(n, n).astype(BF16) for n in n4]
        fold = [_dot(a, fold0 if hh == 0 else fold1) for a, (_, hh) in zip(a_rb, heads)]
        rhs_t = [jnp.concatenate([kt[u].astype(F32), av], axis=1) for av, (u, _) in zip(akv, heads)]
        for nk in (n8, n4, n2, n1):
            rhs_t = [x + _dot(n, x.astype(BF16)) for n, x in zip(nk, rhs_t)]
        for u, q in enumerate(units):
            h0, h1 = 2 * u, 2 * u + 1
            w_s[q] = jnp.where(lane_lo, rhs_t[h0][:, :LANES], rhs_t[h1][:, :LANES]).astype(BF16)
            u_s[q] = jnp.where(lane_lo, rhs_t[h0][:, LANES:], rhs_t[h1][:, LANES:])
            ya_s[q] = ya[u] + jnp.where(lane_lo, y_in[h0], y_in[h1])
            ab_s[q] = (fold[h0] + fold[h1]).astype(BF16)
        return carry

    lax.fori_loop(0, RW_PAIRS, pair_body, 0)

    blockdiag = (_iota((LANES, LANES), 0) // RW_HEAD) == (_iota((LANES, LANES), 1) // RW_HEAD)
    end_row = 0 if reverse else ch - 1

    def chunk_body(i, carry):
        c = (n_ch - 1 - i) if reverse else i
        rows = pl.ds(pl.multiple_of(c * ch, ch), ch)
        s_old = [st_ref[q] for q in range(n_units)]
        m1 = [_dot_nt(jnp.concatenate([w_s[q, rows, :], rt_s[q, rows, :]], axis=0), s_old[q].astype(BF16))
              for q in range(n_units)]
        zc = [-(m1[q][:ch] + u_s[q, rows, :]) for q in range(n_units)]
        upd = []
        for q in range(n_units):
            zv = jnp.concatenate([zc[q].astype(BF16), v_s[q, rows, :]], axis=0)
            bk = jnp.concatenate([bt_s[q, rows, :], kq_s[q, rows, :]], axis=0)
            upd.append(_dot_tn(zv, bk))
        yc = []
        for q in range(n_units):
            z2 = jnp.concatenate([jnp.where(lane_lo, zc[q], 0.0), jnp.where(lane_lo, 0.0, zc[q])], axis=0)
            yc.append(m1[q][ch:] + _dot(ab_s[q, rows, :][:, :2 * ch], z2.astype(BF16)) + ya_s[q, rows, :])
        for q in range(n_units):
            b, p = divmod(q, RW_PAIRS)
            p_end = pin_s[b, rows, LANES * p:LANES * (p + 1)][end_row:end_row + 1]
            st_ref[q] = (s_old[q] + jnp.where(blockdiag, upd[q], 0.0)) * p_end
            y_ref[b, rows, LANES * p:LANES * (p + 1)] = yc[q]
        return carry

    lax.fori_loop(0, n_ch, chunk_body, 0)


def _rwkv_direction(zall, prm, bsz, nc_blocks, nl_blocks, reverse):
    assert bsz == 2
    mu, w0, wup, a0, aup, k_k, k_a, r_k = prm
    rows = zall.shape[0]
    blk = RW_BLOCK
    steps = nc_blocks + nl_blocks
    n_blocks = rows // blk

    def local(s):
        lat = nc_blocks + ((nl_blocks - 1 - (s - nc_blocks)) if reverse else (s - nc_blocks))
        ctx = (nc_blocks - 1 - s) if reverse else s
        return jnp.where(s < nc_blocks, ctx, lat)

    def block(b, s):
        i = local(s)
        return jnp.where(i < nc_blocks, b * nc_blocks + i, bsz * nc_blocks + b * nl_blocks + i - nc_blocks)

    def halo(b, s):
        i = block(b, s)
        if reverse:
            return jnp.minimum((i + 1) * (blk // 8), n_blocks * (blk // 8) - 1)
        return jnp.maximum(i * (blk // 8) - 1, 0)

    vec = lambda w: pl.BlockSpec((1, w), lambda s: (0, 0))
    mat = lambda: pl.BlockSpec((LANES, RW_WIDTH), lambda s: (0, 0))
    n_units = bsz * RW_PAIRS
    unit_bf = pltpu.VMEM((n_units, blk, LANES), BF16)
    unit_f = pltpu.VMEM((n_units, blk, LANES), F32)
    z_spec = lambda b: pl.BlockSpec((blk, RW_Z), lambda s: (block(b, s), 0))
    halo_spec = lambda b: pl.BlockSpec((8, RW_Z), lambda s: (halo(b, s), 0))
    return pl.pallas_call(
        functools.partial(_rwkv_kernel, reverse=reverse, nc_blocks=nc_blocks),
        out_shape=jax.ShapeDtypeStruct((bsz, steps * blk, RW_WIDTH), F32),
        grid=(steps,),
        in_specs=[
            z_spec(0), z_spec(1), halo_spec(0), halo_spec(1),
            vec(RW_Z), vec(RW_WIDTH), mat(), vec(RW_WIDTH), mat(), vec(RW_WIDTH), vec(RW_WIDTH), vec(RW_WIDTH),
        ],
        out_specs=pl.BlockSpec((bsz, blk, RW_WIDTH), lambda s: (0, local(s), 0)),
        scratch_shapes=[
            pltpu.VMEM((n_units, LANES, LANES), F32),
            unit_bf, unit_bf, unit_bf, unit_bf, unit_bf,
            unit_bf, unit_f, unit_f, unit_bf,
            pltpu.VMEM((bsz, blk, RW_WIDTH), F32),
        ],
        compiler_params=_cp("arbitrary"),
        name="rwkv_bwd" if reverse else "rwkv_fwd",
    )(zall, zall, zall, zall, mu, w0, wup, a0, aup, k_k, k_a, r_k)


def _branches_kernel(saf_ref, sab_ref, su_ref, sd_ref, rf_ref, rb_ref, rg_ref, gup_ref, lw_ref, lb_ref,
                     of_ref, ob_ref, og_ref, ng_ref, ga_ref, gb_ref, gc_ref, wglu_ref, wrw_ref, wgl_ref, m_ref):
    za = _gelu_tanh(saf_ref[...] + sab_ref[...] + sd_ref[...] * su_ref[...].astype(F32)).astype(BF16)
    hid = _dot(za, wglu_ref[...])
    m = _sigmoid(ga_ref[...].astype(F32)) * (hid[:, :D_MODEL] * _sigmoid(hid[:, D_MODEL:]))
    ones64 = _head_ones(RW_HEAD)
    gate = _dot(_sigmoid(rg_ref[...]).astype(BF16), gup_ref[...])
    zb = []
    for p in range(RW_PAIRS):
        ls = slice(LANES * p, LANES * (p + 1))
        y = rf_ref[:, ls] + rb_ref[:, ls]
        mean = _head_sum(y, ones64, True) * (1.0 / RW_HEAD)
        yc = y - mean
        var = _head_sum(yc * yc, ones64, True) * (1.0 / RW_HEAD)
        yn = yc * lax.rsqrt(var + RW_GN_EPS) * lw_ref[:, ls] + lb_ref[:, ls]
        zb.append((yn * gate[:, ls]).astype(BF16))
    m = m + _sigmoid(gb_ref[...].astype(F32)) * _dot(jnp.concatenate(zb, axis=1), wrw_ref[...])
    zc = []
    for h in range(GLA_HEADS):
        vs = slice(GLA_DV * h, GLA_DV * (h + 1))
        o = of_ref[:, vs].astype(F32) + ob_ref[:, vs].astype(F32)
        on = o * lax.rsqrt(jnp.mean(o * o, axis=-1, keepdims=True) + NORM_EPS) * ng_ref[:, vs]
        zc.append((on * _silu(og_ref[:, vs].astype(F32))).astype(BF16))
    m = m + _sigmoid(gc_ref[...].astype(F32)) * _dot(jnp.concatenate(zc, axis=1), wgl_ref[...])
    m_ref[...] = m.astype(BF16)


def _branches(ya, yb, o_dir, u_s5, og, gates, zall, prm, row0, nc_blocks, nl_blocks):
    s5_d, g_up, ln_w, ln_b, norm_g, w_glu, w_rw, w_gl = prm
    bm = RW_BLOCK
    bsz = yb[0].shape[0]
    n = zall.shape[0] - row0 * bm
    rows = lambda w, blk=0: pl.BlockSpec((bm, w), lambda i: (i + row0, blk))

    def y_index(i):
        i = i + row0
        t = i - bsz * nc_blocks
        return (jnp.where(t < 0, i // nc_blocks, t // nl_blocks),
                jnp.where(t < 0, i % nc_blocks, nc_blocks + t % nl_blocks), 0)

    y_spec = pl.BlockSpec((None, bm, RW_WIDTH), y_index)
    vec = pl.BlockSpec((1, 1024), lambda i: (0, 0))
    const = lambda a: pl.BlockSpec(a.shape, lambda i: (0,) * a.ndim, pipeline_mode=pl.Buffered(1))
    return pl.pallas_call(
        _branches_kernel,
        out_shape=jax.ShapeDtypeStruct((n, D_MODEL), BF16),
        grid=(n // bm,),
        in_specs=[rows(1024), rows(1024), rows(1024, 0), vec,
                  y_spec, y_spec, rows(LANES, RW_Z // LANES), const(g_up), vec, vec,
                  rows(1024), rows(1024), rows(1024), vec,
                  rows(D_MODEL, 0), rows(D_MODEL, 1), rows(D_MODEL, 2),
                  const(w_glu), const(w_rw), const(w_gl)],
        out_specs=pl.BlockSpec((bm, D_MODEL), lambda i: (i, 0)),
        compiler_params=_cp("parallel"),
        name="branches",
    )(ya[0], ya[1], u_s5, s5_d, yb[0], yb[1], zall, g_up, ln_w, ln_b,
      o_dir[0], o_dir[1], og, norm_g, gates, gates, gates, w_glu, w_rw, w_gl)


def _out_proj_kernel(m_ref, w_ref, x_ref, gate_ref, o_ref):
    o_ref[...] = x_ref[...] + gate_ref[...] * _dot(m_ref[...], w_ref[...])


def _out_proj(m, w_out, x, mod, layer, bm, row0, nc, seq):
    n, d = m.shape
    bn = d
    mod_index = _mod_spec(layer, 2, bm, nc, seq).index_map
    return pl.pallas_call(
        _out_proj_kernel,
        out_shape=jax.ShapeDtypeStruct((n, d), F32),
        grid=(n // bm, d // bn),
        in_specs=[pl.BlockSpec((bm, d), lambda i, j: (i, 0)),
                  pl.BlockSpec((d, bn), lambda i, j: (0, j)),
                  pl.BlockSpec((bm, bn), lambda i, j: (i + row0, j)),
                  pl.BlockSpec((None, None, None, 1, bn), lambda i, j: mod_index(i + row0)[:4] + (j,))],
        out_specs=pl.BlockSpec((bm, bn), lambda i, j: (i, j)),
        compiler_params=_cp("parallel", "parallel"),
        name="out_proj",
    )(m, w_out, x, mod)


MOE_TM = 256
MOE_RING = 3
MOE_ROW_TILES = D_MODEL // (2 * LANES)
META_E1, META_E2, META_R1, META_R2, META_W1, META_W2 = range(6)


def _router_kernel(x_ref, g_ref, sh_ref, sc_ref, wr_hi_ref, wr_lo_ref, br_ref, v_ref, meta_ref, meta_t_ref, cnt_ref,
                   base_s):
    @pl.when(pl.program_id(0) == 0)
    def _():
        base_s[...] = jnp.zeros_like(base_s)

    x = x_ref[...]
    y = x * lax.rsqrt(jnp.mean(x * x, axis=-1, keepdims=True) + NORM_EPS) * g_ref[...]
    t = y * (1.0 + sc_ref[...]) + sh_ref[...]
    word = _pack_halves(t)
    for s in range(MOE_ROW_TILES):
        v_ref[pl.ds(s, x.shape[0], stride=MOE_ROW_TILES), :] = word[:, LANES * s:LANES * (s + 1)]
    t_hi, t_lo = _split_hi_lo(t)
    logits = (_dot(t_hi, wr_hi_ref[...]) + _dot(t_lo, wr_hi_ref[...]) + _dot(t_hi, wr_lo_ref[...])) + br_ref[...]
    lane = _iota(logits.shape, 1).astype(F32)
    neg = jnp.float32(-jnp.inf)
    big = jnp.float32(LANES)
    l1 = jnp.where(lane < MOE_GROUPS, logits, neg)
    m1 = jnp.max(l1, axis=-1, keepdims=True)
    p_top = 1.0 / jnp.sum(jnp.exp(l1 - m1), axis=-1, keepdims=True)
    grp = jnp.min(jnp.where(l1 == m1, lane, big), axis=-1, keepdims=True)
    lo = MOE_LANE0 + MOE_PER_GROUP * grp
    in_grp = (lane >= lo) & (lane < lo + MOE_PER_GROUP)
    l2 = jnp.where(in_grp, logits, neg)
    v1 = jnp.max(l2, axis=-1, keepdims=True)
    i1 = jnp.min(jnp.where(l2 == v1, lane, big), axis=-1, keepdims=True)
    l3 = jnp.where(lane == i1, neg, l2)
    v2 = jnp.max(l3, axis=-1, keepdims=True)
    i2 = jnp.min(jnp.where(l3 == v2, lane, big), axis=-1, keepdims=True)
    e2 = jnp.exp(v2 - v1)
    w1 = p_top / (1.0 + e2)
    w2 = p_top * e2 / (1.0 + e2)
    pick1 = lane == i1
    pick2 = lane == i2
    chosen = jnp.where(pick1 | pick2, 1.0, 0.0)
    bm = x.shape[0]
    earlier = (_iota((bm, bm), 1) < _iota((bm, bm), 0)).astype(BF16)
    before = _dot(earlier, chosen.astype(BF16)) + base_s[...]
    r1 = jnp.sum(jnp.where(pick1, before, 0.0), axis=-1, keepdims=True)
    r2 = jnp.sum(jnp.where(pick2, before, 0.0), axis=-1, keepdims=True)
    base_s[...] += jnp.sum(chosen, axis=0, keepdims=True)
    cnt_ref[...] = base_s[...]
    meta = jnp.zeros_like(logits)
    for slot, val in ((META_E1, i1 - MOE_LANE0), (META_E2, i2 - MOE_LANE0), (META_R1, r1), (META_R2, r2),
                      (META_W1, w1), (META_W2, w2)):
        meta = jnp.where(lane == slot, val, meta)
    meta_ref[...] = meta
    meta_t_ref[...] = meta.T[:8]


def _router(x, g, mod, layer, wr_hi, wr_lo, br, bm, nc, seq):
    n, d = x.shape
    return pl.pallas_call(
        _router_kernel,
        out_shape=(jax.ShapeDtypeStruct((n * MOE_ROW_TILES, LANES), jnp.uint32), jax.ShapeDtypeStruct((n, LANES), F32),
                   jax.ShapeDtypeStruct((8, n), F32), jax.ShapeDtypeStruct((1, LANES), F32)),
        grid=(n // bm,),
        in_specs=[
            pl.BlockSpec((bm, d), lambda i: (i, 0)),
            pl.BlockSpec((None, 1, d), lambda i: (layer, 0, 0)),
            _mod_spec(layer, 3, bm, nc, seq), _mod_spec(layer, 4, bm, nc, seq),
            pl.BlockSpec((d, LANES), lambda i: (0, 0)), pl.BlockSpec((d, LANES), lambda i: (0, 0)),
            pl.BlockSpec((1, LANES), lambda i: (0, 0)),
        ],
        out_specs=(pl.BlockSpec((bm * MOE_ROW_TILES, LANES), lambda i: (i, 0)),
                   pl.BlockSpec((bm, LANES), lambda i: (i, 0)),
                   pl.BlockSpec((8, bm), lambda i: (0, i)), pl.BlockSpec((1, LANES), lambda i: (0, 0))),
        scratch_shapes=[pltpu.VMEM((1, LANES), F32)],
        compiler_params=_cp("arbitrary"),
        name="moe_router",
    )(x, g, mod, mod, wr_hi, wr_lo, br)


def _moe_plan(meta_t, cnt):
    tm = MOE_TM
    n_tok = meta_t.shape[1]
    counts = cnt[0, MOE_LANE0:MOE_LANE0 + MOE_EXPERTS].astype(jnp.int32)
    seg = ((counts + tm - 1) // tm) * tm
    ends = jnp.cumsum(seg)
    off = ends - seg
    rec = meta_t[:4].astype(jnp.int32)
    first_row = jnp.sum(jnp.where(rec[:2, None, :] == jnp.arange(MOE_EXPERTS)[None, :, None], off[None, :, None], 0),
                        axis=1)
    dest = first_row + rec[2:4]
    n_rows = 2 * n_tok + MOE_EXPERTS * tm
    n_tiles = n_rows // tm
    tile_e = jnp.sum(((jnp.arange(n_tiles) * tm)[:, None] >= ends[None, :]).astype(jnp.int32), axis=1)
    tile_e = jnp.minimum(tile_e, MOE_EXPERTS - 1)
    tok = jnp.broadcast_to(jnp.arange(n_tok, dtype=jnp.int32), (2, n_tok))
    src = jnp.zeros((n_rows,), jnp.int32).at[dest.reshape(-1)].set(tok.reshape(-1))
    return dest[0], dest[1], src, tile_e, (ends[-1] // tm).reshape(1)


def _row_copy(src_hbm, row, dst, slot, r, sem):
    return pltpu.make_async_copy(src_hbm.at[pl.ds(row, 1)], dst.at[slot, pl.ds(r, 1)], sem)


def _experts_kernel(src_ref, te_ref, nu_ref, v_hbm, wg_ref, wu_ref, wd_ref, y_ref, xbuf, sem):
    del te_ref
    i = pl.program_id(0)
    n_used = nu_ref[0]
    slot = i % MOE_RING

    nt = MOE_ROW_TILES

    def token_copy(tok, into, r):
        return pltpu.make_async_copy(v_hbm.at[pl.ds(tok * nt, nt)], xbuf.at[into, pl.ds(r * nt, nt)], sem.at[into])

    def gather(tile, into):
        for r in range(MOE_TM):
            token_copy(src_ref[tile * MOE_TM + r], into, r).start(priority=r % 2)

    def expert():
        for r in range(MOE_TM):
            token_copy(0, slot, r).wait()
        word = jnp.concatenate([xbuf[slot, pl.ds(s, MOE_TM, stride=nt), :] for s in range(nt)], axis=1)
        t = jnp.concatenate(_unpack_halves(word), axis=1).astype(BF16)
        hid = _silu(_dot(t, wg_ref[...].astype(BF16))) * _dot(t, wu_ref[...].astype(BF16))
        y_ref[...] = _pack_halves(_dot(hid.astype(BF16), wd_ref[...].astype(BF16)))

    @pl.when((i == 0) & (n_used > 0))
    def _():
        gather(0, 0)

    @pl.when((i == 0) & (n_used > 1))
    def _():
        gather(1, 1)

    @pl.when(i + 2 < n_used)
    def _():
        gather(i + 2, (i + 2) % MOE_RING)
        expert()

    @pl.when((i < n_used) & (i + 2 >= n_used))
    def _():
        expert()

    @pl.when(i >= n_used)
    def _():
        y_ref[...] = jnp.zeros_like(y_ref)


def _experts(v, src, tile_e, n_used, layer, w_gate, w_up, w_down):
    d = w_gate.shape[2]
    hdim = w_gate.shape[3]
    n_rows = src.shape[0]
    tm = MOE_TM
    by_expert = lambda i, src_r, te_r, nu_r: (layer, te_r[i], 0, 0)
    return pl.pallas_call(
        _experts_kernel,
        out_shape=jax.ShapeDtypeStruct((n_rows, d // 2), jnp.uint32),
        grid_spec=pltpu.PrefetchScalarGridSpec(
            num_scalar_prefetch=3,
            grid=(n_rows // tm,),
            in_specs=[
                pl.BlockSpec(memory_space=pl.ANY),
                pl.BlockSpec((None, None, d, hdim), by_expert),
                pl.BlockSpec((None, None, d, hdim), by_expert),
                pl.BlockSpec((None, None, hdim, d), by_expert),
            ],
            out_specs=pl.BlockSpec((tm, d // 2), lambda i, *_: (i, 0)),
            scratch_shapes=[pltpu.VMEM((MOE_RING, tm * MOE_ROW_TILES, LANES), jnp.uint32),
                            pltpu.SemaphoreType.DMA((MOE_RING,))],
        ),
        compiler_params=_cp("arbitrary"),
        name="moe_experts",
    )(src, tile_e, n_used, v, w_gate, w_up, w_down)


def _combine_kernel(d1_ref, d2_ref, y_hbm, meta_ref, x_ref, gate_ref, gfin_ref, o_ref, buf1, buf2, sem, *, final):
    i = pl.program_id(0)
    bm = x_ref.shape[0]
    slot = i % 2

    def gather(tile, into):
        for r in range(bm):
            t = tile * bm + r
            _row_copy(y_hbm, d1_ref[t], buf1, into, r, sem.at[0, into]).start(priority=0)
            _row_copy(y_hbm, d2_ref[t], buf2, into, r, sem.at[1, into]).start(priority=1)

    @pl.when(i == 0)
    def _():
        gather(0, 0)

    @pl.when(i + 1 < pl.num_programs(0))
    def _():
        gather(i + 1, 1 - slot)

    for r in range(bm):
        _row_copy(y_hbm, 0, buf1, slot, r, sem.at[0, slot]).wait()
        _row_copy(y_hbm, 0, buf2, slot, r, sem.at[1, slot]).wait()
    meta = meta_ref[...]
    lane = _iota(meta.shape, 1)
    w1 = jnp.sum(jnp.where(lane == META_W1, meta, 0.0), axis=-1, keepdims=True)
    w2 = jnp.sum(jnp.where(lane == META_W2, meta, 0.0), axis=-1, keepdims=True)
    lo1, hi1 = _unpack_halves(buf1[slot])
    lo2, hi2 = _unpack_halves(buf2[slot])
    moe = jnp.concatenate([w1 * lo1 + w2 * lo2, w1 * hi1 + w2 * hi2], axis=1)
    out = x_ref[...] + gate_ref[...] * moe
    if final:
        out = out * lax.rsqrt(jnp.mean(out * out, axis=-1, keepdims=True) + NORM_EPS) * gfin_ref[...]
    o_ref[...] = out


def _combine(y, d1, d2, meta, x, mod, layer, bm, nc, seq, g_final, final):
    n, d = x.shape
    at_tile = lambda i, *_: (i, 0)
    mod_index = _mod_spec(layer, 5, bm, nc, seq).index_map
    return pl.pallas_call(
        functools.partial(_combine_kernel, final=final),
        out_shape=jax.ShapeDtypeStruct((n, d), F32),
        grid_spec=pltpu.PrefetchScalarGridSpec(
            num_scalar_prefetch=2,
            grid=(n // bm,),
            in_specs=[
                pl.BlockSpec(memory_space=pl.ANY),
                pl.BlockSpec((bm, LANES), at_tile),
                pl.BlockSpec((bm, d), at_tile),
                pl.BlockSpec((None, None, None, 1, d), lambda i, *_: mod_index(i)),
                pl.BlockSpec((1, d), lambda i, *_: (0, 0)),
            ],
            out_specs=pl.BlockSpec((bm, d), at_tile),
            scratch_shapes=[pltpu.VMEM((2, bm, d // 2), jnp.uint32), pltpu.VMEM((2, bm, d // 2), jnp.uint32),
                            pltpu.SemaphoreType.DMA((2, 2))],
        ),
        compiler_params=_cp("arbitrary"),
        name="moe_combine",
    )(d1, d2, y, meta, x, mod, g_final)


_COL = dict(s5=0, rw=S5_WIDTH, rg=S5_WIDTH + 3 * RW_WIDTH + 128, q=4352, k=4864, v=5376, gk=6400, og=6416, gates=7440)


def _pad_rows(w, rows):
    return jnp.pad(w, ((0, rows - w.shape[0]), (0, 0)))


def _layer(i, last, xa, mod, bsz, seq, ctx_len, p):
    nc = bsz * ctx_len
    rows = xa.shape[0]
    bm = 512
    msel = dict(nc=nc, seq=seq)
    u = _normmod(xa, p['g_norm1'].reshape(-1, 1, D_MODEL), mod, i, (0, 1), bm, **msel)

    w_in = p['w_in'][i]
    col = lambda a, w: w_in[:, a:a + w].astype(BF16)
    bm_in = 1088 if rows % 1088 == 0 else bm
    w_z = col(_COL['rw'], 3 * RW_WIDTH + 128 + 128)
    w_qk = col(_COL['q'], 2 * GLA_QK)
    w_v = col(_COL['v'], GLA_V)
    w_gk = jnp.pad(col(_COL['gk'], 16), ((0, 0), (0, LANES - 16)))
    u_s5 = _mm(u, col(0, S5_WIDTH), bm_in, 1024, BF16)
    og = _mm(u, col(_COL['og'], GLA_V), bm_in, 1024, BF16)
    gates = _mm(u, col(_COL['gates'], 3 * D_MODEL), bm_in, 1024, BF16)
    zall = _mm(u, w_z, bm_in, 1664)
    qk = _mm(u, w_qk, bm_in, 1024, BF16)
    vv = _mm(u, w_v, bm_in, 1024, BF16)
    gk = _mm(u, w_gk, bm_in, LANES, BF16)

    ya = []
    for d in range(2):
        tables = _s5_tables(*(p[k][i, d] for k in ('s5_a_re', 's5_a_im', 's5_log_dt', 's5_b_re', 's5_b_im',
                                                   's5_c_re', 's5_c_im')))
        ya.append(_s5_scan(u_s5, tables, bsz, ctx_len // S5_BLOCK, seq // S5_BLOCK, bool(d)))

    yb = []
    for d in range(2):
        prm = (
            jnp.pad(p['rw_mu'][i, d], (0, RW_Z - p['rw_mu'].shape[-1])).reshape(1, RW_Z),
            p['rw_w0'][i, d].reshape(1, -1),
            _pad_rows(p['rw_w_up'][i, d], LANES).astype(BF16),
            p['rw_a0'][i, d].reshape(1, -1),
            jnp.pad(p['rw_a_up'][i, d], ((RW_HEAD, 0), (0, 0))).astype(BF16),
            p['rw_k_k'][i].reshape(1, -1), p['rw_k_a'][i].reshape(1, -1), p['rw_r_k'][i].reshape(1, -1),
        )
        yb.append(_rwkv_direction(zall, prm, bsz, ctx_len // RW_BLOCK, seq // RW_BLOCK, bool(d)))

    gk_up = jnp.pad(p['gl_gk_up'][i], ((0, 0), (0, LANES - 16), (0, 0))).astype(BF16)
    gk_b = p['gl_gk_b'][i].reshape(2, 1, GLA_QK)
    s0 = jnp.zeros((2, bsz, GLA_HEADS, GLA_DV, GLA_DK), F32)
    oc, s_ctx = _gla(_to_chunk_major(qk[:nc], bsz), _to_chunk_major(vv[:nc], bsz), _to_chunk_major(gk[:nc], bsz),
                     gk_up, gk_b, s0)
    rows_l = seq // GRID_W
    lat = lambda t: t[nc:].reshape(bsz, rows_l, GRID_W * t.shape[1])
    ol, _ = _gla(lat(qk), lat(vv), lat(gk), gk_up, gk_b, s_ctx)
    o_dir = [jnp.concatenate([_from_chunk_major(oc[d], GLA_V).reshape(nc, GLA_V),
                              ol[d].reshape(bsz * seq, GLA_V)], axis=0) for d in range(2)]

    row0 = (nc // bm) if last else 0
    branch_prm = (p['s5_d'][i].reshape(1, -1), p['rw_g_up'][i].astype(BF16), p['rw_ln_w'][i].reshape(1, -1),
                  p['rw_ln_b'][i].reshape(1, -1), p['gl_norm_g'][i].reshape(1, -1), p['s5_w_glu'][i].astype(BF16),
                  p['rw_w_proj'][i].astype(BF16), p['gl_w_proj'][i].astype(BF16))
    merged = _branches(ya, yb, o_dir, u_s5, og, gates, zall, branch_prm, row0 * (bm // RW_BLOCK), ctx_len // RW_BLOCK,
                       seq // RW_BLOCK)
    xm = _out_proj(merged, p['w_out'][i].astype(BF16), xa, mod, i, bm, row0, nc, seq)
    if last:
        msel = dict(nc=0, seq=seq)

    wr = jnp.pad(jnp.concatenate([p['moe_wg1'][i], p['moe_wg2'][i]], axis=1), ((0, 0), (0, LANES - 36)))
    wr_hi = wr.astype(BF16)
    wr_lo = (wr - wr_hi.astype(F32)).astype(BF16)
    br = jnp.pad(jnp.concatenate([p['moe_bg1'][i], p['moe_bg2'][i]]), (0, LANES - 36)).reshape(1, LANES)
    vmoe, meta, meta_t, cnt = _router(xm, p['g_norm2'].reshape(-1, 1, D_MODEL), mod, i, wr_hi, wr_lo, br, bm, **msel)
    d1, d2, src, tile_e, n_used = _moe_plan(meta_t, cnt)
    y_sorted = _experts(vmoe, src, tile_e, n_used, i, p['moe_w_gate'], p['moe_w_up'], p['moe_w_down'])
    return _combine(y_sorted, d1, d2, meta, xm, mod, i, bm // 2, g_final=p['g_final'], final=last, **msel)


def kernel(x, c, ctx, c_ctx, w_mod, b_mod, g_norm1, g_norm2, w_in, s5_a_re, s5_a_im, s5_log_dt, s5_b_re, s5_b_im,
           s5_c_re, s5_c_im, s5_d, s5_w_glu, rw_mu, rw_w0, rw_w_up, rw_a0, rw_a_up, rw_k_k, rw_k_a, rw_r_k, rw_g_up,
           rw_ln_w, rw_ln_b, rw_w_proj, gl_gk_up, gl_gk_b, gl_norm_g, gl_w_proj, w_out, moe_wg1, moe_bg1, moe_wg2,
           moe_bg2, moe_w_gate, moe_w_up, moe_w_down, g_final):
    p = dict(g_norm1=g_norm1, g_norm2=g_norm2, w_in=w_in, s5_a_re=s5_a_re, s5_a_im=s5_a_im, s5_log_dt=s5_log_dt,
             s5_b_re=s5_b_re, s5_b_im=s5_b_im, s5_c_re=s5_c_re, s5_c_im=s5_c_im, s5_d=s5_d, s5_w_glu=s5_w_glu,
             rw_mu=rw_mu, rw_w0=rw_w0, rw_w_up=rw_w_up, rw_a0=rw_a0, rw_a_up=rw_a_up, rw_k_k=rw_k_k, rw_k_a=rw_k_a,
             rw_r_k=rw_r_k.reshape(rw_r_k.shape[0], -1), rw_g_up=rw_g_up, rw_ln_w=rw_ln_w, rw_ln_b=rw_ln_b,
             rw_w_proj=rw_w_proj, gl_gk_up=gl_gk_up, gl_gk_b=gl_gk_b, gl_norm_g=gl_norm_g, gl_w_proj=gl_w_proj,
             w_out=w_out, moe_wg1=moe_wg1, moe_bg1=moe_bg1, moe_wg2=moe_wg2, moe_bg2=moe_bg2, moe_w_gate=moe_w_gate,
             moe_w_up=moe_w_up, moe_w_down=moe_w_down, g_final=g_final.reshape(1, -1))
    bsz, seq, d = x.shape
    ctx_len = ctx.shape[1]
    depth = w_mod.shape[0]
    cc = jnp.concatenate([c, c_ctx[None], jnp.zeros((8 - bsz - 1, d), F32)], axis=0)
    mod = _adaln(cc, w_mod, b_mod).reshape(depth, 8, 6, 1, d)
    xa = jnp.concatenate([ctx.reshape(bsz * ctx_len, d), x.reshape(bsz * seq, d)], axis=0)
    for i in range(depth):
        xa = _layer(i, i == depth - 1, xa, mod, bsz, seq, ctx_len, p)
    return xa.reshape(bsz, seq, d)
```
